```python
import jax, jax.numpy as jnp
from jax import lax
import numpy as np

D_MODEL = 1024
BATCH = 8
SEQ = 8192
DEPTH = 4

CHUNK = 64
Q_BLOCK = 128
N_A_LAYERS = DEPTH // 2
N_B_LAYERS = DEPTH - N_A_LAYERS
SB_HEADS = 16
SB_HEAD_DIM = D_MODEL // SB_HEADS
MLA_HEADS = 16
MLA_NOPE_DIM = 64
MLA_ROPE_DIM = 32
MLA_V_DIM = 64
MLA_Q_RANK = 384
MLA_KV_RANK = 256
D_FF = 4 * D_MODEL
ROPE_THETA = 10000.0
NORM_EPS = 1e-6
MAX_POS_OFFSET = 4096

kernel_name = "yoco_stickbreak_mla_hybrid"


def rms_norm(x, g):
    xf = x.astype(jnp.float32)
    y = xf * lax.rsqrt(jnp.mean(xf * xf, axis=-1, keepdims=True) + NORM_EPS)
    return (y * g.astype(jnp.float32)).astype(x.dtype)


def rope_tables(positions):
    inv_freq = ROPE_THETA ** (-jnp.arange(0, MLA_ROPE_DIM, 2, dtype=jnp.float32) / MLA_ROPE_DIM)
    ang = positions.astype(jnp.float32)[..., None] * inv_freq
    return jnp.cos(ang), jnp.sin(ang)


def apply_rope(t, cos, sin):
    half = MLA_ROPE_DIM // 2
    t1 = t[..., :half].astype(jnp.float32)
    t2 = t[..., half:].astype(jnp.float32)
    out = jnp.concatenate([t1 * cos - t2 * sin, t2 * cos + t1 * sin], axis=-1)
    return out.astype(t.dtype)


def to_query_blocks(t):
    b, s, h, d = t.shape
    return t.reshape(b, s // Q_BLOCK, Q_BLOCK, h, d).transpose(1, 0, 3, 2, 4)


def from_query_blocks(t):
    nb, b, qb, h, d = t.shape
    return t.transpose(1, 0, 2, 3, 4).reshape(b, nb * qb, h * d)


def stick_breaking_attention(h, w_qkv, w_o):
    b, s, _ = h.shape
    qkv = (h @ w_qkv).reshape(b, s, 3, SB_HEADS, SB_HEAD_DIM)
    q, k, v = qkv[:, :, 0], qkv[:, :, 1], qkv[:, :, 2]
    k_h = k.transpose(0, 2, 1, 3)
    v_h = v.transpose(0, 2, 1, 3)
    q_blocks = to_query_blocks(q)
    scale = SB_HEAD_DIM ** -0.5
    key_pos = jnp.arange(s)

    def block(args):
        q_blk, blk_idx = args
        q_pos = blk_idx * Q_BLOCK + jnp.arange(Q_BLOCK)
        z = jnp.einsum('bhqd,bhkd->bhqk', q_blk, k_h).astype(jnp.float32) * scale
        causal = key_pos[None, :] < q_pos[:, None]
        log_beta = jax.nn.log_sigmoid(z)
        log_keep = jnp.where(causal, jax.nn.log_sigmoid(-z), 0.0)
        shifted = jnp.concatenate([log_keep[..., 1:], jnp.zeros_like(log_keep[..., :1])], axis=-1)
        log_survive = lax.cumsum(shifted, axis=3, reverse=True)
        weights = jnp.where(causal, jnp.exp(log_beta + log_survive), 0.0)
        return jnp.einsum('bhqk,bhkd->bqhd', weights.astype(v_h.dtype), v_h)

    out = lax.map(block, (q_blocks, jnp.arange(s // Q_BLOCK)))
    return from_query_blocks(out) @ w_o


def mla_shared_kv(h, w_dkv, kv_lat_norm, w_ukv, cos, sin):
    b, s, _ = h.shape
    down = h @ w_dkv
    c_kv = rms_norm(down[..., :MLA_KV_RANK], kv_lat_norm)
    k_rope = apply_rope(down[..., MLA_KV_RANK:], cos, sin)
    kv = (c_kv @ w_ukv).reshape(b, s, MLA_HEADS, MLA_NOPE_DIM + MLA_V_DIM)
    k_nope = kv[..., :MLA_NOPE_DIM].transpose(0, 2, 1, 3)
    v = kv[..., MLA_NOPE_DIM:].transpose(0, 2, 1, 3)
    return k_nope, k_rope, v


def mla_attention(h, w_dq, q_lat_norm, w_uq, w_o, k_nope, k_rope, v, cos, sin):
    b, s, _ = h.shape
    c_q = rms_norm(h @ w_dq, q_lat_norm)
    q = (c_q @ w_uq).reshape(b, s, MLA_HEADS, MLA_NOPE_DIM + MLA_ROPE_DIM)
    q_nope = q[..., :MLA_NOPE_DIM]
    q_rope = apply_rope(q[..., MLA_NOPE_DIM:], cos[:, :, None, :], sin[:, :, None, :])
    qn_blocks = to_query_blocks(q_nope)
    qr_blocks = to_query_blocks(q_rope)
    scale = (MLA_NOPE_DIM + MLA_ROPE_DIM) ** -0.5
    key_chunk = jnp.arange(s) // CHUNK

    def block(args):
        qn, qr, blk_idx = args
        q_chunk = (blk_idx * Q_BLOCK + jnp.arange(Q_BLOCK)) // CHUNK
        scores = (jnp.einsum('bhqd,bhkd->bhqk', qn, k_nope)
                  + jnp.einsum('bhqr,bkr->bhqk', qr, k_rope)).astype(jnp.float32) * scale
        allowed = key_chunk[None, :] <= q_chunk[:, None]
        probs = jax.nn.softmax(jnp.where(allowed, scores, -jnp.inf), axis=-1)
        return jnp.einsum('bhqk,bhkd->bqhd', probs.astype(v.dtype), v)

    out = lax.map(block, (qn_blocks, qr_blocks, jnp.arange(s // Q_BLOCK)))
    return from_query_blocks(out) @ w_o


def squared_relu_mlp(h, w1, w2):
    return jnp.square(jax.nn.relu(h @ w1)) @ w2


def _fwd_setup_inputs(seed: int = 0) -> dict:
    key = jax.random.key(seed)
    ks = jax.random.split(key, 20)

    def w(k, shape, fan_in, gain=1.0):
        return jax.random.normal(k, shape, jnp.float32) * (gain * fan_in ** -0.5)

    def gains(k, shape):
        return 1.0 + 0.02 * jax.random.normal(k, shape, jnp.float32)

    out_gain = (2.0 * DEPTH) ** -0.5
    x = jax.random.normal(ks[0], (BATCH, SEQ, D_MODEL), jnp.float32)
    offsets = jax.random.randint(ks[1], (BATCH, 1), 0, MAX_POS_OFFSET, dtype=jnp.int32)
    positions = offsets + jnp.arange(SEQ, dtype=jnp.int32)[None, :]
    return {
        "x": x,
        "positions": positions,
        "attn_norm": gains(ks[2], (DEPTH, D_MODEL)),
        "mlp_norm": gains(ks[3], (DEPTH, D_MODEL)),
        "sb_w_qkv": w(ks[4], (N_A_LAYERS, D_MODEL, 3 * D_MODEL), D_MODEL),
        "sb_w_o": w(ks[5], (N_A_LAYERS, D_MODEL, D_MODEL), D_MODEL, out_gain),
        "kv_norm": gains(ks[6], (D_MODEL,)),
        "mla_w_dkv": w(ks[7], (D_MODEL, MLA_KV_RANK + MLA_ROPE_DIM), D_MODEL),
        "mla_kv_lat_norm": gains(ks[8], (MLA_KV_RANK,)),
        "mla_w_ukv": w(ks[9], (MLA_KV_RANK, MLA_HEADS * (MLA_NOPE_DIM + MLA_V_DIM)), MLA_KV_RANK),
        "mla_w_dq": w(ks[10], (N_B_LAYERS, D_MODEL, MLA_Q_RANK), D_MODEL),
        "mla_q_lat_norm": gains(ks[11], (N_B_LAYERS, MLA_Q_RANK)),
        "mla_w_uq": w(ks[12], (N_B_LAYERS, MLA_Q_RANK, MLA_HEADS * (MLA_NOPE_DIM + MLA_ROPE_DIM)), MLA_Q_RANK),
        "mla_w_o": w(ks[13], (N_B_LAYERS, MLA_HEADS * MLA_V_DIM, D_MODEL), MLA_HEADS * MLA_V_DIM, out_gain),
        "mlp_w1": w(ks[14], (DEPTH, D_MODEL, D_FF), D_MODEL),
        "mlp_w2": w(ks[15], (DEPTH, D_FF, D_MODEL), D_FF, out_gain),
        "final_norm": gains(ks[16], (D_MODEL,)),
    }


def _fwd_reference(x, positions, attn_norm, mlp_norm, sb_w_qkv, sb_w_o, kv_norm, mla_w_dkv,
              mla_kv_lat_norm, mla_w_ukv, mla_w_dq, mla_q_lat_norm, mla_w_uq, mla_w_o,
              mlp_w1, mlp_w2, final_norm):
    cos, sin = rope_tables(positions)
    k_nope = k_rope = v = None
    for layer in range(DEPTH):
        if layer < N_A_LAYERS:
            x = x + stick_breaking_attention(rms_norm(x, attn_norm[layer]),
                                             sb_w_qkv[layer], sb_w_o[layer])
        else:
            if layer == N_A_LAYERS:
                k_nope, k_rope, v = mla_shared_kv(rms_norm(x, kv_norm), mla_w_dkv,
                                                  mla_kv_lat_norm, mla_w_ukv, cos, sin)
            j = layer - N_A_LAYERS
            x = x + mla_attention(rms_norm(x, attn_norm[layer]), mla_w_dq[j], mla_q_lat_norm[j],
                                  mla_w_uq[j], mla_w_o[j], k_nope, k_rope, v, cos, sin)
        x = x + squared_relu_mlp(rms_norm(x, mlp_norm[layer]), mlp_w1[layer], mlp_w2[layer])
    return rms_norm(x, final_norm)


import jax as _jax
import jax.numpy as _jnp

TWIN_FORMAT = 'train_step'
FWD_PARAMS = ['x', 'positions', 'attn_norm', 'mlp_norm', 'sb_w_qkv', 'sb_w_o', 'kv_norm', 'mla_w_dkv', 'mla_kv_lat_norm', 'mla_w_ukv', 'mla_w_dq', 'mla_q_lat_norm', 'mla_w_uq', 'mla_w_o', 'mlp_w1', 'mlp_w2', 'final_norm']
TWIN_WEIGHTS = ['attn_norm', 'mlp_norm', 'sb_w_qkv', 'sb_w_o', 'kv_norm', 'mla_w_dkv', 'mla_kv_lat_norm', 'mla_w_ukv', 'mla_w_dq', 'mla_q_lat_norm', 'mla_w_uq', 'mla_w_o', 'mlp_w1', 'mlp_w2', 'final_norm']
TWIN_DIFF_INPUT = 'x'
TWIN_INPUTS = ['x', 'positions', 'attn_norm', 'mlp_norm', 'sb_w_qkv', 'sb_w_o', 'kv_norm', 'mla_w_dkv', 'mla_kv_lat_norm', 'mla_w_ukv', 'mla_w_dq', 'mla_q_lat_norm', 'mla_w_uq', 'mla_w_o', 'mlp_w1', 'mlp_w2', 'final_norm', 'loss_target', 'm_attn_norm', 'm_mlp_norm', 'm_sb_w_qkv', 'm_sb_w_o', 'm_kv_norm', 'm_mla_w_dkv', 'm_mla_kv_lat_norm', 'm_mla_w_ukv', 'm_mla_w_dq', 'm_mla_q_lat_norm', 'm_mla_w_uq', 'm_mla_w_o', 'm_mlp_w1', 'm_mlp_w2', 'm_final_norm', 'v_attn_norm', 'v_mlp_norm', 'v_sb_w_qkv', 'v_sb_w_o', 'v_kv_norm', 'v_mla_w_dkv', 'v_mla_kv_lat_norm', 'v_mla_w_ukv', 'v_mla_w_dq', 'v_mla_q_lat_norm', 'v_mla_w_uq', 'v_mla_w_o', 'v_mlp_w1', 'v_mlp_w2', 'v_final_norm']
TWIN_OUTPUTS = ['loss', 'grad_x', 'grad_attn_norm', 'grad_mlp_norm', 'grad_sb_w_qkv', 'grad_sb_w_o', 'grad_kv_norm', 'grad_mla_w_dkv', 'grad_mla_kv_lat_norm', 'grad_mla_w_ukv', 'grad_mla_w_dq', 'grad_mla_q_lat_norm', 'grad_mla_w_uq', 'grad_mla_w_o', 'grad_mlp_w1', 'grad_mlp_w2', 'grad_final_norm', 'delta_attn_norm', 'delta_mlp_norm', 'delta_sb_w_qkv', 'delta_sb_w_o', 'delta_kv_norm', 'delta_mla_w_dkv', 'delta_mla_kv_lat_norm', 'delta_mla_w_ukv', 'delta_mla_w_dq', 'delta_mla_q_lat_norm', 'delta_mla_w_uq', 'delta_mla_w_o', 'delta_mlp_w1', 'delta_mlp_w2', 'delta_final_norm', 'new_m_attn_norm', 'new_m_mlp_norm', 'new_m_sb_w_qkv', 'new_m_sb_w_o', 'new_m_kv_norm', 'new_m_mla_w_dkv', 'new_m_mla_kv_lat_norm', 'new_m_mla_w_ukv', 'new_m_mla_w_dq', 'new_m_mla_q_lat_norm', 'new_m_mla_w_uq', 'new_m_mla_w_o', 'new_m_mlp_w1', 'new_m_mlp_w2', 'new_m_final_norm', 'new_v_attn_norm', 'new_v_mlp_norm', 'new_v_sb_w_qkv', 'new_v_sb_w_o', 'new_v_kv_norm', 'new_v_mla_w_dkv', 'new_v_mla_kv_lat_norm', 'new_v_mla_w_ukv', 'new_v_mla_w_dq', 'new_v_mla_q_lat_norm', 'new_v_mla_w_uq', 'new_v_mla_w_o', 'new_v_mlp_w1', 'new_v_mlp_w2', 'new_v_final_norm']
TWIN_LEAF_KINDS = {'loss': 'loss', 'grad_x': 'grad_x', 'grad_attn_norm': 'grad_w', 'grad_mlp_norm': 'grad_w', 'grad_sb_w_qkv': 'grad_w', 'grad_sb_w_o': 'grad_w', 'grad_kv_norm': 'grad_w', 'grad_mla_w_dkv': 'grad_w', 'grad_mla_kv_lat_norm': 'grad_w', 'grad_mla_w_ukv': 'grad_w', 'grad_mla_w_dq': 'grad_w', 'grad_mla_q_lat_norm': 'grad_w', 'grad_mla_w_uq': 'grad_w', 'grad_mla_w_o': 'grad_w', 'grad_mlp_w1': 'grad_w', 'grad_mlp_w2': 'grad_w', 'grad_final_norm': 'grad_w', 'delta_attn_norm': 'delta_w', 'delta_mlp_norm': 'delta_w', 'delta_sb_w_qkv': 'delta_w', 'delta_sb_w_o': 'delta_w', 'delta_kv_norm': 'delta_w', 'delta_mla_w_dkv': 'delta_w', 'delta_mla_kv_lat_norm': 'delta_w', 'delta_mla_w_ukv': 'delta_w', 'delta_mla_w_dq': 'delta_w', 'delta_mla_q_lat_norm': 'delta_w', 'delta_mla_w_uq': 'delta_w', 'delta_mla_w_o': 'delta_w', 'delta_mlp_w1': 'delta_w', 'delta_mlp_w2': 'delta_w', 'delta_final_norm': 'delta_w', 'new_m_attn_norm': 'new_m', 'new_m_mlp_norm': 'new_m', 'new_m_sb_w_qkv': 'new_m', 'new_m_sb_w_o': 'new_m', 'new_m_kv_norm': 'new_m', 'new_m_mla_w_dkv': 'new_m', 'new_m_mla_kv_lat_norm': 'new_m', 'new_m_mla_w_ukv': 'new_m', 'new_m_mla_w_dq': 'new_m', 'new_m_mla_q_lat_norm': 'new_m', 'new_m_mla_w_uq': 'new_m', 'new_m_mla_w_o': 'new_m', 'new_m_mlp_w1': 'new_m', 'new_m_mlp_w2': 'new_m', 'new_m_final_norm': 'new_m', 'new_v_attn_norm': 'new_v', 'new_v_mlp_norm': 'new_v', 'new_v_sb_w_qkv': 'new_v', 'new_v_sb_w_o': 'new_v', 'new_v_kv_norm': 'new_v', 'new_v_mla_w_dkv': 'new_v', 'new_v_mla_kv_lat_norm': 'new_v', 'new_v_mla_w_ukv': 'new_v', 'new_v_mla_w_dq': 'new_v', 'new_v_mla_q_lat_norm': 'new_v', 'new_v_mla_w_uq': 'new_v', 'new_v_mla_w_o': 'new_v', 'new_v_mlp_w1': 'new_v', 'new_v_mlp_w2': 'new_v', 'new_v_final_norm': 'new_v'}


def _forward(args):
    return _fwd_reference(*[args[k] for k in FWD_PARAMS])


def _output_shape():
    def fwd():
        inp = _fwd_setup_inputs(0)
        return _fwd_reference(*[inp[k] for k in FWD_PARAMS])
    out = _jax.eval_shape(fwd)
    return out.shape, out.dtype

N_MICROBATCH = 1
ADAM_LR = 0.001
ADAM_B1 = 0.9
ADAM_B2 = 0.999
ADAM_EPS = 1e-08
ADAM_WD = 0.01
ADAM_STEP = 10
PER_EXAMPLE_BATCH_AXIS = {'x': 0, 'positions': 0, 'loss_target': 0}
SHARED_INPUTS = []
_WEIGHT_DTYPES = {'attn_norm': _jnp.float32, 'mlp_norm': _jnp.float32, 'sb_w_qkv': _jnp.float32, 'sb_w_o': _jnp.float32, 'kv_norm': _jnp.float32, 'mla_w_dkv': _jnp.float32, 'mla_kv_lat_norm': _jnp.float32, 'mla_w_ukv': _jnp.float32, 'mla_w_dq': _jnp.float32, 'mla_q_lat_norm': _jnp.float32, 'mla_w_uq': _jnp.float32, 'mla_w_o': _jnp.float32, 'mlp_w1': _jnp.float32, 'mlp_w2': _jnp.float32, 'final_norm': _jnp.float32}
MOMENT_SCALE = {'attn_norm': 5.043298e-02, 'mlp_norm': 1.070834e-01, 'sb_w_qkv': 4.020037e-02, 'sb_w_o': 1.664287e-01, 'kv_norm': 3.120316e-02, 'mla_w_dkv': 5.919534e-02, 'mla_kv_lat_norm': 7.620972e-02, 'mla_w_ukv': 2.414448e-02, 'mla_w_dq': 1.672300e-02, 'mla_q_lat_norm': 1.725472e-02, 'mla_w_uq': 8.357067e-03, 'mla_w_o': 7.577933e-02, 'mlp_w1': 5.292691e-02, 'mlp_w2': 3.004388e-01, 'final_norm': 6.448401e+01}


def _to_microbatches(a, axis):
    t = _jnp.moveaxis(a, axis, 0)
    t = t.reshape((N_MICROBATCH, t.shape[0] // N_MICROBATCH) + t.shape[1:])
    return _jnp.moveaxis(t, 1, axis + 1)


def setup_inputs(seed: int = 0) -> dict:
    inp = _fwd_setup_inputs(seed)
    key = _jax.random.fold_in(_jax.random.key(seed), 7919)
    shape, _ = _output_shape()
    out = dict(inp)
    out["loss_target"] = _jax.random.normal(_jax.random.fold_in(key, 0), shape, _jnp.float32)
    for i, name in enumerate(TWIN_WEIGHTS):
        w = inp[name].astype(_jnp.float32)
        if MOMENT_SCALE is None:
            s = _jnp.sqrt(_jnp.mean(_jnp.square(w)) + 1e-30)
        else:
            s = MOMENT_SCALE[name]
        km, kv = _jax.random.split(_jax.random.fold_in(key, i + 1))
        out[name] = w
        out["m_" + name] = s * _jax.random.normal(km, w.shape, _jnp.float32)
        out["v_" + name] = (s * s) * _jax.random.uniform(kv, w.shape, _jnp.float32, 0.5, 1.5)
    if N_MICROBATCH > 1:
        for name, axis in PER_EXAMPLE_BATCH_AXIS.items():
            out[name] = _to_microbatches(out[name], axis)
    return {'x': out['x'], 'positions': out['positions'], 'attn_norm': out['attn_norm'], 'mlp_norm': out['mlp_norm'], 'sb_w_qkv': out['sb_w_qkv'], 'sb_w_o': out['sb_w_o'], 'kv_norm': out['kv_norm'], 'mla_w_dkv': out['mla_w_dkv'], 'mla_kv_lat_norm': out['mla_kv_lat_norm'], 'mla_w_ukv': out['mla_w_ukv'], 'mla_w_dq': out['mla_w_dq'], 'mla_q_lat_norm': out['mla_q_lat_norm'], 'mla_w_uq': out['mla_w_uq'], 'mla_w_o': out['mla_w_o'], 'mlp_w1': out['mlp_w1'], 'mlp_w2': out['mlp_w2'], 'final_norm': out['final_norm'], 'loss_target': out['loss_target'], 'm_attn_norm': out['m_attn_norm'], 'm_mlp_norm': out['m_mlp_norm'], 'm_sb_w_qkv': out['m_sb_w_qkv'], 'm_sb_w_o': out['m_sb_w_o'], 'm_kv_norm': out['m_kv_norm'], 'm_mla_w_dkv': out['m_mla_w_dkv'], 'm_mla_kv_lat_norm': out['m_mla_kv_lat_norm'], 'm_mla_w_ukv': out['m_mla_w_ukv'], 'm_mla_w_dq': out['m_mla_w_dq'], 'm_mla_q_lat_norm': out['m_mla_q_lat_norm'], 'm_mla_w_uq': out['m_mla_w_uq'], 'm_mla_w_o': out['m_mla_w_o'], 'm_mlp_w1': out['m_mlp_w1'], 'm_mlp_w2': out['m_mlp_w2'], 'm_final_norm': out['m_final_norm'], 'v_attn_norm': out['v_attn_norm'], 'v_mlp_norm': out['v_mlp_norm'], 'v_sb_w_qkv': out['v_sb_w_qkv'], 'v_sb_w_o': out['v_sb_w_o'], 'v_kv_norm': out['v_kv_norm'], 'v_mla_w_dkv': out['v_mla_w_dkv'], 'v_mla_kv_lat_norm': out['v_mla_kv_lat_norm'], 'v_mla_w_ukv': out['v_mla_w_ukv'], 'v_mla_w_dq': out['v_mla_w_dq'], 'v_mla_q_lat_norm': out['v_mla_q_lat_norm'], 'v_mla_w_uq': out['v_mla_w_uq'], 'v_mla_w_o': out['v_mla_w_o'], 'v_mlp_w1': out['v_mlp_w1'], 'v_mlp_w2': out['v_mlp_w2'], 'v_final_norm': out['v_final_norm']}


def _loss(weights, diff, rest, loss_target):
    with _jax.named_scope("forward"):
        args = {**rest, TWIN_DIFF_INPUT: diff, **{k: w.astype(_WEIGHT_DTYPES[k]) for k, w in weights.items()}}
        y = _forward(args)
    with _jax.named_scope("loss_head"):
        err = _jnp.square(y.astype(_jnp.float32) - loss_target)
        return 0.5 * _jnp.sum(_jnp.mean(err, axis=-1)) if err.ndim else 0.5 * err


def _adamw(w, g, m, v):
    m = ADAM_B1 * m + (1.0 - ADAM_B1) * g
    v = ADAM_B2 * v + (1.0 - ADAM_B2) * _jnp.square(g)
    m_hat = m / (1.0 - ADAM_B1 ** ADAM_STEP)
    v_hat = v / (1.0 - ADAM_B2 ** ADAM_STEP)
    delta = -ADAM_LR * (m_hat / (_jnp.sqrt(v_hat) + ADAM_EPS) + ADAM_WD * w)
    return delta, m, v


def reference(x, positions, attn_norm, mlp_norm, sb_w_qkv, sb_w_o, kv_norm, mla_w_dkv, mla_kv_lat_norm, mla_w_ukv, mla_w_dq, mla_q_lat_norm, mla_w_uq, mla_w_o, mlp_w1, mlp_w2, final_norm, loss_target, m_attn_norm, m_mlp_norm, m_sb_w_qkv, m_sb_w_o, m_kv_norm, m_mla_w_dkv, m_mla_kv_lat_norm, m_mla_w_ukv, m_mla_w_dq, m_mla_q_lat_norm, m_mla_w_uq, m_mla_w_o, m_mlp_w1, m_mlp_w2, m_final_norm, v_attn_norm, v_mlp_norm, v_sb_w_qkv, v_sb_w_o, v_kv_norm, v_mla_w_dkv, v_mla_kv_lat_norm, v_mla_w_ukv, v_mla_w_dq, v_mla_q_lat_norm, v_mla_w_uq, v_mla_w_o, v_mlp_w1, v_mlp_w2, v_final_norm):
    given = dict(x=x, positions=positions, attn_norm=attn_norm, mlp_norm=mlp_norm, sb_w_qkv=sb_w_qkv, sb_w_o=sb_w_o, kv_norm=kv_norm, mla_w_dkv=mla_w_dkv, mla_kv_lat_norm=mla_kv_lat_norm, mla_w_ukv=mla_w_ukv, mla_w_dq=mla_w_dq, mla_q_lat_norm=mla_q_lat_norm, mla_w_uq=mla_w_uq, mla_w_o=mla_w_o, mlp_w1=mlp_w1, mlp_w2=mlp_w2, final_norm=final_norm, loss_target=loss_target, m_attn_norm=m_attn_norm, m_mlp_norm=m_mlp_norm, m_sb_w_qkv=m_sb_w_qkv, m_sb_w_o=m_sb_w_o, m_kv_norm=m_kv_norm, m_mla_w_dkv=m_mla_w_dkv, m_mla_kv_lat_norm=m_mla_kv_lat_norm, m_mla_w_ukv=m_mla_w_ukv, m_mla_w_dq=m_mla_w_dq, m_mla_q_lat_norm=m_mla_q_lat_norm, m_mla_w_uq=m_mla_w_uq, m_mla_w_o=m_mla_w_o, m_mlp_w1=m_mlp_w1, m_mlp_w2=m_mlp_w2, m_final_norm=m_final_norm, v_attn_norm=v_attn_norm, v_mlp_norm=v_mlp_norm, v_sb_w_qkv=v_sb_w_qkv, v_sb_w_o=v_sb_w_o, v_kv_norm=v_kv_norm, v_mla_w_dkv=v_mla_w_dkv, v_mla_kv_lat_norm=v_mla_kv_lat_norm, v_mla_w_ukv=v_mla_w_ukv, v_mla_w_dq=v_mla_w_dq, v_mla_q_lat_norm=v_mla_q_lat_norm, v_mla_w_uq=v_mla_w_uq, v_mla_w_o=v_mla_w_o, v_mlp_w1=v_mlp_w1, v_mlp_w2=v_mlp_w2, v_final_norm=v_final_norm)
    weights = {n: given[n] for n in TWIN_WEIGHTS}
    shared = {n: given[n] for n in SHARED_INPUTS}
    per_example = {n: given[n] for n in ['x', 'positions']}
    grad_fn = _jax.value_and_grad(_loss, argnums=(0, 1))

    def one_microbatch(ex, loss_target):
        ex = dict(ex)
        diff = ex.pop(TWIN_DIFF_INPUT)
        return grad_fn(weights, diff, {**shared, **ex}, loss_target)

    if N_MICROBATCH == 1:
        loss, (grad_w, grad_x) = one_microbatch(per_example, given["loss_target"])
    else:
        def body(carry, xs):
            loss_sum, grad_sum = carry
            l_k, (gw_k, gx_k) = one_microbatch(xs[0], xs[1])
            with _jax.named_scope("update"):
                return (loss_sum + l_k, _jax.tree.map(_jnp.add, grad_sum, gw_k)), gx_k

        init = (_jnp.zeros((), _jnp.float32), _jax.tree.map(_jnp.zeros_like, weights))
        (loss, grad_w), grad_x = _jax.lax.scan(body, init, (per_example, given["loss_target"]))
    with _jax.named_scope("update"):
        delta_w, new_m, new_v = {}, {}, {}
        for n in TWIN_WEIGHTS:
            delta_w[n], new_m[n], new_v[n] = _adamw(weights[n], grad_w[n], given["m_" + n], given["v_" + n])
    return (loss, grad_x, *[grad_w[n] for n in TWIN_WEIGHTS], *[delta_w[n] for n in TWIN_WEIGHTS],
            *[new_m[n] for n in TWIN_WEIGHTS], *[new_v[n] for n in TWIN_WEIGHTS])
```

```python
import functools
import math

import jax
import jax.numpy as jnp
from jax import lax
from jax.experimental import pallas as pl
from jax.experimental.pallas import tpu as pltpu

F32 = jnp.float32
BF16 = jnp.bfloat16
MESH = pl.DeviceIdType.MESH

N_DEV = 8
DEPTH = 4
N_A = 2
SB_HEADS = 16
SB_HD = 64
MLA_HEADS = 16
NOPE = 64
ROPE = 32
VDIM = 64
QK = NOPE + ROPE
KV_RANK = 256
DKV_PAD = 384
CHUNK_SHIFT = 6
ROPE_THETA = 10000.0
EPS = 1e-6
ATT_BLOCK = 256
PACK_COLS = 1024
PACK_ROW_ALIGN = 128
EXP_ZERO = -104.0
NEG_BIG = -1e30
VMEM_ATT = 56 * 1024 * 1024
VMEM_MM = 40 * 1024 * 1024

ADAM_LR = 0.001
ADAM_B1 = 0.9
ADAM_B2 = 0.999
ADAM_EPS = 1e-08
ADAM_WD = 0.01
ADAM_STEP = 10

WEIGHT_ORDER = ['attn_norm', 'mlp_norm', 'sb_w_qkv', 'sb_w_o', 'kv_norm', 'mla_w_dkv', 'mla_kv_lat_norm',
                'mla_w_ukv', 'mla_w_dq', 'mla_q_lat_norm', 'mla_w_uq', 'mla_w_o', 'mlp_w1', 'mlp_w2', 'final_norm']
SHARDED = [('sb_w_qkv', 2), ('sb_w_o', 1), ('mla_w_dkv', 0), ('mla_w_ukv', 1), ('mla_w_dq', 1),
           ('mla_w_uq', 2), ('mla_w_o', 1), ('mlp_w1', 2), ('mlp_w2', 1)]
REPLICATED = ['attn_norm', 'mlp_norm', 'kv_norm', 'mla_kv_lat_norm', 'mla_q_lat_norm', 'final_norm']


def _tile(dim, prefs=(512, 384, 256, 128, 64, 32, 16, 8)):
    for t in prefs:
        if dim % t == 0:
            return t
    return dim


def _dot(a, b, dims):
    return lax.dot_general(a, b, (dims, ((), ())), preferred_element_type=F32)


NN = ((1,), (0,))
NT = ((1,), (1,))
TN = ((0,), (0,))


def _all_gather(x_shard, name):
    m_per, n = x_shard.shape

    def body(x_ref, out_ref, send_sems, recv_sems, local_sem):
        x, y, c = lax.axis_index("x"), lax.axis_index("y"), lax.axis_index("c")
        me, sibling = (x, y, c), (x, y, 1 - c)
        chips = [(1 - x, y), (x, 1 - y), (1 - x, 1 - y)]

        def rows(px, py, pc):
            return out_ref.at[pl.ds((4 * px + 2 * py + pc) * m_per, m_per), :]

        def copy(k, block, to, src=None):
            return pltpu.make_async_remote_copy(
                src_ref=rows(*block) if src is None else src, dst_ref=rows(*block),
                send_sem=send_sems.at[k], recv_sem=recv_sems.at[k], device_id=to, device_id_type=MESH)

        mine = pltpu.make_async_copy(x_ref, rows(*me), local_sem)
        mine.start()
        first = [copy(0, me, sibling, src=x_ref)]
        first += [copy(1 + j, me, (*chip, c), src=x_ref) for j, chip in enumerate(chips)]
        for cp in first:
            cp.start()
        passed = [copy(4 + j, (*chip, c), sibling) for j, chip in enumerate(chips)]
        for j, chip in enumerate(chips):
            copy(1 + j, (*chip, c), me).wait_recv()
            passed[j].start()
        copy(0, sibling, me).wait_recv()
        for j, chip in enumerate(chips):
            copy(4 + j, (*chip, 1 - c), me).wait_recv()
        for cp in first + passed:
            cp.wait_send()
        mine.wait()

    return pl.pallas_call(
        body, name=name,
        out_shape=jax.ShapeDtypeStruct((N_DEV * m_per, n), x_shard.dtype),
        in_specs=[pl.BlockSpec(memory_space=pl.ANY)],
        out_specs=pl.BlockSpec(memory_space=pl.ANY),
        scratch_shapes=[pltpu.SemaphoreType.DMA((7,)), pltpu.SemaphoreType.DMA((7,)), pltpu.SemaphoreType.DMA],
    )(x_shard)


def _all_to_all(x, name):
    def body(x_ref, out_ref, send_sems, recv_sems, local_sem):
        mx, my, mc = lax.axis_index("x"), lax.axis_index("y"), lax.axis_index("c")
        me = 4 * mx + 2 * my + mc
        mine = pltpu.make_async_copy(x_ref.at[me], out_ref.at[me], local_sem)
        mine.start()
        copies = []
        for k in range(1, N_DEV):
            px = 1 - mx if (k >> 2) & 1 else mx
            py = 1 - my if (k >> 1) & 1 else my
            pc = 1 - mc if k & 1 else mc
            peer = 4 * px + 2 * py + pc
            copies.append(pltpu.make_async_remote_copy(
                src_ref=x_ref.at[peer], dst_ref=out_ref.at[me],
                send_sem=send_sems.at[k - 1], recv_sem=recv_sems.at[k - 1],
                device_id=(px, py, pc), device_id_type=MESH))
        for cp in copies:
            cp.start()
        for cp in copies:
            cp.wait_send()
        for cp in copies:
            cp.wait_recv()
        mine.wait()

    return pl.pallas_call(
        body, name=name,
        out_shape=jax.ShapeDtypeStruct(x.shape, x.dtype),
        in_specs=[pl.BlockSpec(memory_space=pl.ANY)],
        out_specs=pl.BlockSpec(memory_space=pl.ANY),
        scratch_shapes=[pltpu.SemaphoreType.DMA((7,)), pltpu.SemaphoreType.DMA((7,)), pltpu.SemaphoreType.DMA],
    )(x)


def _matmul(a, b, *, ta=False, tb=False, out_dtypes=(F32,), epilogue=None, extras=(), name):
    if ta:
        kdim, m = a.shape
    else:
        m, kdim = a.shape
    if tb:
        n, kb = b.shape
    else:
        kb, n = b.shape
    assert kdim == kb, (a.shape, b.shape, ta, tb)
    tm, tn, tk = _tile(m), _tile(n), _tile(kdim)
    nk = kdim // tk
    n_extra, n_out = len(extras), len(out_dtypes)
    a_spec = pl.BlockSpec((tk, tm), lambda i, j, k: (k, i)) if ta else pl.BlockSpec((tm, tk), lambda i, j, k: (i, k))
    b_spec = pl.BlockSpec((tn, tk), lambda i, j, k: (j, k)) if tb else pl.BlockSpec((tk, tn), lambda i, j, k: (k, j))
    tile_spec = pl.BlockSpec((tm, tn), lambda i, j, k: (i, j))
    dims = ((0,) if ta else (1,), (1,) if tb else (0,))

    def body(a_ref, b_ref, *rest):
        extra_refs = rest[:n_extra]
        out_refs = rest[n_extra:n_extra + n_out]
        acc_ref = rest[-1]
        k = pl.program_id(2)

        @pl.when(k == 0)
        def _():
            acc_ref[...] = jnp.zeros_like(acc_ref)

        acc_ref[...] += _dot(a_ref[...].astype(BF16), b_ref[...].astype(BF16), dims)

        @pl.when(k == nk - 1)
        def _():
            acc = acc_ref[...]
            outs = (acc,) if epilogue is None else epilogue(acc, *[r[...] for r in extra_refs])
            for o_ref, o in zip(out_refs, outs):
                o_ref[...] = o.astype(o_ref.dtype)

    return pl.pallas_call(
        body, name=name, grid=(m // tm, n // tn, nk),
        in_specs=[a_spec, b_spec] + [tile_spec] * n_extra,
        out_specs=[tile_spec] * n_out,
        out_shape=[jax.ShapeDtypeStruct((m, n), dt) for dt in out_dtypes],
        scratch_shapes=[pltpu.VMEM((tm, tn), F32)],
        compiler_params=pltpu.CompilerParams(
            dimension_semantics=("parallel", "parallel", "arbitrary"), vmem_limit_bytes=VMEM_MM),
    )(a, b, *extras)


def _rms_fwd(x, g, name):
    m, d = x.shape
    tm = _tile(m, (512, 256, 128, 64, 32, 16, 8))

    def body(x_ref, g_ref, y_ref):
        xv = x_ref[...]
        r = lax.rsqrt(jnp.mean(xv * xv, axis=-1, keepdims=True) + EPS)
        y_ref[...] = (xv * r * g_ref[...]).astype(y_ref.dtype)

    return pl.pallas_call(
        body, name=name, grid=(m // tm,),
        in_specs=[pl.BlockSpec((tm, d), lambda i: (i, 0)), pl.BlockSpec((1, d), lambda i: (0, 0))],
        out_specs=pl.BlockSpec((tm, d), lambda i: (i, 0)),
        out_shape=jax.ShapeDtypeStruct((m, d), BF16),
        compiler_params=pltpu.CompilerParams(dimension_semantics=("parallel",)),
    )(x, g.reshape(1, d))


def _rms_bwd(x, g, dy, name):
    m, d = x.shape
    tm = _tile(m, (512, 256, 128, 64, 32, 16, 8))

    def body(x_ref, g_ref, dy_ref, dx_ref, dg_ref):
        xv = x_ref[...]
        dyv = dy_ref[...]
        r = lax.rsqrt(jnp.mean(xv * xv, axis=-1, keepdims=True) + EPS)
        xh = xv * r
        t = dyv * g_ref[...]
        dx_ref[...] = r * (t - xh * jnp.mean(t * xh, axis=-1, keepdims=True))

        @pl.when(pl.program_id(0) == 0)
        def _():
            dg_ref[...] = jnp.zeros_like(dg_ref)

        dg_ref[...] += jnp.sum(dyv * xh, axis=0, keepdims=True)

    dx, dg = pl.pallas_call(
        body, name=name, grid=(m // tm,),
        in_specs=[pl.BlockSpec((tm, d), lambda i: (i, 0)), pl.BlockSpec((1, d), lambda i: (0, 0)),
                  pl.BlockSpec((tm, d), lambda i: (i, 0))],
        out_specs=[pl.BlockSpec((tm, d), lambda i: (i, 0)), pl.BlockSpec((1, d), lambda i: (0, 0))],
        out_shape=[jax.ShapeDtypeStruct((m, d), F32), jax.ShapeDtypeStruct((1, d), F32)],
        compiler_params=pltpu.CompilerParams(dimension_semantics=("arbitrary",)),
    )(x, g.reshape(1, d), dy)
    return dx, dg.reshape(d)


def _loss_head(x, g, target):
    m, d = x.shape
    tm = _tile(m, (512, 256, 128, 64, 32, 16, 8))

    def body(x_ref, g_ref, t_ref, loss_ref, dx_ref, dg_ref):
        xv = x_ref[...]
        gv = g_ref[...]
        r = lax.rsqrt(jnp.mean(xv * xv, axis=-1, keepdims=True) + EPS)
        xh = xv * r
        err = xh * gv - t_ref[...]
        row_loss = jnp.mean(err * err, axis=-1, keepdims=True)
        dyv = err * (1.0 / d)
        t = dyv * gv
        dx_ref[...] = r * (t - xh * jnp.mean(t * xh, axis=-1, keepdims=True))

        @pl.when(pl.program_id(0) == 0)
        def _():
            dg_ref[...] = jnp.zeros_like(dg_ref)
            loss_ref[...] = jnp.zeros_like(loss_ref)

        dg_ref[...] += jnp.sum(dyv * xh, axis=0, keepdims=True)
        loss_ref[...] += 0.5 * jnp.sum(row_loss, axis=0, keepdims=True)

    loss, dx, dg = pl.pallas_call(
        body, name="loss_head", grid=(m // tm,),
        in_specs=[pl.BlockSpec((tm, d), lambda i: (i, 0)), pl.BlockSpec((1, d), lambda i: (0, 0)),
                  pl.BlockSpec((tm, d), lambda i: (i, 0))],
        out_specs=[pl.BlockSpec((1, 1), lambda i: (0, 0)), pl.BlockSpec((tm, d), lambda i: (i, 0)),
                   pl.BlockSpec((1, d), lambda i: (0, 0))],
        out_shape=[jax.ShapeDtypeStruct((1, 1), F32), jax.ShapeDtypeStruct((m, d), F32),
                   jax.ShapeDtypeStruct((1, d), F32)],
        compiler_params=pltpu.CompilerParams(dimension_semantics=("arbitrary",)),
    )(x, g.reshape(1, d), target)
    return loss, dx, dg.reshape(d)


def _rope_apply(x, tabs, transpose, name):
    m, w = x.shape
    wt = tabs[0].shape[1]
    reps = w // wt
    half = ROPE // 2
    tm = _tile(m, (256, 128, 64, 32, 16, 8))

    def body(x_ref, c_ref, a_ref, b_ref, y_ref):
        xv = x_ref[...]

        def wide(t_ref):
            t = t_ref[...]
            return t if reps == 1 else jnp.concatenate([t] * reps, axis=1)

        c, a, b = wide(c_ref), wide(a_ref), wide(b_ref)
        if transpose:
            y = xv * c + pltpu.roll(xv * a, half, 1) + pltpu.roll(xv * b, w - half, 1)
        else:
            y = xv * c + pltpu.roll(xv, w - half, 1) * a + pltpu.roll(xv, half, 1) * b
        y_ref[...] = y

    x_spec = pl.BlockSpec((tm, w), lambda i: (i, 0))
    t_spec = pl.BlockSpec((tm, wt), lambda i: (i, 0))
    return pl.pallas_call(
        body, name=name, grid=(m // tm,),
        in_specs=[x_spec, t_spec, t_spec, t_spec], out_specs=x_spec,
        out_shape=jax.ShapeDtypeStruct((m, w), F32),
        compiler_params=pltpu.CompilerParams(dimension_semantics=("parallel",)),
    )(x, *tabs)


def _log_sigmoid_pair(z):
    a = jnp.minimum(z, 0.0) - jnp.log(1.0 + jnp.exp(-jnp.abs(z)))
    return a, a - z


def _split_bf16(x):
    hi = x.astype(BF16)
    return hi, (x - hi.astype(F32)).astype(BF16)


def _sb_fwd(q, k, v):
    h_, s, d = q.shape
    t = min(ATT_BLOCK, s)
    nq = s // t
    scale = d ** -0.5

    def body(q_ref, k_ref, v_ref, o_ref, bt_ref, js_ref, acc_ref, r_ref):
        h, i = pl.program_id(0), pl.program_id(1)
        qv = q_ref[0]
        row = lax.broadcasted_iota(jnp.int32, (t, t), 0)
        col = lax.broadcasted_iota(jnp.int32, (t, t), 1)
        later = jnp.where(row > col, 1.0, 0.0).astype(BF16)
        acc_ref[...] = jnp.zeros_like(acc_ref)
        r_ref[...] = jnp.zeros_like(r_ref)

        def cond(carry):
            j, rmax = carry
            return jnp.logical_and(j >= 0, rmax > EXP_ZERO)

        def step(carry):
            j, _ = carry
            off = pl.multiple_of(j * t, t)
            kj = k_ref[0, pl.ds(off, t), :]
            vj = v_ref[0, pl.ds(off, t), :]
            z = _dot(qv, kj, NT) * scale
            a, b = _log_sigmoid_pair(z)
            valid = col + (j - i) * t < row
            b = jnp.where(valid, b, 0.0)
            bh, bl = _split_bf16(b)
            inner = _dot(bh, later, NN) + _dot(bl, later, NN)
            r = r_ref[...]
            w = jnp.where(valid, jnp.exp(a + inner + r), 0.0)
            acc_ref[...] += _dot(w.astype(BF16), vj, NN)
            rn = r + jnp.sum(b, axis=1, keepdims=True)
            r_ref[...] = rn
            return j - 1, jnp.max(rn)

        jend, _ = lax.while_loop(cond, step, (i, jnp.float32(0.0)))
        o_ref[0] = acc_ref[...]
        bt_ref[0] = r_ref[...]
        js_ref[h, i] = (jend + 1).astype(F32)

    return pl.pallas_call(
        body, name="sb_attn_fwd", grid=(h_, nq),
        in_specs=[pl.BlockSpec((1, t, d), lambda h, i: (h, i, 0)),
                  pl.BlockSpec((1, s, d), lambda h, i: (h, 0, 0)),
                  pl.BlockSpec((1, s, d), lambda h, i: (h, 0, 0))],
        out_specs=[pl.BlockSpec((1, t, d), lambda h, i: (h, i, 0)),
                   pl.BlockSpec((1, t, 1), lambda h, i: (h, i, 0)),
                   pl.BlockSpec(memory_space=pltpu.SMEM)],
        out_shape=[jax.ShapeDtypeStruct((h_, s, d), F32), jax.ShapeDtypeStruct((h_, s, 1), F32),
                   jax.ShapeDtypeStruct((h_, nq), F32)],
        scratch_shapes=[pltpu.VMEM((t, d), F32), pltpu.VMEM((t, 1), F32)],
        compiler_params=pltpu.CompilerParams(
            dimension_semantics=("arbitrary", "arbitrary"), vmem_limit_bytes=VMEM_ATT),
    )(q, k, v)


def _sb_bwd(q, k, v, do, btot, jstart):
    h_, s, d = q.shape
    t = min(ATT_BLOCK, s)
    nq = s // t
    scale = d ** -0.5

    def body(js_ref, q_ref, k_ref, v_ref, do_ref, bt_ref, dq_ref, dk_ref, dv_ref, dq_acc, p_ref, g_ref):
        h, i = pl.program_id(0), pl.program_id(1)

        @pl.when(i == 0)
        def _():
            dk_ref[...] = jnp.zeros_like(dk_ref)
            dv_ref[...] = jnp.zeros_like(dv_ref)

        qv = q_ref[0]
        dov = do_ref[0].astype(BF16)
        bt = bt_ref[0]
        row = lax.broadcasted_iota(jnp.int32, (t, t), 0)
        col = lax.broadcasted_iota(jnp.int32, (t, t), 1)
        upto = jnp.where(row <= col, 1.0, 0.0).astype(BF16)
        before = jnp.where(row < col, 1.0, 0.0).astype(BF16)
        dq_acc[...] = jnp.zeros_like(dq_acc)
        p_ref[...] = jnp.zeros_like(p_ref)
        g_ref[...] = jnp.zeros_like(g_ref)
        j0 = jnp.clip(js_ref[h, i].astype(jnp.int32), 0, i)

        def step(j, carry):
            off = pl.multiple_of(j * t, t)
            kj = k_ref[0, pl.ds(off, t), :]
            vj = v_ref[0, pl.ds(off, t), :]
            z = _dot(qv, kj, NT) * scale
            a, b = _log_sigmoid_pair(z)
            valid = col + (j - i) * t < row
            b = jnp.where(valid, b, 0.0)
            bh, bl = _split_bf16(b)
            pin = _dot(bh, upto, NN) + _dot(bl, upto, NN)
            surv = bt - (p_ref[...] + pin)
            w = jnp.where(valid, jnp.exp(a + surv), 0.0)
            dw = _dot(dov, vj, NT)
            g = w * dw
            gh, gl = _split_bf16(g)
            gsum = g_ref[...] + _dot(gh, before, NN) + _dot(gl, before, NN)
            beta = jnp.exp(a)
            dz = jnp.where(valid, (g * (1.0 - beta) - gsum * beta) * scale, 0.0).astype(BF16)
            dq_acc[...] += _dot(dz, kj, NN)
            dk_ref[0, pl.ds(off, t), :] += _dot(dz, qv, TN)
            dv_ref[0, pl.ds(off, t), :] += _dot(w.astype(BF16), dov, TN)
            p_ref[...] += jnp.sum(b, axis=1, keepdims=True)
            g_ref[...] += jnp.sum(g, axis=1, keepdims=True)
            return carry

        lax.fori_loop(j0, i + 1, step, 0)
        dq_ref[0] = dq_acc[...]

    blk = pl.BlockSpec((1, t, d), lambda h, i: (h, i, 0))
    full = pl.BlockSpec((1, s, d), lambda h, i: (h, 0, 0))
    return pl.pallas_call(
        body, name="sb_attn_bwd", grid=(h_, nq),
        in_specs=[pl.BlockSpec(memory_space=pltpu.SMEM), blk, full, full, blk,
                  pl.BlockSpec((1, t, 1), lambda h, i: (h, i, 0))],
        out_specs=[blk, full, full],
        out_shape=[jax.ShapeDtypeStruct((h_, s, d), F32)] * 3,
        scratch_shapes=[pltpu.VMEM((t, d), F32), pltpu.VMEM((t, 1), F32), pltpu.VMEM((t, 1), F32)],
        compiler_params=pltpu.CompilerParams(
            dimension_semantics=("arbitrary", "arbitrary"), vmem_limit_bytes=VMEM_ATT),
    )(jstart, q, k, v, do, btot)


@jax.custom_vjp
def sb_attention(q, k, v):
    return _sb_fwd(q.astype(BF16), k.astype(BF16), v.astype(BF16))[0]


def _sb_attention_fwd(q, k, v):
    qb, kb, vb = q.astype(BF16), k.astype(BF16), v.astype(BF16)
    o, btot, jstart = _sb_fwd(qb, kb, vb)
    return o, (qb, kb, vb, btot, jstart)


def _sb_attention_bwd(res, do):
    qb, kb, vb, btot, jstart = res
    return tuple(_sb_bwd(qb, kb, vb, do, btot, jstart))


sb_attention.defvjp(_sb_attention_fwd, _sb_attention_bwd)


def _chunk_allowed(row, col, i, j, t):
    return ((col + j * t) >> CHUNK_SHIFT) <= ((row + i * t) >> CHUNK_SHIFT)


def _mla_fwd(q, k, v):
    h_, s, dqk = q.shape
    dv = v.shape[2]
    t = min(ATT_BLOCK, s)
    nq = s // t
    scale = dqk ** -0.5

    def body(q_ref, k_ref, v_ref, o_ref, lse_ref, acc_ref, m_ref, l_ref):
        i = pl.program_id(1)
        qv = q_ref[0]
        row = lax.broadcasted_iota(jnp.int32, (t, t), 0)
        col = lax.broadcasted_iota(jnp.int32, (t, t), 1)
        acc_ref[...] = jnp.zeros_like(acc_ref)
        m_ref[...] = jnp.full_like(m_ref, NEG_BIG)
        l_ref[...] = jnp.zeros_like(l_ref)

        def step(j, carry):
            off = pl.multiple_of(j * t, t)
            kj = k_ref[0, pl.ds(off, t), :]
            vj = v_ref[0, pl.ds(off, t), :]
            sc = _dot(qv, kj, NT) * scale
            sc = jnp.where(_chunk_allowed(row, col, i, j, t), sc, NEG_BIG)
            m_old = m_ref[...]
            m_new = jnp.maximum(m_old, jnp.max(sc, axis=1, keepdims=True))
            p = jnp.exp(sc - m_new)
            alpha = jnp.exp(m_old - m_new)
            l_ref[...] = alpha * l_ref[...] + jnp.sum(p, axis=1, keepdims=True)
            acc_ref[...] = alpha * acc_ref[...] + _dot(p.astype(BF16), vj, NN)
            m_ref[...] = m_new
            return carry

        lax.fori_loop(0, i + 1, step, 0)
        o_ref[0] = acc_ref[...] / l_ref[...]
        lse_ref[0] = m_ref[...] + jnp.log(l_ref[...])

    return pl.pallas_call(
        body, name="mla_attn_fwd", grid=(h_, nq),
        in_specs=[pl.BlockSpec((1, t, dqk), lambda h, i: (h, i, 0)),
                  pl.BlockSpec((1, s, dqk), lambda h, i: (h, 0, 0)),
                  pl.BlockSpec((1, s, dv), lambda h, i: (h, 0, 0))],
        out_specs=[pl.BlockSpec((1, t, dv), lambda h, i: (h, i, 0)),
                   pl.BlockSpec((1, t, 1), lambda h, i: (h, i, 0))],
        out_shape=[jax.ShapeDtypeStruct((h_, s, dv), F32), jax.ShapeDtypeStruct((h_, s, 1), F32)],
        scratch_shapes=[pltpu.VMEM((t, dv), F32), pltpu.VMEM((t, 1), F32), pltpu.VMEM((t, 1), F32)],
        compiler_params=pltpu.CompilerParams(
            dimension_semantics=("arbitrary", "arbitrary"), vmem_limit_bytes=VMEM_ATT),
    )(q, k, v)


def _mla_bwd(q, k, v, o, lse, do):
    h_, s, dqk = q.shape
    dv = v.shape[2]
    t = min(ATT_BLOCK, s)
    nq = s // t
    scale = dqk ** -0.5

    def body(q_ref, k_ref, v_ref, o_ref, lse_ref, do_ref, dq_ref, dk_ref, dv_ref, dq_acc):
        i = pl.program_id(1)

        @pl.when(i == 0)
        def _():
            dk_ref[...] = jnp.zeros_like(dk_ref)
            dv_ref[...] = jnp.zeros_like(dv_ref)

        qv = q_ref[0]
        dof = do_ref[0]
        dov = dof.astype(BF16)
        delta = jnp.sum(dof * o_ref[0], axis=1, keepdims=True)
        lse = lse_ref[0]
        row = lax.broadcasted_iota(jnp.int32, (t, t), 0)
        col = lax.broadcasted_iota(jnp.int32, (t, t), 1)
        dq_acc[...] = jnp.zeros_like(dq_acc)

        def step(j, carry):
            off = pl.multiple_of(j * t, t)
            kj = k_ref[0, pl.ds(off, t), :]
            vj = v_ref[0, pl.ds(off, t), :]
            sc = _dot(qv, kj, NT) * scale
            p = jnp.where(_chunk_allowed(row, col, i, j, t), jnp.exp(sc - lse), 0.0)
            dp = _dot(dov, vj, NT)
            ds = (p * (dp - delta) * scale).astype(BF16)
            dq_acc[...] += _dot(ds, kj, NN)
            dk_ref[0, pl.ds(off, t), :] += _dot(ds, qv, TN)
            dv_ref[0, pl.ds(off, t), :] += _dot(p.astype(BF16), dov, TN)
            return carry

        lax.fori_loop(0, i + 1, step, 0)
        dq_ref[0] = dq_acc[...]

    qblk = pl.BlockSpec((1, t, dqk), lambda h, i: (h, i, 0))
    vblk = pl.BlockSpec((1, t, dv), lambda h, i: (h, i, 0))
    kfull = pl.BlockSpec((1, s, dqk), lambda h, i: (h, 0, 0))
    vfull = pl.BlockSpec((1, s, dv), lambda h, i: (h, 0, 0))
    stat = pl.BlockSpec((1, t, 1), lambda h, i: (h, i, 0))
    return pl.pallas_call(
        body, name="mla_attn_bwd", grid=(h_, nq),
        in_specs=[qblk, kfull, vfull, vblk, stat, vblk],
        out_specs=[qblk, kfull, vfull],
        out_shape=[jax.ShapeDtypeStruct((h_, s, dqk), F32), jax.ShapeDtypeStruct((h_, s, dqk), F32),
                   jax.ShapeDtypeStruct((h_, s, dv), F32)],
        scratch_shapes=[pltpu.VMEM((t, dqk), F32)],
        compiler_params=pltpu.CompilerParams(
            dimension_semantics=("arbitrary", "arbitrary"), vmem_limit_bytes=VMEM_ATT),
    )(q, k, v, o, lse, do)


@jax.custom_vjp
def mla_attention(q, k, v):
    return _mla_fwd(q.astype(BF16), k.astype(BF16), v.astype(BF16))[0]


def _mla_attention_fwd(q, k, v):
    qb, kb, vb = q.astype(BF16), k.astype(BF16), v.astype(BF16)
    o, lse = _mla_fwd(qb, kb, vb)
    return o, (qb, kb, vb, o, lse)


def _mla_attention_bwd(res, do):
    qb, kb, vb, o, lse = res
    return tuple(_mla_bwd(qb, kb, vb, o, lse, do))


mla_attention.defvjp(_mla_attention_fwd, _mla_attention_bwd)


def _make_linear(tag):
    @jax.custom_vjp
    def f(a, w):
        return _matmul(a, w, name=tag + "_fwd")[0]

    def fwd(a, w):
        ab = a.astype(BF16)
        return _matmul(ab, w, name=tag + "_fwd")[0], (ab, w)

    def bwd(res, dy):
        ab, w = res
        da = _matmul(dy, w, tb=True, name=tag + "_da")[0]
        dw = _matmul(ab, dy, ta=True, name=tag + "_dw")[0]
        return da, dw

    f.defvjp(fwd, bwd)
    return f


def _make_norm_linear(tag):
    @jax.custom_vjp
    def f(x, g, w):
        return _matmul(_rms_fwd(x, g, tag + "_norm"), w, name=tag + "_fwd")[0]

    def fwd(x, g, w):
        hb = _rms_fwd(x, g, tag + "_norm")
        return _matmul(hb, w, name=tag + "_fwd")[0], (x, g, w, hb)

    def bwd(res, dy):
        x, g, w, hb = res
        dh = _matmul(dy, w, tb=True, name=tag + "_dh")[0]
        dw = _matmul(hb, dy, ta=True, name=tag + "_dw")[0]
        dx, dg = _rms_bwd(x, g, dh, tag + "_norm_bwd")
        return dx, dg, dw

    f.defvjp(fwd, bwd)
    return f


def _relu2_fwd(acc):
    r = jnp.maximum(acc, 0.0)
    return acc, r * r


def _relu2_bwd(acc, u):
    return (acc * (2.0 * jnp.maximum(u, 0.0)),)


def _make_mlp(tag):
    def forward(x, g, w1, w2):
        hb = _rms_fwd(x, g, tag + "_norm")
        u, act = _matmul(hb, w1, out_dtypes=(F32, BF16), epilogue=_relu2_fwd, name=tag + "_up")
        y = _matmul(act, w2, name=tag + "_down")[0]
        return y, (x, g, w1, w2, hb, u, act)

    @jax.custom_vjp
    def f(x, g, w1, w2):
        return forward(x, g, w1, w2)[0]

    def bwd(res, dy):
        x, g, w1, w2, hb, u, act = res
        du = _matmul(dy, w2, tb=True, out_dtypes=(BF16,), epilogue=_relu2_bwd, extras=(u,), name=tag + "_du")[0]
        dw2 = _matmul(act, dy, ta=True, name=tag + "_dw2")[0]
        dw1 = _matmul(hb, du, ta=True, name=tag + "_dw1")[0]
        dh = _matmul(du, w1, tb=True, name=tag + "_dh")[0]
        dx, dg = _rms_bwd(x, g, dh, tag + "_norm_bwd")
        return dx, dg, dw1, dw2

    f.defvjp(forward, bwd)
    return f


def _make_rope(tag):
    @jax.custom_vjp
    def f(x, c, a, b):
        return _rope_apply(x, (c, a, b), False, tag + "_fwd")

    def fwd(x, c, a, b):
        return _rope_apply(x, (c, a, b), False, tag + "_fwd"), (c, a, b)

    def bwd(res, dy):
        c, a, b = res
        return _rope_apply(dy, (c, a, b), True, tag + "_bwd"), jnp.zeros_like(c), jnp.zeros_like(a), jnp.zeros_like(b)

    f.defvjp(fwd, bwd)
    return f


def _rope_tables(positions):
    half = ROPE // 2
    inv_freq = ROPE_THETA ** (-jnp.arange(0, ROPE, 2, dtype=F32) / ROPE)
    ang = positions.astype(F32)[:, None] * inv_freq
    cos, sin = jnp.cos(ang), jnp.sin(ang)
    s = positions.shape[0]
    one = lambda n: jnp.ones((s, n), F32)
    zero = lambda n: jnp.zeros((s, n), F32)
    reps = DKV_PAD // QK
    q_tabs = (jnp.tile(jnp.concatenate([one(NOPE), cos, cos], axis=1), (1, reps)),
              jnp.tile(jnp.concatenate([zero(NOPE), -sin, zero(half)], axis=1), (1, reps)),
              jnp.tile(jnp.concatenate([zero(NOPE + half), sin], axis=1), (1, reps)))
    tail = DKV_PAD - KV_RANK - ROPE
    kv_tabs = (jnp.concatenate([one(KV_RANK), cos, cos, one(tail)], axis=1),
               jnp.concatenate([zero(KV_RANK), -sin, zero(half + tail)], axis=1),
               jnp.concatenate([zero(KV_RANK + half), sin, zero(tail)], axis=1))
    return q_tabs, kv_tabs


def _trunk(x, w, q_tabs, kv_tabs):
    s = x.shape[0]
    kcat = vh = None
    for layer in range(DEPTH):
        if layer < N_A:
            qkv = _make_norm_linear(f"sb{layer}_qkv")(x, w['attn_norm'][layer], w['sb_w_qkv'][layer])
            qkv = qkv.reshape(s, 3, SB_HEADS, SB_HD).transpose(1, 2, 0, 3)
            o = sb_attention(qkv[0], qkv[1], qkv[2])
            o = o.transpose(1, 0, 2).reshape(s, SB_HEADS * SB_HD)
            x = x + _make_linear(f"sb{layer}_o")(o, w['sb_w_o'][layer])
        else:
            j = layer - N_A
            if j == 0:
                w_dkv = jnp.pad(w['mla_w_dkv'], ((0, 0), (0, DKV_PAD - KV_RANK - ROPE)))
                down = _make_norm_linear("kv_down")(x, w['kv_norm'], w_dkv)
                kv = _make_norm_linear("kv_up")(down[:, :KV_RANK], w['mla_kv_lat_norm'], w['mla_w_ukv'])
                kv = kv.reshape(s, MLA_HEADS, NOPE + VDIM)
                k_rope = _make_rope("rope_k")(down, *kv_tabs)[:, KV_RANK:KV_RANK + ROPE]
                k_rope = jnp.broadcast_to(k_rope[:, None, :], (s, MLA_HEADS, ROPE))
                kcat = jnp.concatenate([kv[..., :NOPE], k_rope], axis=-1).transpose(1, 0, 2)
                vh = kv[..., NOPE:].transpose(1, 0, 2)
            c_q = _make_norm_linear(f"mla{j}_dq")(x, w['attn_norm'][layer], w['mla_w_dq'][j])
            q = _make_norm_linear(f"mla{j}_uq")(c_q, w['mla_q_lat_norm'][j], w['mla_w_uq'][j])
            q = _make_rope(f"rope_q{j}")(q, *q_tabs)
            q = q.reshape(s, MLA_HEADS, QK).transpose(1, 0, 2)
            o = mla_attention(q, kcat, vh)
            o = o.transpose(1, 0, 2).reshape(s, MLA_HEADS * VDIM)
            x = x + _make_linear(f"mla{j}_o")(o, w['mla_w_o'][j])
        x = x + _make_mlp(f"mlp{layer}")(x, w['mlp_norm'][layer], w['mlp_w1'][layer], w['mlp_w2'][layer])
    return x


def _pack_rows(n_elems):
    rows = -(-n_elems // PACK_COLS)
    return -(-rows // PACK_ROW_ALIGN) * PACK_ROW_ALIGN


def _pack(parts, dtype):
    flat = jnp.concatenate([p.reshape(-1).astype(dtype) for p in parts])
    rows = _pack_rows(flat.shape[0])
    flat = jnp.pad(flat, (0, rows * PACK_COLS - flat.shape[0]))
    return flat.reshape(rows, PACK_COLS)


def _unpack(packed, shapes):
    flat = packed.reshape(-1)
    out, off = [], 0
    for shp in shapes:
        n = math.prod(shp)
        out.append(flat[off:off + n].reshape(shp))
        off += n
    return out


def _full_from_gathered(gathered, shard_shapes):
    per_dev = gathered.reshape(N_DEV, -1)
    full, off = {}, 0
    for (name, ax), shp in zip(SHARDED, shard_shapes):
        n = math.prod(shp)
        seg = per_dev[:, off:off + n].reshape((N_DEV,) + tuple(shp))
        seg = jnp.moveaxis(seg, 0, ax)
        full[name] = seg.reshape(shp[:ax] + (N_DEV * shp[ax],) + shp[ax + 1:])
        off += n
    return full


def _chunks_from_full(grads, shard_shapes):
    segs = []
    for (name, ax), shp in zip(SHARDED, shard_shapes):
        g = grads[name].reshape(shp[:ax] + (N_DEV, shp[ax]) + shp[ax + 1:])
        segs.append(jnp.moveaxis(g, ax, 0).reshape(N_DEV, -1))
    flat = jnp.concatenate(segs, axis=1)
    rows = _pack_rows(flat.shape[1])
    flat = jnp.pad(flat, ((0, 0), (0, rows * PACK_COLS - flat.shape[1])))
    return flat.reshape(N_DEV, rows, PACK_COLS)


def _adamw_reduce(parts, w, m, v, name):
    _, r, c = parts.shape
    tr = _tile(r, (128, 64, 32, 16, 8))
    bias1 = 1.0 - ADAM_B1 ** ADAM_STEP
    bias2 = 1.0 - ADAM_B2 ** ADAM_STEP

    def body(p_ref, w_ref, m_ref, v_ref, g_ref, d_ref, nm_ref, nv_ref):
        g = p_ref[0]
        for s in range(1, N_DEV):
            g = g + p_ref[s]
        mn = ADAM_B1 * m_ref[...] + (1.0 - ADAM_B1) * g
        vn = ADAM_B2 * v_ref[...] + (1.0 - ADAM_B2) * (g * g)
        m_hat = mn / bias1
        v_hat = vn / bias2
        g_ref[...] = g
        d_ref[...] = -ADAM_LR * (m_hat / (jnp.sqrt(v_hat) + ADAM_EPS) + ADAM_WD * w_ref[...])
        nm_ref[...] = mn
        nv_ref[...] = vn

    blk = pl.BlockSpec((tr, c), lambda i: (i, 0))
    return pl.pallas_call(
        body, name=name, grid=(r // tr,),
        in_specs=[pl.BlockSpec((N_DEV, tr, c), lambda i: (0, i, 0)), blk, blk, blk],
        out_specs=[blk] * 4,
        out_shape=[jax.ShapeDtypeStruct((r, c), F32)] * 4,
        compiler_params=pltpu.CompilerParams(dimension_semantics=("parallel",), vmem_limit_bytes=VMEM_MM),
    )(parts, w, m, v)


def kernel(x, positions, attn_norm, mlp_norm, sb_w_qkv, sb_w_o, kv_norm, mla_w_dkv, mla_kv_lat_norm, mla_w_ukv, mla_w_dq, mla_q_lat_norm, mla_w_uq, mla_w_o, mlp_w1, mlp_w2, final_norm, loss_target, m_attn_norm, m_mlp_norm, m_sb_w_qkv, m_sb_w_o, m_kv_norm, m_mla_w_dkv, m_mla_kv_lat_norm, m_mla_w_ukv, m_mla_w_dq, m_mla_q_lat_norm, m_mla_w_uq, m_mla_w_o, m_mlp_w1, m_mlp_w2, m_final_norm, v_attn_norm, v_mlp_norm, v_sb_w_qkv, v_sb_w_o, v_kv_norm, v_mla_w_dkv, v_mla_kv_lat_norm, v_mla_w_ukv, v_mla_w_dq, v_mla_q_lat_norm, v_mla_w_uq, v_mla_w_o, v_mlp_w1, v_mlp_w2, v_final_norm):
    weights = dict(attn_norm=attn_norm, mlp_norm=mlp_norm, sb_w_qkv=sb_w_qkv, sb_w_o=sb_w_o, kv_norm=kv_norm,
                   mla_w_dkv=mla_w_dkv, mla_kv_lat_norm=mla_kv_lat_norm, mla_w_ukv=mla_w_ukv, mla_w_dq=mla_w_dq,
                   mla_q_lat_norm=mla_q_lat_norm, mla_w_uq=mla_w_uq, mla_w_o=mla_w_o, mlp_w1=mlp_w1, mlp_w2=mlp_w2,
                   final_norm=final_norm)
    mom_m = dict(attn_norm=m_attn_norm, mlp_norm=m_mlp_norm, sb_w_qkv=m_sb_w_qkv, sb_w_o=m_sb_w_o, kv_norm=m_kv_norm,
                 mla_w_dkv=m_mla_w_dkv, mla_kv_lat_norm=m_mla_kv_lat_norm, mla_w_ukv=m_mla_w_ukv, mla_w_dq=m_mla_w_dq,
                 mla_q_lat_norm=m_mla_q_lat_norm, mla_w_uq=m_mla_w_uq, mla_w_o=m_mla_w_o, mlp_w1=m_mlp_w1,
                 mlp_w2=m_mlp_w2, final_norm=m_final_norm)
    mom_v = dict(attn_norm=v_attn_norm, mlp_norm=v_mlp_norm, sb_w_qkv=v_sb_w_qkv, sb_w_o=v_sb_w_o, kv_norm=v_kv_norm,
                 mla_w_dkv=v_mla_w_dkv, mla_kv_lat_norm=v_mla_kv_lat_norm, mla_w_ukv=v_mla_w_ukv, mla_w_dq=v_mla_w_dq,
                 mla_q_lat_norm=v_mla_q_lat_norm, mla_w_uq=v_mla_w_uq, mla_w_o=v_mla_w_o, mlp_w1=v_mlp_w1,
                 mlp_w2=v_mlp_w2, final_norm=v_final_norm)
    sharded_names = [n for n, _ in SHARDED]
    shard_shapes = [tuple(weights[n].shape) for n in sharded_names]
    repl_shapes = [tuple(weights[n].shape) for n in REPLICATED]

    packed_w = _pack([weights[n] for n in sharded_names], BF16)
    gathered = _all_gather(packed_w, "gather_weights")
    full = {n: a.astype(F32) for n, a in _full_from_gathered(gathered, shard_shapes).items()}
    for n in REPLICATED:
        full[n] = weights[n]

    xs = x[0]
    q_tabs, kv_tabs = _rope_tables(positions[0])
    trunk_w = {n: full[n] for n in WEIGHT_ORDER if n != 'final_norm'}
    x_last, pullback = jax.vjp(lambda xx, ww: _trunk(xx, ww, q_tabs, kv_tabs), xs, trunk_w)
    loss_part, dx_last, d_final = _loss_head(x_last, final_norm, loss_target[0])
    dx, dw = pullback(dx_last)
    dw = dict(dw)
    dw['final_norm'] = d_final
    loss = lax.psum(loss_part[0, 0], ("x", "y", "c"))

    recv = _all_to_all(_chunks_from_full(dw, shard_shapes), "scatter_grads")
    repl_parts = _all_gather(_pack([dw[n] for n in REPLICATED], F32), "gather_norm_grads")
    repl_parts = repl_parts.reshape(N_DEV, -1, PACK_COLS)

    results = {}
    res_sh = _adamw_reduce(recv, _pack([weights[n] for n in sharded_names], F32),
                           _pack([mom_m[n] for n in sharded_names], F32),
                           _pack([mom_v[n] for n in sharded_names], F32), "adamw_sharded")
    res_rp = _adamw_reduce(repl_parts, _pack([weights[n] for n in REPLICATED], F32),
                           _pack([mom_m[n] for n in REPLICATED], F32),
                           _pack([mom_v[n] for n in REPLICATED], F32), "adamw_replicated")
    for kind, a_sh, a_rp in zip(("grad", "delta", "new_m", "new_v"), res_sh, res_rp):
        vals = dict(zip(sharded_names, _unpack(a_sh, shard_shapes)))
        vals.update(zip(REPLICATED, _unpack(a_rp, repl_shapes)))
        results[kind] = vals

    out = [loss, dx[None]]
    for kind in ("grad", "delta", "new_m", "new_v"):
        out += [results[kind][n] for n in WEIGHT_ORDER]
    return tuple(out)
```

```python
import math

import jax
import jax.numpy as jnp
from jax import lax
from jax.experimental import pallas as pl
from jax.experimental.pallas import tpu as pltpu

F32 = jnp.float32
BF16 = jnp.bfloat16
MESH = pl.DeviceIdType.MESH

N_DEV = 8
DEPTH = 4
N_A = 2
SB_HEADS = 16
SB_HD = 64
MLA_HEADS = 16
NOPE = 64
ROPE = 32
VDIM = 64
QK = NOPE + ROPE
KV_RANK = 256
DKV_PAD = 384
LANES = 128
CHUNK_SHIFT = 6
ROPE_THETA = 10000.0
EPS = 1e-6
SB_BLOCK = 256
MLA_BLOCK = 512
PACK_COLS = 1024
PACK_ROW_ALIGN = 16
EXP_ZERO = -104.0
NEG_BIG = -1e30
VMEM_ATT = 56 * 1024 * 1024
VMEM_MM = 48 * 1024 * 1024

ADAM_LR = 0.001
ADAM_B1 = 0.9
ADAM_B2 = 0.999
ADAM_EPS = 1e-08
ADAM_WD = 0.01
ADAM_STEP = 10

WEIGHT_ORDER = ['attn_norm', 'mlp_norm', 'sb_w_qkv', 'sb_w_o', 'kv_norm', 'mla_w_dkv', 'mla_kv_lat_norm',
                'mla_w_ukv', 'mla_w_dq', 'mla_q_lat_norm', 'mla_w_uq', 'mla_w_o', 'mlp_w1', 'mlp_w2', 'final_norm']
SHARDED = [('sb_w_qkv', 2), ('sb_w_o', 1), ('mla_w_dkv', 0), ('mla_w_ukv', 1), ('mla_w_dq', 1),
           ('mla_w_uq', 2), ('mla_w_o', 1), ('mlp_w1', 2), ('mlp_w2', 1)]
REPLICATED = ['attn_norm', 'mlp_norm', 'kv_norm', 'mla_kv_lat_norm', 'mla_q_lat_norm', 'final_norm']


def _tile(dim, prefs=(512, 384, 256, 128, 64, 32, 16, 8)):
    for t in prefs:
        if dim % t == 0:
            return t
    return dim


def _dot(a, b, dims):
    return lax.dot_general(a, b, (dims, ((), ())), preferred_element_type=F32)


NN = ((1,), (0,))
NT = ((1,), (1,))
TN = ((0,), (0,))


def _all_gather(shards, name):
    n_t = len(shards)

    def body(*refs):
        x_refs, out_refs = refs[:n_t], refs[n_t:2 * n_t]
        send_sems, recv_sems, local_sems = refs[2 * n_t:]
        x, y, c = lax.axis_index("x"), lax.axis_index("y"), lax.axis_index("c")
        me, sibling = (x, y, c), (x, y, 1 - c)
        chips = [(1 - x, y), (x, 1 - y), (1 - x, 1 - y)]

        def slot(t, px, py, pc):
            return out_refs[t].at[4 * px + 2 * py + pc]

        def copy(t, k, block, to, src=None):
            return pltpu.make_async_remote_copy(
                src_ref=slot(t, *block) if src is None else src, dst_ref=slot(t, *block),
                send_sem=send_sems.at[7 * t + k], recv_sem=recv_sems.at[7 * t + k],
                device_id=to, device_id_type=MESH)

        mine = [pltpu.make_async_copy(x_refs[t], slot(t, *me), local_sems.at[t]) for t in range(n_t)]
        for cp in mine:
            cp.start()
        first = []
        for t in range(n_t):
            first.append(copy(t, 0, me, sibling, src=x_refs[t]))
            first += [copy(t, 1 + j, me, (*chip, c), src=x_refs[t]) for j, chip in enumerate(chips)]
        for cp in first:
            cp.start()
        passed = []
        for t in range(n_t):
            for j, chip in enumerate(chips):
                copy(t, 1 + j, (*chip, c), me).wait_recv()
                onward = copy(t, 4 + j, (*chip, c), sibling)
                onward.start()
                passed.append(onward)
        for t in range(n_t):
            copy(t, 0, sibling, me).wait_recv()
            for j, chip in enumerate(chips):
                copy(t, 4 + j, (*chip, 1 - c), me).wait_recv()
        for cp in first + passed:
            cp.wait_send()
        for cp in mine:
            cp.wait()

    any_spec = pl.BlockSpec(memory_space=pl.ANY)
    return pl.pallas_call(
        body, name=name,
        out_shape=[jax.ShapeDtypeStruct((N_DEV,) + tuple(s.shape), s.dtype) for s in shards],
        in_specs=[any_spec] * n_t, out_specs=[any_spec] * n_t,
        scratch_shapes=[pltpu.SemaphoreType.DMA((7 * n_t,)), pltpu.SemaphoreType.DMA((7 * n_t,)),
                        pltpu.SemaphoreType.DMA((n_t,))],
    )(*shards)


def _all_to_all(chunks, name):
    n_t = len(chunks)

    def body(*refs):
        x_refs, out_refs = refs[:n_t], refs[n_t:2 * n_t]
        send_sems, recv_sems, local_sems = refs[2 * n_t:]
        mx, my, mc = lax.axis_index("x"), lax.axis_index("y"), lax.axis_index("c")
        me = 4 * mx + 2 * my + mc
        mine = [pltpu.make_async_copy(x_refs[t].at[me], out_refs[t].at[me], local_sems.at[t]) for t in range(n_t)]
        for cp in mine:
            cp.start()
        copies = []
        for k in range(1, N_DEV):
            px = 1 - mx if (k >> 2) & 1 else mx
            py = 1 - my if (k >> 1) & 1 else my
            pc = 1 - mc if k & 1 else mc
            peer = 4 * px + 2 * py + pc
            for t in range(n_t):
                copies.append(pltpu.make_async_remote_copy(
                    src_ref=x_refs[t].at[peer], dst_ref=out_refs[t].at[me],
                    send_sem=send_sems.at[7 * t + k - 1], recv_sem=recv_sems.at[7 * t + k - 1],
                    device_id=(px, py, pc), device_id_type=MESH))
        for cp in copies:
            cp.start()
        for cp in copies:
            cp.wait_send()
        for cp in copies:
            cp.wait_recv()
        for cp in mine:
            cp.wait()

    any_spec = pl.BlockSpec(memory_space=pl.ANY)
    return pl.pallas_call(
        body, name=name,
        out_shape=[jax.ShapeDtypeStruct(c.shape, c.dtype) for c in chunks],
        in_specs=[any_spec] * n_t, out_specs=[any_spec] * n_t,
        scratch_shapes=[pltpu.SemaphoreType.DMA((7 * n_t,)), pltpu.SemaphoreType.DMA((7 * n_t,)),
                        pltpu.SemaphoreType.DMA((n_t,))],
    )(*chunks)


def _matmul(a, b, *, ta=False, tb=False, out_dtypes=(F32,), epilogue=None, extras=(), name):
    if ta:
        kdim, m = a.shape
    else:
        m, kdim = a.shape
    if tb:
        n, kb = b.shape
    else:
        kb, n = b.shape
    assert kdim == kb, (a.shape, b.shape, ta, tb)
    big = (1024, 768, 512, 384, 256, 128, 64, 32, 16, 8)
    tm, tn = _tile(m, big), _tile(n, big)
    tk = kdim if kdim <= 2048 else _tile(kdim, (1024, 512, 256, 128))
    nk = kdim // tk
    n_extra, n_out = len(extras), len(out_dtypes)
    a_spec = pl.BlockSpec((tk, tm), lambda i, j, k: (k, i)) if ta else pl.BlockSpec((tm, tk), lambda i, j, k: (i, k))
    b_spec = pl.BlockSpec((tn, tk), lambda i, j, k: (j, k)) if tb else pl.BlockSpec((tk, tn), lambda i, j, k: (k, j))
    tile_spec = pl.BlockSpec((tm, tn), lambda i, j, k: (i, j))
    dims = ((0,) if ta else (1,), (1,) if tb else (0,))

    def finish(acc, extra_refs, out_refs):
        outs = (acc,) if epilogue is None else epilogue(acc, *[r[...] for r in extra_refs])
        for o_ref, o in zip(out_refs, outs):
            o_ref[...] = o.astype(o_ref.dtype)

    def body_one(a_ref, b_ref, *rest):
        acc = _dot(a_ref[...].astype(BF16), b_ref[...].astype(BF16), dims)
        finish(acc, rest[:n_extra], rest[n_extra:n_extra + n_out])

    def body_acc(a_ref, b_ref, *rest):
        acc_ref = rest[-1]
        k = pl.program_id(2)

        @pl.when(k == 0)
        def _():
            acc_ref[...] = jnp.zeros_like(acc_ref)

        acc_ref[...] += _dot(a_ref[...].astype(BF16), b_ref[...].astype(BF16), dims)

        @pl.when(k == nk - 1)
        def _():
            finish(acc_ref[...], rest[:n_extra], rest[n_extra:n_extra + n_out])

    return pl.pallas_call(
        body_one if nk == 1 else body_acc, name=name, grid=(m // tm, n // tn, nk),
        in_specs=[a_spec, b_spec] + [tile_spec] * n_extra,
        out_specs=[tile_spec] * n_out,
        out_shape=[jax.ShapeDtypeStruct((m, n), dt) for dt in out_dtypes],
        scratch_shapes=[] if nk == 1 else [pltpu.VMEM((tm, tn), F32)],
        compiler_params=pltpu.CompilerParams(
            dimension_semantics=("parallel", "parallel", "arbitrary"), vmem_limit_bytes=VMEM_MM),
    )(a, b, *extras)


def _rms_fwd(x, g, name):
    m, d = x.shape
    tm = _tile(m, (512, 256, 128, 64, 32, 16, 8))

    def body(x_ref, g_ref, y_ref):
        xv = x_ref[...]
        r = lax.rsqrt(jnp.mean(xv * xv, axis=-1, keepdims=True) + EPS)
        y_ref[...] = (xv * r * g_ref[...]).astype(y_ref.dtype)

    return pl.pallas_call(
        body, name=name, grid=(m // tm,),
        in_specs=[pl.BlockSpec((tm, d), lambda i: (i, 0)), pl.BlockSpec((1, d), lambda i: (0, 0))],
        out_specs=pl.BlockSpec((tm, d), lambda i: (i, 0)),
        out_shape=jax.ShapeDtypeStruct((m, d), BF16),
        compiler_params=pltpu.CompilerParams(dimension_semantics=("parallel",)),
    )(x, g.reshape(1, d))


def _rms_bwd(x, g, dy, name, res=None):
    m, d = x.shape
    tm = _tile(m, (512, 256, 128, 64, 32, 16, 8))
    has_res = res is not None

    def body(x_ref, g_ref, dy_ref, *rest):
        dx_ref, dg_ref = rest[-2:]
        xv = x_ref[...]
        dyv = dy_ref[...]
        r = lax.rsqrt(jnp.mean(xv * xv, axis=-1, keepdims=True) + EPS)
        xh = xv * r
        t = dyv * g_ref[...]
        dx = r * (t - xh * jnp.mean(t * xh, axis=-1, keepdims=True))
        dx_ref[...] = dx + rest[0][...] if has_res else dx

        @pl.when(pl.program_id(0) == 0)
        def _():
            dg_ref[...] = jnp.zeros_like(dg_ref)

        dg_ref[...] += jnp.sum(dyv * xh, axis=0, keepdims=True)

    row_spec = pl.BlockSpec((tm, d), lambda i: (i, 0))
    vec_spec = pl.BlockSpec((1, d), lambda i: (0, 0))
    dx, dg = pl.pallas_call(
        body, name=name, grid=(m // tm,),
        in_specs=[row_spec, vec_spec, row_spec] + ([row_spec] if has_res else []),
        out_specs=[row_spec, vec_spec],
        out_shape=[jax.ShapeDtypeStruct((m, d), F32), jax.ShapeDtypeStruct((1, d), F32)],
        compiler_params=pltpu.CompilerParams(dimension_semantics=("arbitrary",)),
    )(x, g.reshape(1, d), dy, *((res,) if has_res else ()))
    return dx, dg.reshape(d)


def _loss_head(x, g, target):
    m, d = x.shape
    tm = _tile(m, (512, 256, 128, 64, 32, 16, 8))

    def body(x_ref, g_ref, t_ref, loss_ref, dx_ref, dg_ref):
        xv = x_ref[...]
        gv = g_ref[...]
        r = lax.rsqrt(jnp.mean(xv * xv, axis=-1, keepdims=True) + EPS)
        xh = xv * r
        err = xh * gv - t_ref[...]
        row_loss = jnp.mean(err * err, axis=-1, keepdims=True)
        dyv = err * (1.0 / d)
        t = dyv * gv
        dx_ref[...] = r * (t - xh * jnp.mean(t * xh, axis=-1, keepdims=True))

        @pl.when(pl.program_id(0) == 0)
        def _():
            dg_ref[...] = jnp.zeros_like(dg_ref)
            loss_ref[...] = jnp.zeros_like(loss_ref)

        dg_ref[...] += jnp.sum(dyv * xh, axis=0, keepdims=True)
        loss_ref[...] += 0.5 * jnp.sum(row_loss, axis=0, keepdims=True)

    loss, dx, dg = pl.pallas_call(
        body, name="loss_head", grid=(m // tm,),
        in_specs=[pl.BlockSpec((tm, d), lambda i: (i, 0)), pl.BlockSpec((1, d), lambda i: (0, 0)),
                  pl.BlockSpec((tm, d), lambda i: (i, 0))],
        out_specs=[pl.BlockSpec((1, 1), lambda i: (0, 0)), pl.BlockSpec((tm, d), lambda i: (i, 0)),
                   pl.BlockSpec((1, d), lambda i: (0, 0))],
        out_shape=[jax.ShapeDtypeStruct((1, 1), F32), jax.ShapeDtypeStruct((m, d), F32),
                   jax.ShapeDtypeStruct((1, d), F32)],
        compiler_params=pltpu.CompilerParams(dimension_semantics=("arbitrary",)),
    )(x, g.reshape(1, d), target)
    return loss, dx, dg.reshape(d)


def _rope_apply(x, tabs, transpose, name):
    m, w = x.shape
    wt = tabs[0].shape[1]
    reps = w // wt
    half = ROPE // 2
    tm = _tile(m, (256, 128, 64, 32, 16, 8))

    def body(x_ref, c_ref, a_ref, b_ref, y_ref):
        xv = x_ref[...]

        def wide(t_ref):
            t = t_ref[...]
            return t if reps == 1 else jnp.concatenate([t] * reps, axis=1)

        c, a, b = wide(c_ref), wide(a_ref), wide(b_ref)
        if transpose:
            y = xv * c + pltpu.roll(xv * a, half, 1) + pltpu.roll(xv * b, w - half, 1)
        else:
            y = xv * c + pltpu.roll(xv, w - half, 1) * a + pltpu.roll(xv, half, 1) * b
        y_ref[...] = y

    x_spec = pl.BlockSpec((tm, w), lambda i: (i, 0))
    t_spec = pl.BlockSpec((tm, wt), lambda i: (i, 0))
    return pl.pallas_call(
        body, name=name, grid=(m // tm,),
        in_specs=[x_spec, t_spec, t_spec, t_spec], out_specs=x_spec,
        out_shape=jax.ShapeDtypeStruct((m, w), F32),
        compiler_params=pltpu.CompilerParams(dimension_semantics=("parallel",)),
    )(x, *tabs)


def _log_sigmoid_pair(z):
    a = jnp.minimum(z, 0.0) - jnp.log(1.0 + jnp.exp(-jnp.abs(z)))
    return a, a - z


def _split_bf16(x):
    hi = x.astype(BF16)
    return hi, (x - hi.astype(F32)).astype(BF16)


def _sb_fwd(q, k, v):
    h_, s, d = q.shape
    t = min(SB_BLOCK, s)
    nq = s // t
    scale = d ** -0.5

    def body(q_ref, k_ref, v_ref, o_ref, bt_ref, js_ref, acc_ref, r_ref):
        h, i = pl.program_id(0), pl.program_id(1)
        qv = q_ref[0]
        row = lax.broadcasted_iota(jnp.int32, (t, t), 0)
        col = lax.broadcasted_iota(jnp.int32, (t, t), 1)
        later = jnp.where(row > col, 1.0, 0.0).astype(BF16)
        acc_ref[...] = jnp.zeros_like(acc_ref)
        r_ref[...] = jnp.zeros_like(r_ref)

        def cond(carry):
            j, rmax = carry
            return jnp.logical_and(j >= 0, rmax > EXP_ZERO)

        def step(carry):
            j, _ = carry
            off = pl.multiple_of(j * t, t)
            kj = k_ref[0, pl.ds(off, t), :]
            vj = v_ref[0, pl.ds(off, t), :]
            z = _dot(qv, kj, NT) * scale
            a, b = _log_sigmoid_pair(z)
            valid = col + (j - i) * t < row
            b = jnp.where(valid, b, 0.0)
            bh, bl = _split_bf16(b)
            inner = _dot(bh, later, NN) + _dot(bl, later, NN)
            r = r_ref[...]
            w = jnp.where(valid, jnp.exp(a + inner + r), 0.0)
            acc_ref[...] += _dot(w.astype(BF16), vj, NN)
            rn = r + jnp.sum(b, axis=1, keepdims=True)
            r_ref[...] = rn
            return j - 1, jnp.max(rn)

        jend, _ = lax.while_loop(cond, step, (i, jnp.float32(0.0)))
        o_ref[0] = acc_ref[...]
        bt_ref[0] = r_ref[...]
        js_ref[h, i] = (jend + 1).astype(F32)

    return pl.pallas_call(
        body, name="sb_attn_fwd", grid=(h_, nq),
        in_specs=[pl.BlockSpec((1, t, d), lambda h, i: (h, i, 0)),
                  pl.BlockSpec((1, s, d), lambda h, i: (h, 0, 0)),
                  pl.BlockSpec((1, s, d), lambda h, i: (h, 0, 0))],
        out_specs=[pl.BlockSpec((1, t, d), lambda h, i: (h, i, 0)),
                   pl.BlockSpec((1, t, 1), lambda h, i: (h, i, 0)),
                   pl.BlockSpec(memory_space=pltpu.SMEM)],
        out_shape=[jax.ShapeDtypeStruct((h_, s, d), F32), jax.ShapeDtypeStruct((h_, s, 1), F32),
                   jax.ShapeDtypeStruct((h_, nq), F32)],
        scratch_shapes=[pltpu.VMEM((t, d), F32), pltpu.VMEM((t, 1), F32)],
        compiler_params=pltpu.CompilerParams(
            dimension_semantics=("arbitrary", "arbitrary"), vmem_limit_bytes=VMEM_ATT),
    )(q, k, v)


def _sb_bwd(q, k, v, do, btot, jstart):
    h_, s, d = q.shape
    t = min(SB_BLOCK, s)
    nq = s // t
    scale = d ** -0.5

    def body(js_ref, q_ref, k_ref, v_ref, do_ref, bt_ref, dq_ref, dk_ref, dv_ref, dq_acc, p_ref, g_ref):
        h, i = pl.program_id(0), pl.program_id(1)

        @pl.when(i == 0)
        def _():
            dk_ref[...] = jnp.zeros_like(dk_ref)
            dv_ref[...] = jnp.zeros_like(dv_ref)

        qv = q_ref[0]
        dov = do_ref[0].astype(BF16)
        bt = bt_ref[0]
        row = lax.broadcasted_iota(jnp.int32, (t, t), 0)
        col = lax.broadcasted_iota(jnp.int32, (t, t), 1)
        upto = jnp.where(row <= col, 1.0, 0.0).astype(BF16)
        before = jnp.where(row < col, 1.0, 0.0).astype(BF16)
        dq_acc[...] = jnp.zeros_like(dq_acc)
        p_ref[...] = jnp.zeros_like(p_ref)
        g_ref[...] = jnp.zeros_like(g_ref)
        j0 = jnp.clip(js_ref[h, i].astype(jnp.int32), 0, i)

        def step(j, carry):
            off = pl.multiple_of(j * t, t)
            kj = k_ref[0, pl.ds(off, t), :]
            vj = v_ref[0, pl.ds(off, t), :]
            z = _dot(qv, kj, NT) * scale
            a, b = _log_sigmoid_pair(z)
            valid = col + (j - i) * t < row
            b = jnp.where(valid, b, 0.0)
            bh, bl = _split_bf16(b)
            pin = _dot(bh, upto, NN) + _dot(bl, upto, NN)
            surv = bt - (p_ref[...] + pin)
            w = jnp.where(valid, jnp.exp(a + surv), 0.0)
            dw = _dot(dov, vj, NT)
            g = w * dw
            gh, gl = _split_bf16(g)
            gsum = g_ref[...] + _dot(gh, before, NN) + _dot(gl, before, NN)
            beta = jnp.exp(a)
            dz = jnp.where(valid, (g * (1.0 - beta) - gsum * beta) * scale, 0.0).astype(BF16)
            dq_acc[...] += _dot(dz, kj, NN)
            dk_ref[0, pl.ds(off, t), :] += _dot(dz, qv, TN)
            dv_ref[0, pl.ds(off, t), :] += _dot(w.astype(BF16), dov, TN)
            p_ref[...] += jnp.sum(b, axis=1, keepdims=True)
            g_ref[...] += jnp.sum(g, axis=1, keepdims=True)
            return carry

        lax.fori_loop(j0, i + 1, step, 0)
        dq_ref[0] = dq_acc[...]

    blk = pl.BlockSpec((1, t, d), lambda h, i: (h, i, 0))
    full = pl.BlockSpec((1, s, d), lambda h, i: (h, 0, 0))
    return pl.pallas_call(
        body, name="sb_attn_bwd", grid=(h_, nq),
        in_specs=[pl.BlockSpec(memory_space=pltpu.SMEM), blk, full, full, blk,
                  pl.BlockSpec((1, t, 1), lambda h, i: (h, i, 0))],
        out_specs=[blk, full, full],
        out_shape=[jax.ShapeDtypeStruct((h_, s, d), F32)] * 3,
        scratch_shapes=[pltpu.VMEM((t, d), F32), pltpu.VMEM((t, 1), F32), pltpu.VMEM((t, 1), F32)],
        compiler_params=pltpu.CompilerParams(
            dimension_semantics=("arbitrary", "arbitrary"), vmem_limit_bytes=VMEM_ATT),
    )(jstart, q, k, v, do, btot)


@jax.custom_vjp
def sb_attention(q, k, v):
    return _sb_fwd(q, k, v)[0]


def _sb_attention_fwd(q, k, v):
    o, btot, jstart = _sb_fwd(q, k, v)
    return o, (q, k, v, btot, jstart)


def _sb_attention_bwd(res, do):
    q, k, v, btot, jstart = res
    return tuple(d.astype(BF16) for d in _sb_bwd(q, k, v, do, btot, jstart))


sb_attention.defvjp(_sb_attention_fwd, _sb_attention_bwd)


def _chunk_allowed(row, col):
    return (col >> CHUNK_SHIFT) <= (row >> CHUNK_SHIFT)


def _mla_fwd(q, k, v_ext):
    h_, s, dqk = q.shape
    t = min(MLA_BLOCK, s)
    nq = s // t
    scale = dqk ** -0.5

    def body(q_ref, k_ref, v_ref, o_ref, lse_ref, acc_ref, m_ref):
        i = pl.program_id(1)
        qv = q_ref[0]
        acc_ref[...] = jnp.zeros_like(acc_ref)
        m_ref[...] = jnp.full_like(m_ref, NEG_BIG)

        def tile(j, diagonal):
            off = pl.multiple_of(j * t, t)
            kj = k_ref[0, pl.ds(off, t), :]
            vj = v_ref[0, pl.ds(off, t), :]
            sc = _dot(qv, kj, NT) * scale
            if diagonal:
                row = lax.broadcasted_iota(jnp.int32, (t, t), 0)
                col = lax.broadcasted_iota(jnp.int32, (t, t), 1)
                sc = jnp.where(_chunk_allowed(row, col), sc, NEG_BIG)
            m_old = m_ref[...]
            m_new = jnp.maximum(m_old, jnp.max(sc, axis=1, keepdims=True))
            p = jnp.exp(sc - m_new)
            acc_ref[...] = jnp.exp(m_old - m_new) * acc_ref[...] + _dot(p.astype(BF16), vj, NN)
            m_ref[...] = m_new

        def step(j, carry):
            tile(j, False)
            return carry

        lax.fori_loop(0, i, step, 0)
        tile(i, True)
        acc = acc_ref[...]
        den = acc[:, VDIM:VDIM + 1]
        o_ref[0] = acc[:, :VDIM] / den
        lse_ref[0] = m_ref[...] + jnp.log(den)

    return pl.pallas_call(
        body, name="mla_attn_fwd", grid=(h_, nq),
        in_specs=[pl.BlockSpec((1, t, dqk), lambda h, i: (h, i, 0)),
                  pl.BlockSpec((1, s, dqk), lambda h, i: (h, 0, 0)),
                  pl.BlockSpec((1, s, LANES), lambda h, i: (h, 0, 0))],
        out_specs=[pl.BlockSpec((1, t, VDIM), lambda h, i: (h, i, 0)),
                   pl.BlockSpec((1, t, 1), lambda h, i: (h, i, 0))],
        out_shape=[jax.ShapeDtypeStruct((h_, s, VDIM), F32), jax.ShapeDtypeStruct((h_, s, 1), F32)],
        scratch_shapes=[pltpu.VMEM((t, LANES), F32), pltpu.VMEM((t, 1), F32)],
        compiler_params=pltpu.CompilerParams(
            dimension_semantics=("arbitrary", "arbitrary"), vmem_limit_bytes=VMEM_ATT),
    )(q, k, v_ext)


def _mla_bwd(q, k, v, o, lse, do):
    h_, s, dqk = q.shape
    dv = v.shape[2]
    t = min(MLA_BLOCK, s)
    nq = s // t
    scale = dqk ** -0.5

    def body(q_ref, k_ref, v_ref, o_ref, lse_ref, do_ref, dq_ref, dk_ref, dv_ref, dq_acc):
        i = pl.program_id(1)

        @pl.when(i == 0)
        def _():
            dk_ref[...] = jnp.zeros_like(dk_ref)
            dv_ref[...] = jnp.zeros_like(dv_ref)

        qv = q_ref[0]
        dof = do_ref[0]
        dov = dof.astype(BF16)
        delta = jnp.sum(dof * o_ref[0], axis=1, keepdims=True)
        lse = lse_ref[0]
        dq_acc[...] = jnp.zeros_like(dq_acc)

        def tile(j, diagonal):
            off = pl.multiple_of(j * t, t)
            kj = k_ref[0, pl.ds(off, t), :]
            vj = v_ref[0, pl.ds(off, t), :]
            p = jnp.exp(_dot(qv, kj, NT) * scale - lse)
            if diagonal:
                row = lax.broadcasted_iota(jnp.int32, (t, t), 0)
                col = lax.broadcasted_iota(jnp.int32, (t, t), 1)
                p = jnp.where(_chunk_allowed(row, col), p, 0.0)
            dp = _dot(dov, vj, NT)
            ds = (p * (dp - delta) * scale).astype(BF16)
            dq_acc[...] += _dot(ds, kj, NN)
            dk_ref[0, pl.ds(off, t), :] += _dot(ds, qv, TN)
            dv_ref[0, pl.ds(off, t), :] += _dot(p.astype(BF16), dov, TN)

        def step(j, carry):
            tile(j, False)
            return carry

        lax.fori_loop(0, i, step, 0)
        tile(i, True)
        dq_ref[0] = dq_acc[...]

    qblk = pl.BlockSpec((1, t, dqk), lambda h, i: (h, i, 0))
    vblk = pl.BlockSpec((1, t, dv), lambda h, i: (h, i, 0))
    kfull = pl.BlockSpec((1, s, dqk), lambda h, i: (h, 0, 0))
    vfull = pl.BlockSpec((1, s, dv), lambda h, i: (h, 0, 0))
    stat = pl.BlockSpec((1, t, 1), lambda h, i: (h, i, 0))
    return pl.pallas_call(
        body, name="mla_attn_bwd", grid=(h_, nq),
        in_specs=[qblk, kfull, vfull, vblk, stat, vblk],
        out_specs=[qblk, kfull, vfull],
        out_shape=[jax.ShapeDtypeStruct((h_, s, dqk), F32), jax.ShapeDtypeStruct((h_, s, dqk), F32),
                   jax.ShapeDtypeStruct((h_, s, dv), F32)],
        scratch_shapes=[pltpu.VMEM((t, dqk), F32)],
        compiler_params=pltpu.CompilerParams(
            dimension_semantics=("arbitrary", "arbitrary"), vmem_limit_bytes=VMEM_ATT),
    )(q, k, v, o, lse, do)


def _with_ones(vb):
    h_, s, dv = vb.shape
    return jnp.concatenate([vb, jnp.ones((h_, s, 1), BF16), jnp.zeros((h_, s, LANES - dv - 1), BF16)], axis=-1)


@jax.custom_vjp
def mla_attention(q, k, v):
    return _mla_fwd(q.astype(BF16), k.astype(BF16), _with_ones(v.astype(BF16)))[0]


def _mla_attention_fwd(q, k, v):
    qb, kb, vb = q.astype(BF16), k.astype(BF16), v.astype(BF16)
    o, lse = _mla_fwd(qb, kb, _with_ones(vb))
    return o, (qb, kb, vb, o, lse)


def _mla_attention_bwd(res, do):
    qb, kb, vb, o, lse = res
    return tuple(_mla_bwd(qb, kb, vb, o, lse, do))


mla_attention.defvjp(_mla_attention_fwd, _mla_attention_bwd)


def _add_tile(acc, res):
    return (acc + res,)


def _make_linear_res(tag):
    def forward(x, a, slot, wb):
        ab = a.astype(BF16)
        y = _matmul(ab, wb, epilogue=_add_tile, extras=(x,), name=tag + "_fwd")[0]
        return y, (ab, wb)

    @jax.custom_vjp
    def f(x, a, slot, wb):
        return forward(x, a, slot, wb)[0]

    def bwd(res, dy):
        ab, wb = res
        dyb = dy.astype(BF16)
        da = _matmul(dyb, wb, tb=True, name=tag + "_da")[0]
        dw = _matmul(ab, dyb, ta=True, name=tag + "_dw")[0]
        return dy, da, dw, jnp.zeros_like(wb)

    f.defvjp(forward, bwd)
    return f


def _make_norm_linear(tag, out_dtype=F32):
    def forward(x, g, slot, wb):
        hb = _rms_fwd(x, g, tag + "_norm")
        y = _matmul(hb, wb, out_dtypes=(out_dtype,), name=tag + "_fwd")[0]
        return y, (x, g, wb, hb)

    @jax.custom_vjp
    def f(x, g, slot, wb):
        return forward(x, g, slot, wb)[0]

    def bwd(res, dy):
        x, g, wb, hb = res
        dyb = dy.astype(BF16)
        dh = _matmul(dyb, wb, tb=True, name=tag + "_dh")[0]
        dw = _matmul(hb, dyb, ta=True, name=tag + "_dw")[0]
        dx, dg = _rms_bwd(x, g, dh, tag + "_norm_bwd")
        return dx, dg, dw, jnp.zeros_like(wb)

    f.defvjp(forward, bwd)
    return f


def _relu2_fwd(acc):
    r = jnp.maximum(acc, 0.0)
    return acc, r * r


def _relu2_bwd(acc, u):
    return (acc * (2.0 * jnp.maximum(u, 0.0)),)


def _make_mlp_res(tag):
    def forward(x, g, slot1, slot2, w1b, w2b):
        hb = _rms_fwd(x, g, tag + "_norm")
        u, act = _matmul(hb, w1b, out_dtypes=(F32, BF16), epilogue=_relu2_fwd, name=tag + "_up")
        y = _matmul(act, w2b, epilogue=_add_tile, extras=(x,), name=tag + "_down")[0]
        return y, (x, g, w1b, w2b, hb, u, act)

    @jax.custom_vjp
    def f(x, g, slot1, slot2, w1b, w2b):
        return forward(x, g, slot1, slot2, w1b, w2b)[0]

    def bwd(res, dy):
        x, g, w1b, w2b, hb, u, act = res
        dyb = dy.astype(BF16)
        du = _matmul(dyb, w2b, tb=True, out_dtypes=(BF16,), epilogue=_relu2_bwd, extras=(u,), name=tag + "_du")[0]
        dw2 = _matmul(act, dyb, ta=True, name=tag + "_dw2")[0]
        dw1 = _matmul(hb, du, ta=True, name=tag + "_dw1")[0]
        dh = _matmul(du, w1b, tb=True, name=tag + "_dh")[0]
        dx, dg = _rms_bwd(x, g, dh, tag + "_norm_bwd", res=dy)
        return dx, dg, dw1, dw2, jnp.zeros_like(w1b), jnp.zeros_like(w2b)

    f.defvjp(forward, bwd)
    return f


def _make_rope(tag):
    @jax.custom_vjp
    def f(x, c, a, b):
        return _rope_apply(x, (c, a, b), False, tag + "_fwd")

    def fwd(x, c, a, b):
        return _rope_apply(x, (c, a, b), False, tag + "_fwd"), (c, a, b)

    def bwd(res, dy):
        c, a, b = res
        return _rope_apply(dy, (c, a, b), True, tag + "_bwd"), jnp.zeros_like(c), jnp.zeros_like(a), jnp.zeros_like(b)

    f.defvjp(fwd, bwd)
    return f


def _rope_tables(positions):
    half = ROPE // 2
    inv_freq = ROPE_THETA ** (-jnp.arange(0, ROPE, 2, dtype=F32) / ROPE)
    ang = positions.astype(F32)[:, None] * inv_freq
    cos, sin = jnp.cos(ang), jnp.sin(ang)
    s = positions.shape[0]
    one = lambda n: jnp.ones((s, n), F32)
    zero = lambda n: jnp.zeros((s, n), F32)
    reps = DKV_PAD // QK
    q_tabs = (jnp.tile(jnp.concatenate([one(NOPE), cos, cos], axis=1), (1, reps)),
              jnp.tile(jnp.concatenate([zero(NOPE), -sin, zero(half)], axis=1), (1, reps)),
              jnp.tile(jnp.concatenate([zero(NOPE + half), sin], axis=1), (1, reps)))
    tail = DKV_PAD - KV_RANK - ROPE
    kv_tabs = (jnp.concatenate([one(KV_RANK), cos, cos, one(tail)], axis=1),
               jnp.concatenate([zero(KV_RANK), -sin, zero(half + tail)], axis=1),
               jnp.concatenate([zero(KV_RANK + half), sin, zero(tail)], axis=1))
    return q_tabs, kv_tabs


def _trunk(x, slots, norms, wb, q_tabs, kv_tabs):
    s = x.shape[0]
    kcat = vh = None
    for layer in range(DEPTH):
        if layer < N_A:
            qkv = _make_norm_linear(f"sb{layer}_qkv", BF16)(
                x, norms['attn_norm'][layer], slots['sb_w_qkv'][layer], wb['sb_w_qkv'][layer])
            qkv = qkv.reshape(s, 3, SB_HEADS, SB_HD).transpose(1, 2, 0, 3)
            o = sb_attention(qkv[0], qkv[1], qkv[2])
            o = o.transpose(1, 0, 2).reshape(s, SB_HEADS * SB_HD)
            x = _make_linear_res(f"sb{layer}_o")(x, o, slots['sb_w_o'][layer], wb['sb_w_o'][layer])
        else:
            j = layer - N_A
            if j == 0:
                pad = ((0, 0), (0, DKV_PAD - KV_RANK - ROPE))
                down = _make_norm_linear("kv_down")(
                    x, norms['kv_norm'], jnp.pad(slots['mla_w_dkv'][0], pad), jnp.pad(wb['mla_w_dkv'][0], pad))
                kv = _make_norm_linear("kv_up")(
                    down[:, :KV_RANK], norms['mla_kv_lat_norm'], slots['mla_w_ukv'][0], wb['mla_w_ukv'][0])
                kv = kv.reshape(s, MLA_HEADS, NOPE + VDIM)
                k_rope = _make_rope("rope_k")(down, *kv_tabs)[:, KV_RANK:KV_RANK + ROPE]
                k_rope = jnp.broadcast_to(k_rope[:, None, :], (s, MLA_HEADS, ROPE))
                kcat = jnp.concatenate([kv[..., :NOPE], k_rope], axis=-1).transpose(1, 0, 2)
                vh = kv[..., NOPE:].transpose(1, 0, 2)
            c_q = _make_norm_linear(f"mla{j}_dq")(
                x, norms['attn_norm'][layer], slots['mla_w_dq'][j], wb['mla_w_dq'][j])
            q = _make_norm_linear(f"mla{j}_uq")(
                c_q, norms['mla_q_lat_norm'][j], slots['mla_w_uq'][j], wb['mla_w_uq'][j])
            q = _make_rope(f"rope_q{j}")(q, *q_tabs)
            q = q.reshape(s, MLA_HEADS, QK).transpose(1, 0, 2)
            o = mla_attention(q, kcat, vh)
            o = o.transpose(1, 0, 2).reshape(s, MLA_HEADS * VDIM)
            x = _make_linear_res(f"mla{j}_o")(x, o, slots['mla_w_o'][j], wb['mla_w_o'][j])
        x = _make_mlp_res(f"mlp{layer}")(
            x, norms['mlp_norm'][layer], slots['mlp_w1'][layer], slots['mlp_w2'][layer],
            wb['mlp_w1'][layer], wb['mlp_w2'][layer])
    return x


def _merge_blocks(gathered, ax):
    shp = gathered.shape[1:]
    return jnp.moveaxis(gathered, 0, ax).reshape(shp[:ax] + (N_DEV * shp[ax],) + shp[ax + 1:])


def _split_blocks(full, ax):
    shp = full.shape
    return jnp.moveaxis(full.reshape(shp[:ax] + (N_DEV, shp[ax] // N_DEV) + shp[ax + 1:]), ax, 0)


def _pack(parts):
    flat = jnp.concatenate([p.reshape(-1) for p in parts])
    rows = -(-flat.shape[0] // PACK_COLS)
    rows = -(-rows // PACK_ROW_ALIGN) * PACK_ROW_ALIGN
    return jnp.pad(flat, (0, rows * PACK_COLS - flat.shape[0])).reshape(rows, PACK_COLS)


def _unpack(packed, shapes):
    flat = packed.reshape(-1)
    out, off = [], 0
    for shp in shapes:
        n = math.prod(shp)
        out.append(flat[off:off + n].reshape(shp))
        off += n
    return out


def _adamw_reduce(parts, w, m, v, name):
    _, r, c = parts.shape
    tr = _tile(r, (128, 64, 32, 16, 8))
    bias1 = 1.0 - ADAM_B1 ** ADAM_STEP
    bias2 = 1.0 - ADAM_B2 ** ADAM_STEP

    def body(p_ref, w_ref, m_ref, v_ref, g_ref, d_ref, nm_ref, nv_ref):
        g = p_ref[0]
        for s in range(1, N_DEV):
            g = g + p_ref[s]
        mn = ADAM_B1 * m_ref[...] + (1.0 - ADAM_B1) * g
        vn = ADAM_B2 * v_ref[...] + (1.0 - ADAM_B2) * (g * g)
        m_hat = mn / bias1
        v_hat = vn / bias2
        g_ref[...] = g
        d_ref[...] = -ADAM_LR * (m_hat / (jnp.sqrt(v_hat) + ADAM_EPS) + ADAM_WD * w_ref[...])
        nm_ref[...] = mn
        nv_ref[...] = vn

    blk = pl.BlockSpec((tr, c), lambda i: (i, 0))
    return pl.pallas_call(
        body, name=name, grid=(r // tr,),
        in_specs=[pl.BlockSpec((N_DEV, tr, c), lambda i: (0, i, 0)), blk, blk, blk],
        out_specs=[blk] * 4,
        out_shape=[jax.ShapeDtypeStruct((r, c), F32)] * 4,
        compiler_params=pltpu.CompilerParams(dimension_semantics=("parallel",), vmem_limit_bytes=VMEM_MM),
    )(parts, w, m, v)


def kernel(x, positions, attn_norm, mlp_norm, sb_w_qkv, sb_w_o, kv_norm, mla_w_dkv, mla_kv_lat_norm, mla_w_ukv, mla_w_dq, mla_q_lat_norm, mla_w_uq, mla_w_o, mlp_w1, mlp_w2, final_norm, loss_target, m_attn_norm, m_mlp_norm, m_sb_w_qkv, m_sb_w_o, m_kv_norm, m_mla_w_dkv, m_mla_kv_lat_norm, m_mla_w_ukv, m_mla_w_dq, m_mla_q_lat_norm, m_mla_w_uq, m_mla_w_o, m_mlp_w1, m_mlp_w2, m_final_norm, v_attn_norm, v_mlp_norm, v_sb_w_qkv, v_sb_w_o, v_kv_norm, v_mla_w_dkv, v_mla_kv_lat_norm, v_mla_w_ukv, v_mla_w_dq, v_mla_q_lat_norm, v_mla_w_uq, v_mla_w_o, v_mlp_w1, v_mlp_w2, v_final_norm):
    weights = dict(attn_norm=attn_norm, mlp_norm=mlp_norm, sb_w_qkv=sb_w_qkv, sb_w_o=sb_w_o, kv_norm=kv_norm,
                   mla_w_dkv=mla_w_dkv, mla_kv_lat_norm=mla_kv_lat_norm, mla_w_ukv=mla_w_ukv, mla_w_dq=mla_w_dq,
                   mla_q_lat_norm=mla_q_lat_norm, mla_w_uq=mla_w_uq, mla_w_o=mla_w_o, mlp_w1=mlp_w1, mlp_w2=mlp_w2,
                   final_norm=final_norm)
    mom_m = dict(attn_norm=m_attn_norm, mlp_norm=m_mlp_norm, sb_w_qkv=m_sb_w_qkv, sb_w_o=m_sb_w_o, kv_norm=m_kv_norm,
                 mla_w_dkv=m_mla_w_dkv, mla_kv_lat_norm=m_mla_kv_lat_norm, mla_w_ukv=m_mla_w_ukv, mla_w_dq=m_mla_w_dq,
                 mla_q_lat_norm=m_mla_q_lat_norm, mla_w_uq=m_mla_w_uq, mla_w_o=m_mla_w_o, mlp_w1=m_mlp_w1,
                 mlp_w2=m_mlp_w2, final_norm=m_final_norm)
    mom_v = dict(attn_norm=v_attn_norm, mlp_norm=v_mlp_norm, sb_w_qkv=v_sb_w_qkv, sb_w_o=v_sb_w_o, kv_norm=v_kv_norm,
                 mla_w_dkv=v_mla_w_dkv, mla_kv_lat_norm=v_mla_kv_lat_norm, mla_w_ukv=v_mla_w_ukv, mla_w_dq=v_mla_w_dq,
                 mla_q_lat_norm=v_mla_q_lat_norm, mla_w_uq=v_mla_w_uq, mla_w_o=v_mla_w_o, mlp_w1=v_mlp_w1,
                 mlp_w2=v_mlp_w2, final_norm=v_final_norm)
    sharded_names = [n for n, _ in SHARDED]
    repl_shapes = [tuple(weights[n].shape) for n in REPLICATED]

    gathered = _all_gather([weights[n].astype(BF16) for n in sharded_names], "gather_weights")
    wb, slots = {}, {}
    for (n, ax), g in zip(SHARDED, gathered):
        full = _merge_blocks(g, ax)
        layers = [full[l] for l in range(full.shape[0])] if full.ndim == 3 else [full]
        wb[n] = layers
        slots[n] = [jnp.zeros(w.shape, F32) for w in layers]
    norms = {n: weights[n] for n in REPLICATED if n != 'final_norm'}

    q_tabs, kv_tabs = _rope_tables(positions[0])
    x_last, pullback = jax.vjp(lambda xx, ss, nn: _trunk(xx, ss, nn, wb, q_tabs, kv_tabs), x[0], slots, norms)
    loss_part, dx_last, d_final = _loss_head(x_last, final_norm, loss_target[0])
    dx, d_slots, d_norms = pullback(dx_last)
    d_norms = dict(d_norms)
    d_norms['final_norm'] = d_final
    loss = lax.psum(loss_part[0, 0], ("x", "y", "c"))

    chunks = []
    for n, ax in SHARDED:
        full = jnp.stack(d_slots[n]) if weights[n].ndim == 3 else d_slots[n][0]
        chunks.append(_split_blocks(full, ax))
    received = _all_to_all(chunks, "scatter_grads")
    repl_parts = _all_gather([_pack([d_norms[n] for n in REPLICATED])], "gather_norm_grads")[0]

    results = {kind: {} for kind in ("grad", "delta", "new_m", "new_v")}
    for n, parts in zip(sharded_names, received):
        shp = weights[n].shape
        two_d = (math.prod(shp[:-1]), shp[-1])
        res = _adamw_reduce(parts.reshape((N_DEV,) + two_d), weights[n].reshape(two_d), mom_m[n].reshape(two_d),
                            mom_v[n].reshape(two_d), "adamw_" + n)
        for kind, a in zip(results, res):
            results[kind][n] = a.reshape(shp)
    res = _adamw_reduce(repl_parts, _pack([weights[n] for n in REPLICATED]), _pack([mom_m[n] for n in REPLICATED]),
                        _pack([mom_v[n] for n in REPLICATED]), "adamw_replicated")
    for kind, a in zip(results, res):
        results[kind].update(zip(REPLICATED, _unpack(a, repl_shapes)))

    out = [loss, dx[None]]
    for kind in ("grad", "delta", "new_m", "new_v"):
        out += [results[kind][n] for n in WEIGHT_ORDER]
    return tuple(out)
```

```python
import math

import jax
import jax.numpy as jnp
from jax import lax
from jax.experimental import pallas as pl
from jax.experimental.pallas import tpu as pltpu

F32 = jnp.float32
BF16 = jnp.bfloat16
MESH = pl.DeviceIdType.MESH

N_DEV = 8
DEPTH = 4
N_A = 2
SB_HEADS = 16
SB_HD = 64
MLA_HEADS = 16
NOPE = 64
ROPE = 32
VDIM = 64
QK = NOPE + ROPE
KV_RANK = 256
DKV_PAD = 384
LANES = 128
CHUNK_SHIFT = 6
ROPE_THETA = 10000.0
EPS = 1e-6
SB_BLOCK = 256
MLA_BLOCK = 512
PACK_COLS = 1024
PACK_ROW_ALIGN = 16
EXP_ZERO = -104.0
NEG_BIG = -1e30
VMEM_ATT = 56 * 1024 * 1024
VMEM_MM = 48 * 1024 * 1024

ADAM_LR = 0.001
ADAM_B1 = 0.9
ADAM_B2 = 0.999
ADAM_EPS = 1e-08
ADAM_WD = 0.01
ADAM_STEP = 10

WEIGHT_ORDER = ['attn_norm', 'mlp_norm', 'sb_w_qkv', 'sb_w_o', 'kv_norm', 'mla_w_dkv', 'mla_kv_lat_norm',
                'mla_w_ukv', 'mla_w_dq', 'mla_q_lat_norm', 'mla_w_uq', 'mla_w_o', 'mlp_w1', 'mlp_w2', 'final_norm']
SHARDED = [('sb_w_qkv', 2), ('sb_w_o', 1), ('mla_w_dkv', 0), ('mla_w_ukv', 1), ('mla_w_dq', 1),
           ('mla_w_uq', 2), ('mla_w_o', 1), ('mlp_w1', 2), ('mlp_w2', 1)]
REPLICATED = ['attn_norm', 'mlp_norm', 'kv_norm', 'mla_kv_lat_norm', 'mla_q_lat_norm', 'final_norm']


def _tile(dim, prefs=(512, 384, 256, 128, 64, 32, 16, 8)):
    for t in prefs:
        if dim % t == 0:
            return t
    return dim


def _dot(a, b, dims):
    return lax.dot_general(a, b, (dims, ((), ())), preferred_element_type=F32)


NN = ((1,), (0,))
NT = ((1,), (1,))
TN = ((0,), (0,))


def _all_gather(shards, name):
    n_t = len(shards)

    def body(*refs):
        x_refs, out_refs = refs[:n_t], refs[n_t:2 * n_t]
        send_sems, recv_sems, local_sems = refs[2 * n_t:]
        x, y, c = lax.axis_index("x"), lax.axis_index("y"), lax.axis_index("c")
        me, sibling = (x, y, c), (x, y, 1 - c)
        chips = [(1 - x, y), (x, 1 - y), (1 - x, 1 - y)]

        def slot(t, px, py, pc):
            return out_refs[t].at[4 * px + 2 * py + pc]

        def copy(t, k, block, to, src=None):
            return pltpu.make_async_remote_copy(
                src_ref=slot(t, *block) if src is None else src, dst_ref=slot(t, *block),
                send_sem=send_sems.at[7 * t + k], recv_sem=recv_sems.at[7 * t + k],
                device_id=to, device_id_type=MESH)

        mine = [pltpu.make_async_copy(x_refs[t], slot(t, *me), local_sems.at[t]) for t in range(n_t)]
        for cp in mine:
            cp.start()
        first = []
        for t in range(n_t):
            first.append(copy(t, 0, me, sibling, src=x_refs[t]))
            first += [copy(t, 1 + j, me, (*chip, c), src=x_refs[t]) for j, chip in enumerate(chips)]
        for cp in first:
            cp.start()
        passed = []
        for t in range(n_t):
            for j, chip in enumerate(chips):
                copy(t, 1 + j, (*chip, c), me).wait_recv()
                onward = copy(t, 4 + j, (*chip, c), sibling)
                onward.start()
                passed.append(onward)
        for t in range(n_t):
            copy(t, 0, sibling, me).wait_recv()
            for j, chip in enumerate(chips):
                copy(t, 4 + j, (*chip, 1 - c), me).wait_recv()
        for cp in first + passed:
            cp.wait_send()
        for cp in mine:
            cp.wait()

    any_spec = pl.BlockSpec(memory_space=pl.ANY)
    return pl.pallas_call(
        body, name=name,
        out_shape=[jax.ShapeDtypeStruct((N_DEV,) + tuple(s.shape), s.dtype) for s in shards],
        in_specs=[any_spec] * n_t, out_specs=[any_spec] * n_t,
        scratch_shapes=[pltpu.SemaphoreType.DMA((7 * n_t,)), pltpu.SemaphoreType.DMA((7 * n_t,)),
                        pltpu.SemaphoreType.DMA((n_t,))],
    )(*shards)


def _pair_exchange(xs, name):
    n_t = len(xs)

    def body(*refs):
        x_refs, out_refs = refs[:n_t], refs[n_t:2 * n_t]
        send_sems, recv_sems = refs[2 * n_t:]
        x, y, c = lax.axis_index("x"), lax.axis_index("y"), lax.axis_index("c")
        copies = [pltpu.make_async_remote_copy(
            src_ref=x_refs[t].at[1 - c], dst_ref=out_refs[t], send_sem=send_sems.at[t], recv_sem=recv_sems.at[t],
            device_id=(x, y, 1 - c), device_id_type=MESH) for t in range(n_t)]
        for cp in copies:
            cp.start()
        for cp in copies:
            cp.wait()

    any_spec = pl.BlockSpec(memory_space=pl.ANY)
    return pl.pallas_call(
        body, name=name,
        out_shape=[jax.ShapeDtypeStruct(x.shape[1:], x.dtype) for x in xs],
        in_specs=[any_spec] * n_t, out_specs=[any_spec] * n_t,
        scratch_shapes=[pltpu.SemaphoreType.DMA((n_t,)), pltpu.SemaphoreType.DMA((n_t,))],
    )(*xs)


def _chip_exchange(ps, name):
    n_t = len(ps)

    def body(*refs):
        p_refs, out_refs = refs[:n_t], refs[n_t:2 * n_t]
        send_sems, recv_sems, local_sems = refs[2 * n_t:]
        mx, my, mc = lax.axis_index("x"), lax.axis_index("y"), lax.axis_index("c")
        me = 2 * mx + my
        mine = [pltpu.make_async_copy(p_refs[t].at[me], out_refs[t].at[me], local_sems.at[t]) for t in range(n_t)]
        for cp in mine:
            cp.start()
        copies = []
        for k in range(1, 4):
            px = 1 - mx if (k >> 1) & 1 else mx
            py = 1 - my if k & 1 else my
            peer = 2 * px + py
            for t in range(n_t):
                copies.append(pltpu.make_async_remote_copy(
                    src_ref=p_refs[t].at[peer], dst_ref=out_refs[t].at[me],
                    send_sem=send_sems.at[3 * t + k - 1], recv_sem=recv_sems.at[3 * t + k - 1],
                    device_id=(px, py, mc), device_id_type=MESH))
        for cp in copies:
            cp.start()
        for cp in copies:
            cp.wait_send()
        for cp in copies:
            cp.wait_recv()
        for cp in mine:
            cp.wait()

    any_spec = pl.BlockSpec(memory_space=pl.ANY)
    return pl.pallas_call(
        body, name=name,
        out_shape=[jax.ShapeDtypeStruct(p.shape, p.dtype) for p in ps],
        in_specs=[any_spec] * n_t, out_specs=[any_spec] * n_t,
        scratch_shapes=[pltpu.SemaphoreType.DMA((3 * n_t,)), pltpu.SemaphoreType.DMA((3 * n_t,)),
                        pltpu.SemaphoreType.DMA((n_t,))],
    )(*ps)


def _matmul(a, b, *, ta=False, tb=False, out_dtypes=(F32,), epilogue=None, extras=(), name):
    if ta:
        kdim, m = a.shape
    else:
        m, kdim = a.shape
    if tb:
        n, kb = b.shape
    else:
        kb, n = b.shape
    assert kdim == kb, (a.shape, b.shape, ta, tb)
    big = (1024, 768, 512, 384, 256, 128, 64, 32, 16, 8)
    tm, tn = _tile(m, big), _tile(n, big)
    tk = kdim if kdim <= 2048 else _tile(kdim, (1024, 512, 256, 128))
    nk = kdim // tk
    n_extra, n_out = len(extras), len(out_dtypes)
    a_spec = pl.BlockSpec((tk, tm), lambda i, j, k: (k, i)) if ta else pl.BlockSpec((tm, tk), lambda i, j, k: (i, k))
    b_spec = pl.BlockSpec((tn, tk), lambda i, j, k: (j, k)) if tb else pl.BlockSpec((tk, tn), lambda i, j, k: (k, j))
    tile_spec = pl.BlockSpec((tm, tn), lambda i, j, k: (i, j))
    dims = ((0,) if ta else (1,), (1,) if tb else (0,))

    def finish(acc, extra_refs, out_refs):
        outs = (acc,) if epilogue is None else epilogue(acc, *[r[...] for r in extra_refs])
        for o_ref, o in zip(out_refs, outs):
            o_ref[...] = o.astype(o_ref.dtype)

    def body_one(a_ref, b_ref, *rest):
        acc = _dot(a_ref[...].astype(BF16), b_ref[...].astype(BF16), dims)
        finish(acc, rest[:n_extra], rest[n_extra:n_extra + n_out])

    def body_acc(a_ref, b_ref, *rest):
        acc_ref = rest[-1]
        k = pl.program_id(2)

        @pl.when(k == 0)
        def _():
            acc_ref[...] = jnp.zeros_like(acc_ref)

        acc_ref[...] += _dot(a_ref[...].astype(BF16), b_ref[...].astype(BF16), dims)

        @pl.when(k == nk - 1)
        def _():
            finish(acc_ref[...], rest[:n_extra], rest[n_extra:n_extra + n_out])

    return pl.pallas_call(
        body_one if nk == 1 else body_acc, name=name, grid=(m // tm, n // tn, nk),
        in_specs=[a_spec, b_spec] + [tile_spec] * n_extra,
        out_specs=[tile_spec] * n_out,
        out_shape=[jax.ShapeDtypeStruct((m, n), dt) for dt in out_dtypes],
        scratch_shapes=[] if nk == 1 else [pltpu.VMEM((tm, tn), F32)],
        compiler_params=pltpu.CompilerParams(
            dimension_semantics=("parallel", "parallel", "arbitrary"), vmem_limit_bytes=VMEM_MM),
    )(a, b, *extras)


def _rms_fwd(x, g, name):
    m, d = x.shape
    tm = _tile(m, (512, 256, 128, 64, 32, 16, 8))

    def body(x_ref, g_ref, y_ref):
        xv = x_ref[...]
        r = lax.rsqrt(jnp.mean(xv * xv, axis=-1, keepdims=True) + EPS)
        y_ref[...] = (xv * r * g_ref[...]).astype(y_ref.dtype)

    return pl.pallas_call(
        body, name=name, grid=(m // tm,),
        in_specs=[pl.BlockSpec((tm, d), lambda i: (i, 0)), pl.BlockSpec((1, d), lambda i: (0, 0))],
        out_specs=pl.BlockSpec((tm, d), lambda i: (i, 0)),
        out_shape=jax.ShapeDtypeStruct((m, d), BF16),
        compiler_params=pltpu.CompilerParams(dimension_semantics=("parallel",)),
    )(x, g.reshape(1, d))


def _rms_bwd(x, g, dy, name, res=None):
    m, d = x.shape
    tm = _tile(m, (512, 256, 128, 64, 32, 16, 8))
    has_res = res is not None

    def body(x_ref, g_ref, dy_ref, *rest):
        dx_ref, dg_ref = rest[-2:]
        xv = x_ref[...]
        dyv = dy_ref[...]
        r = lax.rsqrt(jnp.mean(xv * xv, axis=-1, keepdims=True) + EPS)
        xh = xv * r
        t = dyv * g_ref[...]
        dx = r * (t - xh * jnp.mean(t * xh, axis=-1, keepdims=True))
        dx_ref[...] = dx + rest[0][...] if has_res else dx

        @pl.when(pl.program_id(0) == 0)
        def _():
            dg_ref[...] = jnp.zeros_like(dg_ref)

        dg_ref[...] += jnp.sum(dyv * xh, axis=0, keepdims=True)

    row_spec = pl.BlockSpec((tm, d), lambda i: (i, 0))
    vec_spec = pl.BlockSpec((1, d), lambda i: (0, 0))
    dx, dg = pl.pallas_call(
        body, name=name, grid=(m // tm,),
        in_specs=[row_spec, vec_spec, row_spec] + ([row_spec] if has_res else []),
        out_specs=[row_spec, vec_spec],
        out_shape=[jax.ShapeDtypeStruct((m, d), F32), jax.ShapeDtypeStruct((1, d), F32)],
        compiler_params=pltpu.CompilerParams(dimension_semantics=("arbitrary",)),
    )(x, g.reshape(1, d), dy, *((res,) if has_res else ()))
    return dx, dg.reshape(d)


def _loss_head(x, g, target):
    m, d = x.shape
    tm = _tile(m, (512, 256, 128, 64, 32, 16, 8))

    def body(x_ref, g_ref, t_ref, loss_ref, dx_ref, dg_ref):
        xv = x_ref[...]
        gv = g_ref[...]
        r = lax.rsqrt(jnp.mean(xv * xv, axis=-1, keepdims=True) + EPS)
        xh = xv * r
        err = xh * gv - t_ref[...]
        row_loss = jnp.mean(err * err, axis=-1, keepdims=True)
        dyv = err * (1.0 / d)
        t = dyv * gv
        dx_ref[...] = r * (t - xh * jnp.mean(t * xh, axis=-1, keepdims=True))

        @pl.when(pl.program_id(0) == 0)
        def _():
            dg_ref[...] = jnp.zeros_like(dg_ref)
            loss_ref[...] = jnp.zeros_like(loss_ref)

        dg_ref[...] += jnp.sum(dyv * xh, axis=0, keepdims=True)
        loss_ref[...] += 0.5 * jnp.sum(row_loss, axis=0, keepdims=True)

    loss, dx, dg = pl.pallas_call(
        body, name="loss_head", grid=(m // tm,),
        in_specs=[pl.BlockSpec((tm, d), lambda i: (i, 0)), pl.BlockSpec((1, d), lambda i: (0, 0)),
                  pl.BlockSpec((tm, d), lambda i: (i, 0))],
        out_specs=[pl.BlockSpec((1, 1), lambda i: (0, 0)), pl.BlockSpec((tm, d), lambda i: (i, 0)),
                   pl.BlockSpec((1, d), lambda i: (0, 0))],
        out_shape=[jax.ShapeDtypeStruct((1, 1), F32), jax.ShapeDtypeStruct((m, d), F32),
                   jax.ShapeDtypeStruct((1, d), F32)],
        compiler_params=pltpu.CompilerParams(dimension_semantics=("arbitrary",)),
    )(x, g.reshape(1, d), target)
    return loss, dx, dg.reshape(d)


def _rope_apply(x, tabs, transpose, name):
    m, w = x.shape
    wt = tabs[0].shape[1]
    reps = w // wt
    half = ROPE // 2
    tm = _tile(m, (256, 128, 64, 32, 16, 8))

    def body(x_ref, c_ref, a_ref, b_ref, y_ref):
        xv = x_ref[...]

        def wide(t_ref):
            t = t_ref[...]
            return t if reps == 1 else jnp.concatenate([t] * reps, axis=1)

        c, a, b = wide(c_ref), wide(a_ref), wide(b_ref)
        if transpose:
            y = xv * c + pltpu.roll(xv * a, half, 1) + pltpu.roll(xv * b, w - half, 1)
        else:
            y = xv * c + pltpu.roll(xv, w - half, 1) * a + pltpu.roll(xv, half, 1) * b
        y_ref[...] = y

    x_spec = pl.BlockSpec((tm, w), lambda i: (i, 0))
    t_spec = pl.BlockSpec((tm, wt), lambda i: (i, 0))
    return pl.pallas_call(
        body, name=name, grid=(m // tm,),
        in_specs=[x_spec, t_spec, t_spec, t_spec], out_specs=x_spec,
        out_shape=jax.ShapeDtypeStruct((m, w), F32),
        compiler_params=pltpu.CompilerParams(dimension_semantics=("parallel",)),
    )(x, *tabs)


def _log_sigmoid_pair(z):
    a = jnp.minimum(z, 0.0) - jnp.log(1.0 + jnp.exp(-jnp.abs(z)))
    return a, a - z


def _split_bf16(x):
    hi = x.astype(BF16)
    return hi, (x - hi.astype(F32)).astype(BF16)


def _sb_weights(qh, k2, valid, scale):
    z = _dot(qh, k2, NT) * scale
    a, b = _log_sigmoid_pair(z)
    return a, jnp.where(valid, b, 0.0)


def _sb_fwd(qkv):
    s = qkv.shape[0]
    t = min(SB_BLOCK, s)
    nq = s // t
    npair = SB_HEADS // 2
    scale = SB_HD ** -0.5

    def body(q_ref, k_ref, v_ref, o_ref, bta_ref, btb_ref, js_ref, acc_ref, ra_ref, rb_ref):
        p, i = pl.program_id(0), pl.program_id(1)
        q2 = q_ref[...]
        first = lax.broadcasted_iota(jnp.int32, (t, LANES), 1) < SB_HD
        heads = (jnp.where(first, q2, jnp.zeros_like(q2)), jnp.where(first, jnp.zeros_like(q2), q2))
        row = lax.broadcasted_iota(jnp.int32, (t, t), 0)
        col = lax.broadcasted_iota(jnp.int32, (t, t), 1)
        later = jnp.where(row > col, 1.0, 0.0).astype(BF16)
        acc_ref[...] = jnp.zeros_like(acc_ref)
        ra_ref[...] = jnp.zeros_like(ra_ref)
        rb_ref[...] = jnp.zeros_like(rb_ref)

        def cond(carry):
            j, rmax = carry
            return jnp.logical_and(j >= 0, rmax > EXP_ZERO)

        def step(carry):
            j, _ = carry
            off = pl.multiple_of(j * t, t)
            k2 = k_ref[pl.ds(off, t), :]
            v2 = v_ref[pl.ds(off, t), :]
            valid = col + (j - i) * t < row
            outs, rmax = [], None
            for qh, r_ref in zip(heads, (ra_ref, rb_ref)):
                a, b = _sb_weights(qh, k2, valid, scale)
                bh, bl = _split_bf16(b)
                inner = _dot(bh, later, NN) + _dot(bl, later, NN)
                r = r_ref[...]
                w = jnp.where(valid, jnp.exp(a + inner + r), 0.0)
                outs.append(_dot(w.astype(BF16), v2, NN))
                rn = r + jnp.sum(b, axis=1, keepdims=True)
                r_ref[...] = rn
                rmax = jnp.max(rn) if rmax is None else jnp.maximum(rmax, jnp.max(rn))
            acc_ref[...] += jnp.where(first, outs[0], outs[1])
            return j - 1, rmax

        jend, _ = lax.while_loop(cond, step, (i, jnp.float32(0.0)))
        o_ref[...] = acc_ref[...].astype(o_ref.dtype)
        bta_ref[0] = ra_ref[...]
        btb_ref[0] = rb_ref[...]
        js_ref[p, i] = (jend + 1).astype(F32)

    stat = pl.BlockSpec((1, t, 1), lambda p, i: (p, i, 0))
    return pl.pallas_call(
        body, name="sb_attn_fwd", grid=(npair, nq),
        in_specs=[pl.BlockSpec((t, LANES), lambda p, i: (i, p)),
                  pl.BlockSpec((s, LANES), lambda p, i: (0, npair + p)),
                  pl.BlockSpec((s, LANES), lambda p, i: (0, 2 * npair + p))],
        out_specs=[pl.BlockSpec((t, LANES), lambda p, i: (i, p)), stat, stat,
                   pl.BlockSpec(memory_space=pltpu.SMEM)],
        out_shape=[jax.ShapeDtypeStruct((s, SB_HEADS * SB_HD), BF16), jax.ShapeDtypeStruct((npair, s, 1), F32),
                   jax.ShapeDtypeStruct((npair, s, 1), F32), jax.ShapeDtypeStruct((npair, nq), F32)],
        scratch_shapes=[pltpu.VMEM((t, LANES), F32), pltpu.VMEM((t, 1), F32), pltpu.VMEM((t, 1), F32)],
        compiler_params=pltpu.CompilerParams(
            dimension_semantics=("arbitrary", "arbitrary"), vmem_limit_bytes=VMEM_ATT),
    )(qkv, qkv, qkv)


def _sb_bwd(qkv, do, btot_a, btot_b, jstart):
    s = qkv.shape[0]
    t = min(SB_BLOCK, s)
    nq = s // t
    npair = SB_HEADS // 2
    scale = SB_HD ** -0.5

    def body(js_ref, q_ref, k_ref, v_ref, do_ref, bta_ref, btb_ref, dq_ref, dk_ref, dv_ref,
             dq_acc, pa_ref, pb_ref, ga_ref, gb_ref):
        p, i = pl.program_id(0), pl.program_id(1)

        @pl.when(i == 0)
        def _():
            dk_ref[...] = jnp.zeros_like(dk_ref)
            dv_ref[...] = jnp.zeros_like(dv_ref)

        q2 = q_ref[...]
        do2 = do_ref[...]
        first = lax.broadcasted_iota(jnp.int32, (t, LANES), 1) < SB_HD
        zero = jnp.zeros_like(q2)
        q_heads = (jnp.where(first, q2, zero), jnp.where(first, zero, q2))
        do_heads = (jnp.where(first, do2, zero), jnp.where(first, zero, do2))
        bts = (bta_ref[0], btb_ref[0])
        row = lax.broadcasted_iota(jnp.int32, (t, t), 0)
        col = lax.broadcasted_iota(jnp.int32, (t, t), 1)
        upto = jnp.where(row <= col, 1.0, 0.0).astype(BF16)
        before = jnp.where(row < col, 1.0, 0.0).astype(BF16)
        dq_acc[...] = jnp.zeros_like(dq_acc)
        for r in (pa_ref, pb_ref, ga_ref, gb_ref):
            r[...] = jnp.zeros_like(r)
        j0 = jnp.clip(js_ref[p, i].astype(jnp.int32), 0, i)

        def step(j, carry):
            off = pl.multiple_of(j * t, t)
            k2 = k_ref[pl.ds(off, t), :]
            v2 = v_ref[pl.ds(off, t), :]
            valid = col + (j - i) * t < row
            dqs, dk2, dv2 = [], None, None
            for qh, doh, bt, p_ref, g_ref in zip(q_heads, do_heads, bts, (pa_ref, pb_ref), (ga_ref, gb_ref)):
                a, b = _sb_weights(qh, k2, valid, scale)
                bh, bl = _split_bf16(b)
                pin = _dot(bh, upto, NN) + _dot(bl, upto, NN)
                surv = bt - (p_ref[...] + pin)
                w = jnp.where(valid, jnp.exp(a + surv), 0.0)
                g = w * _dot(doh, v2, NT)
                gh, gl = _split_bf16(g)
                gsum = g_ref[...] + _dot(gh, before, NN) + _dot(gl, before, NN)
                beta = jnp.exp(a)
                dz = jnp.where(valid, (g * (1.0 - beta) - gsum * beta) * scale, 0.0).astype(BF16)
                dqs.append(_dot(dz, k2, NN))
                dkh = _dot(dz, qh, TN)
                dvh = _dot(w.astype(BF16), doh, TN)
                dk2 = dkh if dk2 is None else dk2 + dkh
                dv2 = dvh if dv2 is None else dv2 + dvh
                p_ref[...] += jnp.sum(b, axis=1, keepdims=True)
                g_ref[...] += jnp.sum(g, axis=1, keepdims=True)
            dq_acc[...] += jnp.where(first, dqs[0], dqs[1])
            dk_ref[pl.ds(off, t), :] += dk2
            dv_ref[pl.ds(off, t), :] += dv2
            return carry

        lax.fori_loop(j0, i + 1, step, 0)
        dq_ref[...] = dq_acc[...]

    blk = pl.BlockSpec((t, LANES), lambda p, i: (i, p))
    full = pl.BlockSpec((s, LANES), lambda p, i: (0, p))
    stat = pl.BlockSpec((1, t, 1), lambda p, i: (p, i, 0))
    vec = pltpu.VMEM((t, 1), F32)
    return pl.pallas_call(
        body, name="sb_attn_bwd", grid=(npair, nq),
        in_specs=[pl.BlockSpec(memory_space=pltpu.SMEM), blk,
                  pl.BlockSpec((s, LANES), lambda p, i: (0, npair + p)),
                  pl.BlockSpec((s, LANES), lambda p, i: (0, 2 * npair + p)), blk, stat, stat],
        out_specs=[blk, full, full],
        out_shape=[jax.ShapeDtypeStruct((s, SB_HEADS * SB_HD), F32)] * 3,
        scratch_shapes=[pltpu.VMEM((t, LANES), F32), vec, vec, vec, vec],
        compiler_params=pltpu.CompilerParams(
            dimension_semantics=("arbitrary", "arbitrary"), vmem_limit_bytes=VMEM_ATT),
    )(jstart, qkv, qkv, qkv, do, btot_a, btot_b)


@jax.custom_vjp
def sb_attention(qkv):
    return _sb_fwd(qkv)[0]


def _sb_attention_fwd(qkv):
    o, btot_a, btot_b, jstart = _sb_fwd(qkv)
    return o, (qkv, btot_a, btot_b, jstart)


def _sb_attention_bwd(res, do):
    qkv, btot_a, btot_b, jstart = res
    dq, dk, dv = _sb_bwd(qkv, do, btot_a, btot_b, jstart)
    return (jnp.concatenate([dq, dk, dv], axis=1).astype(BF16),)


sb_attention.defvjp(_sb_attention_fwd, _sb_attention_bwd)


def _chunk_allowed(row, col):
    return (col >> CHUNK_SHIFT) <= (row >> CHUNK_SHIFT)


def _mla_fwd(q, k, v_ext):
    h_, s, dqk = q.shape
    t = min(MLA_BLOCK, s)
    nq = s // t
    scale = dqk ** -0.5
    hp = 2

    def body(q_ref, k_ref, v_ref, o_ref, lse_ref, acc_ref, m_ref):
        i = pl.program_id(1)
        acc_ref[...] = jnp.zeros_like(acc_ref)
        m_ref[...] = jnp.full_like(m_ref, NEG_BIG)

        def tile(j, diagonal):
            off = pl.multiple_of(j * t, t)
            for hh in range(hp):
                kj = k_ref[hh, pl.ds(off, t), :]
                vj = v_ref[hh, pl.ds(off, t), :]
                sc = _dot(q_ref[hh], kj, NT) * scale
                if diagonal:
                    row = lax.broadcasted_iota(jnp.int32, (t, t), 0)
                    col = lax.broadcasted_iota(jnp.int32, (t, t), 1)
                    sc = jnp.where(_chunk_allowed(row, col), sc, NEG_BIG)
                m_old = m_ref[hh]
                m_new = jnp.maximum(m_old, jnp.max(sc, axis=1, keepdims=True))
                p = jnp.exp(sc - m_new)
                acc_ref[hh] = jnp.exp(m_old - m_new) * acc_ref[hh] + _dot(p.astype(BF16), vj, NN)
                m_ref[hh] = m_new

        def step(j, carry):
            tile(j, False)
            return carry

        lax.fori_loop(0, i, step, 0)
        tile(i, True)
        for hh in range(hp):
            acc = acc_ref[hh]
            den = acc[:, VDIM:VDIM + 1]
            o_ref[hh] = acc[:, :VDIM] / den
            lse_ref[hh] = m_ref[hh] + jnp.log(den)

    return pl.pallas_call(
        body, name="mla_attn_fwd", grid=(h_ // hp, nq),
        in_specs=[pl.BlockSpec((hp, t, dqk), lambda h, i: (h, i, 0)),
                  pl.BlockSpec((hp, s, dqk), lambda h, i: (h, 0, 0)),
                  pl.BlockSpec((hp, s, LANES), lambda h, i: (h, 0, 0))],
        out_specs=[pl.BlockSpec((hp, t, VDIM), lambda h, i: (h, i, 0)),
                   pl.BlockSpec((hp, t, 1), lambda h, i: (h, i, 0))],
        out_shape=[jax.ShapeDtypeStruct((h_, s, VDIM), F32), jax.ShapeDtypeStruct((h_, s, 1), F32)],
        scratch_shapes=[pltpu.VMEM((hp, t, LANES), F32), pltpu.VMEM((hp, t, 1), F32)],
        compiler_params=pltpu.CompilerParams(
            dimension_semantics=("arbitrary", "arbitrary"), vmem_limit_bytes=VMEM_ATT),
    )(q, k, v_ext)


def _mla_bwd(q, k, v, o, lse, do):
    h_, s, dqk = q.shape
    dv = v.shape[2]
    t = min(MLA_BLOCK, s)
    nq = s // t
    scale = dqk ** -0.5

    def body(q_ref, k_ref, v_ref, o_ref, lse_ref, do_ref, dq_ref, dk_ref, dv_ref, dq_acc):
        i = pl.program_id(1)

        @pl.when(i == 0)
        def _():
            dk_ref[...] = jnp.zeros_like(dk_ref)
            dv_ref[...] = jnp.zeros_like(dv_ref)

        qv = q_ref[0]
        dof = do_ref[0]
        dov = dof.astype(BF16)
        delta = jnp.sum(dof * o_ref[0], axis=1, keepdims=True)
        lse = lse_ref[0]
        dq_acc[...] = jnp.zeros_like(dq_acc)

        def tile(j, diagonal):
            off = pl.multiple_of(j * t, t)
            kj = k_ref[0, pl.ds(off, t), :]
            vj = v_ref[0, pl.ds(off, t), :]
            p = jnp.exp(_dot(qv, kj, NT) * scale - lse)
            if diagonal:
                row = lax.broadcasted_iota(jnp.int32, (t, t), 0)
                col = lax.broadcasted_iota(jnp.int32, (t, t), 1)
                p = jnp.where(_chunk_allowed(row, col), p, 0.0)
            dp = _dot(dov, vj, NT)
            ds = (p * (dp - delta) * scale).astype(BF16)
            dq_acc[...] += _dot(ds, kj, NN)
            dk_ref[0, pl.ds(off, t), :] += _dot(ds, qv, TN)
            dv_ref[0, pl.ds(off, t), :] += _dot(p.astype(BF16), dov, TN)

        def step(j, carry):
            tile(j, False)
            return carry

        lax.fori_loop(0, i, step, 0)
        tile(i, True)
        dq_ref[0] = dq_acc[...]

    qblk = pl.BlockSpec((1, t, dqk), lambda h, i: (h, i, 0))
    vblk = pl.BlockSpec((1, t, dv), lambda h, i: (h, i, 0))
    kfull = pl.BlockSpec((1, s, dqk), lambda h, i: (h, 0, 0))
    vfull = pl.BlockSpec((1, s, dv), lambda h, i: (h, 0, 0))
    stat = pl.BlockSpec((1, t, 1), lambda h, i: (h, i, 0))
    return pl.pallas_call(
        body, name="mla_attn_bwd", grid=(h_, nq),
        in_specs=[qblk, kfull, vfull, vblk, stat, vblk],
        out_specs=[qblk, kfull, vfull],
        out_shape=[jax.ShapeDtypeStruct((h_, s, dqk), F32), jax.ShapeDtypeStruct((h_, s, dqk), F32),
                   jax.ShapeDtypeStruct((h_, s, dv), F32)],
        scratch_shapes=[pltpu.VMEM((t, dqk), F32)],
        compiler_params=pltpu.CompilerParams(
            dimension_semantics=("arbitrary", "arbitrary"), vmem_limit_bytes=VMEM_ATT),
    )(q, k, v, o, lse, do)


def _with_ones(vb):
    h_, s, dv = vb.shape
    return jnp.concatenate([vb, jnp.ones((h_, s, 1), BF16), jnp.zeros((h_, s, LANES - dv - 1), BF16)], axis=-1)


@jax.custom_vjp
def mla_attention(q, k, v):
    return _mla_fwd(q.astype(BF16), k.astype(BF16), _with_ones(v.astype(BF16)))[0]


def _mla_attention_fwd(q, k, v):
    qb, kb, vb = q.astype(BF16), k.astype(BF16), v.astype(BF16)
    o, lse = _mla_fwd(qb, kb, _with_ones(vb))
    return o, (qb, kb, vb, o, lse)


def _mla_attention_bwd(res, do):
    qb, kb, vb, o, lse = res
    return tuple(_mla_bwd(qb, kb, vb, o, lse, do))


mla_attention.defvjp(_mla_attention_fwd, _mla_attention_bwd)


def _add_tile(acc, res):
    return (acc + res,)


def _make_linear_res(tag, a_dtype=F32):
    def forward(x, a, slot, wb):
        ab = a.astype(BF16)
        y = _matmul(ab, wb, epilogue=_add_tile, extras=(x,), name=tag + "_fwd")[0]
        return y, (ab, wb)

    @jax.custom_vjp
    def f(x, a, slot, wb):
        return forward(x, a, slot, wb)[0]

    def bwd(res, dy):
        ab, wb = res
        dyb = dy.astype(BF16)
        da = _matmul(dyb, wb, tb=True, out_dtypes=(a_dtype,), name=tag + "_da")[0]
        dw = _matmul(ab, dyb, ta=True, name=tag + "_dw")[0]
        return dy, da, dw, jnp.zeros_like(wb)

    f.defvjp(forward, bwd)
    return f


def _make_norm_linear(tag, out_dtype=F32):
    def forward(x, g, slot, wb):
        hb = _rms_fwd(x, g, tag + "_norm")
        y = _matmul(hb, wb, out_dtypes=(out_dtype,), name=tag + "_fwd")[0]
        return y, (x, g, wb, hb)

    @jax.custom_vjp
    def f(x, g, slot, wb):
        return forward(x, g, slot, wb)[0]

    def bwd(res, dy):
        x, g, wb, hb = res
        dyb = dy.astype(BF16)
        dh = _matmul(dyb, wb, tb=True, name=tag + "_dh")[0]
        dw = _matmul(hb, dyb, ta=True, name=tag + "_dw")[0]
        dx, dg = _rms_bwd(x, g, dh, tag + "_norm_bwd")
        return dx, dg, dw, jnp.zeros_like(wb)

    f.defvjp(forward, bwd)
    return f


def _relu2_fwd(acc):
    r = jnp.maximum(acc, 0.0)
    return acc, r * r


def _relu2_bwd(acc, u):
    return (acc * (2.0 * jnp.maximum(u, 0.0)),)


def _make_mlp_res(tag):
    def forward(x, g, slot1, slot2, w1b, w2b):
        hb = _rms_fwd(x, g, tag + "_norm")
        u, act = _matmul(hb, w1b, out_dtypes=(F32, BF16), epilogue=_relu2_fwd, name=tag + "_up")
        y = _matmul(act, w2b, epilogue=_add_tile, extras=(x,), name=tag + "_down")[0]
        return y, (x, g, w1b, w2b, hb, u, act)

    @jax.custom_vjp
    def f(x, g, slot1, slot2, w1b, w2b):
        return forward(x, g, slot1, slot2, w1b, w2b)[0]

    def bwd(res, dy):
        x, g, w1b, w2b, hb, u, act = res
        dyb = dy.astype(BF16)
        du = _matmul(dyb, w2b, tb=True, out_dtypes=(BF16,), epilogue=_relu2_bwd, extras=(u,), name=tag + "_du")[0]
        dw2 = _matmul(act, dyb, ta=True, name=tag + "_dw2")[0]
        dw1 = _matmul(hb, du, ta=True, name=tag + "_dw1")[0]
        dh = _matmul(du, w1b, tb=True, name=tag + "_dh")[0]
        dx, dg = _rms_bwd(x, g, dh, tag + "_norm_bwd", res=dy)
        return dx, dg, dw1, dw2, jnp.zeros_like(w1b), jnp.zeros_like(w2b)

    f.defvjp(forward, bwd)
    return f


def _make_rope(tag):
    @jax.custom_vjp
    def f(x, c, a, b):
        return _rope_apply(x, (c, a, b), False, tag + "_fwd")

    def fwd(x, c, a, b):
        return _rope_apply(x, (c, a, b), False, tag + "_fwd"), (c, a, b)

    def bwd(res, dy):
        c, a, b = res
        return _rope_apply(dy, (c, a, b), True, tag + "_bwd"), jnp.zeros_like(c), jnp.zeros_like(a), jnp.zeros_like(b)

    f.defvjp(fwd, bwd)
    return f


def _rope_tables(positions):
    half = ROPE // 2
    inv_freq = ROPE_THETA ** (-jnp.arange(0, ROPE, 2, dtype=F32) / ROPE)
    ang = positions.astype(F32)[:, None] * inv_freq
    cos, sin = jnp.cos(ang), jnp.sin(ang)
    s = positions.shape[0]
    one = lambda n: jnp.ones((s, n), F32)
    zero = lambda n: jnp.zeros((s, n), F32)
    reps = DKV_PAD // QK
    q_tabs = (jnp.tile(jnp.concatenate([one(NOPE), cos, cos], axis=1), (1, reps)),
              jnp.tile(jnp.concatenate([zero(NOPE), -sin, zero(half)], axis=1), (1, reps)),
              jnp.tile(jnp.concatenate([zero(NOPE + half), sin], axis=1), (1, reps)))
    tail = DKV_PAD - KV_RANK - ROPE
    kv_tabs = (jnp.concatenate([one(KV_RANK), cos, cos, one(tail)], axis=1),
               jnp.concatenate([zero(KV_RANK), -sin, zero(half + tail)], axis=1),
               jnp.concatenate([zero(KV_RANK + half), sin, zero(tail)], axis=1))
    return q_tabs, kv_tabs


def _trunk(x, slots, norms, wb, q_tabs, kv_tabs):
    s = x.shape[0]
    kcat = vh = None
    for layer in range(DEPTH):
        if layer < N_A:
            qkv = _make_norm_linear(f"sb{layer}_qkv", BF16)(
                x, norms['attn_norm'][layer], slots['sb_w_qkv'][layer], wb['sb_w_qkv'][layer])
            o = sb_attention(qkv)
            x = _make_linear_res(f"sb{layer}_o", BF16)(x, o, slots['sb_w_o'][layer], wb['sb_w_o'][layer])
        else:
            j = layer - N_A
            if j == 0:
                pad = ((0, 0), (0, DKV_PAD - KV_RANK - ROPE))
                down = _make_norm_linear("kv_down")(
                    x, norms['kv_norm'], jnp.pad(slots['mla_w_dkv'][0], pad), jnp.pad(wb['mla_w_dkv'][0], pad))
                kv = _make_norm_linear("kv_up")(
                    down[:, :KV_RANK], norms['mla_kv_lat_norm'], slots['mla_w_ukv'][0], wb['mla_w_ukv'][0])
                kv = kv.reshape(s, MLA_HEADS, NOPE + VDIM)
                k_rope = _make_rope("rope_k")(down, *kv_tabs)[:, KV_RANK:KV_RANK + ROPE]
                k_rope = jnp.broadcast_to(k_rope[:, None, :], (s, MLA_HEADS, ROPE))
                kcat = jnp.concatenate([kv[..., :NOPE], k_rope], axis=-1).transpose(1, 0, 2)
                vh = kv[..., NOPE:].transpose(1, 0, 2)
            c_q = _make_norm_linear(f"mla{j}_dq")(
                x, norms['attn_norm'][layer], slots['mla_w_dq'][j], wb['mla_w_dq'][j])
            q = _make_norm_linear(f"mla{j}_uq")(
                c_q, norms['mla_q_lat_norm'][j], slots['mla_w_uq'][j], wb['mla_w_uq'][j])
            q = _make_rope(f"rope_q{j}")(q, *q_tabs)
            q = q.reshape(s, MLA_HEADS, QK).transpose(1, 0, 2)
            o = mla_attention(q, kcat, vh)
            o = o.transpose(1, 0, 2).reshape(s, MLA_HEADS * VDIM)
            x = _make_linear_res(f"mla{j}_o")(x, o, slots['mla_w_o'][j], wb['mla_w_o'][j])
        x = _make_mlp_res(f"mlp{layer}")(
            x, norms['mlp_norm'][layer], slots['mlp_w1'][layer], slots['mlp_w2'][layer],
            wb['mlp_w1'][layer], wb['mlp_w2'][layer])
    return x


def _merge_blocks(gathered, ax):
    shp = gathered.shape[1:]
    return jnp.moveaxis(gathered, 0, ax).reshape(shp[:ax] + (N_DEV * shp[ax],) + shp[ax + 1:])


def _split_blocks(full, ax):
    shp = full.shape
    return jnp.moveaxis(full.reshape(shp[:ax] + (N_DEV, shp[ax] // N_DEV) + shp[ax + 1:]), ax, 0)


def _pack(parts):
    flat = jnp.concatenate([p.reshape(-1) for p in parts])
    rows = -(-flat.shape[0] // PACK_COLS)
    rows = -(-rows // PACK_ROW_ALIGN) * PACK_ROW_ALIGN
    return jnp.pad(flat, (0, rows * PACK_COLS - flat.shape[0])).reshape(rows, PACK_COLS)


def _unpack(packed, shapes):
    flat = packed.reshape(-1)
    out, off = [], 0
    for shp in shapes:
        n = math.prod(shp)
        out.append(flat[off:off + n].reshape(shp))
        off += n
    return out


def _add_round(x, stage, core, name):
    _, r, c = x.shape
    tr = _tile(r, (256, 128, 64, 32, 16, 8))

    def body(core_ref, x_ref, s_ref, o_ref):
        o_ref[...] = (x_ref[0] + s_ref[...]).astype(o_ref.dtype)

    return pl.pallas_call(
        body, name=name,
        grid_spec=pltpu.PrefetchScalarGridSpec(
            num_scalar_prefetch=1, grid=(r // tr,),
            in_specs=[pl.BlockSpec((1, tr, c), lambda i, core_ref: (core_ref[0], i, 0)),
                      pl.BlockSpec((tr, c), lambda i, core_ref: (i, 0))],
            out_specs=pl.BlockSpec((tr, c), lambda i, core_ref: (i, 0))),
        out_shape=jax.ShapeDtypeStruct((r, c), BF16),
        compiler_params=pltpu.CompilerParams(dimension_semantics=("parallel",)),
    )(core, x, stage)


def _adamw_reduce(parts, w, m, v, name):
    n_parts, r, c = parts.shape
    tr = _tile(r, (128, 64, 32, 16, 8))
    bias1 = 1.0 - ADAM_B1 ** ADAM_STEP
    bias2 = 1.0 - ADAM_B2 ** ADAM_STEP

    def body(p_ref, w_ref, m_ref, v_ref, g_ref, d_ref, nm_ref, nv_ref):
        g = p_ref[0].astype(F32)
        for s in range(1, n_parts):
            g = g + p_ref[s].astype(F32)
        mn = ADAM_B1 * m_ref[...] + (1.0 - ADAM_B1) * g
        vn = ADAM_B2 * v_ref[...] + (1.0 - ADAM_B2) * (g * g)
        m_hat = mn / bias1
        v_hat = vn / bias2
        g_ref[...] = g
        d_ref[...] = -ADAM_LR * (m_hat / (jnp.sqrt(v_hat) + ADAM_EPS) + ADAM_WD * w_ref[...])
        nm_ref[...] = mn
        nv_ref[...] = vn

    blk = pl.BlockSpec((tr, c), lambda i: (i, 0))
    return pl.pallas_call(
        body, name=name, grid=(r // tr,),
        in_specs=[pl.BlockSpec((n_parts, tr, c), lambda i: (0, i, 0)), blk, blk, blk],
        out_specs=[blk] * 4,
        out_shape=[jax.ShapeDtypeStruct((r, c), F32)] * 4,
        compiler_params=pltpu.CompilerParams(dimension_semantics=("parallel",), vmem_limit_bytes=VMEM_MM),
    )(parts, w, m, v)


def kernel(x, positions, attn_norm, mlp_norm, sb_w_qkv, sb_w_o, kv_norm, mla_w_dkv, mla_kv_lat_norm, mla_w_ukv, mla_w_dq, mla_q_lat_norm, mla_w_uq, mla_w_o, mlp_w1, mlp_w2, final_norm, loss_target, m_attn_norm, m_mlp_norm, m_sb_w_qkv, m_sb_w_o, m_kv_norm, m_mla_w_dkv, m_mla_kv_lat_norm, m_mla_w_ukv, m_mla_w_dq, m_mla_q_lat_norm, m_mla_w_uq, m_mla_w_o, m_mlp_w1, m_mlp_w2, m_final_norm, v_attn_norm, v_mlp_norm, v_sb_w_qkv, v_sb_w_o, v_kv_norm, v_mla_w_dkv, v_mla_kv_lat_norm, v_mla_w_ukv, v_mla_w_dq, v_mla_q_lat_norm, v_mla_w_uq, v_mla_w_o, v_mlp_w1, v_mlp_w2, v_final_norm):
    weights = dict(attn_norm=attn_norm, mlp_norm=mlp_norm, sb_w_qkv=sb_w_qkv, sb_w_o=sb_w_o, kv_norm=kv_norm,
                   mla_w_dkv=mla_w_dkv, mla_kv_lat_norm=mla_kv_lat_norm, mla_w_ukv=mla_w_ukv, mla_w_dq=mla_w_dq,
                   mla_q_lat_norm=mla_q_lat_norm, mla_w_uq=mla_w_uq, mla_w_o=mla_w_o, mlp_w1=mlp_w1, mlp_w2=mlp_w2,
                   final_norm=final_norm)
    mom_m = dict(attn_norm=m_attn_norm, mlp_norm=m_mlp_norm, sb_w_qkv=m_sb_w_qkv, sb_w_o=m_sb_w_o, kv_norm=m_kv_norm,
                 mla_w_dkv=m_mla_w_dkv, mla_kv_lat_norm=m_mla_kv_lat_norm, mla_w_ukv=m_mla_w_ukv, mla_w_dq=m_mla_w_dq,
                 mla_q_lat_norm=m_mla_q_lat_norm, mla_w_uq=m_mla_w_uq, mla_w_o=m_mla_w_o, mlp_w1=m_mlp_w1,
                 mlp_w2=m_mlp_w2, final_norm=m_final_norm)
    mom_v = dict(attn_norm=v_attn_norm, mlp_norm=v_mlp_norm, sb_w_qkv=v_sb_w_qkv, sb_w_o=v_sb_w_o, kv_norm=v_kv_norm,
                 mla_w_dkv=v_mla_w_dkv, mla_kv_lat_norm=v_mla_kv_lat_norm, mla_w_ukv=v_mla_w_ukv, mla_w_dq=v_mla_w_dq,
                 mla_q_lat_norm=v_mla_q_lat_norm, mla_w_uq=v_mla_w_uq, mla_w_o=v_mla_w_o, mlp_w1=v_mlp_w1,
                 mlp_w2=v_mlp_w2, final_norm=v_final_norm)
    sharded_names = [n for n, _ in SHARDED]
    repl_shapes = [tuple(weights[n].shape) for n in REPLICATED]

    gathered = _all_gather([weights[n].astype(BF16) for n in sharded_names], "gather_weights")
    wb, slots = {}, {}
    for (n, ax), g in zip(SHARDED, gathered):
        full = _merge_blocks(g, ax)
        layers = [full[l] for l in range(full.shape[0])] if full.ndim == 3 else [full]
        wb[n] = layers
        slots[n] = [jnp.zeros(w.shape, F32) for w in layers]
    norms = {n: weights[n] for n in REPLICATED if n != 'final_norm'}

    q_tabs, kv_tabs = _rope_tables(positions[0])
    x_last, pullback = jax.vjp(lambda xx, ss, nn: _trunk(xx, ss, nn, wb, q_tabs, kv_tabs), x[0], slots, norms)
    loss_part, dx_last, d_final = _loss_head(x_last, final_norm, loss_target[0])
    dx, d_slots, d_norms = pullback(dx_last)
    d_norms = dict(d_norms)
    d_norms['final_norm'] = d_final
    loss = lax.psum(loss_part[0, 0], ("x", "y", "c"))

    core = lax.axis_index("c").astype(jnp.int32).reshape(1)
    halves, dims = [], []
    for n, ax in SHARDED:
        full = jnp.stack(d_slots[n]) if weights[n].ndim == 3 else d_slots[n][0]
        blocks = _split_blocks(full, ax)
        two_d = (math.prod(blocks.shape[1:-1]), blocks.shape[-1])
        dims.append(two_d)
        halves.append(jnp.moveaxis(blocks.reshape((N_DEV // 2, 2) + two_d), 1, 0))
    staged = _pair_exchange(halves, "pair_grads")
    sums = []
    for n, h, st, (r, c) in zip(sharded_names, halves, staged, dims):
        rows = (N_DEV // 2) * r
        sums.append(_add_round(h.reshape(2, rows, c), st.reshape(rows, c), core, "pair_sum_" + n).reshape(st.shape))
    received = _chip_exchange(sums, "scatter_grads")
    repl_parts = _all_gather([_pack([d_norms[n] for n in REPLICATED])], "gather_norm_grads")[0]

    results = {kind: {} for kind in ("grad", "delta", "new_m", "new_v")}
    for n, parts, two_d in zip(sharded_names, received, dims):
        shp = weights[n].shape
        res = _adamw_reduce(parts, weights[n].reshape(two_d), mom_m[n].reshape(two_d),
                            mom_v[n].reshape(two_d), "adamw_" + n)
        for kind, a in zip(results, res):
            results[kind][n] = a.reshape(shp)
    res = _adamw_reduce(repl_parts, _pack([weights[n] for n in REPLICATED]), _pack([mom_m[n] for n in REPLICATED]),
                        _pack([mom_v[n] for n in REPLICATED]), "adamw_replicated")
    for kind, a in zip(results, res):
        results[kind].update(zip(REPLICATED, _unpack(a, repl_shapes)))

    out = [loss, dx[None]]
    for kind in ("grad", "delta", "new_m", "new_v"):
        out += [results[kind][n] for n in WEIGHT_ORDER]
    return tuple(out)
```

```python
import math

import jax
import jax.numpy as jnp
from jax import lax
from jax.experimental import pallas as pl
from jax.experimental.pallas import tpu as pltpu

F32 = jnp.float32
BF16 = jnp.bfloat16
MESH = pl.DeviceIdType.MESH

N_DEV = 8
DEPTH = 4
N_A = 2
SB_HEADS = 16
SB_HD = 64
MLA_HEADS = 16
NOPE = 64
ROPE = 32
VDIM = 64
QK = NOPE + ROPE
KV_RANK = 256
DKV_PAD = 384
LANES = 128
CHUNK_SHIFT = 6
ROPE_THETA = 10000.0
EPS = 1e-6
SB_BLOCK = 256
MLA_BLOCK = 512
PACK_COLS = 1024
PACK_ROW_ALIGN = 16
EXP_ZERO = -104.0
NEG_BIG = -1e30
VMEM_ATT = 56 * 1024 * 1024
VMEM_MM = 48 * 1024 * 1024

ADAM_LR = 0.001
ADAM_B1 = 0.9
ADAM_B2 = 0.999
ADAM_EPS = 1e-08
ADAM_WD = 0.01
ADAM_STEP = 10

WEIGHT_ORDER = ['attn_norm', 'mlp_norm', 'sb_w_qkv', 'sb_w_o', 'kv_norm', 'mla_w_dkv', 'mla_kv_lat_norm',
                'mla_w_ukv', 'mla_w_dq', 'mla_q_lat_norm', 'mla_w_uq', 'mla_w_o', 'mlp_w1', 'mlp_w2', 'final_norm']
SHARDED = [('sb_w_qkv', 2), ('sb_w_o', 1), ('mla_w_dkv', 0), ('mla_w_ukv', 1), ('mla_w_dq', 1),
           ('mla_w_uq', 2), ('mla_w_o', 1), ('mlp_w1', 2), ('mlp_w2', 1)]
REPLICATED = ['attn_norm', 'mlp_norm', 'kv_norm', 'mla_kv_lat_norm', 'mla_q_lat_norm', 'final_norm']


def _tile(dim, prefs=(512, 384, 256, 128, 64, 32, 16, 8)):
    for t in prefs:
        if dim % t == 0:
            return t
    return dim


def _dot(a, b, dims):
    return lax.dot_general(a, b, (dims, ((), ())), preferred_element_type=F32)


NN = ((1,), (0,))
NT = ((1,), (1,))
TN = ((0,), (0,))


def _all_gather(shards, name):
    n_t = len(shards)

    def body(*refs):
        x_refs, out_refs = refs[:n_t], refs[n_t:2 * n_t]
        send_sems, recv_sems, local_sems = refs[2 * n_t:]
        x, y, c = lax.axis_index("x"), lax.axis_index("y"), lax.axis_index("c")
        me, sibling = (x, y, c), (x, y, 1 - c)
        chips = [(1 - x, y), (x, 1 - y), (1 - x, 1 - y)]

        def slot(t, px, py, pc):
            return out_refs[t].at[4 * px + 2 * py + pc]

        def copy(t, k, block, to, src=None):
            return pltpu.make_async_remote_copy(
                src_ref=slot(t, *block) if src is None else src, dst_ref=slot(t, *block),
                send_sem=send_sems.at[7 * t + k], recv_sem=recv_sems.at[7 * t + k],
                device_id=to, device_id_type=MESH)

        mine = [pltpu.make_async_copy(x_refs[t], slot(t, *me), local_sems.at[t]) for t in range(n_t)]
        for cp in mine:
            cp.start()
        first = []
        for t in range(n_t):
            first.append(copy(t, 0, me, sibling, src=x_refs[t]))
            first += [copy(t, 1 + j, me, (*chip, c), src=x_refs[t]) for j, chip in enumerate(chips)]
        for cp in first:
            cp.start()
        passed = []
        for t in range(n_t):
            for j, chip in enumerate(chips):
                copy(t, 1 + j, (*chip, c), me).wait_recv()
                onward = copy(t, 4 + j, (*chip, c), sibling)
                onward.start()
                passed.append(onward)
        for t in range(n_t):
            copy(t, 0, sibling, me).wait_recv()
            for j, chip in enumerate(chips):
                copy(t, 4 + j, (*chip, 1 - c), me).wait_recv()
        for cp in first + passed:
            cp.wait_send()
        for cp in mine:
            cp.wait()

    any_spec = pl.BlockSpec(memory_space=pl.ANY)
    return pl.pallas_call(
        body, name=name,
        out_shape=[jax.ShapeDtypeStruct((N_DEV,) + tuple(s.shape), s.dtype) for s in shards],
        in_specs=[any_spec] * n_t, out_specs=[any_spec] * n_t,
        scratch_shapes=[pltpu.SemaphoreType.DMA((7 * n_t,)), pltpu.SemaphoreType.DMA((7 * n_t,)),
                        pltpu.SemaphoreType.DMA((n_t,))],
    )(*shards)


def _pair_exchange(xs, name):
    n_t = len(xs)

    def body(*refs):
        x_refs, out_refs = refs[:n_t], refs[n_t:2 * n_t]
        send_sems, recv_sems = refs[2 * n_t:]
        x, y, c = lax.axis_index("x"), lax.axis_index("y"), lax.axis_index("c")
        copies = [pltpu.make_async_remote_copy(
            src_ref=x_refs[t].at[1 - c], dst_ref=out_refs[t], send_sem=send_sems.at[t], recv_sem=recv_sems.at[t],
            device_id=(x, y, 1 - c), device_id_type=MESH) for t in range(n_t)]
        for cp in copies:
            cp.start()
        for cp in copies:
            cp.wait()

    any_spec = pl.BlockSpec(memory_space=pl.ANY)
    return pl.pallas_call(
        body, name=name,
        out_shape=[jax.ShapeDtypeStruct(x.shape[1:], x.dtype) for x in xs],
        in_specs=[any_spec] * n_t, out_specs=[any_spec] * n_t,
        scratch_shapes=[pltpu.SemaphoreType.DMA((n_t,)), pltpu.SemaphoreType.DMA((n_t,))],
    )(*xs)


def _chip_exchange(ps, name):
    n_t = len(ps)

    def body(*refs):
        p_refs, out_refs = refs[:n_t], refs[n_t:2 * n_t]
        send_sems, recv_sems, local_sems = refs[2 * n_t:]
        mx, my, mc = lax.axis_index("x"), lax.axis_index("y"), lax.axis_index("c")
        me = 2 * mx + my
        mine = [pltpu.make_async_copy(p_refs[t].at[me], out_refs[t].at[me], local_sems.at[t]) for t in range(n_t)]
        for cp in mine:
            cp.start()
        copies = []
        for k in range(1, 4):
            px = 1 - mx if (k >> 1) & 1 else mx
            py = 1 - my if k & 1 else my
            peer = 2 * px + py
            for t in range(n_t):
                copies.append(pltpu.make_async_remote_copy(
                    src_ref=p_refs[t].at[peer], dst_ref=out_refs[t].at[me],
                    send_sem=send_sems.at[3 * t + k - 1], recv_sem=recv_sems.at[3 * t + k - 1],
                    device_id=(px, py, mc), device_id_type=MESH))
        for cp in copies:
            cp.start()
        for cp in copies:
            cp.wait_send()
        for cp in copies:
            cp.wait_recv()
        for cp in mine:
            cp.wait()

    any_spec = pl.BlockSpec(memory_space=pl.ANY)
    return pl.pallas_call(
        body, name=name,
        out_shape=[jax.ShapeDtypeStruct(p.shape, p.dtype) for p in ps],
        in_specs=[any_spec] * n_t, out_specs=[any_spec] * n_t,
        scratch_shapes=[pltpu.SemaphoreType.DMA((3 * n_t,)), pltpu.SemaphoreType.DMA((3 * n_t,)),
                        pltpu.SemaphoreType.DMA((n_t,))],
    )(*ps)


def _matmul(a, b, *, ta=False, tb=False, out_dtypes=(F32,), epilogue=None, extras=(), name):
    if ta:
        kdim, m = a.shape
    else:
        m, kdim = a.shape
    if tb:
        n, kb = b.shape
    else:
        kb, n = b.shape
    assert kdim == kb, (a.shape, b.shape, ta, tb)
    big = (1024, 768, 512, 384, 256, 128, 64, 32, 16, 8)
    tm, tn = _tile(m, big), _tile(n, big)
    tk = kdim if kdim <= 2048 else _tile(kdim, (1024, 512, 256, 128))
    nk = kdim // tk
    n_extra, n_out = len(extras), len(out_dtypes)
    a_spec = pl.BlockSpec((tk, tm), lambda i, j, k: (k, i)) if ta else pl.BlockSpec((tm, tk), lambda i, j, k: (i, k))
    b_spec = pl.BlockSpec((tn, tk), lambda i, j, k: (j, k)) if tb else pl.BlockSpec((tk, tn), lambda i, j, k: (k, j))
    tile_spec = pl.BlockSpec((tm, tn), lambda i, j, k: (i, j))
    dims = ((0,) if ta else (1,), (1,) if tb else (0,))

    def finish(acc, extra_refs, out_refs):
        outs = (acc,) if epilogue is None else epilogue(acc, *[r[...] for r in extra_refs])
        for o_ref, o in zip(out_refs, outs):
            o_ref[...] = o.astype(o_ref.dtype)

    def body_one(a_ref, b_ref, *rest):
        acc = _dot(a_ref[...].astype(BF16), b_ref[...].astype(BF16), dims)
        finish(acc, rest[:n_extra], rest[n_extra:n_extra + n_out])

    def body_acc(a_ref, b_ref, *rest):
        acc_ref = rest[-1]
        k = pl.program_id(2)

        @pl.when(k == 0)
        def _():
            acc_ref[...] = jnp.zeros_like(acc_ref)

        acc_ref[...] += _dot(a_ref[...].astype(BF16), b_ref[...].astype(BF16), dims)

        @pl.when(k == nk - 1)
        def _():
            finish(acc_ref[...], rest[:n_extra], rest[n_extra:n_extra + n_out])

    return pl.pallas_call(
        body_one if nk == 1 else body_acc, name=name, grid=(m // tm, n // tn, nk),
        in_specs=[a_spec, b_spec] + [tile_spec] * n_extra,
        out_specs=[tile_spec] * n_out,
        out_shape=[jax.ShapeDtypeStruct((m, n), dt) for dt in out_dtypes],
        scratch_shapes=[] if nk == 1 else [pltpu.VMEM((tm, tn), F32)],
        compiler_params=pltpu.CompilerParams(
            dimension_semantics=("parallel", "parallel", "arbitrary"), vmem_limit_bytes=VMEM_MM),
    )(a, b, *extras)


def _rms_fwd(x, g, name):
    m, d = x.shape
    tm = _tile(m, (512, 256, 128, 64, 32, 16, 8))

    def body(x_ref, g_ref, y_ref):
        xv = x_ref[...]
        r = lax.rsqrt(jnp.mean(xv * xv, axis=-1, keepdims=True) + EPS)
        y_ref[...] = (xv * r * g_ref[...]).astype(y_ref.dtype)

    return pl.pallas_call(
        body, name=name, grid=(m // tm,),
        in_specs=[pl.BlockSpec((tm, d), lambda i: (i, 0)), pl.BlockSpec((1, d), lambda i: (0, 0))],
        out_specs=pl.BlockSpec((tm, d), lambda i: (i, 0)),
        out_shape=jax.ShapeDtypeStruct((m, d), BF16),
        compiler_params=pltpu.CompilerParams(dimension_semantics=("parallel",)),
    )(x, g.reshape(1, d))


def _rms_bwd(x, g, dy, name, res=None):
    m, d = x.shape
    tm = _tile(m, (512, 256, 128, 64, 32, 16, 8))
    has_res = res is not None

    def body(x_ref, g_ref, dy_ref, *rest):
        dx_ref, dg_ref = rest[-2:]
        xv = x_ref[...]
        dyv = dy_ref[...]
        r = lax.rsqrt(jnp.mean(xv * xv, axis=-1, keepdims=True) + EPS)
        xh = xv * r
        t = dyv * g_ref[...]
        dx = r * (t - xh * jnp.mean(t * xh, axis=-1, keepdims=True))
        dx_ref[...] = dx + rest[0][...] if has_res else dx

        @pl.when(pl.program_id(0) == 0)
        def _():
            dg_ref[...] = jnp.zeros_like(dg_ref)

        dg_ref[...] += jnp.sum(dyv * xh, axis=0, keepdims=True)

    row_spec = pl.BlockSpec((tm, d), lambda i: (i, 0))
    vec_spec = pl.BlockSpec((1, d), lambda i: (0, 0))
    dx, dg = pl.pallas_call(
        body, name=name, grid=(m // tm,),
        in_specs=[row_spec, vec_spec, row_spec] + ([row_spec] if has_res else []),
        out_specs=[row_spec, vec_spec],
        out_shape=[jax.ShapeDtypeStruct((m, d), F32), jax.ShapeDtypeStruct((1, d), F32)],
        compiler_params=pltpu.CompilerParams(dimension_semantics=("arbitrary",)),
    )(x, g.reshape(1, d), dy, *((res,) if has_res else ()))
    return dx, dg.reshape(d)


def _loss_head(x, g, target):
    m, d = x.shape
    tm = _tile(m, (512, 256, 128, 64, 32, 16, 8))

    def body(x_ref, g_ref, t_ref, loss_ref, dx_ref, dg_ref):
        xv = x_ref[...]
        gv = g_ref[...]
        r = lax.rsqrt(jnp.mean(xv * xv, axis=-1, keepdims=True) + EPS)
        xh = xv * r
        err = xh * gv - t_ref[...]
        row_loss = jnp.mean(err * err, axis=-1, keepdims=True)
        dyv = err * (1.0 / d)
        t = dyv * gv
        dx_ref[...] = r * (t - xh * jnp.mean(t * xh, axis=-1, keepdims=True))

        @pl.when(pl.program_id(0) == 0)
        def _():
            dg_ref[...] = jnp.zeros_like(dg_ref)
            loss_ref[...] = jnp.zeros_like(loss_ref)

        dg_ref[...] += jnp.sum(dyv * xh, axis=0, keepdims=True)
        loss_ref[...] += 0.5 * jnp.sum(row_loss, axis=0, keepdims=True)

    loss, dx, dg = pl.pallas_call(
        body, name="loss_head", grid=(m // tm,),
        in_specs=[pl.BlockSpec((tm, d), lambda i: (i, 0)), pl.BlockSpec((1, d), lambda i: (0, 0)),
                  pl.BlockSpec((tm, d), lambda i: (i, 0))],
        out_specs=[pl.BlockSpec((1, 1), lambda i: (0, 0)), pl.BlockSpec((tm, d), lambda i: (i, 0)),
                   pl.BlockSpec((1, d), lambda i: (0, 0))],
        out_shape=[jax.ShapeDtypeStruct((1, 1), F32), jax.ShapeDtypeStruct((m, d), F32),
                   jax.ShapeDtypeStruct((1, d), F32)],
        compiler_params=pltpu.CompilerParams(dimension_semantics=("arbitrary",)),
    )(x, g.reshape(1, d), target)
    return loss, dx, dg.reshape(d)


def _rope_apply(x, tabs, transpose, name):
    m, w = x.shape
    wt = tabs[0].shape[1]
    reps = w // wt
    half = ROPE // 2
    tm = _tile(m, (256, 128, 64, 32, 16, 8))

    def body(x_ref, c_ref, a_ref, b_ref, y_ref):
        xv = x_ref[...]

        def wide(t_ref):
            t = t_ref[...]
            return t if reps == 1 else jnp.concatenate([t] * reps, axis=1)

        c, a, b = wide(c_ref), wide(a_ref), wide(b_ref)
        if transpose:
            y = xv * c + pltpu.roll(xv * a, half, 1) + pltpu.roll(xv * b, w - half, 1)
        else:
            y = xv * c + pltpu.roll(xv, w - half, 1) * a + pltpu.roll(xv, half, 1) * b
        y_ref[...] = y

    x_spec = pl.BlockSpec((tm, w), lambda i: (i, 0))
    t_spec = pl.BlockSpec((tm, wt), lambda i: (i, 0))
    return pl.pallas_call(
        body, name=name, grid=(m // tm,),
        in_specs=[x_spec, t_spec, t_spec, t_spec], out_specs=x_spec,
        out_shape=jax.ShapeDtypeStruct((m, w), F32),
        compiler_params=pltpu.CompilerParams(dimension_semantics=("parallel",)),
    )(x, *tabs)


def _log_sigmoid_pair(z):
    a = jnp.minimum(z, 0.0) - jnp.log(1.0 + jnp.exp(-jnp.abs(z)))
    return a, a - z


def _split_bf16(x):
    hi = x.astype(BF16)
    return hi, (x - hi.astype(F32)).astype(BF16)


def _wide(v, width):
    return v if width == LANES else jnp.concatenate([v] * (width // LANES), axis=1)


def _lanes(col):
    return jnp.broadcast_to(col, (col.shape[0], LANES))


def _sb_weights(qh, k2, valid, scale):
    z = _dot(qh, k2, NT) * scale
    a, b = _log_sigmoid_pair(z)
    return a, jnp.where(valid, b, 0.0)


def _sb_fwd(qkv):
    s = qkv.shape[0]
    t = min(SB_BLOCK, s)
    nq = s // t
    npair = SB_HEADS // 2
    scale = SB_HD ** -0.5

    def body(q_ref, k_ref, v_ref, o_ref, bta_ref, btb_ref, js_ref, acc_ref, ra_ref, rb_ref):
        p, i = pl.program_id(0), pl.program_id(1)
        q2 = q_ref[...]
        first = lax.broadcasted_iota(jnp.int32, (t, LANES), 1) < SB_HD
        heads = (jnp.where(first, q2, jnp.zeros_like(q2)), jnp.where(first, jnp.zeros_like(q2), q2))
        row = lax.broadcasted_iota(jnp.int32, (t, t), 0)
        col = lax.broadcasted_iota(jnp.int32, (t, t), 1)
        later = jnp.where(row > col, 1.0, 0.0).astype(BF16)
        acc_ref[...] = jnp.zeros_like(acc_ref)
        ra_ref[...] = jnp.zeros_like(ra_ref)
        rb_ref[...] = jnp.zeros_like(rb_ref)

        def cond(carry):
            j, rmax = carry
            return jnp.logical_and(j >= 0, rmax > EXP_ZERO)

        def step(carry):
            j, _ = carry
            off = pl.multiple_of(j * t, t)
            k2 = k_ref[pl.ds(off, t), :]
            v2 = v_ref[pl.ds(off, t), :]
            valid = col + (j - i) * t < row
            outs, rmax = [], None
            for qh, r_ref in zip(heads, (ra_ref, rb_ref)):
                a, b = _sb_weights(qh, k2, valid, scale)
                bh, bl = _split_bf16(b)
                inner = _dot(bh, later, NN) + _dot(bl, later, NN)
                r = r_ref[...]
                w = jnp.where(valid, jnp.exp(a + inner + _wide(r, t)), 0.0)
                outs.append(_dot(w.astype(BF16), v2, NN))
                rn = r + jnp.sum(b, axis=1, keepdims=True)
                r_ref[...] = rn
                rmax = jnp.max(rn) if rmax is None else jnp.maximum(rmax, jnp.max(rn))
            acc_ref[...] += jnp.where(first, outs[0], outs[1])
            return j - 1, rmax

        jend, _ = lax.while_loop(cond, step, (i, jnp.float32(0.0)))
        o_ref[...] = acc_ref[...].astype(o_ref.dtype)
        bta_ref[0] = ra_ref[...][:, :1]
        btb_ref[0] = rb_ref[...][:, :1]
        js_ref[p, i] = (jend + 1).astype(F32)

    stat = pl.BlockSpec((1, t, 1), lambda p, i: (p, i, 0))
    return pl.pallas_call(
        body, name="sb_attn_fwd", grid=(npair, nq),
        in_specs=[pl.BlockSpec((t, LANES), lambda p, i: (i, p)),
                  pl.BlockSpec((s, LANES), lambda p, i: (0, npair + p)),
                  pl.BlockSpec((s, LANES), lambda p, i: (0, 2 * npair + p))],
        out_specs=[pl.BlockSpec((t, LANES), lambda p, i: (i, p)), stat, stat,
                   pl.BlockSpec(memory_space=pltpu.SMEM)],
        out_shape=[jax.ShapeDtypeStruct((s, SB_HEADS * SB_HD), BF16), jax.ShapeDtypeStruct((npair, s, 1), F32),
                   jax.ShapeDtypeStruct((npair, s, 1), F32), jax.ShapeDtypeStruct((npair, nq), F32)],
        scratch_shapes=[pltpu.VMEM((t, LANES), F32)] * 3,
        compiler_params=pltpu.CompilerParams(
            dimension_semantics=("arbitrary", "arbitrary"), vmem_limit_bytes=VMEM_ATT),
    )(qkv, qkv, qkv)


def _sb_bwd(qkv, do, btot_a, btot_b, jstart):
    s = qkv.shape[0]
    t = min(SB_BLOCK, s)
    nq = s // t
    npair = SB_HEADS // 2
    scale = SB_HD ** -0.5

    def body(js_ref, q_ref, k_ref, v_ref, do_ref, bta_ref, btb_ref, dq_ref, dk_ref, dv_ref,
             dq_acc, pa_ref, pb_ref, ga_ref, gb_ref):
        p, i = pl.program_id(0), pl.program_id(1)

        @pl.when(i == 0)
        def _():
            dk_ref[...] = jnp.zeros_like(dk_ref)
            dv_ref[...] = jnp.zeros_like(dv_ref)

        q2 = q_ref[...]
        do2 = do_ref[...]
        first = lax.broadcasted_iota(jnp.int32, (t, LANES), 1) < SB_HD
        zero = jnp.zeros_like(q2)
        q_heads = (jnp.where(first, q2, zero), jnp.where(first, zero, q2))
        do_heads = (jnp.where(first, do2, zero), jnp.where(first, zero, do2))
        bts = (_wide(_lanes(bta_ref[0]), t), _wide(_lanes(btb_ref[0]), t))
        row = lax.broadcasted_iota(jnp.int32, (t, t), 0)
        col = lax.broadcasted_iota(jnp.int32, (t, t), 1)
        upto = jnp.where(row <= col, 1.0, 0.0).astype(BF16)
        before = jnp.where(row < col, 1.0, 0.0).astype(BF16)
        dq_acc[...] = jnp.zeros_like(dq_acc)
        for r in (pa_ref, pb_ref, ga_ref, gb_ref):
            r[...] = jnp.zeros_like(r)
        j0 = jnp.clip(js_ref[p, i].astype(jnp.int32), 0, i)

        def step(j, carry):
            off = pl.multiple_of(j * t, t)
            k2 = k_ref[pl.ds(off, t), :]
            v2 = v_ref[pl.ds(off, t), :]
            valid = col + (j - i) * t < row
            dqs, dk2, dv2 = [], None, None
            for qh, doh, bt, p_ref, g_ref in zip(q_heads, do_heads, bts, (pa_ref, pb_ref), (ga_ref, gb_ref)):
                a, b = _sb_weights(qh, k2, valid, scale)
                bh, bl = _split_bf16(b)
                pin = _dot(bh, upto, NN) + _dot(bl, upto, NN)
                surv = bt - (_wide(p_ref[...], t) + pin)
                w = jnp.where(valid, jnp.exp(a + surv), 0.0)
                g = w * _dot(doh, v2, NT)
                gh, gl = _split_bf16(g)
                gsum = _wide(g_ref[...], t) + _dot(gh, before, NN) + _dot(gl, before, NN)
                beta = jnp.exp(a)
                dz = jnp.where(valid, (g * (1.0 - beta) - gsum * beta) * scale, 0.0).astype(BF16)
                dqs.append(_dot(dz, k2, NN))
                dkh = _dot(dz, qh, TN)
                dvh = _dot(w.astype(BF16), doh, TN)
                dk2 = dkh if dk2 is None else dk2 + dkh
                dv2 = dvh if dv2 is None else dv2 + dvh
                p_ref[...] += jnp.sum(b, axis=1, keepdims=True)
                g_ref[...] += jnp.sum(g, axis=1, keepdims=True)
            dq_acc[...] += jnp.where(first, dqs[0], dqs[1])
            dk_ref[pl.ds(off, t), :] += dk2
            dv_ref[pl.ds(off, t), :] += dv2
            return carry

        lax.fori_loop(j0, i + 1, step, 0)
        dq_ref[...] = dq_acc[...]

    blk = pl.BlockSpec((t, LANES), lambda p, i: (i, p))
    full = pl.BlockSpec((s, LANES), lambda p, i: (0, p))
    stat = pl.BlockSpec((1, t, 1), lambda p, i: (p, i, 0))
    vec = pltpu.VMEM((t, LANES), F32)
    return pl.pallas_call(
        body, name="sb_attn_bwd", grid=(npair, nq),
        in_specs=[pl.BlockSpec(memory_space=pltpu.SMEM), blk,
                  pl.BlockSpec((s, LANES), lambda p, i: (0, npair + p)),
                  pl.BlockSpec((s, LANES), lambda p, i: (0, 2 * npair + p)), blk, stat, stat],
        out_specs=[blk, full, full],
        out_shape=[jax.ShapeDtypeStruct((s, SB_HEADS * SB_HD), F32)] * 3,
        scratch_shapes=[pltpu.VMEM((t, LANES), F32), vec, vec, vec, vec],
        compiler_params=pltpu.CompilerParams(
            dimension_semantics=("arbitrary", "arbitrary"), vmem_limit_bytes=VMEM_ATT),
    )(jstart, qkv, qkv, qkv, do, btot_a, btot_b)


@jax.custom_vjp
def sb_attention(qkv):
    return _sb_fwd(qkv)[0]


def _sb_attention_fwd(qkv):
    o, btot_a, btot_b, jstart = _sb_fwd(qkv)
    return o, (qkv, btot_a, btot_b, jstart)


def _sb_attention_bwd(res, do):
    qkv, btot_a, btot_b, jstart = res
    dq, dk, dv = _sb_bwd(qkv, do, btot_a, btot_b, jstart)
    return (jnp.concatenate([dq, dk, dv], axis=1).astype(BF16),)


sb_attention.defvjp(_sb_attention_fwd, _sb_attention_bwd)


def _chunk_allowed(row, col):
    return (col >> CHUNK_SHIFT) <= (row >> CHUNK_SHIFT)


def _mla_fwd(q, k, v_ext):
    h_, s, dqk = q.shape
    t = min(MLA_BLOCK, s)
    nq = s // t
    scale = dqk ** -0.5
    hp = 2

    def body(q_ref, k_ref, v_ref, o_ref, lse_ref, acc_ref, m_ref):
        i = pl.program_id(1)
        acc_ref[...] = jnp.zeros_like(acc_ref)
        m_ref[...] = jnp.full_like(m_ref, NEG_BIG)

        def tile(j, diagonal):
            off = pl.multiple_of(j * t, t)
            for hh in range(hp):
                kj = k_ref[hh, pl.ds(off, t), :]
                vj = v_ref[hh, pl.ds(off, t), :]
                sc = _dot(q_ref[hh], kj, NT) * scale
                if diagonal:
                    row = lax.broadcasted_iota(jnp.int32, (t, t), 0)
                    col = lax.broadcasted_iota(jnp.int32, (t, t), 1)
                    sc = jnp.where(_chunk_allowed(row, col), sc, NEG_BIG)
                m_old = m_ref[hh]
                m_new = jnp.maximum(m_old, jnp.max(sc, axis=1, keepdims=True))
                p = jnp.exp(sc - _wide(m_new, t))
                acc_ref[hh] = jnp.exp(m_old - m_new) * acc_ref[hh] + _dot(p.astype(BF16), vj, NN)
                m_ref[hh] = m_new

        def step(j, carry):
            tile(j, False)
            return carry

        lax.fori_loop(0, i, step, 0)
        tile(i, True)
        for hh in range(hp):
            acc = acc_ref[hh]
            den = acc[:, VDIM:VDIM + 1]
            o_ref[hh] = acc[:, :VDIM] / den
            lse_ref[hh] = m_ref[hh][:, :1] + jnp.log(den)

    return pl.pallas_call(
        body, name="mla_attn_fwd", grid=(h_ // hp, nq),
        in_specs=[pl.BlockSpec((hp, t, dqk), lambda h, i: (h, i, 0)),
                  pl.BlockSpec((hp, s, dqk), lambda h, i: (h, 0, 0)),
                  pl.BlockSpec((hp, s, LANES), lambda h, i: (h, 0, 0))],
        out_specs=[pl.BlockSpec((hp, t, VDIM), lambda h, i: (h, i, 0)),
                   pl.BlockSpec((hp, t, 1), lambda h, i: (h, i, 0))],
        out_shape=[jax.ShapeDtypeStruct((h_, s, VDIM), F32), jax.ShapeDtypeStruct((h_, s, 1), F32)],
        scratch_shapes=[pltpu.VMEM((hp, t, LANES), F32), pltpu.VMEM((hp, t, LANES), F32)],
        compiler_params=pltpu.CompilerParams(
            dimension_semantics=("arbitrary", "arbitrary"), vmem_limit_bytes=VMEM_ATT),
    )(q, k, v_ext)


def _mla_bwd(q, k, v, o, lse, do):
    h_, s, dqk = q.shape
    dv = v.shape[2]
    t = min(MLA_BLOCK, s)
    nq = s // t
    scale = dqk ** -0.5

    def body(q_ref, k_ref, v_ref, o_ref, lse_ref, do_ref, dq_ref, dk_ref, dv_ref, dq_acc):
        i = pl.program_id(1)

        @pl.when(i == 0)
        def _():
            dk_ref[...] = jnp.zeros_like(dk_ref)
            dv_ref[...] = jnp.zeros_like(dv_ref)

        qv = q_ref[0]
        dof = do_ref[0]
        dov = dof.astype(BF16)
        delta = _lanes(jnp.sum(dof * o_ref[0], axis=1, keepdims=True))
        lse = _lanes(lse_ref[0])
        dq_acc[...] = jnp.zeros_like(dq_acc)

        def tile(j, diagonal):
            off = pl.multiple_of(j * t, t)
            kj = k_ref[0, pl.ds(off, t), :]
            vj = v_ref[0, pl.ds(off, t), :]
            p = jnp.exp(_dot(qv, kj, NT) * scale - _wide(lse, t))
            if diagonal:
                row = lax.broadcasted_iota(jnp.int32, (t, t), 0)
                col = lax.broadcasted_iota(jnp.int32, (t, t), 1)
                p = jnp.where(_chunk_allowed(row, col), p, 0.0)
            dp = _dot(dov, vj, NT)
            ds = (p * (dp - _wide(delta, t)) * scale).astype(BF16)
            dq_acc[...] += _dot(ds, kj, NN)
            dk_ref[0, pl.ds(off, t), :] += _dot(ds, qv, TN)
            dv_ref[0, pl.ds(off, t), :] += _dot(p.astype(BF16), dov, TN)

        def step(j, carry):
            tile(j, False)
            return carry

        lax.fori_loop(0, i, step, 0)
        tile(i, True)
        dq_ref[0] = dq_acc[...]

    qblk = pl.BlockSpec((1, t, dqk), lambda h, i: (h, i, 0))
    vblk = pl.BlockSpec((1, t, dv), lambda h, i: (h, i, 0))
    kfull = pl.BlockSpec((1, s, dqk), lambda h, i: (h, 0, 0))
    vfull = pl.BlockSpec((1, s, dv), lambda h, i: (h, 0, 0))
    stat = pl.BlockSpec((1, t, 1), lambda h, i: (h, i, 0))
    return pl.pallas_call(
        body, name="mla_attn_bwd", grid=(h_, nq),
        in_specs=[qblk, kfull, vfull, vblk, stat, vblk],
        out_specs=[qblk, kfull, vfull],
        out_shape=[jax.ShapeDtypeStruct((h_, s, dqk), F32), jax.ShapeDtypeStruct((h_, s, dqk), F32),
                   jax.ShapeDtypeStruct((h_, s, dv), F32)],
        scratch_shapes=[pltpu.VMEM((t, dqk), F32)],
        compiler_params=pltpu.CompilerParams(
            dimension_semantics=("arbitrary", "arbitrary"), vmem_limit_bytes=VMEM_ATT),
    )(q, k, v, o, lse, do)


def _with_ones(vb):
    h_, s, dv = vb.shape
    return jnp.concatenate([vb, jnp.ones((h_, s, 1), BF16), jnp.zeros((h_, s, LANES - dv - 1), BF16)], axis=-1)


@jax.custom_vjp
def mla_attention(q, k, v):
    return _mla_fwd(q.astype(BF16), k.astype(BF16), _with_ones(v.astype(BF16)))[0]


def _mla_attention_fwd(q, k, v):
    qb, kb, vb = q.astype(BF16), k.astype(BF16), v.astype(BF16)
    o, lse = _mla_fwd(qb, kb, _with_ones(vb))
    return o, (qb, kb, vb, o, lse)


def _mla_attention_bwd(res, do):
    qb, kb, vb, o, lse = res
    return tuple(_mla_bwd(qb, kb, vb, o, lse, do))


mla_attention.defvjp(_mla_attention_fwd, _mla_attention_bwd)


def _add_tile(acc, res):
    return (acc + res,)


def _make_linear_res(tag, a_dtype=F32):
    def forward(x, a, slot, wb):
        ab = a.astype(BF16)
        y = _matmul(ab, wb, epilogue=_add_tile, extras=(x,), name=tag + "_fwd")[0]
        return y, (ab, wb)

    @jax.custom_vjp
    def f(x, a, slot, wb):
        return forward(x, a, slot, wb)[0]

    def bwd(res, dy):
        ab, wb = res
        dyb = dy.astype(BF16)
        da = _matmul(dyb, wb, tb=True, out_dtypes=(a_dtype,), name=tag + "_da")[0]
        dw = _matmul(ab, dyb, ta=True, name=tag + "_dw")[0]
        return dy, da, dw, jnp.zeros_like(wb)

    f.defvjp(forward, bwd)
    return f


def _make_norm_linear(tag, out_dtype=F32):
    def forward(x, g, slot, wb):
        hb = _rms_fwd(x, g, tag + "_norm")
        y = _matmul(hb, wb, out_dtypes=(out_dtype,), name=tag + "_fwd")[0]
        return y, (x, g, wb, hb)

    @jax.custom_vjp
    def f(x, g, slot, wb):
        return forward(x, g, slot, wb)[0]

    def bwd(res, dy):
        x, g, wb, hb = res
        dyb = dy.astype(BF16)
        dh = _matmul(dyb, wb, tb=True, name=tag + "_dh")[0]
        dw = _matmul(hb, dyb, ta=True, name=tag + "_dw")[0]
        dx, dg = _rms_bwd(x, g, dh, tag + "_norm_bwd")
        return dx, dg, dw, jnp.zeros_like(wb)

    f.defvjp(forward, bwd)
    return f


def _relu2_fwd(acc):
    r = jnp.maximum(acc, 0.0)
    return acc, r * r


def _relu2_bwd(acc, u):
    return (acc * (2.0 * jnp.maximum(u, 0.0)),)


def _make_mlp_res(tag):
    def forward(x, g, slot1, slot2, w1b, w2b):
        hb = _rms_fwd(x, g, tag + "_norm")
        u, act = _matmul(hb, w1b, out_dtypes=(F32, BF16), epilogue=_relu2_fwd, name=tag + "_up")
        y = _matmul(act, w2b, epilogue=_add_tile, extras=(x,), name=tag + "_down")[0]
        return y, (x, g, w1b, w2b, hb, u, act)

    @jax.custom_vjp
    def f(x, g, slot1, slot2, w1b, w2b):
        return forward(x, g, slot1, slot2, w1b, w2b)[0]

    def bwd(res, dy):
        x, g, w1b, w2b, hb, u, act = res
        dyb = dy.astype(BF16)
        du = _matmul(dyb, w2b, tb=True, out_dtypes=(BF16,), epilogue=_relu2_bwd, extras=(u,), name=tag + "_du")[0]
        dw2 = _matmul(act, dyb, ta=True, name=tag + "_dw2")[0]
        dw1 = _matmul(hb, du, ta=True, name=tag + "_dw1")[0]
        dh = _matmul(du, w1b, tb=True, name=tag + "_dh")[0]
        dx, dg = _rms_bwd(x, g, dh, tag + "_norm_bwd", res=dy)
        return dx, dg, dw1, dw2, jnp.zeros_like(w1b), jnp.zeros_like(w2b)

    f.defvjp(forward, bwd)
    return f


def _make_rope(tag):
    @jax.custom_vjp
    def f(x, c, a, b):
        return _rope_apply(x, (c, a, b), False, tag + "_fwd")

    def fwd(x, c, a, b):
        return _rope_apply(x, (c, a, b), False, tag + "_fwd"), (c, a, b)

    def bwd(res, dy):
        c, a, b = res
        return _rope_apply(dy, (c, a, b), True, tag + "_bwd"), jnp.zeros_like(c), jnp.zeros_like(a), jnp.zeros_like(b)

    f.defvjp(fwd, bwd)
    return f


def _rope_tables(positions):
    half = ROPE // 2
    inv_freq = ROPE_THETA ** (-jnp.arange(0, ROPE, 2, dtype=F32) / ROPE)
    ang = positions.astype(F32)[:, None] * inv_freq
    cos, sin = jnp.cos(ang), jnp.sin(ang)
    s = positions.shape[0]
    one = lambda n: jnp.ones((s, n), F32)
    zero = lambda n: jnp.zeros((s, n), F32)
    reps = DKV_PAD // QK
    q_tabs = (jnp.tile(jnp.concatenate([one(NOPE), cos, cos], axis=1), (1, reps)),
              jnp.tile(jnp.concatenate([zero(NOPE), -sin, zero(half)], axis=1), (1, reps)),
              jnp.tile(jnp.concatenate([zero(NOPE + half), sin], axis=1), (1, reps)))
    tail = DKV_PAD - KV_RANK - ROPE
    kv_tabs = (jnp.concatenate([one(KV_RANK), cos, cos, one(tail)], axis=1),
               jnp.concatenate([zero(KV_RANK), -sin, zero(half + tail)], axis=1),
               jnp.concatenate([zero(KV_RANK + half), sin, zero(tail)], axis=1))
    return q_tabs, kv_tabs


def _trunk(x, slots, norms, wb, q_tabs, kv_tabs):
    s = x.shape[0]
    kcat = vh = None
    for layer in range(DEPTH):
        if layer < N_A:
            qkv = _make_norm_linear(f"sb{layer}_qkv", BF16)(
                x, norms['attn_norm'][layer], slots['sb_w_qkv'][layer], wb['sb_w_qkv'][layer])
            o = sb_attention(qkv)
            x = _make_linear_res(f"sb{layer}_o", BF16)(x, o, slots['sb_w_o'][layer], wb['sb_w_o'][layer])
        else:
            j = layer - N_A
            if j == 0:
                pad = ((0, 0), (0, DKV_PAD - KV_RANK - ROPE))
                down = _make_norm_linear("kv_down")(
                    x, norms['kv_norm'], jnp.pad(slots['mla_w_dkv'][0], pad), jnp.pad(wb['mla_w_dkv'][0], pad))
                kv = _make_norm_linear("kv_up")(
                    down[:, :KV_RANK], norms['mla_kv_lat_norm'], slots['mla_w_ukv'][0], wb['mla_w_ukv'][0])
                kv = kv.reshape(s, MLA_HEADS, NOPE + VDIM)
                k_rope = _make_rope("rope_k")(down, *kv_tabs)[:, KV_RANK:KV_RANK + ROPE]
                k_rope = jnp.broadcast_to(k_rope[:, None, :], (s, MLA_HEADS, ROPE))
                kcat = jnp.concatenate([kv[..., :NOPE], k_rope], axis=-1).transpose(1, 0, 2)
                vh = kv[..., NOPE:].transpose(1, 0, 2)
            c_q = _make_norm_linear(f"mla{j}_dq")(
                x, norms['attn_norm'][layer], slots['mla_w_dq'][j], wb['mla_w_dq'][j])
            q = _make_norm_linear(f"mla{j}_uq")(
                c_q, norms['mla_q_lat_norm'][j], slots['mla_w_uq'][j], wb['mla_w_uq'][j])
            q = _make_rope(f"rope_q{j}")(q, *q_tabs)
            q = q.reshape(s, MLA_HEADS, QK).transpose(1, 0, 2)
            o = mla_attention(q, kcat, vh)
            o = o.transpose(1, 0, 2).reshape(s, MLA_HEADS * VDIM)
            x = _make_linear_res(f"mla{j}_o")(x, o, slots['mla_w_o'][j], wb['mla_w_o'][j])
        x = _make_mlp_res(f"mlp{layer}")(
            x, norms['mlp_norm'][layer], slots['mlp_w1'][layer], slots['mlp_w2'][layer],
            wb['mlp_w1'][layer], wb['mlp_w2'][layer])
    return x


def _merge_blocks(gathered, ax):
    shp = gathered.shape[1:]
    return jnp.moveaxis(gathered, 0, ax).reshape(shp[:ax] + (N_DEV * shp[ax],) + shp[ax + 1:])


def _split_blocks(full, ax):
    shp = full.shape
    return jnp.moveaxis(full.reshape(shp[:ax] + (N_DEV, shp[ax] // N_DEV) + shp[ax + 1:]), ax, 0)


def _pack(parts):
    flat = jnp.concatenate([p.reshape(-1) for p in parts])
    rows = -(-flat.shape[0] // PACK_COLS)
    rows = -(-rows // PACK_ROW_ALIGN) * PACK_ROW_ALIGN
    return jnp.pad(flat, (0, rows * PACK_COLS - flat.shape[0])).reshape(rows, PACK_COLS)


def _unpack(packed, shapes):
    flat = packed.reshape(-1)
    out, off = [], 0
    for shp in shapes:
        n = math.prod(shp)
        out.append(flat[off:off + n].reshape(shp))
        off += n
    return out


def _add_round(x, stage, core, name):
    _, r, c = x.shape
    tr = _tile(r, (256, 128, 64, 32, 16, 8))

    def body(core_ref, x_ref, s_ref, o_ref):
        o_ref[...] = (x_ref[0] + s_ref[...]).astype(o_ref.dtype)

    return pl.pallas_call(
        body, name=name,
        grid_spec=pltpu.PrefetchScalarGridSpec(
            num_scalar_prefetch=1, grid=(r // tr,),
            in_specs=[pl.BlockSpec((1, tr, c), lambda i, core_ref: (core_ref[0], i, 0)),
                      pl.BlockSpec((tr, c), lambda i, core_ref: (i, 0))],
            out_specs=pl.BlockSpec((tr, c), lambda i, core_ref: (i, 0))),
        out_shape=jax.ShapeDtypeStruct((r, c), BF16),
        compiler_params=pltpu.CompilerParams(dimension_semantics=("parallel",)),
    )(core, x, stage)


def _adamw_reduce(parts, w, m, v, name):
    n_parts, r, c = parts.shape
    tr = _tile(r, (128, 64, 32, 16, 8))
    bias1 = 1.0 - ADAM_B1 ** ADAM_STEP
    bias2 = 1.0 - ADAM_B2 ** ADAM_STEP

    def body(p_ref, w_ref, m_ref, v_ref, g_ref, d_ref, nm_ref, nv_ref):
        g = p_ref[0].astype(F32)
        for s in range(1, n_parts):
            g = g + p_ref[s].astype(F32)
        mn = ADAM_B1 * m_ref[...] + (1.0 - ADAM_B1) * g
        vn = ADAM_B2 * v_ref[...] + (1.0 - ADAM_B2) * (g * g)
        m_hat = mn / bias1
        v_hat = vn / bias2
        g_ref[...] = g
        d_ref[...] = -ADAM_LR * (m_hat / (jnp.sqrt(v_hat) + ADAM_EPS) + ADAM_WD * w_ref[...])
        nm_ref[...] = mn
        nv_ref[...] = vn

    blk = pl.BlockSpec((tr, c), lambda i: (i, 0))
    return pl.pallas_call(
        body, name=name, grid=(r // tr,),
        in_specs=[pl.BlockSpec((n_parts, tr, c), lambda i: (0, i, 0)), blk, blk, blk],
        out_specs=[blk] * 4,
        out_shape=[jax.ShapeDtypeStruct((r, c), F32)] * 4,
        compiler_params=pltpu.CompilerParams(dimension_semantics=("parallel",), vmem_limit_bytes=VMEM_MM),
    )(parts, w, m, v)


def kernel(x, positions, attn_norm, mlp_norm, sb_w_qkv, sb_w_o, kv_norm, mla_w_dkv, mla_kv_lat_norm, mla_w_ukv, mla_w_dq, mla_q_lat_norm, mla_w_uq, mla_w_o, mlp_w1, mlp_w2, final_norm, loss_target, m_attn_norm, m_mlp_norm, m_sb_w_qkv, m_sb_w_o, m_kv_norm, m_mla_w_dkv, m_mla_kv_lat_norm, m_mla_w_ukv, m_mla_w_dq, m_mla_q_lat_norm, m_mla_w_uq, m_mla_w_o, m_mlp_w1, m_mlp_w2, m_final_norm, v_attn_norm, v_mlp_norm, v_sb_w_qkv, v_sb_w_o, v_kv_norm, v_mla_w_dkv, v_mla_kv_lat_norm, v_mla_w_ukv, v_mla_w_dq, v_mla_q_lat_norm, v_mla_w_uq, v_mla_w_o, v_mlp_w1, v_mlp_w2, v_final_norm):
    weights = dict(attn_norm=attn_norm, mlp_norm=mlp_norm, sb_w_qkv=sb_w_qkv, sb_w_o=sb_w_o, kv_norm=kv_norm,
                   mla_w_dkv=mla_w_dkv, mla_kv_lat_norm=mla_kv_lat_norm, mla_w_ukv=mla_w_ukv, mla_w_dq=mla_w_dq,
                   mla_q_lat_norm=mla_q_lat_norm, mla_w_uq=mla_w_uq, mla_w_o=mla_w_o, mlp_w1=mlp_w1, mlp_w2=mlp_w2,
                   final_norm=final_norm)
    mom_m = dict(attn_norm=m_attn_norm, mlp_norm=m_mlp_norm, sb_w_qkv=m_sb_w_qkv, sb_w_o=m_sb_w_o, kv_norm=m_kv_norm,
                 mla_w_dkv=m_mla_w_dkv, mla_kv_lat_norm=m_mla_kv_lat_norm, mla_w_ukv=m_mla_w_ukv, mla_w_dq=m_mla_w_dq,
                 mla_q_lat_norm=m_mla_q_lat_norm, mla_w_uq=m_mla_w_uq, mla_w_o=m_mla_w_o, mlp_w1=m_mlp_w1,
                 mlp_w2=m_mlp_w2, final_norm=m_final_norm)
    mom_v = dict(attn_norm=v_attn_norm, mlp_norm=v_mlp_norm, sb_w_qkv=v_sb_w_qkv, sb_w_o=v_sb_w_o, kv_norm=v_kv_norm,
                 mla_w_dkv=v_mla_w_dkv, mla_kv_lat_norm=v_mla_kv_lat_norm, mla_w_ukv=v_mla_w_ukv, mla_w_dq=v_mla_w_dq,
                 mla_q_lat_norm=v_mla_q_lat_norm, mla_w_uq=v_mla_w_uq, mla_w_o=v_mla_w_o, mlp_w1=v_mlp_w1,
                 mlp_w2=v_mlp_w2, final_norm=v_final_norm)
    sharded_names = [n for n, _ in SHARDED]
    repl_shapes = [tuple(weights[n].shape) for n in REPLICATED]

    gathered = _all_gather([weights[n].astype(BF16) for n in sharded_names], "gather_weights")
    wb, slots = {}, {}
    for (n, ax), g in zip(SHARDED, gathered):
        full = _merge_blocks(g, ax)
        layers = [full[l] for l in range(full.shape[0])] if full.ndim == 3 else [full]
        wb[n] = layers
        slots[n] = [jnp.zeros(w.shape, F32) for w in layers]
    norms = {n: weights[n] for n in REPLICATED if n != 'final_norm'}

    q_tabs, kv_tabs = _rope_tables(positions[0])
    x_last, pullback = jax.vjp(lambda xx, ss, nn: _trunk(xx, ss, nn, wb, q_tabs, kv_tabs), x[0], slots, norms)
    loss_part, dx_last, d_final = _loss_head(x_last, final_norm, loss_target[0])
    dx, d_slots, d_norms = pullback(dx_last)
    d_norms = dict(d_norms)
    d_norms['final_norm'] = d_final
    loss = lax.psum(loss_part[0, 0], ("x", "y", "c"))

    core = lax.axis_index("c").astype(jnp.int32).reshape(1)
    halves, dims = [], []
    for n, ax in SHARDED:
        full = jnp.stack(d_slots[n]) if weights[n].ndim == 3 else d_slots[n][0]
        blocks = _split_blocks(full, ax)
        two_d = (math.prod(blocks.shape[1:-1]), blocks.shape[-1])
        dims.append(two_d)
        halves.append(jnp.moveaxis(blocks.reshape((N_DEV // 2, 2) + two_d), 1, 0))
    staged = _pair_exchange(halves, "pair_grads")
    sums = []
    for n, h, st, (r, c) in zip(sharded_names, halves, staged, dims):
        rows = (N_DEV // 2) * r
        sums.append(_add_round(h.reshape(2, rows, c), st.reshape(rows, c), core, "pair_sum_" + n).reshape(st.shape))
    received = _chip_exchange(sums, "scatter_grads")
    repl_parts = _all_gather([_pack([d_norms[n] for n in REPLICATED])], "gather_norm_grads")[0]

    results = {kind: {} for kind in ("grad", "delta", "new_m", "new_v")}
    for n, parts, two_d in zip(sharded_names, received, dims):
        shp = weights[n].shape
        res = _adamw_reduce(parts, weights[n].reshape(two_d), mom_m[n].reshape(two_d),
                            mom_v[n].reshape(two_d), "adamw_" + n)
        for kind, a in zip(results, res):
            results[kind][n] = a.reshape(shp)
    res = _adamw_reduce(repl_parts, _pack([weights[n] for n in REPLICATED]), _pack([mom_m[n] for n in REPLICATED]),
                        _pack([mom_v[n] for n in REPLICATED]), "adamw_replicated")
    for kind, a in zip(results, res):
        results[kind].update(zip(REPLICATED, _unpack(a, repl_shapes)))

    out = [loss, dx[None]]
    for kind in ("grad", "delta", "new_m", "new_v"):
        out += [results[kind][n] for n in WEIGHT_ORDER]
    return tuple(out)
```

```python
import math

import jax
import jax.numpy as jnp
from jax import lax
from jax.experimental import pallas as pl
from jax.experimental.pallas import tpu as pltpu

F32 = jnp.float32
BF16 = jnp.bfloat16
MESH = pl.DeviceIdType.MESH

N_DEV = 8
DEPTH = 4
N_A = 2
SB_HEADS = 16
SB_HD = 64
MLA_HEADS = 16
NOPE = 64
ROPE = 32
VDIM = 64
QK = NOPE + ROPE
KV_RANK = 256
DKV_PAD = 384
LANES = 128
CHUNK_SHIFT = 6
ROPE_THETA = 10000.0
EPS = 1e-6
SB_BLOCK = 256
MLA_BLOCK = 512
PACK_COLS = 1024
PACK_ROW_ALIGN = 16
EXP_ZERO = -104.0
NEG_BIG = -1e30
LOG2_E = 1.4426950408889634
LN_2 = 0.6931471805599453
VMEM_ATT = 56 * 1024 * 1024
VMEM_MM = 48 * 1024 * 1024

ADAM_LR = 0.001
ADAM_B1 = 0.9
ADAM_B2 = 0.999
ADAM_EPS = 1e-08
ADAM_WD = 0.01
ADAM_STEP = 10

WEIGHT_ORDER = ['attn_norm', 'mlp_norm', 'sb_w_qkv', 'sb_w_o', 'kv_norm', 'mla_w_dkv', 'mla_kv_lat_norm',
                'mla_w_ukv', 'mla_w_dq', 'mla_q_lat_norm', 'mla_w_uq', 'mla_w_o', 'mlp_w1', 'mlp_w2', 'final_norm']
SHARDED = [('sb_w_qkv', 2), ('sb_w_o', 1), ('mla_w_dkv', 0), ('mla_w_ukv', 1), ('mla_w_dq', 1),
           ('mla_w_uq', 2), ('mla_w_o', 1), ('mlp_w1', 2), ('mlp_w2', 1)]
REPLICATED = ['attn_norm', 'mlp_norm', 'kv_norm', 'mla_kv_lat_norm', 'mla_q_lat_norm', 'final_norm']


def _tile(dim, prefs=(512, 384, 256, 128, 64, 32, 16, 8)):
    for t in prefs:
        if dim % t == 0:
            return t
    return dim


def _dot(a, b, dims):
    return lax.dot_general(a, b, (dims, ((), ())), preferred_element_type=F32)


NN = ((1,), (0,))
NT = ((1,), (1,))
TN = ((0,), (0,))


def _all_gather(shards, name):
    n_t = len(shards)

    def body(*refs):
        x_refs, out_refs = refs[:n_t], refs[n_t:2 * n_t]
        send_sems, recv_sems, local_sems = refs[2 * n_t:]
        x, y, c = lax.axis_index("x"), lax.axis_index("y"), lax.axis_index("c")
        me, sibling = (x, y, c), (x, y, 1 - c)
        chips = [(1 - x, y), (x, 1 - y), (1 - x, 1 - y)]

        def slot(t, px, py, pc):
            return out_refs[t].at[4 * px + 2 * py + pc]

        def copy(t, k, block, to, src=None):
            return pltpu.make_async_remote_copy(
                src_ref=slot(t, *block) if src is None else src, dst_ref=slot(t, *block),
                send_sem=send_sems.at[7 * t + k], recv_sem=recv_sems.at[7 * t + k],
                device_id=to, device_id_type=MESH)

        mine = [pltpu.make_async_copy(x_refs[t], slot(t, *me), local_sems.at[t]) for t in range(n_t)]
        for cp in mine:
            cp.start()
        first = []
        for t in range(n_t):
            first.append(copy(t, 0, me, sibling, src=x_refs[t]))
            first += [copy(t, 1 + j, me, (*chip, c), src=x_refs[t]) for j, chip in enumerate(chips)]
        for cp in first:
            cp.start()
        passed = []
        for t in range(n_t):
            for j, chip in enumerate(chips):
                copy(t, 1 + j, (*chip, c), me).wait_recv()
                onward = copy(t, 4 + j, (*chip, c), sibling)
                onward.start()
                passed.append(onward)
        for t in range(n_t):
            copy(t, 0, sibling, me).wait_recv()
            for j, chip in enumerate(chips):
                copy(t, 4 + j, (*chip, 1 - c), me).wait_recv()
        for cp in first + passed:
            cp.wait_send()
        for cp in mine:
            cp.wait()

    any_spec = pl.BlockSpec(memory_space=pl.ANY)
    return pl.pallas_call(
        body, name=name,
        out_shape=[jax.ShapeDtypeStruct((N_DEV,) + tuple(s.shape), s.dtype) for s in shards],
        in_specs=[any_spec] * n_t, out_specs=[any_spec] * n_t,
        scratch_shapes=[pltpu.SemaphoreType.DMA((7 * n_t,)), pltpu.SemaphoreType.DMA((7 * n_t,)),
                        pltpu.SemaphoreType.DMA((n_t,))],
    )(*shards)


def _pair_exchange(xs, name):
    n_t = len(xs)

    def body(*refs):
        x_refs, out_refs = refs[:n_t], refs[n_t:2 * n_t]
        send_sems, recv_sems = refs[2 * n_t:]
        x, y, c = lax.axis_index("x"), lax.axis_index("y"), lax.axis_index("c")
        copies = [pltpu.make_async_remote_copy(
            src_ref=x_refs[t].at[1 - c], dst_ref=out_refs[t], send_sem=send_sems.at[t], recv_sem=recv_sems.at[t],
            device_id=(x, y, 1 - c), device_id_type=MESH) for t in range(n_t)]
        for cp in copies:
            cp.start()
        for cp in copies:
            cp.wait()

    any_spec = pl.BlockSpec(memory_space=pl.ANY)
    return pl.pallas_call(
        body, name=name,
        out_shape=[jax.ShapeDtypeStruct(x.shape[1:], x.dtype) for x in xs],
        in_specs=[any_spec] * n_t, out_specs=[any_spec] * n_t,
        scratch_shapes=[pltpu.SemaphoreType.DMA((n_t,)), pltpu.SemaphoreType.DMA((n_t,))],
    )(*xs)


def _chip_exchange(ps, name):
    n_t = len(ps)

    def body(*refs):
        p_refs, out_refs = refs[:n_t], refs[n_t:2 * n_t]
        send_sems, recv_sems, local_sems = refs[2 * n_t:]
        mx, my, mc = lax.axis_index("x"), lax.axis_index("y"), lax.axis_index("c")
        me = 2 * mx + my
        mine = [pltpu.make_async_copy(p_refs[t].at[me], out_refs[t].at[me], local_sems.at[t]) for t in range(n_t)]
        for cp in mine:
            cp.start()
        copies = []
        for k in range(1, 4):
            px = 1 - mx if (k >> 1) & 1 else mx
            py = 1 - my if k & 1 else my
            peer = 2 * px + py
            for t in range(n_t):
                copies.append(pltpu.make_async_remote_copy(
                    src_ref=p_refs[t].at[peer], dst_ref=out_refs[t].at[me],
                    send_sem=send_sems.at[3 * t + k - 1], recv_sem=recv_sems.at[3 * t + k - 1],
                    device_id=(px, py, mc), device_id_type=MESH))
        for cp in copies:
            cp.start()
        for cp in copies:
            cp.wait_send()
        for cp in copies:
            cp.wait_recv()
        for cp in mine:
            cp.wait()

    any_spec = pl.BlockSpec(memory_space=pl.ANY)
    return pl.pallas_call(
        body, name=name,
        out_shape=[jax.ShapeDtypeStruct(p.shape, p.dtype) for p in ps],
        in_specs=[any_spec] * n_t, out_specs=[any_spec] * n_t,
        scratch_shapes=[pltpu.SemaphoreType.DMA((3 * n_t,)), pltpu.SemaphoreType.DMA((3 * n_t,)),
                        pltpu.SemaphoreType.DMA((n_t,))],
    )(*ps)


def _matmul(a, b, *, ta=False, tb=False, out_dtypes=(F32,), epilogue=None, extras=(), name):
    if ta:
        kdim, m = a.shape
    else:
        m, kdim = a.shape
    if tb:
        n, kb = b.shape
    else:
        kb, n = b.shape
    assert kdim == kb, (a.shape, b.shape, ta, tb)
    big = (1024, 768, 512, 384, 256, 128, 64, 32, 16, 8)
    tm, tn = _tile(m, big), _tile(n, big)
    tk = kdim if kdim <= 2048 else _tile(kdim, (1024, 512, 256, 128))
    nk = kdim // tk
    n_extra, n_out = len(extras), len(out_dtypes)
    a_spec = pl.BlockSpec((tk, tm), lambda i, j, k: (k, i)) if ta else pl.BlockSpec((tm, tk), lambda i, j, k: (i, k))
    b_spec = pl.BlockSpec((tn, tk), lambda i, j, k: (j, k)) if tb else pl.BlockSpec((tk, tn), lambda i, j, k: (k, j))
    tile_spec = pl.BlockSpec((tm, tn), lambda i, j, k: (i, j))
    dims = ((0,) if ta else (1,), (1,) if tb else (0,))

    def finish(acc, extra_refs, out_refs):
        outs = (acc,) if epilogue is None else epilogue(acc, *[r[...] for r in extra_refs])
        for o_ref, o in zip(out_refs, outs):
            o_ref[...] = o.astype(o_ref.dtype)

    def body_one(a_ref, b_ref, *rest):
        acc = _dot(a_ref[...].astype(BF16), b_ref[...].astype(BF16), dims)
        finish(acc, rest[:n_extra], rest[n_extra:n_extra + n_out])

    def body_acc(a_ref, b_ref, *rest):
        acc_ref = rest[-1]
        k = pl.program_id(2)

        @pl.when(k == 0)
        def _():
            acc_ref[...] = jnp.zeros_like(acc_ref)

        acc_ref[...] += _dot(a_ref[...].astype(BF16), b_ref[...].astype(BF16), dims)

        @pl.when(k == nk - 1)
        def _():
            finish(acc_ref[...], rest[:n_extra], rest[n_extra:n_extra + n_out])

    return pl.pallas_call(
        body_one if nk == 1 else body_acc, name=name, grid=(m // tm, n // tn, nk),
        in_specs=[a_spec, b_spec] + [tile_spec] * n_extra,
        out_specs=[tile_spec] * n_out,
        out_shape=[jax.ShapeDtypeStruct((m, n), dt) for dt in out_dtypes],
        scratch_shapes=[] if nk == 1 else [pltpu.VMEM((tm, tn), F32)],
        compiler_params=pltpu.CompilerParams(
            dimension_semantics=("parallel", "parallel", "arbitrary"), vmem_limit_bytes=VMEM_MM),
    )(a, b, *extras)


def _rms_fwd(x, g, name):
    m, d = x.shape
    tm = _tile(m, (512, 256, 128, 64, 32, 16, 8))

    def body(x_ref, g_ref, y_ref):
        xv = x_ref[...]
        r = lax.rsqrt(jnp.mean(xv * xv, axis=-1, keepdims=True) + EPS)
        y_ref[...] = (xv * r * g_ref[...]).astype(y_ref.dtype)

    return pl.pallas_call(
        body, name=name, grid=(m // tm,),
        in_specs=[pl.BlockSpec((tm, d), lambda i: (i, 0)), pl.BlockSpec((1, d), lambda i: (0, 0))],
        out_specs=pl.BlockSpec((tm, d), lambda i: (i, 0)),
        out_shape=jax.ShapeDtypeStruct((m, d), BF16),
        compiler_params=pltpu.CompilerParams(dimension_semantics=("parallel",)),
    )(x, g.reshape(1, d))


def _rms_bwd(x, g, dy, name, res=None):
    m, d = x.shape
    tm = _tile(m, (512, 256, 128, 64, 32, 16, 8))
    has_res = res is not None

    def body(x_ref, g_ref, dy_ref, *rest):
        dx_ref, dg_ref = rest[-2:]
        xv = x_ref[...]
        dyv = dy_ref[...]
        r = lax.rsqrt(jnp.mean(xv * xv, axis=-1, keepdims=True) + EPS)
        xh = xv * r
        t = dyv * g_ref[...]
        dx = r * (t - xh * jnp.mean(t * xh, axis=-1, keepdims=True))
        dx_ref[...] = dx + rest[0][...] if has_res else dx

        @pl.when(pl.program_id(0) == 0)
        def _():
            dg_ref[...] = jnp.zeros_like(dg_ref)

        dg_ref[...] += jnp.sum(dyv * xh, axis=0, keepdims=True)

    row_spec = pl.BlockSpec((tm, d), lambda i: (i, 0))
    vec_spec = pl.BlockSpec((1, d), lambda i: (0, 0))
    dx, dg = pl.pallas_call(
        body, name=name, grid=(m // tm,),
        in_specs=[row_spec, vec_spec, row_spec] + ([row_spec] if has_res else []),
        out_specs=[row_spec, vec_spec],
        out_shape=[jax.ShapeDtypeStruct((m, d), F32), jax.ShapeDtypeStruct((1, d), F32)],
        compiler_params=pltpu.CompilerParams(dimension_semantics=("arbitrary",)),
    )(x, g.reshape(1, d), dy, *((res,) if has_res else ()))
    return dx, dg.reshape(d)


def _loss_head(x, g, target):
    m, d = x.shape
    tm = _tile(m, (512, 256, 128, 64, 32, 16, 8))

    def body(x_ref, g_ref, t_ref, loss_ref, dx_ref, dg_ref):
        xv = x_ref[...]
        gv = g_ref[...]
        r = lax.rsqrt(jnp.mean(xv * xv, axis=-1, keepdims=True) + EPS)
        xh = xv * r
        err = xh * gv - t_ref[...]
        row_loss = jnp.mean(err * err, axis=-1, keepdims=True)
        dyv = err * (1.0 / d)
        t = dyv * gv
        dx_ref[...] = r * (t - xh * jnp.mean(t * xh, axis=-1, keepdims=True))

        @pl.when(pl.program_id(0) == 0)
        def _():
            dg_ref[...] = jnp.zeros_like(dg_ref)
            loss_ref[...] = jnp.zeros_like(loss_ref)

        dg_ref[...] += jnp.sum(dyv * xh, axis=0, keepdims=True)
        loss_ref[...] += 0.5 * jnp.sum(row_loss, axis=0, keepdims=True)

    loss, dx, dg = pl.pallas_call(
        body, name="loss_head", grid=(m // tm,),
        in_specs=[pl.BlockSpec((tm, d), lambda i: (i, 0)), pl.BlockSpec((1, d), lambda i: (0, 0)),
                  pl.BlockSpec((tm, d), lambda i: (i, 0))],
        out_specs=[pl.BlockSpec((1, 1), lambda i: (0, 0)), pl.BlockSpec((tm, d), lambda i: (i, 0)),
                   pl.BlockSpec((1, d), lambda i: (0, 0))],
        out_shape=[jax.ShapeDtypeStruct((1, 1), F32), jax.ShapeDtypeStruct((m, d), F32),
                   jax.ShapeDtypeStruct((1, d), F32)],
        compiler_params=pltpu.CompilerParams(dimension_semantics=("arbitrary",)),
    )(x, g.reshape(1, d), target)
    return loss, dx, dg.reshape(d)


def _rope_apply(x, tabs, transpose, name):
    m, w = x.shape
    wt = tabs[0].shape[1]
    reps = w // wt
    half = ROPE // 2
    tm = _tile(m, (256, 128, 64, 32, 16, 8))

    def body(x_ref, c_ref, a_ref, b_ref, y_ref):
        xv = x_ref[...]

        def wide(t_ref):
            t = t_ref[...]
            return t if reps == 1 else jnp.concatenate([t] * reps, axis=1)

        c, a, b = wide(c_ref), wide(a_ref), wide(b_ref)
        if transpose:
            y = xv * c + pltpu.roll(xv * a, half, 1) + pltpu.roll(xv * b, w - half, 1)
        else:
            y = xv * c + pltpu.roll(xv, w - half, 1) * a + pltpu.roll(xv, half, 1) * b
        y_ref[...] = y

    x_spec = pl.BlockSpec((tm, w), lambda i: (i, 0))
    t_spec = pl.BlockSpec((tm, wt), lambda i: (i, 0))
    return pl.pallas_call(
        body, name=name, grid=(m // tm,),
        in_specs=[x_spec, t_spec, t_spec, t_spec], out_specs=x_spec,
        out_shape=jax.ShapeDtypeStruct((m, w), F32),
        compiler_params=pltpu.CompilerParams(dimension_semantics=("parallel",)),
    )(x, *tabs)


def _log_sigmoid_pair(z):
    a = jnp.minimum(z, 0.0) - jnp.log(1.0 + jnp.exp(-jnp.abs(z)))
    return a, a - z


def _split_bf16(x):
    hi = x.astype(BF16)
    return hi, (x - hi.astype(F32)).astype(BF16)


def _wide(v, width):
    return v if width == LANES else jnp.concatenate([v] * (width // LANES), axis=1)


def _lanes(col):
    return jnp.broadcast_to(col, (col.shape[0], LANES))


def _sb_weights(qh, k2, valid):
    a, b = _log_sigmoid_pair(_dot(qh, k2, NT))
    return a, b if valid is None else jnp.where(valid, b, 0.0)


def _sb_fwd(qkv):
    s = qkv.shape[0]
    t = min(SB_BLOCK, s)
    nq = s // t
    npair = SB_HEADS // 2
    scale = SB_HD ** -0.5

    def body(q_ref, k_ref, v_ref, o_ref, bta_ref, btb_ref, js_ref, acc_ref, ra_ref, rb_ref):
        p, i = pl.program_id(0), pl.program_id(1)
        q2 = q_ref[...] * scale
        first = lax.broadcasted_iota(jnp.int32, (t, LANES), 1) < SB_HD
        heads = (jnp.where(first, q2, jnp.zeros_like(q2)), jnp.where(first, jnp.zeros_like(q2), q2))
        row = lax.broadcasted_iota(jnp.int32, (t, t), 0)
        col = lax.broadcasted_iota(jnp.int32, (t, t), 1)
        later = jnp.where(row > col, 1.0, 0.0).astype(BF16)
        acc_ref[...] = jnp.zeros_like(acc_ref)
        ra_ref[...] = jnp.zeros_like(ra_ref)
        rb_ref[...] = jnp.zeros_like(rb_ref)

        def block(j, diagonal):
            off = pl.multiple_of(j * t, t)
            k2 = k_ref[pl.ds(off, t), :]
            v2 = v_ref[pl.ds(off, t), :]
            valid = col < row if diagonal else None
            outs, rmax = [], None
            for qh, r_ref in zip(heads, (ra_ref, rb_ref)):
                a, b = _sb_weights(qh, k2, valid)
                bh, bl = _split_bf16(b)
                inner = _dot(bh, later, NN) + _dot(bl, later, NN)
                r = r_ref[...]
                w = jnp.exp(a + inner + _wide(r, t))
                if diagonal:
                    w = jnp.where(valid, w, 0.0)
                outs.append(_dot(w.astype(BF16), v2, NN))
                rn = r + jnp.sum(b, axis=1, keepdims=True)
                r_ref[...] = rn
                rmax = jnp.max(rn) if rmax is None else jnp.maximum(rmax, jnp.max(rn))
            acc_ref[...] += jnp.where(first, outs[0], outs[1])
            return rmax

        def cond(carry):
            j, rmax = carry
            return jnp.logical_and(j >= 0, rmax > EXP_ZERO)

        def step(carry):
            j, _ = carry
            return j - 1, block(j, False)

        jend, _ = lax.while_loop(cond, step, (i - 1, block(i, True)))
        o_ref[...] = acc_ref[...].astype(o_ref.dtype)
        bta_ref[0] = ra_ref[...][:, :1]
        btb_ref[0] = rb_ref[...][:, :1]
        js_ref[p, i] = (jend + 1).astype(F32)

    stat = pl.BlockSpec((1, t, 1), lambda p, i: (p, i, 0))
    return pl.pallas_call(
        body, name="sb_attn_fwd", grid=(npair, nq),
        in_specs=[pl.BlockSpec((t, LANES), lambda p, i: (i, p)),
                  pl.BlockSpec((s, LANES), lambda p, i: (0, npair + p)),
                  pl.BlockSpec((s, LANES), lambda p, i: (0, 2 * npair + p))],
        out_specs=[pl.BlockSpec((t, LANES), lambda p, i: (i, p)), stat, stat,
                   pl.BlockSpec(memory_space=pltpu.SMEM)],
        out_shape=[jax.ShapeDtypeStruct((s, SB_HEADS * SB_HD), BF16), jax.ShapeDtypeStruct((npair, s, 1), F32),
                   jax.ShapeDtypeStruct((npair, s, 1), F32), jax.ShapeDtypeStruct((npair, nq), F32)],
        scratch_shapes=[pltpu.VMEM((t, LANES), F32)] * 3,
        compiler_params=pltpu.CompilerParams(
            dimension_semantics=("arbitrary", "arbitrary"), vmem_limit_bytes=VMEM_ATT),
    )(qkv, qkv, qkv)


def _sb_bwd(qkv, do, btot_a, btot_b, jstart):
    s = qkv.shape[0]
    t = min(SB_BLOCK, s)
    nq = s // t
    npair = SB_HEADS // 2
    scale = SB_HD ** -0.5

    def body(js_ref, q_ref, k_ref, v_ref, do_ref, bta_ref, btb_ref, dq_ref, dk_ref, dv_ref,
             dq_acc, pa_ref, pb_ref, ga_ref, gb_ref):
        p, i = pl.program_id(0), pl.program_id(1)

        @pl.when(i == 0)
        def _():
            dk_ref[...] = jnp.zeros_like(dk_ref)
            dv_ref[...] = jnp.zeros_like(dv_ref)

        q2 = q_ref[...] * scale
        do2 = do_ref[...]
        first = lax.broadcasted_iota(jnp.int32, (t, LANES), 1) < SB_HD
        zero = jnp.zeros_like(q2)
        q_heads = (jnp.where(first, q2, zero), jnp.where(first, zero, q2))
        do_heads = (jnp.where(first, do2, zero), jnp.where(first, zero, do2))
        bts = (_wide(_lanes(bta_ref[0]), t), _wide(_lanes(btb_ref[0]), t))
        row = lax.broadcasted_iota(jnp.int32, (t, t), 0)
        col = lax.broadcasted_iota(jnp.int32, (t, t), 1)
        upto = jnp.where(row <= col, 1.0, 0.0).astype(BF16)
        before = jnp.where(row < col, 1.0, 0.0).astype(BF16)
        dq_acc[...] = jnp.zeros_like(dq_acc)
        for r in (pa_ref, pb_ref, ga_ref, gb_ref):
            r[...] = jnp.zeros_like(r)
        j0 = jnp.clip(js_ref[p, i].astype(jnp.int32), 0, i)

        def block(j, diagonal):
            off = pl.multiple_of(j * t, t)
            k2 = k_ref[pl.ds(off, t), :]
            v2 = v_ref[pl.ds(off, t), :]
            valid = col < row if diagonal else None
            dqs, dk2, dv2 = [], None, None
            for qh, doh, bt, p_ref, g_ref in zip(q_heads, do_heads, bts, (pa_ref, pb_ref), (ga_ref, gb_ref)):
                a, b = _sb_weights(qh, k2, valid)
                bh, bl = _split_bf16(b)
                pin = _dot(bh, upto, NN) + _dot(bl, upto, NN)
                surv = bt - (_wide(p_ref[...], t) + pin)
                w = jnp.exp(a + surv)
                if diagonal:
                    w = jnp.where(valid, w, 0.0)
                g = w * _dot(doh, v2, NT)
                gh, gl = _split_bf16(g)
                gsum = _wide(g_ref[...], t) + _dot(gh, before, NN) + _dot(gl, before, NN)
                beta = jnp.exp(a)
                dz = g * (1.0 - beta) - gsum * beta
                if diagonal:
                    dz = jnp.where(valid, dz, 0.0)
                dz = dz.astype(BF16)
                dqs.append(_dot(dz, k2, NN))
                dkh = _dot(dz, qh, TN)
                dvh = _dot(w.astype(BF16), doh, TN)
                dk2 = dkh if dk2 is None else dk2 + dkh
                dv2 = dvh if dv2 is None else dv2 + dvh
                p_ref[...] += jnp.sum(b, axis=1, keepdims=True)
                g_ref[...] += jnp.sum(g, axis=1, keepdims=True)
            dq_acc[...] += jnp.where(first, dqs[0], dqs[1])
            dk_ref[pl.ds(off, t), :] += dk2
            dv_ref[pl.ds(off, t), :] += dv2

        def step(j, carry):
            block(j, False)
            return carry

        lax.fori_loop(j0, i, step, 0)
        block(i, True)
        dq_ref[...] = dq_acc[...] * scale

    blk = pl.BlockSpec((t, LANES), lambda p, i: (i, p))
    full = pl.BlockSpec((s, LANES), lambda p, i: (0, p))
    stat = pl.BlockSpec((1, t, 1), lambda p, i: (p, i, 0))
    vec = pltpu.VMEM((t, LANES), F32)
    return pl.pallas_call(
        body, name="sb_attn_bwd", grid=(npair, nq),
        in_specs=[pl.BlockSpec(memory_space=pltpu.SMEM), blk,
                  pl.BlockSpec((s, LANES), lambda p, i: (0, npair + p)),
                  pl.BlockSpec((s, LANES), lambda p, i: (0, 2 * npair + p)), blk, stat, stat],
        out_specs=[blk, full, full],
        out_shape=[jax.ShapeDtypeStruct((s, SB_HEADS * SB_HD), F32)] * 3,
        scratch_shapes=[pltpu.VMEM((t, LANES), F32), vec, vec, vec, vec],
        compiler_params=pltpu.CompilerParams(
            dimension_semantics=("arbitrary", "arbitrary"), vmem_limit_bytes=VMEM_ATT),
    )(jstart, qkv, qkv, qkv, do, btot_a, btot_b)


@jax.custom_vjp
def sb_attention(qkv):
    return _sb_fwd(qkv)[0]


def _sb_attention_fwd(qkv):
    o, btot_a, btot_b, jstart = _sb_fwd(qkv)
    return o, (qkv, btot_a, btot_b, jstart)


def _sb_attention_bwd(res, do):
    qkv, btot_a, btot_b, jstart = res
    dq, dk, dv = _sb_bwd(qkv, do, btot_a, btot_b, jstart)
    return (jnp.concatenate([dq, dk, dv], axis=1).astype(BF16),)


sb_attention.defvjp(_sb_attention_fwd, _sb_attention_bwd)


def _chunk_allowed(row, col):
    return (col >> CHUNK_SHIFT) <= (row >> CHUNK_SHIFT)


def _mla_fwd(q, k, v_ext):
    h_, s, dqk = q.shape
    t = min(MLA_BLOCK, s)
    nq = s // t
    scale = dqk ** -0.5
    hp = 2

    def body(q_ref, k_ref, v_ref, o_ref, lse_ref, acc_ref, m_ref):
        i = pl.program_id(1)
        acc_ref[...] = jnp.zeros_like(acc_ref)
        m_ref[...] = jnp.full_like(m_ref, NEG_BIG)

        def tile(j, diagonal):
            off = pl.multiple_of(j * t, t)
            for hh in range(hp):
                kj = k_ref[hh, pl.ds(off, t), :]
                vj = v_ref[hh, pl.ds(off, t), :]
                sc = _dot(q_ref[hh], kj, NT) * (scale * LOG2_E)
                if diagonal:
                    row = lax.broadcasted_iota(jnp.int32, (t, t), 0)
                    col = lax.broadcasted_iota(jnp.int32, (t, t), 1)
                    sc = jnp.where(_chunk_allowed(row, col), sc, NEG_BIG)
                m_old = m_ref[hh]
                m_new = jnp.maximum(m_old, jnp.max(sc, axis=1, keepdims=True))
                p = jnp.exp2(sc - _wide(m_new, t))
                acc_ref[hh] = jnp.exp2(m_old - m_new) * acc_ref[hh] + _dot(p.astype(BF16), vj, NN)
                m_ref[hh] = m_new

        def step(j, carry):
            tile(j, False)
            return carry

        lax.fori_loop(0, i, step, 0)
        tile(i, True)
        for hh in range(hp):
            acc = acc_ref[hh]
            den = acc[:, VDIM:VDIM + 1]
            o_ref[hh] = acc[:, :VDIM] / den
            lse_ref[hh] = (m_ref[hh][:, :1] + jnp.log2(den)) * LN_2

    return pl.pallas_call(
        body, name="mla_attn_fwd", grid=(h_ // hp, nq),
        in_specs=[pl.BlockSpec((hp, t, dqk), lambda h, i: (h, i, 0)),
                  pl.BlockSpec((hp, s, dqk), lambda h, i: (h, 0, 0)),
                  pl.BlockSpec((hp, s, LANES), lambda h, i: (h, 0, 0))],
        out_specs=[pl.BlockSpec((hp, t, VDIM), lambda h, i: (h, i, 0)),
                   pl.BlockSpec((hp, t, 1), lambda h, i: (h, i, 0))],
        out_shape=[jax.ShapeDtypeStruct((h_, s, VDIM), F32), jax.ShapeDtypeStruct((h_, s, 1), F32)],
        scratch_shapes=[pltpu.VMEM((hp, t, LANES), F32), pltpu.VMEM((hp, t, LANES), F32)],
        compiler_params=pltpu.CompilerParams(
            dimension_semantics=("arbitrary", "arbitrary"), vmem_limit_bytes=VMEM_ATT),
    )(q, k, v_ext)


def _mla_bwd(q, k, v, o, lse, do):
    h_, s, dqk = q.shape
    dv = v.shape[2]
    t = min(MLA_BLOCK, s)
    nq = s // t
    scale = dqk ** -0.5

    def body(q_ref, k_ref, v_ref, o_ref, lse_ref, do_ref, dq_ref, dk_ref, dv_ref, dq_acc):
        i = pl.program_id(1)

        @pl.when(i == 0)
        def _():
            dk_ref[...] = jnp.zeros_like(dk_ref)
            dv_ref[...] = jnp.zeros_like(dv_ref)

        qv = q_ref[0]
        dof = do_ref[0]
        dov = dof.astype(BF16)
        delta = _lanes(jnp.sum(dof * o_ref[0], axis=1, keepdims=True))
        lse = _lanes(lse_ref[0])
        dq_acc[...] = jnp.zeros_like(dq_acc)

        def tile(j, diagonal):
            off = pl.multiple_of(j * t, t)
            kj = k_ref[0, pl.ds(off, t), :]
            vj = v_ref[0, pl.ds(off, t), :]
            p = jnp.exp(_dot(qv, kj, NT) * scale - _wide(lse, t))
            if diagonal:
                row = lax.broadcasted_iota(jnp.int32, (t, t), 0)
                col = lax.broadcasted_iota(jnp.int32, (t, t), 1)
                p = jnp.where(_chunk_allowed(row, col), p, 0.0)
            dp = _dot(dov, vj, NT)
            ds = (p * (dp - _wide(delta, t)) * scale).astype(BF16)
            dq_acc[...] += _dot(ds, kj, NN)
            dk_ref[0, pl.ds(off, t), :] += _dot(ds, qv, TN)
            dv_ref[0, pl.ds(off, t), :] += _dot(p.astype(BF16), dov, TN)

        def step(j, carry):
            tile(j, False)
            return carry

        lax.fori_loop(0, i, step, 0)
        tile(i, True)
        dq_ref[0] = dq_acc[...]

    qblk = pl.BlockSpec((1, t, dqk), lambda h, i: (h, i, 0))
    vblk = pl.BlockSpec((1, t, dv), lambda h, i: (h, i, 0))
    kfull = pl.BlockSpec((1, s, dqk), lambda h, i: (h, 0, 0))
    vfull = pl.BlockSpec((1, s, dv), lambda h, i: (h, 0, 0))
    stat = pl.BlockSpec((1, t, 1), lambda h, i: (h, i, 0))
    return pl.pallas_call(
        body, name="mla_attn_bwd", grid=(h_, nq),
        in_specs=[qblk, kfull, vfull, vblk, stat, vblk],
        out_specs=[qblk, kfull, vfull],
        out_shape=[jax.ShapeDtypeStruct((h_, s, dqk), F32), jax.ShapeDtypeStruct((h_, s, dqk), F32),
                   jax.ShapeDtypeStruct((h_, s, dv), F32)],
        scratch_shapes=[pltpu.VMEM((t, dqk), F32)],
        compiler_params=pltpu.CompilerParams(
            dimension_semantics=("arbitrary", "arbitrary"), vmem_limit_bytes=VMEM_ATT),
    )(q, k, v, o, lse, do)


def _with_ones(vb):
    h_, s, dv = vb.shape
    return jnp.concatenate([vb, jnp.ones((h_, s, 1), BF16), jnp.zeros((h_, s, LANES - dv - 1), BF16)], axis=-1)


@jax.custom_vjp
def mla_attention(q, k, v):
    return _mla_fwd(q.astype(BF16), k.astype(BF16), _with_ones(v.astype(BF16)))[0]


def _mla_attention_fwd(q, k, v):
    qb, kb, vb = q.astype(BF16), k.astype(BF16), v.astype(BF16)
    o, lse = _mla_fwd(qb, kb, _with_ones(vb))
    return o, (qb, kb, vb, o, lse)


def _mla_attention_bwd(res, do):
    qb, kb, vb, o, lse = res
    return tuple(_mla_bwd(qb, kb, vb, o, lse, do))


mla_attention.defvjp(_mla_attention_fwd, _mla_attention_bwd)


def _add_tile(acc, res):
    return (acc + res,)


def _make_linear_res(tag, a_dtype=F32):
    def forward(x, a, slot, wb):
        ab = a.astype(BF16)
        y = _matmul(ab, wb, epilogue=_add_tile, extras=(x,), name=tag + "_fwd")[0]
        return y, (ab, wb)

    @jax.custom_vjp
    def f(x, a, slot, wb):
        return forward(x, a, slot, wb)[0]

    def bwd(res, dy):
        ab, wb = res
        dyb = dy.astype(BF16)
        da = _matmul(dyb, wb, tb=True, out_dtypes=(a_dtype,), name=tag + "_da")[0]
        dw = _matmul(ab, dyb, ta=True, name=tag + "_dw")[0]
        return dy, da, dw, jnp.zeros_like(wb)

    f.defvjp(forward, bwd)
    return f


def _make_norm_linear(tag, out_dtype=F32):
    def forward(x, g, slot, wb):
        hb = _rms_fwd(x, g, tag + "_norm")
        y = _matmul(hb, wb, out_dtypes=(out_dtype,), name=tag + "_fwd")[0]
        return y, (x, g, wb, hb)

    @jax.custom_vjp
    def f(x, g, slot, wb):
        return forward(x, g, slot, wb)[0]

    def bwd(res, dy):
        x, g, wb, hb = res
        dyb = dy.astype(BF16)
        dh = _matmul(dyb, wb, tb=True, name=tag + "_dh")[0]
        dw = _matmul(hb, dyb, ta=True, name=tag + "_dw")[0]
        dx, dg = _rms_bwd(x, g, dh, tag + "_norm_bwd")
        return dx, dg, dw, jnp.zeros_like(wb)

    f.defvjp(forward, bwd)
    return f


def _relu2_fwd(acc):
    r = jnp.maximum(acc, 0.0)
    return acc, r * r


def _relu2_bwd(acc, u):
    return (acc * (2.0 * jnp.maximum(u, 0.0)),)


def _make_mlp_res(tag):
    def forward(x, g, slot1, slot2, w1b, w2b):
        hb = _rms_fwd(x, g, tag + "_norm")
        u, act = _matmul(hb, w1b, out_dtypes=(F32, BF16), epilogue=_relu2_fwd, name=tag + "_up")
        y = _matmul(act, w2b, epilogue=_add_tile, extras=(x,), name=tag + "_down")[0]
        return y, (x, g, w1b, w2b, hb, u, act)

    @jax.custom_vjp
    def f(x, g, slot1, slot2, w1b, w2b):
        return forward(x, g, slot1, slot2, w1b, w2b)[0]

    def bwd(res, dy):
        x, g, w1b, w2b, hb, u, act = res
        dyb = dy.astype(BF16)
        du = _matmul(dyb, w2b, tb=True, out_dtypes=(BF16,), epilogue=_relu2_bwd, extras=(u,), name=tag + "_du")[0]
        dw2 = _matmul(act, dyb, ta=True, name=tag + "_dw2")[0]
        dw1 = _matmul(hb, du, ta=True, name=tag + "_dw1")[0]
        dh = _matmul(du, w1b, tb=True, name=tag + "_dh")[0]
        dx, dg = _rms_bwd(x, g, dh, tag + "_norm_bwd", res=dy)
        return dx, dg, dw1, dw2, jnp.zeros_like(w1b), jnp.zeros_like(w2b)

    f.defvjp(forward, bwd)
    return f


def _make_rope(tag):
    @jax.custom_vjp
    def f(x, c, a, b):
        return _rope_apply(x, (c, a, b), False, tag + "_fwd")

    def fwd(x, c, a, b):
        return _rope_apply(x, (c, a, b), False, tag + "_fwd"), (c, a, b)

    def bwd(res, dy):
        c, a, b = res
        return _rope_apply(dy, (c, a, b), True, tag + "_bwd"), jnp.zeros_like(c), jnp.zeros_like(a), jnp.zeros_like(b)

    f.defvjp(fwd, bwd)
    return f


def _rope_tables(positions):
    half = ROPE // 2
    inv_freq = ROPE_THETA ** (-jnp.arange(0, ROPE, 2, dtype=F32) / ROPE)
    ang = positions.astype(F32)[:, None] * inv_freq
    cos, sin = jnp.cos(ang), jnp.sin(ang)
    s = positions.shape[0]
    one = lambda n: jnp.ones((s, n), F32)
    zero = lambda n: jnp.zeros((s, n), F32)
    reps = DKV_PAD // QK
    q_tabs = (jnp.tile(jnp.concatenate([one(NOPE), cos, cos], axis=1), (1, reps)),
              jnp.tile(jnp.concatenate([zero(NOPE), -sin, zero(half)], axis=1), (1, reps)),
              jnp.tile(jnp.concatenate([zero(NOPE + half), sin], axis=1), (1, reps)))
    tail = DKV_PAD - KV_RANK - ROPE
    kv_tabs = (jnp.concatenate([one(KV_RANK), cos, cos, one(tail)], axis=1),
               jnp.concatenate([zero(KV_RANK), -sin, zero(half + tail)], axis=1),
               jnp.concatenate([zero(KV_RANK + half), sin, zero(tail)], axis=1))
    return q_tabs, kv_tabs


def _trunk(x, slots, norms, wb, q_tabs, kv_tabs):
    s = x.shape[0]
    kcat = vh = None
    for layer in range(DEPTH):
        if layer < N_A:
            qkv = _make_norm_linear(f"sb{layer}_qkv", BF16)(
                x, norms['attn_norm'][layer], slots['sb_w_qkv'][layer], wb['sb_w_qkv'][layer])
            o = sb_attention(qkv)
            x = _make_linear_res(f"sb{layer}_o", BF16)(x, o, slots['sb_w_o'][layer], wb['sb_w_o'][layer])
        else:
            j = layer - N_A
            if j == 0:
                pad = ((0, 0), (0, DKV_PAD - KV_RANK - ROPE))
                down = _make_norm_linear("kv_down")(
                    x, norms['kv_norm'], jnp.pad(slots['mla_w_dkv'][0], pad), jnp.pad(wb['mla_w_dkv'][0], pad))
                kv = _make_norm_linear("kv_up")(
                    down[:, :KV_RANK], norms['mla_kv_lat_norm'], slots['mla_w_ukv'][0], wb['mla_w_ukv'][0])
                kv = kv.reshape(s, MLA_HEADS, NOPE + VDIM)
                k_rope = _make_rope("rope_k")(down, *kv_tabs)[:, KV_RANK:KV_RANK + ROPE]
                k_rope = jnp.broadcast_to(k_rope[:, None, :], (s, MLA_HEADS, ROPE))
                kcat = jnp.concatenate([kv[..., :NOPE], k_rope], axis=-1).transpose(1, 0, 2)
                vh = kv[..., NOPE:].transpose(1, 0, 2)
            c_q = _make_norm_linear(f"mla{j}_dq")(
                x, norms['attn_norm'][layer], slots['mla_w_dq'][j], wb['mla_w_dq'][j])
            q = _make_norm_linear(f"mla{j}_uq")(
                c_q, norms['mla_q_lat_norm'][j], slots['mla_w_uq'][j], wb['mla_w_uq'][j])
            q = _make_rope(f"rope_q{j}")(q, *q_tabs)
            q = q.reshape(s, MLA_HEADS, QK).transpose(1, 0, 2)
            o = mla_attention(q, kcat, vh)
            o = o.transpose(1, 0, 2).reshape(s, MLA_HEADS * VDIM)
            x = _make_linear_res(f"mla{j}_o")(x, o, slots['mla_w_o'][j], wb['mla_w_o'][j])
        x = _make_mlp_res(f"mlp{layer}")(
            x, norms['mlp_norm'][layer], slots['mlp_w1'][layer], slots['mlp_w2'][layer],
            wb['mlp_w1'][layer], wb['mlp_w2'][layer])
    return x


def _merge_blocks(gathered, ax):
    shp = gathered.shape[1:]
    return jnp.moveaxis(gathered, 0, ax).reshape(shp[:ax] + (N_DEV * shp[ax],) + shp[ax + 1:])


def _split_blocks(full, ax):
    shp = full.shape
    return jnp.moveaxis(full.reshape(shp[:ax] + (N_DEV, shp[ax] // N_DEV) + shp[ax + 1:]), ax, 0)


def _pack(parts):
    flat = jnp.concatenate([p.reshape(-1) for p in parts])
    rows = -(-flat.shape[0] // PACK_COLS)
    rows = -(-rows // PACK_ROW_ALIGN) * PACK_ROW_ALIGN
    return jnp.pad(flat, (0, rows * PACK_COLS - flat.shape[0])).reshape(rows, PACK_COLS)


def _unpack(packed, shapes):
    flat = packed.reshape(-1)
    out, off = [], 0
    for shp in shapes:
        n = math.prod(shp)
        out.append(flat[off:off + n].reshape(shp))
        off += n
    return out


def _add_round(x, stage, core, name):
    _, r, c = x.shape
    tr = _tile(r, (256, 128, 64, 32, 16, 8))

    def body(core_ref, x_ref, s_ref, o_ref):
        o_ref[...] = (x_ref[0] + s_ref[...]).astype(o_ref.dtype)

    return pl.pallas_call(
        body, name=name,
        grid_spec=pltpu.PrefetchScalarGridSpec(
            num_scalar_prefetch=1, grid=(r // tr,),
            in_specs=[pl.BlockSpec((1, tr, c), lambda i, core_ref: (core_ref[0], i, 0)),
                      pl.BlockSpec((tr, c), lambda i, core_ref: (i, 0))],
            out_specs=pl.BlockSpec((tr, c), lambda i, core_ref: (i, 0))),
        out_shape=jax.ShapeDtypeStruct((r, c), BF16),
        compiler_params=pltpu.CompilerParams(dimension_semantics=("parallel",)),
    )(core, x, stage)


def _adamw_reduce(parts, w, m, v, name):
    n_parts, r, c = parts.shape
    tr = _tile(r, (128, 64, 32, 16, 8))
    bias1 = 1.0 - ADAM_B1 ** ADAM_STEP
    bias2 = 1.0 - ADAM_B2 ** ADAM_STEP

    def body(p_ref, w_ref, m_ref, v_ref, g_ref, d_ref, nm_ref, nv_ref):
        g = p_ref[0].astype(F32)
        for s in range(1, n_parts):
            g = g + p_ref[s].astype(F32)
        mn = ADAM_B1 * m_ref[...] + (1.0 - ADAM_B1) * g
        vn = ADAM_B2 * v_ref[...] + (1.0 - ADAM_B2) * (g * g)
        m_hat = mn / bias1
        v_hat = vn / bias2
        g_ref[...] = g
        d_ref[...] = -ADAM_LR * (m_hat / (jnp.sqrt(v_hat) + ADAM_EPS) + ADAM_WD * w_ref[...])
        nm_ref[...] = mn
        nv_ref[...] = vn

    blk = pl.BlockSpec((tr, c), lambda i: (i, 0))
    return pl.pallas_call(
        body, name=name, grid=(r // tr,),
        in_specs=[pl.BlockSpec((n_parts, tr, c), lambda i: (0, i, 0)), blk, blk, blk],
        out_specs=[blk] * 4,
        out_shape=[jax.ShapeDtypeStruct((r, c), F32)] * 4,
        compiler_params=pltpu.CompilerParams(dimension_semantics=("parallel",), vmem_limit_bytes=VMEM_MM),
    )(parts, w, m, v)


def kernel(x, positions, attn_norm, mlp_norm, sb_w_qkv, sb_w_o, kv_norm, mla_w_dkv, mla_kv_lat_norm, mla_w_ukv, mla_w_dq, mla_q_lat_norm, mla_w_uq, mla_w_o, mlp_w1, mlp_w2, final_norm, loss_target, m_attn_norm, m_mlp_norm, m_sb_w_qkv, m_sb_w_o, m_kv_norm, m_mla_w_dkv, m_mla_kv_lat_norm, m_mla_w_ukv, m_mla_w_dq, m_mla_q_lat_norm, m_mla_w_uq, m_mla_w_o, m_mlp_w1, m_mlp_w2, m_final_norm, v_attn_norm, v_mlp_norm, v_sb_w_qkv, v_sb_w_o, v_kv_norm, v_mla_w_dkv, v_mla_kv_lat_norm, v_mla_w_ukv, v_mla_w_dq, v_mla_q_lat_norm, v_mla_w_uq, v_mla_w_o, v_mlp_w1, v_mlp_w2, v_final_norm):
    weights = dict(attn_norm=attn_norm, mlp_norm=mlp_norm, sb_w_qkv=sb_w_qkv, sb_w_o=sb_w_o, kv_norm=kv_norm,
                   mla_w_dkv=mla_w_dkv, mla_kv_lat_norm=mla_kv_lat_norm, mla_w_ukv=mla_w_ukv, mla_w_dq=mla_w_dq,
                   mla_q_lat_norm=mla_q_lat_norm, mla_w_uq=mla_w_uq, mla_w_o=mla_w_o, mlp_w1=mlp_w1, mlp_w2=mlp_w2,
                   final_norm=final_norm)
    mom_m = dict(attn_norm=m_attn_norm, mlp_norm=m_mlp_norm, sb_w_qkv=m_sb_w_qkv, sb_w_o=m_sb_w_o, kv_norm=m_kv_norm,
                 mla_w_dkv=m_mla_w_dkv, mla_kv_lat_norm=m_mla_kv_lat_norm, mla_w_ukv=m_mla_w_ukv, mla_w_dq=m_mla_w_dq,
                 mla_q_lat_norm=m_mla_q_lat_norm, mla_w_uq=m_mla_w_uq, mla_w_o=m_mla_w_o, mlp_w1=m_mlp_w1,
                 mlp_w2=m_mlp_w2, final_norm=m_final_norm)
    mom_v = dict(attn_norm=v_attn_norm, mlp_norm=v_mlp_norm, sb_w_qkv=v_sb_w_qkv, sb_w_o=v_sb_w_o, kv_norm=v_kv_norm,
                 mla_w_dkv=v_mla_w_dkv, mla_kv_lat_norm=v_mla_kv_lat_norm, mla_w_ukv=v_mla_w_ukv, mla_w_dq=v_mla_w_dq,
                 mla_q_lat_norm=v_mla_q_lat_norm, mla_w_uq=v_mla_w_uq, mla_w_o=v_mla_w_o, mlp_w1=v_mlp_w1,
                 mlp_w2=v_mlp_w2, final_norm=v_final_norm)
    sharded_names = [n for n, _ in SHARDED]
    repl_shapes = [tuple(weights[n].shape) for n in REPLICATED]

    gathered = _all_gather([weights[n].astype(BF16) for n in sharded_names], "gather_weights")
    wb, slots = {}, {}
    for (n, ax), g in zip(SHARDED, gathered):
        full = _merge_blocks(g, ax)
        layers = [full[l] for l in range(full.shape[0])] if full.ndim == 3 else [full]
        wb[n] = layers
        slots[n] = [jnp.zeros(w.shape, F32) for w in layers]
    norms = {n: weights[n] for n in REPLICATED if n != 'final_norm'}

    q_tabs, kv_tabs = _rope_tables(positions[0])
    x_last, pullback = jax.vjp(lambda xx, ss, nn: _trunk(xx, ss, nn, wb, q_tabs, kv_tabs), x[0], slots, norms)
    loss_part, dx_last, d_final = _loss_head(x_last, final_norm, loss_target[0])
    dx, d_slots, d_norms = pullback(dx_last)
    d_norms = dict(d_norms)
    d_norms['final_norm'] = d_final
    loss = lax.psum(loss_part[0, 0], ("x", "y", "c"))

    core = lax.axis_index("c").astype(jnp.int32).reshape(1)
    halves, dims = [], []
    for n, ax in SHARDED:
        full = jnp.stack(d_slots[n]) if weights[n].ndim == 3 else d_slots[n][0]
        blocks = _split_blocks(full, ax)
        two_d = (math.prod(blocks.shape[1:-1]), blocks.shape[-1])
        dims.append(two_d)
        halves.append(jnp.moveaxis(blocks.reshape((N_DEV // 2, 2) + two_d), 1, 0))
    staged = _pair_exchange(halves, "pair_grads")
    sums = []
    for n, h, st, (r, c) in zip(sharded_names, halves, staged, dims):
        rows = (N_DEV // 2) * r
        sums.append(_add_round(h.reshape(2, rows, c), st.reshape(rows, c), core, "pair_sum_" + n).reshape(st.shape))
    received = _chip_exchange(sums, "scatter_grads")
    repl_parts = _all_gather([_pack([d_norms[n] for n in REPLICATED])], "gather_norm_grads")[0]

    results = {kind: {} for kind in ("grad", "delta", "new_m", "new_v")}
    for n, parts, two_d in zip(sharded_names, received, dims):
        shp = weights[n].shape
        res = _adamw_reduce(parts, weights[n].reshape(two_d), mom_m[n].reshape(two_d),
                            mom_v[n].reshape(two_d), "adamw_" + n)
        for kind, a in zip(results, res):
            results[kind][n] = a.reshape(shp)
    res = _adamw_reduce(repl_parts, _pack([weights[n] for n in REPLICATED]), _pack([mom_m[n] for n in REPLICATED]),
                        _pack([mom_v[n] for n in REPLICATED]), "adamw_replicated")
    for kind, a in zip(results, res):
        results[kind].update(zip(REPLICATED, _unpack(a, repl_shapes)))

    out = [loss, dx[None]]
    for kind in ("grad", "delta", "new_m", "new_v"):
        out += [results[kind][n] for n in WEIGHT_ORDER]
    return tuple(out)
```

```python
import math

import jax
import jax.numpy as jnp
from jax import lax
from jax.experimental import pallas as pl
from jax.experimental.pallas import tpu as pltpu

F32 = jnp.float32
BF16 = jnp.bfloat16
MESH = pl.DeviceIdType.MESH

N_DEV = 8
DEPTH = 4
N_A = 2
SB_HEADS = 16
SB_HD = 64
MLA_HEADS = 16
NOPE = 64
ROPE = 32
VDIM = 64
QK = NOPE + ROPE
KV_RANK = 256
DKV_PAD = 384
LANES = 128
CHUNK_SHIFT = 6
ROPE_THETA = 10000.0
EPS = 1e-6
SB_BLOCK = 256
MLA_BLOCK = 512
PACK_COLS = 1024
PACK_ROW_ALIGN = 16
EXP_ZERO = -104.0
NEG_BIG = -1e30
LOG2_E = 1.4426950408889634
LN_2 = 0.6931471805599453
VMEM_ATT = 56 * 1024 * 1024
VMEM_MM = 48 * 1024 * 1024

ADAM_LR = 0.001
ADAM_B1 = 0.9
ADAM_B2 = 0.999
ADAM_EPS = 1e-08
ADAM_WD = 0.01
ADAM_STEP = 10

WEIGHT_ORDER = ['attn_norm', 'mlp_norm', 'sb_w_qkv', 'sb_w_o', 'kv_norm', 'mla_w_dkv', 'mla_kv_lat_norm',
                'mla_w_ukv', 'mla_w_dq', 'mla_q_lat_norm', 'mla_w_uq', 'mla_w_o', 'mlp_w1', 'mlp_w2', 'final_norm']
SHARDED = [('sb_w_qkv', 2), ('sb_w_o', 1), ('mla_w_dkv', 0), ('mla_w_ukv', 1), ('mla_w_dq', 1),
           ('mla_w_uq', 2), ('mla_w_o', 1), ('mlp_w1', 2), ('mlp_w2', 1)]
REPLICATED = ['attn_norm', 'mlp_norm', 'kv_norm', 'mla_kv_lat_norm', 'mla_q_lat_norm', 'final_norm']


def _tile(dim, prefs=(512, 384, 256, 128, 64, 32, 16, 8)):
    for t in prefs:
        if dim % t == 0:
            return t
    return dim


def _dot(a, b, dims):
    return lax.dot_general(a, b, (dims, ((), ())), preferred_element_type=F32)


NN = ((1,), (0,))
NT = ((1,), (1,))
TN = ((0,), (0,))


def _all_gather(shards, name):
    n_t = len(shards)

    def body(*refs):
        x_refs, out_refs = refs[:n_t], refs[n_t:2 * n_t]
        send_sems, recv_sems, local_sems = refs[2 * n_t:]
        x, y, c = lax.axis_index("x"), lax.axis_index("y"), lax.axis_index("c")
        me, sibling = (x, y, c), (x, y, 1 - c)
        chips = [(1 - x, y), (x, 1 - y), (1 - x, 1 - y)]

        def slot(t, px, py, pc):
            return out_refs[t].at[4 * px + 2 * py + pc]

        def copy(t, k, block, to, src=None):
            return pltpu.make_async_remote_copy(
                src_ref=slot(t, *block) if src is None else src, dst_ref=slot(t, *block),
                send_sem=send_sems.at[7 * t + k], recv_sem=recv_sems.at[7 * t + k],
                device_id=to, device_id_type=MESH)

        mine = [pltpu.make_async_copy(x_refs[t], slot(t, *me), local_sems.at[t]) for t in range(n_t)]
        for cp in mine:
            cp.start()
        first = []
        for t in range(n_t):
            first.append(copy(t, 0, me, sibling, src=x_refs[t]))
            first += [copy(t, 1 + j, me, (*chip, c), src=x_refs[t]) for j, chip in enumerate(chips)]
        for cp in first:
            cp.start()
        passed = []
        for t in range(n_t):
            for j, chip in enumerate(chips):
                copy(t, 1 + j, (*chip, c), me).wait_recv()
                onward = copy(t, 4 + j, (*chip, c), sibling)
                onward.start()
                passed.append(onward)
        for t in range(n_t):
            copy(t, 0, sibling, me).wait_recv()
            for j, chip in enumerate(chips):
                copy(t, 4 + j, (*chip, 1 - c), me).wait_recv()
        for cp in first + passed:
            cp.wait_send()
        for cp in mine:
            cp.wait()

    any_spec = pl.BlockSpec(memory_space=pl.ANY)
    return pl.pallas_call(
        body, name=name,
        out_shape=[jax.ShapeDtypeStruct((N_DEV,) + tuple(s.shape), s.dtype) for s in shards],
        in_specs=[any_spec] * n_t, out_specs=[any_spec] * n_t,
        scratch_shapes=[pltpu.SemaphoreType.DMA((7 * n_t,)), pltpu.SemaphoreType.DMA((7 * n_t,)),
                        pltpu.SemaphoreType.DMA((n_t,))],
    )(*shards)


def _pair_exchange(xs, name):
    n_t = len(xs)

    def body(*refs):
        x_refs, out_refs = refs[:n_t], refs[n_t:2 * n_t]
        send_sems, recv_sems = refs[2 * n_t:]
        x, y, c = lax.axis_index("x"), lax.axis_index("y"), lax.axis_index("c")
        copies = [pltpu.make_async_remote_copy(
            src_ref=x_refs[t].at[1 - c], dst_ref=out_refs[t], send_sem=send_sems.at[t], recv_sem=recv_sems.at[t],
            device_id=(x, y, 1 - c), device_id_type=MESH) for t in range(n_t)]
        for cp in copies:
            cp.start()
        for cp in copies:
            cp.wait()

    any_spec = pl.BlockSpec(memory_space=pl.ANY)
    return pl.pallas_call(
        body, name=name,
        out_shape=[jax.ShapeDtypeStruct(x.shape[1:], x.dtype) for x in xs],
        in_specs=[any_spec] * n_t, out_specs=[any_spec] * n_t,
        scratch_shapes=[pltpu.SemaphoreType.DMA((n_t,)), pltpu.SemaphoreType.DMA((n_t,))],
    )(*xs)


def _chip_exchange(ps, name):
    n_t = len(ps)

    def body(*refs):
        p_refs, out_refs = refs[:n_t], refs[n_t:2 * n_t]
        send_sems, recv_sems, local_sems = refs[2 * n_t:]
        mx, my, mc = lax.axis_index("x"), lax.axis_index("y"), lax.axis_index("c")
        me = 2 * mx + my
        mine = [pltpu.make_async_copy(p_refs[t].at[me], out_refs[t].at[me], local_sems.at[t]) for t in range(n_t)]
        for cp in mine:
            cp.start()
        copies = []
        for k in range(1, 4):
            px = 1 - mx if (k >> 1) & 1 else mx
            py = 1 - my if k & 1 else my
            peer = 2 * px + py
            for t in range(n_t):
                copies.append(pltpu.make_async_remote_copy(
                    src_ref=p_refs[t].at[peer], dst_ref=out_refs[t].at[me],
                    send_sem=send_sems.at[3 * t + k - 1], recv_sem=recv_sems.at[3 * t + k - 1],
                    device_id=(px, py, mc), device_id_type=MESH))
        for cp in copies:
            cp.start()
        for cp in copies:
            cp.wait_send()
        for cp in copies:
            cp.wait_recv()
        for cp in mine:
            cp.wait()

    any_spec = pl.BlockSpec(memory_space=pl.ANY)
    return pl.pallas_call(
        body, name=name,
        out_shape=[jax.ShapeDtypeStruct(p.shape, p.dtype) for p in ps],
        in_specs=[any_spec] * n_t, out_specs=[any_spec] * n_t,
        scratch_shapes=[pltpu.SemaphoreType.DMA((3 * n_t,)), pltpu.SemaphoreType.DMA((3 * n_t,)),
                        pltpu.SemaphoreType.DMA((n_t,))],
    )(*ps)


def _matmul(a, b, *, ta=False, tb=False, out_dtypes=(F32,), epilogue=None, extras=(), name):
    if ta:
        kdim, m = a.shape
    else:
        m, kdim = a.shape
    if tb:
        n, kb = b.shape
    else:
        kb, n = b.shape
    assert kdim == kb, (a.shape, b.shape, ta, tb)
    big = (1024, 768, 512, 384, 256, 128, 64, 32, 16, 8)
    tm, tn = _tile(m, big), _tile(n, big)
    tk = kdim if kdim <= 2048 else _tile(kdim, (1024, 512, 256, 128))
    nk = kdim // tk
    n_extra, n_out = len(extras), len(out_dtypes)
    a_spec = pl.BlockSpec((tk, tm), lambda i, j, k: (k, i)) if ta else pl.BlockSpec((tm, tk), lambda i, j, k: (i, k))
    b_spec = pl.BlockSpec((tn, tk), lambda i, j, k: (j, k)) if tb else pl.BlockSpec((tk, tn), lambda i, j, k: (k, j))
    tile_spec = pl.BlockSpec((tm, tn), lambda i, j, k: (i, j))
    dims = ((0,) if ta else (1,), (1,) if tb else (0,))

    def finish(acc, extra_refs, out_refs):
        outs = (acc,) if epilogue is None else epilogue(acc, *[r[...] for r in extra_refs])
        for o_ref, o in zip(out_refs, outs):
            o_ref[...] = o.astype(o_ref.dtype)

    def body_one(a_ref, b_ref, *rest):
        acc = _dot(a_ref[...].astype(BF16), b_ref[...].astype(BF16), dims)
        finish(acc, rest[:n_extra], rest[n_extra:n_extra + n_out])

    def body_acc(a_ref, b_ref, *rest):
        acc_ref = rest[-1]
        k = pl.program_id(2)

        @pl.when(k == 0)
        def _():
            acc_ref[...] = jnp.zeros_like(acc_ref)

        acc_ref[...] += _dot(a_ref[...].astype(BF16), b_ref[...].astype(BF16), dims)

        @pl.when(k == nk - 1)
        def _():
            finish(acc_ref[...], rest[:n_extra], rest[n_extra:n_extra + n_out])

    return pl.pallas_call(
        body_one if nk == 1 else body_acc, name=name, grid=(m // tm, n // tn, nk),
        in_specs=[a_spec, b_spec] + [tile_spec] * n_extra,
        out_specs=[tile_spec] * n_out,
        out_shape=[jax.ShapeDtypeStruct((m, n), dt) for dt in out_dtypes],
        scratch_shapes=[] if nk == 1 else [pltpu.VMEM((tm, tn), F32)],
        compiler_params=pltpu.CompilerParams(
            dimension_semantics=("parallel", "parallel", "arbitrary"), vmem_limit_bytes=VMEM_MM),
    )(a, b, *extras)


def _rms_fwd(x, g, name):
    m, d = x.shape
    tm = _tile(m, (512, 256, 128, 64, 32, 16, 8))

    def body(x_ref, g_ref, y_ref):
        xv = x_ref[...]
        r = lax.rsqrt(jnp.mean(xv * xv, axis=-1, keepdims=True) + EPS)
        y_ref[...] = (xv * r * g_ref[...]).astype(y_ref.dtype)

    return pl.pallas_call(
        body, name=name, grid=(m // tm,),
        in_specs=[pl.BlockSpec((tm, d), lambda i: (i, 0)), pl.BlockSpec((1, d), lambda i: (0, 0))],
        out_specs=pl.BlockSpec((tm, d), lambda i: (i, 0)),
        out_shape=jax.ShapeDtypeStruct((m, d), BF16),
        compiler_params=pltpu.CompilerParams(dimension_semantics=("parallel",)),
    )(x, g.reshape(1, d))


def _rms_bwd(x, g, dy, name, res=None):
    m, d = x.shape
    tm = _tile(m, (512, 256, 128, 64, 32, 16, 8))
    has_res = res is not None

    def body(x_ref, g_ref, dy_ref, *rest):
        dx_ref, dg_ref = rest[-2:]
        xv = x_ref[...]
        dyv = dy_ref[...]
        r = lax.rsqrt(jnp.mean(xv * xv, axis=-1, keepdims=True) + EPS)
        xh = xv * r
        t = dyv * g_ref[...]
        dx = r * (t - xh * jnp.mean(t * xh, axis=-1, keepdims=True))
        dx_ref[...] = dx + rest[0][...] if has_res else dx

        @pl.when(pl.program_id(0) == 0)
        def _():
            dg_ref[...] = jnp.zeros_like(dg_ref)

        dg_ref[...] += jnp.sum(dyv * xh, axis=0, keepdims=True)

    row_spec = pl.BlockSpec((tm, d), lambda i: (i, 0))
    vec_spec = pl.BlockSpec((1, d), lambda i: (0, 0))
    dx, dg = pl.pallas_call(
        body, name=name, grid=(m // tm,),
        in_specs=[row_spec, vec_spec, row_spec] + ([row_spec] if has_res else []),
        out_specs=[row_spec, vec_spec],
        out_shape=[jax.ShapeDtypeStruct((m, d), F32), jax.ShapeDtypeStruct((1, d), F32)],
        compiler_params=pltpu.CompilerParams(dimension_semantics=("arbitrary",)),
    )(x, g.reshape(1, d), dy, *((res,) if has_res else ()))
    return dx, dg.reshape(d)


def _loss_head(x, g, target):
    m, d = x.shape
    tm = _tile(m, (512, 256, 128, 64, 32, 16, 8))

    def body(x_ref, g_ref, t_ref, loss_ref, dx_ref, dg_ref):
        xv = x_ref[...]
        gv = g_ref[...]
        r = lax.rsqrt(jnp.mean(xv * xv, axis=-1, keepdims=True) + EPS)
        xh = xv * r
        err = xh * gv - t_ref[...]
        row_loss = jnp.mean(err * err, axis=-1, keepdims=True)
        dyv = err * (1.0 / d)
        t = dyv * gv
        dx_ref[...] = r * (t - xh * jnp.mean(t * xh, axis=-1, keepdims=True))

        @pl.when(pl.program_id(0) == 0)
        def _():
            dg_ref[...] = jnp.zeros_like(dg_ref)
            loss_ref[...] = jnp.zeros_like(loss_ref)

        dg_ref[...] += jnp.sum(dyv * xh, axis=0, keepdims=True)
        loss_ref[...] += 0.5 * jnp.sum(row_loss, axis=0, keepdims=True)

    loss, dx, dg = pl.pallas_call(
        body, name="loss_head", grid=(m // tm,),
        in_specs=[pl.BlockSpec((tm, d), lambda i: (i, 0)), pl.BlockSpec((1, d), lambda i: (0, 0)),
                  pl.BlockSpec((tm, d), lambda i: (i, 0))],
        out_specs=[pl.BlockSpec((1, 1), lambda i: (0, 0)), pl.BlockSpec((tm, d), lambda i: (i, 0)),
                   pl.BlockSpec((1, d), lambda i: (0, 0))],
        out_shape=[jax.ShapeDtypeStruct((1, 1), F32), jax.ShapeDtypeStruct((m, d), F32),
                   jax.ShapeDtypeStruct((1, d), F32)],
        compiler_params=pltpu.CompilerParams(dimension_semantics=("arbitrary",)),
    )(x, g.reshape(1, d), target)
    return loss, dx, dg.reshape(d)


def _rope_apply(x, tabs, transpose, name):
    m, w = x.shape
    wt = tabs[0].shape[1]
    reps = w // wt
    half = ROPE // 2
    tm = _tile(m, (256, 128, 64, 32, 16, 8))

    def body(x_ref, c_ref, a_ref, b_ref, y_ref):
        xv = x_ref[...]

        def wide(t_ref):
            t = t_ref[...]
            return t if reps == 1 else jnp.concatenate([t] * reps, axis=1)

        c, a, b = wide(c_ref), wide(a_ref), wide(b_ref)
        if transpose:
            y = xv * c + pltpu.roll(xv * a, half, 1) + pltpu.roll(xv * b, w - half, 1)
        else:
            y = xv * c + pltpu.roll(xv, w - half, 1) * a + pltpu.roll(xv, half, 1) * b
        y_ref[...] = y

    x_spec = pl.BlockSpec((tm, w), lambda i: (i, 0))
    t_spec = pl.BlockSpec((tm, wt), lambda i: (i, 0))
    return pl.pallas_call(
        body, name=name, grid=(m // tm,),
        in_specs=[x_spec, t_spec, t_spec, t_spec], out_specs=x_spec,
        out_shape=jax.ShapeDtypeStruct((m, w), F32),
        compiler_params=pltpu.CompilerParams(dimension_semantics=("parallel",)),
    )(x, *tabs)


def _log_sigmoid_pair(z):
    a = jnp.minimum(z, 0.0) - jnp.log(1.0 + jnp.exp(-jnp.abs(z)))
    return a, a - z


def _split_bf16(x):
    hi = x.astype(BF16)
    return hi, (x - hi.astype(F32)).astype(BF16)


def _wide(v, width):
    return v if width == LANES else jnp.concatenate([v] * (width // LANES), axis=1)


def _lanes(col):
    return jnp.broadcast_to(col, (col.shape[0], LANES))


def _sb_weights(qh, k2, valid):
    a, b = _log_sigmoid_pair(_dot(qh, k2, NT))
    return a, b if valid is None else jnp.where(valid, b, 0.0)


def _sb_fwd(qkv):
    s = qkv.shape[0]
    t = min(SB_BLOCK, s)
    nq = s // t
    npair = SB_HEADS // 2
    scale = SB_HD ** -0.5

    def body(q_ref, k_ref, v_ref, o_ref, bta_ref, btb_ref, js_ref, acc_ref, ra_ref, rb_ref):
        p, i = pl.program_id(0), pl.program_id(1)
        q2 = q_ref[...] * scale
        first = lax.broadcasted_iota(jnp.int32, (t, LANES), 1) < SB_HD
        heads = (jnp.where(first, q2, jnp.zeros_like(q2)), jnp.where(first, jnp.zeros_like(q2), q2))
        row = lax.broadcasted_iota(jnp.int32, (t, t), 0)
        col = lax.broadcasted_iota(jnp.int32, (t, t), 1)
        later = jnp.where(row > col, 1.0, 0.0).astype(BF16)
        acc_ref[...] = jnp.zeros_like(acc_ref)
        ra_ref[...] = jnp.zeros_like(ra_ref)
        rb_ref[...] = jnp.zeros_like(rb_ref)

        def block(j, diagonal):
            off = pl.multiple_of(j * t, t)
            k2 = k_ref[pl.ds(off, t), :]
            v2 = v_ref[pl.ds(off, t), :]
            valid = col < row if diagonal else None
            outs, rmax = [], None
            for qh, r_ref in zip(heads, (ra_ref, rb_ref)):
                a, b = _sb_weights(qh, k2, valid)
                bh, bl = _split_bf16(b)
                inner = _dot(bh, later, NN) + _dot(bl, later, NN)
                r = r_ref[...]
                w = jnp.exp(a + inner + _wide(r, t))
                if diagonal:
                    w = jnp.where(valid, w, 0.0)
                outs.append(_dot(w.astype(BF16), v2, NN))
                rn = r + jnp.sum(b, axis=1, keepdims=True)
                r_ref[...] = rn
                rmax = jnp.max(rn) if rmax is None else jnp.maximum(rmax, jnp.max(rn))
            acc_ref[...] += jnp.where(first, outs[0], outs[1])
            return rmax

        def cond(carry):
            j, rmax = carry
            return jnp.logical_and(j >= 0, rmax > EXP_ZERO)

        def step(carry):
            j, _ = carry
            return j - 1, block(j, False)

        jend, _ = lax.while_loop(cond, step, (i - 1, block(i, True)))
        o_ref[...] = acc_ref[...].astype(o_ref.dtype)
        bta_ref[0] = ra_ref[...][:, :1]
        btb_ref[0] = rb_ref[...][:, :1]
        js_ref[p, i] = (jend + 1).astype(F32)

    stat = pl.BlockSpec((1, t, 1), lambda p, i: (p, i, 0))
    return pl.pallas_call(
        body, name="sb_attn_fwd", grid=(npair, nq),
        in_specs=[pl.BlockSpec((t, LANES), lambda p, i: (i, p)),
                  pl.BlockSpec((s, LANES), lambda p, i: (0, npair + p)),
                  pl.BlockSpec((s, LANES), lambda p, i: (0, 2 * npair + p))],
        out_specs=[pl.BlockSpec((t, LANES), lambda p, i: (i, p)), stat, stat,
                   pl.BlockSpec(memory_space=pltpu.SMEM)],
        out_shape=[jax.ShapeDtypeStruct((s, SB_HEADS * SB_HD), BF16), jax.ShapeDtypeStruct((npair, s, 1), F32),
                   jax.ShapeDtypeStruct((npair, s, 1), F32), jax.ShapeDtypeStruct((npair, nq), F32)],
        scratch_shapes=[pltpu.VMEM((t, LANES), F32)] * 3,
        compiler_params=pltpu.CompilerParams(
            dimension_semantics=("arbitrary", "arbitrary"), vmem_limit_bytes=VMEM_ATT),
    )(qkv, qkv, qkv)


def _sb_bwd(qkv, do, btot_a, btot_b, jstart):
    s = qkv.shape[0]
    t = min(SB_BLOCK, s)
    nq = s // t
    npair = SB_HEADS // 2
    scale = SB_HD ** -0.5

    def body(js_ref, q_ref, k_ref, v_ref, do_ref, bta_ref, btb_ref, dq_ref, dk_ref, dv_ref,
             dq_acc, pa_ref, pb_ref, ga_ref, gb_ref, dk_acc, dv_acc):
        p, i = pl.program_id(0), pl.program_id(1)

        @pl.when(i == 0)
        def _():
            dk_acc[...] = jnp.zeros_like(dk_acc)
            dv_acc[...] = jnp.zeros_like(dv_acc)

        q2 = q_ref[...] * scale
        do2 = do_ref[...]
        first = lax.broadcasted_iota(jnp.int32, (t, LANES), 1) < SB_HD
        zero = jnp.zeros_like(q2)
        q_heads = (jnp.where(first, q2, zero), jnp.where(first, zero, q2))
        do_heads = (jnp.where(first, do2, zero), jnp.where(first, zero, do2))
        bts = (_wide(_lanes(bta_ref[0]), t), _wide(_lanes(btb_ref[0]), t))
        row = lax.broadcasted_iota(jnp.int32, (t, t), 0)
        col = lax.broadcasted_iota(jnp.int32, (t, t), 1)
        upto = jnp.where(row <= col, 1.0, 0.0).astype(BF16)
        before = jnp.where(row < col, 1.0, 0.0).astype(BF16)
        dq_acc[...] = jnp.zeros_like(dq_acc)
        for r in (pa_ref, pb_ref, ga_ref, gb_ref):
            r[...] = jnp.zeros_like(r)
        j0 = jnp.clip(js_ref[p, i].astype(jnp.int32), 0, i)

        def block(j, diagonal):
            off = pl.multiple_of(j * t, t)
            k2 = k_ref[pl.ds(off, t), :]
            v2 = v_ref[pl.ds(off, t), :]
            valid = col < row if diagonal else None
            dqs, dk2, dv2 = [], None, None
            for qh, doh, bt, p_ref, g_ref in zip(q_heads, do_heads, bts, (pa_ref, pb_ref), (ga_ref, gb_ref)):
                a, b = _sb_weights(qh, k2, valid)
                bh, bl = _split_bf16(b)
                pin = _dot(bh, upto, NN) + _dot(bl, upto, NN)
                surv = bt - (_wide(p_ref[...], t) + pin)
                w = jnp.exp(a + surv)
                if diagonal:
                    w = jnp.where(valid, w, 0.0)
                g = w * _dot(doh, v2, NT)
                gh, gl = _split_bf16(g)
                gsum = _wide(g_ref[...], t) + _dot(gh, before, NN) + _dot(gl, before, NN)
                beta = jnp.exp(a)
                dz = g * (1.0 - beta) - gsum * beta
                if diagonal:
                    dz = jnp.where(valid, dz, 0.0)
                dz = dz.astype(BF16)
                dqs.append(_dot(dz, k2, NN))
                dkh = _dot(dz, qh, TN)
                dvh = _dot(w.astype(BF16), doh, TN)
                dk2 = dkh if dk2 is None else dk2 + dkh
                dv2 = dvh if dv2 is None else dv2 + dvh
                p_ref[...] += jnp.sum(b, axis=1, keepdims=True)
                g_ref[...] += jnp.sum(g, axis=1, keepdims=True)
            dq_acc[...] += jnp.where(first, dqs[0], dqs[1])
            dk_acc[pl.ds(off, t), :] += dk2
            dv_acc[pl.ds(off, t), :] += dv2

        def step(j, carry):
            block(j, False)
            return carry

        lax.fori_loop(j0, i, step, 0)
        block(i, True)
        dq_ref[...] = (dq_acc[...] * scale).astype(dq_ref.dtype)

        @pl.when(i == nq - 1)
        def _():
            dk_ref[...] = dk_acc[...].astype(dk_ref.dtype)
            dv_ref[...] = dv_acc[...].astype(dv_ref.dtype)

    blk = pl.BlockSpec((t, LANES), lambda p, i: (i, p))
    full = pl.BlockSpec((s, LANES), lambda p, i: (0, p))
    stat = pl.BlockSpec((1, t, 1), lambda p, i: (p, i, 0))
    vec = pltpu.VMEM((t, LANES), F32)
    return pl.pallas_call(
        body, name="sb_attn_bwd", grid=(npair, nq),
        in_specs=[pl.BlockSpec(memory_space=pltpu.SMEM), blk,
                  pl.BlockSpec((s, LANES), lambda p, i: (0, npair + p)),
                  pl.BlockSpec((s, LANES), lambda p, i: (0, 2 * npair + p)), blk, stat, stat],
        out_specs=[blk, full, full],
        out_shape=[jax.ShapeDtypeStruct((s, SB_HEADS * SB_HD), BF16)] * 3,
        scratch_shapes=[pltpu.VMEM((t, LANES), F32), vec, vec, vec, vec,
                        pltpu.VMEM((s, LANES), F32), pltpu.VMEM((s, LANES), F32)],
        compiler_params=pltpu.CompilerParams(
            dimension_semantics=("arbitrary", "arbitrary"), vmem_limit_bytes=VMEM_ATT),
    )(jstart, qkv, qkv, qkv, do, btot_a, btot_b)


@jax.custom_vjp
def sb_attention(qkv):
    return _sb_fwd(qkv)[0]


def _sb_attention_fwd(qkv):
    o, btot_a, btot_b, jstart = _sb_fwd(qkv)
    return o, (qkv, btot_a, btot_b, jstart)


def _sb_attention_bwd(res, do):
    qkv, btot_a, btot_b, jstart = res
    dq, dk, dv = _sb_bwd(qkv, do, btot_a, btot_b, jstart)
    return (jnp.concatenate([dq, dk, dv], axis=1),)


sb_attention.defvjp(_sb_attention_fwd, _sb_attention_bwd)


def _chunk_allowed(row, col):
    return (col >> CHUNK_SHIFT) <= (row >> CHUNK_SHIFT)


def _mla_fwd(q, k, v_ext):
    h_, s, dqk = q.shape
    t = min(MLA_BLOCK, s)
    nq = s // t
    scale = dqk ** -0.5
    hp = 4

    def body(q_ref, k_ref, v_ref, o_ref, lse_ref, acc_ref, m_ref):
        i = pl.program_id(1)
        acc_ref[...] = jnp.zeros_like(acc_ref)
        m_ref[...] = jnp.full_like(m_ref, NEG_BIG)

        def tile(j, diagonal):
            off = pl.multiple_of(j * t, t)
            for hh in range(hp):
                kj = k_ref[hh, pl.ds(off, t), :]
                vj = v_ref[hh, pl.ds(off, t), :]
                sc = _dot(q_ref[hh], kj, NT) * (scale * LOG2_E)
                if diagonal:
                    row = lax.broadcasted_iota(jnp.int32, (t, t), 0)
                    col = lax.broadcasted_iota(jnp.int32, (t, t), 1)
                    sc = jnp.where(_chunk_allowed(row, col), sc, NEG_BIG)
                m_old = m_ref[hh]
                m_new = jnp.maximum(m_old, jnp.max(sc, axis=1, keepdims=True))
                p = jnp.exp2(sc - _wide(m_new, t))
                acc_ref[hh] = jnp.exp2(m_old - m_new) * acc_ref[hh] + _dot(p.astype(BF16), vj, NN)
                m_ref[hh] = m_new

        def step(j, carry):
            tile(j, False)
            return carry

        lax.fori_loop(0, i, step, 0)
        tile(i, True)
        for hh in range(hp):
            acc = acc_ref[hh]
            den = acc[:, VDIM:VDIM + 1]
            o_ref[hh] = acc[:, :VDIM] / den
            lse_ref[hh] = (m_ref[hh][:, :1] + jnp.log2(den)) * LN_2

    return pl.pallas_call(
        body, name="mla_attn_fwd", grid=(h_ // hp, nq),
        in_specs=[pl.BlockSpec((hp, t, dqk), lambda h, i: (h, i, 0)),
                  pl.BlockSpec((hp, s, dqk), lambda h, i: (h, 0, 0)),
                  pl.BlockSpec((hp, s, LANES), lambda h, i: (h, 0, 0))],
        out_specs=[pl.BlockSpec((hp, t, VDIM), lambda h, i: (h, i, 0)),
                   pl.BlockSpec((hp, t, 1), lambda h, i: (h, i, 0))],
        out_shape=[jax.ShapeDtypeStruct((h_, s, VDIM), F32), jax.ShapeDtypeStruct((h_, s, 1), F32)],
        scratch_shapes=[pltpu.VMEM((hp, t, LANES), F32), pltpu.VMEM((hp, t, LANES), F32)],
        compiler_params=pltpu.CompilerParams(
            dimension_semantics=("arbitrary", "arbitrary"), vmem_limit_bytes=VMEM_ATT),
    )(q, k, v_ext)


def _mla_bwd(q, k, v, o, lse, do):
    h_, s, dqk = q.shape
    dv = v.shape[2]
    t = min(MLA_BLOCK, s)
    nq = s // t
    scale = dqk ** -0.5

    def body(q_ref, k_ref, v_ref, o_ref, lse_ref, do_ref, dq_ref, dk_ref, dv_ref, dq_acc):
        i = pl.program_id(1)

        @pl.when(i == 0)
        def _():
            dk_ref[...] = jnp.zeros_like(dk_ref)
            dv_ref[...] = jnp.zeros_like(dv_ref)

        qv = q_ref[0]
        dof = do_ref[0]
        dov = dof.astype(BF16)
        delta = _lanes(jnp.sum(dof * o_ref[0], axis=1, keepdims=True))
        lse = _lanes(lse_ref[0])
        dq_acc[...] = jnp.zeros_like(dq_acc)

        def tile(j, diagonal):
            off = pl.multiple_of(j * t, t)
            kj = k_ref[0, pl.ds(off, t), :]
            vj = v_ref[0, pl.ds(off, t), :]
            p = jnp.exp(_dot(qv, kj, NT) * scale - _wide(lse, t))
            if diagonal:
                row = lax.broadcasted_iota(jnp.int32, (t, t), 0)
                col = lax.broadcasted_iota(jnp.int32, (t, t), 1)
                p = jnp.where(_chunk_allowed(row, col), p, 0.0)
            dp = _dot(dov, vj, NT)
            ds = (p * (dp - _wide(delta, t)) * scale).astype(BF16)
            dq_acc[...] += _dot(ds, kj, NN)
            dk_ref[0, pl.ds(off, t), :] += _dot(ds, qv, TN)
            dv_ref[0, pl.ds(off, t), :] += _dot(p.astype(BF16), dov, TN)

        def step(j, carry):
            tile(j, False)
            return carry

        lax.fori_loop(0, i, step, 0)
        tile(i, True)
        dq_ref[0] = dq_acc[...]

    qblk = pl.BlockSpec((1, t, dqk), lambda h, i: (h, i, 0))
    vblk = pl.BlockSpec((1, t, dv), lambda h, i: (h, i, 0))
    kfull = pl.BlockSpec((1, s, dqk), lambda h, i: (h, 0, 0))
    vfull = pl.BlockSpec((1, s, dv), lambda h, i: (h, 0, 0))
    stat = pl.BlockSpec((1, t, 1), lambda h, i: (h, i, 0))
    return pl.pallas_call(
        body, name="mla_attn_bwd", grid=(h_, nq),
        in_specs=[qblk, kfull, vfull, vblk, stat, vblk],
        out_specs=[qblk, kfull, vfull],
        out_shape=[jax.ShapeDtypeStruct((h_, s, dqk), F32), jax.ShapeDtypeStruct((h_, s, dqk), F32),
                   jax.ShapeDtypeStruct((h_, s, dv), F32)],
        scratch_shapes=[pltpu.VMEM((t, dqk), F32)],
        compiler_params=pltpu.CompilerParams(
            dimension_semantics=("arbitrary", "arbitrary"), vmem_limit_bytes=VMEM_ATT),
    )(q, k, v, o, lse, do)


def _with_ones(vb):
    h_, s, dv = vb.shape
    return jnp.concatenate([vb, jnp.ones((h_, s, 1), BF16), jnp.zeros((h_, s, LANES - dv - 1), BF16)], axis=-1)


@jax.custom_vjp
def mla_attention(q, k, v):
    return _mla_fwd(q.astype(BF16), k.astype(BF16), _with_ones(v.astype(BF16)))[0]


def _mla_attention_fwd(q, k, v):
    qb, kb, vb = q.astype(BF16), k.astype(BF16), v.astype(BF16)
    o, lse = _mla_fwd(qb, kb, _with_ones(vb))
    return o, (qb, kb, vb, o, lse)


def _mla_attention_bwd(res, do):
    qb, kb, vb, o, lse = res
    return tuple(_mla_bwd(qb, kb, vb, o, lse, do))


mla_attention.defvjp(_mla_attention_fwd, _mla_attention_bwd)


def _add_tile(acc, res):
    return (acc + res,)


def _make_linear_res(tag, a_dtype=F32):
    def forward(x, a, slot, wb):
        ab = a.astype(BF16)
        y = _matmul(ab, wb, epilogue=_add_tile, extras=(x,), name=tag + "_fwd")[0]
        return y, (ab, wb)

    @jax.custom_vjp
    def f(x, a, slot, wb):
        return forward(x, a, slot, wb)[0]

    def bwd(res, dy):
        ab, wb = res
        dyb = dy.astype(BF16)
        da = _matmul(dyb, wb, tb=True, out_dtypes=(a_dtype,), name=tag + "_da")[0]
        dw = _matmul(ab, dyb, ta=True, name=tag + "_dw")[0]
        return dy, da, dw, jnp.zeros_like(wb)

    f.defvjp(forward, bwd)
    return f


def _make_norm_linear(tag, out_dtype=F32):
    def forward(x, g, slot, wb):
        hb = _rms_fwd(x, g, tag + "_norm")
        y = _matmul(hb, wb, out_dtypes=(out_dtype,), name=tag + "_fwd")[0]
        return y, (x, g, wb, hb)

    @jax.custom_vjp
    def f(x, g, slot, wb):
        return forward(x, g, slot, wb)[0]

    def bwd(res, dy):
        x, g, wb, hb = res
        dyb = dy.astype(BF16)
        dh = _matmul(dyb, wb, tb=True, name=tag + "_dh")[0]
        dw = _matmul(hb, dyb, ta=True, name=tag + "_dw")[0]
        dx, dg = _rms_bwd(x, g, dh, tag + "_norm_bwd")
        return dx, dg, dw, jnp.zeros_like(wb)

    f.defvjp(forward, bwd)
    return f


def _relu2_fwd(acc):
    r = jnp.maximum(acc, 0.0)
    return acc, r * r


def _relu2_bwd(acc, u):
    return (acc * (2.0 * jnp.maximum(u.astype(F32), 0.0)),)


def _make_mlp_res(tag):
    def forward(x, g, slot1, slot2, w1b, w2b):
        hb = _rms_fwd(x, g, tag + "_norm")
        u, act = _matmul(hb, w1b, out_dtypes=(BF16, BF16), epilogue=_relu2_fwd, name=tag + "_up")
        y = _matmul(act, w2b, epilogue=_add_tile, extras=(x,), name=tag + "_down")[0]
        return y, (x, g, w1b, w2b, hb, u, act)

    @jax.custom_vjp
    def f(x, g, slot1, slot2, w1b, w2b):
        return forward(x, g, slot1, slot2, w1b, w2b)[0]

    def bwd(res, dy):
        x, g, w1b, w2b, hb, u, act = res
        dyb = dy.astype(BF16)
        du = _matmul(dyb, w2b, tb=True, out_dtypes=(BF16,), epilogue=_relu2_bwd, extras=(u,), name=tag + "_du")[0]
        dw2 = _matmul(act, dyb, ta=True, name=tag + "_dw2")[0]
        dw1 = _matmul(hb, du, ta=True, name=tag + "_dw1")[0]
        dh = _matmul(du, w1b, tb=True, name=tag + "_dh")[0]
        dx, dg = _rms_bwd(x, g, dh, tag + "_norm_bwd", res=dy)
        return dx, dg, dw1, dw2, jnp.zeros_like(w1b), jnp.zeros_like(w2b)

    f.defvjp(forward, bwd)
    return f


def _make_rope(tag):
    @jax.custom_vjp
    def f(x, c, a, b):
        return _rope_apply(x, (c, a, b), False, tag + "_fwd")

    def fwd(x, c, a, b):
        return _rope_apply(x, (c, a, b), False, tag + "_fwd"), (c, a, b)

    def bwd(res, dy):
        c, a, b = res
        return _rope_apply(dy, (c, a, b), True, tag + "_bwd"), jnp.zeros_like(c), jnp.zeros_like(a), jnp.zeros_like(b)

    f.defvjp(fwd, bwd)
    return f


def _rope_tables(positions):
    half = ROPE // 2
    inv_freq = ROPE_THETA ** (-jnp.arange(0, ROPE, 2, dtype=F32) / ROPE)
    ang = positions.astype(F32)[:, None] * inv_freq
    cos, sin = jnp.cos(ang), jnp.sin(ang)
    s = positions.shape[0]
    one = lambda n: jnp.ones((s, n), F32)
    zero = lambda n: jnp.zeros((s, n), F32)
    reps = DKV_PAD // QK
    q_tabs = (jnp.tile(jnp.concatenate([one(NOPE), cos, cos], axis=1), (1, reps)),
              jnp.tile(jnp.concatenate([zero(NOPE), -sin, zero(half)], axis=1), (1, reps)),
              jnp.tile(jnp.concatenate([zero(NOPE + half), sin], axis=1), (1, reps)))
    tail = DKV_PAD - KV_RANK - ROPE
    kv_tabs = (jnp.concatenate([one(KV_RANK), cos, cos, one(tail)], axis=1),
               jnp.concatenate([zero(KV_RANK), -sin, zero(half + tail)], axis=1),
               jnp.concatenate([zero(KV_RANK + half), sin, zero(tail)], axis=1))
    return q_tabs, kv_tabs


def _trunk(x, slots, norms, wb, q_tabs, kv_tabs):
    s = x.shape[0]
    kcat = vh = None
    for layer in range(DEPTH):
        if layer < N_A:
            qkv = _make_norm_linear(f"sb{layer}_qkv", BF16)(
                x, norms['attn_norm'][layer], slots['sb_w_qkv'][layer], wb['sb_w_qkv'][layer])
            o = sb_attention(qkv)
            x = _make_linear_res(f"sb{layer}_o", BF16)(x, o, slots['sb_w_o'][layer], wb['sb_w_o'][layer])
        else:
            j = layer - N_A
            if j == 0:
                pad = ((0, 0), (0, DKV_PAD - KV_RANK - ROPE))
                down = _make_norm_linear("kv_down")(
                    x, norms['kv_norm'], jnp.pad(slots['mla_w_dkv'][0], pad), jnp.pad(wb['mla_w_dkv'][0], pad))
                kv = _make_norm_linear("kv_up")(
                    down[:, :KV_RANK], norms['mla_kv_lat_norm'], slots['mla_w_ukv'][0], wb['mla_w_ukv'][0])
                kv = kv.reshape(s, MLA_HEADS, NOPE + VDIM)
                k_rope = _make_rope("rope_k")(down, *kv_tabs)[:, KV_RANK:KV_RANK + ROPE]
                k_rope = jnp.broadcast_to(k_rope[:, None, :], (s, MLA_HEADS, ROPE))
                kcat = jnp.concatenate([kv[..., :NOPE], k_rope], axis=-1).transpose(1, 0, 2)
                vh = kv[..., NOPE:].transpose(1, 0, 2)
            c_q = _make_norm_linear(f"mla{j}_dq")(
                x, norms['attn_norm'][layer], slots['mla_w_dq'][j], wb['mla_w_dq'][j])
            q = _make_norm_linear(f"mla{j}_uq")(
                c_q, norms['mla_q_lat_norm'][j], slots['mla_w_uq'][j], wb['mla_w_uq'][j])
            q = _make_rope(f"rope_q{j}")(q, *q_tabs)
            q = q.reshape(s, MLA_HEADS, QK).transpose(1, 0, 2)
            o = mla_attention(q, kcat, vh)
            o = o.transpose(1, 0, 2).reshape(s, MLA_HEADS * VDIM)
            x = _make_linear_res(f"mla{j}_o")(x, o, slots['mla_w_o'][j], wb['mla_w_o'][j])
        x = _make_mlp_res(f"mlp{layer}")(
            x, norms['mlp_norm'][layer], slots['mlp_w1'][layer], slots['mlp_w2'][layer],
            wb['mlp_w1'][layer], wb['mlp_w2'][layer])
    return x


def _merge_blocks(gathered, ax):
    shp = gathered.shape[1:]
    return jnp.moveaxis(gathered, 0, ax).reshape(shp[:ax] + (N_DEV * shp[ax],) + shp[ax + 1:])


def _split_blocks(full, ax):
    shp = full.shape
    return jnp.moveaxis(full.reshape(shp[:ax] + (N_DEV, shp[ax] // N_DEV) + shp[ax + 1:]), ax, 0)


def _pack(parts):
    flat = jnp.concatenate([p.reshape(-1) for p in parts])
    rows = -(-flat.shape[0] // PACK_COLS)
    rows = -(-rows // PACK_ROW_ALIGN) * PACK_ROW_ALIGN
    return jnp.pad(flat, (0, rows * PACK_COLS - flat.shape[0])).reshape(rows, PACK_COLS)


def _unpack(packed, shapes):
    flat = packed.reshape(-1)
    out, off = [], 0
    for shp in shapes:
        n = math.prod(shp)
        out.append(flat[off:off + n].reshape(shp))
        off += n
    return out


def _add_round(x, stage, core, name):
    _, r, c = x.shape
    tr = _tile(r, (256, 128, 64, 32, 16, 8))

    def body(core_ref, x_ref, s_ref, o_ref):
        o_ref[...] = (x_ref[0] + s_ref[...]).astype(o_ref.dtype)

    return pl.pallas_call(
        body, name=name,
        grid_spec=pltpu.PrefetchScalarGridSpec(
            num_scalar_prefetch=1, grid=(r // tr,),
            in_specs=[pl.BlockSpec((1, tr, c), lambda i, core_ref: (core_ref[0], i, 0)),
                      pl.BlockSpec((tr, c), lambda i, core_ref: (i, 0))],
            out_specs=pl.BlockSpec((tr, c), lambda i, core_ref: (i, 0))),
        out_shape=jax.ShapeDtypeStruct((r, c), BF16),
        compiler_params=pltpu.CompilerParams(dimension_semantics=("parallel",)),
    )(core, x, stage)


def _adamw_reduce(parts, w, m, v, name):
    n_parts, r, c = parts.shape
    tr = _tile(r, (128, 64, 32, 16, 8))
    bias1 = 1.0 - ADAM_B1 ** ADAM_STEP
    bias2 = 1.0 - ADAM_B2 ** ADAM_STEP

    def body(p_ref, w_ref, m_ref, v_ref, g_ref, d_ref, nm_ref, nv_ref):
        g = p_ref[0].astype(F32)
        for s in range(1, n_parts):
            g = g + p_ref[s].astype(F32)
        mn = ADAM_B1 * m_ref[...] + (1.0 - ADAM_B1) * g
        vn = ADAM_B2 * v_ref[...] + (1.0 - ADAM_B2) * (g * g)
        m_hat = mn / bias1
        v_hat = vn / bias2
        g_ref[...] = g
        d_ref[...] = -ADAM_LR * (m_hat / (jnp.sqrt(v_hat) + ADAM_EPS) + ADAM_WD * w_ref[...])
        nm_ref[...] = mn
        nv_ref[...] = vn

    blk = pl.BlockSpec((tr, c), lambda i: (i, 0))
    return pl.pallas_call(
        body, name=name, grid=(r // tr,),
        in_specs=[pl.BlockSpec((n_parts, tr, c), lambda i: (0, i, 0)), blk, blk, blk],
        out_specs=[blk] * 4,
        out_shape=[jax.ShapeDtypeStruct((r, c), F32)] * 4,
        compiler_params=pltpu.CompilerParams(dimension_semantics=("parallel",), vmem_limit_bytes=VMEM_MM),
    )(parts, w, m, v)


def kernel(x, positions, attn_norm, mlp_norm, sb_w_qkv, sb_w_o, kv_norm, mla_w_dkv, mla_kv_lat_norm, mla_w_ukv, mla_w_dq, mla_q_lat_norm, mla_w_uq, mla_w_o, mlp_w1, mlp_w2, final_norm, loss_target, m_attn_norm, m_mlp_norm, m_sb_w_qkv, m_sb_w_o, m_kv_norm, m_mla_w_dkv, m_mla_kv_lat_norm, m_mla_w_ukv, m_mla_w_dq, m_mla_q_lat_norm, m_mla_w_uq, m_mla_w_o, m_mlp_w1, m_mlp_w2, m_final_norm, v_attn_norm, v_mlp_norm, v_sb_w_qkv, v_sb_w_o, v_kv_norm, v_mla_w_dkv, v_mla_kv_lat_norm, v_mla_w_ukv, v_mla_w_dq, v_mla_q_lat_norm, v_mla_w_uq, v_mla_w_o, v_mlp_w1, v_mlp_w2, v_final_norm):
    weights = dict(attn_norm=attn_norm, mlp_norm=mlp_norm, sb_w_qkv=sb_w_qkv, sb_w_o=sb_w_o, kv_norm=kv_norm,
                   mla_w_dkv=mla_w_dkv, mla_kv_lat_norm=mla_kv_lat_norm, mla_w_ukv=mla_w_ukv, mla_w_dq=mla_w_dq,
                   mla_q_lat_norm=mla_q_lat_norm, mla_w_uq=mla_w_uq, mla_w_o=mla_w_o, mlp_w1=mlp_w1, mlp_w2=mlp_w2,
                   final_norm=final_norm)
    mom_m = dict(attn_norm=m_attn_norm, mlp_norm=m_mlp_norm, sb_w_qkv=m_sb_w_qkv, sb_w_o=m_sb_w_o, kv_norm=m_kv_norm,
                 mla_w_dkv=m_mla_w_dkv, mla_kv_lat_norm=m_mla_kv_lat_norm, mla_w_ukv=m_mla_w_ukv, mla_w_dq=m_mla_w_dq,
                 mla_q_lat_norm=m_mla_q_lat_norm, mla_w_uq=m_mla_w_uq, mla_w_o=m_mla_w_o, mlp_w1=m_mlp_w1,
                 mlp_w2=m_mlp_w2, final_norm=m_final_norm)
    mom_v = dict(attn_norm=v_attn_norm, mlp_norm=v_mlp_norm, sb_w_qkv=v_sb_w_qkv, sb_w_o=v_sb_w_o, kv_norm=v_kv_norm,
                 mla_w_dkv=v_mla_w_dkv, mla_kv_lat_norm=v_mla_kv_lat_norm, mla_w_ukv=v_mla_w_ukv, mla_w_dq=v_mla_w_dq,
                 mla_q_lat_norm=v_mla_q_lat_norm, mla_w_uq=v_mla_w_uq, mla_w_o=v_mla_w_o, mlp_w1=v_mlp_w1,
                 mlp_w2=v_mlp_w2, final_norm=v_final_norm)
    sharded_names = [n for n, _ in SHARDED]
    repl_shapes = [tuple(weights[n].shape) for n in REPLICATED]

    gathered = _all_gather([weights[n].astype(BF16) for n in sharded_names], "gather_weights")
    wb, slots = {}, {}
    for (n, ax), g in zip(SHARDED, gathered):
        full = _merge_blocks(g, ax)
        layers = [full[l] for l in range(full.shape[0])] if full.ndim == 3 else [full]
        wb[n] = layers
        slots[n] = [jnp.zeros(w.shape, F32) for w in layers]
    norms = {n: weights[n] for n in REPLICATED if n != 'final_norm'}

    q_tabs, kv_tabs = _rope_tables(positions[0])
    x_last, pullback = jax.vjp(lambda xx, ss, nn: _trunk(xx, ss, nn, wb, q_tabs, kv_tabs), x[0], slots, norms)
    loss_part, dx_last, d_final = _loss_head(x_last, final_norm, loss_target[0])
    dx, d_slots, d_norms = pullback(dx_last)
    d_norms = dict(d_norms)
    d_norms['final_norm'] = d_final
    loss = lax.psum(loss_part[0, 0], ("x", "y", "c"))

    core = lax.axis_index("c").astype(jnp.int32).reshape(1)
    halves, dims = [], []
    for n, ax in SHARDED:
        full = jnp.stack(d_slots[n]) if weights[n].ndim == 3 else d_slots[n][0]
        blocks = _split_blocks(full, ax)
        two_d = (math.prod(blocks.shape[1:-1]), blocks.shape[-1])
        dims.append(two_d)
        halves.append(jnp.moveaxis(blocks.reshape((N_DEV // 2, 2) + two_d), 1, 0))
    staged = _pair_exchange(halves, "pair_grads")
    sums = []
    for n, h, st, (r, c) in zip(sharded_names, halves, staged, dims):
        rows = (N_DEV // 2) * r
        sums.append(_add_round(h.reshape(2, rows, c), st.reshape(rows, c), core, "pair_sum_" + n).reshape(st.shape))
    received = _chip_exchange(sums, "scatter_grads")
    repl_parts = _all_gather([_pack([d_norms[n] for n in REPLICATED])], "gather_norm_grads")[0]

    results = {kind: {} for kind in ("grad", "delta", "new_m", "new_v")}
    for n, parts, two_d in zip(sharded_names, received, dims):
        shp = weights[n].shape
        res = _adamw_reduce(parts, weights[n].reshape(two_d), mom_m[n].reshape(two_d),
                            mom_v[n].reshape(two_d), "adamw_" + n)
        for kind, a in zip(results, res):
            results[kind][n] = a.reshape(shp)
    res = _adamw_reduce(repl_parts, _pack([weights[n] for n in REPLICATED]), _pack([mom_m[n] for n in REPLICATED]),
                        _pack([mom_v[n] for n in REPLICATED]), "adamw_replicated")
    for kind, a in zip(results, res):
        results[kind].update(zip(REPLICATED, _unpack(a, repl_shapes)))

    out = [loss, dx[None]]
    for kind in ("grad", "delta", "new_m", "new_v"):
        out += [results[kind][n] for n in WEIGHT_ORDER]
    return tuple(out)
```

```python
import math

import jax
import jax.numpy as jnp
from jax import lax
from jax.experimental import pallas as pl
from jax.experimental.pallas import tpu as pltpu

F32 = jnp.float32
BF16 = jnp.bfloat16
MESH = pl.DeviceIdType.MESH

N_DEV = 8
DEPTH = 4
N_A = 2
SB_HEADS = 16
SB_HD = 64
MLA_HEADS = 16
NOPE = 64
ROPE = 32
VDIM = 64
QK = NOPE + ROPE
KV_RANK = 256
DKV_PAD = 384
LANES = 128
CHUNK_SHIFT = 6
ROPE_THETA = 10000.0
EPS = 1e-6
SB_BLOCK = 256
MLA_BLOCK = 512
MLA_GROUP = 4
PACK_COLS = 1024
PACK_ROW_ALIGN = 16
EXP_ZERO = -104.0
NEG_BIG = -1e30
LOG2_E = 1.4426950408889634
LN_2 = 0.6931471805599453
VMEM_ATT = 56 * 1024 * 1024
VMEM_MM = 48 * 1024 * 1024

ADAM_LR = 0.001
ADAM_B1 = 0.9
ADAM_B2 = 0.999
ADAM_EPS = 1e-08
ADAM_WD = 0.01
ADAM_STEP = 10

WEIGHT_ORDER = ['attn_norm', 'mlp_norm', 'sb_w_qkv', 'sb_w_o', 'kv_norm', 'mla_w_dkv', 'mla_kv_lat_norm',
                'mla_w_ukv', 'mla_w_dq', 'mla_q_lat_norm', 'mla_w_uq', 'mla_w_o', 'mlp_w1', 'mlp_w2', 'final_norm']
SHARDED = [('sb_w_qkv', 2), ('sb_w_o', 1), ('mla_w_dkv', 0), ('mla_w_ukv', 1), ('mla_w_dq', 1),
           ('mla_w_uq', 2), ('mla_w_o', 1), ('mlp_w1', 2), ('mlp_w2', 1)]
REPLICATED = ['attn_norm', 'mlp_norm', 'kv_norm', 'mla_kv_lat_norm', 'mla_q_lat_norm', 'final_norm']


def _tile(dim, prefs=(512, 384, 256, 128, 64, 32, 16, 8)):
    for t in prefs:
        if dim % t == 0:
            return t
    return dim


def _dot(a, b, dims):
    return lax.dot_general(a, b, (dims, ((), ())), preferred_element_type=F32)


NN = ((1,), (0,))
NT = ((1,), (1,))
TN = ((0,), (0,))


def _all_gather(shards, name):
    n_t = len(shards)

    def body(*refs):
        x_refs, out_refs = refs[:n_t], refs[n_t:2 * n_t]
        send_sems, recv_sems, local_sems = refs[2 * n_t:]
        x, y, c = lax.axis_index("x"), lax.axis_index("y"), lax.axis_index("c")
        me, sibling = (x, y, c), (x, y, 1 - c)
        chips = [(1 - x, y), (x, 1 - y), (1 - x, 1 - y)]

        def slot(t, px, py, pc):
            return out_refs[t].at[4 * px + 2 * py + pc]

        def copy(t, k, block, to, src=None):
            return pltpu.make_async_remote_copy(
                src_ref=slot(t, *block) if src is None else src, dst_ref=slot(t, *block),
                send_sem=send_sems.at[7 * t + k], recv_sem=recv_sems.at[7 * t + k],
                device_id=to, device_id_type=MESH)

        mine = [pltpu.make_async_copy(x_refs[t], slot(t, *me), local_sems.at[t]) for t in range(n_t)]
        for cp in mine:
            cp.start()
        first = []
        for t in range(n_t):
            first.append(copy(t, 0, me, sibling, src=x_refs[t]))
            first += [copy(t, 1 + j, me, (*chip, c), src=x_refs[t]) for j, chip in enumerate(chips)]
        for cp in first:
            cp.start()
        passed = []
        for t in range(n_t):
            for j, chip in enumerate(chips):
                copy(t, 1 + j, (*chip, c), me).wait_recv()
                onward = copy(t, 4 + j, (*chip, c), sibling)
                onward.start()
                passed.append(onward)
        for t in range(n_t):
            copy(t, 0, sibling, me).wait_recv()
            for j, chip in enumerate(chips):
                copy(t, 4 + j, (*chip, 1 - c), me).wait_recv()
        for cp in first + passed:
            cp.wait_send()
        for cp in mine:
            cp.wait()

    any_spec = pl.BlockSpec(memory_space=pl.ANY)
    return pl.pallas_call(
        body, name=name,
        out_shape=[jax.ShapeDtypeStruct((N_DEV,) + tuple(s.shape), s.dtype) for s in shards],
        in_specs=[any_spec] * n_t, out_specs=[any_spec] * n_t,
        scratch_shapes=[pltpu.SemaphoreType.DMA((7 * n_t,)), pltpu.SemaphoreType.DMA((7 * n_t,)),
                        pltpu.SemaphoreType.DMA((n_t,))],
    )(*shards)


def _pair_exchange(xs, name):
    n_t = len(xs)

    def body(*refs):
        x_refs, out_refs = refs[:n_t], refs[n_t:2 * n_t]
        send_sems, recv_sems = refs[2 * n_t:]
        x, y, c = lax.axis_index("x"), lax.axis_index("y"), lax.axis_index("c")
        copies = [pltpu.make_async_remote_copy(
            src_ref=x_refs[t].at[1 - c], dst_ref=out_refs[t], send_sem=send_sems.at[t], recv_sem=recv_sems.at[t],
            device_id=(x, y, 1 - c), device_id_type=MESH) for t in range(n_t)]
        for cp in copies:
            cp.start()
        for cp in copies:
            cp.wait()

    any_spec = pl.BlockSpec(memory_space=pl.ANY)
    return pl.pallas_call(
        body, name=name,
        out_shape=[jax.ShapeDtypeStruct(x.shape[1:], x.dtype) for x in xs],
        in_specs=[any_spec] * n_t, out_specs=[any_spec] * n_t,
        scratch_shapes=[pltpu.SemaphoreType.DMA((n_t,)), pltpu.SemaphoreType.DMA((n_t,))],
    )(*xs)


def _chip_exchange(ps, name):
    n_t = len(ps)

    def body(*refs):
        p_refs, out_refs = refs[:n_t], refs[n_t:2 * n_t]
        send_sems, recv_sems, local_sems = refs[2 * n_t:]
        mx, my, mc = lax.axis_index("x"), lax.axis_index("y"), lax.axis_index("c")
        me = 2 * mx + my
        mine = [pltpu.make_async_copy(p_refs[t].at[me], out_refs[t].at[me], local_sems.at[t]) for t in range(n_t)]
        for cp in mine:
            cp.start()
        copies = []
        for k in range(1, 4):
            px = 1 - mx if (k >> 1) & 1 else mx
            py = 1 - my if k & 1 else my
            peer = 2 * px + py
            for t in range(n_t):
                copies.append(pltpu.make_async_remote_copy(
                    src_ref=p_refs[t].at[peer], dst_ref=out_refs[t].at[me],
                    send_sem=send_sems.at[3 * t + k - 1], recv_sem=recv_sems.at[3 * t + k - 1],
                    device_id=(px, py, mc), device_id_type=MESH))
        for cp in copies:
            cp.start()
        for cp in copies:
            cp.wait_send()
        for cp in copies:
            cp.wait_recv()
        for cp in mine:
            cp.wait()

    any_spec = pl.BlockSpec(memory_space=pl.ANY)
    return pl.pallas_call(
        body, name=name,
        out_shape=[jax.ShapeDtypeStruct(p.shape, p.dtype) for p in ps],
        in_specs=[any_spec] * n_t, out_specs=[any_spec] * n_t,
        scratch_shapes=[pltpu.SemaphoreType.DMA((3 * n_t,)), pltpu.SemaphoreType.DMA((3 * n_t,)),
                        pltpu.SemaphoreType.DMA((n_t,))],
    )(*ps)


def _matmul(a, b, *, ta=False, tb=False, out_dtypes=(F32,), epilogue=None, extras=(), name):
    if ta:
        kdim, m = a.shape
    else:
        m, kdim = a.shape
    if tb:
        n, kb = b.shape
    else:
        kb, n = b.shape
    assert kdim == kb, (a.shape, b.shape, ta, tb)
    big = (1024, 768, 512, 384, 256, 128, 64, 32, 16, 8)
    tm, tn = _tile(m, big), _tile(n, big)
    tk = kdim if kdim <= 2048 else _tile(kdim, (1024, 512, 256, 128))
    nk = kdim // tk
    n_extra, n_out = len(extras), len(out_dtypes)
    a_spec = pl.BlockSpec((tk, tm), lambda i, j, k: (k, i)) if ta else pl.BlockSpec((tm, tk), lambda i, j, k: (i, k))
    b_spec = pl.BlockSpec((tn, tk), lambda i, j, k: (j, k)) if tb else pl.BlockSpec((tk, tn), lambda i, j, k: (k, j))
    tile_spec = pl.BlockSpec((tm, tn), lambda i, j, k: (i, j))
    dims = ((0,) if ta else (1,), (1,) if tb else (0,))

    def finish(acc, extra_refs, out_refs):
        outs = (acc,) if epilogue is None else epilogue(acc, *[r[...] for r in extra_refs])
        for o_ref, o in zip(out_refs, outs):
            o_ref[...] = o.astype(o_ref.dtype)

    def body_one(a_ref, b_ref, *rest):
        acc = _dot(a_ref[...].astype(BF16), b_ref[...].astype(BF16), dims)
        finish(acc, rest[:n_extra], rest[n_extra:n_extra + n_out])

    def body_acc(a_ref, b_ref, *rest):
        acc_ref = rest[-1]
        k = pl.program_id(2)

        @pl.when(k == 0)
        def _():
            acc_ref[...] = jnp.zeros_like(acc_ref)

        acc_ref[...] += _dot(a_ref[...].astype(BF16), b_ref[...].astype(BF16), dims)

        @pl.when(k == nk - 1)
        def _():
            finish(acc_ref[...], rest[:n_extra], rest[n_extra:n_extra + n_out])

    return pl.pallas_call(
        body_one if nk == 1 else body_acc, name=name, grid=(m // tm, n // tn, nk),
        in_specs=[a_spec, b_spec] + [tile_spec] * n_extra,
        out_specs=[tile_spec] * n_out,
        out_shape=[jax.ShapeDtypeStruct((m, n), dt) for dt in out_dtypes],
        scratch_shapes=[] if nk == 1 else [pltpu.VMEM((tm, tn), F32)],
        compiler_params=pltpu.CompilerParams(
            dimension_semantics=("parallel", "parallel", "arbitrary"), vmem_limit_bytes=VMEM_MM),
    )(a, b, *extras)


def _rms_fwd(x, g, name):
    m, d = x.shape
    tm = _tile(m, (512, 256, 128, 64, 32, 16, 8))

    def body(x_ref, g_ref, y_ref):
        xv = x_ref[...]
        r = lax.rsqrt(jnp.mean(xv * xv, axis=-1, keepdims=True) + EPS)
        y_ref[...] = (xv * r * g_ref[...]).astype(y_ref.dtype)

    return pl.pallas_call(
        body, name=name, grid=(m // tm,),
        in_specs=[pl.BlockSpec((tm, d), lambda i: (i, 0)), pl.BlockSpec((1, d), lambda i: (0, 0))],
        out_specs=pl.BlockSpec((tm, d), lambda i: (i, 0)),
        out_shape=jax.ShapeDtypeStruct((m, d), BF16),
        compiler_params=pltpu.CompilerParams(dimension_semantics=("parallel",)),
    )(x, g.reshape(1, d))


def _rms_bwd(x, g, dy, name, res=None):
    m, d = x.shape
    tm = _tile(m, (512, 256, 128, 64, 32, 16, 8))
    has_res = res is not None

    def body(x_ref, g_ref, dy_ref, *rest):
        dx_ref, dg_ref = rest[-2:]
        xv = x_ref[...]
        dyv = dy_ref[...]
        r = lax.rsqrt(jnp.mean(xv * xv, axis=-1, keepdims=True) + EPS)
        xh = xv * r
        t = dyv * g_ref[...]
        dx = r * (t - xh * jnp.mean(t * xh, axis=-1, keepdims=True))
        dx_ref[...] = dx + rest[0][...] if has_res else dx

        @pl.when(pl.program_id(0) == 0)
        def _():
            dg_ref[...] = jnp.zeros_like(dg_ref)

        dg_ref[...] += jnp.sum(dyv * xh, axis=0, keepdims=True)

    row_spec = pl.BlockSpec((tm, d), lambda i: (i, 0))
    vec_spec = pl.BlockSpec((1, d), lambda i: (0, 0))
    dx, dg = pl.pallas_call(
        body, name=name, grid=(m // tm,),
        in_specs=[row_spec, vec_spec, row_spec] + ([row_spec] if has_res else []),
        out_specs=[row_spec, vec_spec],
        out_shape=[jax.ShapeDtypeStruct((m, d), F32), jax.ShapeDtypeStruct((1, d), F32)],
        compiler_params=pltpu.CompilerParams(dimension_semantics=("arbitrary",)),
    )(x, g.reshape(1, d), dy, *((res,) if has_res else ()))
    return dx, dg.reshape(d)


def _loss_head(x, g, target):
    m, d = x.shape
    tm = _tile(m, (512, 256, 128, 64, 32, 16, 8))

    def body(x_ref, g_ref, t_ref, loss_ref, dx_ref, dg_ref):
        xv = x_ref[...]
        gv = g_ref[...]
        r = lax.rsqrt(jnp.mean(xv * xv, axis=-1, keepdims=True) + EPS)
        xh = xv * r
        err = xh * gv - t_ref[...]
        row_loss = jnp.mean(err * err, axis=-1, keepdims=True)
        dyv = err * (1.0 / d)
        t = dyv * gv
        dx_ref[...] = r * (t - xh * jnp.mean(t * xh, axis=-1, keepdims=True))

        @pl.when(pl.program_id(0) == 0)
        def _():
            dg_ref[...] = jnp.zeros_like(dg_ref)
            loss_ref[...] = jnp.zeros_like(loss_ref)

        dg_ref[...] += jnp.sum(dyv * xh, axis=0, keepdims=True)
        loss_ref[...] += 0.5 * jnp.sum(row_loss, axis=0, keepdims=True)

    loss, dx, dg = pl.pallas_call(
        body, name="loss_head", grid=(m // tm,),
        in_specs=[pl.BlockSpec((tm, d), lambda i: (i, 0)), pl.BlockSpec((1, d), lambda i: (0, 0)),
                  pl.BlockSpec((tm, d), lambda i: (i, 0))],
        out_specs=[pl.BlockSpec((1, 1), lambda i: (0, 0)), pl.BlockSpec((tm, d), lambda i: (i, 0)),
                   pl.BlockSpec((1, d), lambda i: (0, 0))],
        out_shape=[jax.ShapeDtypeStruct((1, 1), F32), jax.ShapeDtypeStruct((m, d), F32),
                   jax.ShapeDtypeStruct((1, d), F32)],
        compiler_params=pltpu.CompilerParams(dimension_semantics=("arbitrary",)),
    )(x, g.reshape(1, d), target)
    return loss, dx, dg.reshape(d)


def _rope_apply(x, tabs, transpose, name):
    m, w = x.shape
    wt = tabs[0].shape[1]
    reps = w // wt
    half = ROPE // 2
    tm = _tile(m, (256, 128, 64, 32, 16, 8))

    def body(x_ref, c_ref, a_ref, b_ref, y_ref):
        xv = x_ref[...]

        def wide(t_ref):
            t = t_ref[...]
            return t if reps == 1 else jnp.concatenate([t] * reps, axis=1)

        c, a, b = wide(c_ref), wide(a_ref), wide(b_ref)
        if transpose:
            y = xv * c + pltpu.roll(xv * a, half, 1) + pltpu.roll(xv * b, w - half, 1)
        else:
            y = xv * c + pltpu.roll(xv, w - half, 1) * a + pltpu.roll(xv, half, 1) * b
        y_ref[...] = y

    x_spec = pl.BlockSpec((tm, w), lambda i: (i, 0))
    t_spec = pl.BlockSpec((tm, wt), lambda i: (i, 0))
    return pl.pallas_call(
        body, name=name, grid=(m // tm,),
        in_specs=[x_spec, t_spec, t_spec, t_spec], out_specs=x_spec,
        out_shape=jax.ShapeDtypeStruct((m, w), F32),
        compiler_params=pltpu.CompilerParams(dimension_semantics=("parallel",)),
    )(x, *tabs)


def _log_sigmoid_pair(z):
    a = jnp.minimum(z, 0.0) - jnp.log(1.0 + jnp.exp(-jnp.abs(z)))
    return a, a - z


def _split_bf16(x):
    hi = x.astype(BF16)
    return hi, (x - hi.astype(F32)).astype(BF16)


def _wide(v, width):
    return v if width == LANES else jnp.concatenate([v] * (width // LANES), axis=1)


def _lanes(col):
    return jnp.broadcast_to(col, (col.shape[0], LANES))


def _sb_weights(qh, k2, valid):
    a, b = _log_sigmoid_pair(_dot(qh, k2, NT))
    return a, b if valid is None else jnp.where(valid, b, 0.0)


def _sb_fwd(qkv):
    s = qkv.shape[0]
    t = min(SB_BLOCK, s)
    nq = s // t
    npair = SB_HEADS // 2
    scale = SB_HD ** -0.5

    def body(q_ref, k_ref, v_ref, o_ref, bta_ref, btb_ref, js_ref, acc_ref, ra_ref, rb_ref):
        p, i = pl.program_id(0), pl.program_id(1)
        q2 = q_ref[...] * scale
        first = lax.broadcasted_iota(jnp.int32, (t, LANES), 1) < SB_HD
        heads = (jnp.where(first, q2, jnp.zeros_like(q2)), jnp.where(first, jnp.zeros_like(q2), q2))
        row = lax.broadcasted_iota(jnp.int32, (t, t), 0)
        col = lax.broadcasted_iota(jnp.int32, (t, t), 1)
        later = jnp.where(row > col, 1.0, 0.0).astype(BF16)
        acc_ref[...] = jnp.zeros_like(acc_ref)
        ra_ref[...] = jnp.zeros_like(ra_ref)
        rb_ref[...] = jnp.zeros_like(rb_ref)

        def block(j, diagonal):
            off = pl.multiple_of(j * t, t)
            k2 = k_ref[pl.ds(off, t), :]
            v2 = v_ref[pl.ds(off, t), :]
            valid = col < row if diagonal else None
            outs, rmax = [], None
            for qh, r_ref in zip(heads, (ra_ref, rb_ref)):
                a, b = _sb_weights(qh, k2, valid)
                bh, bl = _split_bf16(b)
                inner = _dot(bh, later, NN) + _dot(bl, later, NN)
                r = r_ref[...]
                w = jnp.exp(a + inner + _wide(r, t))
                if diagonal:
                    w = jnp.where(valid, w, 0.0)
                outs.append(_dot(w.astype(BF16), v2, NN))
                rn = r + jnp.sum(b, axis=1, keepdims=True)
                r_ref[...] = rn
                rmax = jnp.max(rn) if rmax is None else jnp.maximum(rmax, jnp.max(rn))
            acc_ref[...] += jnp.where(first, outs[0], outs[1])
            return rmax

        def cond(carry):
            j, rmax = carry
            return jnp.logical_and(j >= 0, rmax > EXP_ZERO)

        def step(carry):
            j, _ = carry
            return j - 1, block(j, False)

        jend, _ = lax.while_loop(cond, step, (i - 1, block(i, True)))
        o_ref[...] = acc_ref[...].astype(o_ref.dtype)
        bta_ref[0] = ra_ref[...][:, :1]
        btb_ref[0] = rb_ref[...][:, :1]
        js_ref[p, i] = (jend + 1).astype(F32)

    stat = pl.BlockSpec((1, t, 1), lambda p, i: (p, i, 0))
    return pl.pallas_call(
        body, name="sb_attn_fwd", grid=(npair, nq),
        in_specs=[pl.BlockSpec((t, LANES), lambda p, i: (i, p)),
                  pl.BlockSpec((s, LANES), lambda p, i: (0, npair + p)),
                  pl.BlockSpec((s, LANES), lambda p, i: (0, 2 * npair + p))],
        out_specs=[pl.BlockSpec((t, LANES), lambda p, i: (i, p)), stat, stat,
                   pl.BlockSpec(memory_space=pltpu.SMEM)],
        out_shape=[jax.ShapeDtypeStruct((s, SB_HEADS * SB_HD), BF16), jax.ShapeDtypeStruct((npair, s, 1), F32),
                   jax.ShapeDtypeStruct((npair, s, 1), F32), jax.ShapeDtypeStruct((npair, nq), F32)],
        scratch_shapes=[pltpu.VMEM((t, LANES), F32)] * 3,
        compiler_params=pltpu.CompilerParams(
            dimension_semantics=("arbitrary", "arbitrary"), vmem_limit_bytes=VMEM_ATT),
    )(qkv, qkv, qkv)


def _sb_bwd(qkv, do, btot_a, btot_b, jstart):
    s = qkv.shape[0]
    t = min(SB_BLOCK, s)
    nq = s // t
    npair = SB_HEADS // 2
    scale = SB_HD ** -0.5

    def body(js_ref, q_ref, k_ref, v_ref, do_ref, bta_ref, btb_ref, dq_ref, dk_ref, dv_ref,
             dq_acc, pa_ref, pb_ref, ga_ref, gb_ref, dk_acc, dv_acc):
        p, i = pl.program_id(0), pl.program_id(1)

        @pl.when(i == 0)
        def _():
            dk_acc[...] = jnp.zeros_like(dk_acc)
            dv_acc[...] = jnp.zeros_like(dv_acc)

        q2 = q_ref[...] * scale
        do2 = do_ref[...]
        first = lax.broadcasted_iota(jnp.int32, (t, LANES), 1) < SB_HD
        zero = jnp.zeros_like(q2)
        q_heads = (jnp.where(first, q2, zero), jnp.where(first, zero, q2))
        do_heads = (jnp.where(first, do2, zero), jnp.where(first, zero, do2))
        bts = (_wide(_lanes(bta_ref[0]), t), _wide(_lanes(btb_ref[0]), t))
        row = lax.broadcasted_iota(jnp.int32, (t, t), 0)
        col = lax.broadcasted_iota(jnp.int32, (t, t), 1)
        upto = jnp.where(row <= col, 1.0, 0.0).astype(BF16)
        before = jnp.where(row < col, 1.0, 0.0).astype(BF16)
        dq_acc[...] = jnp.zeros_like(dq_acc)
        for r in (pa_ref, pb_ref, ga_ref, gb_ref):
            r[...] = jnp.zeros_like(r)
        j0 = jnp.clip(js_ref[p, i].astype(jnp.int32), 0, i)

        def block(j, diagonal):
            off = pl.multiple_of(j * t, t)
            k2 = k_ref[pl.ds(off, t), :]
            v2 = v_ref[pl.ds(off, t), :]
            valid = col < row if diagonal else None
            dqs, dk2, dv2 = [], None, None
            for qh, doh, bt, p_ref, g_ref in zip(q_heads, do_heads, bts, (pa_ref, pb_ref), (ga_ref, gb_ref)):
                a, b = _sb_weights(qh, k2, valid)
                bh, bl = _split_bf16(b)
                pin = _dot(bh, upto, NN) + _dot(bl, upto, NN)
                surv = bt - (_wide(p_ref[...], t) + pin)
                w = jnp.exp(a + surv)
                if diagonal:
                    w = jnp.where(valid, w, 0.0)
                g = w * _dot(doh, v2, NT)
                gh, gl = _split_bf16(g)
                gsum = _wide(g_ref[...], t) + _dot(gh, before, NN) + _dot(gl, before, NN)
                beta = jnp.exp(a)
                dz = g * (1.0 - beta) - gsum * beta
                if diagonal:
                    dz = jnp.where(valid, dz, 0.0)
                dz = dz.astype(BF16)
                dqs.append(_dot(dz, k2, NN))
                dkh = _dot(dz, qh, TN)
                dvh = _dot(w.astype(BF16), doh, TN)
                dk2 = dkh if dk2 is None else dk2 + dkh
                dv2 = dvh if dv2 is None else dv2 + dvh
                p_ref[...] += jnp.sum(b, axis=1, keepdims=True)
                g_ref[...] += jnp.sum(g, axis=1, keepdims=True)
            dq_acc[...] += jnp.where(first, dqs[0], dqs[1])
            dk_acc[pl.ds(off, t), :] += dk2
            dv_acc[pl.ds(off, t), :] += dv2

        def step(j, carry):
            block(j, False)
            return carry

        lax.fori_loop(j0, i, step, 0)
        block(i, True)
        dq_ref[...] = (dq_acc[...] * scale).astype(dq_ref.dtype)

        @pl.when(i == nq - 1)
        def _():
            dk_ref[...] = dk_acc[...].astype(dk_ref.dtype)
            dv_ref[...] = dv_acc[...].astype(dv_ref.dtype)

    blk = pl.BlockSpec((t, LANES), lambda p, i: (i, p))
    full = pl.BlockSpec((s, LANES), lambda p, i: (0, p))
    stat = pl.BlockSpec((1, t, 1), lambda p, i: (p, i, 0))
    vec = pltpu.VMEM((t, LANES), F32)
    return pl.pallas_call(
        body, name="sb_attn_bwd", grid=(npair, nq),
        in_specs=[pl.BlockSpec(memory_space=pltpu.SMEM), blk,
                  pl.BlockSpec((s, LANES), lambda p, i: (0, npair + p)),
                  pl.BlockSpec((s, LANES), lambda p, i: (0, 2 * npair + p)), blk, stat, stat],
        out_specs=[blk, full, full],
        out_shape=[jax.ShapeDtypeStruct((s, SB_HEADS * SB_HD), BF16)] * 3,
        scratch_shapes=[pltpu.VMEM((t, LANES), F32), vec, vec, vec, vec,
                        pltpu.VMEM((s, LANES), F32), pltpu.VMEM((s, LANES), F32)],
        compiler_params=pltpu.CompilerParams(
            dimension_semantics=("arbitrary", "arbitrary"), vmem_limit_bytes=VMEM_ATT),
    )(jstart, qkv, qkv, qkv, do, btot_a, btot_b)


@jax.custom_vjp
def sb_attention(qkv):
    return _sb_fwd(qkv)[0]


def _sb_attention_fwd(qkv):
    o, btot_a, btot_b, jstart = _sb_fwd(qkv)
    return o, (qkv, btot_a, btot_b, jstart)


def _sb_attention_bwd(res, do):
    qkv, btot_a, btot_b, jstart = res
    dq, dk, dv = _sb_bwd(qkv, do, btot_a, btot_b, jstart)
    return (jnp.concatenate([dq, dk, dv], axis=1),)


sb_attention.defvjp(_sb_attention_fwd, _sb_attention_bwd)


def _chunk_allowed(row, col):
    return (col >> CHUNK_SHIFT) <= (row >> CHUNK_SHIFT)


def _mla_fwd(q, k, vx):
    s = q.shape[0]
    h_ = q.shape[1] // LANES
    t = min(MLA_BLOCK, s)
    nq = s // t
    scale = QK ** -0.5
    hp = MLA_GROUP

    def body(q_ref, k_ref, v_ref, o_ref, lse_ref, acc_ref, m_ref):
        i = pl.program_id(1)
        acc_ref[...] = jnp.zeros_like(acc_ref)
        m_ref[...] = jnp.full_like(m_ref, NEG_BIG)

        def tile(j, diagonal):
            off = pl.multiple_of(j * t, t)
            for hh in range(hp):
                lanes = slice(hh * LANES, (hh + 1) * LANES)
                kj = k_ref[pl.ds(off, t), lanes]
                vj = v_ref[pl.ds(off, t), lanes]
                sc = _dot(q_ref[:, lanes], kj, NT) * (scale * LOG2_E)
                if diagonal:
                    row = lax.broadcasted_iota(jnp.int32, (t, t), 0)
                    col = lax.broadcasted_iota(jnp.int32, (t, t), 1)
                    sc = jnp.where(_chunk_allowed(row, col), sc, NEG_BIG)
                m_old = m_ref[hh]
                m_new = jnp.maximum(m_old, jnp.max(sc, axis=1, keepdims=True))
                p = jnp.exp2(sc - _wide(m_new, t))
                acc_ref[hh] = jnp.exp2(m_old - m_new) * acc_ref[hh] + _dot(p.astype(BF16), vj, NN)
                m_ref[hh] = m_new

        def step(j, carry):
            tile(j, False)
            return carry

        lax.fori_loop(0, i, step, 0)
        tile(i, True)
        first = lax.broadcasted_iota(jnp.int32, (t, LANES), 1) < VDIM
        outs = []
        for hh in range(hp):
            acc = acc_ref[hh]
            den = acc[:, VDIM:VDIM + 1]
            outs.append(acc / den)
            lse_ref[hh] = (m_ref[hh][:, :1] + jnp.log2(den)) * LN_2
        for pp in range(hp // 2):
            o_ref[:, pp * LANES:(pp + 1) * LANES] = jnp.where(
                first, outs[2 * pp], pltpu.roll(outs[2 * pp + 1], VDIM, 1))

    wide = hp * LANES
    return pl.pallas_call(
        body, name="mla_attn_fwd", grid=(h_ // hp, nq),
        in_specs=[pl.BlockSpec((t, wide), lambda g, i: (i, g)),
                  pl.BlockSpec((s, wide), lambda g, i: (0, g)),
                  pl.BlockSpec((s, wide), lambda g, i: (0, g))],
        out_specs=[pl.BlockSpec((t, hp * VDIM), lambda g, i: (i, g)),
                   pl.BlockSpec((hp, t, 1), lambda g, i: (g, i, 0))],
        out_shape=[jax.ShapeDtypeStruct((s, h_ * VDIM), F32), jax.ShapeDtypeStruct((h_, s, 1), F32)],
        scratch_shapes=[pltpu.VMEM((hp, t, LANES), F32), pltpu.VMEM((hp, t, LANES), F32)],
        compiler_params=pltpu.CompilerParams(
            dimension_semantics=("arbitrary", "arbitrary"), vmem_limit_bytes=VMEM_ATT),
    )(q, k, vx)


def _mla_bwd(q, k, vx, o, lse, do):
    s = q.shape[0]
    h_ = q.shape[1] // LANES
    t = min(MLA_BLOCK, s)
    nq = s // t
    scale = QK ** -0.5

    def body(q_ref, k_ref, v_ref, o_ref, lse_ref, do_ref, dq_ref, dk_ref, dv_ref, dq_acc):
        h, i = pl.program_id(0), pl.program_id(1)

        @pl.when(i == 0)
        def _():
            dk_ref[...] = jnp.zeros_like(dk_ref)
            dv_ref[...] = jnp.zeros_like(dv_ref)

        qv = q_ref[...]
        lane = lax.broadcasted_iota(jnp.int32, (t, LANES), 1)
        start = (h % 2) * VDIM
        mine = jnp.logical_and(lane >= start, lane < start + VDIM)
        do2 = do_ref[...]
        delta = _lanes(jnp.sum(jnp.where(mine, do2 * o_ref[...], 0.0), axis=1, keepdims=True))
        odd = (start + jnp.zeros_like(lane)) > 0
        do_head = jnp.where(odd, pltpu.roll(do2, VDIM, 1), do2)
        dov = jnp.where(lane < VDIM, do_head, 0.0).astype(BF16)
        lse = _lanes(lse_ref[0])
        dq_acc[...] = jnp.zeros_like(dq_acc)

        def tile(j, diagonal):
            off = pl.multiple_of(j * t, t)
            kj = k_ref[pl.ds(off, t), :]
            vj = v_ref[pl.ds(off, t), :]
            p = jnp.exp(_dot(qv, kj, NT) * scale - _wide(lse, t))
            if diagonal:
                row = lax.broadcasted_iota(jnp.int32, (t, t), 0)
                col = lax.broadcasted_iota(jnp.int32, (t, t), 1)
                p = jnp.where(_chunk_allowed(row, col), p, 0.0)
            dp = _dot(dov, vj, NT)
            ds = (p * (dp - _wide(delta, t)) * scale).astype(BF16)
            dq_acc[...] += _dot(ds, kj, NN)
            dk_ref[pl.ds(off, t), :] += _dot(ds, qv, TN)
            dv_ref[pl.ds(off, t), :] += _dot(p.astype(BF16), dov, TN)

        def step(j, carry):
            tile(j, False)
            return carry

        lax.fori_loop(0, i, step, 0)
        tile(i, True)
        dq_ref[...] = dq_acc[...]

    blk = pl.BlockSpec((t, LANES), lambda h, i: (i, h))
    full = pl.BlockSpec((s, LANES), lambda h, i: (0, h))
    pair = pl.BlockSpec((t, LANES), lambda h, i: (i, h // 2))
    stat = pl.BlockSpec((1, t, 1), lambda h, i: (h, i, 0))
    return pl.pallas_call(
        body, name="mla_attn_bwd", grid=(h_, nq),
        in_specs=[blk, full, full, pair, stat, pair],
        out_specs=[blk, full, full],
        out_shape=[jax.ShapeDtypeStruct((s, h_ * LANES), F32)] * 3,
        scratch_shapes=[pltpu.VMEM((t, LANES), F32)],
        compiler_params=pltpu.CompilerParams(
            dimension_semantics=("arbitrary", "arbitrary"), vmem_limit_bytes=VMEM_ATT),
    )(q, k, vx, o, lse, do)


def _assemble_kv(kvp, kr):
    half = kvp.shape[1] // 2
    ones = jnp.tile((jnp.arange(LANES) == VDIM).astype(F32), MLA_HEADS)
    k = kvp[:, :half] + jnp.tile(kr, (1, MLA_HEADS))
    return k.astype(BF16), (kvp[:, half:] + ones).astype(BF16)


@jax.custom_vjp
def mla_attention(q, kvp, kr):
    return _mla_fwd(q.astype(BF16), *_assemble_kv(kvp, kr))[0]


def _mla_attention_fwd(q, kvp, kr):
    qb = q.astype(BF16)
    kb, vb = _assemble_kv(kvp, kr)
    o, lse = _mla_fwd(qb, kb, vb)
    return o, (qb, kb, vb, o, lse)


def _mla_attention_bwd(res, do):
    qb, kb, vb, o, lse = res
    dq, dk, dvx = _mla_bwd(qb, kb, vb, o, lse, do)
    dkr = dk.reshape(dk.shape[0], MLA_HEADS, LANES).sum(axis=1)
    return dq, jnp.concatenate([dk, dvx], axis=1), dkr


mla_attention.defvjp(_mla_attention_fwd, _mla_attention_bwd)


def _add_tile(acc, res):
    return (acc + res,)


def _make_linear_res(tag, a_dtype=F32):
    def forward(x, a, slot, wb):
        ab = a.astype(BF16)
        y = _matmul(ab, wb, epilogue=_add_tile, extras=(x,), name=tag + "_fwd")[0]
        return y, (ab, wb)

    @jax.custom_vjp
    def f(x, a, slot, wb):
        return forward(x, a, slot, wb)[0]

    def bwd(res, dy):
        ab, wb = res
        dyb = dy.astype(BF16)
        da = _matmul(dyb, wb, tb=True, out_dtypes=(a_dtype,), name=tag + "_da")[0]
        dw = _matmul(ab, dyb, ta=True, name=tag + "_dw")[0]
        return dy, da, dw, jnp.zeros_like(wb)

    f.defvjp(forward, bwd)
    return f


def _make_norm_linear(tag, out_dtype=F32):
    def forward(x, g, slot, wb):
        hb = _rms_fwd(x, g, tag + "_norm")
        y = _matmul(hb, wb, out_dtypes=(out_dtype,), name=tag + "_fwd")[0]
        return y, (x, g, wb, hb)

    @jax.custom_vjp
    def f(x, g, slot, wb):
        return forward(x, g, slot, wb)[0]

    def bwd(res, dy):
        x, g, wb, hb = res
        dyb = dy.astype(BF16)
        dh = _matmul(dyb, wb, tb=True, name=tag + "_dh")[0]
        dw = _matmul(hb, dyb, ta=True, name=tag + "_dw")[0]
        dx, dg = _rms_bwd(x, g, dh, tag + "_norm_bwd")
        return dx, dg, dw, jnp.zeros_like(wb)

    f.defvjp(forward, bwd)
    return f


def _relu2_fwd(acc):
    r = jnp.maximum(acc, 0.0)
    return acc, r * r


def _relu2_bwd(acc, u):
    return (acc * (2.0 * jnp.maximum(u.astype(F32), 0.0)),)


def _make_mlp_res(tag):
    def forward(x, g, slot1, slot2, w1b, w2b):
        hb = _rms_fwd(x, g, tag + "_norm")
        u, act = _matmul(hb, w1b, out_dtypes=(BF16, BF16), epilogue=_relu2_fwd, name=tag + "_up")
        y = _matmul(act, w2b, epilogue=_add_tile, extras=(x,), name=tag + "_down")[0]
        return y, (x, g, w1b, w2b, hb, u, act)

    @jax.custom_vjp
    def f(x, g, slot1, slot2, w1b, w2b):
        return forward(x, g, slot1, slot2, w1b, w2b)[0]

    def bwd(res, dy):
        x, g, w1b, w2b, hb, u, act = res
        dyb = dy.astype(BF16)
        du = _matmul(dyb, w2b, tb=True, out_dtypes=(BF16,), epilogue=_relu2_bwd, extras=(u,), name=tag + "_du")[0]
        dw2 = _matmul(act, dyb, ta=True, name=tag + "_dw2")[0]
        dw1 = _matmul(hb, du, ta=True, name=tag + "_dw1")[0]
        dh = _matmul(du, w1b, tb=True, name=tag + "_dh")[0]
        dx, dg = _rms_bwd(x, g, dh, tag + "_norm_bwd", res=dy)
        return dx, dg, dw1, dw2, jnp.zeros_like(w1b), jnp.zeros_like(w2b)

    f.defvjp(forward, bwd)
    return f


def _make_rope(tag):
    @jax.custom_vjp
    def f(x, c, a, b):
        return _rope_apply(x, (c, a, b), False, tag + "_fwd")

    def fwd(x, c, a, b):
        return _rope_apply(x, (c, a, b), False, tag + "_fwd"), (c, a, b)

    def bwd(res, dy):
        c, a, b = res
        return _rope_apply(dy, (c, a, b), True, tag + "_bwd"), jnp.zeros_like(c), jnp.zeros_like(a), jnp.zeros_like(b)

    f.defvjp(fwd, bwd)
    return f


def _rope_tables(positions):
    half = ROPE // 2
    inv_freq = ROPE_THETA ** (-jnp.arange(0, ROPE, 2, dtype=F32) / ROPE)
    ang = positions.astype(F32)[:, None] * inv_freq
    cos, sin = jnp.cos(ang), jnp.sin(ang)
    s = positions.shape[0]
    one = lambda n: jnp.ones((s, n), F32)
    zero = lambda n: jnp.zeros((s, n), F32)

    def tables(before, after):
        return (jnp.concatenate([one(before), cos, cos, one(after)], axis=1),
                jnp.concatenate([zero(before), -sin, zero(half + after)], axis=1),
                jnp.concatenate([zero(before + half), sin, zero(after)], axis=1))

    return tables(NOPE, LANES - QK), tables(KV_RANK, DKV_PAD - KV_RANK - ROPE)


def _pad_heads(w, per):
    lead = w.shape[:-1]
    w = jnp.pad(w.reshape(lead + (MLA_HEADS, per)), [(0, 0)] * len(lead) + [(0, 0), (0, LANES - per)])
    return w.reshape(lead + (MLA_HEADS * LANES,))


def _split_kv_heads(w):
    w3 = w.reshape(w.shape[0], MLA_HEADS, NOPE + VDIM)
    return jnp.concatenate([_pad_heads(w3[..., :NOPE].reshape(w.shape[0], -1), NOPE),
                            _pad_heads(w3[..., NOPE:].reshape(w.shape[0], -1), VDIM)], axis=1)


def _trunk(x, slots, norms, wb, q_tabs, kv_tabs):
    kvp = kr = None
    for layer in range(DEPTH):
        if layer < N_A:
            qkv = _make_norm_linear(f"sb{layer}_qkv", BF16)(
                x, norms['attn_norm'][layer], slots['sb_w_qkv'][layer], wb['sb_w_qkv'][layer])
            o = sb_attention(qkv)
            x = _make_linear_res(f"sb{layer}_o", BF16)(x, o, slots['sb_w_o'][layer], wb['sb_w_o'][layer])
        else:
            j = layer - N_A
            if j == 0:
                pad = ((0, 0), (0, DKV_PAD - KV_RANK - ROPE))
                down = _make_norm_linear("kv_down")(
                    x, norms['kv_norm'], jnp.pad(slots['mla_w_dkv'][0], pad), jnp.pad(wb['mla_w_dkv'][0], pad))
                kvp = _make_norm_linear("kv_up")(
                    down[:, :KV_RANK], norms['mla_kv_lat_norm'],
                    _split_kv_heads(slots['mla_w_ukv'][0]), _split_kv_heads(wb['mla_w_ukv'][0]))
                k_rope = _make_rope("rope_k")(down, *kv_tabs)[:, KV_RANK:KV_RANK + ROPE]
                kr = jnp.pad(k_rope, ((0, 0), (NOPE, LANES - QK)))
            c_q = _make_norm_linear(f"mla{j}_dq")(
                x, norms['attn_norm'][layer], slots['mla_w_dq'][j], wb['mla_w_dq'][j])
            q = _make_norm_linear(f"mla{j}_uq")(
                c_q, norms['mla_q_lat_norm'][j], _pad_heads(slots['mla_w_uq'][j], QK), _pad_heads(wb['mla_w_uq'][j], QK))
            q = _make_rope(f"rope_q{j}")(q, *q_tabs)
            o = mla_attention(q, kvp, kr)
            x = _make_linear_res(f"mla{j}_o")(x, o, slots['mla_w_o'][j], wb['mla_w_o'][j])
        x = _make_mlp_res(f"mlp{layer}")(
            x, norms['mlp_norm'][layer], slots['mlp_w1'][layer], slots['mlp_w2'][layer],
            wb['mlp_w1'][layer], wb['mlp_w2'][layer])
    return x


def _merge_blocks(gathered, ax):
    shp = gathered.shape[1:]
    return jnp.moveaxis(gathered, 0, ax).reshape(shp[:ax] + (N_DEV * shp[ax],) + shp[ax + 1:])


def _split_blocks(full, ax):
    shp = full.shape
    return jnp.moveaxis(full.reshape(shp[:ax] + (N_DEV, shp[ax] // N_DEV) + shp[ax + 1:]), ax, 0)


def _pack(parts):
    flat = jnp.concatenate([p.reshape(-1) for p in parts])
    rows = -(-flat.shape[0] // PACK_COLS)
    rows = -(-rows // PACK_ROW_ALIGN) * PACK_ROW_ALIGN
    return jnp.pad(flat, (0, rows * PACK_COLS - flat.shape[0])).reshape(rows, PACK_COLS)


def _unpack(packed, shapes):
    flat = packed.reshape(-1)
    out, off = [], 0
    for shp in shapes:
        n = math.prod(shp)
        out.append(flat[off:off + n].reshape(shp))
        off += n
    return out


def _add_round(x, stage, core, name):
    _, r, c = x.shape
    tr = _tile(r, (256, 128, 64, 32, 16, 8))

    def body(core_ref, x_ref, s_ref, o_ref):
        o_ref[...] = (x_ref[0] + s_ref[...]).astype(o_ref.dtype)

    return pl.pallas_call(
        body, name=name,
        grid_spec=pltpu.PrefetchScalarGridSpec(
            num_scalar_prefetch=1, grid=(r // tr,),
            in_specs=[pl.BlockSpec((1, tr, c), lambda i, core_ref: (core_ref[0], i, 0)),
                      pl.BlockSpec((tr, c), lambda i, core_ref: (i, 0))],
            out_specs=pl.BlockSpec((tr, c), lambda i, core_ref: (i, 0))),
        out_shape=jax.ShapeDtypeStruct((r, c), BF16),
        compiler_params=pltpu.CompilerParams(dimension_semantics=("parallel",)),
    )(core, x, stage)


def _adamw_reduce(parts, w, m, v, name):
    n_parts, r, c = parts.shape
    tr = _tile(r, (128, 64, 32, 16, 8))
    bias1 = 1.0 - ADAM_B1 ** ADAM_STEP
    bias2 = 1.0 - ADAM_B2 ** ADAM_STEP

    def body(p_ref, w_ref, m_ref, v_ref, g_ref, d_ref, nm_ref, nv_ref):
        g = p_ref[0].astype(F32)
        for s in range(1, n_parts):
            g = g + p_ref[s].astype(F32)
        mn = ADAM_B1 * m_ref[...] + (1.0 - ADAM_B1) * g
        vn = ADAM_B2 * v_ref[...] + (1.0 - ADAM_B2) * (g * g)
        m_hat = mn / bias1
        v_hat = vn / bias2
        g_ref[...] = g
        d_ref[...] = -ADAM_LR * (m_hat / (jnp.sqrt(v_hat) + ADAM_EPS) + ADAM_WD * w_ref[...])
        nm_ref[...] = mn
        nv_ref[...] = vn

    blk = pl.BlockSpec((tr, c), lambda i: (i, 0))
    return pl.pallas_call(
        body, name=name, grid=(r // tr,),
        in_specs=[pl.BlockSpec((n_parts, tr, c), lambda i: (0, i, 0)), blk, blk, blk],
        out_specs=[blk] * 4,
        out_shape=[jax.ShapeDtypeStruct((r, c), F32)] * 4,
        compiler_params=pltpu.CompilerParams(dimension_semantics=("parallel",), vmem_limit_bytes=VMEM_MM),
    )(parts, w, m, v)


def kernel(x, positions, attn_norm, mlp_norm, sb_w_qkv, sb_w_o, kv_norm, mla_w_dkv, mla_kv_lat_norm, mla_w_ukv, mla_w_dq, mla_q_lat_norm, mla_w_uq, mla_w_o, mlp_w1, mlp_w2, final_norm, loss_target, m_attn_norm, m_mlp_norm, m_sb_w_qkv, m_sb_w_o, m_kv_norm, m_mla_w_dkv, m_mla_kv_lat_norm, m_mla_w_ukv, m_mla_w_dq, m_mla_q_lat_norm, m_mla_w_uq, m_mla_w_o, m_mlp_w1, m_mlp_w2, m_final_norm, v_attn_norm, v_mlp_norm, v_sb_w_qkv, v_sb_w_o, v_kv_norm, v_mla_w_dkv, v_mla_kv_lat_norm, v_mla_w_ukv, v_mla_w_dq, v_mla_q_lat_norm, v_mla_w_uq, v_mla_w_o, v_mlp_w1, v_mlp_w2, v_final_norm):
    weights = dict(attn_norm=attn_norm, mlp_norm=mlp_norm, sb_w_qkv=sb_w_qkv, sb_w_o=sb_w_o, kv_norm=kv_norm,
                   mla_w_dkv=mla_w_dkv, mla_kv_lat_norm=mla_kv_lat_norm, mla_w_ukv=mla_w_ukv, mla_w_dq=mla_w_dq,
                   mla_q_lat_norm=mla_q_lat_norm, mla_w_uq=mla_w_uq, mla_w_o=mla_w_o, mlp_w1=mlp_w1, mlp_w2=mlp_w2,
                   final_norm=final_norm)
    mom_m = dict(attn_norm=m_attn_norm, mlp_norm=m_mlp_norm, sb_w_qkv=m_sb_w_qkv, sb_w_o=m_sb_w_o, kv_norm=m_kv_norm,
                 mla_w_dkv=m_mla_w_dkv, mla_kv_lat_norm=m_mla_kv_lat_norm, mla_w_ukv=m_mla_w_ukv, mla_w_dq=m_mla_w_dq,
                 mla_q_lat_norm=m_mla_q_lat_norm, mla_w_uq=m_mla_w_uq, mla_w_o=m_mla_w_o, mlp_w1=m_mlp_w1,
                 mlp_w2=m_mlp_w2, final_norm=m_final_norm)
    mom_v = dict(attn_norm=v_attn_norm, mlp_norm=v_mlp_norm, sb_w_qkv=v_sb_w_qkv, sb_w_o=v_sb_w_o, kv_norm=v_kv_norm,
                 mla_w_dkv=v_mla_w_dkv, mla_kv_lat_norm=v_mla_kv_lat_norm, mla_w_ukv=v_mla_w_ukv, mla_w_dq=v_mla_w_dq,
                 mla_q_lat_norm=v_mla_q_lat_norm, mla_w_uq=v_mla_w_uq, mla_w_o=v_mla_w_o, mlp_w1=v_mlp_w1,
                 mlp_w2=v_mlp_w2, final_norm=v_final_norm)
    sharded_names = [n for n, _ in SHARDED]
    repl_shapes = [tuple(weights[n].shape) for n in REPLICATED]

    gathered = _all_gather([weights[n].astype(BF16) for n in sharded_names], "gather_weights")
    wb, slots = {}, {}
    for (n, ax), g in zip(SHARDED, gathered):
        full = _merge_blocks(g, ax)
        layers = [full[l] for l in range(full.shape[0])] if full.ndim == 3 else [full]
        wb[n] = layers
        slots[n] = [jnp.zeros(w.shape, F32) for w in layers]
    norms = {n: weights[n] for n in REPLICATED if n != 'final_norm'}

    q_tabs, kv_tabs = _rope_tables(positions[0])
    x_last, pullback = jax.vjp(lambda xx, ss, nn: _trunk(xx, ss, nn, wb, q_tabs, kv_tabs), x[0], slots, norms)
    loss_part, dx_last, d_final = _loss_head(x_last, final_norm, loss_target[0])
    dx, d_slots, d_norms = pullback(dx_last)
    d_norms = dict(d_norms)
    d_norms['final_norm'] = d_final
    loss = lax.psum(loss_part[0, 0], ("x", "y", "c"))

    core = lax.axis_index("c").astype(jnp.int32).reshape(1)
    halves, dims = [], []
    for n, ax in SHARDED:
        full = jnp.stack(d_slots[n]) if weights[n].ndim == 3 else d_slots[n][0]
        blocks = _split_blocks(full, ax)
        two_d = (math.prod(blocks.shape[1:-1]), blocks.shape[-1])
        dims.append(two_d)
        halves.append(jnp.moveaxis(blocks.reshape((N_DEV // 2, 2) + two_d), 1, 0))
    staged = _pair_exchange(halves, "pair_grads")
    sums = []
    for n, h, st, (r, c) in zip(sharded_names, halves, staged, dims):
        rows = (N_DEV // 2) * r
        sums.append(_add_round(h.reshape(2, rows, c), st.reshape(rows, c), core, "pair_sum_" + n).reshape(st.shape))
    received = _chip_exchange(sums, "scatter_grads")
    repl_parts = _all_gather([_pack([d_norms[n] for n in REPLICATED])], "gather_norm_grads")[0]

    results = {kind: {} for kind in ("grad", "delta", "new_m", "new_v")}
    for n, parts, two_d in zip(sharded_names, received, dims):
        shp = weights[n].shape
        res = _adamw_reduce(parts, weights[n].reshape(two_d), mom_m[n].reshape(two_d),
                            mom_v[n].reshape(two_d), "adamw_" + n)
        for kind, a in zip(results, res):
            results[kind][n] = a.reshape(shp)
    res = _adamw_reduce(repl_parts, _pack([weights[n] for n in REPLICATED]), _pack([mom_m[n] for n in REPLICATED]),
                        _pack([mom_v[n] for n in REPLICATED]), "adamw_replicated")
    for kind, a in zip(results, res):
        results[kind].update(zip(REPLICATED, _unpack(a, repl_shapes)))

    out = [loss, dx[None]]
    for kind in ("grad", "delta", "new_m", "new_v"):
        out += [results[kind][n] for n in WEIGHT_ORDER]
    return tuple(out)
```

```python
import math

import jax
import jax.numpy as jnp
from jax import lax
from jax.experimental import pallas as pl
from jax.experimental.pallas import tpu as pltpu

F32 = jnp.float32
BF16 = jnp.bfloat16
MESH = pl.DeviceIdType.MESH

N_DEV = 8
DEPTH = 4
N_A = 2
SB_HEADS = 16
SB_HD = 64
MLA_HEADS = 16
NOPE = 64
ROPE = 32
VDIM = 64
QK = NOPE + ROPE
KV_RANK = 256
DKV_PAD = 384
LANES = 128
CHUNK_SHIFT = 6
ROPE_THETA = 10000.0
EPS = 1e-6
SB_BLOCK = 256
MLA_BLOCK = 512
MLA_GROUP = 4
PACK_COLS = 1024
PACK_ROW_ALIGN = 16
EXP_ZERO = -104.0
NEG_BIG = -1e30
LOG2_E = 1.4426950408889634
LN_2 = 0.6931471805599453
VMEM_ATT = 56 * 1024 * 1024
VMEM_MM = 48 * 1024 * 1024

ADAM_LR = 0.001
ADAM_B1 = 0.9
ADAM_B2 = 0.999
ADAM_EPS = 1e-08
ADAM_WD = 0.01
ADAM_STEP = 10

WEIGHT_ORDER = ['attn_norm', 'mlp_norm', 'sb_w_qkv', 'sb_w_o', 'kv_norm', 'mla_w_dkv', 'mla_kv_lat_norm',
                'mla_w_ukv', 'mla_w_dq', 'mla_q_lat_norm', 'mla_w_uq', 'mla_w_o', 'mlp_w1', 'mlp_w2', 'final_norm']
SHARDED = [('sb_w_qkv', 2), ('sb_w_o', 1), ('mla_w_dkv', 0), ('mla_w_ukv', 1), ('mla_w_dq', 1),
           ('mla_w_uq', 2), ('mla_w_o', 1), ('mlp_w1', 2), ('mlp_w2', 1)]
REPLICATED = ['attn_norm', 'mlp_norm', 'kv_norm', 'mla_kv_lat_norm', 'mla_q_lat_norm', 'final_norm']


def _tile(dim, prefs=(512, 384, 256, 128, 64, 32, 16, 8)):
    for t in prefs:
        if dim % t == 0:
            return t
    return dim


def _dot(a, b, dims):
    return lax.dot_general(a, b, (dims, ((), ())), preferred_element_type=F32)


NN = ((1,), (0,))
NT = ((1,), (1,))
TN = ((0,), (0,))


def _all_gather(shards, name):
    n_t = len(shards)

    def body(*refs):
        x_refs, out_refs = refs[:n_t], refs[n_t:2 * n_t]
        send_sems, recv_sems, local_sems = refs[2 * n_t:]
        x, y, c = lax.axis_index("x"), lax.axis_index("y"), lax.axis_index("c")
        me, sibling = (x, y, c), (x, y, 1 - c)
        chips = [(1 - x, y), (x, 1 - y), (1 - x, 1 - y)]

        def slot(t, px, py, pc):
            return out_refs[t].at[4 * px + 2 * py + pc]

        def copy(t, k, block, to, src=None):
            return pltpu.make_async_remote_copy(
                src_ref=slot(t, *block) if src is None else src, dst_ref=slot(t, *block),
                send_sem=send_sems.at[7 * t + k], recv_sem=recv_sems.at[7 * t + k],
                device_id=to, device_id_type=MESH)

        mine = [pltpu.make_async_copy(x_refs[t], slot(t, *me), local_sems.at[t]) for t in range(n_t)]
        for cp in mine:
            cp.start()
        first = []
        for t in range(n_t):
            first.append(copy(t, 0, me, sibling, src=x_refs[t]))
            first += [copy(t, 1 + j, me, (*chip, c), src=x_refs[t]) for j, chip in enumerate(chips)]
        for cp in first:
            cp.start()
        passed = []
        for t in range(n_t):
            for j, chip in enumerate(chips):
                copy(t, 1 + j, (*chip, c), me).wait_recv()
                onward = copy(t, 4 + j, (*chip, c), sibling)
                onward.start()
                passed.append(onward)
        for t in range(n_t):
            copy(t, 0, sibling, me).wait_recv()
            for j, chip in enumerate(chips):
                copy(t, 4 + j, (*chip, 1 - c), me).wait_recv()
        for cp in first + passed:
            cp.wait_send()
        for cp in mine:
            cp.wait()

    any_spec = pl.BlockSpec(memory_space=pl.ANY)
    return pl.pallas_call(
        body, name=name,
        out_shape=[jax.ShapeDtypeStruct((N_DEV,) + tuple(s.shape), s.dtype) for s in shards],
        in_specs=[any_spec] * n_t, out_specs=[any_spec] * n_t,
        scratch_shapes=[pltpu.SemaphoreType.DMA((7 * n_t,)), pltpu.SemaphoreType.DMA((7 * n_t,)),
                        pltpu.SemaphoreType.DMA((n_t,))],
    )(*shards)


def _pair_exchange(xs, name):
    n_t = len(xs)

    def body(*refs):
        x_refs, out_refs = refs[:n_t], refs[n_t:2 * n_t]
        send_sems, recv_sems = refs[2 * n_t:]
        x, y, c = lax.axis_index("x"), lax.axis_index("y"), lax.axis_index("c")
        copies = [pltpu.make_async_remote_copy(
            src_ref=x_refs[t].at[1 - c], dst_ref=out_refs[t], send_sem=send_sems.at[t], recv_sem=recv_sems.at[t],
            device_id=(x, y, 1 - c), device_id_type=MESH) for t in range(n_t)]
        for cp in copies:
            cp.start()
        for cp in copies:
            cp.wait()

    any_spec = pl.BlockSpec(memory_space=pl.ANY)
    return pl.pallas_call(
        body, name=name,
        out_shape=[jax.ShapeDtypeStruct(x.shape[1:], x.dtype) for x in xs],
        in_specs=[any_spec] * n_t, out_specs=[any_spec] * n_t,
        scratch_shapes=[pltpu.SemaphoreType.DMA((n_t,)), pltpu.SemaphoreType.DMA((n_t,))],
    )(*xs)


def _chip_exchange(ps, name):
    n_t = len(ps)

    def body(*refs):
        p_refs, out_refs = refs[:n_t], refs[n_t:2 * n_t]
        send_sems, recv_sems, local_sems = refs[2 * n_t:]
        mx, my, mc = lax.axis_index("x"), lax.axis_index("y"), lax.axis_index("c")
        me = 2 * mx + my
        mine = [pltpu.make_async_copy(p_refs[t].at[me], out_refs[t].at[me], local_sems.at[t]) for t in range(n_t)]
        for cp in mine:
            cp.start()
        copies = []
        for k in range(1, 4):
            px = 1 - mx if (k >> 1) & 1 else mx
            py = 1 - my if k & 1 else my
            peer = 2 * px + py
            for t in range(n_t):
                copies.append(pltpu.make_async_remote_copy(
                    src_ref=p_refs[t].at[peer], dst_ref=out_refs[t].at[me],
                    send_sem=send_sems.at[3 * t + k - 1], recv_sem=recv_sems.at[3 * t + k - 1],
                    device_id=(px, py, mc), device_id_type=MESH))
        for cp in copies:
            cp.start()
        for cp in copies:
            cp.wait_send()
        for cp in copies:
            cp.wait_recv()
        for cp in mine:
            cp.wait()

    any_spec = pl.BlockSpec(memory_space=pl.ANY)
    return pl.pallas_call(
        body, name=name,
        out_shape=[jax.ShapeDtypeStruct(p.shape, p.dtype) for p in ps],
        in_specs=[any_spec] * n_t, out_specs=[any_spec] * n_t,
        scratch_shapes=[pltpu.SemaphoreType.DMA((3 * n_t,)), pltpu.SemaphoreType.DMA((3 * n_t,)),
                        pltpu.SemaphoreType.DMA((n_t,))],
    )(*ps)


def _matmul(a, b, *, ta=False, tb=False, out_dtypes=(F32,), epilogue=None, extras=(), name):
    if ta:
        kdim, m = a.shape
    else:
        m, kdim = a.shape
    if tb:
        n, kb = b.shape
    else:
        kb, n = b.shape
    assert kdim == kb, (a.shape, b.shape, ta, tb)
    big = (1024, 768, 512, 384, 256, 128, 64, 32, 16, 8)
    tm, tn = _tile(m, big), _tile(n, big)
    tk = kdim if kdim <= 2048 else _tile(kdim, (2048, 1024, 512, 256, 128))
    nk = kdim // tk
    n_extra, n_out = len(extras), len(out_dtypes)
    a_spec = pl.BlockSpec((tk, tm), lambda i, j, k: (k, i)) if ta else pl.BlockSpec((tm, tk), lambda i, j, k: (i, k))
    b_spec = pl.BlockSpec((tn, tk), lambda i, j, k: (j, k)) if tb else pl.BlockSpec((tk, tn), lambda i, j, k: (k, j))
    tile_spec = pl.BlockSpec((tm, tn), lambda i, j, k: (i, j))
    dims = ((0,) if ta else (1,), (1,) if tb else (0,))

    def finish(acc, extra_refs, out_refs):
        outs = (acc,) if epilogue is None else epilogue(acc, *[r[...] for r in extra_refs])
        for o_ref, o in zip(out_refs, outs):
            o_ref[...] = o.astype(o_ref.dtype)

    def body_one(a_ref, b_ref, *rest):
        acc = _dot(a_ref[...].astype(BF16), b_ref[...].astype(BF16), dims)
        finish(acc, rest[:n_extra], rest[n_extra:n_extra + n_out])

    def body_acc(a_ref, b_ref, *rest):
        acc_ref = rest[-1]
        k = pl.program_id(2)

        @pl.when(k == 0)
        def _():
            acc_ref[...] = jnp.zeros_like(acc_ref)

        acc_ref[...] += _dot(a_ref[...].astype(BF16), b_ref[...].astype(BF16), dims)

        @pl.when(k == nk - 1)
        def _():
            finish(acc_ref[...], rest[:n_extra], rest[n_extra:n_extra + n_out])

    return pl.pallas_call(
        body_one if nk == 1 else body_acc, name=name, grid=(m // tm, n // tn, nk),
        in_specs=[a_spec, b_spec] + [tile_spec] * n_extra,
        out_specs=[tile_spec] * n_out,
        out_shape=[jax.ShapeDtypeStruct((m, n), dt) for dt in out_dtypes],
        scratch_shapes=[] if nk == 1 else [pltpu.VMEM((tm, tn), F32)],
        compiler_params=pltpu.CompilerParams(
            dimension_semantics=("parallel", "parallel", "arbitrary"), vmem_limit_bytes=VMEM_MM),
    )(a, b, *extras)


def _rms_fwd(x, g, name):
    m, d = x.shape
    tm = _tile(m, (512, 256, 128, 64, 32, 16, 8))

    def body(x_ref, g_ref, y_ref):
        xv = x_ref[...]
        r = lax.rsqrt(jnp.mean(xv * xv, axis=-1, keepdims=True) + EPS)
        y_ref[...] = (xv * r * g_ref[...]).astype(y_ref.dtype)

    return pl.pallas_call(
        body, name=name, grid=(m // tm,),
        in_specs=[pl.BlockSpec((tm, d), lambda i: (i, 0)), pl.BlockSpec((1, d), lambda i: (0, 0))],
        out_specs=pl.BlockSpec((tm, d), lambda i: (i, 0)),
        out_shape=jax.ShapeDtypeStruct((m, d), BF16),
        compiler_params=pltpu.CompilerParams(dimension_semantics=("parallel",)),
    )(x, g.reshape(1, d))


def _rms_bwd(x, g, dy, name, res=None):
    m, d = x.shape
    tm = _tile(m, (512, 256, 128, 64, 32, 16, 8))
    has_res = res is not None

    def body(x_ref, g_ref, dy_ref, *rest):
        dx_ref, dg_ref = rest[-2:]
        xv = x_ref[...]
        dyv = dy_ref[...]
        r = lax.rsqrt(jnp.mean(xv * xv, axis=-1, keepdims=True) + EPS)
        xh = xv * r
        t = dyv * g_ref[...]
        dx = r * (t - xh * jnp.mean(t * xh, axis=-1, keepdims=True))
        dx_ref[...] = dx + rest[0][...] if has_res else dx

        @pl.when(pl.program_id(0) == 0)
        def _():
            dg_ref[...] = jnp.zeros_like(dg_ref)

        dg_ref[...] += jnp.sum(dyv * xh, axis=0, keepdims=True)

    row_spec = pl.BlockSpec((tm, d), lambda i: (i, 0))
    vec_spec = pl.BlockSpec((1, d), lambda i: (0, 0))
    dx, dg = pl.pallas_call(
        body, name=name, grid=(m // tm,),
        in_specs=[row_spec, vec_spec, row_spec] + ([row_spec] if has_res else []),
        out_specs=[row_spec, vec_spec],
        out_shape=[jax.ShapeDtypeStruct((m, d), F32), jax.ShapeDtypeStruct((1, d), F32)],
        compiler_params=pltpu.CompilerParams(dimension_semantics=("arbitrary",)),
    )(x, g.reshape(1, d), dy, *((res,) if has_res else ()))
    return dx, dg.reshape(d)


def _loss_head(x, g, target):
    m, d = x.shape
    tm = _tile(m, (512, 256, 128, 64, 32, 16, 8))

    def body(x_ref, g_ref, t_ref, loss_ref, dx_ref, dg_ref):
        xv = x_ref[...]
        gv = g_ref[...]
        r = lax.rsqrt(jnp.mean(xv * xv, axis=-1, keepdims=True) + EPS)
        xh = xv * r
        err = xh * gv - t_ref[...]
        row_loss = jnp.mean(err * err, axis=-1, keepdims=True)
        dyv = err * (1.0 / d)
        t = dyv * gv
        dx_ref[...] = r * (t - xh * jnp.mean(t * xh, axis=-1, keepdims=True))

        @pl.when(pl.program_id(0) == 0)
        def _():
            dg_ref[...] = jnp.zeros_like(dg_ref)
            loss_ref[...] = jnp.zeros_like(loss_ref)

        dg_ref[...] += jnp.sum(dyv * xh, axis=0, keepdims=True)
        loss_ref[...] += 0.5 * jnp.sum(row_loss, axis=0, keepdims=True)

    loss, dx, dg = pl.pallas_call(
        body, name="loss_head", grid=(m // tm,),
        in_specs=[pl.BlockSpec((tm, d), lambda i: (i, 0)), pl.BlockSpec((1, d), lambda i: (0, 0)),
                  pl.BlockSpec((tm, d), lambda i: (i, 0))],
        out_specs=[pl.BlockSpec((1, 1), lambda i: (0, 0)), pl.BlockSpec((tm, d), lambda i: (i, 0)),
                   pl.BlockSpec((1, d), lambda i: (0, 0))],
        out_shape=[jax.ShapeDtypeStruct((1, 1), F32), jax.ShapeDtypeStruct((m, d), F32),
                   jax.ShapeDtypeStruct((1, d), F32)],
        compiler_params=pltpu.CompilerParams(dimension_semantics=("arbitrary",)),
    )(x, g.reshape(1, d), target)
    return loss, dx, dg.reshape(d)


def _rope_apply(x, tabs, transpose, name):
    m, w = x.shape
    wt = tabs[0].shape[1]
    reps = w // wt
    half = ROPE // 2
    tm = _tile(m, (256, 128, 64, 32, 16, 8))

    def body(x_ref, c_ref, a_ref, b_ref, y_ref):
        xv = x_ref[...]

        def wide(t_ref):
            t = t_ref[...]
            return t if reps == 1 else jnp.concatenate([t] * reps, axis=1)

        c, a, b = wide(c_ref), wide(a_ref), wide(b_ref)
        if transpose:
            y = xv * c + pltpu.roll(xv * a, half, 1) + pltpu.roll(xv * b, w - half, 1)
        else:
            y = xv * c + pltpu.roll(xv, w - half, 1) * a + pltpu.roll(xv, half, 1) * b
        y_ref[...] = y

    x_spec = pl.BlockSpec((tm, w), lambda i: (i, 0))
    t_spec = pl.BlockSpec((tm, wt), lambda i: (i, 0))
    return pl.pallas_call(
        body, name=name, grid=(m // tm,),
        in_specs=[x_spec, t_spec, t_spec, t_spec], out_specs=x_spec,
        out_shape=jax.ShapeDtypeStruct((m, w), F32),
        compiler_params=pltpu.CompilerParams(dimension_semantics=("parallel",)),
    )(x, *tabs)


def _log_sigmoid_pair(z):
    a = jnp.minimum(z, 0.0) - jnp.log(1.0 + jnp.exp(-jnp.abs(z)))
    return a, a - z


def _split_bf16(x):
    hi = x.astype(BF16)
    return hi, (x - hi.astype(F32)).astype(BF16)


def _wide(v, width):
    return v if width == LANES else jnp.concatenate([v] * (width // LANES), axis=1)


def _lanes(col):
    return jnp.broadcast_to(col, (col.shape[0], LANES))


def _rope_head(x, c, a, b):
    half = ROPE // 2
    return x * c + pltpu.roll(x, LANES - half, 1) * a + pltpu.roll(x, half, 1) * b


def _rope_head_t(dy, c, a, b):
    half = ROPE // 2
    return dy * c + pltpu.roll(dy * a, half, 1) + pltpu.roll(dy * b, LANES - half, 1)


def _sb_weights(qh, k2, valid):
    a, b = _log_sigmoid_pair(_dot(qh, k2, NT))
    return a, b if valid is None else jnp.where(valid, b, 0.0)


def _sb_fwd(qkv):
    s = qkv.shape[0]
    t = min(SB_BLOCK, s)
    nq = s // t
    npair = SB_HEADS // 2
    scale = SB_HD ** -0.5

    def body(q_ref, k_ref, v_ref, o_ref, bta_ref, btb_ref, js_ref, acc_ref, ra_ref, rb_ref):
        p, i = pl.program_id(0), pl.program_id(1)
        q2 = q_ref[...] * scale
        first = lax.broadcasted_iota(jnp.int32, (t, LANES), 1) < SB_HD
        heads = (jnp.where(first, q2, jnp.zeros_like(q2)), jnp.where(first, jnp.zeros_like(q2), q2))
        row = lax.broadcasted_iota(jnp.int32, (t, t), 0)
        col = lax.broadcasted_iota(jnp.int32, (t, t), 1)
        later = jnp.where(row > col, 1.0, 0.0).astype(BF16)
        acc_ref[...] = jnp.zeros_like(acc_ref)
        ra_ref[...] = jnp.zeros_like(ra_ref)
        rb_ref[...] = jnp.zeros_like(rb_ref)

        def block(j, diagonal):
            off = pl.multiple_of(j * t, t)
            k2 = k_ref[pl.ds(off, t), :]
            v2 = v_ref[pl.ds(off, t), :]
            valid = col < row if diagonal else None
            outs, rmax = [], None
            for qh, r_ref in zip(heads, (ra_ref, rb_ref)):
                a, b = _sb_weights(qh, k2, valid)
                bh, bl = _split_bf16(b)
                inner = _dot(bh, later, NN) + _dot(bl, later, NN)
                r = r_ref[...]
                w = jnp.exp(a + inner + _wide(r, t))
                if diagonal:
                    w = jnp.where(valid, w, 0.0)
                outs.append(_dot(w.astype(BF16), v2, NN))
                rn = r + jnp.sum(b, axis=1, keepdims=True)
                r_ref[...] = rn
                rmax = jnp.max(rn) if rmax is None else jnp.maximum(rmax, jnp.max(rn))
            acc_ref[...] += jnp.where(first, outs[0], outs[1])
            return rmax

        def cond(carry):
            j, rmax = carry
            return jnp.logical_and(j >= 0, rmax > EXP_ZERO)

        def step(carry):
            j, _ = carry
            return j - 1, block(j, False)

        jend, _ = lax.while_loop(cond, step, (i - 1, block(i, True)))
        o_ref[...] = acc_ref[...].astype(o_ref.dtype)
        bta_ref[0] = ra_ref[...][:, :1]
        btb_ref[0] = rb_ref[...][:, :1]
        js_ref[p, i] = (jend + 1).astype(F32)

    stat = pl.BlockSpec((1, t, 1), lambda p, i: (p, i, 0))
    return pl.pallas_call(
        body, name="sb_attn_fwd", grid=(npair, nq),
        in_specs=[pl.BlockSpec((t, LANES), lambda p, i: (i, p)),
                  pl.BlockSpec((s, LANES), lambda p, i: (0, npair + p)),
                  pl.BlockSpec((s, LANES), lambda p, i: (0, 2 * npair + p))],
        out_specs=[pl.BlockSpec((t, LANES), lambda p, i: (i, p)), stat, stat,
                   pl.BlockSpec(memory_space=pltpu.SMEM)],
        out_shape=[jax.ShapeDtypeStruct((s, SB_HEADS * SB_HD), BF16), jax.ShapeDtypeStruct((npair, s, 1), F32),
                   jax.ShapeDtypeStruct((npair, s, 1), F32), jax.ShapeDtypeStruct((npair, nq), F32)],
        scratch_shapes=[pltpu.VMEM((t, LANES), F32)] * 3,
        compiler_params=pltpu.CompilerParams(
            dimension_semantics=("arbitrary", "arbitrary"), vmem_limit_bytes=VMEM_ATT),
    )(qkv, qkv, qkv)


def _sb_bwd(qkv, do, btot_a, btot_b, jstart):
    s = qkv.shape[0]
    t = min(SB_BLOCK, s)
    nq = s // t
    npair = SB_HEADS // 2
    scale = SB_HD ** -0.5

    def body(js_ref, q_ref, k_ref, v_ref, do_ref, bta_ref, btb_ref, dq_ref, dk_ref, dv_ref,
             dq_acc, pa_ref, pb_ref, ga_ref, gb_ref, dk_acc, dv_acc):
        p, i = pl.program_id(0), pl.program_id(1)

        @pl.when(i == 0)
        def _():
            dk_acc[...] = jnp.zeros_like(dk_acc)
            dv_acc[...] = jnp.zeros_like(dv_acc)

        q2 = q_ref[...] * scale
        do2 = do_ref[...]
        first = lax.broadcasted_iota(jnp.int32, (t, LANES), 1) < SB_HD
        zero = jnp.zeros_like(q2)
        q_heads = (jnp.where(first, q2, zero), jnp.where(first, zero, q2))
        do_heads = (jnp.where(first, do2, zero), jnp.where(first, zero, do2))
        bts = (_wide(_lanes(bta_ref[0]), t), _wide(_lanes(btb_ref[0]), t))
        row = lax.broadcasted_iota(jnp.int32, (t, t), 0)
        col = lax.broadcasted_iota(jnp.int32, (t, t), 1)
        upto = jnp.where(row <= col, 1.0, 0.0).astype(BF16)
        before = jnp.where(row < col, 1.0, 0.0).astype(BF16)
        dq_acc[...] = jnp.zeros_like(dq_acc)
        for r in (pa_ref, pb_ref, ga_ref, gb_ref):
            r[...] = jnp.zeros_like(r)
        j0 = jnp.clip(js_ref[p, i].astype(jnp.int32), 0, i)

        def block(j, diagonal):
            off = pl.multiple_of(j * t, t)
            k2 = k_ref[pl.ds(off, t), :]
            v2 = v_ref[pl.ds(off, t), :]
            valid = col < row if diagonal else None
            dqs, dk2, dv2 = [], None, None
            for qh, doh, bt, p_ref, g_ref in zip(q_heads, do_heads, bts, (pa_ref, pb_ref), (ga_ref, gb_ref)):
                a, b = _sb_weights(qh, k2, valid)
                bh, bl = _split_bf16(b)
                pin = _dot(bh, upto, NN) + _dot(bl, upto, NN)
                surv = bt - (_wide(p_ref[...], t) + pin)
                w = jnp.exp(a + surv)
                if diagonal:
                    w = jnp.where(valid, w, 0.0)
                g = w * _dot(doh, v2, NT)
                gh, gl = _split_bf16(g)
                gsum = _wide(g_ref[...], t) + _dot(gh, before, NN) + _dot(gl, before, NN)
                beta = jnp.exp(a)
                dz = g * (1.0 - beta) - gsum * beta
                if diagonal:
                    dz = jnp.where(valid, dz, 0.0)
                dz = dz.astype(BF16)
                dqs.append(_dot(dz, k2, NN))
                dkh = _dot(dz, qh, TN)
                dvh = _dot(w.astype(BF16), doh, TN)
                dk2 = dkh if dk2 is None else dk2 + dkh
                dv2 = dvh if dv2 is None else dv2 + dvh
                p_ref[...] += jnp.sum(b, axis=1, keepdims=True)
                g_ref[...] += jnp.sum(g, axis=1, keepdims=True)
            dq_acc[...] += jnp.where(first, dqs[0], dqs[1])
            dk_acc[pl.ds(off, t), :] += dk2
            dv_acc[pl.ds(off, t), :] += dv2

        def step(j, carry):
            block(j, False)
            return carry

        lax.fori_loop(j0, i, step, 0)
        block(i, True)
        dq_ref[...] = (dq_acc[...] * scale).astype(dq_ref.dtype)

        @pl.when(i == nq - 1)
        def _():
            dk_ref[...] = dk_acc[...].astype(dk_ref.dtype)
            dv_ref[...] = dv_acc[...].astype(dv_ref.dtype)

    blk = pl.BlockSpec((t, LANES), lambda p, i: (i, p))
    full = pl.BlockSpec((s, LANES), lambda p, i: (0, p))
    stat = pl.BlockSpec((1, t, 1), lambda p, i: (p, i, 0))
    vec = pltpu.VMEM((t, LANES), F32)
    return pl.pallas_call(
        body, name="sb_attn_bwd", grid=(npair, nq),
        in_specs=[pl.BlockSpec(memory_space=pltpu.SMEM), blk,
                  pl.BlockSpec((s, LANES), lambda p, i: (0, npair + p)),
                  pl.BlockSpec((s, LANES), lambda p, i: (0, 2 * npair + p)), blk, stat, stat],
        out_specs=[blk, full, full],
        out_shape=[jax.ShapeDtypeStruct((s, SB_HEADS * SB_HD), BF16)] * 3,
        scratch_shapes=[pltpu.VMEM((t, LANES), F32), vec, vec, vec, vec,
                        pltpu.VMEM((s, LANES), F32), pltpu.VMEM((s, LANES), F32)],
        compiler_params=pltpu.CompilerParams(
            dimension_semantics=("arbitrary", "arbitrary"), vmem_limit_bytes=VMEM_ATT),
    )(jstart, qkv, qkv, qkv, do, btot_a, btot_b)


@jax.custom_vjp
def sb_attention(qkv):
    return _sb_fwd(qkv)[0]


def _sb_attention_fwd(qkv):
    o, btot_a, btot_b, jstart = _sb_fwd(qkv)
    return o, (qkv, btot_a, btot_b, jstart)


def _sb_attention_bwd(res, do):
    qkv, btot_a, btot_b, jstart = res
    dq, dk, dv = _sb_bwd(qkv, do, btot_a, btot_b, jstart)
    return (jnp.concatenate([dq, dk, dv], axis=1),)


sb_attention.defvjp(_sb_attention_fwd, _sb_attention_bwd)


def _chunk_allowed(row, col):
    return (col >> CHUNK_SHIFT) <= (row >> CHUNK_SHIFT)


def _mla_fwd(q, k, vx, tabs):
    s = q.shape[0]
    h_ = q.shape[1] // LANES
    t = min(MLA_BLOCK, s)
    nq = s // t
    scale = QK ** -0.5
    hp = MLA_GROUP

    def body(q_ref, k_ref, v_ref, c_ref, a_ref, b_ref, o_ref, lse_ref, acc_ref, m_ref):
        i = pl.program_id(1)
        acc_ref[...] = jnp.zeros_like(acc_ref)
        m_ref[...] = jnp.full_like(m_ref, NEG_BIG)
        rot = (c_ref[...], a_ref[...], b_ref[...])
        qs = [_rope_head(q_ref[:, hh * LANES:(hh + 1) * LANES], *rot).astype(BF16) for hh in range(hp)]

        def tile(j, diagonal):
            off = pl.multiple_of(j * t, t)
            for hh in range(hp):
                lanes = slice(hh * LANES, (hh + 1) * LANES)
                kj = k_ref[pl.ds(off, t), lanes]
                vj = v_ref[pl.ds(off, t), lanes]
                sc = _dot(qs[hh], kj, NT) * (scale * LOG2_E)
                if diagonal:
                    row = lax.broadcasted_iota(jnp.int32, (t, t), 0)
                    col = lax.broadcasted_iota(jnp.int32, (t, t), 1)
                    sc = jnp.where(_chunk_allowed(row, col), sc, NEG_BIG)
                m_old = m_ref[hh]
                m_new = jnp.maximum(m_old, jnp.max(sc, axis=1, keepdims=True))
                p = jnp.exp2(sc - _wide(m_new, t))
                acc_ref[hh] = jnp.exp2(m_old - m_new) * acc_ref[hh] + _dot(p.astype(BF16), vj, NN)
                m_ref[hh] = m_new

        def step(j, carry):
            tile(j, False)
            return carry

        lax.fori_loop(0, i, step, 0)
        tile(i, True)
        first = lax.broadcasted_iota(jnp.int32, (t, LANES), 1) < VDIM
        outs = []
        for hh in range(hp):
            acc = acc_ref[hh]
            den = acc[:, VDIM:VDIM + 1]
            outs.append(acc / den)
            lse_ref[hh] = (m_ref[hh][:, :1] + jnp.log2(den)) * LN_2
        for pp in range(hp // 2):
            o_ref[:, pp * LANES:(pp + 1) * LANES] = jnp.where(
                first, outs[2 * pp], pltpu.roll(outs[2 * pp + 1], VDIM, 1))

    wide = hp * LANES
    return pl.pallas_call(
        body, name="mla_attn_fwd", grid=(h_ // hp, nq),
        in_specs=[pl.BlockSpec((t, wide), lambda g, i: (i, g)),
                  pl.BlockSpec((s, wide), lambda g, i: (0, g)),
                  pl.BlockSpec((s, wide), lambda g, i: (0, g))] + [pl.BlockSpec((t, LANES), lambda g, i: (i, 0))] * 3,
        out_specs=[pl.BlockSpec((t, hp * VDIM), lambda g, i: (i, g)),
                   pl.BlockSpec((hp, t, 1), lambda g, i: (g, i, 0))],
        out_shape=[jax.ShapeDtypeStruct((s, h_ * VDIM), F32), jax.ShapeDtypeStruct((h_, s, 1), F32)],
        scratch_shapes=[pltpu.VMEM((hp, t, LANES), F32), pltpu.VMEM((hp, t, LANES), F32)],
        compiler_params=pltpu.CompilerParams(
            dimension_semantics=("arbitrary", "arbitrary"), vmem_limit_bytes=VMEM_ATT),
    )(q, k, vx, *tabs)


def _mla_bwd(q, k, vx, tabs, o, lse, do):
    s = q.shape[0]
    h_ = q.shape[1] // LANES
    t = min(MLA_BLOCK, s)
    nq = s // t
    scale = QK ** -0.5

    def body(q_ref, k_ref, v_ref, c_ref, a_ref, b_ref, o_ref, lse_ref, do_ref, dq_ref, dk_ref, dv_ref, dq_acc):
        h, i = pl.program_id(0), pl.program_id(1)
        rot = (c_ref[...], a_ref[...], b_ref[...])

        @pl.when(i == 0)
        def _():
            dk_ref[...] = jnp.zeros_like(dk_ref)
            dv_ref[...] = jnp.zeros_like(dv_ref)

        qv = _rope_head(q_ref[...], *rot).astype(BF16)
        lane = lax.broadcasted_iota(jnp.int32, (t, LANES), 1)
        start = (h % 2) * VDIM
        mine = jnp.logical_and(lane >= start, lane < start + VDIM)
        do2 = do_ref[...]
        delta = _lanes(jnp.sum(jnp.where(mine, do2 * o_ref[...], 0.0), axis=1, keepdims=True))
        odd = (start + jnp.zeros_like(lane)) > 0
        do_head = jnp.where(odd, pltpu.roll(do2, VDIM, 1), do2)
        dov = jnp.where(lane < VDIM, do_head, 0.0).astype(BF16)
        lse = _lanes(lse_ref[0])
        dq_acc[...] = jnp.zeros_like(dq_acc)

        def tile(j, diagonal):
            off = pl.multiple_of(j * t, t)
            kj = k_ref[pl.ds(off, t), :]
            vj = v_ref[pl.ds(off, t), :]
            p = jnp.exp(_dot(qv, kj, NT) * scale - _wide(lse, t))
            if diagonal:
                row = lax.broadcasted_iota(jnp.int32, (t, t), 0)
                col = lax.broadcasted_iota(jnp.int32, (t, t), 1)
                p = jnp.where(_chunk_allowed(row, col), p, 0.0)
            dp = _dot(dov, vj, NT)
            ds = (p * (dp - _wide(delta, t)) * scale).astype(BF16)
            dq_acc[...] += _dot(ds, kj, NN)
            dk_ref[pl.ds(off, t), :] += _dot(ds, qv, TN)
            dv_ref[pl.ds(off, t), :] += _dot(p.astype(BF16), dov, TN)

        def step(j, carry):
            tile(j, False)
            return carry

        lax.fori_loop(0, i, step, 0)
        tile(i, True)
        dq_ref[...] = _rope_head_t(dq_acc[...], *rot)

    blk = pl.BlockSpec((t, LANES), lambda h, i: (i, h))
    full = pl.BlockSpec((s, LANES), lambda h, i: (0, h))
    pair = pl.BlockSpec((t, LANES), lambda h, i: (i, h // 2))
    stat = pl.BlockSpec((1, t, 1), lambda h, i: (h, i, 0))
    return pl.pallas_call(
        body, name="mla_attn_bwd", grid=(h_, nq),
        in_specs=[blk, full, full] + [pl.BlockSpec((t, LANES), lambda h, i: (i, 0))] * 3 + [pair, stat, pair],
        out_specs=[blk, full, full],
        out_shape=[jax.ShapeDtypeStruct((s, h_ * LANES), F32)] * 3,
        scratch_shapes=[pltpu.VMEM((t, LANES), F32)],
        compiler_params=pltpu.CompilerParams(
            dimension_semantics=("arbitrary", "arbitrary"), vmem_limit_bytes=VMEM_ATT),
    )(q, k, vx, *tabs, o, lse, do)


def _assemble_kv(kvp, kr):
    half = kvp.shape[1] // 2
    ones = jnp.tile((jnp.arange(LANES) == VDIM).astype(F32), MLA_HEADS)
    k = kvp[:, :half] + jnp.tile(kr, (1, MLA_HEADS))
    return k.astype(BF16), (kvp[:, half:] + ones).astype(BF16)


@jax.custom_vjp
def mla_attention(q, kvp, kr, c, a, b):
    return _mla_fwd(q, *_assemble_kv(kvp, kr), (c, a, b))[0]


def _mla_attention_fwd(q, kvp, kr, c, a, b):
    kb, vb = _assemble_kv(kvp, kr)
    o, lse = _mla_fwd(q, kb, vb, (c, a, b))
    return o, (q, kb, vb, (c, a, b), o, lse)


def _mla_attention_bwd(res, do):
    q, kb, vb, tabs, o, lse = res
    dq, dk, dvx = _mla_bwd(q, kb, vb, tabs, o, lse, do)
    dkr = dk.reshape(dk.shape[0], MLA_HEADS, LANES).sum(axis=1)
    return (dq, jnp.concatenate([dk, dvx], axis=1), dkr) + tuple(jnp.zeros_like(t) for t in tabs)


mla_attention.defvjp(_mla_attention_fwd, _mla_attention_bwd)


def _add_tile(acc, res):
    return (acc + res,)


def _make_linear_res(tag, a_dtype=F32):
    def forward(x, a, slot, wb):
        ab = a.astype(BF16)
        y = _matmul(ab, wb, epilogue=_add_tile, extras=(x,), name=tag + "_fwd")[0]
        return y, (ab, wb)

    @jax.custom_vjp
    def f(x, a, slot, wb):
        return forward(x, a, slot, wb)[0]

    def bwd(res, dy):
        ab, wb = res
        dyb = dy.astype(BF16)
        da = _matmul(dyb, wb, tb=True, out_dtypes=(a_dtype,), name=tag + "_da")[0]
        dw = _matmul(ab, dyb, ta=True, name=tag + "_dw")[0]
        return dy, da, dw, jnp.zeros_like(wb)

    f.defvjp(forward, bwd)
    return f


def _make_norm_linear(tag, out_dtype=F32):
    def forward(x, g, slot, wb):
        hb = _rms_fwd(x, g, tag + "_norm")
        y = _matmul(hb, wb, out_dtypes=(out_dtype,), name=tag + "_fwd")[0]
        return y, (x, g, wb, hb)

    @jax.custom_vjp
    def f(x, g, slot, wb):
        return forward(x, g, slot, wb)[0]

    def bwd(res, dy):
        x, g, wb, hb = res
        dyb = dy.astype(BF16)
        dh = _matmul(dyb, wb, tb=True, name=tag + "_dh")[0]
        dw = _matmul(hb, dyb, ta=True, name=tag + "_dw")[0]
        dx, dg = _rms_bwd(x, g, dh, tag + "_norm_bwd")
        return dx, dg, dw, jnp.zeros_like(wb)

    f.defvjp(forward, bwd)
    return f


def _relu2_fwd(acc):
    r = jnp.maximum(acc, 0.0)
    return acc, r * r


def _relu2_bwd(acc, u):
    return (acc * (2.0 * jnp.maximum(u.astype(F32), 0.0)),)


def _make_mlp_res(tag):
    def forward(x, g, slot1, slot2, w1b, w2b):
        hb = _rms_fwd(x, g, tag + "_norm")
        u, act = _matmul(hb, w1b, out_dtypes=(BF16, BF16), epilogue=_relu2_fwd, name=tag + "_up")
        y = _matmul(act, w2b, epilogue=_add_tile, extras=(x,), name=tag + "_down")[0]
        return y, (x, g, w1b, w2b, hb, u, act)

    @jax.custom_vjp
    def f(x, g, slot1, slot2, w1b, w2b):
        return forward(x, g, slot1, slot2, w1b, w2b)[0]

    def bwd(res, dy):
        x, g, w1b, w2b, hb, u, act = res
        dyb = dy.astype(BF16)
        du = _matmul(dyb, w2b, tb=True, out_dtypes=(BF16,), epilogue=_relu2_bwd, extras=(u,), name=tag + "_du")[0]
        dw2 = _matmul(act, dyb, ta=True, name=tag + "_dw2")[0]
        dw1 = _matmul(hb, du, ta=True, name=tag + "_dw1")[0]
        dh = _matmul(du, w1b, tb=True, name=tag + "_dh")[0]
        dx, dg = _rms_bwd(x, g, dh, tag + "_norm_bwd", res=dy)
        return dx, dg, dw1, dw2, jnp.zeros_like(w1b), jnp.zeros_like(w2b)

    f.defvjp(forward, bwd)
    return f


def _make_rope(tag):
    @jax.custom_vjp
    def f(x, c, a, b):
        return _rope_apply(x, (c, a, b), False, tag + "_fwd")

    def fwd(x, c, a, b):
        return _rope_apply(x, (c, a, b), False, tag + "_fwd"), (c, a, b)

    def bwd(res, dy):
        c, a, b = res
        return _rope_apply(dy, (c, a, b), True, tag + "_bwd"), jnp.zeros_like(c), jnp.zeros_like(a), jnp.zeros_like(b)

    f.defvjp(fwd, bwd)
    return f


def _rope_tables(positions):
    half = ROPE // 2
    inv_freq = ROPE_THETA ** (-jnp.arange(0, ROPE, 2, dtype=F32) / ROPE)
    ang = positions.astype(F32)[:, None] * inv_freq
    cos, sin = jnp.cos(ang), jnp.sin(ang)
    s = positions.shape[0]
    one = lambda n: jnp.ones((s, n), F32)
    zero = lambda n: jnp.zeros((s, n), F32)

    def tables(before, after):
        return (jnp.concatenate([one(before), cos, cos, one(after)], axis=1),
                jnp.concatenate([zero(before), -sin, zero(half + after)], axis=1),
                jnp.concatenate([zero(before + half), sin, zero(after)], axis=1))

    return tables(NOPE, LANES - QK), tables(KV_RANK, DKV_PAD - KV_RANK - ROPE)


def _pad_heads(w, per):
    lead = w.shape[:-1]
    w = jnp.pad(w.reshape(lead + (MLA_HEADS, per)), [(0, 0)] * len(lead) + [(0, 0), (0, LANES - per)])
    return w.reshape(lead + (MLA_HEADS * LANES,))


def _split_kv_heads(w):
    w3 = w.reshape(w.shape[0], MLA_HEADS, NOPE + VDIM)
    return jnp.concatenate([_pad_heads(w3[..., :NOPE].reshape(w.shape[0], -1), NOPE),
                            _pad_heads(w3[..., NOPE:].reshape(w.shape[0], -1), VDIM)], axis=1)


def _trunk(x, slots, norms, wb, q_tabs, kv_tabs):
    kvp = kr = None
    for layer in range(DEPTH):
        if layer < N_A:
            qkv = _make_norm_linear(f"sb{layer}_qkv", BF16)(
                x, norms['attn_norm'][layer], slots['sb_w_qkv'][layer], wb['sb_w_qkv'][layer])
            o = sb_attention(qkv)
            x = _make_linear_res(f"sb{layer}_o", BF16)(x, o, slots['sb_w_o'][layer], wb['sb_w_o'][layer])
        else:
            j = layer - N_A
            if j == 0:
                pad = ((0, 0), (0, DKV_PAD - KV_RANK - ROPE))
                down = _make_norm_linear("kv_down")(
                    x, norms['kv_norm'], jnp.pad(slots['mla_w_dkv'][0], pad), jnp.pad(wb['mla_w_dkv'][0], pad))
                kvp = _make_norm_linear("kv_up")(
                    down[:, :KV_RANK], norms['mla_kv_lat_norm'],
                    _split_kv_heads(slots['mla_w_ukv'][0]), _split_kv_heads(wb['mla_w_ukv'][0]))
                k_rope = _make_rope("rope_k")(down, *kv_tabs)[:, KV_RANK:KV_RANK + ROPE]
                kr = jnp.pad(k_rope, ((0, 0), (NOPE, LANES - QK)))
            c_q = _make_norm_linear(f"mla{j}_dq")(
                x, norms['attn_norm'][layer], slots['mla_w_dq'][j], wb['mla_w_dq'][j])
            q = _make_norm_linear(f"mla{j}_uq")(
                c_q, norms['mla_q_lat_norm'][j], _pad_heads(slots['mla_w_uq'][j], QK), _pad_heads(wb['mla_w_uq'][j], QK))
            o = mla_attention(q, kvp, kr, *q_tabs)
            x = _make_linear_res(f"mla{j}_o")(x, o, slots['mla_w_o'][j], wb['mla_w_o'][j])
        x = _make_mlp_res(f"mlp{layer}")(
            x, norms['mlp_norm'][layer], slots['mlp_w1'][layer], slots['mlp_w2'][layer],
            wb['mlp_w1'][layer], wb['mlp_w2'][layer])
    return x


def _merge_blocks(gathered, ax):
    shp = gathered.shape[1:]
    return jnp.moveaxis(gathered, 0, ax).reshape(shp[:ax] + (N_DEV * shp[ax],) + shp[ax + 1:])


def _split_blocks(full, ax):
    shp = full.shape
    return jnp.moveaxis(full.reshape(shp[:ax] + (N_DEV, shp[ax] // N_DEV) + shp[ax + 1:]), ax, 0)


def _pack(parts):
    flat = jnp.concatenate([p.reshape(-1) for p in parts])
    rows = -(-flat.shape[0] // PACK_COLS)
    rows = -(-rows // PACK_ROW_ALIGN) * PACK_ROW_ALIGN
    return jnp.pad(flat, (0, rows * PACK_COLS - flat.shape[0])).reshape(rows, PACK_COLS)


def _unpack(packed, shapes):
    flat = packed.reshape(-1)
    out, off = [], 0
    for shp in shapes:
        n = math.prod(shp)
        out.append(flat[off:off + n].reshape(shp))
        off += n
    return out


def _add_round(x, stage, core, name):
    _, r, c = x.shape
    tr = _tile(r, (256, 128, 64, 32, 16, 8))

    def body(core_ref, x_ref, s_ref, o_ref):
        o_ref[...] = (x_ref[0] + s_ref[...]).astype(o_ref.dtype)

    return pl.pallas_call(
        body, name=name,
        grid_spec=pltpu.PrefetchScalarGridSpec(
            num_scalar_prefetch=1, grid=(r // tr,),
            in_specs=[pl.BlockSpec((1, tr, c), lambda i, core_ref: (core_ref[0], i, 0)),
                      pl.BlockSpec((tr, c), lambda i, core_ref: (i, 0))],
            out_specs=pl.BlockSpec((tr, c), lambda i, core_ref: (i, 0))),
        out_shape=jax.ShapeDtypeStruct((r, c), BF16),
        compiler_params=pltpu.CompilerParams(dimension_semantics=("parallel",)),
    )(core, x, stage)


def _adamw_reduce(parts, w, m, v, name):
    n_parts, r, c = parts.shape
    tr = _tile(r, (128, 64, 32, 16, 8))
    bias1 = 1.0 - ADAM_B1 ** ADAM_STEP
    bias2 = 1.0 - ADAM_B2 ** ADAM_STEP

    def body(p_ref, w_ref, m_ref, v_ref, g_ref, d_ref, nm_ref, nv_ref):
        g = p_ref[0].astype(F32)
        for s in range(1, n_parts):
            g = g + p_ref[s].astype(F32)
        mn = ADAM_B1 * m_ref[...] + (1.0 - ADAM_B1) * g
        vn = ADAM_B2 * v_ref[...] + (1.0 - ADAM_B2) * (g * g)
        m_hat = mn / bias1
        v_hat = vn / bias2
        g_ref[...] = g
        d_ref[...] = -ADAM_LR * (m_hat / (jnp.sqrt(v_hat) + ADAM_EPS) + ADAM_WD * w_ref[...])
        nm_ref[...] = mn
        nv_ref[...] = vn

    blk = pl.BlockSpec((tr, c), lambda i: (i, 0))
    return pl.pallas_call(
        body, name=name, grid=(r // tr,),
        in_specs=[pl.BlockSpec((n_parts, tr, c), lambda i: (0, i, 0)), blk, blk, blk],
        out_specs=[blk] * 4,
        out_shape=[jax.ShapeDtypeStruct((r, c), F32)] * 4,
        compiler_params=pltpu.CompilerParams(dimension_semantics=("parallel",), vmem_limit_bytes=VMEM_MM),
    )(parts, w, m, v)


def kernel(x, positions, attn_norm, mlp_norm, sb_w_qkv, sb_w_o, kv_norm, mla_w_dkv, mla_kv_lat_norm, mla_w_ukv, mla_w_dq, mla_q_lat_norm, mla_w_uq, mla_w_o, mlp_w1, mlp_w2, final_norm, loss_target, m_attn_norm, m_mlp_norm, m_sb_w_qkv, m_sb_w_o, m_kv_norm, m_mla_w_dkv, m_mla_kv_lat_norm, m_mla_w_ukv, m_mla_w_dq, m_mla_q_lat_norm, m_mla_w_uq, m_mla_w_o, m_mlp_w1, m_mlp_w2, m_final_norm, v_attn_norm, v_mlp_norm, v_sb_w_qkv, v_sb_w_o, v_kv_norm, v_mla_w_dkv, v_mla_kv_lat_norm, v_mla_w_ukv, v_mla_w_dq, v_mla_q_lat_norm, v_mla_w_uq, v_mla_w_o, v_mlp_w1, v_mlp_w2, v_final_norm):
    weights = dict(attn_norm=attn_norm, mlp_norm=mlp_norm, sb_w_qkv=sb_w_qkv, sb_w_o=sb_w_o, kv_norm=kv_norm,
                   mla_w_dkv=mla_w_dkv, mla_kv_lat_norm=mla_kv_lat_norm, mla_w_ukv=mla_w_ukv, mla_w_dq=mla_w_dq,
                   mla_q_lat_norm=mla_q_lat_norm, mla_w_uq=mla_w_uq, mla_w_o=mla_w_o, mlp_w1=mlp_w1, mlp_w2=mlp_w2,
                   final_norm=final_norm)
    mom_m = dict(attn_norm=m_attn_norm, mlp_norm=m_mlp_norm, sb_w_qkv=m_sb_w_qkv, sb_w_o=m_sb_w_o, kv_norm=m_kv_norm,
                 mla_w_dkv=m_mla_w_dkv, mla_kv_lat_norm=m_mla_kv_lat_norm, mla_w_ukv=m_mla_w_ukv, mla_w_dq=m_mla_w_dq,
                 mla_q_lat_norm=m_mla_q_lat_norm, mla_w_uq=m_mla_w_uq, mla_w_o=m_mla_w_o, mlp_w1=m_mlp_w1,
                 mlp_w2=m_mlp_w2, final_norm=m_final_norm)
    mom_v = dict(attn_norm=v_attn_norm, mlp_norm=v_mlp_norm, sb_w_qkv=v_sb_w_qkv, sb_w_o=v_sb_w_o, kv_norm=v_kv_norm,
                 mla_w_dkv=v_mla_w_dkv, mla_kv_lat_norm=v_mla_kv_lat_norm, mla_w_ukv=v_mla_w_ukv, mla_w_dq=v_mla_w_dq,
                 mla_q_lat_norm=v_mla_q_lat_norm, mla_w_uq=v_mla_w_uq, mla_w_o=v_mla_w_o, mlp_w1=v_mlp_w1,
                 mlp_w2=v_mlp_w2, final_norm=v_final_norm)
    sharded_names = [n for n, _ in SHARDED]
    repl_shapes = [tuple(weights[n].shape) for n in REPLICATED]

    gathered = _all_gather([weights[n].astype(BF16) for n in sharded_names], "gather_weights")
    wb, slots = {}, {}
    for (n, ax), g in zip(SHARDED, gathered):
        full = _merge_blocks(g, ax)
        layers = [full[l] for l in range(full.shape[0])] if full.ndim == 3 else [full]
        wb[n] = layers
        slots[n] = [jnp.zeros(w.shape, F32) for w in layers]
    norms = {n: weights[n] for n in REPLICATED if n != 'final_norm'}

    q_tabs, kv_tabs = _rope_tables(positions[0])
    x_last, pullback = jax.vjp(lambda xx, ss, nn: _trunk(xx, ss, nn, wb, q_tabs, kv_tabs), x[0], slots, norms)
    loss_part, dx_last, d_final = _loss_head(x_last, final_norm, loss_target[0])
    dx, d_slots, d_norms = pullback(dx_last)
    d_norms = dict(d_norms)
    d_norms['final_norm'] = d_final
    loss = lax.psum(loss_part[0, 0], ("x", "y", "c"))

    core = lax.axis_index("c").astype(jnp.int32).reshape(1)
    halves, dims = [], []
    for n, ax in SHARDED:
        full = jnp.stack(d_slots[n]) if weights[n].ndim == 3 else d_slots[n][0]
        blocks = _split_blocks(full, ax)
        two_d = (math.prod(blocks.shape[1:-1]), blocks.shape[-1])
        dims.append(two_d)
        halves.append(jnp.moveaxis(blocks.reshape((N_DEV // 2, 2) + two_d), 1, 0))
    staged = _pair_exchange(halves, "pair_grads")
    sums = []
    for n, h, st, (r, c) in zip(sharded_names, halves, staged, dims):
        rows = (N_DEV // 2) * r
        sums.append(_add_round(h.reshape(2, rows, c), st.reshape(rows, c), core, "pair_sum_" + n).reshape(st.shape))
    received = _chip_exchange(sums, "scatter_grads")
    repl_parts = _all_gather([_pack([d_norms[n] for n in REPLICATED])], "gather_norm_grads")[0]

    results = {kind: {} for kind in ("grad", "delta", "new_m", "new_v")}
    for n, parts, two_d in zip(sharded_names, received, dims):
        shp = weights[n].shape
        res = _adamw_reduce(parts, weights[n].reshape(two_d), mom_m[n].reshape(two_d),
                            mom_v[n].reshape(two_d), "adamw_" + n)
        for kind, a in zip(results, res):
            results[kind][n] = a.reshape(shp)
    res = _adamw_reduce(repl_parts, _pack([weights[n] for n in REPLICATED]), _pack([mom_m[n] for n in REPLICATED]),
                        _pack([mom_v[n] for n in REPLICATED]), "adamw_replicated")
    for kind, a in zip(results, res):
        results[kind].update(zip(REPLICATED, _unpack(a, repl_shapes)))

    out = [loss, dx[None]]
    for kind in ("grad", "delta", "new_m", "new_v"):
        out += [results[kind][n] for n in WEIGHT_ORDER]
    return tuple(out)
```

```python
import math

import jax
import jax.numpy as jnp
from jax import lax
from jax.experimental import pallas as pl
from jax.experimental.pallas import tpu as pltpu

F32 = jnp.float32
BF16 = jnp.bfloat16
MESH = pl.DeviceIdType.MESH

N_DEV = 8
DEPTH = 4
N_A = 2
SB_HEADS = 16
SB_HD = 64
MLA_HEADS = 16
NOPE = 64
ROPE = 32
VDIM = 64
QK = NOPE + ROPE
KV_RANK = 256
DKV_PAD = 384
LANES = 128
CHUNK_SHIFT = 6
ROPE_THETA = 10000.0
EPS = 1e-6
SB_BLOCK = 256
MLA_BLOCK = 512
MLA_GROUP = 4
PACK_COLS = 1024
PACK_ROW_ALIGN = 16
EXP_ZERO = -104.0
NEG_BIG = -1e30
LOG2_E = 1.4426950408889634
LN_2 = 0.6931471805599453
VMEM_ATT = 56 * 1024 * 1024
VMEM_MM = 48 * 1024 * 1024

ADAM_LR = 0.001
ADAM_B1 = 0.9
ADAM_B2 = 0.999
ADAM_EPS = 1e-08
ADAM_WD = 0.01
ADAM_STEP = 10

WEIGHT_ORDER = ['attn_norm', 'mlp_norm', 'sb_w_qkv', 'sb_w_o', 'kv_norm', 'mla_w_dkv', 'mla_kv_lat_norm',
                'mla_w_ukv', 'mla_w_dq', 'mla_q_lat_norm', 'mla_w_uq', 'mla_w_o', 'mlp_w1', 'mlp_w2', 'final_norm']
SHARDED = [('sb_w_qkv', 2), ('sb_w_o', 1), ('mla_w_dkv', 0), ('mla_w_ukv', 1), ('mla_w_dq', 1),
           ('mla_w_uq', 2), ('mla_w_o', 1), ('mlp_w1', 2), ('mlp_w2', 1)]
REPLICATED = ['attn_norm', 'mlp_norm', 'kv_norm', 'mla_kv_lat_norm', 'mla_q_lat_norm', 'final_norm']


def _tile(dim, prefs=(512, 384, 256, 128, 64, 32, 16, 8)):
    for t in prefs:
        if dim % t == 0:
            return t
    return dim


def _dot(a, b, dims):
    return lax.dot_general(a, b, (dims, ((), ())), preferred_element_type=F32)


NN = ((1,), (0,))
NT = ((1,), (1,))
TN = ((0,), (0,))


def _all_gather(shards, name):
    n_t = len(shards)

    def body(*refs):
        x_refs, out_refs = refs[:n_t], refs[n_t:2 * n_t]
        send_sems, recv_sems, local_sems = refs[2 * n_t:]
        x, y, c = lax.axis_index("x"), lax.axis_index("y"), lax.axis_index("c")
        me, sibling = (x, y, c), (x, y, 1 - c)
        chips = [(1 - x, y), (x, 1 - y), (1 - x, 1 - y)]

        def slot(t, px, py, pc):
            return out_refs[t].at[4 * px + 2 * py + pc]

        def copy(t, k, block, to, src=None):
            return pltpu.make_async_remote_copy(
                src_ref=slot(t, *block) if src is None else src, dst_ref=slot(t, *block),
                send_sem=send_sems.at[7 * t + k], recv_sem=recv_sems.at[7 * t + k],
                device_id=to, device_id_type=MESH)

        mine = [pltpu.make_async_copy(x_refs[t], slot(t, *me), local_sems.at[t]) for t in range(n_t)]
        for cp in mine:
            cp.start()
        first = []
        for t in range(n_t):
            first.append(copy(t, 0, me, sibling, src=x_refs[t]))
            first += [copy(t, 1 + j, me, (*chip, c), src=x_refs[t]) for j, chip in enumerate(chips)]
        for cp in first:
            cp.start()
        passed = []
        for t in range(n_t):
            for j, chip in enumerate(chips):
                copy(t, 1 + j, (*chip, c), me).wait_recv()
                onward = copy(t, 4 + j, (*chip, c), sibling)
                onward.start()
                passed.append(onward)
        for t in range(n_t):
            copy(t, 0, sibling, me).wait_recv()
            for j, chip in enumerate(chips):
                copy(t, 4 + j, (*chip, 1 - c), me).wait_recv()
        for cp in first + passed:
            cp.wait_send()
        for cp in mine:
            cp.wait()

    any_spec = pl.BlockSpec(memory_space=pl.ANY)
    return pl.pallas_call(
        body, name=name,
        out_shape=[jax.ShapeDtypeStruct((N_DEV,) + tuple(s.shape), s.dtype) for s in shards],
        in_specs=[any_spec] * n_t, out_specs=[any_spec] * n_t,
        scratch_shapes=[pltpu.SemaphoreType.DMA((7 * n_t,)), pltpu.SemaphoreType.DMA((7 * n_t,)),
                        pltpu.SemaphoreType.DMA((n_t,))],
    )(*shards)


def _pair_exchange(xs, name):
    n_t = len(xs)

    def body(*refs):
        x_refs, out_refs = refs[:n_t], refs[n_t:2 * n_t]
        send_sems, recv_sems = refs[2 * n_t:]
        x, y, c = lax.axis_index("x"), lax.axis_index("y"), lax.axis_index("c")
        copies = [pltpu.make_async_remote_copy(
            src_ref=x_refs[t].at[1 - c], dst_ref=out_refs[t], send_sem=send_sems.at[t], recv_sem=recv_sems.at[t],
            device_id=(x, y, 1 - c), device_id_type=MESH) for t in range(n_t)]
        for cp in copies:
            cp.start()
        for cp in copies:
            cp.wait()

    any_spec = pl.BlockSpec(memory_space=pl.ANY)
    return pl.pallas_call(
        body, name=name,
        out_shape=[jax.ShapeDtypeStruct(x.shape[1:], x.dtype) for x in xs],
        in_specs=[any_spec] * n_t, out_specs=[any_spec] * n_t,
        scratch_shapes=[pltpu.SemaphoreType.DMA((n_t,)), pltpu.SemaphoreType.DMA((n_t,))],
    )(*xs)


def _chip_exchange(ps, name):
    n_t = len(ps)

    def body(*refs):
        p_refs, out_refs = refs[:n_t], refs[n_t:2 * n_t]
        send_sems, recv_sems, local_sems = refs[2 * n_t:]
        mx, my, mc = lax.axis_index("x"), lax.axis_index("y"), lax.axis_index("c")
        me = 2 * mx + my
        mine = [pltpu.make_async_copy(p_refs[t].at[me], out_refs[t].at[me], local_sems.at[t]) for t in range(n_t)]
        for cp in mine:
            cp.start()
        copies = []
        for k in range(1, 4):
            px = 1 - mx if (k >> 1) & 1 else mx
            py = 1 - my if k & 1 else my
            peer = 2 * px + py
            for t in range(n_t):
                copies.append(pltpu.make_async_remote_copy(
                    src_ref=p_refs[t].at[peer], dst_ref=out_refs[t].at[me],
                    send_sem=send_sems.at[3 * t + k - 1], recv_sem=recv_sems.at[3 * t + k - 1],
                    device_id=(px, py, mc), device_id_type=MESH))
        for cp in copies:
            cp.start()
        for cp in copies:
            cp.wait_send()
        for cp in copies:
            cp.wait_recv()
        for cp in mine:
            cp.wait()

    any_spec = pl.BlockSpec(memory_space=pl.ANY)
    return pl.pallas_call(
        body, name=name,
        out_shape=[jax.ShapeDtypeStruct(p.shape, p.dtype) for p in ps],
        in_specs=[any_spec] * n_t, out_specs=[any_spec] * n_t,
        scratch_shapes=[pltpu.SemaphoreType.DMA((3 * n_t,)), pltpu.SemaphoreType.DMA((3 * n_t,)),
                        pltpu.SemaphoreType.DMA((n_t,))],
    )(*ps)


def _matmul(a, b, *, ta=False, tb=False, out_dtypes=(F32,), epilogue=None, extras=(), name):
    if ta:
        kdim, m = a.shape
    else:
        m, kdim = a.shape
    if tb:
        n, kb = b.shape
    else:
        kb, n = b.shape
    assert kdim == kb, (a.shape, b.shape, ta, tb)
    big = (1024, 768, 512, 384, 256, 128, 64, 32, 16, 8)
    tm, tn = _tile(m, big), _tile(n, big)
    tk = kdim if kdim <= 2048 else _tile(kdim, (2048, 1024, 512, 256, 128))
    nk = kdim // tk
    n_extra, n_out = len(extras), len(out_dtypes)
    a_spec = pl.BlockSpec((tk, tm), lambda i, j, k: (k, i)) if ta else pl.BlockSpec((tm, tk), lambda i, j, k: (i, k))
    b_spec = pl.BlockSpec((tn, tk), lambda i, j, k: (j, k)) if tb else pl.BlockSpec((tk, tn), lambda i, j, k: (k, j))
    tile_spec = pl.BlockSpec((tm, tn), lambda i, j, k: (i, j))
    dims = ((0,) if ta else (1,), (1,) if tb else (0,))

    def finish(acc, extra_refs, out_refs):
        outs = (acc,) if epilogue is None else epilogue(acc, *[r[...] for r in extra_refs])
        for o_ref, o in zip(out_refs, outs):
            o_ref[...] = o.astype(o_ref.dtype)

    def body_one(a_ref, b_ref, *rest):
        acc = _dot(a_ref[...].astype(BF16), b_ref[...].astype(BF16), dims)
        finish(acc, rest[:n_extra], rest[n_extra:n_extra + n_out])

    def body_acc(a_ref, b_ref, *rest):
        acc_ref = rest[-1]
        k = pl.program_id(2)

        @pl.when(k == 0)
        def _():
            acc_ref[...] = jnp.zeros_like(acc_ref)

        acc_ref[...] += _dot(a_ref[...].astype(BF16), b_ref[...].astype(BF16), dims)

        @pl.when(k == nk - 1)
        def _():
            finish(acc_ref[...], rest[:n_extra], rest[n_extra:n_extra + n_out])

    return pl.pallas_call(
        body_one if nk == 1 else body_acc, name=name, grid=(m // tm, n // tn, nk),
        in_specs=[a_spec, b_spec] + [tile_spec] * n_extra,
        out_specs=[tile_spec] * n_out,
        out_shape=[jax.ShapeDtypeStruct((m, n), dt) for dt in out_dtypes],
        scratch_shapes=[] if nk == 1 else [pltpu.VMEM((tm, tn), F32)],
        compiler_params=pltpu.CompilerParams(
            dimension_semantics=("parallel", "parallel", "arbitrary"), vmem_limit_bytes=VMEM_MM),
    )(a, b, *extras)


def _rms_fwd(x, g, name):
    m, d = x.shape
    tm = _tile(m, (512, 256, 128, 64, 32, 16, 8))

    def body(x_ref, g_ref, y_ref):
        xv = x_ref[...]
        r = lax.rsqrt(jnp.mean(xv * xv, axis=-1, keepdims=True) + EPS)
        y_ref[...] = (xv * r * g_ref[...]).astype(y_ref.dtype)

    return pl.pallas_call(
        body, name=name, grid=(m // tm,),
        in_specs=[pl.BlockSpec((tm, d), lambda i: (i, 0)), pl.BlockSpec((1, d), lambda i: (0, 0))],
        out_specs=pl.BlockSpec((tm, d), lambda i: (i, 0)),
        out_shape=jax.ShapeDtypeStruct((m, d), BF16),
        compiler_params=pltpu.CompilerParams(dimension_semantics=("parallel",)),
    )(x, g.reshape(1, d))


def _rms_bwd(x, g, dy, name, res=None):
    m, d = x.shape
    tm = _tile(m, (512, 256, 128, 64, 32, 16, 8))
    has_res = res is not None

    def body(x_ref, g_ref, dy_ref, *rest):
        dx_ref, dg_ref = rest[-2:]
        xv = x_ref[...]
        dyv = dy_ref[...]
        r = lax.rsqrt(jnp.mean(xv * xv, axis=-1, keepdims=True) + EPS)
        xh = xv * r
        t = dyv * g_ref[...]
        dx = r * (t - xh * jnp.mean(t * xh, axis=-1, keepdims=True))
        dx_ref[...] = dx + rest[0][...] if has_res else dx

        @pl.when(pl.program_id(0) == 0)
        def _():
            dg_ref[...] = jnp.zeros_like(dg_ref)

        dg_ref[...] += jnp.sum(dyv * xh, axis=0, keepdims=True)

    row_spec = pl.BlockSpec((tm, d), lambda i: (i, 0))
    vec_spec = pl.BlockSpec((1, d), lambda i: (0, 0))
    dx, dg = pl.pallas_call(
        body, name=name, grid=(m // tm,),
        in_specs=[row_spec, vec_spec, row_spec] + ([row_spec] if has_res else []),
        out_specs=[row_spec, vec_spec],
        out_shape=[jax.ShapeDtypeStruct((m, d), F32), jax.ShapeDtypeStruct((1, d), F32)],
        compiler_params=pltpu.CompilerParams(dimension_semantics=("arbitrary",)),
    )(x, g.reshape(1, d), dy, *((res,) if has_res else ()))
    return dx, dg.reshape(d)


def _loss_head(x, g, target):
    m, d = x.shape
    tm = _tile(m, (512, 256, 128, 64, 32, 16, 8))

    def body(x_ref, g_ref, t_ref, loss_ref, dx_ref, dg_ref):
        xv = x_ref[...]
        gv = g_ref[...]
        r = lax.rsqrt(jnp.mean(xv * xv, axis=-1, keepdims=True) + EPS)
        xh = xv * r
        err = xh * gv - t_ref[...]
        row_loss = jnp.mean(err * err, axis=-1, keepdims=True)
        dyv = err * (1.0 / d)
        t = dyv * gv
        dx_ref[...] = r * (t - xh * jnp.mean(t * xh, axis=-1, keepdims=True))

        @pl.when(pl.program_id(0) == 0)
        def _():
            dg_ref[...] = jnp.zeros_like(dg_ref)
            loss_ref[...] = jnp.zeros_like(loss_ref)

        dg_ref[...] += jnp.sum(dyv * xh, axis=0, keepdims=True)
        loss_ref[...] += 0.5 * jnp.sum(row_loss, axis=0, keepdims=True)

    loss, dx, dg = pl.pallas_call(
        body, name="loss_head", grid=(m // tm,),
        in_specs=[pl.BlockSpec((tm, d), lambda i: (i, 0)), pl.BlockSpec((1, d), lambda i: (0, 0)),
                  pl.BlockSpec((tm, d), lambda i: (i, 0))],
        out_specs=[pl.BlockSpec((1, 1), lambda i: (0, 0)), pl.BlockSpec((tm, d), lambda i: (i, 0)),
                   pl.BlockSpec((1, d), lambda i: (0, 0))],
        out_shape=[jax.ShapeDtypeStruct((1, 1), F32), jax.ShapeDtypeStruct((m, d), F32),
                   jax.ShapeDtypeStruct((1, d), F32)],
        compiler_params=pltpu.CompilerParams(dimension_semantics=("arbitrary",)),
    )(x, g.reshape(1, d), target)
    return loss, dx, dg.reshape(d)


def _rope_apply(x, tabs, transpose, name):
    m, w = x.shape
    wt = tabs[0].shape[1]
    reps = w // wt
    half = ROPE // 2
    tm = _tile(m, (256, 128, 64, 32, 16, 8))

    def body(x_ref, c_ref, a_ref, b_ref, y_ref):
        xv = x_ref[...]

        def wide(t_ref):
            t = t_ref[...]
            return t if reps == 1 else jnp.concatenate([t] * reps, axis=1)

        c, a, b = wide(c_ref), wide(a_ref), wide(b_ref)
        if transpose:
            y = xv * c + pltpu.roll(xv * a, half, 1) + pltpu.roll(xv * b, w - half, 1)
        else:
            y = xv * c + pltpu.roll(xv, w - half, 1) * a + pltpu.roll(xv, half, 1) * b
        y_ref[...] = y

    x_spec = pl.BlockSpec((tm, w), lambda i: (i, 0))
    t_spec = pl.BlockSpec((tm, wt), lambda i: (i, 0))
    return pl.pallas_call(
        body, name=name, grid=(m // tm,),
        in_specs=[x_spec, t_spec, t_spec, t_spec], out_specs=x_spec,
        out_shape=jax.ShapeDtypeStruct((m, w), F32),
        compiler_params=pltpu.CompilerParams(dimension_semantics=("parallel",)),
    )(x, *tabs)


def _log_sigmoid_pair(z):
    a = jnp.minimum(z, 0.0) - jnp.log(1.0 + jnp.exp(-jnp.abs(z)))
    return a, a - z


def _split_bf16(x):
    hi = x.astype(BF16)
    return hi, (x - hi.astype(F32)).astype(BF16)


def _wide(v, width):
    return v if width == LANES else jnp.concatenate([v] * (width // LANES), axis=1)


def _lanes(col):
    return jnp.broadcast_to(col, (col.shape[0], LANES))


def _rope_head(x, c, a, b):
    half = ROPE // 2
    return x * c + pltpu.roll(x, LANES - half, 1) * a + pltpu.roll(x, half, 1) * b


def _rope_head_t(dy, c, a, b):
    half = ROPE // 2
    return dy * c + pltpu.roll(dy * a, half, 1) + pltpu.roll(dy * b, LANES - half, 1)


def _sb_weights(qh, k2, valid):
    a, b = _log_sigmoid_pair(_dot(qh, k2, NT))
    return a, b if valid is None else jnp.where(valid, b, 0.0)


def _sb_fwd(qkv):
    s = qkv.shape[0]
    t = min(SB_BLOCK, s)
    nq = s // t
    npair = SB_HEADS // 2
    scale = SB_HD ** -0.5

    def body(q_ref, k_ref, v_ref, o_ref, bta_ref, btb_ref, js_ref, acc_ref, ra_ref, rb_ref):
        p, i = pl.program_id(0), pl.program_id(1)
        q2 = q_ref[...] * scale
        first = lax.broadcasted_iota(jnp.int32, (t, LANES), 1) < SB_HD
        heads = (jnp.where(first, q2, jnp.zeros_like(q2)), jnp.where(first, jnp.zeros_like(q2), q2))
        row = lax.broadcasted_iota(jnp.int32, (t, t), 0)
        col = lax.broadcasted_iota(jnp.int32, (t, t), 1)
        later = jnp.where(row > col, 1.0, 0.0).astype(BF16)
        acc_ref[...] = jnp.zeros_like(acc_ref)
        ra_ref[...] = jnp.zeros_like(ra_ref)
        rb_ref[...] = jnp.zeros_like(rb_ref)

        def block(j, diagonal):
            off = pl.multiple_of(j * t, t)
            k2 = k_ref[pl.ds(off, t), :]
            v2 = v_ref[pl.ds(off, t), :]
            valid = col < row if diagonal else None
            outs, rmax = [], None
            for qh, r_ref in zip(heads, (ra_ref, rb_ref)):
                a, b = _sb_weights(qh, k2, valid)
                bh, bl = _split_bf16(b)
                inner = _dot(bh, later, NN) + _dot(bl, later, NN)
                r = r_ref[...]
                w = jnp.exp(a + inner + _wide(r, t))
                if diagonal:
                    w = jnp.where(valid, w, 0.0)
                outs.append(_dot(w.astype(BF16), v2, NN))
                rn = r + jnp.sum(b, axis=1, keepdims=True)
                r_ref[...] = rn
                rmax = jnp.max(rn) if rmax is None else jnp.maximum(rmax, jnp.max(rn))
            acc_ref[...] += jnp.where(first, outs[0], outs[1])
            return rmax

        def cond(carry):
            j, rmax = carry
            return jnp.logical_and(j >= 0, rmax > EXP_ZERO)

        def step(carry):
            j, _ = carry
            return j - 1, block(j, False)

        jend, _ = lax.while_loop(cond, step, (i - 1, block(i, True)))
        o_ref[...] = acc_ref[...].astype(o_ref.dtype)
        bta_ref[0] = ra_ref[...][:, :1]
        btb_ref[0] = rb_ref[...][:, :1]
        js_ref[p, i] = (jend + 1).astype(F32)

    stat = pl.BlockSpec((1, t, 1), lambda p, i: (p, i, 0))
    return pl.pallas_call(
        body, name="sb_attn_fwd", grid=(npair, nq),
        in_specs=[pl.BlockSpec((t, LANES), lambda p, i: (i, p)),
                  pl.BlockSpec((s, LANES), lambda p, i: (0, npair + p)),
                  pl.BlockSpec((s, LANES), lambda p, i: (0, 2 * npair + p))],
        out_specs=[pl.BlockSpec((t, LANES), lambda p, i: (i, p)), stat, stat,
                   pl.BlockSpec(memory_space=pltpu.SMEM)],
        out_shape=[jax.ShapeDtypeStruct((s, SB_HEADS * SB_HD), BF16), jax.ShapeDtypeStruct((npair, s, 1), F32),
                   jax.ShapeDtypeStruct((npair, s, 1), F32), jax.ShapeDtypeStruct((npair, nq), F32)],
        scratch_shapes=[pltpu.VMEM((t, LANES), F32)] * 3,
        compiler_params=pltpu.CompilerParams(
            dimension_semantics=("arbitrary", "arbitrary"), vmem_limit_bytes=VMEM_ATT),
    )(qkv, qkv, qkv)


def _sb_bwd(qkv, do, btot_a, btot_b, jstart):
    s = qkv.shape[0]
    t = min(SB_BLOCK, s)
    nq = s // t
    npair = SB_HEADS // 2
    scale = SB_HD ** -0.5

    def body(js_ref, q_ref, k_ref, v_ref, do_ref, bta_ref, btb_ref, dq_ref, dk_ref, dv_ref,
             dq_acc, pa_ref, pb_ref, ga_ref, gb_ref, dk_acc, dv_acc):
        p, i = pl.program_id(0), pl.program_id(1)

        @pl.when(i == 0)
        def _():
            dk_acc[...] = jnp.zeros_like(dk_acc)
            dv_acc[...] = jnp.zeros_like(dv_acc)

        q2 = q_ref[...] * scale
        do2 = do_ref[...]
        first = lax.broadcasted_iota(jnp.int32, (t, LANES), 1) < SB_HD
        zero = jnp.zeros_like(q2)
        q_heads = (jnp.where(first, q2, zero), jnp.where(first, zero, q2))
        do_heads = (jnp.where(first, do2, zero), jnp.where(first, zero, do2))
        bts = (_wide(_lanes(bta_ref[0]), t), _wide(_lanes(btb_ref[0]), t))
        row = lax.broadcasted_iota(jnp.int32, (t, t), 0)
        col = lax.broadcasted_iota(jnp.int32, (t, t), 1)
        upto = jnp.where(row <= col, 1.0, 0.0).astype(BF16)
        before = jnp.where(row < col, 1.0, 0.0).astype(BF16)
        dq_acc[...] = jnp.zeros_like(dq_acc)
        for r in (pa_ref, pb_ref, ga_ref, gb_ref):
            r[...] = jnp.zeros_like(r)
        j0 = jnp.clip(js_ref[p, i].astype(jnp.int32), 0, i)

        def block(j, diagonal):
            off = pl.multiple_of(j * t, t)
            k2 = k_ref[pl.ds(off, t), :]
            v2 = v_ref[pl.ds(off, t), :]
            valid = col < row if diagonal else None
            dqs, dk2, dv2 = [], None, None
            for qh, doh, bt, p_ref, g_ref in zip(q_heads, do_heads, bts, (pa_ref, pb_ref), (ga_ref, gb_ref)):
                a, b = _sb_weights(qh, k2, valid)
                bh, bl = _split_bf16(b)
                pin = _dot(bh, upto, NN) + _dot(bl, upto, NN)
                surv = bt - (_wide(p_ref[...], t) + pin)
                w = jnp.exp(a + surv)
                if diagonal:
                    w = jnp.where(valid, w, 0.0)
                g = w * _dot(doh, v2, NT)
                gh, gl = _split_bf16(g)
                gsum = _wide(g_ref[...], t) + _dot(gh, before, NN) + _dot(gl, before, NN)
                beta = jnp.exp(a)
                dz = g * (1.0 - beta) - gsum * beta
                if diagonal:
                    dz = jnp.where(valid, dz, 0.0)
                dz = dz.astype(BF16)
                dqs.append(_dot(dz, k2, NN))
                dkh = _dot(dz, qh, TN)
                dvh = _dot(w.astype(BF16), doh, TN)
                dk2 = dkh if dk2 is None else dk2 + dkh
                dv2 = dvh if dv2 is None else dv2 + dvh
                p_ref[...] += jnp.sum(b, axis=1, keepdims=True)
                g_ref[...] += jnp.sum(g, axis=1, keepdims=True)
            dq_acc[...] += jnp.where(first, dqs[0], dqs[1])
            dk_acc[pl.ds(off, t), :] += dk2
            dv_acc[pl.ds(off, t), :] += dv2

        def step(j, carry):
            block(j, False)
            return carry

        lax.fori_loop(j0, i, step, 0)
        block(i, True)
        dq_ref[...] = (dq_acc[...] * scale).astype(dq_ref.dtype)

        @pl.when(i == nq - 1)
        def _():
            dk_ref[...] = dk_acc[...].astype(dk_ref.dtype)
            dv_ref[...] = dv_acc[...].astype(dv_ref.dtype)

    blk = pl.BlockSpec((t, LANES), lambda p, i: (i, p))
    full = pl.BlockSpec((s, LANES), lambda p, i: (0, p))
    stat = pl.BlockSpec((1, t, 1), lambda p, i: (p, i, 0))
    vec = pltpu.VMEM((t, LANES), F32)
    return pl.pallas_call(
        body, name="sb_attn_bwd", grid=(npair, nq),
        in_specs=[pl.BlockSpec(memory_space=pltpu.SMEM), blk,
                  pl.BlockSpec((s, LANES), lambda p, i: (0, npair + p)),
                  pl.BlockSpec((s, LANES), lambda p, i: (0, 2 * npair + p)), blk, stat, stat],
        out_specs=[blk, full, full],
        out_shape=[jax.ShapeDtypeStruct((s, SB_HEADS * SB_HD), BF16)] * 3,
        scratch_shapes=[pltpu.VMEM((t, LANES), F32), vec, vec, vec, vec,
                        pltpu.VMEM((s, LANES), F32), pltpu.VMEM((s, LANES), F32)],
        compiler_params=pltpu.CompilerParams(
            dimension_semantics=("arbitrary", "arbitrary"), vmem_limit_bytes=VMEM_ATT),
    )(jstart, qkv, qkv, qkv, do, btot_a, btot_b)


@jax.custom_vjp
def sb_attention(qkv):
    return _sb_fwd(qkv)[0]


def _sb_attention_fwd(qkv):
    o, btot_a, btot_b, jstart = _sb_fwd(qkv)
    return o, (qkv, btot_a, btot_b, jstart)


def _sb_attention_bwd(res, do):
    qkv, btot_a, btot_b, jstart = res
    dq, dk, dv = _sb_bwd(qkv, do, btot_a, btot_b, jstart)
    return (jnp.concatenate([dq, dk, dv], axis=1),)


sb_attention.defvjp(_sb_attention_fwd, _sb_attention_bwd)


def _chunk_allowed(row, col):
    return (col >> CHUNK_SHIFT) <= (row >> CHUNK_SHIFT)


def _mla_fwd(q, k, vx, tabs):
    s = q.shape[0]
    h_ = q.shape[1] // LANES
    t = min(MLA_BLOCK, s)
    nq = s // t
    scale = QK ** -0.5
    hp = MLA_GROUP

    def body(q_ref, k_ref, v_ref, c_ref, a_ref, b_ref, o_ref, lse_ref, acc_ref, m_ref):
        i = pl.program_id(1)
        acc_ref[...] = jnp.zeros_like(acc_ref)
        m_ref[...] = jnp.full_like(m_ref, NEG_BIG)
        rot = (c_ref[...], a_ref[...], b_ref[...])
        qs = [_rope_head(q_ref[:, hh * LANES:(hh + 1) * LANES], *rot).astype(BF16) for hh in range(hp)]

        def tile(j, diagonal):
            off = pl.multiple_of(j * t, t)
            for hh in range(hp):
                lanes = slice(hh * LANES, (hh + 1) * LANES)
                kj = k_ref[pl.ds(off, t), lanes]
                vj = v_ref[pl.ds(off, t), lanes]
                sc = _dot(qs[hh], kj, NT) * (scale * LOG2_E)
                if diagonal:
                    row = lax.broadcasted_iota(jnp.int32, (t, t), 0)
                    col = lax.broadcasted_iota(jnp.int32, (t, t), 1)
                    sc = jnp.where(_chunk_allowed(row, col), sc, NEG_BIG)
                m_old = m_ref[hh]
                m_new = jnp.maximum(m_old, jnp.max(sc, axis=1, keepdims=True))
                p = jnp.exp2(sc - _wide(m_new, t))
                acc_ref[hh] = jnp.exp2(m_old - m_new) * acc_ref[hh] + _dot(p.astype(BF16), vj, NN)
                m_ref[hh] = m_new

        def step(jj, carry):
            tile(2 * jj, False)
            tile(2 * jj + 1, False)
            return carry

        lax.fori_loop(0, i // 2, step, 0)

        @pl.when(i % 2 == 1)
        def _():
            tile(i - 1, False)

        tile(i, True)
        first = lax.broadcasted_iota(jnp.int32, (t, LANES), 1) < VDIM
        outs = []
        for hh in range(hp):
            acc = acc_ref[hh]
            den = acc[:, VDIM:VDIM + 1]
            outs.append(acc / den)
            lse_ref[hh] = (m_ref[hh][:, :1] + jnp.log2(den)) * LN_2
        for pp in range(hp // 2):
            o_ref[:, pp * LANES:(pp + 1) * LANES] = jnp.where(
                first, outs[2 * pp], pltpu.roll(outs[2 * pp + 1], VDIM, 1))

    wide = hp * LANES
    return pl.pallas_call(
        body, name="mla_attn_fwd", grid=(h_ // hp, nq),
        in_specs=[pl.BlockSpec((t, wide), lambda g, i: (i, g)),
                  pl.BlockSpec((s, wide), lambda g, i: (0, g)),
                  pl.BlockSpec((s, wide), lambda g, i: (0, g))] + [pl.BlockSpec((t, LANES), lambda g, i: (i, 0))] * 3,
        out_specs=[pl.BlockSpec((t, hp * VDIM), lambda g, i: (i, g)),
                   pl.BlockSpec((hp, t, 1), lambda g, i: (g, i, 0))],
        out_shape=[jax.ShapeDtypeStruct((s, h_ * VDIM), F32), jax.ShapeDtypeStruct((h_, s, 1), F32)],
        scratch_shapes=[pltpu.VMEM((hp, t, LANES), F32), pltpu.VMEM((hp, t, LANES), F32)],
        compiler_params=pltpu.CompilerParams(
            dimension_semantics=("arbitrary", "arbitrary"), vmem_limit_bytes=VMEM_ATT),
    )(q, k, vx, *tabs)


def _mla_bwd(q, k, vx, tabs, o, lse, do):
    s = q.shape[0]
    h_ = q.shape[1] // LANES
    t = min(MLA_BLOCK, s)
    nq = s // t
    scale = QK ** -0.5

    def body(q_ref, k_ref, v_ref, c_ref, a_ref, b_ref, o_ref, lse_ref, do_ref, dq_ref, dk_ref, dv_ref, dq_acc):
        h, i = pl.program_id(0), pl.program_id(1)
        rot = (c_ref[...], a_ref[...], b_ref[...])

        @pl.when(i == 0)
        def _():
            dk_ref[...] = jnp.zeros_like(dk_ref)
            dv_ref[...] = jnp.zeros_like(dv_ref)

        qv = _rope_head(q_ref[...], *rot).astype(BF16)
        lane = lax.broadcasted_iota(jnp.int32, (t, LANES), 1)
        start = (h % 2) * VDIM
        mine = jnp.logical_and(lane >= start, lane < start + VDIM)
        do2 = do_ref[...]
        delta = _lanes(jnp.sum(jnp.where(mine, do2 * o_ref[...], 0.0), axis=1, keepdims=True))
        odd = (start + jnp.zeros_like(lane)) > 0
        do_head = jnp.where(odd, pltpu.roll(do2, VDIM, 1), do2)
        dov = jnp.where(lane < VDIM, do_head, 0.0).astype(BF16)
        lse = _lanes(lse_ref[0])
        dq_acc[...] = jnp.zeros_like(dq_acc)

        def tile(j, diagonal):
            off = pl.multiple_of(j * t, t)
            kj = k_ref[pl.ds(off, t), :]
            vj = v_ref[pl.ds(off, t), :]
            p = jnp.exp(_dot(qv, kj, NT) * scale - _wide(lse, t))
            if diagonal:
                row = lax.broadcasted_iota(jnp.int32, (t, t), 0)
                col = lax.broadcasted_iota(jnp.int32, (t, t), 1)
                p = jnp.where(_chunk_allowed(row, col), p, 0.0)
            dp = _dot(dov, vj, NT)
            ds = (p * (dp - _wide(delta, t)) * scale).astype(BF16)
            dq_acc[...] += _dot(ds, kj, NN)
            dk_ref[pl.ds(off, t), :] += _dot(ds, qv, TN)
            dv_ref[pl.ds(off, t), :] += _dot(p.astype(BF16), dov, TN)

        def step(jj, carry):
            tile(2 * jj, False)
            tile(2 * jj + 1, False)
            return carry

        lax.fori_loop(0, i // 2, step, 0)

        @pl.when(i % 2 == 1)
        def _():
            tile(i - 1, False)

        tile(i, True)
        dq_ref[...] = _rope_head_t(dq_acc[...], *rot)

    blk = pl.BlockSpec((t, LANES), lambda h, i: (i, h))
    full = pl.BlockSpec((s, LANES), lambda h, i: (0, h))
    pair = pl.BlockSpec((t, LANES), lambda h, i: (i, h // 2))
    stat = pl.BlockSpec((1, t, 1), lambda h, i: (h, i, 0))
    return pl.pallas_call(
        body, name="mla_attn_bwd", grid=(h_, nq),
        in_specs=[blk, full, full] + [pl.BlockSpec((t, LANES), lambda h, i: (i, 0))] * 3 + [pair, stat, pair],
        out_specs=[blk, full, full],
        out_shape=[jax.ShapeDtypeStruct((s, h_ * LANES), F32)] * 3,
        scratch_shapes=[pltpu.VMEM((t, LANES), F32)],
        compiler_params=pltpu.CompilerParams(
            dimension_semantics=("arbitrary", "arbitrary"), vmem_limit_bytes=VMEM_ATT),
    )(q, k, vx, *tabs, o, lse, do)


def _assemble_kv(kvp, kr):
    half = kvp.shape[1] // 2
    ones = jnp.tile((jnp.arange(LANES) == VDIM).astype(F32), MLA_HEADS)
    k = kvp[:, :half] + jnp.tile(kr, (1, MLA_HEADS))
    return k.astype(BF16), (kvp[:, half:] + ones).astype(BF16)


@jax.custom_vjp
def mla_attention(q, kvp, kr, c, a, b):
    return _mla_fwd(q, *_assemble_kv(kvp, kr), (c, a, b))[0]


def _mla_attention_fwd(q, kvp, kr, c, a, b):
    kb, vb = _assemble_kv(kvp, kr)
    o, lse = _mla_fwd(q, kb, vb, (c, a, b))
    return o, (q, kb, vb, (c, a, b), o, lse)


def _mla_attention_bwd(res, do):
    q, kb, vb, tabs, o, lse = res
    dq, dk, dvx = _mla_bwd(q, kb, vb, tabs, o, lse, do)
    dkr = dk.reshape(dk.shape[0], MLA_HEADS, LANES).sum(axis=1)
    return (dq, jnp.concatenate([dk, dvx], axis=1), dkr) + tuple(jnp.zeros_like(t) for t in tabs)


mla_attention.defvjp(_mla_attention_fwd, _mla_attention_bwd)


def _add_tile(acc, res):
    return (acc + res,)


def _make_linear_res(tag, a_dtype=F32):
    def forward(x, a, slot, wb):
        ab = a.astype(BF16)
        y = _matmul(ab, wb, epilogue=_add_tile, extras=(x,), name=tag + "_fwd")[0]
        return y, (ab, wb)

    @jax.custom_vjp
    def f(x, a, slot, wb):
        return forward(x, a, slot, wb)[0]

    def bwd(res, dy):
        ab, wb = res
        dyb = dy.astype(BF16)
        da = _matmul(dyb, wb, tb=True, out_dtypes=(a_dtype,), name=tag + "_da")[0]
        dw = _matmul(ab, dyb, ta=True, name=tag + "_dw")[0]
        return dy, da, dw, jnp.zeros_like(wb)

    f.defvjp(forward, bwd)
    return f


def _make_norm_linear(tag, out_dtype=F32):
    def forward(x, g, slot, wb):
        hb = _rms_fwd(x, g, tag + "_norm")
        y = _matmul(hb, wb, out_dtypes=(out_dtype,), name=tag + "_fwd")[0]
        return y, (x, g, wb, hb)

    @jax.custom_vjp
    def f(x, g, slot, wb):
        return forward(x, g, slot, wb)[0]

    def bwd(res, dy):
        x, g, wb, hb = res
        dyb = dy.astype(BF16)
        dh = _matmul(dyb, wb, tb=True, name=tag + "_dh")[0]
        dw = _matmul(hb, dyb, ta=True, name=tag + "_dw")[0]
        dx, dg = _rms_bwd(x, g, dh, tag + "_norm_bwd")
        return dx, dg, dw, jnp.zeros_like(wb)

    f.defvjp(forward, bwd)
    return f


def _relu2_fwd(acc):
    r = jnp.maximum(acc, 0.0)
    return acc, r * r


def _relu2_bwd(acc, u):
    return (acc * (2.0 * jnp.maximum(u.astype(F32), 0.0)),)


def _make_mlp_res(tag):
    def forward(x, g, slot1, slot2, w1b, w2b):
        hb = _rms_fwd(x, g, tag + "_norm")
        u, act = _matmul(hb, w1b, out_dtypes=(BF16, BF16), epilogue=_relu2_fwd, name=tag + "_up")
        y = _matmul(act, w2b, epilogue=_add_tile, extras=(x,), name=tag + "_down")[0]
        return y, (x, g, w1b, w2b, hb, u, act)

    @jax.custom_vjp
    def f(x, g, slot1, slot2, w1b, w2b):
        return forward(x, g, slot1, slot2, w1b, w2b)[0]

    def bwd(res, dy):
        x, g, w1b, w2b, hb, u, act = res
        dyb = dy.astype(BF16)
        du = _matmul(dyb, w2b, tb=True, out_dtypes=(BF16,), epilogue=_relu2_bwd, extras=(u,), name=tag + "_du")[0]
        dw2 = _matmul(act, dyb, ta=True, name=tag + "_dw2")[0]
        dw1 = _matmul(hb, du, ta=True, name=tag + "_dw1")[0]
        dh = _matmul(du, w1b, tb=True, name=tag + "_dh")[0]
        dx, dg = _rms_bwd(x, g, dh, tag + "_norm_bwd", res=dy)
        return dx, dg, dw1, dw2, jnp.zeros_like(w1b), jnp.zeros_like(w2b)

    f.defvjp(forward, bwd)
    return f


def _make_rope(tag):
    @jax.custom_vjp
    def f(x, c, a, b):
        return _rope_apply(x, (c, a, b), False, tag + "_fwd")

    def fwd(x, c, a, b):
        return _rope_apply(x, (c, a, b), False, tag + "_fwd"), (c, a, b)

    def bwd(res, dy):
        c, a, b = res
        return _rope_apply(dy, (c, a, b), True, tag + "_bwd"), jnp.zeros_like(c), jnp.zeros_like(a), jnp.zeros_like(b)

    f.defvjp(fwd, bwd)
    return f


def _rope_tables(positions):
    half = ROPE // 2
    inv_freq = ROPE_THETA ** (-jnp.arange(0, ROPE, 2, dtype=F32) / ROPE)
    ang = positions.astype(F32)[:, None] * inv_freq
    cos, sin = jnp.cos(ang), jnp.sin(ang)
    s = positions.shape[0]
    one = lambda n: jnp.ones((s, n), F32)
    zero = lambda n: jnp.zeros((s, n), F32)

    def tables(before, after):
        return (jnp.concatenate([one(before), cos, cos, one(after)], axis=1),
                jnp.concatenate([zero(before), -sin, zero(half + after)], axis=1),
                jnp.concatenate([zero(before + half), sin, zero(after)], axis=1))

    return tables(NOPE, LANES - QK), tables(KV_RANK, DKV_PAD - KV_RANK - ROPE)


def _pad_heads(w, per):
    lead = w.shape[:-1]
    w = jnp.pad(w.reshape(lead + (MLA_HEADS, per)), [(0, 0)] * len(lead) + [(0, 0), (0, LANES - per)])
    return w.reshape(lead + (MLA_HEADS * LANES,))


def _split_kv_heads(w):
    w3 = w.reshape(w.shape[0], MLA_HEADS, NOPE + VDIM)
    return jnp.concatenate([_pad_heads(w3[..., :NOPE].reshape(w.shape[0], -1), NOPE),
                            _pad_heads(w3[..., NOPE:].reshape(w.shape[0], -1), VDIM)], axis=1)


def _trunk(x, slots, norms, wb, q_tabs, kv_tabs):
    kvp = kr = None
    for layer in range(DEPTH):
        if layer < N_A:
            qkv = _make_norm_linear(f"sb{layer}_qkv", BF16)(
                x, norms['attn_norm'][layer], slots['sb_w_qkv'][layer], wb['sb_w_qkv'][layer])
            o = sb_attention(qkv)
            x = _make_linear_res(f"sb{layer}_o", BF16)(x, o, slots['sb_w_o'][layer], wb['sb_w_o'][layer])
        else:
            j = layer - N_A
            if j == 0:
                pad = ((0, 0), (0, DKV_PAD - KV_RANK - ROPE))
                down = _make_norm_linear("kv_down")(
                    x, norms['kv_norm'], jnp.pad(slots['mla_w_dkv'][0], pad), jnp.pad(wb['mla_w_dkv'][0], pad))
                kvp = _make_norm_linear("kv_up")(
                    down[:, :KV_RANK], norms['mla_kv_lat_norm'],
                    _split_kv_heads(slots['mla_w_ukv'][0]), _split_kv_heads(wb['mla_w_ukv'][0]))
                k_rope = _make_rope("rope_k")(down, *kv_tabs)[:, KV_RANK:KV_RANK + ROPE]
                kr = jnp.pad(k_rope, ((0, 0), (NOPE, LANES - QK)))
            c_q = _make_norm_linear(f"mla{j}_dq")(
                x, norms['attn_norm'][layer], slots['mla_w_dq'][j], wb['mla_w_dq'][j])
            q = _make_norm_linear(f"mla{j}_uq")(
                c_q, norms['mla_q_lat_norm'][j], _pad_heads(slots['mla_w_uq'][j], QK), _pad_heads(wb['mla_w_uq'][j], QK))
            o = mla_attention(q, kvp, kr, *q_tabs)
            x = _make_linear_res(f"mla{j}_o")(x, o, slots['mla_w_o'][j], wb['mla_w_o'][j])
        x = _make_mlp_res(f"mlp{layer}")(
            x, norms['mlp_norm'][layer], slots['mlp_w1'][layer], slots['mlp_w2'][layer],
            wb['mlp_w1'][layer], wb['mlp_w2'][layer])
    return x


def _merge_blocks(gathered, ax):
    shp = gathered.shape[1:]
    return jnp.moveaxis(gathered, 0, ax).reshape(shp[:ax] + (N_DEV * shp[ax],) + shp[ax + 1:])


def _split_blocks(full, ax):
    shp = full.shape
    return jnp.moveaxis(full.reshape(shp[:ax] + (N_DEV, shp[ax] // N_DEV) + shp[ax + 1:]), ax, 0)


def _pack(parts):
    flat = jnp.concatenate([p.reshape(-1) for p in parts])
    rows = -(-flat.shape[0] // PACK_COLS)
    rows = -(-rows // PACK_ROW_ALIGN) * PACK_ROW_ALIGN
    return jnp.pad(flat, (0, rows * PACK_COLS - flat.shape[0])).reshape(rows, PACK_COLS)


def _unpack(packed, shapes):
    flat = packed.reshape(-1)
    out, off = [], 0
    for shp in shapes:
        n = math.prod(shp)
        out.append(flat[off:off + n].reshape(shp))
        off += n
    return out


def _add_round(x, stage, core, name):
    _, r, c = x.shape
    tr = _tile(r, (256, 128, 64, 32, 16, 8))

    def body(core_ref, x_ref, s_ref, o_ref):
        o_ref[...] = (x_ref[0] + s_ref[...]).astype(o_ref.dtype)

    return pl.pallas_call(
        body, name=name,
        grid_spec=pltpu.PrefetchScalarGridSpec(
            num_scalar_prefetch=1, grid=(r // tr,),
            in_specs=[pl.BlockSpec((1, tr, c), lambda i, core_ref: (core_ref[0], i, 0)),
                      pl.BlockSpec((tr, c), lambda i, core_ref: (i, 0))],
            out_specs=pl.BlockSpec((tr, c), lambda i, core_ref: (i, 0))),
        out_shape=jax.ShapeDtypeStruct((r, c), BF16),
        compiler_params=pltpu.CompilerParams(dimension_semantics=("parallel",)),
    )(core, x, stage)


def _adamw_reduce(parts, w, m, v, name):
    n_parts, r, c = parts.shape
    tr = _tile(r, (128, 64, 32, 16, 8))
    bias1 = 1.0 - ADAM_B1 ** ADAM_STEP
    bias2 = 1.0 - ADAM_B2 ** ADAM_STEP

    def body(p_ref, w_ref, m_ref, v_ref, g_ref, d_ref, nm_ref, nv_ref):
        g = p_ref[0].astype(F32)
        for s in range(1, n_parts):
            g = g + p_ref[s].astype(F32)
        mn = ADAM_B1 * m_ref[...] + (1.0 - ADAM_B1) * g
        vn = ADAM_B2 * v_ref[...] + (1.0 - ADAM_B2) * (g * g)
        m_hat = mn / bias1
        v_hat = vn / bias2
        g_ref[...] = g
        d_ref[...] = -ADAM_LR * (m_hat / (jnp.sqrt(v_hat) + ADAM_EPS) + ADAM_WD * w_ref[...])
        nm_ref[...] = mn
        nv_ref[...] = vn

    blk = pl.BlockSpec((tr, c), lambda i: (i, 0))
    return pl.pallas_call(
        body, name=name, grid=(r // tr,),
        in_specs=[pl.BlockSpec((n_parts, tr, c), lambda i: (0, i, 0)), blk, blk, blk],
        out_specs=[blk] * 4,
        out_shape=[jax.ShapeDtypeStruct((r, c), F32)] * 4,
        compiler_params=pltpu.CompilerParams(dimension_semantics=("parallel",), vmem_limit_bytes=VMEM_MM),
    )(parts, w, m, v)


def kernel(x, positions, attn_norm, mlp_norm, sb_w_qkv, sb_w_o, kv_norm, mla_w_dkv, mla_kv_lat_norm, mla_w_ukv, mla_w_dq, mla_q_lat_norm, mla_w_uq, mla_w_o, mlp_w1, mlp_w2, final_norm, loss_target, m_attn_norm, m_mlp_norm, m_sb_w_qkv, m_sb_w_o, m_kv_norm, m_mla_w_dkv, m_mla_kv_lat_norm, m_mla_w_ukv, m_mla_w_dq, m_mla_q_lat_norm, m_mla_w_uq, m_mla_w_o, m_mlp_w1, m_mlp_w2, m_final_norm, v_attn_norm, v_mlp_norm, v_sb_w_qkv, v_sb_w_o, v_kv_norm, v_mla_w_dkv, v_mla_kv_lat_norm, v_mla_w_ukv, v_mla_w_dq, v_mla_q_lat_norm, v_mla_w_uq, v_mla_w_o, v_mlp_w1, v_mlp_w2, v_final_norm):
    weights = dict(attn_norm=attn_norm, mlp_norm=mlp_norm, sb_w_qkv=sb_w_qkv, sb_w_o=sb_w_o, kv_norm=kv_norm,
                   mla_w_dkv=mla_w_dkv, mla_kv_lat_norm=mla_kv_lat_norm, mla_w_ukv=mla_w_ukv, mla_w_dq=mla_w_dq,
                   mla_q_lat_norm=mla_q_lat_norm, mla_w_uq=mla_w_uq, mla_w_o=mla_w_o, mlp_w1=mlp_w1, mlp_w2=mlp_w2,
                   final_norm=final_norm)
    mom_m = dict(attn_norm=m_attn_norm, mlp_norm=m_mlp_norm, sb_w_qkv=m_sb_w_qkv, sb_w_o=m_sb_w_o, kv_norm=m_kv_norm,
                 mla_w_dkv=m_mla_w_dkv, mla_kv_lat_norm=m_mla_kv_lat_norm, mla_w_ukv=m_mla_w_ukv, mla_w_dq=m_mla_w_dq,
                 mla_q_lat_norm=m_mla_q_lat_norm, mla_w_uq=m_mla_w_uq, mla_w_o=m_mla_w_o, mlp_w1=m_mlp_w1,
                 mlp_w2=m_mlp_w2, final_norm=m_final_norm)
    mom_v = dict(attn_norm=v_attn_norm, mlp_norm=v_mlp_norm, sb_w_qkv=v_sb_w_qkv, sb_w_o=v_sb_w_o, kv_norm=v_kv_norm,
                 mla_w_dkv=v_mla_w_dkv, mla_kv_lat_norm=v_mla_kv_lat_norm, mla_w_ukv=v_mla_w_ukv, mla_w_dq=v_mla_w_dq,
                 mla_q_lat_norm=v_mla_q_lat_norm, mla_w_uq=v_mla_w_uq, mla_w_o=v_mla_w_o, mlp_w1=v_mlp_w1,
                 mlp_w2=v_mlp_w2, final_norm=v_final_norm)
    sharded_names = [n for n, _ in SHARDED]
    repl_shapes = [tuple(weights[n].shape) for n in REPLICATED]

    gathered = _all_gather([weights[n].astype(BF16) for n in sharded_names], "gather_weights")
    wb, slots = {}, {}
    for (n, ax), g in zip(SHARDED, gathered):
        full = _merge_blocks(g, ax)
        layers = [full[l] for l in range(full.shape[0])] if full.ndim == 3 else [full]
        wb[n] = layers
        slots[n] = [jnp.zeros(w.shape, F32) for w in layers]
    norms = {n: weights[n] for n in REPLICATED if n != 'final_norm'}

    q_tabs, kv_tabs = _rope_tables(positions[0])
    x_last, pullback = jax.vjp(lambda xx, ss, nn: _trunk(xx, ss, nn, wb, q_tabs, kv_tabs), x[0], slots, norms)
    loss_part, dx_last, d_final = _loss_head(x_last, final_norm, loss_target[0])
    dx, d_slots, d_norms = pullback(dx_last)
    d_norms = dict(d_norms)
    d_norms['final_norm'] = d_final
    loss = lax.psum(loss_part[0, 0], ("x", "y", "c"))

    core = lax.axis_index("c").astype(jnp.int32).reshape(1)
    halves, dims = [], []
    for n, ax in SHARDED:
        full = jnp.stack(d_slots[n]) if weights[n].ndim == 3 else d_slots[n][0]
        blocks = _split_blocks(full, ax)
        two_d = (math.prod(blocks.shape[1:-1]), blocks.shape[-1])
        dims.append(two_d)
        halves.append(jnp.moveaxis(blocks.reshape((N_DEV // 2, 2) + two_d), 1, 0))
    staged = _pair_exchange(halves, "pair_grads")
    sums = []
    for n, h, st, (r, c) in zip(sharded_names, halves, staged, dims):
        rows = (N_DEV // 2) * r
        sums.append(_add_round(h.reshape(2, rows, c), st.reshape(rows, c), core, "pair_sum_" + n).reshape(st.shape))
    received = _chip_exchange(sums, "scatter_grads")
    repl_parts = _all_gather([_pack([d_norms[n] for n in REPLICATED])], "gather_norm_grads")[0]

    results = {kind: {} for kind in ("grad", "delta", "new_m", "new_v")}
    for n, parts, two_d in zip(sharded_names, received, dims):
        shp = weights[n].shape
        res = _adamw_reduce(parts, weights[n].reshape(two_d), mom_m[n].reshape(two_d),
                            mom_v[n].reshape(two_d), "adamw_" + n)
        for kind, a in zip(results, res):
            results[kind][n] = a.reshape(shp)
    res = _adamw_reduce(repl_parts, _pack([weights[n] for n in REPLICATED]), _pack([mom_m[n] for n in REPLICATED]),
                        _pack([mom_v[n] for n in REPLICATED]), "adamw_replicated")
    for kind, a in zip(results, res):
        results[kind].update(zip(REPLICATED, _unpack(a, repl_shapes)))

    out = [loss, dx[None]]
    for kind in ("grad", "delta", "new_m", "new_v"):
        out += [results[kind][n] for n in WEIGHT_ORDER]
    return tuple(out)
```

```python
import math

import jax
import jax.numpy as jnp
from jax import lax
from jax.experimental import pallas as pl
from jax.experimental.pallas import tpu as pltpu

F32 = jnp.float32
BF16 = jnp.bfloat16
MESH = pl.DeviceIdType.MESH

N_DEV = 8
DEPTH = 4
N_A = 2
SB_HEADS = 16
SB_HD = 64
MLA_HEADS = 16
NOPE = 64
ROPE = 32
VDIM = 64
QK = NOPE + ROPE
KV_RANK = 256
DKV_PAD = 384
LANES = 128
CHUNK_SHIFT = 6
ROPE_THETA = 10000.0
EPS = 1e-6
SB_BLOCK = 256
MLA_BLOCK = 512
MLA_GROUP = 4
MLA_UNROLL = 4
PACK_COLS = 1024
PACK_ROW_ALIGN = 16
EXP_ZERO = -104.0
NEG_BIG = -1e30
LOG2_E = 1.4426950408889634
LN_2 = 0.6931471805599453
VMEM_ATT = 56 * 1024 * 1024
VMEM_MM = 48 * 1024 * 1024

ADAM_LR = 0.001
ADAM_B1 = 0.9
ADAM_B2 = 0.999
ADAM_EPS = 1e-08
ADAM_WD = 0.01
ADAM_STEP = 10

WEIGHT_ORDER = ['attn_norm', 'mlp_norm', 'sb_w_qkv', 'sb_w_o', 'kv_norm', 'mla_w_dkv', 'mla_kv_lat_norm',
                'mla_w_ukv', 'mla_w_dq', 'mla_q_lat_norm', 'mla_w_uq', 'mla_w_o', 'mlp_w1', 'mlp_w2', 'final_norm']
SHARDED = [('sb_w_qkv', 2), ('sb_w_o', 1), ('mla_w_dkv', 0), ('mla_w_ukv', 1), ('mla_w_dq', 1),
           ('mla_w_uq', 2), ('mla_w_o', 1), ('mlp_w1', 2), ('mlp_w2', 1)]
REPLICATED = ['attn_norm', 'mlp_norm', 'kv_norm', 'mla_kv_lat_norm', 'mla_q_lat_norm', 'final_norm']


def _tile(dim, prefs=(512, 384, 256, 128, 64, 32, 16, 8)):
    for t in prefs:
        if dim % t == 0:
            return t
    return dim


def _dot(a, b, dims):
    return lax.dot_general(a, b, (dims, ((), ())), preferred_element_type=F32)


NN = ((1,), (0,))
NT = ((1,), (1,))
TN = ((0,), (0,))


def _all_gather(shards, name):
    n_t = len(shards)

    def body(*refs):
        x_refs, out_refs = refs[:n_t], refs[n_t:2 * n_t]
        send_sems, recv_sems, local_sems = refs[2 * n_t:]
        x, y, c = lax.axis_index("x"), lax.axis_index("y"), lax.axis_index("c")
        me, sibling = (x, y, c), (x, y, 1 - c)
        chips = [(1 - x, y), (x, 1 - y), (1 - x, 1 - y)]

        def slot(t, px, py, pc):
            return out_refs[t].at[4 * px + 2 * py + pc]

        def copy(t, k, block, to, src=None):
            return pltpu.make_async_remote_copy(
                src_ref=slot(t, *block) if src is None else src, dst_ref=slot(t, *block),
                send_sem=send_sems.at[7 * t + k], recv_sem=recv_sems.at[7 * t + k],
                device_id=to, device_id_type=MESH)

        mine = [pltpu.make_async_copy(x_refs[t], slot(t, *me), local_sems.at[t]) for t in range(n_t)]
        for cp in mine:
            cp.start()
        first = []
        for t in range(n_t):
            first.append(copy(t, 0, me, sibling, src=x_refs[t]))
            first += [copy(t, 1 + j, me, (*chip, c), src=x_refs[t]) for j, chip in enumerate(chips)]
        for cp in first:
            cp.start()
        passed = []
        for t in range(n_t):
            for j, chip in enumerate(chips):
                copy(t, 1 + j, (*chip, c), me).wait_recv()
                onward = copy(t, 4 + j, (*chip, c), sibling)
                onward.start()
                passed.append(onward)
        for t in range(n_t):
            copy(t, 0, sibling, me).wait_recv()
            for j, chip in enumerate(chips):
                copy(t, 4 + j, (*chip, 1 - c), me).wait_recv()
        for cp in first + passed:
            cp.wait_send()
        for cp in mine:
            cp.wait()

    any_spec = pl.BlockSpec(memory_space=pl.ANY)
    return pl.pallas_call(
        body, name=name,
        out_shape=[jax.ShapeDtypeStruct((N_DEV,) + tuple(s.shape), s.dtype) for s in shards],
        in_specs=[any_spec] * n_t, out_specs=[any_spec] * n_t,
        scratch_shapes=[pltpu.SemaphoreType.DMA((7 * n_t,)), pltpu.SemaphoreType.DMA((7 * n_t,)),
                        pltpu.SemaphoreType.DMA((n_t,))],
    )(*shards)


def _pair_exchange(xs, name):
    n_t = len(xs)

    def body(*refs):
        x_refs, out_refs = refs[:n_t], refs[n_t:2 * n_t]
        send_sems, recv_sems = refs[2 * n_t:]
        x, y, c = lax.axis_index("x"), lax.axis_index("y"), lax.axis_index("c")
        copies = [pltpu.make_async_remote_copy(
            src_ref=x_refs[t].at[1 - c], dst_ref=out_refs[t], send_sem=send_sems.at[t], recv_sem=recv_sems.at[t],
            device_id=(x, y, 1 - c), device_id_type=MESH) for t in range(n_t)]
        for cp in copies:
            cp.start()
        for cp in copies:
            cp.wait()

    any_spec = pl.BlockSpec(memory_space=pl.ANY)
    return pl.pallas_call(
        body, name=name,
        out_shape=[jax.ShapeDtypeStruct(x.shape[1:], x.dtype) for x in xs],
        in_specs=[any_spec] * n_t, out_specs=[any_spec] * n_t,
        scratch_shapes=[pltpu.SemaphoreType.DMA((n_t,)), pltpu.SemaphoreType.DMA((n_t,))],
    )(*xs)


def _chip_exchange(ps, name):
    n_t = len(ps)

    def body(*refs):
        p_refs, out_refs = refs[:n_t], refs[n_t:2 * n_t]
        send_sems, recv_sems, local_sems = refs[2 * n_t:]
        mx, my, mc = lax.axis_index("x"), lax.axis_index("y"), lax.axis_index("c")
        me = 2 * mx + my
        mine = [pltpu.make_async_copy(p_refs[t].at[me], out_refs[t].at[me], local_sems.at[t]) for t in range(n_t)]
        for cp in mine:
            cp.start()
        copies = []
        for k in range(1, 4):
            px = 1 - mx if (k >> 1) & 1 else mx
            py = 1 - my if k & 1 else my
            peer = 2 * px + py
            for t in range(n_t):
                copies.append(pltpu.make_async_remote_copy(
                    src_ref=p_refs[t].at[peer], dst_ref=out_refs[t].at[me],
                    send_sem=send_sems.at[3 * t + k - 1], recv_sem=recv_sems.at[3 * t + k - 1],
                    device_id=(px, py, mc), device_id_type=MESH))
        for cp in copies:
            cp.start()
        for cp in copies:
            cp.wait_send()
        for cp in copies:
            cp.wait_recv()
        for cp in mine:
            cp.wait()

    any_spec = pl.BlockSpec(memory_space=pl.ANY)
    return pl.pallas_call(
        body, name=name,
        out_shape=[jax.ShapeDtypeStruct(p.shape, p.dtype) for p in ps],
        in_specs=[any_spec] * n_t, out_specs=[any_spec] * n_t,
        scratch_shapes=[pltpu.SemaphoreType.DMA((3 * n_t,)), pltpu.SemaphoreType.DMA((3 * n_t,)),
                        pltpu.SemaphoreType.DMA((n_t,))],
    )(*ps)


def _matmul(a, b, *, ta=False, tb=False, out_dtypes=(F32,), epilogue=None, extras=(), name):
    if ta:
        kdim, m = a.shape
    else:
        m, kdim = a.shape
    if tb:
        n, kb = b.shape
    else:
        kb, n = b.shape
    assert kdim == kb, (a.shape, b.shape, ta, tb)
    big = (1024, 768, 512, 384, 256, 128, 64, 32, 16, 8)
    tm, tn = _tile(m, big), _tile(n, big)
    tk = kdim if kdim <= 2048 else _tile(kdim, (2048, 1024, 512, 256, 128))
    nk = kdim // tk
    n_extra, n_out = len(extras), len(out_dtypes)
    a_spec = pl.BlockSpec((tk, tm), lambda i, j, k: (k, i)) if ta else pl.BlockSpec((tm, tk), lambda i, j, k: (i, k))
    b_spec = pl.BlockSpec((tn, tk), lambda i, j, k: (j, k)) if tb else pl.BlockSpec((tk, tn), lambda i, j, k: (k, j))
    tile_spec = pl.BlockSpec((tm, tn), lambda i, j, k: (i, j))
    dims = ((0,) if ta else (1,), (1,) if tb else (0,))

    def finish(acc, extra_refs, out_refs):
        outs = (acc,) if epilogue is None else epilogue(acc, *[r[...] for r in extra_refs])
        for o_ref, o in zip(out_refs, outs):
            o_ref[...] = o.astype(o_ref.dtype)

    def body_one(a_ref, b_ref, *rest):
        acc = _dot(a_ref[...].astype(BF16), b_ref[...].astype(BF16), dims)
        finish(acc, rest[:n_extra], rest[n_extra:n_extra + n_out])

    def body_acc(a_ref, b_ref, *rest):
        acc_ref = rest[-1]
        k = pl.program_id(2)

        @pl.when(k == 0)
        def _():
            acc_ref[...] = jnp.zeros_like(acc_ref)

        acc_ref[...] += _dot(a_ref[...].astype(BF16), b_ref[...].astype(BF16), dims)

        @pl.when(k == nk - 1)
        def _():
            finish(acc_ref[...], rest[:n_extra], rest[n_extra:n_extra + n_out])

    return pl.pallas_call(
        body_one if nk == 1 else body_acc, name=name, grid=(m // tm, n // tn, nk),
        in_specs=[a_spec, b_spec] + [tile_spec] * n_extra,
        out_specs=[tile_spec] * n_out,
        out_shape=[jax.ShapeDtypeStruct((m, n), dt) for dt in out_dtypes],
        scratch_shapes=[] if nk == 1 else [pltpu.VMEM((tm, tn), F32)],
        compiler_params=pltpu.CompilerParams(
            dimension_semantics=("parallel", "parallel", "arbitrary"), vmem_limit_bytes=VMEM_MM),
    )(a, b, *extras)


def _rms_fwd(x, g, name):
    m, d = x.shape
    tm = _tile(m, (512, 256, 128, 64, 32, 16, 8))

    def body(x_ref, g_ref, y_ref):
        xv = x_ref[...]
        r = lax.rsqrt(jnp.mean(xv * xv, axis=-1, keepdims=True) + EPS)
        y_ref[...] = (xv * r * g_ref[...]).astype(y_ref.dtype)

    return pl.pallas_call(
        body, name=name, grid=(m // tm,),
        in_specs=[pl.BlockSpec((tm, d), lambda i: (i, 0)), pl.BlockSpec((1, d), lambda i: (0, 0))],
        out_specs=pl.BlockSpec((tm, d), lambda i: (i, 0)),
        out_shape=jax.ShapeDtypeStruct((m, d), BF16),
        compiler_params=pltpu.CompilerParams(dimension_semantics=("parallel",)),
    )(x, g.reshape(1, d))


def _rms_bwd(x, g, dy, name, res=None):
    m, d = x.shape
    tm = _tile(m, (512, 256, 128, 64, 32, 16, 8))
    has_res = res is not None

    def body(x_ref, g_ref, dy_ref, *rest):
        dx_ref, dg_ref = rest[-2:]
        xv = x_ref[...]
        dyv = dy_ref[...]
        r = lax.rsqrt(jnp.mean(xv * xv, axis=-1, keepdims=True) + EPS)
        xh = xv * r
        t = dyv * g_ref[...]
        dx = r * (t - xh * jnp.mean(t * xh, axis=-1, keepdims=True))
        dx_ref[...] = dx + rest[0][...] if has_res else dx

        @pl.when(pl.program_id(0) == 0)
        def _():
            dg_ref[...] = jnp.zeros_like(dg_ref)

        dg_ref[...] += jnp.sum(dyv * xh, axis=0, keepdims=True)

    row_spec = pl.BlockSpec((tm, d), lambda i: (i, 0))
    vec_spec = pl.BlockSpec((1, d), lambda i: (0, 0))
    dx, dg = pl.pallas_call(
        body, name=name, grid=(m // tm,),
        in_specs=[row_spec, vec_spec, row_spec] + ([row_spec] if has_res else []),
        out_specs=[row_spec, vec_spec],
        out_shape=[jax.ShapeDtypeStruct((m, d), F32), jax.ShapeDtypeStruct((1, d), F32)],
        compiler_params=pltpu.CompilerParams(dimension_semantics=("arbitrary",)),
    )(x, g.reshape(1, d), dy, *((res,) if has_res else ()))
    return dx, dg.reshape(d)


def _loss_head(x, g, target):
    m, d = x.shape
    tm = _tile(m, (512, 256, 128, 64, 32, 16, 8))

    def body(x_ref, g_ref, t_ref, loss_ref, dx_ref, dg_ref):
        xv = x_ref[...]
        gv = g_ref[...]
        r = lax.rsqrt(jnp.mean(xv * xv, axis=-1, keepdims=True) + EPS)
        xh = xv * r
        err = xh * gv - t_ref[...]
        row_loss = jnp.mean(err * err, axis=-1, keepdims=True)
        dyv = err * (1.0 / d)
        t = dyv * gv
        dx_ref[...] = r * (t - xh * jnp.mean(t * xh, axis=-1, keepdims=True))

        @pl.when(pl.program_id(0) == 0)
        def _():
            dg_ref[...] = jnp.zeros_like(dg_ref)
            loss_ref[...] = jnp.zeros_like(loss_ref)

        dg_ref[...] += jnp.sum(dyv * xh, axis=0, keepdims=True)
        loss_ref[...] += 0.5 * jnp.sum(row_loss, axis=0, keepdims=True)

    loss, dx, dg = pl.pallas_call(
        body, name="loss_head", grid=(m // tm,),
        in_specs=[pl.BlockSpec((tm, d), lambda i: (i, 0)), pl.BlockSpec((1, d), lambda i: (0, 0)),
                  pl.BlockSpec((tm, d), lambda i: (i, 0))],
        out_specs=[pl.BlockSpec((1, 1), lambda i: (0, 0)), pl.BlockSpec((tm, d), lambda i: (i, 0)),
                   pl.BlockSpec((1, d), lambda i: (0, 0))],
        out_shape=[jax.ShapeDtypeStruct((1, 1), F32), jax.ShapeDtypeStruct((m, d), F32),
                   jax.ShapeDtypeStruct((1, d), F32)],
        compiler_params=pltpu.CompilerParams(dimension_semantics=("arbitrary",)),
    )(x, g.reshape(1, d), target)
    return loss, dx, dg.reshape(d)


def _rope_apply(x, tabs, transpose, name):
    m, w = x.shape
    wt = tabs[0].shape[1]
    reps = w // wt
    half = ROPE // 2
    tm = _tile(m, (256, 128, 64, 32, 16, 8))

    def body(x_ref, c_ref, a_ref, b_ref, y_ref):
        xv = x_ref[...]

        def wide(t_ref):
            t = t_ref[...]
            return t if reps == 1 else jnp.concatenate([t] * reps, axis=1)

        c, a, b = wide(c_ref), wide(a_ref), wide(b_ref)
        if transpose:
            y = xv * c + pltpu.roll(xv * a, half, 1) + pltpu.roll(xv * b, w - half, 1)
        else:
            y = xv * c + pltpu.roll(xv, w - half, 1) * a + pltpu.roll(xv, half, 1) * b
        y_ref[...] = y

    x_spec = pl.BlockSpec((tm, w), lambda i: (i, 0))
    t_spec = pl.BlockSpec((tm, wt), lambda i: (i, 0))
    return pl.pallas_call(
        body, name=name, grid=(m // tm,),
        in_specs=[x_spec, t_spec, t_spec, t_spec], out_specs=x_spec,
        out_shape=jax.ShapeDtypeStruct((m, w), F32),
        compiler_params=pltpu.CompilerParams(dimension_semantics=("parallel",)),
    )(x, *tabs)


def _log_sigmoid_pair(z):
    a = jnp.minimum(z, 0.0) - jnp.log(1.0 + jnp.exp(-jnp.abs(z)))
    return a, a - z


def _split_bf16(x):
    hi = x.astype(BF16)
    return hi, (x - hi.astype(F32)).astype(BF16)


def _wide(v, width):
    return v if width == LANES else jnp.concatenate([v] * (width // LANES), axis=1)


def _lanes(col):
    return jnp.broadcast_to(col, (col.shape[0], LANES))


def _rope_head(x, c, a, b):
    half = ROPE // 2
    return x * c + pltpu.roll(x, LANES - half, 1) * a + pltpu.roll(x, half, 1) * b


def _rope_head_t(dy, c, a, b):
    half = ROPE // 2
    return dy * c + pltpu.roll(dy * a, half, 1) + pltpu.roll(dy * b, LANES - half, 1)


def _sb_weights(qh, k2, valid):
    a, b = _log_sigmoid_pair(_dot(qh, k2, NT))
    return a, b if valid is None else jnp.where(valid, b, 0.0)


def _sb_fwd(qkv):
    s = qkv.shape[0]
    t = min(SB_BLOCK, s)
    nq = s // t
    npair = SB_HEADS // 2
    scale = SB_HD ** -0.5

    def body(q_ref, k_ref, v_ref, o_ref, bta_ref, btb_ref, js_ref, acc_ref, ra_ref, rb_ref):
        p, i = pl.program_id(0), pl.program_id(1)
        q2 = q_ref[...] * scale
        first = lax.broadcasted_iota(jnp.int32, (t, LANES), 1) < SB_HD
        heads = (jnp.where(first, q2, jnp.zeros_like(q2)), jnp.where(first, jnp.zeros_like(q2), q2))
        row = lax.broadcasted_iota(jnp.int32, (t, t), 0)
        col = lax.broadcasted_iota(jnp.int32, (t, t), 1)
        later = jnp.where(row > col, 1.0, 0.0).astype(BF16)
        acc_ref[...] = jnp.zeros_like(acc_ref)
        ra_ref[...] = jnp.zeros_like(ra_ref)
        rb_ref[...] = jnp.zeros_like(rb_ref)

        def block(j, diagonal):
            off = pl.multiple_of(j * t, t)
            k2 = k_ref[pl.ds(off, t), :]
            v2 = v_ref[pl.ds(off, t), :]
            valid = col < row if diagonal else None
            outs, rmax = [], None
            for qh, r_ref in zip(heads, (ra_ref, rb_ref)):
                a, b = _sb_weights(qh, k2, valid)
                bh, bl = _split_bf16(b)
                inner = _dot(bh, later, NN) + _dot(bl, later, NN)
                r = r_ref[...]
                w = jnp.exp(a + inner + _wide(r, t))
                if diagonal:
                    w = jnp.where(valid, w, 0.0)
                outs.append(_dot(w.astype(BF16), v2, NN))
                rn = r + jnp.sum(b, axis=1, keepdims=True)
                r_ref[...] = rn
                rmax = jnp.max(rn) if rmax is None else jnp.maximum(rmax, jnp.max(rn))
            acc_ref[...] += jnp.where(first, outs[0], outs[1])
            return rmax

        def cond(carry):
            j, rmax = carry
            return jnp.logical_and(j >= 0, rmax > EXP_ZERO)

        def step(carry):
            j, _ = carry
            return j - 1, block(j, False)

        def first_two():
            block(i, True)
            return block(i - 1, False)

        rmax = lax.cond(i >= 1, first_two, lambda: block(i, True))
        jend, _ = lax.while_loop(cond, step, (jnp.maximum(i - 2, -1), rmax))
        o_ref[...] = acc_ref[...].astype(o_ref.dtype)
        bta_ref[0] = ra_ref[...][:, :1]
        btb_ref[0] = rb_ref[...][:, :1]
        js_ref[p, i] = (jend + 1).astype(F32)

    stat = pl.BlockSpec((1, t, 1), lambda p, i: (p, i, 0))
    return pl.pallas_call(
        body, name="sb_attn_fwd", grid=(npair, nq),
        in_specs=[pl.BlockSpec((t, LANES), lambda p, i: (i, p)),
                  pl.BlockSpec((s, LANES), lambda p, i: (0, npair + p)),
                  pl.BlockSpec((s, LANES), lambda p, i: (0, 2 * npair + p))],
        out_specs=[pl.BlockSpec((t, LANES), lambda p, i: (i, p)), stat, stat,
                   pl.BlockSpec(memory_space=pltpu.SMEM)],
        out_shape=[jax.ShapeDtypeStruct((s, SB_HEADS * SB_HD), BF16), jax.ShapeDtypeStruct((npair, s, 1), F32),
                   jax.ShapeDtypeStruct((npair, s, 1), F32), jax.ShapeDtypeStruct((npair, nq), F32)],
        scratch_shapes=[pltpu.VMEM((t, LANES), F32)] * 3,
        compiler_params=pltpu.CompilerParams(
            dimension_semantics=("arbitrary", "arbitrary"), vmem_limit_bytes=VMEM_ATT),
    )(qkv, qkv, qkv)


def _sb_bwd(qkv, do, btot_a, btot_b, jstart):
    s = qkv.shape[0]
    t = min(SB_BLOCK, s)
    nq = s // t
    npair = SB_HEADS // 2
    scale = SB_HD ** -0.5

    def body(js_ref, q_ref, k_ref, v_ref, do_ref, bta_ref, btb_ref, dq_ref, dk_ref, dv_ref,
             dq_acc, pa_ref, pb_ref, ga_ref, gb_ref, dk_acc, dv_acc):
        p, i = pl.program_id(0), pl.program_id(1)

        @pl.when(i == 0)
        def _():
            dk_acc[...] = jnp.zeros_like(dk_acc)
            dv_acc[...] = jnp.zeros_like(dv_acc)

        q2 = q_ref[...] * scale
        do2 = do_ref[...]
        first = lax.broadcasted_iota(jnp.int32, (t, LANES), 1) < SB_HD
        zero = jnp.zeros_like(q2)
        q_heads = (jnp.where(first, q2, zero), jnp.where(first, zero, q2))
        do_heads = (jnp.where(first, do2, zero), jnp.where(first, zero, do2))
        bts = (_wide(_lanes(bta_ref[0]), t), _wide(_lanes(btb_ref[0]), t))
        row = lax.broadcasted_iota(jnp.int32, (t, t), 0)
        col = lax.broadcasted_iota(jnp.int32, (t, t), 1)
        upto = jnp.where(row <= col, 1.0, 0.0).astype(BF16)
        before = jnp.where(row < col, 1.0, 0.0).astype(BF16)
        dq_acc[...] = jnp.zeros_like(dq_acc)
        for r in (pa_ref, pb_ref, ga_ref, gb_ref):
            r[...] = jnp.zeros_like(r)
        j0 = jnp.clip(js_ref[p, i].astype(jnp.int32), 0, i)

        def block(j, diagonal):
            off = pl.multiple_of(j * t, t)
            k2 = k_ref[pl.ds(off, t), :]
            v2 = v_ref[pl.ds(off, t), :]
            valid = col < row if diagonal else None
            dqs, dk2, dv2 = [], None, None
            for qh, doh, bt, p_ref, g_ref in zip(q_heads, do_heads, bts, (pa_ref, pb_ref), (ga_ref, gb_ref)):
                a, b = _sb_weights(qh, k2, valid)
                bh, bl = _split_bf16(b)
                pin = _dot(bh, upto, NN) + _dot(bl, upto, NN)
                surv = bt - (_wide(p_ref[...], t) + pin)
                w = jnp.exp(a + surv)
                if diagonal:
                    w = jnp.where(valid, w, 0.0)
                g = w * _dot(doh, v2, NT)
                gh, gl = _split_bf16(g)
                gsum = _wide(g_ref[...], t) + _dot(gh, before, NN) + _dot(gl, before, NN)
                beta = jnp.exp(a)
                dz = g * (1.0 - beta) - gsum * beta
                if diagonal:
                    dz = jnp.where(valid, dz, 0.0)
                dz = dz.astype(BF16)
                dqs.append(_dot(dz, k2, NN))
                dkh = _dot(dz, qh, TN)
                dvh = _dot(w.astype(BF16), doh, TN)
                dk2 = dkh if dk2 is None else dk2 + dkh
                dv2 = dvh if dv2 is None else dv2 + dvh
                p_ref[...] += jnp.sum(b, axis=1, keepdims=True)
                g_ref[...] += jnp.sum(g, axis=1, keepdims=True)
            dq_acc[...] += jnp.where(first, dqs[0], dqs[1])
            dk_acc[pl.ds(off, t), :] += dk2
            dv_acc[pl.ds(off, t), :] += dv2

        def step(j, carry):
            block(j, False)
            return carry

        lax.fori_loop(j0, jnp.maximum(i - 1, j0), step, 0)

        @pl.when(j0 < i)
        def _():
            block(i - 1, False)
            block(i, True)

        @pl.when(j0 >= i)
        def _():
            block(i, True)

        dq_ref[...] = (dq_acc[...] * scale).astype(dq_ref.dtype)

        @pl.when(i == nq - 1)
        def _():
            dk_ref[...] = dk_acc[...].astype(dk_ref.dtype)
            dv_ref[...] = dv_acc[...].astype(dv_ref.dtype)

    blk = pl.BlockSpec((t, LANES), lambda p, i: (i, p))
    full = pl.BlockSpec((s, LANES), lambda p, i: (0, p))
    stat = pl.BlockSpec((1, t, 1), lambda p, i: (p, i, 0))
    vec = pltpu.VMEM((t, LANES), F32)
    return pl.pallas_call(
        body, name="sb_attn_bwd", grid=(npair, nq),
        in_specs=[pl.BlockSpec(memory_space=pltpu.SMEM), blk,
                  pl.BlockSpec((s, LANES), lambda p, i: (0, npair + p)),
                  pl.BlockSpec((s, LANES), lambda p, i: (0, 2 * npair + p)), blk, stat, stat],
        out_specs=[blk, full, full],
        out_shape=[jax.ShapeDtypeStruct((s, SB_HEADS * SB_HD), BF16)] * 3,
        scratch_shapes=[pltpu.VMEM((t, LANES), F32), vec, vec, vec, vec,
                        pltpu.VMEM((s, LANES), F32), pltpu.VMEM((s, LANES), F32)],
        compiler_params=pltpu.CompilerParams(
            dimension_semantics=("arbitrary", "arbitrary"), vmem_limit_bytes=VMEM_ATT),
    )(jstart, qkv, qkv, qkv, do, btot_a, btot_b)


@jax.custom_vjp
def sb_attention(qkv):
    return _sb_fwd(qkv)[0]


def _sb_attention_fwd(qkv):
    o, btot_a, btot_b, jstart = _sb_fwd(qkv)
    return o, (qkv, btot_a, btot_b, jstart)


def _sb_attention_bwd(res, do):
    qkv, btot_a, btot_b, jstart = res
    dq, dk, dv = _sb_bwd(qkv, do, btot_a, btot_b, jstart)
    return (jnp.concatenate([dq, dk, dv], axis=1),)


sb_attention.defvjp(_sb_attention_fwd, _sb_attention_bwd)


def _chunk_allowed(row, col):
    return (col >> CHUNK_SHIFT) <= (row >> CHUNK_SHIFT)


def _mla_fwd(q, k, vx, tabs):
    s = q.shape[0]
    h_ = q.shape[1] // LANES
    t = min(MLA_BLOCK, s)
    nq = s // t
    scale = QK ** -0.5
    hp = MLA_GROUP

    def body(q_ref, k_ref, v_ref, c_ref, a_ref, b_ref, o_ref, lse_ref, acc_ref, m_ref):
        i = pl.program_id(1)
        acc_ref[...] = jnp.zeros_like(acc_ref)
        m_ref[...] = jnp.full_like(m_ref, NEG_BIG)
        rot = (c_ref[...], a_ref[...], b_ref[...])
        qs = [_rope_head(q_ref[:, hh * LANES:(hh + 1) * LANES], *rot).astype(BF16) for hh in range(hp)]

        def tile(j, diagonal):
            off = pl.multiple_of(j * t, t)
            for hh in range(hp):
                lanes = slice(hh * LANES, (hh + 1) * LANES)
                kj = k_ref[pl.ds(off, t), lanes]
                vj = v_ref[pl.ds(off, t), lanes]
                sc = _dot(qs[hh], kj, NT) * (scale * LOG2_E)
                if diagonal:
                    row = lax.broadcasted_iota(jnp.int32, (t, t), 0)
                    col = lax.broadcasted_iota(jnp.int32, (t, t), 1)
                    sc = jnp.where(_chunk_allowed(row, col), sc, NEG_BIG)
                m_old = m_ref[hh]
                m_new = jnp.maximum(m_old, jnp.max(sc, axis=1, keepdims=True))
                p = jnp.exp2(sc - _wide(m_new, t))
                acc_ref[hh] = jnp.exp2(m_old - m_new) * acc_ref[hh] + _dot(p.astype(BF16), vj, NN)
                m_ref[hh] = m_new

        def step(jj, carry):
            tile(2 * jj, False)
            tile(2 * jj + 1, False)
            return carry

        lax.fori_loop(0, i // 2, step, 0)

        @pl.when(i % 2 == 1)
        def _():
            tile(i - 1, False)

        tile(i, True)
        first = lax.broadcasted_iota(jnp.int32, (t, LANES), 1) < VDIM
        outs = []
        for hh in range(hp):
            acc = acc_ref[hh]
            den = acc[:, VDIM:VDIM + 1]
            outs.append(acc / den)
            lse_ref[hh] = (m_ref[hh][:, :1] + jnp.log2(den)) * LN_2
        for pp in range(hp // 2):
            o_ref[:, pp * LANES:(pp + 1) * LANES] = jnp.where(
                first, outs[2 * pp], pltpu.roll(outs[2 * pp + 1], VDIM, 1))

    wide = hp * LANES
    return pl.pallas_call(
        body, name="mla_attn_fwd", grid=(h_ // hp, nq),
        in_specs=[pl.BlockSpec((t, wide), lambda g, i: (i, g)),
                  pl.BlockSpec((s, wide), lambda g, i: (0, g)),
                  pl.BlockSpec((s, wide), lambda g, i: (0, g))] + [pl.BlockSpec((t, LANES), lambda g, i: (i, 0))] * 3,
        out_specs=[pl.BlockSpec((t, hp * VDIM), lambda g, i: (i, g)),
                   pl.BlockSpec((hp, t, 1), lambda g, i: (g, i, 0))],
        out_shape=[jax.ShapeDtypeStruct((s, h_ * VDIM), F32), jax.ShapeDtypeStruct((h_, s, 1), F32)],
        scratch_shapes=[pltpu.VMEM((hp, t, LANES), F32), pltpu.VMEM((hp, t, LANES), F32)],
        compiler_params=pltpu.CompilerParams(
            dimension_semantics=("arbitrary", "arbitrary"), vmem_limit_bytes=VMEM_ATT),
    )(q, k, vx, *tabs)


def _mla_bwd(q, k, vx, tabs, o, lse, do):
    s = q.shape[0]
    h_ = q.shape[1] // LANES
    t = min(MLA_BLOCK, s)
    nq = s // t
    scale = QK ** -0.5

    def body(q_ref, k_ref, v_ref, c_ref, a_ref, b_ref, o_ref, lse_ref, do_ref, dq_ref, dk_ref, dv_ref, dq_acc):
        h, i = pl.program_id(0), pl.program_id(1)
        rot = (c_ref[...], a_ref[...], b_ref[...])

        @pl.when(i == 0)
        def _():
            dk_ref[...] = jnp.zeros_like(dk_ref)
            dv_ref[...] = jnp.zeros_like(dv_ref)

        qv = _rope_head(q_ref[...], *rot).astype(BF16)
        lane = lax.broadcasted_iota(jnp.int32, (t, LANES), 1)
        start = (h % 2) * VDIM
        mine = jnp.logical_and(lane >= start, lane < start + VDIM)
        do2 = do_ref[...]
        delta = _lanes(jnp.sum(jnp.where(mine, do2 * o_ref[...], 0.0), axis=1, keepdims=True))
        odd = (start + jnp.zeros_like(lane)) > 0
        do_head = jnp.where(odd, pltpu.roll(do2, VDIM, 1), do2)
        dov = jnp.where(lane < VDIM, do_head, 0.0).astype(BF16)
        lse = _lanes(lse_ref[0])
        dq_acc[...] = jnp.zeros_like(dq_acc)

        def tile(j, diagonal):
            off = pl.multiple_of(j * t, t)
            kj = k_ref[pl.ds(off, t), :]
            vj = v_ref[pl.ds(off, t), :]
            p = jnp.exp(_dot(qv, kj, NT) * scale - _wide(lse, t))
            if diagonal:
                row = lax.broadcasted_iota(jnp.int32, (t, t), 0)
                col = lax.broadcasted_iota(jnp.int32, (t, t), 1)
                p = jnp.where(_chunk_allowed(row, col), p, 0.0)
            dp = _dot(dov, vj, NT)
            ds = (p * (dp - _wide(delta, t)) * scale).astype(BF16)
            dq_acc[...] += _dot(ds, kj, NN)
            dk_ref[pl.ds(off, t), :] += _dot(ds, qv, TN)
            dv_ref[pl.ds(off, t), :] += _dot(p.astype(BF16), dov, TN)

        def step(jj, carry):
            for u in range(MLA_UNROLL):
                tile(MLA_UNROLL * jj + u, False)
            return carry

        def single(j, carry):
            tile(j, False)
            return carry

        lax.fori_loop(0, i // MLA_UNROLL, step, 0)
        lax.fori_loop(i - i % MLA_UNROLL, i, single, 0)
        tile(i, True)
        dq_ref[...] = _rope_head_t(dq_acc[...], *rot)

    blk = pl.BlockSpec((t, LANES), lambda h, i: (i, h))
    full = pl.BlockSpec((s, LANES), lambda h, i: (0, h))
    pair = pl.BlockSpec((t, LANES), lambda h, i: (i, h // 2))
    stat = pl.BlockSpec((1, t, 1), lambda h, i: (h, i, 0))
    return pl.pallas_call(
        body, name="mla_attn_bwd", grid=(h_, nq),
        in_specs=[blk, full, full] + [pl.BlockSpec((t, LANES), lambda h, i: (i, 0))] * 3 + [pair, stat, pair],
        out_specs=[blk, full, full],
        out_shape=[jax.ShapeDtypeStruct((s, h_ * LANES), F32)] * 3,
        scratch_shapes=[pltpu.VMEM((t, LANES), F32)],
        compiler_params=pltpu.CompilerParams(
            dimension_semantics=("arbitrary", "arbitrary"), vmem_limit_bytes=VMEM_ATT),
    )(q, k, vx, *tabs, o, lse, do)


def _assemble_kv(kvp, kr):
    half = kvp.shape[1] // 2
    ones = jnp.tile((jnp.arange(LANES) == VDIM).astype(F32), MLA_HEADS)
    k = kvp[:, :half] + jnp.tile(kr, (1, MLA_HEADS))
    return k.astype(BF16), (kvp[:, half:] + ones).astype(BF16)


@jax.custom_vjp
def mla_attention(q, kvp, kr, c, a, b):
    return _mla_fwd(q, *_assemble_kv(kvp, kr), (c, a, b))[0]


def _mla_attention_fwd(q, kvp, kr, c, a, b):
    kb, vb = _assemble_kv(kvp, kr)
    o, lse = _mla_fwd(q, kb, vb, (c, a, b))
    return o, (q, kb, vb, (c, a, b), o, lse)


def _mla_attention_bwd(res, do):
    q, kb, vb, tabs, o, lse = res
    dq, dk, dvx = _mla_bwd(q, kb, vb, tabs, o, lse, do)
    dkr = dk.reshape(dk.shape[0], MLA_HEADS, LANES).sum(axis=1)
    return (dq, jnp.concatenate([dk, dvx], axis=1), dkr) + tuple(jnp.zeros_like(t) for t in tabs)


mla_attention.defvjp(_mla_attention_fwd, _mla_attention_bwd)


def _add_tile(acc, res):
    return (acc + res,)


def _make_linear_res(tag, a_dtype=F32):
    def forward(x, a, slot, wb):
        ab = a.astype(BF16)
        y = _matmul(ab, wb, epilogue=_add_tile, extras=(x,), name=tag + "_fwd")[0]
        return y, (ab, wb)

    @jax.custom_vjp
    def f(x, a, slot, wb):
        return forward(x, a, slot, wb)[0]

    def bwd(res, dy):
        ab, wb = res
        dyb = dy.astype(BF16)
        da = _matmul(dyb, wb, tb=True, out_dtypes=(a_dtype,), name=tag + "_da")[0]
        dw = _matmul(ab, dyb, ta=True, name=tag + "_dw")[0]
        return dy, da, dw, jnp.zeros_like(wb)

    f.defvjp(forward, bwd)
    return f


def _make_norm_linear(tag, out_dtype=F32):
    def forward(x, g, slot, wb):
        hb = _rms_fwd(x, g, tag + "_norm")
        y = _matmul(hb, wb, out_dtypes=(out_dtype,), name=tag + "_fwd")[0]
        return y, (x, g, wb, hb)

    @jax.custom_vjp
    def f(x, g, slot, wb):
        return forward(x, g, slot, wb)[0]

    def bwd(res, dy):
        x, g, wb, hb = res
        dyb = dy.astype(BF16)
        dh = _matmul(dyb, wb, tb=True, name=tag + "_dh")[0]
        dw = _matmul(hb, dyb, ta=True, name=tag + "_dw")[0]
        dx, dg = _rms_bwd(x, g, dh, tag + "_norm_bwd")
        return dx, dg, dw, jnp.zeros_like(wb)

    f.defvjp(forward, bwd)
    return f


def _relu2_fwd(acc):
    r = jnp.maximum(acc, 0.0)
    return acc, r * r


def _relu2_bwd(acc, u):
    return (acc * (2.0 * jnp.maximum(u.astype(F32), 0.0)),)


def _make_mlp_res(tag):
    def forward(x, g, slot1, slot2, w1b, w2b):
        hb = _rms_fwd(x, g, tag + "_norm")
        u, act = _matmul(hb, w1b, out_dtypes=(BF16, BF16), epilogue=_relu2_fwd, name=tag + "_up")
        y = _matmul(act, w2b, epilogue=_add_tile, extras=(x,), name=tag + "_down")[0]
        return y, (x, g, w1b, w2b, hb, u, act)

    @jax.custom_vjp
    def f(x, g, slot1, slot2, w1b, w2b):
        return forward(x, g, slot1, slot2, w1b, w2b)[0]

    def bwd(res, dy):
        x, g, w1b, w2b, hb, u, act = res
        dyb = dy.astype(BF16)
        du = _matmul(dyb, w2b, tb=True, out_dtypes=(BF16,), epilogue=_relu2_bwd, extras=(u,), name=tag + "_du")[0]
        dw2 = _matmul(act, dyb, ta=True, name=tag + "_dw2")[0]
        dw1 = _matmul(hb, du, ta=True, name=tag + "_dw1")[0]
        dh = _matmul(du, w1b, tb=True, name=tag + "_dh")[0]
        dx, dg = _rms_bwd(x, g, dh, tag + "_norm_bwd", res=dy)
        return dx, dg, dw1, dw2, jnp.zeros_like(w1b), jnp.zeros_like(w2b)

    f.defvjp(forward, bwd)
    return f


def _make_rope(tag):
    @jax.custom_vjp
    def f(x, c, a, b):
        return _rope_apply(x, (c, a, b), False, tag + "_fwd")

    def fwd(x, c, a, b):
        return _rope_apply(x, (c, a, b), False, tag + "_fwd"), (c, a, b)

    def bwd(res, dy):
        c, a, b = res
        return _rope_apply(dy, (c, a, b), True, tag + "_bwd"), jnp.zeros_like(c), jnp.zeros_like(a), jnp.zeros_like(b)

    f.defvjp(fwd, bwd)
    return f


def _rope_tables(positions):
    half = ROPE // 2
    inv_freq = ROPE_THETA ** (-jnp.arange(0, ROPE, 2, dtype=F32) / ROPE)
    ang = positions.astype(F32)[:, None] * inv_freq
    cos, sin = jnp.cos(ang), jnp.sin(ang)
    s = positions.shape[0]
    one = lambda n: jnp.ones((s, n), F32)
    zero = lambda n: jnp.zeros((s, n), F32)

    def tables(before, after):
        return (jnp.concatenate([one(before), cos, cos, one(after)], axis=1),
                jnp.concatenate([zero(before), -sin, zero(half + after)], axis=1),
                jnp.concatenate([zero(before + half), sin, zero(after)], axis=1))

    return tables(NOPE, LANES - QK), tables(KV_RANK, DKV_PAD - KV_RANK - ROPE)


def _pad_heads(w, per):
    lead = w.shape[:-1]
    w = jnp.pad(w.reshape(lead + (MLA_HEADS, per)), [(0, 0)] * len(lead) + [(0, 0), (0, LANES - per)])
    return w.reshape(lead + (MLA_HEADS * LANES,))


def _split_kv_heads(w):
    w3 = w.reshape(w.shape[0], MLA_HEADS, NOPE + VDIM)
    return jnp.concatenate([_pad_heads(w3[..., :NOPE].reshape(w.shape[0], -1), NOPE),
                            _pad_heads(w3[..., NOPE:].reshape(w.shape[0], -1), VDIM)], axis=1)


def _trunk(x, slots, norms, wb, q_tabs, kv_tabs):
    kvp = kr = None
    for layer in range(DEPTH):
        if layer < N_A:
            qkv = _make_norm_linear(f"sb{layer}_qkv", BF16)(
                x, norms['attn_norm'][layer], slots['sb_w_qkv'][layer], wb['sb_w_qkv'][layer])
            o = sb_attention(qkv)
            x = _make_linear_res(f"sb{layer}_o", BF16)(x, o, slots['sb_w_o'][layer], wb['sb_w_o'][layer])
        else:
            j = layer - N_A
            if j == 0:
                pad = ((0, 0), (0, DKV_PAD - KV_RANK - ROPE))
                down = _make_norm_linear("kv_down")(
                    x, norms['kv_norm'], jnp.pad(slots['mla_w_dkv'][0], pad), jnp.pad(wb['mla_w_dkv'][0], pad))
                kvp = _make_norm_linear("kv_up")(
                    down[:, :KV_RANK], norms['mla_kv_lat_norm'],
                    _split_kv_heads(slots['mla_w_ukv'][0]), _split_kv_heads(wb['mla_w_ukv'][0]))
                k_rope = _make_rope("rope_k")(down, *kv_tabs)[:, KV_RANK:KV_RANK + ROPE]
                kr = jnp.pad(k_rope, ((0, 0), (NOPE, LANES - QK)))
            c_q = _make_norm_linear(f"mla{j}_dq")(
                x, norms['attn_norm'][layer], slots['mla_w_dq'][j], wb['mla_w_dq'][j])
            q = _make_norm_linear(f"mla{j}_uq")(
                c_q, norms['mla_q_lat_norm'][j], _pad_heads(slots['mla_w_uq'][j], QK), _pad_heads(wb['mla_w_uq'][j], QK))
            o = mla_attention(q, kvp, kr, *q_tabs)
            x = _make_linear_res(f"mla{j}_o")(x, o, slots['mla_w_o'][j], wb['mla_w_o'][j])
        x = _make_mlp_res(f"mlp{layer}")(
            x, norms['mlp_norm'][layer], slots['mlp_w1'][layer], slots['mlp_w2'][layer],
            wb['mlp_w1'][layer], wb['mlp_w2'][layer])
    return x


def _merge_blocks(gathered, ax):
    shp = gathered.shape[1:]
    return jnp.moveaxis(gathered, 0, ax).reshape(shp[:ax] + (N_DEV * shp[ax],) + shp[ax + 1:])


def _split_blocks(full, ax):
    shp = full.shape
    return jnp.moveaxis(full.reshape(shp[:ax] + (N_DEV, shp[ax] // N_DEV) + shp[ax + 1:]), ax, 0)


def _pack(parts):
    flat = jnp.concatenate([p.reshape(-1) for p in parts])
    rows = -(-flat.shape[0] // PACK_COLS)
    rows = -(-rows // PACK_ROW_ALIGN) * PACK_ROW_ALIGN
    return jnp.pad(flat, (0, rows * PACK_COLS - flat.shape[0])).reshape(rows, PACK_COLS)


def _unpack(packed, shapes):
    flat = packed.reshape(-1)
    out, off = [], 0
    for shp in shapes:
        n = math.prod(shp)
        out.append(flat[off:off + n].reshape(shp))
        off += n
    return out


def _add_round(x, stage, core, name):
    _, r, c = x.shape
    tr = _tile(r, (256, 128, 64, 32, 16, 8))

    def body(core_ref, x_ref, s_ref, o_ref):
        o_ref[...] = (x_ref[0] + s_ref[...]).astype(o_ref.dtype)

    return pl.pallas_call(
        body, name=name,
        grid_spec=pltpu.PrefetchScalarGridSpec(
            num_scalar_prefetch=1, grid=(r // tr,),
            in_specs=[pl.BlockSpec((1, tr, c), lambda i, core_ref: (core_ref[0], i, 0)),
                      pl.BlockSpec((tr, c), lambda i, core_ref: (i, 0))],
            out_specs=pl.BlockSpec((tr, c), lambda i, core_ref: (i, 0))),
        out_shape=jax.ShapeDtypeStruct((r, c), BF16),
        compiler_params=pltpu.CompilerParams(dimension_semantics=("parallel",)),
    )(core, x, stage)


def _adamw_reduce(parts, w, m, v, name):
    n_parts, r, c = parts.shape
    tr = _tile(r, (128, 64, 32, 16, 8))
    bias1 = 1.0 - ADAM_B1 ** ADAM_STEP
    bias2 = 1.0 - ADAM_B2 ** ADAM_STEP

    def body(p_ref, w_ref, m_ref, v_ref, g_ref, d_ref, nm_ref, nv_ref):
        g = p_ref[0].astype(F32)
        for s in range(1, n_parts):
            g = g + p_ref[s].astype(F32)
        mn = ADAM_B1 * m_ref[...] + (1.0 - ADAM_B1) * g
        vn = ADAM_B2 * v_ref[...] + (1.0 - ADAM_B2) * (g * g)
        m_hat = mn / bias1
        v_hat = vn / bias2
        g_ref[...] = g
        d_ref[...] = -ADAM_LR * (m_hat / (jnp.sqrt(v_hat) + ADAM_EPS) + ADAM_WD * w_ref[...])
        nm_ref[...] = mn
        nv_ref[...] = vn

    blk = pl.BlockSpec((tr, c), lambda i: (i, 0))
    return pl.pallas_call(
        body, name=name, grid=(r // tr,),
        in_specs=[pl.BlockSpec((n_parts, tr, c), lambda i: (0, i, 0)), blk, blk, blk],
        out_specs=[blk] * 4,
        out_shape=[jax.ShapeDtypeStruct((r, c), F32)] * 4,
        compiler_params=pltpu.CompilerParams(dimension_semantics=("parallel",), vmem_limit_bytes=VMEM_MM),
    )(parts, w, m, v)


def kernel(x, positions, attn_norm, mlp_norm, sb_w_qkv, sb_w_o, kv_norm, mla_w_dkv, mla_kv_lat_norm, mla_w_ukv, mla_w_dq, mla_q_lat_norm, mla_w_uq, mla_w_o, mlp_w1, mlp_w2, final_norm, loss_target, m_attn_norm, m_mlp_norm, m_sb_w_qkv, m_sb_w_o, m_kv_norm, m_mla_w_dkv, m_mla_kv_lat_norm, m_mla_w_ukv, m_mla_w_dq, m_mla_q_lat_norm, m_mla_w_uq, m_mla_w_o, m_mlp_w1, m_mlp_w2, m_final_norm, v_attn_norm, v_mlp_norm, v_sb_w_qkv, v_sb_w_o, v_kv_norm, v_mla_w_dkv, v_mla_kv_lat_norm, v_mla_w_ukv, v_mla_w_dq, v_mla_q_lat_norm, v_mla_w_uq, v_mla_w_o, v_mlp_w1, v_mlp_w2, v_final_norm):
    weights = dict(attn_norm=attn_norm, mlp_norm=mlp_norm, sb_w_qkv=sb_w_qkv, sb_w_o=sb_w_o, kv_norm=kv_norm,
                   mla_w_dkv=mla_w_dkv, mla_kv_lat_norm=mla_kv_lat_norm, mla_w_ukv=mla_w_ukv, mla_w_dq=mla_w_dq,
                   mla_q_lat_norm=mla_q_lat_norm, mla_w_uq=mla_w_uq, mla_w_o=mla_w_o, mlp_w1=mlp_w1, mlp_w2=mlp_w2,
                   final_norm=final_norm)
    mom_m = dict(attn_norm=m_attn_norm, mlp_norm=m_mlp_norm, sb_w_qkv=m_sb_w_qkv, sb_w_o=m_sb_w_o, kv_norm=m_kv_norm,
                 mla_w_dkv=m_mla_w_dkv, mla_kv_lat_norm=m_mla_kv_lat_norm, mla_w_ukv=m_mla_w_ukv, mla_w_dq=m_mla_w_dq,
                 mla_q_lat_norm=m_mla_q_lat_norm, mla_w_uq=m_mla_w_uq, mla_w_o=m_mla_w_o, mlp_w1=m_mlp_w1,
                 mlp_w2=m_mlp_w2, final_norm=m_final_norm)
    mom_v = dict(attn_norm=v_attn_norm, mlp_norm=v_mlp_norm, sb_w_qkv=v_sb_w_qkv, sb_w_o=v_sb_w_o, kv_norm=v_kv_norm,
                 mla_w_dkv=v_mla_w_dkv, mla_kv_lat_norm=v_mla_kv_lat_norm, mla_w_ukv=v_mla_w_ukv, mla_w_dq=v_mla_w_dq,
                 mla_q_lat_norm=v_mla_q_lat_norm, mla_w_uq=v_mla_w_uq, mla_w_o=v_mla_w_o, mlp_w1=v_mlp_w1,
                 mlp_w2=v_mlp_w2, final_norm=v_final_norm)
    sharded_names = [n for n, _ in SHARDED]
    repl_shapes = [tuple(weights[n].shape) for n in REPLICATED]

    gathered = _all_gather([weights[n].astype(BF16) for n in sharded_names], "gather_weights")
    wb, slots = {}, {}
    for (n, ax), g in zip(SHARDED, gathered):
        full = _merge_blocks(g, ax)
        layers = [full[l] for l in range(full.shape[0])] if full.ndim == 3 else [full]
        wb[n] = layers
        slots[n] = [jnp.zeros(w.shape, F32) for w in layers]
    norms = {n: weights[n] for n in REPLICATED if n != 'final_norm'}

    q_tabs, kv_tabs = _rope_tables(positions[0])
    x_last, pullback = jax.vjp(lambda xx, ss, nn: _trunk(xx, ss, nn, wb, q_tabs, kv_tabs), x[0], slots, norms)
    loss_part, dx_last, d_final = _loss_head(x_last, final_norm, loss_target[0])
    dx, d_slots, d_norms = pullback(dx_last)
    d_norms = dict(d_norms)
    d_norms['final_norm'] = d_final
    loss = lax.psum(loss_part[0, 0], ("x", "y", "c"))

    core = lax.axis_index("c").astype(jnp.int32).reshape(1)
    halves, dims = [], []
    for n, ax in SHARDED:
        full = jnp.stack(d_slots[n]) if weights[n].ndim == 3 else d_slots[n][0]
        blocks = _split_blocks(full, ax)
        two_d = (math.prod(blocks.shape[1:-1]), blocks.shape[-1])
        dims.append(two_d)
        halves.append(jnp.moveaxis(blocks.reshape((N_DEV // 2, 2) + two_d), 1, 0))
    staged = _pair_exchange(halves, "pair_grads")
    sums = []
    for n, h, st, (r, c) in zip(sharded_names, halves, staged, dims):
        rows = (N_DEV // 2) * r
        sums.append(_add_round(h.reshape(2, rows, c), st.reshape(rows, c), core, "pair_sum_" + n).reshape(st.shape))
    received = _chip_exchange(sums, "scatter_grads")
    repl_parts = _all_gather([_pack([d_norms[n] for n in REPLICATED])], "gather_norm_grads")[0]

    results = {kind: {} for kind in ("grad", "delta", "new_m", "new_v")}
    for n, parts, two_d in zip(sharded_names, received, dims):
        shp = weights[n].shape
        res = _adamw_reduce(parts, weights[n].reshape(two_d), mom_m[n].reshape(two_d),
                            mom_v[n].reshape(two_d), "adamw_" + n)
        for kind, a in zip(results, res):
            results[kind][n] = a.reshape(shp)
    res = _adamw_reduce(repl_parts, _pack([weights[n] for n in REPLICATED]), _pack([mom_m[n] for n in REPLICATED]),
                        _pack([mom_v[n] for n in REPLICATED]), "adamw_replicated")
    for kind, a in zip(results, res):
        results[kind].update(zip(REPLICATED, _unpack(a, repl_shapes)))

    out = [loss, dx[None]]
    for kind in ("grad", "delta", "new_m", "new_v"):
        out += [results[kind][n] for n in WEIGHT_ORDER]
    return tuple(out)
```

```python
import math

import jax
import jax.numpy as jnp
from jax import lax
from jax.experimental import pallas as pl
from jax.experimental.pallas import tpu as pltpu

F32 = jnp.float32
BF16 = jnp.bfloat16
MESH = pl.DeviceIdType.MESH

N_DEV = 8
DEPTH = 4
N_A = 2
SB_HEADS = 16
SB_HD = 64
MLA_HEADS = 16
NOPE = 64
ROPE = 32
VDIM = 64
QK = NOPE + ROPE
KV_RANK = 256
DKV_PAD = 384
LANES = 128
CHUNK_SHIFT = 6
ROPE_THETA = 10000.0
EPS = 1e-6
SB_BLOCK = 256
MLA_BLOCK = 512
MLA_GROUP = 4
MLA_UNROLL = 4
PACK_COLS = 1024
PACK_ROW_ALIGN = 16
EXP_ZERO = -104.0
NEG_BIG = -1e30
LOG2_E = 1.4426950408889634
LN_2 = 0.6931471805599453
VMEM_ATT = 56 * 1024 * 1024
VMEM_MM = 48 * 1024 * 1024

ADAM_LR = 0.001
ADAM_B1 = 0.9
ADAM_B2 = 0.999
ADAM_EPS = 1e-08
ADAM_WD = 0.01
ADAM_STEP = 10

WEIGHT_ORDER = ['attn_norm', 'mlp_norm', 'sb_w_qkv', 'sb_w_o', 'kv_norm', 'mla_w_dkv', 'mla_kv_lat_norm',
                'mla_w_ukv', 'mla_w_dq', 'mla_q_lat_norm', 'mla_w_uq', 'mla_w_o', 'mlp_w1', 'mlp_w2', 'final_norm']
SHARDED = [('sb_w_qkv', 2), ('sb_w_o', 1), ('mla_w_dkv', 0), ('mla_w_ukv', 1), ('mla_w_dq', 1),
           ('mla_w_uq', 2), ('mla_w_o', 1), ('mlp_w1', 2), ('mlp_w2', 1)]
REPLICATED = ['attn_norm', 'mlp_norm', 'kv_norm', 'mla_kv_lat_norm', 'mla_q_lat_norm', 'final_norm']


def _tile(dim, prefs=(512, 384, 256, 128, 64, 32, 16, 8)):
    for t in prefs:
        if dim % t == 0:
            return t
    return dim


def _dot(a, b, dims):
    return lax.dot_general(a, b, (dims, ((), ())), preferred_element_type=F32)


NN = ((1,), (0,))
NT = ((1,), (1,))
TN = ((0,), (0,))


def _gather_phases(x_refs, out_refs, send_sems, recv_sems, local_sems):
    n_t = len(x_refs)
    x, y, c = lax.axis_index("x"), lax.axis_index("y"), lax.axis_index("c")
    me, sibling = (x, y, c), (x, y, 1 - c)
    chips = [(1 - x, y), (x, 1 - y), (1 - x, 1 - y)]

    def slot(t, px, py, pc):
        return out_refs[t].at[4 * px + 2 * py + pc]

    def copy(t, k, block, to, src=None):
        return pltpu.make_async_remote_copy(
            src_ref=slot(t, *block) if src is None else src, dst_ref=slot(t, *block),
            send_sem=send_sems.at[7 * t + k], recv_sem=recv_sems.at[7 * t + k],
            device_id=to, device_id_type=MESH)

    def mine():
        return [pltpu.make_async_copy(x_refs[t], slot(t, *me), local_sems.at[t]) for t in range(n_t)]

    def first():
        out = []
        for t in range(n_t):
            out.append(copy(t, 0, me, sibling, src=x_refs[t]))
            out += [copy(t, 1 + j, me, (*chip, c), src=x_refs[t]) for j, chip in enumerate(chips)]
        return out

    def passed():
        return [copy(t, 4 + j, (*chip, c), sibling) for t in range(n_t) for j, chip in enumerate(chips)]

    def start():
        for cp in mine() + first():
            cp.start()

    def forward():
        onward = passed()
        for t in range(n_t):
            for j, chip in enumerate(chips):
                copy(t, 1 + j, (*chip, c), me).wait_recv()
                onward[3 * t + j].start()

    def finish():
        for t in range(n_t):
            copy(t, 0, sibling, me).wait_recv()
            for j, chip in enumerate(chips):
                copy(t, 4 + j, (*chip, 1 - c), me).wait_recv()
        for cp in first() + passed():
            cp.wait_send()
        for cp in mine():
            cp.wait()

    return start, forward, finish


def _gather_scratch(n_t):
    return [pltpu.SemaphoreType.DMA((7 * n_t,)), pltpu.SemaphoreType.DMA((7 * n_t,)), pltpu.SemaphoreType.DMA((n_t,))]


def _all_gather(shards, name):
    n_t = len(shards)

    def body(*refs):
        for phase in _gather_phases(refs[:n_t], refs[n_t:2 * n_t], *refs[2 * n_t:]):
            phase()

    any_spec = pl.BlockSpec(memory_space=pl.ANY)
    return pl.pallas_call(
        body, name=name,
        out_shape=[jax.ShapeDtypeStruct((N_DEV,) + tuple(s.shape), s.dtype) for s in shards],
        in_specs=[any_spec] * n_t, out_specs=[any_spec] * n_t,
        scratch_shapes=_gather_scratch(n_t),
    )(*shards)


def _pair_exchange(xs, name):
    n_t = len(xs)

    def body(*refs):
        x_refs, out_refs = refs[:n_t], refs[n_t:2 * n_t]
        send_sems, recv_sems = refs[2 * n_t:]
        x, y, c = lax.axis_index("x"), lax.axis_index("y"), lax.axis_index("c")
        copies = [pltpu.make_async_remote_copy(
            src_ref=x_refs[t].at[1 - c], dst_ref=out_refs[t], send_sem=send_sems.at[t], recv_sem=recv_sems.at[t],
            device_id=(x, y, 1 - c), device_id_type=MESH) for t in range(n_t)]
        for cp in copies:
            cp.start()
        for cp in copies:
            cp.wait()

    any_spec = pl.BlockSpec(memory_space=pl.ANY)
    return pl.pallas_call(
        body, name=name,
        out_shape=[jax.ShapeDtypeStruct(x.shape[1:], x.dtype) for x in xs],
        in_specs=[any_spec] * n_t, out_specs=[any_spec] * n_t,
        scratch_shapes=[pltpu.SemaphoreType.DMA((n_t,)), pltpu.SemaphoreType.DMA((n_t,))],
    )(*xs)


def _chip_exchange(ps, name):
    n_t = len(ps)

    def body(*refs):
        p_refs, out_refs = refs[:n_t], refs[n_t:2 * n_t]
        send_sems, recv_sems, local_sems = refs[2 * n_t:]
        mx, my, mc = lax.axis_index("x"), lax.axis_index("y"), lax.axis_index("c")
        me = 2 * mx + my
        mine = [pltpu.make_async_copy(p_refs[t].at[me], out_refs[t].at[me], local_sems.at[t]) for t in range(n_t)]
        for cp in mine:
            cp.start()
        copies = []
        for k in range(1, 4):
            px = 1 - mx if (k >> 1) & 1 else mx
            py = 1 - my if k & 1 else my
            peer = 2 * px + py
            for t in range(n_t):
                copies.append(pltpu.make_async_remote_copy(
                    src_ref=p_refs[t].at[peer], dst_ref=out_refs[t].at[me],
                    send_sem=send_sems.at[3 * t + k - 1], recv_sem=recv_sems.at[3 * t + k - 1],
                    device_id=(px, py, mc), device_id_type=MESH))
        for cp in copies:
            cp.start()
        for cp in copies:
            cp.wait_send()
        for cp in copies:
            cp.wait_recv()
        for cp in mine:
            cp.wait()

    any_spec = pl.BlockSpec(memory_space=pl.ANY)
    return pl.pallas_call(
        body, name=name,
        out_shape=[jax.ShapeDtypeStruct(p.shape, p.dtype) for p in ps],
        in_specs=[any_spec] * n_t, out_specs=[any_spec] * n_t,
        scratch_shapes=[pltpu.SemaphoreType.DMA((3 * n_t,)), pltpu.SemaphoreType.DMA((3 * n_t,)),
                        pltpu.SemaphoreType.DMA((n_t,))],
    )(*ps)


def _matmul(a, b, *, ta=False, tb=False, out_dtypes=(F32,), epilogue=None, extras=(), name):
    if ta:
        kdim, m = a.shape
    else:
        m, kdim = a.shape
    if tb:
        n, kb = b.shape
    else:
        kb, n = b.shape
    assert kdim == kb, (a.shape, b.shape, ta, tb)
    big = (1024, 768, 512, 384, 256, 128, 64, 32, 16, 8)
    tm, tn = _tile(m, big), _tile(n, big)
    tk = kdim if kdim <= 2048 else _tile(kdim, (2048, 1024, 512, 256, 128))
    nk = kdim // tk
    n_extra, n_out = len(extras), len(out_dtypes)
    a_spec = pl.BlockSpec((tk, tm), lambda i, j, k: (k, i)) if ta else pl.BlockSpec((tm, tk), lambda i, j, k: (i, k))
    b_spec = pl.BlockSpec((tn, tk), lambda i, j, k: (j, k)) if tb else pl.BlockSpec((tk, tn), lambda i, j, k: (k, j))
    tile_spec = pl.BlockSpec((tm, tn), lambda i, j, k: (i, j))
    dims = ((0,) if ta else (1,), (1,) if tb else (0,))

    def finish(acc, extra_refs, out_refs):
        outs = (acc,) if epilogue is None else epilogue(acc, *[r[...] for r in extra_refs])
        for o_ref, o in zip(out_refs, outs):
            o_ref[...] = o.astype(o_ref.dtype)

    def body_one(a_ref, b_ref, *rest):
        acc = _dot(a_ref[...].astype(BF16), b_ref[...].astype(BF16), dims)
        finish(acc, rest[:n_extra], rest[n_extra:n_extra + n_out])

    def body_acc(a_ref, b_ref, *rest):
        acc_ref = rest[-1]
        k = pl.program_id(2)

        @pl.when(k == 0)
        def _():
            acc_ref[...] = jnp.zeros_like(acc_ref)

        acc_ref[...] += _dot(a_ref[...].astype(BF16), b_ref[...].astype(BF16), dims)

        @pl.when(k == nk - 1)
        def _():
            finish(acc_ref[...], rest[:n_extra], rest[n_extra:n_extra + n_out])

    return pl.pallas_call(
        body_one if nk == 1 else body_acc, name=name, grid=(m // tm, n // tn, nk),
        in_specs=[a_spec, b_spec] + [tile_spec] * n_extra,
        out_specs=[tile_spec] * n_out,
        out_shape=[jax.ShapeDtypeStruct((m, n), dt) for dt in out_dtypes],
        scratch_shapes=[] if nk == 1 else [pltpu.VMEM((tm, tn), F32)],
        compiler_params=pltpu.CompilerParams(
            dimension_semantics=("parallel", "parallel", "arbitrary"), vmem_limit_bytes=VMEM_MM),
    )(a, b, *extras)


def _rms_fwd(x, g, name):
    m, d = x.shape
    tm = _tile(m, (512, 256, 128, 64, 32, 16, 8))

    def body(x_ref, g_ref, y_ref):
        xv = x_ref[...]
        r = lax.rsqrt(jnp.mean(xv * xv, axis=-1, keepdims=True) + EPS)
        y_ref[...] = (xv * r * g_ref[...]).astype(y_ref.dtype)

    return pl.pallas_call(
        body, name=name, grid=(m // tm,),
        in_specs=[pl.BlockSpec((tm, d), lambda i: (i, 0)), pl.BlockSpec((1, d), lambda i: (0, 0))],
        out_specs=pl.BlockSpec((tm, d), lambda i: (i, 0)),
        out_shape=jax.ShapeDtypeStruct((m, d), BF16),
        compiler_params=pltpu.CompilerParams(dimension_semantics=("parallel",)),
    )(x, g.reshape(1, d))


def _rms_bwd(x, g, dy, name, res=None):
    m, d = x.shape
    tm = _tile(m, (512, 256, 128, 64, 32, 16, 8))
    has_res = res is not None

    def body(x_ref, g_ref, dy_ref, *rest):
        dx_ref, dg_ref = rest[-2:]
        xv = x_ref[...]
        dyv = dy_ref[...]
        r = lax.rsqrt(jnp.mean(xv * xv, axis=-1, keepdims=True) + EPS)
        xh = xv * r
        t = dyv * g_ref[...]
        dx = r * (t - xh * jnp.mean(t * xh, axis=-1, keepdims=True))
        dx_ref[...] = dx + rest[0][...] if has_res else dx

        @pl.when(pl.program_id(0) == 0)
        def _():
            dg_ref[...] = jnp.zeros_like(dg_ref)

        dg_ref[...] += jnp.sum(dyv * xh, axis=0, keepdims=True)

    row_spec = pl.BlockSpec((tm, d), lambda i: (i, 0))
    vec_spec = pl.BlockSpec((1, d), lambda i: (0, 0))
    dx, dg = pl.pallas_call(
        body, name=name, grid=(m // tm,),
        in_specs=[row_spec, vec_spec, row_spec] + ([row_spec] if has_res else []),
        out_specs=[row_spec, vec_spec],
        out_shape=[jax.ShapeDtypeStruct((m, d), F32), jax.ShapeDtypeStruct((1, d), F32)],
        compiler_params=pltpu.CompilerParams(dimension_semantics=("arbitrary",)),
    )(x, g.reshape(1, d), dy, *((res,) if has_res else ()))
    return dx, dg.reshape(d)


def _loss_head(x, g, target):
    m, d = x.shape
    tm = _tile(m, (512, 256, 128, 64, 32, 16, 8))

    def body(x_ref, g_ref, t_ref, loss_ref, dx_ref, dg_ref):
        xv = x_ref[...]
        gv = g_ref[...]
        r = lax.rsqrt(jnp.mean(xv * xv, axis=-1, keepdims=True) + EPS)
        xh = xv * r
        err = xh * gv - t_ref[...]
        row_loss = jnp.mean(err * err, axis=-1, keepdims=True)
        dyv = err * (1.0 / d)
        t = dyv * gv
        dx_ref[...] = r * (t - xh * jnp.mean(t * xh, axis=-1, keepdims=True))

        @pl.when(pl.program_id(0) == 0)
        def _():
            dg_ref[...] = jnp.zeros_like(dg_ref)
            loss_ref[...] = jnp.zeros_like(loss_ref)

        dg_ref[...] += jnp.sum(dyv * xh, axis=0, keepdims=True)
        loss_ref[...] += 0.5 * jnp.sum(row_loss, axis=0, keepdims=True)

    loss, dx, dg = pl.pallas_call(
        body, name="loss_head", grid=(m // tm,),
        in_specs=[pl.BlockSpec((tm, d), lambda i: (i, 0)), pl.BlockSpec((1, d), lambda i: (0, 0)),
                  pl.BlockSpec((tm, d), lambda i: (i, 0))],
        out_specs=[pl.BlockSpec((1, 1), lambda i: (0, 0)), pl.BlockSpec((tm, d), lambda i: (i, 0)),
                   pl.BlockSpec((1, d), lambda i: (0, 0))],
        out_shape=[jax.ShapeDtypeStruct((1, 1), F32), jax.ShapeDtypeStruct((m, d), F32),
                   jax.ShapeDtypeStruct((1, d), F32)],
        compiler_params=pltpu.CompilerParams(dimension_semantics=("arbitrary",)),
    )(x, g.reshape(1, d), target)
    return loss, dx, dg.reshape(d)


def _rope_apply(x, tabs, transpose, name):
    m, w = x.shape
    wt = tabs[0].shape[1]
    reps = w // wt
    half = ROPE // 2
    tm = _tile(m, (256, 128, 64, 32, 16, 8))

    def body(x_ref, c_ref, a_ref, b_ref, y_ref):
        xv = x_ref[...]

        def wide(t_ref):
            t = t_ref[...]
            return t if reps == 1 else jnp.concatenate([t] * reps, axis=1)

        c, a, b = wide(c_ref), wide(a_ref), wide(b_ref)
        if transpose:
            y = xv * c + pltpu.roll(xv * a, half, 1) + pltpu.roll(xv * b, w - half, 1)
        else:
            y = xv * c + pltpu.roll(xv, w - half, 1) * a + pltpu.roll(xv, half, 1) * b
        y_ref[...] = y

    x_spec = pl.BlockSpec((tm, w), lambda i: (i, 0))
    t_spec = pl.BlockSpec((tm, wt), lambda i: (i, 0))
    return pl.pallas_call(
        body, name=name, grid=(m // tm,),
        in_specs=[x_spec, t_spec, t_spec, t_spec], out_specs=x_spec,
        out_shape=jax.ShapeDtypeStruct((m, w), F32),
        compiler_params=pltpu.CompilerParams(dimension_semantics=("parallel",)),
    )(x, *tabs)


def _log_sigmoid_pair(z):
    a = jnp.minimum(z, 0.0) - jnp.log(1.0 + jnp.exp(-jnp.abs(z)))
    return a, a - z


def _split_bf16(x):
    hi = x.astype(BF16)
    return hi, (x - hi.astype(F32)).astype(BF16)


def _wide(v, width):
    return v if width == LANES else jnp.concatenate([v] * (width // LANES), axis=1)


def _lanes(col):
    return jnp.broadcast_to(col, (col.shape[0], LANES))


def _rope_head(x, c, a, b):
    half = ROPE // 2
    return x * c + pltpu.roll(x, LANES - half, 1) * a + pltpu.roll(x, half, 1) * b


def _rope_head_t(dy, c, a, b):
    half = ROPE // 2
    return dy * c + pltpu.roll(dy * a, half, 1) + pltpu.roll(dy * b, LANES - half, 1)


def _sb_weights(qh, k2, valid):
    a, b = _log_sigmoid_pair(_dot(qh, k2, NT))
    return a, b if valid is None else jnp.where(valid, b, 0.0)


def _sb_fwd(qkv, shards=()):
    s = qkv.shape[0]
    t = min(SB_BLOCK, s)
    nq = s // t
    npair = SB_HEADS // 2
    scale = SB_HD ** -0.5

    n_t = len(shards)

    def body(q_ref, k_ref, v_ref, *rest):
        x_refs, rest = rest[:n_t], rest[n_t:]
        o_ref, bta_ref, btb_ref, js_ref = rest[:4]
        g_refs, rest = rest[4:4 + n_t], rest[4 + n_t:]
        acc_ref, ra_ref, rb_ref = rest[:3]
        p, i = pl.program_id(0), pl.program_id(1)
        if n_t:
            start, forward, finish = _gather_phases(x_refs, g_refs, *rest[3:])
            pl.when(jnp.logical_and(p == 0, i == 0))(start)
            pl.when(jnp.logical_and(p == (5 * npair) // 8, i == 0))(forward)
        q2 = q_ref[...] * scale
        first = lax.broadcasted_iota(jnp.int32, (t, LANES), 1) < SB_HD
        heads = (jnp.where(first, q2, jnp.zeros_like(q2)), jnp.where(first, jnp.zeros_like(q2), q2))
        row = lax.broadcasted_iota(jnp.int32, (t, t), 0)
        col = lax.broadcasted_iota(jnp.int32, (t, t), 1)
        later = jnp.where(row > col, 1.0, 0.0).astype(BF16)
        acc_ref[...] = jnp.zeros_like(acc_ref)
        ra_ref[...] = jnp.zeros_like(ra_ref)
        rb_ref[...] = jnp.zeros_like(rb_ref)

        def block(j, diagonal):
            off = pl.multiple_of(j * t, t)
            k2 = k_ref[pl.ds(off, t), :]
            v2 = v_ref[pl.ds(off, t), :]
            valid = col < row if diagonal else None
            outs, rmax = [], None
            for qh, r_ref in zip(heads, (ra_ref, rb_ref)):
                a, b = _sb_weights(qh, k2, valid)
                bh, bl = _split_bf16(b)
                inner = _dot(bh, later, NN) + _dot(bl, later, NN)
                r = r_ref[...]
                w = jnp.exp(a + inner + _wide(r, t))
                if diagonal:
                    w = jnp.where(valid, w, 0.0)
                outs.append(_dot(w.astype(BF16), v2, NN))
                rn = r + jnp.sum(b, axis=1, keepdims=True)
                r_ref[...] = rn
                rmax = jnp.max(rn) if rmax is None else jnp.maximum(rmax, jnp.max(rn))
            acc_ref[...] += jnp.where(first, outs[0], outs[1])
            return rmax

        def cond(carry):
            j, rmax = carry
            return jnp.logical_and(j >= 0, rmax > EXP_ZERO)

        def step(carry):
            j, _ = carry
            return j - 1, block(j, False)

        def first_two():
            block(i, True)
            return block(i - 1, False)

        rmax = lax.cond(i >= 1, first_two, lambda: block(i, True))
        jend, _ = lax.while_loop(cond, step, (jnp.maximum(i - 2, -1), rmax))
        o_ref[...] = acc_ref[...].astype(o_ref.dtype)
        bta_ref[0] = ra_ref[...][:, :1]
        btb_ref[0] = rb_ref[...][:, :1]
        js_ref[p, i] = (jend + 1).astype(F32)
        if n_t:
            pl.when(jnp.logical_and(p == npair - 1, i == nq - 1))(finish)

    stat = pl.BlockSpec((1, t, 1), lambda p, i: (p, i, 0))
    any_spec = pl.BlockSpec(memory_space=pl.ANY)
    return pl.pallas_call(
        body, name="sb_attn_fwd_gather" if n_t else "sb_attn_fwd", grid=(npair, nq),
        in_specs=[pl.BlockSpec((t, LANES), lambda p, i: (i, p)),
                  pl.BlockSpec((s, LANES), lambda p, i: (0, npair + p)),
                  pl.BlockSpec((s, LANES), lambda p, i: (0, 2 * npair + p))] + [any_spec] * n_t,
        out_specs=[pl.BlockSpec((t, LANES), lambda p, i: (i, p)), stat, stat,
                   pl.BlockSpec(memory_space=pltpu.SMEM)] + [any_spec] * n_t,
        out_shape=[jax.ShapeDtypeStruct((s, SB_HEADS * SB_HD), BF16), jax.ShapeDtypeStruct((npair, s, 1), F32),
                   jax.ShapeDtypeStruct((npair, s, 1), F32), jax.ShapeDtypeStruct((npair, nq), F32)]
        + [jax.ShapeDtypeStruct((N_DEV,) + tuple(x.shape), x.dtype) for x in shards],
        scratch_shapes=[pltpu.VMEM((t, LANES), F32)] * 3 + (_gather_scratch(n_t) if n_t else []),
        compiler_params=pltpu.CompilerParams(
            dimension_semantics=("arbitrary", "arbitrary"), vmem_limit_bytes=VMEM_ATT),
    )(qkv, qkv, qkv, *shards)


def _sb_bwd(qkv, do, btot_a, btot_b, jstart):
    s = qkv.shape[0]
    t = min(SB_BLOCK, s)
    nq = s // t
    npair = SB_HEADS // 2
    scale = SB_HD ** -0.5

    def body(js_ref, q_ref, k_ref, v_ref, do_ref, bta_ref, btb_ref, dq_ref, dk_ref, dv_ref,
             dq_acc, pa_ref, pb_ref, ga_ref, gb_ref, dk_acc, dv_acc):
        p, i = pl.program_id(0), pl.program_id(1)

        @pl.when(i == 0)
        def _():
            dk_acc[...] = jnp.zeros_like(dk_acc)
            dv_acc[...] = jnp.zeros_like(dv_acc)

        q2 = q_ref[...] * scale
        do2 = do_ref[...]
        first = lax.broadcasted_iota(jnp.int32, (t, LANES), 1) < SB_HD
        zero = jnp.zeros_like(q2)
        q_heads = (jnp.where(first, q2, zero), jnp.where(first, zero, q2))
        do_heads = (jnp.where(first, do2, zero), jnp.where(first, zero, do2))
        bts = (_wide(_lanes(bta_ref[0]), t), _wide(_lanes(btb_ref[0]), t))
        row = lax.broadcasted_iota(jnp.int32, (t, t), 0)
        col = lax.broadcasted_iota(jnp.int32, (t, t), 1)
        upto = jnp.where(row <= col, 1.0, 0.0).astype(BF16)
        before = jnp.where(row < col, 1.0, 0.0).astype(BF16)
        dq_acc[...] = jnp.zeros_like(dq_acc)
        for r in (pa_ref, pb_ref, ga_ref, gb_ref):
            r[...] = jnp.zeros_like(r)
        j0 = jnp.clip(js_ref[p, i].astype(jnp.int32), 0, i)

        def block(j, diagonal):
            off = pl.multiple_of(j * t, t)
            k2 = k_ref[pl.ds(off, t), :]
            v2 = v_ref[pl.ds(off, t), :]
            valid = col < row if diagonal else None
            dqs, dk2, dv2 = [], None, None
            for qh, doh, bt, p_ref, g_ref in zip(q_heads, do_heads, bts, (pa_ref, pb_ref), (ga_ref, gb_ref)):
                a, b = _sb_weights(qh, k2, valid)
                bh, bl = _split_bf16(b)
                pin = _dot(bh, upto, NN) + _dot(bl, upto, NN)
                surv = bt - (_wide(p_ref[...], t) + pin)
                w = jnp.exp(a + surv)
                if diagonal:
                    w = jnp.where(valid, w, 0.0)
                g = w * _dot(doh, v2, NT)
                gh, gl = _split_bf16(g)
                gsum = _wide(g_ref[...], t) + _dot(gh, before, NN) + _dot(gl, before, NN)
                beta = jnp.exp(a)
                dz = g * (1.0 - beta) - gsum * beta
                if diagonal:
                    dz = jnp.where(valid, dz, 0.0)
                dz = dz.astype(BF16)
                dqs.append(_dot(dz, k2, NN))
                dkh = _dot(dz, qh, TN)
                dvh = _dot(w.astype(BF16), doh, TN)
                dk2 = dkh if dk2 is None else dk2 + dkh
                dv2 = dvh if dv2 is None else dv2 + dvh
                p_ref[...] += jnp.sum(b, axis=1, keepdims=True)
                g_ref[...] += jnp.sum(g, axis=1, keepdims=True)
            dq_acc[...] += jnp.where(first, dqs[0], dqs[1])
            dk_acc[pl.ds(off, t), :] += dk2
            dv_acc[pl.ds(off, t), :] += dv2

        def step(j, carry):
            block(j, False)
            return carry

        lax.fori_loop(j0, jnp.maximum(i - 1, j0), step, 0)

        @pl.when(j0 < i)
        def _():
            block(i - 1, False)
            block(i, True)

        @pl.when(j0 >= i)
        def _():
            block(i, True)

        dq_ref[...] = (dq_acc[...] * scale).astype(dq_ref.dtype)

        @pl.when(i == nq - 1)
        def _():
            dk_ref[...] = dk_acc[...].astype(dk_ref.dtype)
            dv_ref[...] = dv_acc[...].astype(dv_ref.dtype)

    blk = pl.BlockSpec((t, LANES), lambda p, i: (i, p))
    full = pl.BlockSpec((s, LANES), lambda p, i: (0, p))
    stat = pl.BlockSpec((1, t, 1), lambda p, i: (p, i, 0))
    vec = pltpu.VMEM((t, LANES), F32)
    return pl.pallas_call(
        body, name="sb_attn_bwd", grid=(npair, nq),
        in_specs=[pl.BlockSpec(memory_space=pltpu.SMEM), blk,
                  pl.BlockSpec((s, LANES), lambda p, i: (0, npair + p)),
                  pl.BlockSpec((s, LANES), lambda p, i: (0, 2 * npair + p)), blk, stat, stat],
        out_specs=[blk, full, full],
        out_shape=[jax.ShapeDtypeStruct((s, SB_HEADS * SB_HD), BF16)] * 3,
        scratch_shapes=[pltpu.VMEM((t, LANES), F32), vec, vec, vec, vec,
                        pltpu.VMEM((s, LANES), F32), pltpu.VMEM((s, LANES), F32)],
        compiler_params=pltpu.CompilerParams(
            dimension_semantics=("arbitrary", "arbitrary"), vmem_limit_bytes=VMEM_ATT),
    )(jstart, qkv, qkv, qkv, do, btot_a, btot_b)


@jax.custom_vjp
def sb_attention(qkv, *shards):
    o, _, _, _, *gathered = _sb_fwd(qkv, shards)
    return (o, *gathered)


def _sb_attention_fwd(qkv, *shards):
    o, btot_a, btot_b, jstart, *gathered = _sb_fwd(qkv, shards)
    return (o, *gathered), (qkv, btot_a, btot_b, jstart, shards)


def _sb_attention_bwd(res, cts):
    qkv, btot_a, btot_b, jstart, shards = res
    dq, dk, dv = _sb_bwd(qkv, cts[0], btot_a, btot_b, jstart)
    return (jnp.concatenate([dq, dk, dv], axis=1),) + tuple(jnp.zeros_like(x) for x in shards)


sb_attention.defvjp(_sb_attention_fwd, _sb_attention_bwd)


def _chunk_allowed(row, col):
    return (col >> CHUNK_SHIFT) <= (row >> CHUNK_SHIFT)


def _mla_fwd(q, k, vx, tabs):
    s = q.shape[0]
    h_ = q.shape[1] // LANES
    t = min(MLA_BLOCK, s)
    nq = s // t
    scale = QK ** -0.5
    hp = MLA_GROUP

    def body(q_ref, k_ref, v_ref, c_ref, a_ref, b_ref, o_ref, lse_ref, acc_ref, m_ref):
        i = pl.program_id(1)
        acc_ref[...] = jnp.zeros_like(acc_ref)
        m_ref[...] = jnp.full_like(m_ref, NEG_BIG)
        rot = (c_ref[...], a_ref[...], b_ref[...])
        qs = [_rope_head(q_ref[:, hh * LANES:(hh + 1) * LANES], *rot).astype(BF16) for hh in range(hp)]

        def tile(j, diagonal):
            off = pl.multiple_of(j * t, t)
            for hh in range(hp):
                lanes = slice(hh * LANES, (hh + 1) * LANES)
                kj = k_ref[pl.ds(off, t), lanes]
                vj = v_ref[pl.ds(off, t), lanes]
                sc = _dot(qs[hh], kj, NT) * (scale * LOG2_E)
                if diagonal:
                    row = lax.broadcasted_iota(jnp.int32, (t, t), 0)
                    col = lax.broadcasted_iota(jnp.int32, (t, t), 1)
                    sc = jnp.where(_chunk_allowed(row, col), sc, NEG_BIG)
                m_old = m_ref[hh]
                m_new = jnp.maximum(m_old, jnp.max(sc, axis=1, keepdims=True))
                p = jnp.exp2(sc - _wide(m_new, t))
                acc_ref[hh] = jnp.exp2(m_old - m_new) * acc_ref[hh] + _dot(p.astype(BF16), vj, NN)
                m_ref[hh] = m_new

        def step(jj, carry):
            tile(2 * jj, False)
            tile(2 * jj + 1, False)
            return carry

        lax.fori_loop(0, i // 2, step, 0)

        @pl.when(i % 2 == 1)
        def _():
            tile(i - 1, False)

        tile(i, True)
        first = lax.broadcasted_iota(jnp.int32, (t, LANES), 1) < VDIM
        outs = []
        for hh in range(hp):
            acc = acc_ref[hh]
            den = acc[:, VDIM:VDIM + 1]
            outs.append(acc / den)
            lse_ref[hh] = (m_ref[hh][:, :1] + jnp.log2(den)) * LN_2
        for pp in range(hp // 2):
            o_ref[:, pp * LANES:(pp + 1) * LANES] = jnp.where(
                first, outs[2 * pp], pltpu.roll(outs[2 * pp + 1], VDIM, 1))

    wide = hp * LANES
    return pl.pallas_call(
        body, name="mla_attn_fwd", grid=(h_ // hp, nq),
        in_specs=[pl.BlockSpec((t, wide), lambda g, i: (i, g)),
                  pl.BlockSpec((s, wide), lambda g, i: (0, g)),
                  pl.BlockSpec((s, wide), lambda g, i: (0, g))] + [pl.BlockSpec((t, LANES), lambda g, i: (i, 0))] * 3,
        out_specs=[pl.BlockSpec((t, hp * VDIM), lambda g, i: (i, g)),
                   pl.BlockSpec((hp, t, 1), lambda g, i: (g, i, 0))],
        out_shape=[jax.ShapeDtypeStruct((s, h_ * VDIM), F32), jax.ShapeDtypeStruct((h_, s, 1), F32)],
        scratch_shapes=[pltpu.VMEM((hp, t, LANES), F32), pltpu.VMEM((hp, t, LANES), F32)],
        compiler_params=pltpu.CompilerParams(
            dimension_semantics=("arbitrary", "arbitrary"), vmem_limit_bytes=VMEM_ATT),
    )(q, k, vx, *tabs)


def _mla_bwd(q, k, vx, tabs, o, lse, do):
    s = q.shape[0]
    h_ = q.shape[1] // LANES
    t = min(MLA_BLOCK, s)
    nq = s // t
    scale = QK ** -0.5

    def body(q_ref, k_ref, v_ref, c_ref, a_ref, b_ref, o_ref, lse_ref, do_ref, dq_ref, dk_ref, dv_ref, dq_acc):
        h, i = pl.program_id(0), pl.program_id(1)
        rot = (c_ref[...], a_ref[...], b_ref[...])

        @pl.when(i == 0)
        def _():
            dk_ref[...] = jnp.zeros_like(dk_ref)
            dv_ref[...] = jnp.zeros_like(dv_ref)

        qv = _rope_head(q_ref[...], *rot).astype(BF16)
        lane = lax.broadcasted_iota(jnp.int32, (t, LANES), 1)
        start = (h % 2) * VDIM
        mine = jnp.logical_and(lane >= start, lane < start + VDIM)
        do2 = do_ref[...]
        delta = _lanes(jnp.sum(jnp.where(mine, do2 * o_ref[...], 0.0), axis=1, keepdims=True))
        odd = (start + jnp.zeros_like(lane)) > 0
        do_head = jnp.where(odd, pltpu.roll(do2, VDIM, 1), do2)
        dov = jnp.where(lane < VDIM, do_head, 0.0).astype(BF16)
        lse = _lanes(lse_ref[0])
        dq_acc[...] = jnp.zeros_like(dq_acc)

        def tile(j, diagonal):
            off = pl.multiple_of(j * t, t)
            kj = k_ref[pl.ds(off, t), :]
            vj = v_ref[pl.ds(off, t), :]
            p = jnp.exp(_dot(qv, kj, NT) * scale - _wide(lse, t))
            if diagonal:
                row = lax.broadcasted_iota(jnp.int32, (t, t), 0)
                col = lax.broadcasted_iota(jnp.int32, (t, t), 1)
                p = jnp.where(_chunk_allowed(row, col), p, 0.0)
            dp = _dot(dov, vj, NT)
            ds = (p * (dp - _wide(delta, t)) * scale).astype(BF16)
            dq_acc[...] += _dot(ds, kj, NN)
            dk_ref[pl.ds(off, t), :] += _dot(ds, qv, TN)
            dv_ref[pl.ds(off, t), :] += _dot(p.astype(BF16), dov, TN)

        def step(jj, carry):
            for u in range(MLA_UNROLL):
                tile(MLA_UNROLL * jj + u, False)
            return carry

        def single(j, carry):
            tile(j, False)
            return carry

        lax.fori_loop(0, i // MLA_UNROLL, step, 0)
        lax.fori_loop(i - i % MLA_UNROLL, i, single, 0)
        tile(i, True)
        dq_ref[...] = _rope_head_t(dq_acc[...], *rot)

    blk = pl.BlockSpec((t, LANES), lambda h, i: (i, h))
    full = pl.BlockSpec((s, LANES), lambda h, i: (0, h))
    pair = pl.BlockSpec((t, LANES), lambda h, i: (i, h // 2))
    stat = pl.BlockSpec((1, t, 1), lambda h, i: (h, i, 0))
    return pl.pallas_call(
        body, name="mla_attn_bwd", grid=(h_, nq),
        in_specs=[blk, full, full] + [pl.BlockSpec((t, LANES), lambda h, i: (i, 0))] * 3 + [pair, stat, pair],
        out_specs=[blk, full, full],
        out_shape=[jax.ShapeDtypeStruct((s, h_ * LANES), F32)] * 3,
        scratch_shapes=[pltpu.VMEM((t, LANES), F32)],
        compiler_params=pltpu.CompilerParams(
            dimension_semantics=("arbitrary", "arbitrary"), vmem_limit_bytes=VMEM_ATT),
    )(q, k, vx, *tabs, o, lse, do)


def _assemble_kv(kvp, kr):
    half = kvp.shape[1] // 2
    ones = jnp.tile((jnp.arange(LANES) == VDIM).astype(F32), MLA_HEADS)
    k = kvp[:, :half] + jnp.tile(kr, (1, MLA_HEADS))
    return k.astype(BF16), (kvp[:, half:] + ones).astype(BF16)


@jax.custom_vjp
def mla_attention(q, kvp, kr, c, a, b):
    return _mla_fwd(q, *_assemble_kv(kvp, kr), (c, a, b))[0]


def _mla_attention_fwd(q, kvp, kr, c, a, b):
    kb, vb = _assemble_kv(kvp, kr)
    o, lse = _mla_fwd(q, kb, vb, (c, a, b))
    return o, (q, kb, vb, (c, a, b), o, lse)


def _mla_attention_bwd(res, do):
    q, kb, vb, tabs, o, lse = res
    dq, dk, dvx = _mla_bwd(q, kb, vb, tabs, o, lse, do)
    dkr = dk.reshape(dk.shape[0], MLA_HEADS, LANES).sum(axis=1)
    return (dq, jnp.concatenate([dk, dvx], axis=1), dkr) + tuple(jnp.zeros_like(t) for t in tabs)


mla_attention.defvjp(_mla_attention_fwd, _mla_attention_bwd)


def _add_tile(acc, res):
    return (acc + res,)


def _make_linear_res(tag, a_dtype=F32):
    def forward(x, a, slot, wb):
        ab = a.astype(BF16)
        y = _matmul(ab, wb, epilogue=_add_tile, extras=(x,), name=tag + "_fwd")[0]
        return y, (ab, wb)

    @jax.custom_vjp
    def f(x, a, slot, wb):
        return forward(x, a, slot, wb)[0]

    def bwd(res, dy):
        ab, wb = res
        dyb = dy.astype(BF16)
        da = _matmul(dyb, wb, tb=True, out_dtypes=(a_dtype,), name=tag + "_da")[0]
        dw = _matmul(ab, dyb, ta=True, name=tag + "_dw")[0]
        return dy, da, dw, jnp.zeros_like(wb)

    f.defvjp(forward, bwd)
    return f


def _make_norm_linear(tag, out_dtype=F32):
    def forward(x, g, slot, wb):
        hb = _rms_fwd(x, g, tag + "_norm")
        y = _matmul(hb, wb, out_dtypes=(out_dtype,), name=tag + "_fwd")[0]
        return y, (x, g, wb, hb)

    @jax.custom_vjp
    def f(x, g, slot, wb):
        return forward(x, g, slot, wb)[0]

    def bwd(res, dy):
        x, g, wb, hb = res
        dyb = dy.astype(BF16)
        dh = _matmul(dyb, wb, tb=True, name=tag + "_dh")[0]
        dw = _matmul(hb, dyb, ta=True, name=tag + "_dw")[0]
        dx, dg = _rms_bwd(x, g, dh, tag + "_norm_bwd")
        return dx, dg, dw, jnp.zeros_like(wb)

    f.defvjp(forward, bwd)
    return f


def _relu2_fwd(acc):
    r = jnp.maximum(acc, 0.0)
    return acc, r * r


def _relu2_bwd(acc, u):
    return (acc * (2.0 * jnp.maximum(u.astype(F32), 0.0)),)


def _make_mlp_res(tag):
    def forward(x, g, slot1, slot2, w1b, w2b):
        hb = _rms_fwd(x, g, tag + "_norm")
        u, act = _matmul(hb, w1b, out_dtypes=(BF16, BF16), epilogue=_relu2_fwd, name=tag + "_up")
        y = _matmul(act, w2b, epilogue=_add_tile, extras=(x,), name=tag + "_down")[0]
        return y, (x, g, w1b, w2b, hb, u, act)

    @jax.custom_vjp
    def f(x, g, slot1, slot2, w1b, w2b):
        return forward(x, g, slot1, slot2, w1b, w2b)[0]

    def bwd(res, dy):
        x, g, w1b, w2b, hb, u, act = res
        dyb = dy.astype(BF16)
        du = _matmul(dyb, w2b, tb=True, out_dtypes=(BF16,), epilogue=_relu2_bwd, extras=(u,), name=tag + "_du")[0]
        dw2 = _matmul(act, dyb, ta=True, name=tag + "_dw2")[0]
        dw1 = _matmul(hb, du, ta=True, name=tag + "_dw1")[0]
        dh = _matmul(du, w1b, tb=True, name=tag + "_dh")[0]
        dx, dg = _rms_bwd(x, g, dh, tag + "_norm_bwd", res=dy)
        return dx, dg, dw1, dw2, jnp.zeros_like(w1b), jnp.zeros_like(w2b)

    f.defvjp(forward, bwd)
    return f


def _make_rope(tag):
    @jax.custom_vjp
    def f(x, c, a, b):
        return _rope_apply(x, (c, a, b), False, tag + "_fwd")

    def fwd(x, c, a, b):
        return _rope_apply(x, (c, a, b), False, tag + "_fwd"), (c, a, b)

    def bwd(res, dy):
        c, a, b = res
        return _rope_apply(dy, (c, a, b), True, tag + "_bwd"), jnp.zeros_like(c), jnp.zeros_like(a), jnp.zeros_like(b)

    f.defvjp(fwd, bwd)
    return f


def _rope_tables(positions):
    half = ROPE // 2
    inv_freq = ROPE_THETA ** (-jnp.arange(0, ROPE, 2, dtype=F32) / ROPE)
    ang = positions.astype(F32)[:, None] * inv_freq
    cos, sin = jnp.cos(ang), jnp.sin(ang)
    s = positions.shape[0]
    one = lambda n: jnp.ones((s, n), F32)
    zero = lambda n: jnp.zeros((s, n), F32)

    def tables(before, after):
        return (jnp.concatenate([one(before), cos, cos, one(after)], axis=1),
                jnp.concatenate([zero(before), -sin, zero(half + after)], axis=1),
                jnp.concatenate([zero(before + half), sin, zero(after)], axis=1))

    return tables(NOPE, LANES - QK), tables(KV_RANK, DKV_PAD - KV_RANK - ROPE)


def _pad_heads(w, per):
    lead = w.shape[:-1]
    w = jnp.pad(w.reshape(lead + (MLA_HEADS, per)), [(0, 0)] * len(lead) + [(0, 0), (0, LANES - per)])
    return w.reshape(lead + (MLA_HEADS * LANES,))


def _split_kv_heads(w):
    w3 = w.reshape(w.shape[0], MLA_HEADS, NOPE + VDIM)
    return jnp.concatenate([_pad_heads(w3[..., :NOPE].reshape(w.shape[0], -1), NOPE),
                            _pad_heads(w3[..., NOPE:].reshape(w.shape[0], -1), VDIM)], axis=1)


def _layers(full):
    return [full[l] for l in range(full.shape[0])] if full.ndim == 3 else [full]


def _trunk(x, slots, norms, w_qkv, shards, q_tabs, kv_tabs):
    kvp = kr = None
    wb = {'sb_w_qkv': w_qkv}
    for layer in range(DEPTH):
        if layer < N_A:
            qkv = _make_norm_linear(f"sb{layer}_qkv", BF16)(
                x, norms['attn_norm'][layer], slots['sb_w_qkv'][layer], wb['sb_w_qkv'][layer])
            if layer == 0:
                o, *gathered = sb_attention(qkv, *shards)
                for (n, ax), g in zip(SHARDED[1:], gathered):
                    wb[n] = _layers(_merge_blocks(g, ax))
            else:
                o = sb_attention(qkv)[0]
            x = _make_linear_res(f"sb{layer}_o", BF16)(x, o, slots['sb_w_o'][layer], wb['sb_w_o'][layer])
        else:
            j = layer - N_A
            if j == 0:
                pad = ((0, 0), (0, DKV_PAD - KV_RANK - ROPE))
                down = _make_norm_linear("kv_down")(
                    x, norms['kv_norm'], jnp.pad(slots['mla_w_dkv'][0], pad), jnp.pad(wb['mla_w_dkv'][0], pad))
                kvp = _make_norm_linear("kv_up")(
                    down[:, :KV_RANK], norms['mla_kv_lat_norm'],
                    _split_kv_heads(slots['mla_w_ukv'][0]), _split_kv_heads(wb['mla_w_ukv'][0]))
                k_rope = _make_rope("rope_k")(down, *kv_tabs)[:, KV_RANK:KV_RANK + ROPE]
                kr = jnp.pad(k_rope, ((0, 0), (NOPE, LANES - QK)))
            c_q = _make_norm_linear(f"mla{j}_dq")(
                x, norms['attn_norm'][layer], slots['mla_w_dq'][j], wb['mla_w_dq'][j])
            q = _make_norm_linear(f"mla{j}_uq")(
                c_q, norms['mla_q_lat_norm'][j], _pad_heads(slots['mla_w_uq'][j], QK), _pad_heads(wb['mla_w_uq'][j], QK))
            o = mla_attention(q, kvp, kr, *q_tabs)
            x = _make_linear_res(f"mla{j}_o")(x, o, slots['mla_w_o'][j], wb['mla_w_o'][j])
        x = _make_mlp_res(f"mlp{layer}")(
            x, norms['mlp_norm'][layer], slots['mlp_w1'][layer], slots['mlp_w2'][layer],
            wb['mlp_w1'][layer], wb['mlp_w2'][layer])
    return x


def _merge_blocks(gathered, ax):
    shp = gathered.shape[1:]
    return jnp.moveaxis(gathered, 0, ax).reshape(shp[:ax] + (N_DEV * shp[ax],) + shp[ax + 1:])


def _split_blocks(full, ax):
    shp = full.shape
    return jnp.moveaxis(full.reshape(shp[:ax] + (N_DEV, shp[ax] // N_DEV) + shp[ax + 1:]), ax, 0)


def _pack(parts):
    flat = jnp.concatenate([p.reshape(-1) for p in parts])
    rows = -(-flat.shape[0] // PACK_COLS)
    rows = -(-rows // PACK_ROW_ALIGN) * PACK_ROW_ALIGN
    return jnp.pad(flat, (0, rows * PACK_COLS - flat.shape[0])).reshape(rows, PACK_COLS)


def _unpack(packed, shapes):
    flat = packed.reshape(-1)
    out, off = [], 0
    for shp in shapes:
        n = math.prod(shp)
        out.append(flat[off:off + n].reshape(shp))
        off += n
    return out


def _add_round(x, stage, core, name):
    _, r, c = x.shape
    tr = _tile(r, (256, 128, 64, 32, 16, 8))

    def body(core_ref, x_ref, s_ref, o_ref):
        o_ref[...] = (x_ref[0] + s_ref[...]).astype(o_ref.dtype)

    return pl.pallas_call(
        body, name=name,
        grid_spec=pltpu.PrefetchScalarGridSpec(
            num_scalar_prefetch=1, grid=(r // tr,),
            in_specs=[pl.BlockSpec((1, tr, c), lambda i, core_ref: (core_ref[0], i, 0)),
                      pl.BlockSpec((tr, c), lambda i, core_ref: (i, 0))],
            out_specs=pl.BlockSpec((tr, c), lambda i, core_ref: (i, 0))),
        out_shape=jax.ShapeDtypeStruct((r, c), BF16),
        compiler_params=pltpu.CompilerParams(dimension_semantics=("parallel",)),
    )(core, x, stage)


def _adamw_reduce(parts, w, m, v, name):
    n_parts, r, c = parts.shape
    tr = _tile(r, (128, 64, 32, 16, 8))
    bias1 = 1.0 - ADAM_B1 ** ADAM_STEP
    bias2 = 1.0 - ADAM_B2 ** ADAM_STEP

    def body(p_ref, w_ref, m_ref, v_ref, g_ref, d_ref, nm_ref, nv_ref):
        g = p_ref[0].astype(F32)
        for s in range(1, n_parts):
            g = g + p_ref[s].astype(F32)
        mn = ADAM_B1 * m_ref[...] + (1.0 - ADAM_B1) * g
        vn = ADAM_B2 * v_ref[...] + (1.0 - ADAM_B2) * (g * g)
        m_hat = mn / bias1
        v_hat = vn / bias2
        g_ref[...] = g
        d_ref[...] = -ADAM_LR * (m_hat / (jnp.sqrt(v_hat) + ADAM_EPS) + ADAM_WD * w_ref[...])
        nm_ref[...] = mn
        nv_ref[...] = vn

    blk = pl.BlockSpec((tr, c), lambda i: (i, 0))
    return pl.pallas_call(
        body, name=name, grid=(r // tr,),
        in_specs=[pl.BlockSpec((n_parts, tr, c), lambda i: (0, i, 0)), blk, blk, blk],
        out_specs=[blk] * 4,
        out_shape=[jax.ShapeDtypeStruct((r, c), F32)] * 4,
        compiler_params=pltpu.CompilerParams(dimension_semantics=("parallel",), vmem_limit_bytes=VMEM_MM),
    )(parts, w, m, v)


def kernel(x, positions, attn_norm, mlp_norm, sb_w_qkv, sb_w_o, kv_norm, mla_w_dkv, mla_kv_lat_norm, mla_w_ukv, mla_w_dq, mla_q_lat_norm, mla_w_uq, mla_w_o, mlp_w1, mlp_w2, final_norm, loss_target, m_attn_norm, m_mlp_norm, m_sb_w_qkv, m_sb_w_o, m_kv_norm, m_mla_w_dkv, m_mla_kv_lat_norm, m_mla_w_ukv, m_mla_w_dq, m_mla_q_lat_norm, m_mla_w_uq, m_mla_w_o, m_mlp_w1, m_mlp_w2, m_final_norm, v_attn_norm, v_mlp_norm, v_sb_w_qkv, v_sb_w_o, v_kv_norm, v_mla_w_dkv, v_mla_kv_lat_norm, v_mla_w_ukv, v_mla_w_dq, v_mla_q_lat_norm, v_mla_w_uq, v_mla_w_o, v_mlp_w1, v_mlp_w2, v_final_norm):
    weights = dict(attn_norm=attn_norm, mlp_norm=mlp_norm, sb_w_qkv=sb_w_qkv, sb_w_o=sb_w_o, kv_norm=kv_norm,
                   mla_w_dkv=mla_w_dkv, mla_kv_lat_norm=mla_kv_lat_norm, mla_w_ukv=mla_w_ukv, mla_w_dq=mla_w_dq,
                   mla_q_lat_norm=mla_q_lat_norm, mla_w_uq=mla_w_uq, mla_w_o=mla_w_o, mlp_w1=mlp_w1, mlp_w2=mlp_w2,
                   final_norm=final_norm)
    mom_m = dict(attn_norm=m_attn_norm, mlp_norm=m_mlp_norm, sb_w_qkv=m_sb_w_qkv, sb_w_o=m_sb_w_o, kv_norm=m_kv_norm,
                 mla_w_dkv=m_mla_w_dkv, mla_kv_lat_norm=m_mla_kv_lat_norm, mla_w_ukv=m_mla_w_ukv, mla_w_dq=m_mla_w_dq,
                 mla_q_lat_norm=m_mla_q_lat_norm, mla_w_uq=m_mla_w_uq, mla_w_o=m_mla_w_o, mlp_w1=m_mlp_w1,
                 mlp_w2=m_mlp_w2, final_norm=m_final_norm)
    mom_v = dict(attn_norm=v_attn_norm, mlp_norm=v_mlp_norm, sb_w_qkv=v_sb_w_qkv, sb_w_o=v_sb_w_o, kv_norm=v_kv_norm,
                 mla_w_dkv=v_mla_w_dkv, mla_kv_lat_norm=v_mla_kv_lat_norm, mla_w_ukv=v_mla_w_ukv, mla_w_dq=v_mla_w_dq,
                 mla_q_lat_norm=v_mla_q_lat_norm, mla_w_uq=v_mla_w_uq, mla_w_o=v_mla_w_o, mlp_w1=v_mlp_w1,
                 mlp_w2=v_mlp_w2, final_norm=v_final_norm)
    sharded_names = [n for n, _ in SHARDED]
    repl_shapes = [tuple(weights[n].shape) for n in REPLICATED]

    first_name, first_ax = SHARDED[0]
    w_qkv = _layers(_merge_blocks(_all_gather([weights[first_name].astype(BF16)], "gather_qkv")[0], first_ax))
    shards = [weights[n].astype(BF16) for n in sharded_names[1:]]
    slots = {}
    for n, ax in SHARDED:
        shp = weights[n].shape
        full = shp[:ax] + (N_DEV * shp[ax],) + shp[ax + 1:]
        slots[n] = [jnp.zeros(full[1:] if len(full) == 3 else full, F32) for _ in range(full[0] if len(full) == 3 else 1)]
    norms = {n: weights[n] for n in REPLICATED if n != 'final_norm'}

    q_tabs, kv_tabs = _rope_tables(positions[0])
    x_last, pullback = jax.vjp(
        lambda xx, ss, nn: _trunk(xx, ss, nn, w_qkv, shards, q_tabs, kv_tabs), x[0], slots, norms)
    loss_part, dx_last, d_final = _loss_head(x_last, final_norm, loss_target[0])
    dx, d_slots, d_norms = pullback(dx_last)
    d_norms = dict(d_norms)
    d_norms['final_norm'] = d_final
    loss = lax.psum(loss_part[0, 0], ("x", "y", "c"))

    core = lax.axis_index("c").astype(jnp.int32).reshape(1)
    halves, dims = [], []
    for n, ax in SHARDED:
        full = jnp.stack(d_slots[n]) if weights[n].ndim == 3 else d_slots[n][0]
        blocks = _split_blocks(full, ax)
        two_d = (math.prod(blocks.shape[1:-1]), blocks.shape[-1])
        dims.append(two_d)
        halves.append(jnp.moveaxis(blocks.reshape((N_DEV // 2, 2) + two_d), 1, 0))
    staged = _pair_exchange(halves, "pair_grads")
    sums = []
    for n, h, st, (r, c) in zip(sharded_names, halves, staged, dims):
        rows = (N_DEV // 2) * r
        sums.append(_add_round(h.reshape(2, rows, c), st.reshape(rows, c), core, "pair_sum_" + n).reshape(st.shape))
    received = _chip_exchange(sums, "scatter_grads")
    repl_parts = _all_gather([_pack([d_norms[n] for n in REPLICATED])], "gather_norm_grads")[0]

    results = {kind: {} for kind in ("grad", "delta", "new_m", "new_v")}
    for n, parts, two_d in zip(sharded_names, received, dims):
        shp = weights[n].shape
        res = _adamw_reduce(parts, weights[n].reshape(two_d), mom_m[n].reshape(two_d),
                            mom_v[n].reshape(two_d), "adamw_" + n)
        for kind, a in zip(results, res):
            results[kind][n] = a.reshape(shp)
    res = _adamw_reduce(repl_parts, _pack([weights[n] for n in REPLICATED]), _pack([mom_m[n] for n in REPLICATED]),
                        _pack([mom_v[n] for n in REPLICATED]), "adamw_replicated")
    for kind, a in zip(results, res):
        results[kind].update(zip(REPLICATED, _unpack(a, repl_shapes)))

    out = [loss, dx[None]]
    for kind in ("grad", "delta", "new_m", "new_v"):
        out += [results[kind][n] for n in WEIGHT_ORDER]
    return tuple(out)
```

```python
import math

import jax
import jax.numpy as jnp
from jax import lax
from jax.experimental import pallas as pl
from jax.experimental.pallas import tpu as pltpu

F32 = jnp.float32
BF16 = jnp.bfloat16
MESH = pl.DeviceIdType.MESH

N_DEV = 8
DEPTH = 4
N_A = 2
SB_HEADS = 16
SB_HD = 64
MLA_HEADS = 16
NOPE = 64
ROPE = 32
VDIM = 64
QK = NOPE + ROPE
KV_RANK = 256
DKV_PAD = 384
LANES = 128
CHUNK_SHIFT = 6
ROPE_THETA = 10000.0
EPS = 1e-6
SB_BLOCK = 256
MLA_BLOCK = 512
MLA_GROUP = 4
MLA_UNROLL = 4
PACK_COLS = 1024
PACK_ROW_ALIGN = 16
EXP_ZERO = -104.0
NEG_BIG = -1e30
LOG2_E = 1.4426950408889634
LN_2 = 0.6931471805599453
VMEM_ATT = 56 * 1024 * 1024
VMEM_MM = 48 * 1024 * 1024

ADAM_LR = 0.001
ADAM_B1 = 0.9
ADAM_B2 = 0.999
ADAM_EPS = 1e-08
ADAM_WD = 0.01
ADAM_STEP = 10

WEIGHT_ORDER = ['attn_norm', 'mlp_norm', 'sb_w_qkv', 'sb_w_o', 'kv_norm', 'mla_w_dkv', 'mla_kv_lat_norm',
                'mla_w_ukv', 'mla_w_dq', 'mla_q_lat_norm', 'mla_w_uq', 'mla_w_o', 'mlp_w1', 'mlp_w2', 'final_norm']
SHARDED = [('sb_w_qkv', 2), ('sb_w_o', 1), ('mla_w_dkv', 0), ('mla_w_ukv', 1), ('mla_w_dq', 1),
           ('mla_w_uq', 2), ('mla_w_o', 1), ('mlp_w1', 2), ('mlp_w2', 1)]
REPLICATED = ['attn_norm', 'mlp_norm', 'kv_norm', 'mla_kv_lat_norm', 'mla_q_lat_norm', 'final_norm']


def _tile(dim, prefs=(512, 384, 256, 128, 64, 32, 16, 8)):
    for t in prefs:
        if dim % t == 0:
            return t
    return dim


def _dot(a, b, dims):
    return lax.dot_general(a, b, (dims, ((), ())), preferred_element_type=F32)


NN = ((1,), (0,))
NT = ((1,), (1,))
TN = ((0,), (0,))


def _gather_phases(x_refs, out_refs, send_sems, recv_sems, local_sems):
    n_t = len(x_refs)
    x, y, c = lax.axis_index("x"), lax.axis_index("y"), lax.axis_index("c")
    me, sibling = (x, y, c), (x, y, 1 - c)
    chips = [(1 - x, y), (x, 1 - y), (1 - x, 1 - y)]

    def slot(t, px, py, pc):
        return out_refs[t].at[4 * px + 2 * py + pc]

    def copy(t, k, block, to, src=None):
        return pltpu.make_async_remote_copy(
            src_ref=slot(t, *block) if src is None else src, dst_ref=slot(t, *block),
            send_sem=send_sems.at[7 * t + k], recv_sem=recv_sems.at[7 * t + k],
            device_id=to, device_id_type=MESH)

    def mine():
        return [pltpu.make_async_copy(x_refs[t], slot(t, *me), local_sems.at[t]) for t in range(n_t)]

    def first():
        out = []
        for t in range(n_t):
            out.append(copy(t, 0, me, sibling, src=x_refs[t]))
            out += [copy(t, 1 + j, me, (*chip, c), src=x_refs[t]) for j, chip in enumerate(chips)]
        return out

    def passed():
        return [copy(t, 4 + j, (*chip, c), sibling) for t in range(n_t) for j, chip in enumerate(chips)]

    def start():
        for cp in mine() + first():
            cp.start()

    def forward():
        onward = passed()
        for t in range(n_t):
            for j, chip in enumerate(chips):
                copy(t, 1 + j, (*chip, c), me).wait_recv()
                onward[3 * t + j].start()

    def finish():
        for t in range(n_t):
            copy(t, 0, sibling, me).wait_recv()
            for j, chip in enumerate(chips):
                copy(t, 4 + j, (*chip, 1 - c), me).wait_recv()
        for cp in first() + passed():
            cp.wait_send()
        for cp in mine():
            cp.wait()

    return start, forward, finish


def _gather_scratch(n_t):
    return [pltpu.SemaphoreType.DMA((7 * n_t,)), pltpu.SemaphoreType.DMA((7 * n_t,)), pltpu.SemaphoreType.DMA((n_t,))]


def _all_gather(shards, name):
    n_t = len(shards)

    def body(*refs):
        for phase in _gather_phases(refs[:n_t], refs[n_t:2 * n_t], *refs[2 * n_t:]):
            phase()

    any_spec = pl.BlockSpec(memory_space=pl.ANY)
    return pl.pallas_call(
        body, name=name,
        out_shape=[jax.ShapeDtypeStruct((N_DEV,) + tuple(s.shape), s.dtype) for s in shards],
        in_specs=[any_spec] * n_t, out_specs=[any_spec] * n_t,
        scratch_shapes=_gather_scratch(n_t),
    )(*shards)


def _pair_exchange(xs, name):
    n_t = len(xs)

    def body(*refs):
        x_refs, out_refs = refs[:n_t], refs[n_t:2 * n_t]
        send_sems, recv_sems = refs[2 * n_t:]
        x, y, c = lax.axis_index("x"), lax.axis_index("y"), lax.axis_index("c")
        copies = [pltpu.make_async_remote_copy(
            src_ref=x_refs[t].at[1 - c], dst_ref=out_refs[t], send_sem=send_sems.at[t], recv_sem=recv_sems.at[t],
            device_id=(x, y, 1 - c), device_id_type=MESH) for t in range(n_t)]
        for cp in copies:
            cp.start()
        for cp in copies:
            cp.wait()

    any_spec = pl.BlockSpec(memory_space=pl.ANY)
    return pl.pallas_call(
        body, name=name,
        out_shape=[jax.ShapeDtypeStruct(x.shape[1:], x.dtype) for x in xs],
        in_specs=[any_spec] * n_t, out_specs=[any_spec] * n_t,
        scratch_shapes=[pltpu.SemaphoreType.DMA((n_t,)), pltpu.SemaphoreType.DMA((n_t,))],
    )(*xs)


def _exchange_phases(p_refs, out_refs, send_sems, recv_sems, local_sems):
    n_t = len(p_refs)
    mx, my, mc = lax.axis_index("x"), lax.axis_index("y"), lax.axis_index("c")
    me = 2 * mx + my

    def mine():
        return [pltpu.make_async_copy(p_refs[t].at[me], out_refs[t].at[me], local_sems.at[t]) for t in range(n_t)]

    def copies():
        out = []
        for k in range(1, 4):
            px = 1 - mx if (k >> 1) & 1 else mx
            py = 1 - my if k & 1 else my
            peer = 2 * px + py
            for t in range(n_t):
                out.append(pltpu.make_async_remote_copy(
                    src_ref=p_refs[t].at[peer], dst_ref=out_refs[t].at[me],
                    send_sem=send_sems.at[3 * t + k - 1], recv_sem=recv_sems.at[3 * t + k - 1],
                    device_id=(px, py, mc), device_id_type=MESH))
        return out

    def start():
        for cp in mine() + copies():
            cp.start()

    def finish():
        for cp in copies():
            cp.wait_send()
        for cp in copies():
            cp.wait_recv()
        for cp in mine():
            cp.wait()

    return start, finish


def _exchange_scratch(n_t):
    return [pltpu.SemaphoreType.DMA((3 * n_t,)), pltpu.SemaphoreType.DMA((3 * n_t,)), pltpu.SemaphoreType.DMA((n_t,))]


def _chip_exchange(ps, name):
    n_t = len(ps)

    def body(*refs):
        for phase in _exchange_phases(refs[:n_t], refs[n_t:2 * n_t], *refs[2 * n_t:]):
            phase()

    any_spec = pl.BlockSpec(memory_space=pl.ANY)
    return pl.pallas_call(
        body, name=name,
        out_shape=[jax.ShapeDtypeStruct(p.shape, p.dtype) for p in ps],
        in_specs=[any_spec] * n_t, out_specs=[any_spec] * n_t,
        scratch_shapes=_exchange_scratch(n_t),
    )(*ps)


def _matmul(a, b, *, ta=False, tb=False, out_dtypes=(F32,), epilogue=None, extras=(), name):
    if ta:
        kdim, m = a.shape
    else:
        m, kdim = a.shape
    if tb:
        n, kb = b.shape
    else:
        kb, n = b.shape
    assert kdim == kb, (a.shape, b.shape, ta, tb)
    big = (1024, 768, 512, 384, 256, 128, 64, 32, 16, 8)
    tm, tn = _tile(m, big), _tile(n, big)
    tk = kdim if kdim <= 2048 else _tile(kdim, (2048, 1024, 512, 256, 128))
    nk = kdim // tk
    n_extra, n_out = len(extras), len(out_dtypes)
    a_spec = pl.BlockSpec((tk, tm), lambda i, j, k: (k, i)) if ta else pl.BlockSpec((tm, tk), lambda i, j, k: (i, k))
    b_spec = pl.BlockSpec((tn, tk), lambda i, j, k: (j, k)) if tb else pl.BlockSpec((tk, tn), lambda i, j, k: (k, j))
    tile_spec = pl.BlockSpec((tm, tn), lambda i, j, k: (i, j))
    dims = ((0,) if ta else (1,), (1,) if tb else (0,))

    def finish(acc, extra_refs, out_refs):
        outs = (acc,) if epilogue is None else epilogue(acc, *[r[...] for r in extra_refs])
        for o_ref, o in zip(out_refs, outs):
            o_ref[...] = o.astype(o_ref.dtype)

    def body_one(a_ref, b_ref, *rest):
        acc = _dot(a_ref[...].astype(BF16), b_ref[...].astype(BF16), dims)
        finish(acc, rest[:n_extra], rest[n_extra:n_extra + n_out])

    def body_acc(a_ref, b_ref, *rest):
        acc_ref = rest[-1]
        k = pl.program_id(2)

        @pl.when(k == 0)
        def _():
            acc_ref[...] = jnp.zeros_like(acc_ref)

        acc_ref[...] += _dot(a_ref[...].astype(BF16), b_ref[...].astype(BF16), dims)

        @pl.when(k == nk - 1)
        def _():
            finish(acc_ref[...], rest[:n_extra], rest[n_extra:n_extra + n_out])

    return pl.pallas_call(
        body_one if nk == 1 else body_acc, name=name, grid=(m // tm, n // tn, nk),
        in_specs=[a_spec, b_spec] + [tile_spec] * n_extra,
        out_specs=[tile_spec] * n_out,
        out_shape=[jax.ShapeDtypeStruct((m, n), dt) for dt in out_dtypes],
        scratch_shapes=[] if nk == 1 else [pltpu.VMEM((tm, tn), F32)],
        compiler_params=pltpu.CompilerParams(
            dimension_semantics=("parallel", "parallel", "arbitrary"), vmem_limit_bytes=VMEM_MM),
    )(a, b, *extras)


def _rms_fwd(x, g, name):
    m, d = x.shape
    tm = _tile(m, (512, 256, 128, 64, 32, 16, 8))

    def body(x_ref, g_ref, y_ref):
        xv = x_ref[...]
        r = lax.rsqrt(jnp.mean(xv * xv, axis=-1, keepdims=True) + EPS)
        y_ref[...] = (xv * r * g_ref[...]).astype(y_ref.dtype)

    return pl.pallas_call(
        body, name=name, grid=(m // tm,),
        in_specs=[pl.BlockSpec((tm, d), lambda i: (i, 0)), pl.BlockSpec((1, d), lambda i: (0, 0))],
        out_specs=pl.BlockSpec((tm, d), lambda i: (i, 0)),
        out_shape=jax.ShapeDtypeStruct((m, d), BF16),
        compiler_params=pltpu.CompilerParams(dimension_semantics=("parallel",)),
    )(x, g.reshape(1, d))


def _rms_bwd(x, g, dy, name, res=None):
    m, d = x.shape
    tm = _tile(m, (512, 256, 128, 64, 32, 16, 8))
    has_res = res is not None

    def body(x_ref, g_ref, dy_ref, *rest):
        dx_ref, dg_ref = rest[-2:]
        xv = x_ref[...]
        dyv = dy_ref[...]
        r = lax.rsqrt(jnp.mean(xv * xv, axis=-1, keepdims=True) + EPS)
        xh = xv * r
        t = dyv * g_ref[...]
        dx = r * (t - xh * jnp.mean(t * xh, axis=-1, keepdims=True))
        dx_ref[...] = dx + rest[0][...] if has_res else dx

        @pl.when(pl.program_id(0) == 0)
        def _():
            dg_ref[...] = jnp.zeros_like(dg_ref)

        dg_ref[...] += jnp.sum(dyv * xh, axis=0, keepdims=True)

    row_spec = pl.BlockSpec((tm, d), lambda i: (i, 0))
    vec_spec = pl.BlockSpec((1, d), lambda i: (0, 0))
    dx, dg = pl.pallas_call(
        body, name=name, grid=(m // tm,),
        in_specs=[row_spec, vec_spec, row_spec] + ([row_spec] if has_res else []),
        out_specs=[row_spec, vec_spec],
        out_shape=[jax.ShapeDtypeStruct((m, d), F32), jax.ShapeDtypeStruct((1, d), F32)],
        compiler_params=pltpu.CompilerParams(dimension_semantics=("arbitrary",)),
    )(x, g.reshape(1, d), dy, *((res,) if has_res else ()))
    return dx, dg.reshape(d)


def _loss_head(x, g, target):
    m, d = x.shape
    tm = _tile(m, (512, 256, 128, 64, 32, 16, 8))

    def body(x_ref, g_ref, t_ref, loss_ref, dx_ref, dg_ref):
        xv = x_ref[...]
        gv = g_ref[...]
        r = lax.rsqrt(jnp.mean(xv * xv, axis=-1, keepdims=True) + EPS)
        xh = xv * r
        err = xh * gv - t_ref[...]
        row_loss = jnp.mean(err * err, axis=-1, keepdims=True)
        dyv = err * (1.0 / d)
        t = dyv * gv
        dx_ref[...] = r * (t - xh * jnp.mean(t * xh, axis=-1, keepdims=True))

        @pl.when(pl.program_id(0) == 0)
        def _():
            dg_ref[...] = jnp.zeros_like(dg_ref)
            loss_ref[...] = jnp.zeros_like(loss_ref)

        dg_ref[...] += jnp.sum(dyv * xh, axis=0, keepdims=True)
        loss_ref[...] += 0.5 * jnp.sum(row_loss, axis=0, keepdims=True)

    loss, dx, dg = pl.pallas_call(
        body, name="loss_head", grid=(m // tm,),
        in_specs=[pl.BlockSpec((tm, d), lambda i: (i, 0)), pl.BlockSpec((1, d), lambda i: (0, 0)),
                  pl.BlockSpec((tm, d), lambda i: (i, 0))],
        out_specs=[pl.BlockSpec((1, 1), lambda i: (0, 0)), pl.BlockSpec((tm, d), lambda i: (i, 0)),
                   pl.BlockSpec((1, d), lambda i: (0, 0))],
        out_shape=[jax.ShapeDtypeStruct((1, 1), F32), jax.ShapeDtypeStruct((m, d), F32),
                   jax.ShapeDtypeStruct((1, d), F32)],
        compiler_params=pltpu.CompilerParams(dimension_semantics=("arbitrary",)),
    )(x, g.reshape(1, d), target)
    return loss, dx, dg.reshape(d)


def _rope_apply(x, tabs, transpose, name):
    m, w = x.shape
    wt = tabs[0].shape[1]
    reps = w // wt
    half = ROPE // 2
    tm = _tile(m, (256, 128, 64, 32, 16, 8))

    def body(x_ref, c_ref, a_ref, b_ref, y_ref):
        xv = x_ref[...]

        def wide(t_ref):
            t = t_ref[...]
            return t if reps == 1 else jnp.concatenate([t] * reps, axis=1)

        c, a, b = wide(c_ref), wide(a_ref), wide(b_ref)
        if transpose:
            y = xv * c + pltpu.roll(xv * a, half, 1) + pltpu.roll(xv * b, w - half, 1)
        else:
            y = xv * c + pltpu.roll(xv, w - half, 1) * a + pltpu.roll(xv, half, 1) * b
        y_ref[...] = y

    x_spec = pl.BlockSpec((tm, w), lambda i: (i, 0))
    t_spec = pl.BlockSpec((tm, wt), lambda i: (i, 0))
    return pl.pallas_call(
        body, name=name, grid=(m // tm,),
        in_specs=[x_spec, t_spec, t_spec, t_spec], out_specs=x_spec,
        out_shape=jax.ShapeDtypeStruct((m, w), F32),
        compiler_params=pltpu.CompilerParams(dimension_semantics=("parallel",)),
    )(x, *tabs)


def _log_sigmoid_pair(z):
    a = jnp.minimum(z, 0.0) - jnp.log(1.0 + jnp.exp(-jnp.abs(z)))
    return a, a - z


def _split_bf16(x):
    hi = x.astype(BF16)
    return hi, (x - hi.astype(F32)).astype(BF16)


def _wide(v, width):
    return v if width == LANES else jnp.concatenate([v] * (width // LANES), axis=1)


def _lanes(col):
    return jnp.broadcast_to(col, (col.shape[0], LANES))


def _rope_head(x, c, a, b):
    half = ROPE // 2
    return x * c + pltpu.roll(x, LANES - half, 1) * a + pltpu.roll(x, half, 1) * b


def _rope_head_t(dy, c, a, b):
    half = ROPE // 2
    return dy * c + pltpu.roll(dy * a, half, 1) + pltpu.roll(dy * b, LANES - half, 1)


def _sb_weights(qh, k2, valid):
    a, b = _log_sigmoid_pair(_dot(qh, k2, NT))
    return a, b if valid is None else jnp.where(valid, b, 0.0)


def _sb_fwd(qkv, shards=()):
    s = qkv.shape[0]
    t = min(SB_BLOCK, s)
    nq = s // t
    npair = SB_HEADS // 2
    scale = SB_HD ** -0.5

    n_t = len(shards)

    def body(q_ref, k_ref, v_ref, *rest):
        x_refs, rest = rest[:n_t], rest[n_t:]
        o_ref, bta_ref, btb_ref, js_ref = rest[:4]
        g_refs, rest = rest[4:4 + n_t], rest[4 + n_t:]
        acc_ref, ra_ref, rb_ref = rest[:3]
        p, i = pl.program_id(0), pl.program_id(1)
        if n_t:
            start, forward, finish = _gather_phases(x_refs, g_refs, *rest[3:])
            pl.when(jnp.logical_and(p == 0, i == 0))(start)
            pl.when(jnp.logical_and(p == (5 * npair) // 8, i == 0))(forward)
        q2 = q_ref[...] * scale
        first = lax.broadcasted_iota(jnp.int32, (t, LANES), 1) < SB_HD
        heads = (jnp.where(first, q2, jnp.zeros_like(q2)), jnp.where(first, jnp.zeros_like(q2), q2))
        row = lax.broadcasted_iota(jnp.int32, (t, t), 0)
        col = lax.broadcasted_iota(jnp.int32, (t, t), 1)
        later = jnp.where(row > col, 1.0, 0.0).astype(BF16)
        acc_ref[...] = jnp.zeros_like(acc_ref)
        ra_ref[...] = jnp.zeros_like(ra_ref)
        rb_ref[...] = jnp.zeros_like(rb_ref)

        def block(j, diagonal):
            off = pl.multiple_of(j * t, t)
            k2 = k_ref[pl.ds(off, t), :]
            v2 = v_ref[pl.ds(off, t), :]
            valid = col < row if diagonal else None
            outs, rmax = [], None
            for qh, r_ref in zip(heads, (ra_ref, rb_ref)):
                a, b = _sb_weights(qh, k2, valid)
                bh, bl = _split_bf16(b)
                inner = _dot(bh, later, NN) + _dot(bl, later, NN)
                r = r_ref[...]
                w = jnp.exp(a + inner + _wide(r, t))
                if diagonal:
                    w = jnp.where(valid, w, 0.0)
                outs.append(_dot(w.astype(BF16), v2, NN))
                rn = r + jnp.sum(b, axis=1, keepdims=True)
                r_ref[...] = rn
                rmax = jnp.max(rn) if rmax is None else jnp.maximum(rmax, jnp.max(rn))
            acc_ref[...] += jnp.where(first, outs[0], outs[1])
            return rmax

        def cond(carry):
            j, rmax = carry
            return jnp.logical_and(j >= 0, rmax > EXP_ZERO)

        def step(carry):
            j, _ = carry
            return j - 1, block(j, False)

        def first_two():
            block(i, True)
            return block(i - 1, False)

        rmax = lax.cond(i >= 1, first_two, lambda: block(i, True))
        jend, _ = lax.while_loop(cond, step, (jnp.maximum(i - 2, -1), rmax))
        o_ref[...] = acc_ref[...].astype(o_ref.dtype)
        bta_ref[0] = ra_ref[...][:, :1]
        btb_ref[0] = rb_ref[...][:, :1]
        js_ref[p, i] = (jend + 1).astype(F32)
        if n_t:
            pl.when(jnp.logical_and(p == npair - 1, i == nq - 1))(finish)

    stat = pl.BlockSpec((1, t, 1), lambda p, i: (p, i, 0))
    any_spec = pl.BlockSpec(memory_space=pl.ANY)
    return pl.pallas_call(
        body, name="sb_attn_fwd_gather" if n_t else "sb_attn_fwd", grid=(npair, nq),
        in_specs=[pl.BlockSpec((t, LANES), lambda p, i: (i, p)),
                  pl.BlockSpec((s, LANES), lambda p, i: (0, npair + p)),
                  pl.BlockSpec((s, LANES), lambda p, i: (0, 2 * npair + p))] + [any_spec] * n_t,
        out_specs=[pl.BlockSpec((t, LANES), lambda p, i: (i, p)), stat, stat,
                   pl.BlockSpec(memory_space=pltpu.SMEM)] + [any_spec] * n_t,
        out_shape=[jax.ShapeDtypeStruct((s, SB_HEADS * SB_HD), BF16), jax.ShapeDtypeStruct((npair, s, 1), F32),
                   jax.ShapeDtypeStruct((npair, s, 1), F32), jax.ShapeDtypeStruct((npair, nq), F32)]
        + [jax.ShapeDtypeStruct((N_DEV,) + tuple(x.shape), x.dtype) for x in shards],
        scratch_shapes=[pltpu.VMEM((t, LANES), F32)] * 3 + (_gather_scratch(n_t) if n_t else []),
        compiler_params=pltpu.CompilerParams(
            dimension_semantics=("arbitrary", "arbitrary"), vmem_limit_bytes=VMEM_ATT),
    )(qkv, qkv, qkv, *shards)


def _sb_bwd(qkv, do, btot_a, btot_b, jstart, sums=()):
    s = qkv.shape[0]
    t = min(SB_BLOCK, s)
    nq = s // t
    npair = SB_HEADS // 2
    scale = SB_HD ** -0.5

    n_t = len(sums)

    def body(js_ref, q_ref, k_ref, v_ref, do_ref, bta_ref, btb_ref, *rest):
        s_refs, rest = rest[:n_t], rest[n_t:]
        dq_ref, dk_ref, dv_ref = rest[:3]
        r_refs, rest = rest[3:3 + n_t], rest[3 + n_t:]
        dq_acc, pa_ref, pb_ref, ga_ref, gb_ref, dk_acc, dv_acc = rest[:7]
        p, i = pl.program_id(0), pl.program_id(1)
        if n_t:
            start, finish = _exchange_phases(s_refs, r_refs, *rest[7:])
            pl.when(jnp.logical_and(p == 0, i == 0))(start)

        @pl.when(i == 0)
        def _():
            dk_acc[...] = jnp.zeros_like(dk_acc)
            dv_acc[...] = jnp.zeros_like(dv_acc)

        q2 = q_ref[...] * scale
        do2 = do_ref[...]
        first = lax.broadcasted_iota(jnp.int32, (t, LANES), 1) < SB_HD
        zero = jnp.zeros_like(q2)
        q_heads = (jnp.where(first, q2, zero), jnp.where(first, zero, q2))
        do_heads = (jnp.where(first, do2, zero), jnp.where(first, zero, do2))
        bts = (_wide(_lanes(bta_ref[0]), t), _wide(_lanes(btb_ref[0]), t))
        row = lax.broadcasted_iota(jnp.int32, (t, t), 0)
        col = lax.broadcasted_iota(jnp.int32, (t, t), 1)
        upto = jnp.where(row <= col, 1.0, 0.0).astype(BF16)
        before = jnp.where(row < col, 1.0, 0.0).astype(BF16)
        dq_acc[...] = jnp.zeros_like(dq_acc)
        for r in (pa_ref, pb_ref, ga_ref, gb_ref):
            r[...] = jnp.zeros_like(r)
        j0 = jnp.clip(js_ref[p, i].astype(jnp.int32), 0, i)

        def block(j, diagonal):
            off = pl.multiple_of(j * t, t)
            k2 = k_ref[pl.ds(off, t), :]
            v2 = v_ref[pl.ds(off, t), :]
            valid = col < row if diagonal else None
            dqs, dk2, dv2 = [], None, None
            for qh, doh, bt, p_ref, g_ref in zip(q_heads, do_heads, bts, (pa_ref, pb_ref), (ga_ref, gb_ref)):
                a, b = _sb_weights(qh, k2, valid)
                bh, bl = _split_bf16(b)
                pin = _dot(bh, upto, NN) + _dot(bl, upto, NN)
                surv = bt - (_wide(p_ref[...], t) + pin)
                w = jnp.exp(a + surv)
                if diagonal:
                    w = jnp.where(valid, w, 0.0)
                g = w * _dot(doh, v2, NT)
                gh, gl = _split_bf16(g)
                gsum = _wide(g_ref[...], t) + _dot(gh, before, NN) + _dot(gl, before, NN)
                beta = jnp.exp(a)
                dz = g * (1.0 - beta) - gsum * beta
                if diagonal:
                    dz = jnp.where(valid, dz, 0.0)
                dz = dz.astype(BF16)
                dqs.append(_dot(dz, k2, NN))
                dkh = _dot(dz, qh, TN)
                dvh = _dot(w.astype(BF16), doh, TN)
                dk2 = dkh if dk2 is None else dk2 + dkh
                dv2 = dvh if dv2 is None else dv2 + dvh
                p_ref[...] += jnp.sum(b, axis=1, keepdims=True)
                g_ref[...] += jnp.sum(g, axis=1, keepdims=True)
            dq_acc[...] += jnp.where(first, dqs[0], dqs[1])
            dk_acc[pl.ds(off, t), :] += dk2
            dv_acc[pl.ds(off, t), :] += dv2

        def step(j, carry):
            block(j, False)
            return carry

        lax.fori_loop(j0, jnp.maximum(i - 1, j0), step, 0)

        @pl.when(j0 < i)
        def _():
            block(i - 1, False)
            block(i, True)

        @pl.when(j0 >= i)
        def _():
            block(i, True)

        dq_ref[...] = (dq_acc[...] * scale).astype(dq_ref.dtype)

        @pl.when(i == nq - 1)
        def _():
            dk_ref[...] = dk_acc[...].astype(dk_ref.dtype)
            dv_ref[...] = dv_acc[...].astype(dv_ref.dtype)

        if n_t:
            pl.when(jnp.logical_and(p == npair - 1, i == nq - 1))(finish)

    blk = pl.BlockSpec((t, LANES), lambda p, i: (i, p))
    full = pl.BlockSpec((s, LANES), lambda p, i: (0, p))
    stat = pl.BlockSpec((1, t, 1), lambda p, i: (p, i, 0))
    vec = pltpu.VMEM((t, LANES), F32)
    any_spec = pl.BlockSpec(memory_space=pl.ANY)
    return pl.pallas_call(
        body, name="sb_attn_bwd_exchange" if n_t else "sb_attn_bwd", grid=(npair, nq),
        in_specs=[pl.BlockSpec(memory_space=pltpu.SMEM), blk,
                  pl.BlockSpec((s, LANES), lambda p, i: (0, npair + p)),
                  pl.BlockSpec((s, LANES), lambda p, i: (0, 2 * npair + p)), blk, stat, stat] + [any_spec] * n_t,
        out_specs=[blk, full, full] + [any_spec] * n_t,
        out_shape=[jax.ShapeDtypeStruct((s, SB_HEADS * SB_HD), BF16)] * 3
        + [jax.ShapeDtypeStruct(x.shape, x.dtype) for x in sums],
        scratch_shapes=[pltpu.VMEM((t, LANES), F32), vec, vec, vec, vec,
                        pltpu.VMEM((s, LANES), F32), pltpu.VMEM((s, LANES), F32)]
        + (_exchange_scratch(n_t) if n_t else []),
        compiler_params=pltpu.CompilerParams(
            dimension_semantics=("arbitrary", "arbitrary"), vmem_limit_bytes=VMEM_ATT),
    )(jstart, qkv, qkv, qkv, do, btot_a, btot_b, *sums)


def _make_sb_attention(side=None):
    @jax.custom_vjp
    def f(qkv, *shards):
        o, _, _, _, *gathered = _sb_fwd(qkv, shards)
        return (o, *gathered)

    def fwd(qkv, *shards):
        o, btot_a, btot_b, jstart, *gathered = _sb_fwd(qkv, shards)
        return (o, *gathered), (qkv, btot_a, btot_b, jstart, shards)

    def bwd(res, cts):
        qkv, btot_a, btot_b, jstart, shards = res
        sums = () if side is None else tuple(side['sums'])
        dq, dk, dv, *received = _sb_bwd(qkv, cts[0], btot_a, btot_b, jstart, sums)
        if side is not None:
            side['received'] = received
        return (jnp.concatenate([dq, dk, dv], axis=1),) + tuple(jnp.zeros_like(x) for x in shards)

    f.defvjp(fwd, bwd)
    return f


def sb_attention(qkv):
    return _make_sb_attention()(qkv)[0]


def _chunk_allowed(row, col):
    return (col >> CHUNK_SHIFT) <= (row >> CHUNK_SHIFT)


def _mla_fwd(q, k, vx, tabs):
    s = q.shape[0]
    h_ = q.shape[1] // LANES
    t = min(MLA_BLOCK, s)
    nq = s // t
    scale = QK ** -0.5
    hp = MLA_GROUP

    def body(q_ref, k_ref, v_ref, c_ref, a_ref, b_ref, o_ref, lse_ref, acc_ref, m_ref):
        i = pl.program_id(1)
        acc_ref[...] = jnp.zeros_like(acc_ref)
        m_ref[...] = jnp.full_like(m_ref, NEG_BIG)
        rot = (c_ref[...], a_ref[...], b_ref[...])
        qs = [_rope_head(q_ref[:, hh * LANES:(hh + 1) * LANES], *rot).astype(BF16) for hh in range(hp)]

        def tile(j, diagonal):
            off = pl.multiple_of(j * t, t)
            for hh in range(hp):
                lanes = slice(hh * LANES, (hh + 1) * LANES)
                kj = k_ref[pl.ds(off, t), lanes]
                vj = v_ref[pl.ds(off, t), lanes]
                sc = _dot(qs[hh], kj, NT) * (scale * LOG2_E)
                if diagonal:
                    row = lax.broadcasted_iota(jnp.int32, (t, t), 0)
                    col = lax.broadcasted_iota(jnp.int32, (t, t), 1)
                    sc = jnp.where(_chunk_allowed(row, col), sc, NEG_BIG)
                m_old = m_ref[hh]
                m_new = jnp.maximum(m_old, jnp.max(sc, axis=1, keepdims=True))
                p = jnp.exp2(sc - _wide(m_new, t))
                acc_ref[hh] = jnp.exp2(m_old - m_new) * acc_ref[hh] + _dot(p.astype(BF16), vj, NN)
                m_ref[hh] = m_new

        def step(jj, carry):
            tile(2 * jj, False)
            tile(2 * jj + 1, False)
            return carry

        lax.fori_loop(0, i // 2, step, 0)

        @pl.when(i % 2 == 1)
        def _():
            tile(i - 1, False)

        tile(i, True)
        first = lax.broadcasted_iota(jnp.int32, (t, LANES), 1) < VDIM
        outs = []
        for hh in range(hp):
            acc = acc_ref[hh]
            den = acc[:, VDIM:VDIM + 1]
            outs.append(acc / den)
            lse_ref[hh] = (m_ref[hh][:, :1] + jnp.log2(den)) * LN_2
        for pp in range(hp // 2):
            o_ref[:, pp * LANES:(pp + 1) * LANES] = jnp.where(
                first, outs[2 * pp], pltpu.roll(outs[2 * pp + 1], VDIM, 1))

    wide = hp * LANES
    return pl.pallas_call(
        body, name="mla_attn_fwd", grid=(h_ // hp, nq),
        in_specs=[pl.BlockSpec((t, wide), lambda g, i: (i, g)),
                  pl.BlockSpec((s, wide), lambda g, i: (0, g)),
                  pl.BlockSpec((s, wide), lambda g, i: (0, g))] + [pl.BlockSpec((t, LANES), lambda g, i: (i, 0))] * 3,
        out_specs=[pl.BlockSpec((t, hp * VDIM), lambda g, i: (i, g)),
                   pl.BlockSpec((hp, t, 1), lambda g, i: (g, i, 0))],
        out_shape=[jax.ShapeDtypeStruct((s, h_ * VDIM), F32), jax.ShapeDtypeStruct((h_, s, 1), F32)],
        scratch_shapes=[pltpu.VMEM((hp, t, LANES), F32), pltpu.VMEM((hp, t, LANES), F32)],
        compiler_params=pltpu.CompilerParams(
            dimension_semantics=("arbitrary", "arbitrary"), vmem_limit_bytes=VMEM_ATT),
    )(q, k, vx, *tabs)


def _mla_bwd(q, k, vx, tabs, o, lse, do):
    s = q.shape[0]
    h_ = q.shape[1] // LANES
    t = min(MLA_BLOCK, s)
    nq = s // t
    scale = QK ** -0.5

    def body(q_ref, k_ref, v_ref, c_ref, a_ref, b_ref, o_ref, lse_ref, do_ref, dq_ref, dk_ref, dv_ref, dq_acc):
        h, i = pl.program_id(0), pl.program_id(1)
        rot = (c_ref[...], a_ref[...], b_ref[...])

        @pl.when(i == 0)
        def _():
            dk_ref[...] = jnp.zeros_like(dk_ref)
            dv_ref[...] = jnp.zeros_like(dv_ref)

        qv = _rope_head(q_ref[...], *rot).astype(BF16)
        lane = lax.broadcasted_iota(jnp.int32, (t, LANES), 1)
        start = (h % 2) * VDIM
        mine = jnp.logical_and(lane >= start, lane < start + VDIM)
        do2 = do_ref[...]
        delta = _lanes(jnp.sum(jnp.where(mine, do2 * o_ref[...], 0.0), axis=1, keepdims=True))
        odd = (start + jnp.zeros_like(lane)) > 0
        do_head = jnp.where(odd, pltpu.roll(do2, VDIM, 1), do2)
        dov = jnp.where(lane < VDIM, do_head, 0.0).astype(BF16)
        lse = _lanes(lse_ref[0])
        dq_acc[...] = jnp.zeros_like(dq_acc)

        def tile(j, diagonal):
            off = pl.multiple_of(j * t, t)
            kj = k_ref[pl.ds(off, t), :]
            vj = v_ref[pl.ds(off, t), :]
            p = jnp.exp(_dot(qv, kj, NT) * scale - _wide(lse, t))
            if diagonal:
                row = lax.broadcasted_iota(jnp.int32, (t, t), 0)
                col = lax.broadcasted_iota(jnp.int32, (t, t), 1)
                p = jnp.where(_chunk_allowed(row, col), p, 0.0)
            dp = _dot(dov, vj, NT)
            ds = (p * (dp - _wide(delta, t)) * scale).astype(BF16)
            dq_acc[...] += _dot(ds, kj, NN)
            dk_ref[pl.ds(off, t), :] += _dot(ds, qv, TN)
            dv_ref[pl.ds(off, t), :] += _dot(p.astype(BF16), dov, TN)

        def step(jj, carry):
            for u in range(MLA_UNROLL):
                tile(MLA_UNROLL * jj + u, False)
            return carry

        def single(j, carry):
            tile(j, False)
            return carry

        lax.fori_loop(0, i // MLA_UNROLL, step, 0)
        lax.fori_loop(i - i % MLA_UNROLL, i, single, 0)
        tile(i, True)
        dq_ref[...] = _rope_head_t(dq_acc[...], *rot)

    blk = pl.BlockSpec((t, LANES), lambda h, i: (i, h))
    full = pl.BlockSpec((s, LANES), lambda h, i: (0, h))
    pair = pl.BlockSpec((t, LANES), lambda h, i: (i, h // 2))
    stat = pl.BlockSpec((1, t, 1), lambda h, i: (h, i, 0))
    return pl.pallas_call(
        body, name="mla_attn_bwd", grid=(h_, nq),
        in_specs=[blk, full, full] + [pl.BlockSpec((t, LANES), lambda h, i: (i, 0))] * 3 + [pair, stat, pair],
        out_specs=[blk, full, full],
        out_shape=[jax.ShapeDtypeStruct((s, h_ * LANES), F32)] * 3,
        scratch_shapes=[pltpu.VMEM((t, LANES), F32)],
        compiler_params=pltpu.CompilerParams(
            dimension_semantics=("arbitrary", "arbitrary"), vmem_limit_bytes=VMEM_ATT),
    )(q, k, vx, *tabs, o, lse, do)


def _assemble_kv(kvp, kr):
    half = kvp.shape[1] // 2
    ones = jnp.tile((jnp.arange(LANES) == VDIM).astype(F32), MLA_HEADS)
    k = kvp[:, :half] + jnp.tile(kr, (1, MLA_HEADS))
    return k.astype(BF16), (kvp[:, half:] + ones).astype(BF16)


@jax.custom_vjp
def mla_attention(q, kvp, kr, c, a, b):
    return _mla_fwd(q, *_assemble_kv(kvp, kr), (c, a, b))[0]


def _mla_attention_fwd(q, kvp, kr, c, a, b):
    kb, vb = _assemble_kv(kvp, kr)
    o, lse = _mla_fwd(q, kb, vb, (c, a, b))
    return o, (q, kb, vb, (c, a, b), o, lse)


def _mla_attention_bwd(res, do):
    q, kb, vb, tabs, o, lse = res
    dq, dk, dvx = _mla_bwd(q, kb, vb, tabs, o, lse, do)
    dkr = dk.reshape(dk.shape[0], MLA_HEADS, LANES).sum(axis=1)
    return (dq, jnp.concatenate([dk, dvx], axis=1), dkr) + tuple(jnp.zeros_like(t) for t in tabs)


mla_attention.defvjp(_mla_attention_fwd, _mla_attention_bwd)


def _add_tile(acc, res):
    return (acc + res,)


def _make_linear_res(tag, a_dtype=F32):
    def forward(x, a, slot, wb):
        ab = a.astype(BF16)
        y = _matmul(ab, wb, epilogue=_add_tile, extras=(x,), name=tag + "_fwd")[0]
        return y, (ab, wb)

    @jax.custom_vjp
    def f(x, a, slot, wb):
        return forward(x, a, slot, wb)[0]

    def bwd(res, dy):
        ab, wb = res
        dyb = dy.astype(BF16)
        da = _matmul(dyb, wb, tb=True, out_dtypes=(a_dtype,), name=tag + "_da")[0]
        dw = _matmul(ab, dyb, ta=True, name=tag + "_dw")[0]
        return dy, da, dw, jnp.zeros_like(wb)

    f.defvjp(forward, bwd)
    return f


def _make_norm_linear(tag, out_dtype=F32):
    def forward(x, g, slot, wb):
        hb = _rms_fwd(x, g, tag + "_norm")
        y = _matmul(hb, wb, out_dtypes=(out_dtype,), name=tag + "_fwd")[0]
        return y, (x, g, wb, hb)

    @jax.custom_vjp
    def f(x, g, slot, wb):
        return forward(x, g, slot, wb)[0]

    def bwd(res, dy):
        x, g, wb, hb = res
        dyb = dy.astype(BF16)
        dh = _matmul(dyb, wb, tb=True, name=tag + "_dh")[0]
        dw = _matmul(hb, dyb, ta=True, name=tag + "_dw")[0]
        dx, dg = _rms_bwd(x, g, dh, tag + "_norm_bwd")
        return dx, dg, dw, jnp.zeros_like(wb)

    f.defvjp(forward, bwd)
    return f


def _relu2_fwd(acc):
    r = jnp.maximum(acc, 0.0)
    return acc, r * r


def _relu2_bwd(acc, u):
    return (acc * (2.0 * jnp.maximum(u.astype(F32), 0.0)),)


def _make_mlp_res(tag):
    def forward(x, g, slot1, slot2, w1b, w2b):
        hb = _rms_fwd(x, g, tag + "_norm")
        u, act = _matmul(hb, w1b, out_dtypes=(BF16, BF16), epilogue=_relu2_fwd, name=tag + "_up")
        y = _matmul(act, w2b, epilogue=_add_tile, extras=(x,), name=tag + "_down")[0]
        return y, (x, g, w1b, w2b, hb, u, act)

    @jax.custom_vjp
    def f(x, g, slot1, slot2, w1b, w2b):
        return forward(x, g, slot1, slot2, w1b, w2b)[0]

    def bwd(res, dy):
        x, g, w1b, w2b, hb, u, act = res
        dyb = dy.astype(BF16)
        du = _matmul(dyb, w2b, tb=True, out_dtypes=(BF16,), epilogue=_relu2_bwd, extras=(u,), name=tag + "_du")[0]
        dw2 = _matmul(act, dyb, ta=True, name=tag + "_dw2")[0]
        dw1 = _matmul(hb, du, ta=True, name=tag + "_dw1")[0]
        dh = _matmul(du, w1b, tb=True, name=tag + "_dh")[0]
        dx, dg = _rms_bwd(x, g, dh, tag + "_norm_bwd", res=dy)
        return dx, dg, dw1, dw2, jnp.zeros_like(w1b), jnp.zeros_like(w2b)

    f.defvjp(forward, bwd)
    return f


def _make_rope(tag):
    @jax.custom_vjp
    def f(x, c, a, b):
        return _rope_apply(x, (c, a, b), False, tag + "_fwd")

    def fwd(x, c, a, b):
        return _rope_apply(x, (c, a, b), False, tag + "_fwd"), (c, a, b)

    def bwd(res, dy):
        c, a, b = res
        return _rope_apply(dy, (c, a, b), True, tag + "_bwd"), jnp.zeros_like(c), jnp.zeros_like(a), jnp.zeros_like(b)

    f.defvjp(fwd, bwd)
    return f


def _rope_tables(positions):
    half = ROPE // 2
    inv_freq = ROPE_THETA ** (-jnp.arange(0, ROPE, 2, dtype=F32) / ROPE)
    ang = positions.astype(F32)[:, None] * inv_freq
    cos, sin = jnp.cos(ang), jnp.sin(ang)
    s = positions.shape[0]
    one = lambda n: jnp.ones((s, n), F32)
    zero = lambda n: jnp.zeros((s, n), F32)

    def tables(before, after):
        return (jnp.concatenate([one(before), cos, cos, one(after)], axis=1),
                jnp.concatenate([zero(before), -sin, zero(half + after)], axis=1),
                jnp.concatenate([zero(before + half), sin, zero(after)], axis=1))

    return tables(NOPE, LANES - QK), tables(KV_RANK, DKV_PAD - KV_RANK - ROPE)


def _pad_heads(w, per):
    lead = w.shape[:-1]
    w = jnp.pad(w.reshape(lead + (MLA_HEADS, per)), [(0, 0)] * len(lead) + [(0, 0), (0, LANES - per)])
    return w.reshape(lead + (MLA_HEADS * LANES,))


def _split_kv_heads(w):
    w3 = w.reshape(w.shape[0], MLA_HEADS, NOPE + VDIM)
    return jnp.concatenate([_pad_heads(w3[..., :NOPE].reshape(w.shape[0], -1), NOPE),
                            _pad_heads(w3[..., NOPE:].reshape(w.shape[0], -1), VDIM)], axis=1)


def _layers(full):
    return [full[l] for l in range(full.shape[0])] if full.ndim == 3 else [full]


def _first_attention(x, slot_qkv, slot_o, norms, w_qkv0, shards, side):
    qkv = _make_norm_linear("sb0_qkv", BF16)(x, norms['attn_norm'][0], slot_qkv, w_qkv0)
    o, *gathered = _make_sb_attention(side)(qkv, *shards)
    w_o = _layers(_merge_blocks(gathered[0], SHARDED[1][1]))[0]
    return _make_linear_res("sb0_o", BF16)(x, o, slot_o, w_o), gathered


def _rest(x, slots, norms, wb, q_tabs, kv_tabs):
    kvp = kr = None
    for layer in range(DEPTH):
        if 0 < layer < N_A:
            qkv = _make_norm_linear(f"sb{layer}_qkv", BF16)(
                x, norms['attn_norm'][layer], slots['sb_w_qkv'][layer], wb['sb_w_qkv'][layer])
            x = _make_linear_res(f"sb{layer}_o", BF16)(
                x, sb_attention(qkv), slots['sb_w_o'][layer], wb['sb_w_o'][layer])
        elif layer >= N_A:
            j = layer - N_A
            if j == 0:
                pad = ((0, 0), (0, DKV_PAD - KV_RANK - ROPE))
                down = _make_norm_linear("kv_down")(
                    x, norms['kv_norm'], jnp.pad(slots['mla_w_dkv'][0], pad), jnp.pad(wb['mla_w_dkv'][0], pad))
                kvp = _make_norm_linear("kv_up")(
                    down[:, :KV_RANK], norms['mla_kv_lat_norm'],
                    _split_kv_heads(slots['mla_w_ukv'][0]), _split_kv_heads(wb['mla_w_ukv'][0]))
                k_rope = _make_rope("rope_k")(down, *kv_tabs)[:, KV_RANK:KV_RANK + ROPE]
                kr = jnp.pad(k_rope, ((0, 0), (NOPE, LANES - QK)))
            c_q = _make_norm_linear(f"mla{j}_dq")(
                x, norms['attn_norm'][layer], slots['mla_w_dq'][j], wb['mla_w_dq'][j])
            q = _make_norm_linear(f"mla{j}_uq")(
                c_q, norms['mla_q_lat_norm'][j], _pad_heads(slots['mla_w_uq'][j], QK), _pad_heads(wb['mla_w_uq'][j], QK))
            o = mla_attention(q, kvp, kr, *q_tabs)
            x = _make_linear_res(f"mla{j}_o")(x, o, slots['mla_w_o'][j], wb['mla_w_o'][j])
        x = _make_mlp_res(f"mlp{layer}")(
            x, norms['mlp_norm'][layer], slots['mlp_w1'][layer], slots['mlp_w2'][layer],
            wb['mlp_w1'][layer], wb['mlp_w2'][layer])
    return x


def _merge_blocks(gathered, ax):
    shp = gathered.shape[1:]
    return jnp.moveaxis(gathered, 0, ax).reshape(shp[:ax] + (N_DEV * shp[ax],) + shp[ax + 1:])


def _split_blocks(full, ax):
    shp = full.shape
    return jnp.moveaxis(full.reshape(shp[:ax] + (N_DEV, shp[ax] // N_DEV) + shp[ax + 1:]), ax, 0)


def _pack(parts):
    flat = jnp.concatenate([p.reshape(-1) for p in parts])
    rows = -(-flat.shape[0] // PACK_COLS)
    rows = -(-rows // PACK_ROW_ALIGN) * PACK_ROW_ALIGN
    return jnp.pad(flat, (0, rows * PACK_COLS - flat.shape[0])).reshape(rows, PACK_COLS)


def _unpack(packed, shapes):
    flat = packed.reshape(-1)
    out, off = [], 0
    for shp in shapes:
        n = math.prod(shp)
        out.append(flat[off:off + n].reshape(shp))
        off += n
    return out


def _add_round(x, stage, core, name):
    _, r, c = x.shape
    tr = _tile(r, (256, 128, 64, 32, 16, 8))

    def body(core_ref, x_ref, s_ref, o_ref):
        o_ref[...] = (x_ref[0] + s_ref[...]).astype(o_ref.dtype)

    return pl.pallas_call(
        body, name=name,
        grid_spec=pltpu.PrefetchScalarGridSpec(
            num_scalar_prefetch=1, grid=(r // tr,),
            in_specs=[pl.BlockSpec((1, tr, c), lambda i, core_ref: (core_ref[0], i, 0)),
                      pl.BlockSpec((tr, c), lambda i, core_ref: (i, 0))],
            out_specs=pl.BlockSpec((tr, c), lambda i, core_ref: (i, 0))),
        out_shape=jax.ShapeDtypeStruct((r, c), BF16),
        compiler_params=pltpu.CompilerParams(dimension_semantics=("parallel",)),
    )(core, x, stage)


def _adamw_reduce(parts, w, m, v, name):
    n_parts, r, c = parts.shape
    tr = _tile(r, (128, 64, 32, 16, 8))
    bias1 = 1.0 - ADAM_B1 ** ADAM_STEP
    bias2 = 1.0 - ADAM_B2 ** ADAM_STEP

    def body(p_ref, w_ref, m_ref, v_ref, g_ref, d_ref, nm_ref, nv_ref):
        g = p_ref[0].astype(F32)
        for s in range(1, n_parts):
            g = g + p_ref[s].astype(F32)
        mn = ADAM_B1 * m_ref[...] + (1.0 - ADAM_B1) * g
        vn = ADAM_B2 * v_ref[...] + (1.0 - ADAM_B2) * (g * g)
        m_hat = mn / bias1
        v_hat = vn / bias2
        g_ref[...] = g
        d_ref[...] = -ADAM_LR * (m_hat / (jnp.sqrt(v_hat) + ADAM_EPS) + ADAM_WD * w_ref[...])
        nm_ref[...] = mn
        nv_ref[...] = vn

    blk = pl.BlockSpec((tr, c), lambda i: (i, 0))
    return pl.pallas_call(
        body, name=name, grid=(r // tr,),
        in_specs=[pl.BlockSpec((n_parts, tr, c), lambda i: (0, i, 0)), blk, blk, blk],
        out_specs=[blk] * 4,
        out_shape=[jax.ShapeDtypeStruct((r, c), F32)] * 4,
        compiler_params=pltpu.CompilerParams(dimension_semantics=("parallel",), vmem_limit_bytes=VMEM_MM),
    )(parts, w, m, v)


def kernel(x, positions, attn_norm, mlp_norm, sb_w_qkv, sb_w_o, kv_norm, mla_w_dkv, mla_kv_lat_norm, mla_w_ukv, mla_w_dq, mla_q_lat_norm, mla_w_uq, mla_w_o, mlp_w1, mlp_w2, final_norm, loss_target, m_attn_norm, m_mlp_norm, m_sb_w_qkv, m_sb_w_o, m_kv_norm, m_mla_w_dkv, m_mla_kv_lat_norm, m_mla_w_ukv, m_mla_w_dq, m_mla_q_lat_norm, m_mla_w_uq, m_mla_w_o, m_mlp_w1, m_mlp_w2, m_final_norm, v_attn_norm, v_mlp_norm, v_sb_w_qkv, v_sb_w_o, v_kv_norm, v_mla_w_dkv, v_mla_kv_lat_norm, v_mla_w_ukv, v_mla_w_dq, v_mla_q_lat_norm, v_mla_w_uq, v_mla_w_o, v_mlp_w1, v_mlp_w2, v_final_norm):
    weights = dict(attn_norm=attn_norm, mlp_norm=mlp_norm, sb_w_qkv=sb_w_qkv, sb_w_o=sb_w_o, kv_norm=kv_norm,
                   mla_w_dkv=mla_w_dkv, mla_kv_lat_norm=mla_kv_lat_norm, mla_w_ukv=mla_w_ukv, mla_w_dq=mla_w_dq,
                   mla_q_lat_norm=mla_q_lat_norm, mla_w_uq=mla_w_uq, mla_w_o=mla_w_o, mlp_w1=mlp_w1, mlp_w2=mlp_w2,
                   final_norm=final_norm)
    mom_m = dict(attn_norm=m_attn_norm, mlp_norm=m_mlp_norm, sb_w_qkv=m_sb_w_qkv, sb_w_o=m_sb_w_o, kv_norm=m_kv_norm,
                 mla_w_dkv=m_mla_w_dkv, mla_kv_lat_norm=m_mla_kv_lat_norm, mla_w_ukv=m_mla_w_ukv, mla_w_dq=m_mla_w_dq,
                 mla_q_lat_norm=m_mla_q_lat_norm, mla_w_uq=m_mla_w_uq, mla_w_o=m_mla_w_o, mlp_w1=m_mlp_w1,
                 mlp_w2=m_mlp_w2, final_norm=m_final_norm)
    mom_v = dict(attn_norm=v_attn_norm, mlp_norm=v_mlp_norm, sb_w_qkv=v_sb_w_qkv, sb_w_o=v_sb_w_o, kv_norm=v_kv_norm,
                 mla_w_dkv=v_mla_w_dkv, mla_kv_lat_norm=v_mla_kv_lat_norm, mla_w_ukv=v_mla_w_ukv, mla_w_dq=v_mla_w_dq,
                 mla_q_lat_norm=v_mla_q_lat_norm, mla_w_uq=v_mla_w_uq, mla_w_o=v_mla_w_o, mlp_w1=v_mlp_w1,
                 mlp_w2=v_mlp_w2, final_norm=v_final_norm)
    sharded_names = [n for n, _ in SHARDED]
    repl_shapes = [tuple(weights[n].shape) for n in REPLICATED]

    first_name, first_ax = SHARDED[0]
    w_qkv = _layers(_merge_blocks(_all_gather([weights[first_name].astype(BF16)], "gather_qkv")[0], first_ax))
    shards = [weights[n].astype(BF16) for n in sharded_names[1:]]
    slots, layer_ax = {}, {}
    for n, ax in SHARDED:
        shp = weights[n].shape
        full = shp[:ax] + (N_DEV * shp[ax],) + shp[ax + 1:]
        n_layers, per_layer = (full[0], full[1:]) if len(full) == 3 else (1, full)
        slots[n] = [jnp.zeros(per_layer, F32) for _ in range(n_layers)]
        layer_ax[n] = ax - 1 if len(full) == 3 else ax
    norms = {n: weights[n] for n in REPLICATED if n != 'final_norm'}
    late = [('sb_w_qkv', 0), ('sb_w_o', 0)]
    early = [(n, l) for n in sharded_names for l in range(len(slots[n])) if (n, l) not in late]

    q_tabs, kv_tabs = _rope_tables(positions[0])
    side = {}
    x_mid, pull_first, gathered = jax.vjp(
        lambda xx, sq, so, nn: _first_attention(xx, sq, so, nn, w_qkv[0], shards, side),
        x[0], slots['sb_w_qkv'][0], slots['sb_w_o'][0], norms, has_aux=True)
    wb = {'sb_w_qkv': w_qkv}
    for (n, ax), g in zip(SHARDED[1:], gathered):
        wb[n] = _layers(_merge_blocks(g, ax))
    rest_slots = {n: [None if (n, l) in late else a for l, a in enumerate(v)] for n, v in slots.items()}
    x_last, pull_rest = jax.vjp(lambda xx, ss, nn: _rest(xx, ss, nn, wb, q_tabs, kv_tabs), x_mid, rest_slots, norms)
    loss_part, dx_last, d_final = _loss_head(x_last, final_norm, loss_target[0])
    loss = lax.psum(loss_part[0, 0], ("x", "y", "c"))

    core = lax.axis_index("c").astype(jnp.int32).reshape(1)

    def pair_sums(units, grads, tag):
        halves, dims = [], []
        for (n, l), g in zip(units, grads):
            blocks = _split_blocks(g, layer_ax[n])
            two_d = (math.prod(blocks.shape[1:-1]), blocks.shape[-1])
            dims.append(two_d)
            halves.append(jnp.moveaxis(blocks.reshape((N_DEV // 2, 2) + two_d), 1, 0))
        staged = _pair_exchange(halves, "pair_grads_" + tag)
        sums = []
        for (n, l), h, st, (r, c) in zip(units, halves, staged, dims):
            rows = (N_DEV // 2) * r
            sums.append(_add_round(h.reshape(2, rows, c), st.reshape(rows, c), core,
                                   f"pair_sum_{n}_{l}").reshape(st.shape))
        return sums

    dx_mid, d_rest, d_norms_rest = pull_rest(dx_last)
    side['sums'] = pair_sums(early, [d_rest[n][l] for n, l in early], "early")
    dx, d_qkv0, d_o0, d_norms_first = pull_first(dx_mid)
    received = dict(zip(early, side['received']))
    received.update(zip(late, _chip_exchange(pair_sums(late, [d_qkv0, d_o0], "late"), "scatter_late")))
    d_norms = {n: d_norms_rest[n] + d_norms_first[n] for n in norms}
    d_norms['final_norm'] = d_final
    repl_parts = _all_gather([_pack([d_norms[n] for n in REPLICATED])], "gather_norm_grads")[0]

    results = {kind: {} for kind in ("grad", "delta", "new_m", "new_v")}
    for n in sharded_names:
        shp = weights[n].shape
        two_d = (math.prod(shp[:-1]), shp[-1])
        parts = jnp.concatenate([received[(n, l)] for l in range(len(slots[n]))], axis=1)
        res = _adamw_reduce(parts, weights[n].reshape(two_d), mom_m[n].reshape(two_d),
                            mom_v[n].reshape(two_d), "adamw_" + n)
        for kind, a in zip(results, res):
            results[kind][n] = a.reshape(shp)
    res = _adamw_reduce(repl_parts, _pack([weights[n] for n in REPLICATED]), _pack([mom_m[n] for n in REPLICATED]),
                        _pack([mom_v[n] for n in REPLICATED]), "adamw_replicated")
    for kind, a in zip(results, res):
        results[kind].update(zip(REPLICATED, _unpack(a, repl_shapes)))

    out = [loss, dx[None]]
    for kind in ("grad", "delta", "new_m", "new_v"):
        out += [results[kind][n] for n in WEIGHT_ORDER]
    return tuple(out)
```

```python
import math

import jax
import jax.numpy as jnp
from jax import lax
from jax.experimental import pallas as pl
from jax.experimental.pallas import tpu as pltpu

F32 = jnp.float32
BF16 = jnp.bfloat16
MESH = pl.DeviceIdType.MESH

N_DEV = 8
DEPTH = 4
N_A = 2
SB_HEADS = 16
SB_HD = 64
MLA_HEADS = 16
NOPE = 64
ROPE = 32
VDIM = 64
QK = NOPE + ROPE
KV_RANK = 256
DKV_PAD = 384
LANES = 128
CHUNK_SHIFT = 6
ROPE_THETA = 10000.0
EPS = 1e-6
SB_BLOCK = 256
MLA_BLOCK = 512
MLA_GROUP = 4
MLA_UNROLL = 4
PACK_COLS = 1024
PACK_ROW_ALIGN = 16
EXP_ZERO = -104.0
NEG_BIG = -1e30
LOG2_E = 1.4426950408889634
LN_2 = 0.6931471805599453
VMEM_ATT = 56 * 1024 * 1024
VMEM_MM = 48 * 1024 * 1024

ADAM_LR = 0.001
ADAM_B1 = 0.9
ADAM_B2 = 0.999
ADAM_EPS = 1e-08
ADAM_WD = 0.01
ADAM_STEP = 10

WEIGHT_ORDER = ['attn_norm', 'mlp_norm', 'sb_w_qkv', 'sb_w_o', 'kv_norm', 'mla_w_dkv', 'mla_kv_lat_norm',
                'mla_w_ukv', 'mla_w_dq', 'mla_q_lat_norm', 'mla_w_uq', 'mla_w_o', 'mlp_w1', 'mlp_w2', 'final_norm']
SHARDED = [('sb_w_qkv', 2), ('sb_w_o', 1), ('mla_w_dkv', 0), ('mla_w_ukv', 1), ('mla_w_dq', 1),
           ('mla_w_uq', 2), ('mla_w_o', 1), ('mlp_w1', 2), ('mlp_w2', 1)]
REPLICATED = ['attn_norm', 'mlp_norm', 'kv_norm', 'mla_kv_lat_norm', 'mla_q_lat_norm', 'final_norm']


def _tile(dim, prefs=(512, 384, 256, 128, 64, 32, 16, 8)):
    for t in prefs:
        if dim % t == 0:
            return t
    return dim


def _dot(a, b, dims):
    return lax.dot_general(a, b, (dims, ((), ())), preferred_element_type=F32)


NN = ((1,), (0,))
NT = ((1,), (1,))
TN = ((0,), (0,))


def _gather_phases(x_refs, out_refs, send_sems, recv_sems, local_sems):
    n_t = len(x_refs)
    x, y, c = lax.axis_index("x"), lax.axis_index("y"), lax.axis_index("c")
    me, sibling = (x, y, c), (x, y, 1 - c)
    chips = [(1 - x, y), (x, 1 - y), (1 - x, 1 - y)]

    def slot(t, px, py, pc):
        return out_refs[t].at[4 * px + 2 * py + pc]

    def copy(t, k, block, to, src=None):
        return pltpu.make_async_remote_copy(
            src_ref=slot(t, *block) if src is None else src, dst_ref=slot(t, *block),
            send_sem=send_sems.at[7 * t + k], recv_sem=recv_sems.at[7 * t + k],
            device_id=to, device_id_type=MESH)

    def mine():
        return [pltpu.make_async_copy(x_refs[t], slot(t, *me), local_sems.at[t]) for t in range(n_t)]

    def first():
        out = []
        for t in range(n_t):
            out.append(copy(t, 0, me, sibling, src=x_refs[t]))
            out += [copy(t, 1 + j, me, (*chip, c), src=x_refs[t]) for j, chip in enumerate(chips)]
        return out

    def passed():
        return [copy(t, 4 + j, (*chip, c), sibling) for t in range(n_t) for j, chip in enumerate(chips)]

    def start():
        for cp in mine() + first():
            cp.start()

    def forward(which=None):
        onward = passed()
        for t in range(n_t) if which is None else which:
            for j, chip in enumerate(chips):
                copy(t, 1 + j, (*chip, c), me).wait_recv()
                onward[3 * t + j].start()

    def finish():
        for t in range(n_t):
            copy(t, 0, sibling, me).wait_recv()
            for j, chip in enumerate(chips):
                copy(t, 4 + j, (*chip, 1 - c), me).wait_recv()
        for cp in first() + passed():
            cp.wait_send()
        for cp in mine():
            cp.wait()

    return start, forward, finish


def _gather_scratch(n_t):
    return [pltpu.SemaphoreType.DMA((7 * n_t,)), pltpu.SemaphoreType.DMA((7 * n_t,)), pltpu.SemaphoreType.DMA((n_t,))]


def _all_gather(shards, name):
    n_t = len(shards)

    def body(*refs):
        for phase in _gather_phases(refs[:n_t], refs[n_t:2 * n_t], *refs[2 * n_t:]):
            phase()

    any_spec = pl.BlockSpec(memory_space=pl.ANY)
    return pl.pallas_call(
        body, name=name,
        out_shape=[jax.ShapeDtypeStruct((N_DEV,) + tuple(s.shape), s.dtype) for s in shards],
        in_specs=[any_spec] * n_t, out_specs=[any_spec] * n_t,
        scratch_shapes=_gather_scratch(n_t),
    )(*shards)


def _pair_exchange(xs, name):
    n_t = len(xs)

    def body(*refs):
        x_refs, out_refs = refs[:n_t], refs[n_t:2 * n_t]
        send_sems, recv_sems = refs[2 * n_t:]
        x, y, c = lax.axis_index("x"), lax.axis_index("y"), lax.axis_index("c")
        copies = [pltpu.make_async_remote_copy(
            src_ref=x_refs[t].at[1 - c], dst_ref=out_refs[t], send_sem=send_sems.at[t], recv_sem=recv_sems.at[t],
            device_id=(x, y, 1 - c), device_id_type=MESH) for t in range(n_t)]
        for cp in copies:
            cp.start()
        for cp in copies:
            cp.wait()

    any_spec = pl.BlockSpec(memory_space=pl.ANY)
    return pl.pallas_call(
        body, name=name,
        out_shape=[jax.ShapeDtypeStruct(x.shape[1:], x.dtype) for x in xs],
        in_specs=[any_spec] * n_t, out_specs=[any_spec] * n_t,
        scratch_shapes=[pltpu.SemaphoreType.DMA((n_t,)), pltpu.SemaphoreType.DMA((n_t,))],
    )(*xs)


def _exchange_phases(p_refs, out_refs, send_sems, recv_sems, local_sems):
    n_t = len(p_refs)
    mx, my, mc = lax.axis_index("x"), lax.axis_index("y"), lax.axis_index("c")
    me = 2 * mx + my

    def mine():
        return [pltpu.make_async_copy(p_refs[t].at[me], out_refs[t].at[me], local_sems.at[t]) for t in range(n_t)]

    def copies():
        out = []
        for k in range(1, 4):
            px = 1 - mx if (k >> 1) & 1 else mx
            py = 1 - my if k & 1 else my
            peer = 2 * px + py
            for t in range(n_t):
                out.append(pltpu.make_async_remote_copy(
                    src_ref=p_refs[t].at[peer], dst_ref=out_refs[t].at[me],
                    send_sem=send_sems.at[3 * t + k - 1], recv_sem=recv_sems.at[3 * t + k - 1],
                    device_id=(px, py, mc), device_id_type=MESH))
        return out

    def start():
        for cp in mine() + copies():
            cp.start()

    def finish():
        for cp in copies():
            cp.wait_send()
        for cp in copies():
            cp.wait_recv()
        for cp in mine():
            cp.wait()

    return start, finish


def _exchange_scratch(n_t):
    return [pltpu.SemaphoreType.DMA((3 * n_t,)), pltpu.SemaphoreType.DMA((3 * n_t,)), pltpu.SemaphoreType.DMA((n_t,))]


def _chip_exchange(ps, name):
    n_t = len(ps)

    def body(*refs):
        for phase in _exchange_phases(refs[:n_t], refs[n_t:2 * n_t], *refs[2 * n_t:]):
            phase()

    any_spec = pl.BlockSpec(memory_space=pl.ANY)
    return pl.pallas_call(
        body, name=name,
        out_shape=[jax.ShapeDtypeStruct(p.shape, p.dtype) for p in ps],
        in_specs=[any_spec] * n_t, out_specs=[any_spec] * n_t,
        scratch_shapes=_exchange_scratch(n_t),
    )(*ps)


def _matmul(a, b, *, ta=False, tb=False, out_dtypes=(F32,), epilogue=None, extras=(), name):
    if ta:
        kdim, m = a.shape
    else:
        m, kdim = a.shape
    if tb:
        n, kb = b.shape
    else:
        kb, n = b.shape
    assert kdim == kb, (a.shape, b.shape, ta, tb)
    big = (1024, 768, 512, 384, 256, 128, 64, 32, 16, 8)
    tm, tn = _tile(m, big), _tile(n, big)
    tk = kdim if kdim <= 2048 else _tile(kdim, (2048, 1024, 512, 256, 128))
    nk = kdim // tk
    n_extra, n_out = len(extras), len(out_dtypes)
    a_spec = pl.BlockSpec((tk, tm), lambda i, j, k: (k, i)) if ta else pl.BlockSpec((tm, tk), lambda i, j, k: (i, k))
    b_spec = pl.BlockSpec((tn, tk), lambda i, j, k: (j, k)) if tb else pl.BlockSpec((tk, tn), lambda i, j, k: (k, j))
    tile_spec = pl.BlockSpec((tm, tn), lambda i, j, k: (i, j))
    dims = ((0,) if ta else (1,), (1,) if tb else (0,))

    def finish(acc, extra_refs, out_refs):
        outs = (acc,) if epilogue is None else epilogue(acc, *[r[...] for r in extra_refs])
        for o_ref, o in zip(out_refs, outs):
            o_ref[...] = o.astype(o_ref.dtype)

    def body_one(a_ref, b_ref, *rest):
        acc = _dot(a_ref[...].astype(BF16), b_ref[...].astype(BF16), dims)
        finish(acc, rest[:n_extra], rest[n_extra:n_extra + n_out])

    def body_acc(a_ref, b_ref, *rest):
        acc_ref = rest[-1]
        k = pl.program_id(2)

        @pl.when(k == 0)
        def _():
            acc_ref[...] = jnp.zeros_like(acc_ref)

        acc_ref[...] += _dot(a_ref[...].astype(BF16), b_ref[...].astype(BF16), dims)

        @pl.when(k == nk - 1)
        def _():
            finish(acc_ref[...], rest[:n_extra], rest[n_extra:n_extra + n_out])

    return pl.pallas_call(
        body_one if nk == 1 else body_acc, name=name, grid=(m // tm, n // tn, nk),
        in_specs=[a_spec, b_spec] + [tile_spec] * n_extra,
        out_specs=[tile_spec] * n_out,
        out_shape=[jax.ShapeDtypeStruct((m, n), dt) for dt in out_dtypes],
        scratch_shapes=[] if nk == 1 else [pltpu.VMEM((tm, tn), F32)],
        compiler_params=pltpu.CompilerParams(
            dimension_semantics=("parallel", "parallel", "arbitrary"), vmem_limit_bytes=VMEM_MM),
    )(a, b, *extras)


def _rms_fwd(x, g, name):
    m, d = x.shape
    tm = _tile(m, (512, 256, 128, 64, 32, 16, 8))

    def body(x_ref, g_ref, y_ref):
        xv = x_ref[...]
        r = lax.rsqrt(jnp.mean(xv * xv, axis=-1, keepdims=True) + EPS)
        y_ref[...] = (xv * r * g_ref[...]).astype(y_ref.dtype)

    return pl.pallas_call(
        body, name=name, grid=(m // tm,),
        in_specs=[pl.BlockSpec((tm, d), lambda i: (i, 0)), pl.BlockSpec((1, d), lambda i: (0, 0))],
        out_specs=pl.BlockSpec((tm, d), lambda i: (i, 0)),
        out_shape=jax.ShapeDtypeStruct((m, d), BF16),
        compiler_params=pltpu.CompilerParams(dimension_semantics=("parallel",)),
    )(x, g.reshape(1, d))


def _rms_bwd(x, g, dy, name, res=None):
    m, d = x.shape
    tm = _tile(m, (512, 256, 128, 64, 32, 16, 8))
    has_res = res is not None

    def body(x_ref, g_ref, dy_ref, *rest):
        dx_ref, dg_ref = rest[-2:]
        xv = x_ref[...]
        dyv = dy_ref[...]
        r = lax.rsqrt(jnp.mean(xv * xv, axis=-1, keepdims=True) + EPS)
        xh = xv * r
        t = dyv * g_ref[...]
        dx = r * (t - xh * jnp.mean(t * xh, axis=-1, keepdims=True))
        dx_ref[...] = dx + rest[0][...] if has_res else dx

        @pl.when(pl.program_id(0) == 0)
        def _():
            dg_ref[...] = jnp.zeros_like(dg_ref)

        dg_ref[...] += jnp.sum(dyv * xh, axis=0, keepdims=True)

    row_spec = pl.BlockSpec((tm, d), lambda i: (i, 0))
    vec_spec = pl.BlockSpec((1, d), lambda i: (0, 0))
    dx, dg = pl.pallas_call(
        body, name=name, grid=(m // tm,),
        in_specs=[row_spec, vec_spec, row_spec] + ([row_spec] if has_res else []),
        out_specs=[row_spec, vec_spec],
        out_shape=[jax.ShapeDtypeStruct((m, d), F32), jax.ShapeDtypeStruct((1, d), F32)],
        compiler_params=pltpu.CompilerParams(dimension_semantics=("arbitrary",)),
    )(x, g.reshape(1, d), dy, *((res,) if has_res else ()))
    return dx, dg.reshape(d)


def _loss_head(x, g, target):
    m, d = x.shape
    tm = _tile(m, (512, 256, 128, 64, 32, 16, 8))

    def body(x_ref, g_ref, t_ref, loss_ref, dx_ref, dg_ref):
        xv = x_ref[...]
        gv = g_ref[...]
        r = lax.rsqrt(jnp.mean(xv * xv, axis=-1, keepdims=True) + EPS)
        xh = xv * r
        err = xh * gv - t_ref[...]
        row_loss = jnp.mean(err * err, axis=-1, keepdims=True)
        dyv = err * (1.0 / d)
        t = dyv * gv
        dx_ref[...] = r * (t - xh * jnp.mean(t * xh, axis=-1, keepdims=True))

        @pl.when(pl.program_id(0) == 0)
        def _():
            dg_ref[...] = jnp.zeros_like(dg_ref)
            loss_ref[...] = jnp.zeros_like(loss_ref)

        dg_ref[...] += jnp.sum(dyv * xh, axis=0, keepdims=True)
        loss_ref[...] += 0.5 * jnp.sum(row_loss, axis=0, keepdims=True)

    loss, dx, dg = pl.pallas_call(
        body, name="loss_head", grid=(m // tm,),
        in_specs=[pl.BlockSpec((tm, d), lambda i: (i, 0)), pl.BlockSpec((1, d), lambda i: (0, 0)),
                  pl.BlockSpec((tm, d), lambda i: (i, 0))],
        out_specs=[pl.BlockSpec((1, 1), lambda i: (0, 0)), pl.BlockSpec((tm, d), lambda i: (i, 0)),
                   pl.BlockSpec((1, d), lambda i: (0, 0))],
        out_shape=[jax.ShapeDtypeStruct((1, 1), F32), jax.ShapeDtypeStruct((m, d), F32),
                   jax.ShapeDtypeStruct((1, d), F32)],
        compiler_params=pltpu.CompilerParams(dimension_semantics=("arbitrary",)),
    )(x, g.reshape(1, d), target)
    return loss, dx, dg.reshape(d)


def _rope_apply(x, tabs, transpose, name):
    m, w = x.shape
    wt = tabs[0].shape[1]
    reps = w // wt
    half = ROPE // 2
    tm = _tile(m, (256, 128, 64, 32, 16, 8))

    def body(x_ref, c_ref, a_ref, b_ref, y_ref):
        xv = x_ref[...]

        def wide(t_ref):
            t = t_ref[...]
            return t if reps == 1 else jnp.concatenate([t] * reps, axis=1)

        c, a, b = wide(c_ref), wide(a_ref), wide(b_ref)
        if transpose:
            y = xv * c + pltpu.roll(xv * a, half, 1) + pltpu.roll(xv * b, w - half, 1)
        else:
            y = xv * c + pltpu.roll(xv, w - half, 1) * a + pltpu.roll(xv, half, 1) * b
        y_ref[...] = y

    x_spec = pl.BlockSpec((tm, w), lambda i: (i, 0))
    t_spec = pl.BlockSpec((tm, wt), lambda i: (i, 0))
    return pl.pallas_call(
        body, name=name, grid=(m // tm,),
        in_specs=[x_spec, t_spec, t_spec, t_spec], out_specs=x_spec,
        out_shape=jax.ShapeDtypeStruct((m, w), F32),
        compiler_params=pltpu.CompilerParams(dimension_semantics=("parallel",)),
    )(x, *tabs)


def _log_sigmoid_pair(z):
    a = jnp.minimum(z, 0.0) - jnp.log(1.0 + jnp.exp(-jnp.abs(z)))
    return a, a - z


def _split_bf16(x):
    hi = x.astype(BF16)
    return hi, (x - hi.astype(F32)).astype(BF16)


def _wide(v, width):
    return v if width == LANES else jnp.concatenate([v] * (width // LANES), axis=1)


def _lanes(col):
    return jnp.broadcast_to(col, (col.shape[0], LANES))


def _rope_head(x, c, a, b):
    half = ROPE // 2
    return x * c + pltpu.roll(x, LANES - half, 1) * a + pltpu.roll(x, half, 1) * b


def _rope_head_t(dy, c, a, b):
    half = ROPE // 2
    return dy * c + pltpu.roll(dy * a, half, 1) + pltpu.roll(dy * b, LANES - half, 1)


def _sb_weights(qh, k2, valid):
    a, b = _log_sigmoid_pair(_dot(qh, k2, NT))
    return a, b if valid is None else jnp.where(valid, b, 0.0)


def _sb_fwd(qkv, shards=()):
    s = qkv.shape[0]
    t = min(SB_BLOCK, s)
    nq = s // t
    npair = SB_HEADS // 2
    scale = SB_HD ** -0.5

    n_t = len(shards)

    def body(q_ref, k_ref, v_ref, *rest):
        x_refs, rest = rest[:n_t], rest[n_t:]
        o_ref, bta_ref, btb_ref, js_ref = rest[:4]
        g_refs, rest = rest[4:4 + n_t], rest[4 + n_t:]
        acc_ref, ra_ref, rb_ref = rest[:3]
        p, i = pl.program_id(0), pl.program_id(1)
        if n_t:
            start, forward, finish = _gather_phases(x_refs, g_refs, *rest[3:])
            pl.when(jnp.logical_and(p == 0, i == 0))(start)
            pl.when(jnp.logical_and(p == (5 * npair) // 8, i == 0))(lambda: forward(range(n_t - 1)))
            pl.when(jnp.logical_and(p == (7 * npair) // 8, i == 0))(lambda: forward([n_t - 1]))
        q2 = q_ref[...] * scale
        first = lax.broadcasted_iota(jnp.int32, (t, LANES), 1) < SB_HD
        heads = (jnp.where(first, q2, jnp.zeros_like(q2)), jnp.where(first, jnp.zeros_like(q2), q2))
        row = lax.broadcasted_iota(jnp.int32, (t, t), 0)
        col = lax.broadcasted_iota(jnp.int32, (t, t), 1)
        later = jnp.where(row > col, 1.0, 0.0).astype(BF16)
        acc_ref[...] = jnp.zeros_like(acc_ref)
        ra_ref[...] = jnp.zeros_like(ra_ref)
        rb_ref[...] = jnp.zeros_like(rb_ref)

        def block(j, diagonal):
            off = pl.multiple_of(j * t, t)
            k2 = k_ref[pl.ds(off, t), :]
            v2 = v_ref[pl.ds(off, t), :]
            valid = col < row if diagonal else None
            outs, rmax = [], None
            for qh, r_ref in zip(heads, (ra_ref, rb_ref)):
                a, b = _sb_weights(qh, k2, valid)
                bh, bl = _split_bf16(b)
                inner = _dot(bh, later, NN) + _dot(bl, later, NN)
                r = r_ref[...]
                w = jnp.exp(a + inner + _wide(r, t))
                if diagonal:
                    w = jnp.where(valid, w, 0.0)
                outs.append(_dot(w.astype(BF16), v2, NN))
                rn = r + jnp.sum(b, axis=1, keepdims=True)
                r_ref[...] = rn
                rmax = jnp.max(rn) if rmax is None else jnp.maximum(rmax, jnp.max(rn))
            acc_ref[...] += jnp.where(first, outs[0], outs[1])
            return rmax

        def cond(carry):
            j, rmax = carry
            return jnp.logical_and(j >= 0, rmax > EXP_ZERO)

        def step(carry):
            j, _ = carry
            return j - 1, block(j, False)

        def first_two():
            block(i, True)
            return block(i - 1, False)

        rmax = lax.cond(i >= 1, first_two, lambda: block(i, True))
        jend, _ = lax.while_loop(cond, step, (jnp.maximum(i - 2, -1), rmax))
        o_ref[...] = acc_ref[...].astype(o_ref.dtype)
        bta_ref[0] = ra_ref[...][:, :1]
        btb_ref[0] = rb_ref[...][:, :1]
        js_ref[p, i] = (jend + 1).astype(F32)
        if n_t:
            pl.when(jnp.logical_and(p == npair - 1, i == nq - 1))(finish)

    stat = pl.BlockSpec((1, t, 1), lambda p, i: (p, i, 0))
    any_spec = pl.BlockSpec(memory_space=pl.ANY)
    return pl.pallas_call(
        body, name="sb_attn_fwd_gather" if n_t else "sb_attn_fwd", grid=(npair, nq),
        in_specs=[pl.BlockSpec((t, LANES), lambda p, i: (i, p)),
                  pl.BlockSpec((s, LANES), lambda p, i: (0, npair + p)),
                  pl.BlockSpec((s, LANES), lambda p, i: (0, 2 * npair + p))] + [any_spec] * n_t,
        out_specs=[pl.BlockSpec((t, LANES), lambda p, i: (i, p)), stat, stat,
                   pl.BlockSpec(memory_space=pltpu.SMEM)] + [any_spec] * n_t,
        out_shape=[jax.ShapeDtypeStruct((s, SB_HEADS * SB_HD), BF16), jax.ShapeDtypeStruct((npair, s, 1), F32),
                   jax.ShapeDtypeStruct((npair, s, 1), F32), jax.ShapeDtypeStruct((npair, nq), F32)]
        + [jax.ShapeDtypeStruct((N_DEV,) + tuple(x.shape), x.dtype) for x in shards],
        scratch_shapes=[pltpu.VMEM((t, LANES), F32)] * 3 + (_gather_scratch(n_t) if n_t else []),
        compiler_params=pltpu.CompilerParams(
            dimension_semantics=("arbitrary", "arbitrary"), vmem_limit_bytes=VMEM_ATT),
    )(qkv, qkv, qkv, *shards)


def _sb_bwd(qkv, do, btot_a, btot_b, jstart, sums=()):
    s = qkv.shape[0]
    t = min(SB_BLOCK, s)
    nq = s // t
    npair = SB_HEADS // 2
    scale = SB_HD ** -0.5

    n_t = len(sums)

    def body(js_ref, q_ref, k_ref, v_ref, do_ref, bta_ref, btb_ref, *rest):
        s_refs, rest = rest[:n_t], rest[n_t:]
        dq_ref, dk_ref, dv_ref = rest[:3]
        r_refs, rest = rest[3:3 + n_t], rest[3 + n_t:]
        dq_acc, pa_ref, pb_ref, ga_ref, gb_ref, dk_acc, dv_acc = rest[:7]
        p, i = pl.program_id(0), pl.program_id(1)
        if n_t:
            start, finish = _exchange_phases(s_refs, r_refs, *rest[7:])
            pl.when(jnp.logical_and(p == 0, i == 0))(start)

        @pl.when(i == 0)
        def _():
            dk_acc[...] = jnp.zeros_like(dk_acc)
            dv_acc[...] = jnp.zeros_like(dv_acc)

        q2 = q_ref[...] * scale
        do2 = do_ref[...]
        first = lax.broadcasted_iota(jnp.int32, (t, LANES), 1) < SB_HD
        zero = jnp.zeros_like(q2)
        q_heads = (jnp.where(first, q2, zero), jnp.where(first, zero, q2))
        do_heads = (jnp.where(first, do2, zero), jnp.where(first, zero, do2))
        bts = (_wide(_lanes(bta_ref[0]), t), _wide(_lanes(btb_ref[0]), t))
        row = lax.broadcasted_iota(jnp.int32, (t, t), 0)
        col = lax.broadcasted_iota(jnp.int32, (t, t), 1)
        upto = jnp.where(row <= col, 1.0, 0.0).astype(BF16)
        before = jnp.where(row < col, 1.0, 0.0).astype(BF16)
        dq_acc[...] = jnp.zeros_like(dq_acc)
        for r in (pa_ref, pb_ref, ga_ref, gb_ref):
            r[...] = jnp.zeros_like(r)
        j0 = jnp.clip(js_ref[p, i].astype(jnp.int32), 0, i)

        def block(j, diagonal):
            off = pl.multiple_of(j * t, t)
            k2 = k_ref[pl.ds(off, t), :]
            v2 = v_ref[pl.ds(off, t), :]
            valid = col < row if diagonal else None
            dqs, dk2, dv2 = [], None, None
            for qh, doh, bt, p_ref, g_ref in zip(q_heads, do_heads, bts, (pa_ref, pb_ref), (ga_ref, gb_ref)):
                a, b = _sb_weights(qh, k2, valid)
                bh, bl = _split_bf16(b)
                pin = _dot(bh, upto, NN) + _dot(bl, upto, NN)
                surv = bt - (_wide(p_ref[...], t) + pin)
                w = jnp.exp(a + surv)
                if diagonal:
                    w = jnp.where(valid, w, 0.0)
                g = w * _dot(doh, v2, NT)
                gh, gl = _split_bf16(g)
                gsum = _wide(g_ref[...], t) + _dot(gh, before, NN) + _dot(gl, before, NN)
                beta = jnp.exp(a)
                dz = g * (1.0 - beta) - gsum * beta
                if diagonal:
                    dz = jnp.where(valid, dz, 0.0)
                dz = dz.astype(BF16)
                dqs.append(_dot(dz, k2, NN))
                dkh = _dot(dz, qh, TN)
                dvh = _dot(w.astype(BF16), doh, TN)
                dk2 = dkh if dk2 is None else dk2 + dkh
                dv2 = dvh if dv2 is None else dv2 + dvh
                p_ref[...] += jnp.sum(b, axis=1, keepdims=True)
                g_ref[...] += jnp.sum(g, axis=1, keepdims=True)
            dq_acc[...] += jnp.where(first, dqs[0], dqs[1])
            dk_acc[pl.ds(off, t), :] += dk2
            dv_acc[pl.ds(off, t), :] += dv2

        def step(j, carry):
            block(j, False)
            return carry

        lax.fori_loop(j0, jnp.maximum(i - 1, j0), step, 0)

        @pl.when(j0 < i)
        def _():
            block(i - 1, False)
            block(i, True)

        @pl.when(j0 >= i)
        def _():
            block(i, True)

        dq_ref[...] = (dq_acc[...] * scale).astype(dq_ref.dtype)

        @pl.when(i == nq - 1)
        def _():
            dk_ref[...] = dk_acc[...].astype(dk_ref.dtype)
            dv_ref[...] = dv_acc[...].astype(dv_ref.dtype)

        if n_t:
            pl.when(jnp.logical_and(p == npair - 1, i == nq - 1))(finish)

    blk = pl.BlockSpec((t, LANES), lambda p, i: (i, p))
    full = pl.BlockSpec((s, LANES), lambda p, i: (0, p))
    stat = pl.BlockSpec((1, t, 1), lambda p, i: (p, i, 0))
    vec = pltpu.VMEM((t, LANES), F32)
    any_spec = pl.BlockSpec(memory_space=pl.ANY)
    return pl.pallas_call(
        body, name="sb_attn_bwd_exchange" if n_t else "sb_attn_bwd", grid=(npair, nq),
        in_specs=[pl.BlockSpec(memory_space=pltpu.SMEM), blk,
                  pl.BlockSpec((s, LANES), lambda p, i: (0, npair + p)),
                  pl.BlockSpec((s, LANES), lambda p, i: (0, 2 * npair + p)), blk, stat, stat] + [any_spec] * n_t,
        out_specs=[blk, full, full] + [any_spec] * n_t,
        out_shape=[jax.ShapeDtypeStruct((s, SB_HEADS * SB_HD), BF16)] * 3
        + [jax.ShapeDtypeStruct(x.shape, x.dtype) for x in sums],
        scratch_shapes=[pltpu.VMEM((t, LANES), F32), vec, vec, vec, vec,
                        pltpu.VMEM((s, LANES), F32), pltpu.VMEM((s, LANES), F32)]
        + (_exchange_scratch(n_t) if n_t else []),
        compiler_params=pltpu.CompilerParams(
            dimension_semantics=("arbitrary", "arbitrary"), vmem_limit_bytes=VMEM_ATT),
    )(jstart, qkv, qkv, qkv, do, btot_a, btot_b, *sums)


def _make_sb_attention(side=None):
    @jax.custom_vjp
    def f(qkv, *shards):
        o, _, _, _, *gathered = _sb_fwd(qkv, shards)
        return (o, *gathered)

    def fwd(qkv, *shards):
        o, btot_a, btot_b, jstart, *gathered = _sb_fwd(qkv, shards)
        return (o, *gathered), (qkv, btot_a, btot_b, jstart, shards)

    def bwd(res, cts):
        qkv, btot_a, btot_b, jstart, shards = res
        sums = () if side is None else tuple(side['sums'])
        dq, dk, dv, *received = _sb_bwd(qkv, cts[0], btot_a, btot_b, jstart, sums)
        if side is not None:
            side['received'] = received
        return (jnp.concatenate([dq, dk, dv], axis=1),) + tuple(jnp.zeros_like(x) for x in shards)

    f.defvjp(fwd, bwd)
    return f


def sb_attention(qkv):
    return _make_sb_attention()(qkv)[0]


def _chunk_allowed(row, col):
    return (col >> CHUNK_SHIFT) <= (row >> CHUNK_SHIFT)


def _mla_fwd(q, k, vx, tabs):
    s = q.shape[0]
    h_ = q.shape[1] // LANES
    t = min(MLA_BLOCK, s)
    nq = s // t
    scale = QK ** -0.5
    hp = MLA_GROUP

    def body(q_ref, k_ref, v_ref, c_ref, a_ref, b_ref, o_ref, lse_ref, acc_ref, m_ref):
        i = pl.program_id(1)
        acc_ref[...] = jnp.zeros_like(acc_ref)
        m_ref[...] = jnp.full_like(m_ref, NEG_BIG)
        rot = (c_ref[...], a_ref[...], b_ref[...])
        qs = [_rope_head(q_ref[:, hh * LANES:(hh + 1) * LANES], *rot).astype(BF16) for hh in range(hp)]

        def tile(j, diagonal):
            off = pl.multiple_of(j * t, t)
            for hh in range(hp):
                lanes = slice(hh * LANES, (hh + 1) * LANES)
                kj = k_ref[pl.ds(off, t), lanes]
                vj = v_ref[pl.ds(off, t), lanes]
                sc = _dot(qs[hh], kj, NT) * (scale * LOG2_E)
                if diagonal:
                    row = lax.broadcasted_iota(jnp.int32, (t, t), 0)
                    col = lax.broadcasted_iota(jnp.int32, (t, t), 1)
                    sc = jnp.where(_chunk_allowed(row, col), sc, NEG_BIG)
                m_old = m_ref[hh]
                m_new = jnp.maximum(m_old, jnp.max(sc, axis=1, keepdims=True))
                p = jnp.exp2(sc - _wide(m_new, t))
                acc_ref[hh] = jnp.exp2(m_old - m_new) * acc_ref[hh] + _dot(p.astype(BF16), vj, NN)
                m_ref[hh] = m_new

        def step(jj, carry):
            tile(2 * jj, False)
            tile(2 * jj + 1, False)
            return carry

        lax.fori_loop(0, i // 2, step, 0)

        @pl.when(i % 2 == 1)
        def _():
            tile(i - 1, False)

        tile(i, True)
        first = lax.broadcasted_iota(jnp.int32, (t, LANES), 1) < VDIM
        outs = []
        for hh in range(hp):
            acc = acc_ref[hh]
            den = acc[:, VDIM:VDIM + 1]
            outs.append(acc / den)
            lse_ref[hh] = (m_ref[hh][:, :1] + jnp.log2(den)) * LN_2
        for pp in range(hp // 2):
            o_ref[:, pp * LANES:(pp + 1) * LANES] = jnp.where(
                first, outs[2 * pp], pltpu.roll(outs[2 * pp + 1], VDIM, 1))

    wide = hp * LANES
    return pl.pallas_call(
        body, name="mla_attn_fwd", grid=(h_ // hp, nq),
        in_specs=[pl.BlockSpec((t, wide), lambda g, i: (i, g)),
                  pl.BlockSpec((s, wide), lambda g, i: (0, g)),
                  pl.BlockSpec((s, wide), lambda g, i: (0, g))] + [pl.BlockSpec((t, LANES), lambda g, i: (i, 0))] * 3,
        out_specs=[pl.BlockSpec((t, hp * VDIM), lambda g, i: (i, g)),
                   pl.BlockSpec((hp, t, 1), lambda g, i: (g, i, 0))],
        out_shape=[jax.ShapeDtypeStruct((s, h_ * VDIM), F32), jax.ShapeDtypeStruct((h_, s, 1), F32)],
        scratch_shapes=[pltpu.VMEM((hp, t, LANES), F32), pltpu.VMEM((hp, t, LANES), F32)],
        compiler_params=pltpu.CompilerParams(
            dimension_semantics=("arbitrary", "arbitrary"), vmem_limit_bytes=VMEM_ATT),
    )(q, k, vx, *tabs)


def _mla_bwd(q, k, vx, tabs, o, lse, do):
    s = q.shape[0]
    h_ = q.shape[1] // LANES
    t = min(MLA_BLOCK, s)
    nq = s // t
    scale = QK ** -0.5

    def body(q_ref, k_ref, v_ref, c_ref, a_ref, b_ref, o_ref, lse_ref, do_ref, dq_ref, dk_ref, dv_ref, dq_acc):
        h, i = pl.program_id(0), pl.program_id(1)
        rot = (c_ref[...], a_ref[...], b_ref[...])

        @pl.when(i == 0)
        def _():
            dk_ref[...] = jnp.zeros_like(dk_ref)
            dv_ref[...] = jnp.zeros_like(dv_ref)

        qv = _rope_head(q_ref[...], *rot).astype(BF16)
        lane = lax.broadcasted_iota(jnp.int32, (t, LANES), 1)
        start = (h % 2) * VDIM
        mine = jnp.logical_and(lane >= start, lane < start + VDIM)
        do2 = do_ref[...]
        delta = _lanes(jnp.sum(jnp.where(mine, do2 * o_ref[...], 0.0), axis=1, keepdims=True))
        odd = (start + jnp.zeros_like(lane)) > 0
        do_head = jnp.where(odd, pltpu.roll(do2, VDIM, 1), do2)
        dov = jnp.where(lane < VDIM, do_head, 0.0).astype(BF16)
        lse = _lanes(lse_ref[0])
        dq_acc[...] = jnp.zeros_like(dq_acc)

        def tile(j, diagonal):
            off = pl.multiple_of(j * t, t)
            kj = k_ref[pl.ds(off, t), :]
            vj = v_ref[pl.ds(off, t), :]
            p = jnp.exp(_dot(qv, kj, NT) * scale - _wide(lse, t))
            if diagonal:
                row = lax.broadcasted_iota(jnp.int32, (t, t), 0)
                col = lax.broadcasted_iota(jnp.int32, (t, t), 1)
                p = jnp.where(_chunk_allowed(row, col), p, 0.0)
            dp = _dot(dov, vj, NT)
            ds = (p * (dp - _wide(delta, t)) * scale).astype(BF16)
            dq_acc[...] += _dot(ds, kj, NN)
            dk_ref[pl.ds(off, t), :] += _dot(ds, qv, TN)
            dv_ref[pl.ds(off, t), :] += _dot(p.astype(BF16), dov, TN)

        def step(jj, carry):
            for u in range(MLA_UNROLL):
                tile(MLA_UNROLL * jj + u, False)
            return carry

        def single(j, carry):
            tile(j, False)
            return carry

        lax.fori_loop(0, i // MLA_UNROLL, step, 0)
        lax.fori_loop(i - i % MLA_UNROLL, i, single, 0)
        tile(i, True)
        dq_ref[...] = _rope_head_t(dq_acc[...], *rot)

    blk = pl.BlockSpec((t, LANES), lambda h, i: (i, h))
    full = pl.BlockSpec((s, LANES), lambda h, i: (0, h))
    pair = pl.BlockSpec((t, LANES), lambda h, i: (i, h // 2))
    stat = pl.BlockSpec((1, t, 1), lambda h, i: (h, i, 0))
    return pl.pallas_call(
        body, name="mla_attn_bwd", grid=(h_, nq),
        in_specs=[blk, full, full] + [pl.BlockSpec((t, LANES), lambda h, i: (i, 0))] * 3 + [pair, stat, pair],
        out_specs=[blk, full, full],
        out_shape=[jax.ShapeDtypeStruct((s, h_ * LANES), F32)] * 3,
        scratch_shapes=[pltpu.VMEM((t, LANES), F32)],
        compiler_params=pltpu.CompilerParams(
            dimension_semantics=("arbitrary", "arbitrary"), vmem_limit_bytes=VMEM_ATT),
    )(q, k, vx, *tabs, o, lse, do)


def _assemble_kv(kvp, kr):
    half = kvp.shape[1] // 2
    ones = jnp.tile((jnp.arange(LANES) == VDIM).astype(F32), MLA_HEADS)
    k = kvp[:, :half] + jnp.tile(kr, (1, MLA_HEADS))
    return k.astype(BF16), (kvp[:, half:] + ones).astype(BF16)


@jax.custom_vjp
def mla_attention(q, kvp, kr, c, a, b):
    return _mla_fwd(q, *_assemble_kv(kvp, kr), (c, a, b))[0]


def _mla_attention_fwd(q, kvp, kr, c, a, b):
    kb, vb = _assemble_kv(kvp, kr)
    o, lse = _mla_fwd(q, kb, vb, (c, a, b))
    return o, (q, kb, vb, (c, a, b), o, lse)


def _mla_attention_bwd(res, do):
    q, kb, vb, tabs, o, lse = res
    dq, dk, dvx = _mla_bwd(q, kb, vb, tabs, o, lse, do)
    dkr = dk.reshape(dk.shape[0], MLA_HEADS, LANES).sum(axis=1)
    return (dq, jnp.concatenate([dk, dvx], axis=1), dkr) + tuple(jnp.zeros_like(t) for t in tabs)


mla_attention.defvjp(_mla_attention_fwd, _mla_attention_bwd)


def _add_tile(acc, res):
    return (acc + res,)


def _make_linear_res(tag, a_dtype=F32):
    def forward(x, a, slot, wb):
        ab = a.astype(BF16)
        y = _matmul(ab, wb, epilogue=_add_tile, extras=(x,), name=tag + "_fwd")[0]
        return y, (ab, wb)

    @jax.custom_vjp
    def f(x, a, slot, wb):
        return forward(x, a, slot, wb)[0]

    def bwd(res, dy):
        ab, wb = res
        dyb = dy.astype(BF16)
        da = _matmul(dyb, wb, tb=True, out_dtypes=(a_dtype,), name=tag + "_da")[0]
        dw = _matmul(ab, dyb, ta=True, name=tag + "_dw")[0]
        return dy, da, dw, jnp.zeros_like(wb)

    f.defvjp(forward, bwd)
    return f


def _make_norm_linear(tag, out_dtype=F32):
    def forward(x, g, slot, wb):
        hb = _rms_fwd(x, g, tag + "_norm")
        y = _matmul(hb, wb, out_dtypes=(out_dtype,), name=tag + "_fwd")[0]
        return y, (x, g, wb, hb)

    @jax.custom_vjp
    def f(x, g, slot, wb):
        return forward(x, g, slot, wb)[0]

    def bwd(res, dy):
        x, g, wb, hb = res
        dyb = dy.astype(BF16)
        dh = _matmul(dyb, wb, tb=True, name=tag + "_dh")[0]
        dw = _matmul(hb, dyb, ta=True, name=tag + "_dw")[0]
        dx, dg = _rms_bwd(x, g, dh, tag + "_norm_bwd")
        return dx, dg, dw, jnp.zeros_like(wb)

    f.defvjp(forward, bwd)
    return f


def _relu2_fwd(acc):
    r = jnp.maximum(acc, 0.0)
    return acc, r * r


def _relu2_bwd(acc, u):
    return (acc * (2.0 * jnp.maximum(u.astype(F32), 0.0)),)


def _make_mlp_res(tag):
    def forward(x, g, slot1, slot2, w1b, w2b):
        hb = _rms_fwd(x, g, tag + "_norm")
        u, act = _matmul(hb, w1b, out_dtypes=(BF16, BF16), epilogue=_relu2_fwd, name=tag + "_up")
        y = _matmul(act, w2b, epilogue=_add_tile, extras=(x,), name=tag + "_down")[0]
        return y, (x, g, w1b, w2b, hb, u, act)

    @jax.custom_vjp
    def f(x, g, slot1, slot2, w1b, w2b):
        return forward(x, g, slot1, slot2, w1b, w2b)[0]

    def bwd(res, dy):
        x, g, w1b, w2b, hb, u, act = res
        dyb = dy.astype(BF16)
        du = _matmul(dyb, w2b, tb=True, out_dtypes=(BF16,), epilogue=_relu2_bwd, extras=(u,), name=tag + "_du")[0]
        dw2 = _matmul(act, dyb, ta=True, name=tag + "_dw2")[0]
        dw1 = _matmul(hb, du, ta=True, name=tag + "_dw1")[0]
        dh = _matmul(du, w1b, tb=True, name=tag + "_dh")[0]
        dx, dg = _rms_bwd(x, g, dh, tag + "_norm_bwd", res=dy)
        return dx, dg, dw1, dw2, jnp.zeros_like(w1b), jnp.zeros_like(w2b)

    f.defvjp(forward, bwd)
    return f


def _make_rope(tag):
    @jax.custom_vjp
    def f(x, c, a, b):
        return _rope_apply(x, (c, a, b), False, tag + "_fwd")

    def fwd(x, c, a, b):
        return _rope_apply(x, (c, a, b), False, tag + "_fwd"), (c, a, b)

    def bwd(res, dy):
        c, a, b = res
        return _rope_apply(dy, (c, a, b), True, tag + "_bwd"), jnp.zeros_like(c), jnp.zeros_like(a), jnp.zeros_like(b)

    f.defvjp(fwd, bwd)
    return f


def _rope_tables(positions):
    half = ROPE // 2
    inv_freq = ROPE_THETA ** (-jnp.arange(0, ROPE, 2, dtype=F32) / ROPE)
    ang = positions.astype(F32)[:, None] * inv_freq
    cos, sin = jnp.cos(ang), jnp.sin(ang)
    s = positions.shape[0]
    one = lambda n: jnp.ones((s, n), F32)
    zero = lambda n: jnp.zeros((s, n), F32)

    def tables(before, after):
        return (jnp.concatenate([one(before), cos, cos, one(after)], axis=1),
                jnp.concatenate([zero(before), -sin, zero(half + after)], axis=1),
                jnp.concatenate([zero(before + half), sin, zero(after)], axis=1))

    return tables(NOPE, LANES - QK), tables(KV_RANK, DKV_PAD - KV_RANK - ROPE)


def _pad_heads(w, per):
    lead = w.shape[:-1]
    w = jnp.pad(w.reshape(lead + (MLA_HEADS, per)), [(0, 0)] * len(lead) + [(0, 0), (0, LANES - per)])
    return w.reshape(lead + (MLA_HEADS * LANES,))


def _split_kv_heads(w):
    w3 = w.reshape(w.shape[0], MLA_HEADS, NOPE + VDIM)
    return jnp.concatenate([_pad_heads(w3[..., :NOPE].reshape(w.shape[0], -1), NOPE),
                            _pad_heads(w3[..., NOPE:].reshape(w.shape[0], -1), VDIM)], axis=1)


def _layers(full):
    return [full[l] for l in range(full.shape[0])] if full.ndim == 3 else [full]


def _first_attention(x, slot_qkv, slot_o, norms, w_qkv0, shards, side):
    qkv = _make_norm_linear("sb0_qkv", BF16)(x, norms['attn_norm'][0], slot_qkv, w_qkv0)
    o, *gathered = _make_sb_attention(side)(qkv, *shards)
    w_o = _layers(_merge_blocks(gathered[0], SHARDED[1][1]))[0]
    return _make_linear_res("sb0_o", BF16)(x, o, slot_o, w_o), gathered


def _rest(x, slots, norms, wb, q_tabs, kv_tabs):
    kvp = kr = None
    for layer in range(DEPTH):
        if 0 < layer < N_A:
            qkv = _make_norm_linear(f"sb{layer}_qkv", BF16)(
                x, norms['attn_norm'][layer], slots['sb_w_qkv'][layer], wb['sb_w_qkv'][layer])
            x = _make_linear_res(f"sb{layer}_o", BF16)(
                x, sb_attention(qkv), slots['sb_w_o'][layer], wb['sb_w_o'][layer])
        elif layer >= N_A:
            j = layer - N_A
            if j == 0:
                pad = ((0, 0), (0, DKV_PAD - KV_RANK - ROPE))
                down = _make_norm_linear("kv_down")(
                    x, norms['kv_norm'], jnp.pad(slots['mla_w_dkv'][0], pad), jnp.pad(wb['mla_w_dkv'][0], pad))
                kvp = _make_norm_linear("kv_up")(
                    down[:, :KV_RANK], norms['mla_kv_lat_norm'],
                    _split_kv_heads(slots['mla_w_ukv'][0]), _split_kv_heads(wb['mla_w_ukv'][0]))
                k_rope = _make_rope("rope_k")(down, *kv_tabs)[:, KV_RANK:KV_RANK + ROPE]
                kr = jnp.pad(k_rope, ((0, 0), (NOPE, LANES - QK)))
            c_q = _make_norm_linear(f"mla{j}_dq")(
                x, norms['attn_norm'][layer], slots['mla_w_dq'][j], wb['mla_w_dq'][j])
            q = _make_norm_linear(f"mla{j}_uq")(
                c_q, norms['mla_q_lat_norm'][j], _pad_heads(slots['mla_w_uq'][j], QK), _pad_heads(wb['mla_w_uq'][j], QK))
            o = mla_attention(q, kvp, kr, *q_tabs)
            x = _make_linear_res(f"mla{j}_o")(x, o, slots['mla_w_o'][j], wb['mla_w_o'][j])
        x = _make_mlp_res(f"mlp{layer}")(
            x, norms['mlp_norm'][layer], slots['mlp_w1'][layer], slots['mlp_w2'][layer],
            wb['mlp_w1'][layer], wb['mlp_w2'][layer])
    return x


def _merge_blocks(gathered, ax):
    shp = gathered.shape[1:]
    return jnp.moveaxis(gathered, 0, ax).reshape(shp[:ax] + (N_DEV * shp[ax],) + shp[ax + 1:])


def _split_blocks(full, ax):
    shp = full.shape
    return jnp.moveaxis(full.reshape(shp[:ax] + (N_DEV, shp[ax] // N_DEV) + shp[ax + 1:]), ax, 0)


def _pack(parts):
    flat = jnp.concatenate([p.reshape(-1) for p in parts])
    rows = -(-flat.shape[0] // PACK_COLS)
    rows = -(-rows // PACK_ROW_ALIGN) * PACK_ROW_ALIGN
    return jnp.pad(flat, (0, rows * PACK_COLS - flat.shape[0])).reshape(rows, PACK_COLS)


def _unpack(packed, shapes):
    flat = packed.reshape(-1)
    out, off = [], 0
    for shp in shapes:
        n = math.prod(shp)
        out.append(flat[off:off + n].reshape(shp))
        off += n
    return out


def _add_round(x, stage, core, name):
    _, r, c = x.shape
    tr = _tile(r, (256, 128, 64, 32, 16, 8))

    def body(core_ref, x_ref, s_ref, o_ref):
        o_ref[...] = (x_ref[0] + s_ref[...]).astype(o_ref.dtype)

    return pl.pallas_call(
        body, name=name,
        grid_spec=pltpu.PrefetchScalarGridSpec(
            num_scalar_prefetch=1, grid=(r // tr,),
            in_specs=[pl.BlockSpec((1, tr, c), lambda i, core_ref: (core_ref[0], i, 0)),
                      pl.BlockSpec((tr, c), lambda i, core_ref: (i, 0))],
            out_specs=pl.BlockSpec((tr, c), lambda i, core_ref: (i, 0))),
        out_shape=jax.ShapeDtypeStruct((r, c), BF16),
        compiler_params=pltpu.CompilerParams(dimension_semantics=("parallel",)),
    )(core, x, stage)


def _adamw_reduce(parts, w, m, v, name):
    n_parts, r, c = parts.shape
    tr = _tile(r, (128, 64, 32, 16, 8))
    bias1 = 1.0 - ADAM_B1 ** ADAM_STEP
    bias2 = 1.0 - ADAM_B2 ** ADAM_STEP

    def body(p_ref, w_ref, m_ref, v_ref, g_ref, d_ref, nm_ref, nv_ref):
        g = p_ref[0].astype(F32)
        for s in range(1, n_parts):
            g = g + p_ref[s].astype(F32)
        mn = ADAM_B1 * m_ref[...] + (1.0 - ADAM_B1) * g
        vn = ADAM_B2 * v_ref[...] + (1.0 - ADAM_B2) * (g * g)
        m_hat = mn / bias1
        v_hat = vn / bias2
        g_ref[...] = g
        d_ref[...] = -ADAM_LR * (m_hat / (jnp.sqrt(v_hat) + ADAM_EPS) + ADAM_WD * w_ref[...])
        nm_ref[...] = mn
        nv_ref[...] = vn

    blk = pl.BlockSpec((tr, c), lambda i: (i, 0))
    return pl.pallas_call(
        body, name=name, grid=(r // tr,),
        in_specs=[pl.BlockSpec((n_parts, tr, c), lambda i: (0, i, 0)), blk, blk, blk],
        out_specs=[blk] * 4,
        out_shape=[jax.ShapeDtypeStruct((r, c), F32)] * 4,
        compiler_params=pltpu.CompilerParams(dimension_semantics=("parallel",), vmem_limit_bytes=VMEM_MM),
    )(parts, w, m, v)


def kernel(x, positions, attn_norm, mlp_norm, sb_w_qkv, sb_w_o, kv_norm, mla_w_dkv, mla_kv_lat_norm, mla_w_ukv, mla_w_dq, mla_q_lat_norm, mla_w_uq, mla_w_o, mlp_w1, mlp_w2, final_norm, loss_target, m_attn_norm, m_mlp_norm, m_sb_w_qkv, m_sb_w_o, m_kv_norm, m_mla_w_dkv, m_mla_kv_lat_norm, m_mla_w_ukv, m_mla_w_dq, m_mla_q_lat_norm, m_mla_w_uq, m_mla_w_o, m_mlp_w1, m_mlp_w2, m_final_norm, v_attn_norm, v_mlp_norm, v_sb_w_qkv, v_sb_w_o, v_kv_norm, v_mla_w_dkv, v_mla_kv_lat_norm, v_mla_w_ukv, v_mla_w_dq, v_mla_q_lat_norm, v_mla_w_uq, v_mla_w_o, v_mlp_w1, v_mlp_w2, v_final_norm):
    weights = dict(attn_norm=attn_norm, mlp_norm=mlp_norm, sb_w_qkv=sb_w_qkv, sb_w_o=sb_w_o, kv_norm=kv_norm,
                   mla_w_dkv=mla_w_dkv, mla_kv_lat_norm=mla_kv_lat_norm, mla_w_ukv=mla_w_ukv, mla_w_dq=mla_w_dq,
                   mla_q_lat_norm=mla_q_lat_norm, mla_w_uq=mla_w_uq, mla_w_o=mla_w_o, mlp_w1=mlp_w1, mlp_w2=mlp_w2,
                   final_norm=final_norm)
    mom_m = dict(attn_norm=m_attn_norm, mlp_norm=m_mlp_norm, sb_w_qkv=m_sb_w_qkv, sb_w_o=m_sb_w_o, kv_norm=m_kv_norm,
                 mla_w_dkv=m_mla_w_dkv, mla_kv_lat_norm=m_mla_kv_lat_norm, mla_w_ukv=m_mla_w_ukv, mla_w_dq=m_mla_w_dq,
                 mla_q_lat_norm=m_mla_q_lat_norm, mla_w_uq=m_mla_w_uq, mla_w_o=m_mla_w_o, mlp_w1=m_mlp_w1,
                 mlp_w2=m_mlp_w2, final_norm=m_final_norm)
    mom_v = dict(attn_norm=v_attn_norm, mlp_norm=v_mlp_norm, sb_w_qkv=v_sb_w_qkv, sb_w_o=v_sb_w_o, kv_norm=v_kv_norm,
                 mla_w_dkv=v_mla_w_dkv, mla_kv_lat_norm=v_mla_kv_lat_norm, mla_w_ukv=v_mla_w_ukv, mla_w_dq=v_mla_w_dq,
                 mla_q_lat_norm=v_mla_q_lat_norm, mla_w_uq=v_mla_w_uq, mla_w_o=v_mla_w_o, mlp_w1=v_mlp_w1,
                 mlp_w2=v_mlp_w2, final_norm=v_final_norm)
    sharded_names = [n for n, _ in SHARDED]
    repl_shapes = [tuple(weights[n].shape) for n in REPLICATED]

    first_name, first_ax = SHARDED[0]
    w_qkv = _layers(_merge_blocks(_all_gather([weights[first_name].astype(BF16)], "gather_qkv")[0], first_ax))
    shards = [weights[n].astype(BF16) for n in sharded_names[1:]]
    slots, layer_ax = {}, {}
    for n, ax in SHARDED:
        shp = weights[n].shape
        full = shp[:ax] + (N_DEV * shp[ax],) + shp[ax + 1:]
        n_layers, per_layer = (full[0], full[1:]) if len(full) == 3 else (1, full)
        slots[n] = [jnp.zeros(per_layer, F32) for _ in range(n_layers)]
        layer_ax[n] = ax - 1 if len(full) == 3 else ax
    norms = {n: weights[n] for n in REPLICATED if n != 'final_norm'}
    late = [('sb_w_qkv', 0), ('sb_w_o', 0)]
    early = [(n, l) for n in sharded_names for l in range(len(slots[n])) if (n, l) not in late]

    q_tabs, kv_tabs = _rope_tables(positions[0])
    side = {}
    x_mid, pull_first, gathered = jax.vjp(
        lambda xx, sq, so, nn: _first_attention(xx, sq, so, nn, w_qkv[0], shards, side),
        x[0], slots['sb_w_qkv'][0], slots['sb_w_o'][0], norms, has_aux=True)
    wb = {'sb_w_qkv': w_qkv}
    for (n, ax), g in zip(SHARDED[1:], gathered):
        wb[n] = _layers(_merge_blocks(g, ax))
    rest_slots = {n: [None if (n, l) in late else a for l, a in enumerate(v)] for n, v in slots.items()}
    x_last, pull_rest = jax.vjp(lambda xx, ss, nn: _rest(xx, ss, nn, wb, q_tabs, kv_tabs), x_mid, rest_slots, norms)
    loss_part, dx_last, d_final = _loss_head(x_last, final_norm, loss_target[0])
    loss = lax.psum(loss_part[0, 0], ("x", "y", "c"))

    core = lax.axis_index("c").astype(jnp.int32).reshape(1)

    def pair_sums(units, grads, tag):
        halves, dims = [], []
        for (n, l), g in zip(units, grads):
            blocks = _split_blocks(g, layer_ax[n])
            two_d = (math.prod(blocks.shape[1:-1]), blocks.shape[-1])
            dims.append(two_d)
            halves.append(jnp.moveaxis(blocks.reshape((N_DEV // 2, 2) + two_d), 1, 0))
        staged = _pair_exchange(halves, "pair_grads_" + tag)
        sums = []
        for (n, l), h, st, (r, c) in zip(units, halves, staged, dims):
            rows = (N_DEV // 2) * r
            sums.append(_add_round(h.reshape(2, rows, c), st.reshape(rows, c), core,
                                   f"pair_sum_{n}_{l}").reshape(st.shape))
        return sums

    dx_mid, d_rest, d_norms_rest = pull_rest(dx_last)
    side['sums'] = pair_sums(early, [d_rest[n][l] for n, l in early], "early")
    dx, d_qkv0, d_o0, d_norms_first = pull_first(dx_mid)
    received = dict(zip(early, side['received']))
    received.update(zip(late, _chip_exchange(pair_sums(late, [d_qkv0, d_o0], "late"), "scatter_late")))
    d_norms = {n: d_norms_rest[n] + d_norms_first[n] for n in norms}
    d_norms['final_norm'] = d_final
    repl_parts = _all_gather([_pack([d_norms[n] for n in REPLICATED])], "gather_norm_grads")[0]

    results = {kind: {} for kind in ("grad", "delta", "new_m", "new_v")}
    for n in sharded_names:
        shp = weights[n].shape
        two_d = (math.prod(shp[:-1]), shp[-1])
        parts = jnp.concatenate([received[(n, l)] for l in range(len(slots[n]))], axis=1)
        res = _adamw_reduce(parts, weights[n].reshape(two_d), mom_m[n].reshape(two_d),
                            mom_v[n].reshape(two_d), "adamw_" + n)
        for kind, a in zip(results, res):
            results[kind][n] = a.reshape(shp)
    res = _adamw_reduce(repl_parts, _pack([weights[n] for n in REPLICATED]), _pack([mom_m[n] for n in REPLICATED]),
                        _pack([mom_v[n] for n in REPLICATED]), "adamw_replicated")
    for kind, a in zip(results, res):
        results[kind].update(zip(REPLICATED, _unpack(a, repl_shapes)))

    out = [loss, dx[None]]
    for kind in ("grad", "delta", "new_m", "new_v"):
        out += [results[kind][n] for n in WEIGHT_ORDER]
    return tuple(out)
```

```python
import math

import jax
import jax.numpy as jnp
from jax import lax
from jax.experimental import pallas as pl
from jax.experimental.pallas import tpu as pltpu

F32 = jnp.float32
BF16 = jnp.bfloat16
MESH = pl.DeviceIdType.MESH

N_DEV = 8
DEPTH = 4
N_A = 2
SB_HEADS = 16
SB_HD = 64
MLA_HEADS = 16
NOPE = 64
ROPE = 32
VDIM = 64
QK = NOPE + ROPE
KV_RANK = 256
DKV_PAD = 384
LANES = 128
CHUNK_SHIFT = 6
ROPE_THETA = 10000.0
EPS = 1e-6
SB_BLOCK = 256
MLA_BLOCK = 512
MLA_GROUP = 4
MLA_UNROLL = 4
PACK_COLS = 1024
PACK_ROW_ALIGN = 16
EXP_ZERO = -104.0
NEG_BIG = -1e30
LOG2_E = 1.4426950408889634
LN_2 = 0.6931471805599453
VMEM_ATT = 56 * 1024 * 1024
VMEM_MM = 48 * 1024 * 1024

ADAM_LR = 0.001
ADAM_B1 = 0.9
ADAM_B2 = 0.999
ADAM_EPS = 1e-08
ADAM_WD = 0.01
ADAM_STEP = 10

WEIGHT_ORDER = ['attn_norm', 'mlp_norm', 'sb_w_qkv', 'sb_w_o', 'kv_norm', 'mla_w_dkv', 'mla_kv_lat_norm',
                'mla_w_ukv', 'mla_w_dq', 'mla_q_lat_norm', 'mla_w_uq', 'mla_w_o', 'mlp_w1', 'mlp_w2', 'final_norm']
SHARDED = [('sb_w_qkv', 2), ('sb_w_o', 1), ('mla_w_dkv', 0), ('mla_w_ukv', 1), ('mla_w_dq', 1),
           ('mla_w_uq', 2), ('mla_w_o', 1), ('mlp_w1', 2), ('mlp_w2', 1)]
REPLICATED = ['attn_norm', 'mlp_norm', 'kv_norm', 'mla_kv_lat_norm', 'mla_q_lat_norm', 'final_norm']


def _tile(dim, prefs=(512, 384, 256, 128, 64, 32, 16, 8)):
    for t in prefs:
        if dim % t == 0:
            return t
    return dim


def _dot(a, b, dims):
    return lax.dot_general(a, b, (dims, ((), ())), preferred_element_type=F32)


NN = ((1,), (0,))
NT = ((1,), (1,))
TN = ((0,), (0,))


def _gather_phases(x_refs, out_refs, send_sems, recv_sems, local_sems):
    n_t = len(x_refs)
    x, y, c = lax.axis_index("x"), lax.axis_index("y"), lax.axis_index("c")
    me, sibling = (x, y, c), (x, y, 1 - c)
    chips = [(1 - x, y), (x, 1 - y), (1 - x, 1 - y)]

    def slot(t, px, py, pc):
        return out_refs[t].at[4 * px + 2 * py + pc]

    def copy(t, k, block, to, src=None):
        return pltpu.make_async_remote_copy(
            src_ref=slot(t, *block) if src is None else src, dst_ref=slot(t, *block),
            send_sem=send_sems.at[7 * t + k], recv_sem=recv_sems.at[7 * t + k],
            device_id=to, device_id_type=MESH)

    def mine():
        return [pltpu.make_async_copy(x_refs[t], slot(t, *me), local_sems.at[t]) for t in range(n_t)]

    def first():
        out = []
        for t in range(n_t):
            out.append(copy(t, 0, me, sibling, src=x_refs[t]))
            out += [copy(t, 1 + j, me, (*chip, c), src=x_refs[t]) for j, chip in enumerate(chips)]
        return out

    def passed():
        return [copy(t, 4 + j, (*chip, c), sibling) for t in range(n_t) for j, chip in enumerate(chips)]

    def start():
        for cp in mine() + first():
            cp.start()

    def forward(which=None):
        onward = passed()
        for t in range(n_t) if which is None else which:
            for j, chip in enumerate(chips):
                copy(t, 1 + j, (*chip, c), me).wait_recv()
                onward[3 * t + j].start()

    def finish():
        for t in range(n_t):
            copy(t, 0, sibling, me).wait_recv()
            for j, chip in enumerate(chips):
                copy(t, 4 + j, (*chip, 1 - c), me).wait_recv()
        for cp in first() + passed():
            cp.wait_send()
        for cp in mine():
            cp.wait()

    return start, forward, finish


def _gather_scratch(n_t):
    return [pltpu.SemaphoreType.DMA((7 * n_t,)), pltpu.SemaphoreType.DMA((7 * n_t,)), pltpu.SemaphoreType.DMA((n_t,))]


def _all_gather(shards, name):
    n_t = len(shards)

    def body(*refs):
        for phase in _gather_phases(refs[:n_t], refs[n_t:2 * n_t], *refs[2 * n_t:]):
            phase()

    any_spec = pl.BlockSpec(memory_space=pl.ANY)
    return pl.pallas_call(
        body, name=name,
        out_shape=[jax.ShapeDtypeStruct((N_DEV,) + tuple(s.shape), s.dtype) for s in shards],
        in_specs=[any_spec] * n_t, out_specs=[any_spec] * n_t,
        scratch_shapes=_gather_scratch(n_t),
    )(*shards)


def _pair_exchange(xs, name):
    n_t = len(xs)

    def body(*refs):
        x_refs, out_refs = refs[:n_t], refs[n_t:2 * n_t]
        send_sems, recv_sems = refs[2 * n_t:]
        x, y, c = lax.axis_index("x"), lax.axis_index("y"), lax.axis_index("c")
        copies = [pltpu.make_async_remote_copy(
            src_ref=x_refs[t].at[1 - c], dst_ref=out_refs[t], send_sem=send_sems.at[t], recv_sem=recv_sems.at[t],
            device_id=(x, y, 1 - c), device_id_type=MESH) for t in range(n_t)]
        for cp in copies:
            cp.start()
        for cp in copies:
            cp.wait()

    any_spec = pl.BlockSpec(memory_space=pl.ANY)
    return pl.pallas_call(
        body, name=name,
        out_shape=[jax.ShapeDtypeStruct(x.shape[1:], x.dtype) for x in xs],
        in_specs=[any_spec] * n_t, out_specs=[any_spec] * n_t,
        scratch_shapes=[pltpu.SemaphoreType.DMA((n_t,)), pltpu.SemaphoreType.DMA((n_t,))],
    )(*xs)


def _exchange_phases(p_refs, out_refs, send_sems, recv_sems, local_sems):
    n_t = len(p_refs)
    mx, my, mc = lax.axis_index("x"), lax.axis_index("y"), lax.axis_index("c")
    me = 2 * mx + my

    def mine():
        return [pltpu.make_async_copy(p_refs[t].at[me], out_refs[t].at[me], local_sems.at[t]) for t in range(n_t)]

    def copies():
        out = []
        for k in range(1, 4):
            px = 1 - mx if (k >> 1) & 1 else mx
            py = 1 - my if k & 1 else my
            peer = 2 * px + py
            for t in range(n_t):
                out.append(pltpu.make_async_remote_copy(
                    src_ref=p_refs[t].at[peer], dst_ref=out_refs[t].at[me],
                    send_sem=send_sems.at[3 * t + k - 1], recv_sem=recv_sems.at[3 * t + k - 1],
                    device_id=(px, py, mc), device_id_type=MESH))
        return out

    def start():
        for cp in mine() + copies():
            cp.start()

    def finish():
        for cp in copies():
            cp.wait_send()
        for cp in copies():
            cp.wait_recv()
        for cp in mine():
            cp.wait()

    return start, finish


def _exchange_scratch(n_t):
    return [pltpu.SemaphoreType.DMA((3 * n_t,)), pltpu.SemaphoreType.DMA((3 * n_t,)), pltpu.SemaphoreType.DMA((n_t,))]


def _chip_exchange(ps, name):
    n_t = len(ps)

    def body(*refs):
        for phase in _exchange_phases(refs[:n_t], refs[n_t:2 * n_t], *refs[2 * n_t:]):
            phase()

    any_spec = pl.BlockSpec(memory_space=pl.ANY)
    return pl.pallas_call(
        body, name=name,
        out_shape=[jax.ShapeDtypeStruct(p.shape, p.dtype) for p in ps],
        in_specs=[any_spec] * n_t, out_specs=[any_spec] * n_t,
        scratch_shapes=_exchange_scratch(n_t),
    )(*ps)


def _matmul(a, b, *, ta=False, tb=False, out_dtypes=(F32,), epilogue=None, extras=(), name):
    if ta:
        kdim, m = a.shape
    else:
        m, kdim = a.shape
    if tb:
        n, kb = b.shape
    else:
        kb, n = b.shape
    assert kdim == kb, (a.shape, b.shape, ta, tb)
    big = (1024, 768, 512, 384, 256, 128, 64, 32, 16, 8)
    tm, tn = _tile(m, big), _tile(n, big)
    tk = kdim if kdim <= 2048 else _tile(kdim, (2048, 1024, 512, 256, 128))
    nk = kdim // tk
    n_extra, n_out = len(extras), len(out_dtypes)
    a_spec = pl.BlockSpec((tk, tm), lambda i, j, k: (k, i)) if ta else pl.BlockSpec((tm, tk), lambda i, j, k: (i, k))
    b_spec = pl.BlockSpec((tn, tk), lambda i, j, k: (j, k)) if tb else pl.BlockSpec((tk, tn), lambda i, j, k: (k, j))
    tile_spec = pl.BlockSpec((tm, tn), lambda i, j, k: (i, j))
    dims = ((0,) if ta else (1,), (1,) if tb else (0,))

    def finish(acc, extra_refs, out_refs):
        outs = (acc,) if epilogue is None else epilogue(acc, *[r[...] for r in extra_refs])
        for o_ref, o in zip(out_refs, outs):
            o_ref[...] = o.astype(o_ref.dtype)

    def body_one(a_ref, b_ref, *rest):
        acc = _dot(a_ref[...].astype(BF16), b_ref[...].astype(BF16), dims)
        finish(acc, rest[:n_extra], rest[n_extra:n_extra + n_out])

    def body_acc(a_ref, b_ref, *rest):
        acc_ref = rest[-1]
        k = pl.program_id(2)

        @pl.when(k == 0)
        def _():
            acc_ref[...] = jnp.zeros_like(acc_ref)

        acc_ref[...] += _dot(a_ref[...].astype(BF16), b_ref[...].astype(BF16), dims)

        @pl.when(k == nk - 1)
        def _():
            finish(acc_ref[...], rest[:n_extra], rest[n_extra:n_extra + n_out])

    return pl.pallas_call(
        body_one if nk == 1 else body_acc, name=name, grid=(m // tm, n // tn, nk),
        in_specs=[a_spec, b_spec] + [tile_spec] * n_extra,
        out_specs=[tile_spec] * n_out,
        out_shape=[jax.ShapeDtypeStruct((m, n), dt) for dt in out_dtypes],
        scratch_shapes=[] if nk == 1 else [pltpu.VMEM((tm, tn), F32)],
        compiler_params=pltpu.CompilerParams(
            dimension_semantics=("parallel", "parallel", "arbitrary"), vmem_limit_bytes=VMEM_MM),
    )(a, b, *extras)


def _rms_fwd(x, g, name):
    m, d = x.shape
    tm = _tile(m, (512, 256, 128, 64, 32, 16, 8))

    def body(x_ref, g_ref, y_ref):
        xv = x_ref[...]
        r = lax.rsqrt(jnp.mean(xv * xv, axis=-1, keepdims=True) + EPS)
        y_ref[...] = (xv * r * g_ref[...]).astype(y_ref.dtype)

    return pl.pallas_call(
        body, name=name, grid=(m // tm,),
        in_specs=[pl.BlockSpec((tm, d), lambda i: (i, 0)), pl.BlockSpec((1, d), lambda i: (0, 0))],
        out_specs=pl.BlockSpec((tm, d), lambda i: (i, 0)),
        out_shape=jax.ShapeDtypeStruct((m, d), BF16),
        compiler_params=pltpu.CompilerParams(dimension_semantics=("parallel",)),
    )(x, g.reshape(1, d))


def _matmul_rms_bwd(dy, w, x, g, name, res=None):
    m, n = dy.shape
    d = w.shape[0]
    tm = _tile(m, (512, 256, 128, 64, 32, 16, 8))
    tk = n if n <= 2048 else _tile(n, (2048, 1024, 512, 256, 128))
    nk = n // tk
    has_res = res is not None

    def body(dy_ref, w_ref, x_ref, g_ref, *rest):
        dx_ref, dg_ref, acc_ref = rest[-3:]
        i, k = pl.program_id(0), pl.program_id(1)

        @pl.when(k == 0)
        def _():
            acc_ref[...] = jnp.zeros_like(acc_ref)

        @pl.when(jnp.logical_and(i == 0, k == 0))
        def _():
            dg_ref[...] = jnp.zeros_like(dg_ref)

        acc_ref[...] += _dot(dy_ref[...], w_ref[...], NT)

        @pl.when(k == nk - 1)
        def _():
            dh = acc_ref[...]
            xv = x_ref[...]
            r = lax.rsqrt(jnp.mean(xv * xv, axis=-1, keepdims=True) + EPS)
            xh = xv * r
            t = dh * g_ref[...]
            dx = r * (t - xh * jnp.mean(t * xh, axis=-1, keepdims=True))
            dx_ref[...] = dx + rest[0][...] if has_res else dx
            dg_ref[...] += jnp.sum(dh * xh, axis=0, keepdims=True)

    row_spec = pl.BlockSpec((tm, d), lambda i, k: (i, 0))
    vec_spec = pl.BlockSpec((1, d), lambda i, k: (0, 0))
    dx, dg = pl.pallas_call(
        body, name=name, grid=(m // tm, nk),
        in_specs=[pl.BlockSpec((tm, tk), lambda i, k: (i, k)), pl.BlockSpec((d, tk), lambda i, k: (0, k)),
                  row_spec, vec_spec] + ([row_spec] if has_res else []),
        out_specs=[row_spec, vec_spec],
        out_shape=[jax.ShapeDtypeStruct((m, d), F32), jax.ShapeDtypeStruct((1, d), F32)],
        scratch_shapes=[pltpu.VMEM((tm, d), F32)],
        compiler_params=pltpu.CompilerParams(
            dimension_semantics=("arbitrary", "arbitrary"), vmem_limit_bytes=VMEM_MM),
    )(dy, w, x, g.reshape(1, d), *((res,) if has_res else ()))
    return dx, dg.reshape(d)


def _loss_head(x, g, target):
    m, d = x.shape
    tm = _tile(m, (512, 256, 128, 64, 32, 16, 8))

    def body(x_ref, g_ref, t_ref, loss_ref, dx_ref, dg_ref):
        xv = x_ref[...]
        gv = g_ref[...]
        r = lax.rsqrt(jnp.mean(xv * xv, axis=-1, keepdims=True) + EPS)
        xh = xv * r
        err = xh * gv - t_ref[...]
        row_loss = jnp.mean(err * err, axis=-1, keepdims=True)
        dyv = err * (1.0 / d)
        t = dyv * gv
        dx_ref[...] = r * (t - xh * jnp.mean(t * xh, axis=-1, keepdims=True))

        @pl.when(pl.program_id(0) == 0)
        def _():
            dg_ref[...] = jnp.zeros_like(dg_ref)
            loss_ref[...] = jnp.zeros_like(loss_ref)

        dg_ref[...] += jnp.sum(dyv * xh, axis=0, keepdims=True)
        loss_ref[...] += 0.5 * jnp.sum(row_loss, axis=0, keepdims=True)

    loss, dx, dg = pl.pallas_call(
        body, name="loss_head", grid=(m // tm,),
        in_specs=[pl.BlockSpec((tm, d), lambda i: (i, 0)), pl.BlockSpec((1, d), lambda i: (0, 0)),
                  pl.BlockSpec((tm, d), lambda i: (i, 0))],
        out_specs=[pl.BlockSpec((1, 1), lambda i: (0, 0)), pl.BlockSpec((tm, d), lambda i: (i, 0)),
                   pl.BlockSpec((1, d), lambda i: (0, 0))],
        out_shape=[jax.ShapeDtypeStruct((1, 1), F32), jax.ShapeDtypeStruct((m, d), F32),
                   jax.ShapeDtypeStruct((1, d), F32)],
        compiler_params=pltpu.CompilerParams(dimension_semantics=("arbitrary",)),
    )(x, g.reshape(1, d), target)
    return loss, dx, dg.reshape(d)


def _rope_apply(x, tabs, transpose, name):
    m, w = x.shape
    wt = tabs[0].shape[1]
    reps = w // wt
    half = ROPE // 2
    tm = _tile(m, (256, 128, 64, 32, 16, 8))

    def body(x_ref, c_ref, a_ref, b_ref, y_ref):
        xv = x_ref[...]

        def wide(t_ref):
            t = t_ref[...]
            return t if reps == 1 else jnp.concatenate([t] * reps, axis=1)

        c, a, b = wide(c_ref), wide(a_ref), wide(b_ref)
        if transpose:
            y = xv * c + pltpu.roll(xv * a, half, 1) + pltpu.roll(xv * b, w - half, 1)
        else:
            y = xv * c + pltpu.roll(xv, w - half, 1) * a + pltpu.roll(xv, half, 1) * b
        y_ref[...] = y

    x_spec = pl.BlockSpec((tm, w), lambda i: (i, 0))
    t_spec = pl.BlockSpec((tm, wt), lambda i: (i, 0))
    return pl.pallas_call(
        body, name=name, grid=(m // tm,),
        in_specs=[x_spec, t_spec, t_spec, t_spec], out_specs=x_spec,
        out_shape=jax.ShapeDtypeStruct((m, w), F32),
        compiler_params=pltpu.CompilerParams(dimension_semantics=("parallel",)),
    )(x, *tabs)


def _log_sigmoid_pair(z):
    a = jnp.minimum(z, 0.0) - jnp.log(1.0 + jnp.exp(-jnp.abs(z)))
    return a, a - z


def _split_bf16(x):
    hi = x.astype(BF16)
    return hi, (x - hi.astype(F32)).astype(BF16)


def _wide(v, width):
    return v if width == LANES else jnp.concatenate([v] * (width // LANES), axis=1)


def _lanes(col):
    return jnp.broadcast_to(col, (col.shape[0], LANES))


def _rope_head(x, c, a, b):
    half = ROPE // 2
    return x * c + pltpu.roll(x, LANES - half, 1) * a + pltpu.roll(x, half, 1) * b


def _rope_head_t(dy, c, a, b):
    half = ROPE // 2
    return dy * c + pltpu.roll(dy * a, half, 1) + pltpu.roll(dy * b, LANES - half, 1)


def _sb_weights(qh, k2, valid):
    a, b = _log_sigmoid_pair(_dot(qh, k2, NT))
    return a, b if valid is None else jnp.where(valid, b, 0.0)


def _sb_fwd(qkv, shards=()):
    s = qkv.shape[0]
    t = min(SB_BLOCK, s)
    nq = s // t
    npair = SB_HEADS // 2
    scale = SB_HD ** -0.5

    n_t = len(shards)

    def body(q_ref, k_ref, v_ref, *rest):
        x_refs, rest = rest[:n_t], rest[n_t:]
        o_ref, bta_ref, btb_ref, js_ref = rest[:4]
        g_refs, rest = rest[4:4 + n_t], rest[4 + n_t:]
        acc_ref, ra_ref, rb_ref = rest[:3]
        p, i = pl.program_id(0), pl.program_id(1)
        if n_t:
            start, forward, finish = _gather_phases(x_refs, g_refs, *rest[3:])
            pl.when(jnp.logical_and(p == 0, i == 0))(start)
            pl.when(jnp.logical_and(p == (5 * npair) // 8, i == 0))(lambda: forward(range(n_t - 1)))
            pl.when(jnp.logical_and(p == (7 * npair) // 8, i == 0))(lambda: forward([n_t - 1]))
        q2 = q_ref[...] * scale
        first = lax.broadcasted_iota(jnp.int32, (t, LANES), 1) < SB_HD
        heads = (jnp.where(first, q2, jnp.zeros_like(q2)), jnp.where(first, jnp.zeros_like(q2), q2))
        row = lax.broadcasted_iota(jnp.int32, (t, t), 0)
        col = lax.broadcasted_iota(jnp.int32, (t, t), 1)
        later = jnp.where(row > col, 1.0, 0.0).astype(BF16)
        acc_ref[...] = jnp.zeros_like(acc_ref)
        ra_ref[...] = jnp.zeros_like(ra_ref)
        rb_ref[...] = jnp.zeros_like(rb_ref)

        def block(j, diagonal):
            off = pl.multiple_of(j * t, t)
            k2 = k_ref[pl.ds(off, t), :]
            v2 = v_ref[pl.ds(off, t), :]
            valid = col < row if diagonal else None
            outs, rmax = [], None
            for qh, r_ref in zip(heads, (ra_ref, rb_ref)):
                a, b = _sb_weights(qh, k2, valid)
                bh, bl = _split_bf16(b)
                inner = _dot(bh, later, NN) + _dot(bl, later, NN)
                r = r_ref[...]
                w = jnp.exp(a + inner + _wide(r, t))
                if diagonal:
                    w = jnp.where(valid, w, 0.0)
                outs.append(_dot(w.astype(BF16), v2, NN))
                rn = r + jnp.sum(b, axis=1, keepdims=True)
                r_ref[...] = rn
                rmax = jnp.max(rn) if rmax is None else jnp.maximum(rmax, jnp.max(rn))
            acc_ref[...] += jnp.where(first, outs[0], outs[1])
            return rmax

        def cond(carry):
            j, rmax = carry
            return jnp.logical_and(j >= 0, rmax > EXP_ZERO)

        def step(carry):
            j, _ = carry
            return j - 1, block(j, False)

        def first_two():
            block(i, True)
            return block(i - 1, False)

        rmax = lax.cond(i >= 1, first_two, lambda: block(i, True))
        jend, _ = lax.while_loop(cond, step, (jnp.maximum(i - 2, -1), rmax))
        o_ref[...] = acc_ref[...].astype(o_ref.dtype)
        bta_ref[0] = ra_ref[...][:, :1]
        btb_ref[0] = rb_ref[...][:, :1]
        js_ref[p, i] = (jend + 1).astype(F32)
        if n_t:
            pl.when(jnp.logical_and(p == npair - 1, i == nq - 1))(finish)

    stat = pl.BlockSpec((1, t, 1), lambda p, i: (p, i, 0))
    any_spec = pl.BlockSpec(memory_space=pl.ANY)
    return pl.pallas_call(
        body, name="sb_attn_fwd_gather" if n_t else "sb_attn_fwd", grid=(npair, nq),
        in_specs=[pl.BlockSpec((t, LANES), lambda p, i: (i, p)),
                  pl.BlockSpec((s, LANES), lambda p, i: (0, npair + p)),
                  pl.BlockSpec((s, LANES), lambda p, i: (0, 2 * npair + p))] + [any_spec] * n_t,
        out_specs=[pl.BlockSpec((t, LANES), lambda p, i: (i, p)), stat, stat,
                   pl.BlockSpec(memory_space=pltpu.SMEM)] + [any_spec] * n_t,
        out_shape=[jax.ShapeDtypeStruct((s, SB_HEADS * SB_HD), BF16), jax.ShapeDtypeStruct((npair, s, 1), F32),
                   jax.ShapeDtypeStruct((npair, s, 1), F32), jax.ShapeDtypeStruct((npair, nq), F32)]
        + [jax.ShapeDtypeStruct((N_DEV,) + tuple(x.shape), x.dtype) for x in shards],
        scratch_shapes=[pltpu.VMEM((t, LANES), F32)] * 3 + (_gather_scratch(n_t) if n_t else []),
        compiler_params=pltpu.CompilerParams(
            dimension_semantics=("arbitrary", "arbitrary"), vmem_limit_bytes=VMEM_ATT),
    )(qkv, qkv, qkv, *shards)


def _sb_bwd(qkv, do, btot_a, btot_b, jstart, sums=()):
    s = qkv.shape[0]
    t = min(SB_BLOCK, s)
    nq = s // t
    npair = SB_HEADS // 2
    scale = SB_HD ** -0.5

    n_t = len(sums)

    def body(js_ref, q_ref, k_ref, v_ref, do_ref, bta_ref, btb_ref, *rest):
        s_refs, rest = rest[:n_t], rest[n_t:]
        dq_ref, dk_ref, dv_ref = rest[:3]
        r_refs, rest = rest[3:3 + n_t], rest[3 + n_t:]
        dq_acc, pa_ref, pb_ref, ga_ref, gb_ref, dk_acc, dv_acc = rest[:7]
        p, i = pl.program_id(0), pl.program_id(1)
        if n_t:
            start, finish = _exchange_phases(s_refs, r_refs, *rest[7:])
            pl.when(jnp.logical_and(p == 0, i == 0))(start)

        @pl.when(i == 0)
        def _():
            dk_acc[...] = jnp.zeros_like(dk_acc)
            dv_acc[...] = jnp.zeros_like(dv_acc)

        q2 = q_ref[...] * scale
        do2 = do_ref[...]
        first = lax.broadcasted_iota(jnp.int32, (t, LANES), 1) < SB_HD
        zero = jnp.zeros_like(q2)
        q_heads = (jnp.where(first, q2, zero), jnp.where(first, zero, q2))
        do_heads = (jnp.where(first, do2, zero), jnp.where(first, zero, do2))
        bts = (_wide(_lanes(bta_ref[0]), t), _wide(_lanes(btb_ref[0]), t))
        row = lax.broadcasted_iota(jnp.int32, (t, t), 0)
        col = lax.broadcasted_iota(jnp.int32, (t, t), 1)
        upto = jnp.where(row <= col, 1.0, 0.0).astype(BF16)
        before = jnp.where(row < col, 1.0, 0.0).astype(BF16)
        dq_acc[...] = jnp.zeros_like(dq_acc)
        for r in (pa_ref, pb_ref, ga_ref, gb_ref):
            r[...] = jnp.zeros_like(r)
        j0 = jnp.clip(js_ref[p, i].astype(jnp.int32), 0, i)

        def block(j, diagonal):
            off = pl.multiple_of(j * t, t)
            k2 = k_ref[pl.ds(off, t), :]
            v2 = v_ref[pl.ds(off, t), :]
            valid = col < row if diagonal else None
            dqs, dk2, dv2 = [], None, None
            for qh, doh, bt, p_ref, g_ref in zip(q_heads, do_heads, bts, (pa_ref, pb_ref), (ga_ref, gb_ref)):
                a, b = _sb_weights(qh, k2, valid)
                bh, bl = _split_bf16(b)
                pin = _dot(bh, upto, NN) + _dot(bl, upto, NN)
                surv = bt - (_wide(p_ref[...], t) + pin)
                w = jnp.exp(a + surv)
                if diagonal:
                    w = jnp.where(valid, w, 0.0)
                g = w * _dot(doh, v2, NT)
                gh, gl = _split_bf16(g)
                gsum = _wide(g_ref[...], t) + _dot(gh, before, NN) + _dot(gl, before, NN)
                beta = jnp.exp(a)
                dz = g * (1.0 - beta) - gsum * beta
                if diagonal:
                    dz = jnp.where(valid, dz, 0.0)
                dz = dz.astype(BF16)
                dqs.append(_dot(dz, k2, NN))
                dkh = _dot(dz, qh, TN)
                dvh = _dot(w.astype(BF16), doh, TN)
                dk2 = dkh if dk2 is None else dk2 + dkh
                dv2 = dvh if dv2 is None else dv2 + dvh
                p_ref[...] += jnp.sum(b, axis=1, keepdims=True)
                g_ref[...] += jnp.sum(g, axis=1, keepdims=True)
            dq_acc[...] += jnp.where(first, dqs[0], dqs[1])
            dk_acc[pl.ds(off, t), :] += dk2
            dv_acc[pl.ds(off, t), :] += dv2

        def step(j, carry):
            block(j, False)
            return carry

        lax.fori_loop(j0, jnp.maximum(i - 1, j0), step, 0)

        @pl.when(j0 < i)
        def _():
            block(i - 1, False)
            block(i, True)

        @pl.when(j0 >= i)
        def _():
            block(i, True)

        dq_ref[...] = (dq_acc[...] * scale).astype(dq_ref.dtype)

        @pl.when(i == nq - 1)
        def _():
            dk_ref[...] = dk_acc[...].astype(dk_ref.dtype)
            dv_ref[...] = dv_acc[...].astype(dv_ref.dtype)

        if n_t:
            pl.when(jnp.logical_and(p == npair - 1, i == nq - 1))(finish)

    blk = pl.BlockSpec((t, LANES), lambda p, i: (i, p))
    full = pl.BlockSpec((s, LANES), lambda p, i: (0, p))
    stat = pl.BlockSpec((1, t, 1), lambda p, i: (p, i, 0))
    vec = pltpu.VMEM((t, LANES), F32)
    any_spec = pl.BlockSpec(memory_space=pl.ANY)
    return pl.pallas_call(
        body, name="sb_attn_bwd_exchange" if n_t else "sb_attn_bwd", grid=(npair, nq),
        in_specs=[pl.BlockSpec(memory_space=pltpu.SMEM), blk,
                  pl.BlockSpec((s, LANES), lambda p, i: (0, npair + p)),
                  pl.BlockSpec((s, LANES), lambda p, i: (0, 2 * npair + p)), blk, stat, stat] + [any_spec] * n_t,
        out_specs=[blk, full, full] + [any_spec] * n_t,
        out_shape=[jax.ShapeDtypeStruct((s, SB_HEADS * SB_HD), BF16)] * 3
        + [jax.ShapeDtypeStruct(x.shape, x.dtype) for x in sums],
        scratch_shapes=[pltpu.VMEM((t, LANES), F32), vec, vec, vec, vec,
                        pltpu.VMEM((s, LANES), F32), pltpu.VMEM((s, LANES), F32)]
        + (_exchange_scratch(n_t) if n_t else []),
        compiler_params=pltpu.CompilerParams(
            dimension_semantics=("arbitrary", "arbitrary"), vmem_limit_bytes=VMEM_ATT),
    )(jstart, qkv, qkv, qkv, do, btot_a, btot_b, *sums)


def _make_sb_attention(side=None):
    @jax.custom_vjp
    def f(qkv, *shards):
        o, _, _, _, *gathered = _sb_fwd(qkv, shards)
        return (o, *gathered)

    def fwd(qkv, *shards):
        o, btot_a, btot_b, jstart, *gathered = _sb_fwd(qkv, shards)
        return (o, *gathered), (qkv, btot_a, btot_b, jstart, shards)

    def bwd(res, cts):
        qkv, btot_a, btot_b, jstart, shards = res
        sums = () if side is None else tuple(side['sums'])
        dq, dk, dv, *received = _sb_bwd(qkv, cts[0], btot_a, btot_b, jstart, sums)
        if side is not None:
            side['received'] = received
        return (jnp.concatenate([dq, dk, dv], axis=1),) + tuple(jnp.zeros_like(x) for x in shards)

    f.defvjp(fwd, bwd)
    return f


def sb_attention(qkv):
    return _make_sb_attention()(qkv)[0]


def _chunk_allowed(row, col):
    return (col >> CHUNK_SHIFT) <= (row >> CHUNK_SHIFT)


def _mla_fwd(q, k, vx, tabs):
    s = q.shape[0]
    h_ = q.shape[1] // LANES
    t = min(MLA_BLOCK, s)
    nq = s // t
    scale = QK ** -0.5
    hp = MLA_GROUP

    def body(q_ref, k_ref, v_ref, c_ref, a_ref, b_ref, o_ref, lse_ref, acc_ref, m_ref):
        i = pl.program_id(1)
        acc_ref[...] = jnp.zeros_like(acc_ref)
        m_ref[...] = jnp.full_like(m_ref, NEG_BIG)
        rot = (c_ref[...], a_ref[...], b_ref[...])
        qs = [_rope_head(q_ref[:, hh * LANES:(hh + 1) * LANES], *rot).astype(BF16) for hh in range(hp)]

        def tile(j, diagonal):
            off = pl.multiple_of(j * t, t)
            for hh in range(hp):
                lanes = slice(hh * LANES, (hh + 1) * LANES)
                kj = k_ref[pl.ds(off, t), lanes]
                vj = v_ref[pl.ds(off, t), lanes]
                sc = _dot(qs[hh], kj, NT) * (scale * LOG2_E)
                if diagonal:
                    row = lax.broadcasted_iota(jnp.int32, (t, t), 0)
                    col = lax.broadcasted_iota(jnp.int32, (t, t), 1)
                    sc = jnp.where(_chunk_allowed(row, col), sc, NEG_BIG)
                m_old = m_ref[hh]
                m_new = jnp.maximum(m_old, jnp.max(sc, axis=1, keepdims=True))
                p = jnp.exp2(sc - _wide(m_new, t))
                acc_ref[hh] = jnp.exp2(m_old - m_new) * acc_ref[hh] + _dot(p.astype(BF16), vj, NN)
                m_ref[hh] = m_new

        def step(jj, carry):
            tile(2 * jj, False)
            tile(2 * jj + 1, False)
            return carry

        lax.fori_loop(0, i // 2, step, 0)

        @pl.when(i % 2 == 1)
        def _():
            tile(i - 1, False)

        tile(i, True)
        first = lax.broadcasted_iota(jnp.int32, (t, LANES), 1) < VDIM
        outs = []
        for hh in range(hp):
            acc = acc_ref[hh]
            den = acc[:, VDIM:VDIM + 1]
            outs.append(acc / den)
            lse_ref[hh] = (m_ref[hh][:, :1] + jnp.log2(den)) * LN_2
        for pp in range(hp // 2):
            o_ref[:, pp * LANES:(pp + 1) * LANES] = jnp.where(
                first, outs[2 * pp], pltpu.roll(outs[2 * pp + 1], VDIM, 1))

    wide = hp * LANES
    return pl.pallas_call(
        body, name="mla_attn_fwd", grid=(h_ // hp, nq),
        in_specs=[pl.BlockSpec((t, wide), lambda g, i: (i, g)),
                  pl.BlockSpec((s, wide), lambda g, i: (0, g)),
                  pl.BlockSpec((s, wide), lambda g, i: (0, g))] + [pl.BlockSpec((t, LANES), lambda g, i: (i, 0))] * 3,
        out_specs=[pl.BlockSpec((t, hp * VDIM), lambda g, i: (i, g)),
                   pl.BlockSpec((hp, t, 1), lambda g, i: (g, i, 0))],
        out_shape=[jax.ShapeDtypeStruct((s, h_ * VDIM), F32), jax.ShapeDtypeStruct((h_, s, 1), F32)],
        scratch_shapes=[pltpu.VMEM((hp, t, LANES), F32), pltpu.VMEM((hp, t, LANES), F32)],
        compiler_params=pltpu.CompilerParams(
            dimension_semantics=("arbitrary", "arbitrary"), vmem_limit_bytes=VMEM_ATT),
    )(q, k, vx, *tabs)


def _mla_bwd(q, k, vx, tabs, o, lse, do):
    s = q.shape[0]
    h_ = q.shape[1] // LANES
    t = min(MLA_BLOCK, s)
    nq = s // t
    scale = QK ** -0.5

    def body(q_ref, k_ref, v_ref, c_ref, a_ref, b_ref, o_ref, lse_ref, do_ref, dq_ref, dk_ref, dv_ref, dq_acc):
        h, i = pl.program_id(0), pl.program_id(1)
        rot = (c_ref[...], a_ref[...], b_ref[...])

        @pl.when(i == 0)
        def _():
            dk_ref[...] = jnp.zeros_like(dk_ref)
            dv_ref[...] = jnp.zeros_like(dv_ref)

        qv = _rope_head(q_ref[...], *rot).astype(BF16)
        lane = lax.broadcasted_iota(jnp.int32, (t, LANES), 1)
        start = (h % 2) * VDIM
        mine = jnp.logical_and(lane >= start, lane < start + VDIM)
        do2 = do_ref[...]
        delta = _lanes(jnp.sum(jnp.where(mine, do2 * o_ref[...], 0.0), axis=1, keepdims=True))
        odd = (start + jnp.zeros_like(lane)) > 0
        do_head = jnp.where(odd, pltpu.roll(do2, VDIM, 1), do2)
        dov = jnp.where(lane < VDIM, do_head, 0.0).astype(BF16)
        lse = _lanes(lse_ref[0])
        dq_acc[...] = jnp.zeros_like(dq_acc)

        def tile(j, diagonal):
            off = pl.multiple_of(j * t, t)
            kj = k_ref[pl.ds(off, t), :]
            vj = v_ref[pl.ds(off, t), :]
            p = jnp.exp(_dot(qv, kj, NT) * scale - _wide(lse, t))
            if diagonal:
                row = lax.broadcasted_iota(jnp.int32, (t, t), 0)
                col = lax.broadcasted_iota(jnp.int32, (t, t), 1)
                p = jnp.where(_chunk_allowed(row, col), p, 0.0)
            dp = _dot(dov, vj, NT)
            ds = (p * (dp - _wide(delta, t)) * scale).astype(BF16)
            dq_acc[...] += _dot(ds, kj, NN)
            dk_ref[pl.ds(off, t), :] += _dot(ds, qv, TN)
            dv_ref[pl.ds(off, t), :] += _dot(p.astype(BF16), dov, TN)

        def step(jj, carry):
            for u in range(MLA_UNROLL):
                tile(MLA_UNROLL * jj + u, False)
            return carry

        def single(j, carry):
            tile(j, False)
            return carry

        lax.fori_loop(0, i // MLA_UNROLL, step, 0)
        lax.fori_loop(i - i % MLA_UNROLL, i, single, 0)
        tile(i, True)
        dq_ref[...] = _rope_head_t(dq_acc[...], *rot)

    blk = pl.BlockSpec((t, LANES), lambda h, i: (i, h))
    full = pl.BlockSpec((s, LANES), lambda h, i: (0, h))
    pair = pl.BlockSpec((t, LANES), lambda h, i: (i, h // 2))
    stat = pl.BlockSpec((1, t, 1), lambda h, i: (h, i, 0))
    return pl.pallas_call(
        body, name="mla_attn_bwd", grid=(h_, nq),
        in_specs=[blk, full, full] + [pl.BlockSpec((t, LANES), lambda h, i: (i, 0))] * 3 + [pair, stat, pair],
        out_specs=[blk, full, full],
        out_shape=[jax.ShapeDtypeStruct((s, h_ * LANES), F32)] * 3,
        scratch_shapes=[pltpu.VMEM((t, LANES), F32)],
        compiler_params=pltpu.CompilerParams(
            dimension_semantics=("arbitrary", "arbitrary"), vmem_limit_bytes=VMEM_ATT),
    )(q, k, vx, *tabs, o, lse, do)


def _assemble_kv(kvp, kr):
    half = kvp.shape[1] // 2
    ones = jnp.tile((jnp.arange(LANES) == VDIM).astype(F32), MLA_HEADS)
    k = kvp[:, :half] + jnp.tile(kr, (1, MLA_HEADS))
    return k.astype(BF16), (kvp[:, half:] + ones).astype(BF16)


@jax.custom_vjp
def mla_attention(q, kvp, kr, c, a, b):
    return _mla_fwd(q, *_assemble_kv(kvp, kr), (c, a, b))[0]


def _mla_attention_fwd(q, kvp, kr, c, a, b):
    kb, vb = _assemble_kv(kvp, kr)
    o, lse = _mla_fwd(q, kb, vb, (c, a, b))
    return o, (q, kb, vb, (c, a, b), o, lse)


def _mla_attention_bwd(res, do):
    q, kb, vb, tabs, o, lse = res
    dq, dk, dvx = _mla_bwd(q, kb, vb, tabs, o, lse, do)
    dkr = dk.reshape(dk.shape[0], MLA_HEADS, LANES).sum(axis=1)
    return (dq, jnp.concatenate([dk, dvx], axis=1), dkr) + tuple(jnp.zeros_like(t) for t in tabs)


mla_attention.defvjp(_mla_attention_fwd, _mla_attention_bwd)


def _add_tile(acc, res):
    return (acc + res,)


def _make_linear_res(tag, a_dtype=F32):
    def forward(x, a, slot, wb):
        ab = a.astype(BF16)
        y = _matmul(ab, wb, epilogue=_add_tile, extras=(x,), name=tag + "_fwd")[0]
        return y, (ab, wb)

    @jax.custom_vjp
    def f(x, a, slot, wb):
        return forward(x, a, slot, wb)[0]

    def bwd(res, dy):
        ab, wb = res
        dyb = dy.astype(BF16)
        da = _matmul(dyb, wb, tb=True, out_dtypes=(a_dtype,), name=tag + "_da")[0]
        dw = _matmul(ab, dyb, ta=True, name=tag + "_dw")[0]
        return dy, da, dw, jnp.zeros_like(wb)

    f.defvjp(forward, bwd)
    return f


def _make_norm_linear(tag, out_dtype=F32):
    def forward(x, g, slot, wb):
        hb = _rms_fwd(x, g, tag + "_norm")
        y = _matmul(hb, wb, out_dtypes=(out_dtype,), name=tag + "_fwd")[0]
        return y, (x, g, wb, hb)

    @jax.custom_vjp
    def f(x, g, slot, wb):
        return forward(x, g, slot, wb)[0]

    def bwd(res, dy):
        x, g, wb, hb = res
        dyb = dy.astype(BF16)
        dw = _matmul(hb, dyb, ta=True, name=tag + "_dw")[0]
        dx, dg = _matmul_rms_bwd(dyb, wb, x, g, tag + "_dh_norm_bwd")
        return dx, dg, dw, jnp.zeros_like(wb)

    f.defvjp(forward, bwd)
    return f


def _relu2_fwd(acc):
    r = jnp.maximum(acc, 0.0)
    return acc, r * r


def _relu2_bwd(acc, u):
    return (acc * (2.0 * jnp.maximum(u.astype(F32), 0.0)),)


def _make_mlp_res(tag):
    def forward(x, g, slot1, slot2, w1b, w2b):
        hb = _rms_fwd(x, g, tag + "_norm")
        u, act = _matmul(hb, w1b, out_dtypes=(BF16, BF16), epilogue=_relu2_fwd, name=tag + "_up")
        y = _matmul(act, w2b, epilogue=_add_tile, extras=(x,), name=tag + "_down")[0]
        return y, (x, g, w1b, w2b, hb, u, act)

    @jax.custom_vjp
    def f(x, g, slot1, slot2, w1b, w2b):
        return forward(x, g, slot1, slot2, w1b, w2b)[0]

    def bwd(res, dy):
        x, g, w1b, w2b, hb, u, act = res
        dyb = dy.astype(BF16)
        du = _matmul(dyb, w2b, tb=True, out_dtypes=(BF16,), epilogue=_relu2_bwd, extras=(u,), name=tag + "_du")[0]
        dw2 = _matmul(act, dyb, ta=True, name=tag + "_dw2")[0]
        dw1 = _matmul(hb, du, ta=True, name=tag + "_dw1")[0]
        dx, dg = _matmul_rms_bwd(du, w1b, x, g, tag + "_dh_norm_bwd", res=dy)
        return dx, dg, dw1, dw2, jnp.zeros_like(w1b), jnp.zeros_like(w2b)

    f.defvjp(forward, bwd)
    return f


def _make_rope(tag):
    @jax.custom_vjp
    def f(x, c, a, b):
        return _rope_apply(x, (c, a, b), False, tag + "_fwd")

    def fwd(x, c, a, b):
        return _rope_apply(x, (c, a, b), False, tag + "_fwd"), (c, a, b)

    def bwd(res, dy):
        c, a, b = res
        return _rope_apply(dy, (c, a, b), True, tag + "_bwd"), jnp.zeros_like(c), jnp.zeros_like(a), jnp.zeros_like(b)

    f.defvjp(fwd, bwd)
    return f


def _rope_tables(positions):
    half = ROPE // 2
    inv_freq = ROPE_THETA ** (-jnp.arange(0, ROPE, 2, dtype=F32) / ROPE)
    ang = positions.astype(F32)[:, None] * inv_freq
    cos, sin = jnp.cos(ang), jnp.sin(ang)
    s = positions.shape[0]
    one = lambda n: jnp.ones((s, n), F32)
    zero = lambda n: jnp.zeros((s, n), F32)

    def tables(before, after):
        return (jnp.concatenate([one(before), cos, cos, one(after)], axis=1),
                jnp.concatenate([zero(before), -sin, zero(half + after)], axis=1),
                jnp.concatenate([zero(before + half), sin, zero(after)], axis=1))

    return tables(NOPE, LANES - QK), tables(KV_RANK, DKV_PAD - KV_RANK - ROPE)


def _pad_heads(w, per):
    lead = w.shape[:-1]
    w = jnp.pad(w.reshape(lead + (MLA_HEADS, per)), [(0, 0)] * len(lead) + [(0, 0), (0, LANES - per)])
    return w.reshape(lead + (MLA_HEADS * LANES,))


def _split_kv_heads(w):
    w3 = w.reshape(w.shape[0], MLA_HEADS, NOPE + VDIM)
    return jnp.concatenate([_pad_heads(w3[..., :NOPE].reshape(w.shape[0], -1), NOPE),
                            _pad_heads(w3[..., NOPE:].reshape(w.shape[0], -1), VDIM)], axis=1)


def _layers(full):
    return [full[l] for l in range(full.shape[0])] if full.ndim == 3 else [full]


def _first_attention(x, slot_qkv, slot_o, norms, w_qkv0, shards, side):
    qkv = _make_norm_linear("sb0_qkv", BF16)(x, norms['attn_norm'][0], slot_qkv, w_qkv0)
    o, *gathered = _make_sb_attention(side)(qkv, *shards)
    w_o = _layers(_merge_blocks(gathered[0], SHARDED[1][1]))[0]
    return _make_linear_res("sb0_o", BF16)(x, o, slot_o, w_o), gathered


def _rest(x, slots, norms, wb, q_tabs, kv_tabs):
    kvp = kr = None
    for layer in range(DEPTH):
        if 0 < layer < N_A:
            qkv = _make_norm_linear(f"sb{layer}_qkv", BF16)(
                x, norms['attn_norm'][layer], slots['sb_w_qkv'][layer], wb['sb_w_qkv'][layer])
            x = _make_linear_res(f"sb{layer}_o", BF16)(
                x, sb_attention(qkv), slots['sb_w_o'][layer], wb['sb_w_o'][layer])
        elif layer >= N_A:
            j = layer - N_A
            if j == 0:
                pad = ((0, 0), (0, DKV_PAD - KV_RANK - ROPE))
                down = _make_norm_linear("kv_down")(
                    x, norms['kv_norm'], jnp.pad(slots['mla_w_dkv'][0], pad), jnp.pad(wb['mla_w_dkv'][0], pad))
                kvp = _make_norm_linear("kv_up")(
                    down[:, :KV_RANK], norms['mla_kv_lat_norm'],
                    _split_kv_heads(slots['mla_w_ukv'][0]), _split_kv_heads(wb['mla_w_ukv'][0]))
                k_rope = _make_rope("rope_k")(down, *kv_tabs)[:, KV_RANK:KV_RANK + ROPE]
                kr = jnp.pad(k_rope, ((0, 0), (NOPE, LANES - QK)))
            c_q = _make_norm_linear(f"mla{j}_dq")(
                x, norms['attn_norm'][layer], slots['mla_w_dq'][j], wb['mla_w_dq'][j])
            q = _make_norm_linear(f"mla{j}_uq")(
                c_q, norms['mla_q_lat_norm'][j], _pad_heads(slots['mla_w_uq'][j], QK), _pad_heads(wb['mla_w_uq'][j], QK))
            o = mla_attention(q, kvp, kr, *q_tabs)
            x = _make_linear_res(f"mla{j}_o")(x, o, slots['mla_w_o'][j], wb['mla_w_o'][j])
        x = _make_mlp_res(f"mlp{layer}")(
            x, norms['mlp_norm'][layer], slots['mlp_w1'][layer], slots['mlp_w2'][layer],
            wb['mlp_w1'][layer], wb['mlp_w2'][layer])
    return x


def _merge_blocks(gathered, ax):
    shp = gathered.shape[1:]
    return jnp.moveaxis(gathered, 0, ax).reshape(shp[:ax] + (N_DEV * shp[ax],) + shp[ax + 1:])


def _split_blocks(full, ax):
    shp = full.shape
    return jnp.moveaxis(full.reshape(shp[:ax] + (N_DEV, shp[ax] // N_DEV) + shp[ax + 1:]), ax, 0)


def _pack(parts):
    flat = jnp.concatenate([p.reshape(-1) for p in parts])
    rows = -(-flat.shape[0] // PACK_COLS)
    rows = -(-rows // PACK_ROW_ALIGN) * PACK_ROW_ALIGN
    return jnp.pad(flat, (0, rows * PACK_COLS - flat.shape[0])).reshape(rows, PACK_COLS)


def _unpack(packed, shapes):
    flat = packed.reshape(-1)
    out, off = [], 0
    for shp in shapes:
        n = math.prod(shp)
        out.append(flat[off:off + n].reshape(shp))
        off += n
    return out


def _add_round(x, stage, core, name):
    _, r, c = x.shape
    tr = _tile(r, (256, 128, 64, 32, 16, 8))

    def body(core_ref, x_ref, s_ref, o_ref):
        o_ref[...] = (x_ref[0] + s_ref[...]).astype(o_ref.dtype)

    return pl.pallas_call(
        body, name=name,
        grid_spec=pltpu.PrefetchScalarGridSpec(
            num_scalar_prefetch=1, grid=(r // tr,),
            in_specs=[pl.BlockSpec((1, tr, c), lambda i, core_ref: (core_ref[0], i, 0)),
                      pl.BlockSpec((tr, c), lambda i, core_ref: (i, 0))],
            out_specs=pl.BlockSpec((tr, c), lambda i, core_ref: (i, 0))),
        out_shape=jax.ShapeDtypeStruct((r, c), BF16),
        compiler_params=pltpu.CompilerParams(dimension_semantics=("parallel",)),
    )(core, x, stage)


def _adamw_reduce(parts, w, m, v, name):
    n_parts, r, c = parts.shape
    tr = _tile(r, (128, 64, 32, 16, 8))
    bias1 = 1.0 - ADAM_B1 ** ADAM_STEP
    bias2 = 1.0 - ADAM_B2 ** ADAM_STEP

    def body(p_ref, w_ref, m_ref, v_ref, g_ref, d_ref, nm_ref, nv_ref):
        g = p_ref[0].astype(F32)
        for s in range(1, n_parts):
            g = g + p_ref[s].astype(F32)
        mn = ADAM_B1 * m_ref[...] + (1.0 - ADAM_B1) * g
        vn = ADAM_B2 * v_ref[...] + (1.0 - ADAM_B2) * (g * g)
        m_hat = mn / bias1
        v_hat = vn / bias2
        g_ref[...] = g
        d_ref[...] = -ADAM_LR * (m_hat / (jnp.sqrt(v_hat) + ADAM_EPS) + ADAM_WD * w_ref[...])
        nm_ref[...] = mn
        nv_ref[...] = vn

    blk = pl.BlockSpec((tr, c), lambda i: (i, 0))
    return pl.pallas_call(
        body, name=name, grid=(r // tr,),
        in_specs=[pl.BlockSpec((n_parts, tr, c), lambda i: (0, i, 0)), blk, blk, blk],
        out_specs=[blk] * 4,
        out_shape=[jax.ShapeDtypeStruct((r, c), F32)] * 4,
        compiler_params=pltpu.CompilerParams(dimension_semantics=("parallel",), vmem_limit_bytes=VMEM_MM),
    )(parts, w, m, v)


def kernel(x, positions, attn_norm, mlp_norm, sb_w_qkv, sb_w_o, kv_norm, mla_w_dkv, mla_kv_lat_norm, mla_w_ukv, mla_w_dq, mla_q_lat_norm, mla_w_uq, mla_w_o, mlp_w1, mlp_w2, final_norm, loss_target, m_attn_norm, m_mlp_norm, m_sb_w_qkv, m_sb_w_o, m_kv_norm, m_mla_w_dkv, m_mla_kv_lat_norm, m_mla_w_ukv, m_mla_w_dq, m_mla_q_lat_norm, m_mla_w_uq, m_mla_w_o, m_mlp_w1, m_mlp_w2, m_final_norm, v_attn_norm, v_mlp_norm, v_sb_w_qkv, v_sb_w_o, v_kv_norm, v_mla_w_dkv, v_mla_kv_lat_norm, v_mla_w_ukv, v_mla_w_dq, v_mla_q_lat_norm, v_mla_w_uq, v_mla_w_o, v_mlp_w1, v_mlp_w2, v_final_norm):
    weights = dict(attn_norm=attn_norm, mlp_norm=mlp_norm, sb_w_qkv=sb_w_qkv, sb_w_o=sb_w_o, kv_norm=kv_norm,
                   mla_w_dkv=mla_w_dkv, mla_kv_lat_norm=mla_kv_lat_norm, mla_w_ukv=mla_w_ukv, mla_w_dq=mla_w_dq,
                   mla_q_lat_norm=mla_q_lat_norm, mla_w_uq=mla_w_uq, mla_w_o=mla_w_o, mlp_w1=mlp_w1, mlp_w2=mlp_w2,
                   final_norm=final_norm)
    mom_m = dict(attn_norm=m_attn_norm, mlp_norm=m_mlp_norm, sb_w_qkv=m_sb_w_qkv, sb_w_o=m_sb_w_o, kv_norm=m_kv_norm,
                 mla_w_dkv=m_mla_w_dkv, mla_kv_lat_norm=m_mla_kv_lat_norm, mla_w_ukv=m_mla_w_ukv, mla_w_dq=m_mla_w_dq,
                 mla_q_lat_norm=m_mla_q_lat_norm, mla_w_uq=m_mla_w_uq, mla_w_o=m_mla_w_o, mlp_w1=m_mlp_w1,
                 mlp_w2=m_mlp_w2, final_norm=m_final_norm)
    mom_v = dict(attn_norm=v_attn_norm, mlp_norm=v_mlp_norm, sb_w_qkv=v_sb_w_qkv, sb_w_o=v_sb_w_o, kv_norm=v_kv_norm,
                 mla_w_dkv=v_mla_w_dkv, mla_kv_lat_norm=v_mla_kv_lat_norm, mla_w_ukv=v_mla_w_ukv, mla_w_dq=v_mla_w_dq,
                 mla_q_lat_norm=v_mla_q_lat_norm, mla_w_uq=v_mla_w_uq, mla_w_o=v_mla_w_o, mlp_w1=v_mlp_w1,
                 mlp_w2=v_mlp_w2, final_norm=v_final_norm)
    sharded_names = [n for n, _ in SHARDED]
    repl_shapes = [tuple(weights[n].shape) for n in REPLICATED]

    first_name, first_ax = SHARDED[0]
    w_qkv = _layers(_merge_blocks(_all_gather([weights[first_name].astype(BF16)], "gather_qkv")[0], first_ax))
    shards = [weights[n].astype(BF16) for n in sharded_names[1:]]
    slots, layer_ax = {}, {}
    for n, ax in SHARDED:
        shp = weights[n].shape
        full = shp[:ax] + (N_DEV * shp[ax],) + shp[ax + 1:]
        n_layers, per_layer = (full[0], full[1:]) if len(full) == 3 else (1, full)
        slots[n] = [jnp.zeros(per_layer, F32) for _ in range(n_layers)]
        layer_ax[n] = ax - 1 if len(full) == 3 else ax
    norms = {n: weights[n] for n in REPLICATED if n != 'final_norm'}
    late = [('sb_w_qkv', 0), ('sb_w_o', 0)]
    early = [(n, l) for n in sharded_names for l in range(len(slots[n])) if (n, l) not in late]

    q_tabs, kv_tabs = _rope_tables(positions[0])
    side = {}
    x_mid, pull_first, gathered = jax.vjp(
        lambda xx, sq, so, nn: _first_attention(xx, sq, so, nn, w_qkv[0], shards, side),
        x[0], slots['sb_w_qkv'][0], slots['sb_w_o'][0], norms, has_aux=True)
    wb = {'sb_w_qkv': w_qkv}
    for (n, ax), g in zip(SHARDED[1:], gathered):
        wb[n] = _layers(_merge_blocks(g, ax))
    rest_slots = {n: [None if (n, l) in late else a for l, a in enumerate(v)] for n, v in slots.items()}
    x_last, pull_rest = jax.vjp(lambda xx, ss, nn: _rest(xx, ss, nn, wb, q_tabs, kv_tabs), x_mid, rest_slots, norms)
    loss_part, dx_last, d_final = _loss_head(x_last, final_norm, loss_target[0])
    loss = lax.psum(loss_part[0, 0], ("x", "y", "c"))

    core = lax.axis_index("c").astype(jnp.int32).reshape(1)

    def pair_sums(units, grads, tag):
        halves, dims = [], []
        for (n, l), g in zip(units, grads):
            blocks = _split_blocks(g, layer_ax[n])
            two_d = (math.prod(blocks.shape[1:-1]), blocks.shape[-1])
            dims.append(two_d)
            halves.append(jnp.moveaxis(blocks.reshape((N_DEV // 2, 2) + two_d), 1, 0))
        staged = _pair_exchange(halves, "pair_grads_" + tag)
        sums = []
        for (n, l), h, st, (r, c) in zip(units, halves, staged, dims):
            rows = (N_DEV // 2) * r
            sums.append(_add_round(h.reshape(2, rows, c), st.reshape(rows, c), core,
                                   f"pair_sum_{n}_{l}").reshape(st.shape))
        return sums

    dx_mid, d_rest, d_norms_rest = pull_rest(dx_last)
    side['sums'] = pair_sums(early, [d_rest[n][l] for n, l in early], "early")
    dx, d_qkv0, d_o0, d_norms_first = pull_first(dx_mid)
    received = dict(zip(early, side['received']))
    received.update(zip(late, _chip_exchange(pair_sums(late, [d_qkv0, d_o0], "late"), "scatter_late")))
    d_norms = {n: d_norms_rest[n] + d_norms_first[n] for n in norms}
    d_norms['final_norm'] = d_final
    repl_parts = _all_gather([_pack([d_norms[n] for n in REPLICATED])], "gather_norm_grads")[0]

    results = {kind: {} for kind in ("grad", "delta", "new_m", "new_v")}
    for n in sharded_names:
        shp = weights[n].shape
        two_d = (math.prod(shp[:-1]), shp[-1])
        parts = jnp.concatenate([received[(n, l)] for l in range(len(slots[n]))], axis=1)
        res = _adamw_reduce(parts, weights[n].reshape(two_d), mom_m[n].reshape(two_d),
                            mom_v[n].reshape(two_d), "adamw_" + n)
        for kind, a in zip(results, res):
            results[kind][n] = a.reshape(shp)
    res = _adamw_reduce(repl_parts, _pack([weights[n] for n in REPLICATED]), _pack([mom_m[n] for n in REPLICATED]),
                        _pack([mom_v[n] for n in REPLICATED]), "adamw_replicated")
    for kind, a in zip(results, res):
        results[kind].update(zip(REPLICATED, _unpack(a, repl_shapes)))

    out = [loss, dx[None]]
    for kind in ("grad", "delta", "new_m", "new_v"):
        out += [results[kind][n] for n in WEIGHT_ORDER]
    return tuple(out)
```

```python
import math

import jax
import jax.numpy as jnp
from jax import lax
from jax.experimental import pallas as pl
from jax.experimental.pallas import tpu as pltpu

F32 = jnp.float32
BF16 = jnp.bfloat16
MESH = pl.DeviceIdType.MESH

N_DEV = 8
DEPTH = 4
N_A = 2
SB_HEADS = 16
SB_HD = 64
MLA_HEADS = 16
NOPE = 64
ROPE = 32
VDIM = 64
QK = NOPE + ROPE
KV_RANK = 256
DKV_PAD = 384
LANES = 128
CHUNK_SHIFT = 6
ROPE_THETA = 10000.0
EPS = 1e-6
SB_BLOCK = 256
MLA_BLOCK = 512
MLA_GROUP = 4
MLA_UNROLL = 4
PACK_COLS = 1024
PACK_ROW_ALIGN = 16
EXP_ZERO = -104.0
NEG_BIG = -1e30
LOG2_E = 1.4426950408889634
LN_2 = 0.6931471805599453
VMEM_ATT = 56 * 1024 * 1024
VMEM_MM = 48 * 1024 * 1024

ADAM_LR = 0.001
ADAM_B1 = 0.9
ADAM_B2 = 0.999
ADAM_EPS = 1e-08
ADAM_WD = 0.01
ADAM_STEP = 10

WEIGHT_ORDER = ['attn_norm', 'mlp_norm', 'sb_w_qkv', 'sb_w_o', 'kv_norm', 'mla_w_dkv', 'mla_kv_lat_norm',
                'mla_w_ukv', 'mla_w_dq', 'mla_q_lat_norm', 'mla_w_uq', 'mla_w_o', 'mlp_w1', 'mlp_w2', 'final_norm']
SHARDED = [('sb_w_qkv', 2), ('sb_w_o', 1), ('mla_w_dkv', 0), ('mla_w_ukv', 1), ('mla_w_dq', 1),
           ('mla_w_uq', 2), ('mla_w_o', 1), ('mlp_w1', 2), ('mlp_w2', 1)]
REPLICATED = ['attn_norm', 'mlp_norm', 'kv_norm', 'mla_kv_lat_norm', 'mla_q_lat_norm', 'final_norm']


def _tile(dim, prefs=(512, 384, 256, 128, 64, 32, 16, 8)):
    for t in prefs:
        if dim % t == 0:
            return t
    return dim


def _dot(a, b, dims):
    return lax.dot_general(a, b, (dims, ((), ())), preferred_element_type=F32)


NN = ((1,), (0,))
NT = ((1,), (1,))
TN = ((0,), (0,))


def _gather_phases(x_refs, out_refs, send_sems, recv_sems, local_sems):
    n_t = len(x_refs)
    x, y, c = lax.axis_index("x"), lax.axis_index("y"), lax.axis_index("c")
    me, sibling = (x, y, c), (x, y, 1 - c)
    chips = [(1 - x, y), (x, 1 - y), (1 - x, 1 - y)]

    def slot(t, px, py, pc):
        return out_refs[t].at[4 * px + 2 * py + pc]

    def copy(t, k, block, to, src=None):
        return pltpu.make_async_remote_copy(
            src_ref=slot(t, *block) if src is None else src, dst_ref=slot(t, *block),
            send_sem=send_sems.at[7 * t + k], recv_sem=recv_sems.at[7 * t + k],
            device_id=to, device_id_type=MESH)

    def mine():
        return [pltpu.make_async_copy(x_refs[t], slot(t, *me), local_sems.at[t]) for t in range(n_t)]

    def first():
        out = []
        for t in range(n_t):
            out.append(copy(t, 0, me, sibling, src=x_refs[t]))
            out += [copy(t, 1 + j, me, (*chip, c), src=x_refs[t]) for j, chip in enumerate(chips)]
        return out

    def passed():
        return [copy(t, 4 + j, (*chip, c), sibling) for t in range(n_t) for j, chip in enumerate(chips)]

    def start():
        for cp in mine() + first():
            cp.start()

    def forward(which=None):
        onward = passed()
        for t in range(n_t) if which is None else which:
            for j, chip in enumerate(chips):
                copy(t, 1 + j, (*chip, c), me).wait_recv()
                onward[3 * t + j].start()

    def finish():
        for t in range(n_t):
            copy(t, 0, sibling, me).wait_recv()
            for j, chip in enumerate(chips):
                copy(t, 4 + j, (*chip, 1 - c), me).wait_recv()
        for cp in first() + passed():
            cp.wait_send()
        for cp in mine():
            cp.wait()

    return start, forward, finish


def _gather_scratch(n_t):
    return [pltpu.SemaphoreType.DMA((7 * n_t,)), pltpu.SemaphoreType.DMA((7 * n_t,)), pltpu.SemaphoreType.DMA((n_t,))]


def _all_gather(shards, name):
    n_t = len(shards)

    def body(*refs):
        for phase in _gather_phases(refs[:n_t], refs[n_t:2 * n_t], *refs[2 * n_t:]):
            phase()

    any_spec = pl.BlockSpec(memory_space=pl.ANY)
    return pl.pallas_call(
        body, name=name,
        out_shape=[jax.ShapeDtypeStruct((N_DEV,) + tuple(s.shape), s.dtype) for s in shards],
        in_specs=[any_spec] * n_t, out_specs=[any_spec] * n_t,
        scratch_shapes=_gather_scratch(n_t),
    )(*shards)


def _pair_exchange(xs, name):
    n_t = len(xs)

    def body(*refs):
        x_refs, out_refs = refs[:n_t], refs[n_t:2 * n_t]
        send_sems, recv_sems = refs[2 * n_t:]
        x, y, c = lax.axis_index("x"), lax.axis_index("y"), lax.axis_index("c")
        copies = [pltpu.make_async_remote_copy(
            src_ref=x_refs[t].at[1 - c], dst_ref=out_refs[t], send_sem=send_sems.at[t], recv_sem=recv_sems.at[t],
            device_id=(x, y, 1 - c), device_id_type=MESH) for t in range(n_t)]
        for cp in copies:
            cp.start()
        for cp in copies:
            cp.wait()

    any_spec = pl.BlockSpec(memory_space=pl.ANY)
    return pl.pallas_call(
        body, name=name,
        out_shape=[jax.ShapeDtypeStruct(x.shape[1:], x.dtype) for x in xs],
        in_specs=[any_spec] * n_t, out_specs=[any_spec] * n_t,
        scratch_shapes=[pltpu.SemaphoreType.DMA((n_t,)), pltpu.SemaphoreType.DMA((n_t,))],
    )(*xs)


def _exchange_phases(p_refs, out_refs, send_sems, recv_sems, local_sems):
    n_t = len(p_refs)
    mx, my, mc = lax.axis_index("x"), lax.axis_index("y"), lax.axis_index("c")
    me = 2 * mx + my

    def mine():
        return [pltpu.make_async_copy(p_refs[t].at[me], out_refs[t].at[me], local_sems.at[t]) for t in range(n_t)]

    def copies():
        out = []
        for k in range(1, 4):
            px = 1 - mx if (k >> 1) & 1 else mx
            py = 1 - my if k & 1 else my
            peer = 2 * px + py
            for t in range(n_t):
                out.append(pltpu.make_async_remote_copy(
                    src_ref=p_refs[t].at[peer], dst_ref=out_refs[t].at[me],
                    send_sem=send_sems.at[3 * t + k - 1], recv_sem=recv_sems.at[3 * t + k - 1],
                    device_id=(px, py, mc), device_id_type=MESH))
        return out

    def start():
        for cp in mine() + copies():
            cp.start()

    def finish():
        for cp in copies():
            cp.wait_send()
        for cp in copies():
            cp.wait_recv()
        for cp in mine():
            cp.wait()

    return start, finish


def _exchange_scratch(n_t):
    return [pltpu.SemaphoreType.DMA((3 * n_t,)), pltpu.SemaphoreType.DMA((3 * n_t,)), pltpu.SemaphoreType.DMA((n_t,))]


def _chip_exchange(ps, name):
    n_t = len(ps)

    def body(*refs):
        for phase in _exchange_phases(refs[:n_t], refs[n_t:2 * n_t], *refs[2 * n_t:]):
            phase()

    any_spec = pl.BlockSpec(memory_space=pl.ANY)
    return pl.pallas_call(
        body, name=name,
        out_shape=[jax.ShapeDtypeStruct(p.shape, p.dtype) for p in ps],
        in_specs=[any_spec] * n_t, out_specs=[any_spec] * n_t,
        scratch_shapes=_exchange_scratch(n_t),
    )(*ps)


def _matmul(a, b, *, ta=False, tb=False, out_dtypes=(F32,), epilogue=None, extras=(), name):
    if ta:
        kdim, m = a.shape
    else:
        m, kdim = a.shape
    if tb:
        n, kb = b.shape
    else:
        kb, n = b.shape
    assert kdim == kb, (a.shape, b.shape, ta, tb)
    big = (1024, 768, 512, 384, 256, 128, 64, 32, 16, 8)
    tm, tn = _tile(m, big), _tile(n, big)
    tk = kdim if kdim <= 2048 else _tile(kdim, (2048, 1024, 512, 256, 128))
    nk = kdim // tk
    n_extra, n_out = len(extras), len(out_dtypes)
    a_spec = pl.BlockSpec((tk, tm), lambda i, j, k: (k, i)) if ta else pl.BlockSpec((tm, tk), lambda i, j, k: (i, k))
    b_spec = pl.BlockSpec((tn, tk), lambda i, j, k: (j, k)) if tb else pl.BlockSpec((tk, tn), lambda i, j, k: (k, j))
    tile_spec = pl.BlockSpec((tm, tn), lambda i, j, k: (i, j))
    dims = ((0,) if ta else (1,), (1,) if tb else (0,))

    def finish(acc, extra_refs, out_refs):
        outs = (acc,) if epilogue is None else epilogue(acc, *[r[...] for r in extra_refs])
        for o_ref, o in zip(out_refs, outs):
            o_ref[...] = o.astype(o_ref.dtype)

    def body_one(a_ref, b_ref, *rest):
        acc = _dot(a_ref[...].astype(BF16), b_ref[...].astype(BF16), dims)
        finish(acc, rest[:n_extra], rest[n_extra:n_extra + n_out])

    def body_acc(a_ref, b_ref, *rest):
        acc_ref = rest[-1]
        k = pl.program_id(2)

        @pl.when(k == 0)
        def _():
            acc_ref[...] = jnp.zeros_like(acc_ref)

        acc_ref[...] += _dot(a_ref[...].astype(BF16), b_ref[...].astype(BF16), dims)

        @pl.when(k == nk - 1)
        def _():
            finish(acc_ref[...], rest[:n_extra], rest[n_extra:n_extra + n_out])

    return pl.pallas_call(
        body_one if nk == 1 else body_acc, name=name, grid=(m // tm, n // tn, nk),
        in_specs=[a_spec, b_spec] + [tile_spec] * n_extra,
        out_specs=[tile_spec] * n_out,
        out_shape=[jax.ShapeDtypeStruct((m, n), dt) for dt in out_dtypes],
        scratch_shapes=[] if nk == 1 else [pltpu.VMEM((tm, tn), F32)],
        compiler_params=pltpu.CompilerParams(
            dimension_semantics=("parallel", "parallel", "arbitrary"), vmem_limit_bytes=VMEM_MM),
    )(a, b, *extras)


def _rms_matmul(x, g, w, *, out_dtypes=(F32,), epilogue=None, name):
    m, d = x.shape
    n = w.shape[1]
    big = (1024, 768, 512, 384, 256, 128, 64, 32, 16, 8)
    tm, tn = _tile(m, big), _tile(n, big)
    n_out = len(out_dtypes)

    def body(x_ref, g_ref, w_ref, h_ref, *rest):
        out_refs, h_scr = rest[:n_out], rest[n_out]

        @pl.when(pl.program_id(1) == 0)
        def _():
            xv = x_ref[...]
            r = lax.rsqrt(jnp.mean(xv * xv, axis=-1, keepdims=True) + EPS)
            h_scr[...] = (xv * r * g_ref[...]).astype(h_scr.dtype)
            h_ref[...] = h_scr[...]

        acc = _dot(h_scr[...], w_ref[...], NN)
        outs = (acc,) if epilogue is None else epilogue(acc)
        for o_ref, o in zip(out_refs, outs):
            o_ref[...] = o.astype(o_ref.dtype)

    row_spec = pl.BlockSpec((tm, d), lambda i, j: (i, 0))
    tile_spec = pl.BlockSpec((tm, tn), lambda i, j: (i, j))
    return pl.pallas_call(
        body, name=name, grid=(m // tm, n // tn),
        in_specs=[row_spec, pl.BlockSpec((1, d), lambda i, j: (0, 0)), pl.BlockSpec((d, tn), lambda i, j: (0, j))],
        out_specs=[row_spec] + [tile_spec] * n_out,
        out_shape=[jax.ShapeDtypeStruct((m, d), BF16)] + [jax.ShapeDtypeStruct((m, n), dt) for dt in out_dtypes],
        scratch_shapes=[pltpu.VMEM((tm, d), BF16)],
        compiler_params=pltpu.CompilerParams(
            dimension_semantics=("parallel", "arbitrary"), vmem_limit_bytes=VMEM_MM),
    )(x, g.reshape(1, d), w)


def _matmul_rms_bwd(dy, w, x, g, name, res=None):
    m, n = dy.shape
    d = w.shape[0]
    tm = _tile(m, (512, 256, 128, 64, 32, 16, 8))
    tk = n if n <= 2048 else _tile(n, (2048, 1024, 512, 256, 128))
    nk = n // tk
    has_res = res is not None

    def body(dy_ref, w_ref, x_ref, g_ref, *rest):
        dx_ref, dg_ref, acc_ref = rest[-3:]
        i, k = pl.program_id(0), pl.program_id(1)

        @pl.when(k == 0)
        def _():
            acc_ref[...] = jnp.zeros_like(acc_ref)

        @pl.when(jnp.logical_and(i == 0, k == 0))
        def _():
            dg_ref[...] = jnp.zeros_like(dg_ref)

        acc_ref[...] += _dot(dy_ref[...], w_ref[...], NT)

        @pl.when(k == nk - 1)
        def _():
            dh = acc_ref[...]
            xv = x_ref[...]
            r = lax.rsqrt(jnp.mean(xv * xv, axis=-1, keepdims=True) + EPS)
            xh = xv * r
            t = dh * g_ref[...]
            dx = r * (t - xh * jnp.mean(t * xh, axis=-1, keepdims=True))
            dx_ref[...] = dx + rest[0][...] if has_res else dx
            dg_ref[...] += jnp.sum(dh * xh, axis=0, keepdims=True)

    row_spec = pl.BlockSpec((tm, d), lambda i, k: (i, 0))
    vec_spec = pl.BlockSpec((1, d), lambda i, k: (0, 0))
    dx, dg = pl.pallas_call(
        body, name=name, grid=(m // tm, nk),
        in_specs=[pl.BlockSpec((tm, tk), lambda i, k: (i, k)), pl.BlockSpec((d, tk), lambda i, k: (0, k)),
                  row_spec, vec_spec] + ([row_spec] if has_res else []),
        out_specs=[row_spec, vec_spec],
        out_shape=[jax.ShapeDtypeStruct((m, d), F32), jax.ShapeDtypeStruct((1, d), F32)],
        scratch_shapes=[pltpu.VMEM((tm, d), F32)],
        compiler_params=pltpu.CompilerParams(
            dimension_semantics=("arbitrary", "arbitrary"), vmem_limit_bytes=VMEM_MM),
    )(dy, w, x, g.reshape(1, d), *((res,) if has_res else ()))
    return dx, dg.reshape(d)


def _loss_head(x, g, target):
    m, d = x.shape
    tm = _tile(m, (512, 256, 128, 64, 32, 16, 8))

    def body(x_ref, g_ref, t_ref, loss_ref, dx_ref, dg_ref):
        xv = x_ref[...]
        gv = g_ref[...]
        r = lax.rsqrt(jnp.mean(xv * xv, axis=-1, keepdims=True) + EPS)
        xh = xv * r
        err = xh * gv - t_ref[...]
        row_loss = jnp.mean(err * err, axis=-1, keepdims=True)
        dyv = err * (1.0 / d)
        t = dyv * gv
        dx_ref[...] = r * (t - xh * jnp.mean(t * xh, axis=-1, keepdims=True))

        @pl.when(pl.program_id(0) == 0)
        def _():
            dg_ref[...] = jnp.zeros_like(dg_ref)
            loss_ref[...] = jnp.zeros_like(loss_ref)

        dg_ref[...] += jnp.sum(dyv * xh, axis=0, keepdims=True)
        loss_ref[...] += 0.5 * jnp.sum(row_loss, axis=0, keepdims=True)

    loss, dx, dg = pl.pallas_call(
        body, name="loss_head", grid=(m // tm,),
        in_specs=[pl.BlockSpec((tm, d), lambda i: (i, 0)), pl.BlockSpec((1, d), lambda i: (0, 0)),
                  pl.BlockSpec((tm, d), lambda i: (i, 0))],
        out_specs=[pl.BlockSpec((1, 1), lambda i: (0, 0)), pl.BlockSpec((tm, d), lambda i: (i, 0)),
                   pl.BlockSpec((1, d), lambda i: (0, 0))],
        out_shape=[jax.ShapeDtypeStruct((1, 1), F32), jax.ShapeDtypeStruct((m, d), F32),
                   jax.ShapeDtypeStruct((1, d), F32)],
        compiler_params=pltpu.CompilerParams(dimension_semantics=("arbitrary",)),
    )(x, g.reshape(1, d), target)
    return loss, dx, dg.reshape(d)


def _rope_apply(x, tabs, transpose, name):
    m, w = x.shape
    wt = tabs[0].shape[1]
    reps = w // wt
    half = ROPE // 2
    tm = _tile(m, (256, 128, 64, 32, 16, 8))

    def body(x_ref, c_ref, a_ref, b_ref, y_ref):
        xv = x_ref[...]

        def wide(t_ref):
            t = t_ref[...]
            return t if reps == 1 else jnp.concatenate([t] * reps, axis=1)

        c, a, b = wide(c_ref), wide(a_ref), wide(b_ref)
        if transpose:
            y = xv * c + pltpu.roll(xv * a, half, 1) + pltpu.roll(xv * b, w - half, 1)
        else:
            y = xv * c + pltpu.roll(xv, w - half, 1) * a + pltpu.roll(xv, half, 1) * b
        y_ref[...] = y

    x_spec = pl.BlockSpec((tm, w), lambda i: (i, 0))
    t_spec = pl.BlockSpec((tm, wt), lambda i: (i, 0))
    return pl.pallas_call(
        body, name=name, grid=(m // tm,),
        in_specs=[x_spec, t_spec, t_spec, t_spec], out_specs=x_spec,
        out_shape=jax.ShapeDtypeStruct((m, w), F32),
        compiler_params=pltpu.CompilerParams(dimension_semantics=("parallel",)),
    )(x, *tabs)


def _log_sigmoid_pair(z):
    a = jnp.minimum(z, 0.0) - jnp.log(1.0 + jnp.exp(-jnp.abs(z)))
    return a, a - z


def _split_bf16(x):
    hi = x.astype(BF16)
    return hi, (x - hi.astype(F32)).astype(BF16)


def _wide(v, width):
    return v if width == LANES else jnp.concatenate([v] * (width // LANES), axis=1)


def _lanes(col):
    return jnp.broadcast_to(col, (col.shape[0], LANES))


def _rope_head(x, c, a, b):
    half = ROPE // 2
    return x * c + pltpu.roll(x, LANES - half, 1) * a + pltpu.roll(x, half, 1) * b


def _rope_head_t(dy, c, a, b):
    half = ROPE // 2
    return dy * c + pltpu.roll(dy * a, half, 1) + pltpu.roll(dy * b, LANES - half, 1)


def _sb_weights(qh, k2, valid):
    a, b = _log_sigmoid_pair(_dot(qh, k2, NT))
    return a, b if valid is None else jnp.where(valid, b, 0.0)


def _sb_fwd(qkv, shards=()):
    s = qkv.shape[0]
    t = min(SB_BLOCK, s)
    nq = s // t
    npair = SB_HEADS // 2
    scale = SB_HD ** -0.5

    n_t = len(shards)

    def body(q_ref, k_ref, v_ref, *rest):
        x_refs, rest = rest[:n_t], rest[n_t:]
        o_ref, bta_ref, btb_ref, js_ref = rest[:4]
        g_refs, rest = rest[4:4 + n_t], rest[4 + n_t:]
        acc_ref, ra_ref, rb_ref = rest[:3]
        p, i = pl.program_id(0), pl.program_id(1)
        if n_t:
            start, forward, finish = _gather_phases(x_refs, g_refs, *rest[3:])
            pl.when(jnp.logical_and(p == 0, i == 0))(start)
            pl.when(jnp.logical_and(p == (5 * npair) // 8, i == 0))(lambda: forward(range(n_t - 1)))
            pl.when(jnp.logical_and(p == (7 * npair) // 8, i == 0))(lambda: forward([n_t - 1]))
        q2 = q_ref[...] * scale
        first = lax.broadcasted_iota(jnp.int32, (t, LANES), 1) < SB_HD
        heads = (jnp.where(first, q2, jnp.zeros_like(q2)), jnp.where(first, jnp.zeros_like(q2), q2))
        row = lax.broadcasted_iota(jnp.int32, (t, t), 0)
        col = lax.broadcasted_iota(jnp.int32, (t, t), 1)
        later = jnp.where(row > col, 1.0, 0.0).astype(BF16)
        acc_ref[...] = jnp.zeros_like(acc_ref)
        ra_ref[...] = jnp.zeros_like(ra_ref)
        rb_ref[...] = jnp.zeros_like(rb_ref)

        def block(j, diagonal):
            off = pl.multiple_of(j * t, t)
            k2 = k_ref[pl.ds(off, t), :]
            v2 = v_ref[pl.ds(off, t), :]
            valid = col < row if diagonal else None
            outs, rmax = [], None
            for qh, r_ref in zip(heads, (ra_ref, rb_ref)):
                a, b = _sb_weights(qh, k2, valid)
                bh, bl = _split_bf16(b)
                inner = _dot(bh, later, NN) + _dot(bl, later, NN)
                r = r_ref[...]
                w = jnp.exp(a + inner + _wide(r, t))
                if diagonal:
                    w = jnp.where(valid, w, 0.0)
                outs.append(_dot(w.astype(BF16), v2, NN))
                rn = r + jnp.sum(b, axis=1, keepdims=True)
                r_ref[...] = rn
                rmax = jnp.max(rn) if rmax is None else jnp.maximum(rmax, jnp.max(rn))
            acc_ref[...] += jnp.where(first, outs[0], outs[1])
            return rmax

        def cond(carry):
            j, rmax = carry
            return jnp.logical_and(j >= 0, rmax > EXP_ZERO)

        def step(carry):
            j, _ = carry
            return j - 1, block(j, False)

        def first_two():
            block(i, True)
            return block(i - 1, False)

        rmax = lax.cond(i >= 1, first_two, lambda: block(i, True))
        jend, _ = lax.while_loop(cond, step, (jnp.maximum(i - 2, -1), rmax))
        o_ref[...] = acc_ref[...].astype(o_ref.dtype)
        bta_ref[0] = ra_ref[...][:, :1]
        btb_ref[0] = rb_ref[...][:, :1]
        js_ref[p, i] = (jend + 1).astype(F32)
        if n_t:
            pl.when(jnp.logical_and(p == npair - 1, i == nq - 1))(finish)

    stat = pl.BlockSpec((1, t, 1), lambda p, i: (p, i, 0))
    any_spec = pl.BlockSpec(memory_space=pl.ANY)
    return pl.pallas_call(
        body, name="sb_attn_fwd_gather" if n_t else "sb_attn_fwd", grid=(npair, nq),
        in_specs=[pl.BlockSpec((t, LANES), lambda p, i: (i, p)),
                  pl.BlockSpec((s, LANES), lambda p, i: (0, npair + p)),
                  pl.BlockSpec((s, LANES), lambda p, i: (0, 2 * npair + p))] + [any_spec] * n_t,
        out_specs=[pl.BlockSpec((t, LANES), lambda p, i: (i, p)), stat, stat,
                   pl.BlockSpec(memory_space=pltpu.SMEM)] + [any_spec] * n_t,
        out_shape=[jax.ShapeDtypeStruct((s, SB_HEADS * SB_HD), BF16), jax.ShapeDtypeStruct((npair, s, 1), F32),
                   jax.ShapeDtypeStruct((npair, s, 1), F32), jax.ShapeDtypeStruct((npair, nq), F32)]
        + [jax.ShapeDtypeStruct((N_DEV,) + tuple(x.shape), x.dtype) for x in shards],
        scratch_shapes=[pltpu.VMEM((t, LANES), F32)] * 3 + (_gather_scratch(n_t) if n_t else []),
        compiler_params=pltpu.CompilerParams(
            dimension_semantics=("arbitrary", "arbitrary"), vmem_limit_bytes=VMEM_ATT),
    )(qkv, qkv, qkv, *shards)


def _sb_bwd(qkv, do, btot_a, btot_b, jstart, sums=()):
    s = qkv.shape[0]
    t = min(SB_BLOCK, s)
    nq = s // t
    npair = SB_HEADS // 2
    scale = SB_HD ** -0.5

    n_t = len(sums)

    def body(js_ref, q_ref, k_ref, v_ref, do_ref, bta_ref, btb_ref, *rest):
        s_refs, rest = rest[:n_t], rest[n_t:]
        dq_ref, dk_ref, dv_ref = rest[:3]
        r_refs, rest = rest[3:3 + n_t], rest[3 + n_t:]
        dq_acc, pa_ref, pb_ref, ga_ref, gb_ref, dk_acc, dv_acc = rest[:7]
        p, i = pl.program_id(0), pl.program_id(1)
        if n_t:
            start, finish = _exchange_phases(s_refs, r_refs, *rest[7:])
            pl.when(jnp.logical_and(p == 0, i == 0))(start)

        @pl.when(i == 0)
        def _():
            dk_acc[...] = jnp.zeros_like(dk_acc)
            dv_acc[...] = jnp.zeros_like(dv_acc)

        q2 = q_ref[...] * scale
        do2 = do_ref[...]
        first = lax.broadcasted_iota(jnp.int32, (t, LANES), 1) < SB_HD
        zero = jnp.zeros_like(q2)
        q_heads = (jnp.where(first, q2, zero), jnp.where(first, zero, q2))
        do_heads = (jnp.where(first, do2, zero), jnp.where(first, zero, do2))
        bts = (_wide(_lanes(bta_ref[0]), t), _wide(_lanes(btb_ref[0]), t))
        row = lax.broadcasted_iota(jnp.int32, (t, t), 0)
        col = lax.broadcasted_iota(jnp.int32, (t, t), 1)
        upto = jnp.where(row <= col, 1.0, 0.0).astype(BF16)
        before = jnp.where(row < col, 1.0, 0.0).astype(BF16)
        dq_acc[...] = jnp.zeros_like(dq_acc)
        for r in (pa_ref, pb_ref, ga_ref, gb_ref):
            r[...] = jnp.zeros_like(r)
        j0 = jnp.clip(js_ref[p, i].astype(jnp.int32), 0, i)

        def block(j, diagonal):
            off = pl.multiple_of(j * t, t)
            k2 = k_ref[pl.ds(off, t), :]
            v2 = v_ref[pl.ds(off, t), :]
            valid = col < row if diagonal else None
            dqs, dk2, dv2 = [], None, None
            for qh, doh, bt, p_ref, g_ref in zip(q_heads, do_heads, bts, (pa_ref, pb_ref), (ga_ref, gb_ref)):
                a, b = _sb_weights(qh, k2, valid)
                bh, bl = _split_bf16(b)
                pin = _dot(bh, upto, NN) + _dot(bl, upto, NN)
                surv = bt - (_wide(p_ref[...], t) + pin)
                w = jnp.exp(a + surv)
                if diagonal:
                    w = jnp.where(valid, w, 0.0)
                g = w * _dot(doh, v2, NT)
                gh, gl = _split_bf16(g)
                gsum = _wide(g_ref[...], t) + _dot(gh, before, NN) + _dot(gl, before, NN)
                beta = jnp.exp(a)
                dz = g * (1.0 - beta) - gsum * beta
                if diagonal:
                    dz = jnp.where(valid, dz, 0.0)
                dz = dz.astype(BF16)
                dqs.append(_dot(dz, k2, NN))
                dkh = _dot(dz, qh, TN)
                dvh = _dot(w.astype(BF16), doh, TN)
                dk2 = dkh if dk2 is None else dk2 + dkh
                dv2 = dvh if dv2 is None else dv2 + dvh
                p_ref[...] += jnp.sum(b, axis=1, keepdims=True)
                g_ref[...] += jnp.sum(g, axis=1, keepdims=True)
            dq_acc[...] += jnp.where(first, dqs[0], dqs[1])
            dk_acc[pl.ds(off, t), :] += dk2
            dv_acc[pl.ds(off, t), :] += dv2

        def step(j, carry):
            block(j, False)
            return carry

        lax.fori_loop(j0, jnp.maximum(i - 1, j0), step, 0)

        @pl.when(j0 < i)
        def _():
            block(i - 1, False)
            block(i, True)

        @pl.when(j0 >= i)
        def _():
            block(i, True)

        dq_ref[...] = (dq_acc[...] * scale).astype(dq_ref.dtype)

        @pl.when(i == nq - 1)
        def _():
            dk_ref[...] = dk_acc[...].astype(dk_ref.dtype)
            dv_ref[...] = dv_acc[...].astype(dv_ref.dtype)

        if n_t:
            pl.when(jnp.logical_and(p == npair - 1, i == nq - 1))(finish)

    blk = pl.BlockSpec((t, LANES), lambda p, i: (i, p))
    full = pl.BlockSpec((s, LANES), lambda p, i: (0, p))
    stat = pl.BlockSpec((1, t, 1), lambda p, i: (p, i, 0))
    vec = pltpu.VMEM((t, LANES), F32)
    any_spec = pl.BlockSpec(memory_space=pl.ANY)
    return pl.pallas_call(
        body, name="sb_attn_bwd_exchange" if n_t else "sb_attn_bwd", grid=(npair, nq),
        in_specs=[pl.BlockSpec(memory_space=pltpu.SMEM), blk,
                  pl.BlockSpec((s, LANES), lambda p, i: (0, npair + p)),
                  pl.BlockSpec((s, LANES), lambda p, i: (0, 2 * npair + p)), blk, stat, stat] + [any_spec] * n_t,
        out_specs=[blk, full, full] + [any_spec] * n_t,
        out_shape=[jax.ShapeDtypeStruct((s, SB_HEADS * SB_HD), BF16)] * 3
        + [jax.ShapeDtypeStruct(x.shape, x.dtype) for x in sums],
        scratch_shapes=[pltpu.VMEM((t, LANES), F32), vec, vec, vec, vec,
                        pltpu.VMEM((s, LANES), F32), pltpu.VMEM((s, LANES), F32)]
        + (_exchange_scratch(n_t) if n_t else []),
        compiler_params=pltpu.CompilerParams(
            dimension_semantics=("arbitrary", "arbitrary"), vmem_limit_bytes=VMEM_ATT),
    )(jstart, qkv, qkv, qkv, do, btot_a, btot_b, *sums)


def _make_sb_attention(side=None):
    @jax.custom_vjp
    def f(qkv, *shards):
        o, _, _, _, *gathered = _sb_fwd(qkv, shards)
        return (o, *gathered)

    def fwd(qkv, *shards):
        o, btot_a, btot_b, jstart, *gathered = _sb_fwd(qkv, shards)
        return (o, *gathered), (qkv, btot_a, btot_b, jstart, shards)

    def bwd(res, cts):
        qkv, btot_a, btot_b, jstart, shards = res
        sums = () if side is None else tuple(side['sums'])
        dq, dk, dv, *received = _sb_bwd(qkv, cts[0], btot_a, btot_b, jstart, sums)
        if side is not None:
            side['received'] = received
        return (jnp.concatenate([dq, dk, dv], axis=1),) + tuple(jnp.zeros_like(x) for x in shards)

    f.defvjp(fwd, bwd)
    return f


def sb_attention(qkv):
    return _make_sb_attention()(qkv)[0]


def _chunk_allowed(row, col):
    return (col >> CHUNK_SHIFT) <= (row >> CHUNK_SHIFT)


def _mla_fwd(q, k, vx, tabs):
    s = q.shape[0]
    h_ = q.shape[1] // LANES
    t = min(MLA_BLOCK, s)
    nq = s // t
    scale = QK ** -0.5
    hp = MLA_GROUP

    def body(q_ref, k_ref, v_ref, c_ref, a_ref, b_ref, o_ref, lse_ref, acc_ref, m_ref):
        i = pl.program_id(1)
        acc_ref[...] = jnp.zeros_like(acc_ref)
        m_ref[...] = jnp.full_like(m_ref, NEG_BIG)
        rot = (c_ref[...], a_ref[...], b_ref[...])
        qs = [_rope_head(q_ref[:, hh * LANES:(hh + 1) * LANES], *rot).astype(BF16) for hh in range(hp)]

        def tile(j, diagonal):
            off = pl.multiple_of(j * t, t)
            for hh in range(hp):
                lanes = slice(hh * LANES, (hh + 1) * LANES)
                kj = k_ref[pl.ds(off, t), lanes]
                vj = v_ref[pl.ds(off, t), lanes]
                sc = _dot(qs[hh], kj, NT) * (scale * LOG2_E)
                if diagonal:
                    row = lax.broadcasted_iota(jnp.int32, (t, t), 0)
                    col = lax.broadcasted_iota(jnp.int32, (t, t), 1)
                    sc = jnp.where(_chunk_allowed(row, col), sc, NEG_BIG)
                m_old = m_ref[hh]
                m_new = jnp.maximum(m_old, jnp.max(sc, axis=1, keepdims=True))
                p = jnp.exp2(sc - _wide(m_new, t))
                acc_ref[hh] = jnp.exp2(m_old - m_new) * acc_ref[hh] + _dot(p.astype(BF16), vj, NN)
                m_ref[hh] = m_new

        def step(jj, carry):
            tile(2 * jj, False)
            tile(2 * jj + 1, False)
            return carry

        lax.fori_loop(0, i // 2, step, 0)

        @pl.when(i % 2 == 1)
        def _():
            tile(i - 1, False)

        tile(i, True)
        first = lax.broadcasted_iota(jnp.int32, (t, LANES), 1) < VDIM
        outs = []
        for hh in range(hp):
            acc = acc_ref[hh]
            den = acc[:, VDIM:VDIM + 1]
            outs.append(acc / den)
            lse_ref[hh] = (m_ref[hh][:, :1] + jnp.log2(den)) * LN_2
        for pp in range(hp // 2):
            o_ref[:, pp * LANES:(pp + 1) * LANES] = jnp.where(
                first, outs[2 * pp], pltpu.roll(outs[2 * pp + 1], VDIM, 1))

    wide = hp * LANES
    return pl.pallas_call(
        body, name="mla_attn_fwd", grid=(h_ // hp, nq),
        in_specs=[pl.BlockSpec((t, wide), lambda g, i: (i, g)),
                  pl.BlockSpec((s, wide), lambda g, i: (0, g)),
                  pl.BlockSpec((s, wide), lambda g, i: (0, g))] + [pl.BlockSpec((t, LANES), lambda g, i: (i, 0))] * 3,
        out_specs=[pl.BlockSpec((t, hp * VDIM), lambda g, i: (i, g)),
                   pl.BlockSpec((hp, t, 1), lambda g, i: (g, i, 0))],
        out_shape=[jax.ShapeDtypeStruct((s, h_ * VDIM), F32), jax.ShapeDtypeStruct((h_, s, 1), F32)],
        scratch_shapes=[pltpu.VMEM((hp, t, LANES), F32), pltpu.VMEM((hp, t, LANES), F32)],
        compiler_params=pltpu.CompilerParams(
            dimension_semantics=("arbitrary", "arbitrary"), vmem_limit_bytes=VMEM_ATT),
    )(q, k, vx, *tabs)


def _mla_bwd(q, k, vx, tabs, o, lse, do):
    s = q.shape[0]
    h_ = q.shape[1] // LANES
    t = min(MLA_BLOCK, s)
    nq = s // t
    scale = QK ** -0.5

    def body(q_ref, k_ref, v_ref, c_ref, a_ref, b_ref, o_ref, lse_ref, do_ref, dq_ref, dk_ref, dv_ref, dq_acc):
        h, i = pl.program_id(0), pl.program_id(1)
        rot = (c_ref[...], a_ref[...], b_ref[...])

        @pl.when(i == 0)
        def _():
            dk_ref[...] = jnp.zeros_like(dk_ref)
            dv_ref[...] = jnp.zeros_like(dv_ref)

        qv = _rope_head(q_ref[...], *rot).astype(BF16)
        lane = lax.broadcasted_iota(jnp.int32, (t, LANES), 1)
        start = (h % 2) * VDIM
        mine = jnp.logical_and(lane >= start, lane < start + VDIM)
        do2 = do_ref[...]
        delta = _lanes(jnp.sum(jnp.where(mine, do2 * o_ref[...], 0.0), axis=1, keepdims=True))
        odd = (start + jnp.zeros_like(lane)) > 0
        do_head = jnp.where(odd, pltpu.roll(do2, VDIM, 1), do2)
        dov = jnp.where(lane < VDIM, do_head, 0.0).astype(BF16)
        lse = _lanes(lse_ref[0])
        dq_acc[...] = jnp.zeros_like(dq_acc)

        def tile(j, diagonal):
            off = pl.multiple_of(j * t, t)
            kj = k_ref[pl.ds(off, t), :]
            vj = v_ref[pl.ds(off, t), :]
            p = jnp.exp(_dot(qv, kj, NT) * scale - _wide(lse, t))
            if diagonal:
                row = lax.broadcasted_iota(jnp.int32, (t, t), 0)
                col = lax.broadcasted_iota(jnp.int32, (t, t), 1)
                p = jnp.where(_chunk_allowed(row, col), p, 0.0)
            dp = _dot(dov, vj, NT)
            ds = (p * (dp - _wide(delta, t)) * scale).astype(BF16)
            dq_acc[...] += _dot(ds, kj, NN)
            dk_ref[pl.ds(off, t), :] += _dot(ds, qv, TN)
            dv_ref[pl.ds(off, t), :] += _dot(p.astype(BF16), dov, TN)

        def step(jj, carry):
            for u in range(MLA_UNROLL):
                tile(MLA_UNROLL * jj + u, False)
            return carry

        def single(j, carry):
            tile(j, False)
            return carry

        lax.fori_loop(0, i // MLA_UNROLL, step, 0)
        lax.fori_loop(i - i % MLA_UNROLL, i, single, 0)
        tile(i, True)
        dq_ref[...] = _rope_head_t(dq_acc[...], *rot)

    blk = pl.BlockSpec((t, LANES), lambda h, i: (i, h))
    full = pl.BlockSpec((s, LANES), lambda h, i: (0, h))
    pair = pl.BlockSpec((t, LANES), lambda h, i: (i, h // 2))
    stat = pl.BlockSpec((1, t, 1), lambda h, i: (h, i, 0))
    return pl.pallas_call(
        body, name="mla_attn_bwd", grid=(h_, nq),
        in_specs=[blk, full, full] + [pl.BlockSpec((t, LANES), lambda h, i: (i, 0))] * 3 + [pair, stat, pair],
        out_specs=[blk, full, full],
        out_shape=[jax.ShapeDtypeStruct((s, h_ * LANES), F32)] * 3,
        scratch_shapes=[pltpu.VMEM((t, LANES), F32)],
        compiler_params=pltpu.CompilerParams(
            dimension_semantics=("arbitrary", "arbitrary"), vmem_limit_bytes=VMEM_ATT),
    )(q, k, vx, *tabs, o, lse, do)


def _assemble_kv(kvp, kr):
    half = kvp.shape[1] // 2
    ones = jnp.tile((jnp.arange(LANES) == VDIM).astype(F32), MLA_HEADS)
    k = kvp[:, :half] + jnp.tile(kr, (1, MLA_HEADS))
    return k.astype(BF16), (kvp[:, half:] + ones).astype(BF16)


@jax.custom_vjp
def mla_attention(q, kvp, kr, c, a, b):
    return _mla_fwd(q, *_assemble_kv(kvp, kr), (c, a, b))[0]


def _mla_attention_fwd(q, kvp, kr, c, a, b):
    kb, vb = _assemble_kv(kvp, kr)
    o, lse = _mla_fwd(q, kb, vb, (c, a, b))
    return o, (q, kb, vb, (c, a, b), o, lse)


def _mla_attention_bwd(res, do):
    q, kb, vb, tabs, o, lse = res
    dq, dk, dvx = _mla_bwd(q, kb, vb, tabs, o, lse, do)
    dkr = dk.reshape(dk.shape[0], MLA_HEADS, LANES).sum(axis=1)
    return (dq, jnp.concatenate([dk, dvx], axis=1), dkr) + tuple(jnp.zeros_like(t) for t in tabs)


mla_attention.defvjp(_mla_attention_fwd, _mla_attention_bwd)


def _add_tile(acc, res):
    return (acc + res,)


def _make_linear_res(tag, a_dtype=F32):
    def forward(x, a, slot, wb):
        ab = a.astype(BF16)
        y = _matmul(ab, wb, epilogue=_add_tile, extras=(x,), name=tag + "_fwd")[0]
        return y, (ab, wb)

    @jax.custom_vjp
    def f(x, a, slot, wb):
        return forward(x, a, slot, wb)[0]

    def bwd(res, dy):
        ab, wb = res
        dyb = dy.astype(BF16)
        da = _matmul(dyb, wb, tb=True, out_dtypes=(a_dtype,), name=tag + "_da")[0]
        dw = _matmul(ab, dyb, ta=True, name=tag + "_dw")[0]
        return dy, da, dw, jnp.zeros_like(wb)

    f.defvjp(forward, bwd)
    return f


def _make_norm_linear(tag, out_dtype=F32):
    def forward(x, g, slot, wb):
        hb, y = _rms_matmul(x, g, wb, out_dtypes=(out_dtype,), name=tag + "_norm_fwd")
        return y, (x, g, wb, hb)

    @jax.custom_vjp
    def f(x, g, slot, wb):
        return forward(x, g, slot, wb)[0]

    def bwd(res, dy):
        x, g, wb, hb = res
        dyb = dy.astype(BF16)
        dw = _matmul(hb, dyb, ta=True, name=tag + "_dw")[0]
        dx, dg = _matmul_rms_bwd(dyb, wb, x, g, tag + "_dh_norm_bwd")
        return dx, dg, dw, jnp.zeros_like(wb)

    f.defvjp(forward, bwd)
    return f


def _relu2_fwd(acc):
    r = jnp.maximum(acc, 0.0)
    return acc, r * r


def _relu2_bwd(acc, u):
    return (acc * (2.0 * jnp.maximum(u.astype(F32), 0.0)),)


def _make_mlp_res(tag):
    def forward(x, g, slot1, slot2, w1b, w2b):
        hb, u, act = _rms_matmul(x, g, w1b, out_dtypes=(BF16, BF16), epilogue=_relu2_fwd, name=tag + "_norm_up")
        y = _matmul(act, w2b, epilogue=_add_tile, extras=(x,), name=tag + "_down")[0]
        return y, (x, g, w1b, w2b, hb, u, act)

    @jax.custom_vjp
    def f(x, g, slot1, slot2, w1b, w2b):
        return forward(x, g, slot1, slot2, w1b, w2b)[0]

    def bwd(res, dy):
        x, g, w1b, w2b, hb, u, act = res
        dyb = dy.astype(BF16)
        du = _matmul(dyb, w2b, tb=True, out_dtypes=(BF16,), epilogue=_relu2_bwd, extras=(u,), name=tag + "_du")[0]
        dw2 = _matmul(act, dyb, ta=True, name=tag + "_dw2")[0]
        dw1 = _matmul(hb, du, ta=True, name=tag + "_dw1")[0]
        dx, dg = _matmul_rms_bwd(du, w1b, x, g, tag + "_dh_norm_bwd", res=dy)
        return dx, dg, dw1, dw2, jnp.zeros_like(w1b), jnp.zeros_like(w2b)

    f.defvjp(forward, bwd)
    return f


def _make_rope(tag):
    @jax.custom_vjp
    def f(x, c, a, b):
        return _rope_apply(x, (c, a, b), False, tag + "_fwd")

    def fwd(x, c, a, b):
        return _rope_apply(x, (c, a, b), False, tag + "_fwd"), (c, a, b)

    def bwd(res, dy):
        c, a, b = res
        return _rope_apply(dy, (c, a, b), True, tag + "_bwd"), jnp.zeros_like(c), jnp.zeros_like(a), jnp.zeros_like(b)

    f.defvjp(fwd, bwd)
    return f


def _rope_tables(positions):
    half = ROPE // 2
    inv_freq = ROPE_THETA ** (-jnp.arange(0, ROPE, 2, dtype=F32) / ROPE)
    ang = positions.astype(F32)[:, None] * inv_freq
    cos, sin = jnp.cos(ang), jnp.sin(ang)
    s = positions.shape[0]
    one = lambda n: jnp.ones((s, n), F32)
    zero = lambda n: jnp.zeros((s, n), F32)

    def tables(before, after):
        return (jnp.concatenate([one(before), cos, cos, one(after)], axis=1),
                jnp.concatenate([zero(before), -sin, zero(half + after)], axis=1),
                jnp.concatenate([zero(before + half), sin, zero(after)], axis=1))

    return tables(NOPE, LANES - QK), tables(KV_RANK, DKV_PAD - KV_RANK - ROPE)


def _pad_heads(w, per):
    lead = w.shape[:-1]
    w = jnp.pad(w.reshape(lead + (MLA_HEADS, per)), [(0, 0)] * len(lead) + [(0, 0), (0, LANES - per)])
    return w.reshape(lead + (MLA_HEADS * LANES,))


def _split_kv_heads(w):
    w3 = w.reshape(w.shape[0], MLA_HEADS, NOPE + VDIM)
    return jnp.concatenate([_pad_heads(w3[..., :NOPE].reshape(w.shape[0], -1), NOPE),
                            _pad_heads(w3[..., NOPE:].reshape(w.shape[0], -1), VDIM)], axis=1)


def _layers(full):
    return [full[l] for l in range(full.shape[0])] if full.ndim == 3 else [full]


def _first_attention(x, slot_qkv, slot_o, norms, w_qkv0, shards, side):
    qkv = _make_norm_linear("sb0_qkv", BF16)(x, norms['attn_norm'][0], slot_qkv, w_qkv0)
    o, *gathered = _make_sb_attention(side)(qkv, *shards)
    w_o = _layers(_merge_blocks(gathered[0], SHARDED[1][1]))[0]
    return _make_linear_res("sb0_o", BF16)(x, o, slot_o, w_o), gathered


def _rest(x, slots, norms, wb, q_tabs, kv_tabs):
    kvp = kr = None
    for layer in range(DEPTH):
        if 0 < layer < N_A:
            qkv = _make_norm_linear(f"sb{layer}_qkv", BF16)(
                x, norms['attn_norm'][layer], slots['sb_w_qkv'][layer], wb['sb_w_qkv'][layer])
            x = _make_linear_res(f"sb{layer}_o", BF16)(
                x, sb_attention(qkv), slots['sb_w_o'][layer], wb['sb_w_o'][layer])
        elif layer >= N_A:
            j = layer - N_A
            if j == 0:
                pad = ((0, 0), (0, DKV_PAD - KV_RANK - ROPE))
                down = _make_norm_linear("kv_down")(
                    x, norms['kv_norm'], jnp.pad(slots['mla_w_dkv'][0], pad), jnp.pad(wb['mla_w_dkv'][0], pad))
                kvp = _make_norm_linear("kv_up")(
                    down[:, :KV_RANK], norms['mla_kv_lat_norm'],
                    _split_kv_heads(slots['mla_w_ukv'][0]), _split_kv_heads(wb['mla_w_ukv'][0]))
                k_rope = _make_rope("rope_k")(down, *kv_tabs)[:, KV_RANK:KV_RANK + ROPE]
                kr = jnp.pad(k_rope, ((0, 0), (NOPE, LANES - QK)))
            c_q = _make_norm_linear(f"mla{j}_dq")(
                x, norms['attn_norm'][layer], slots['mla_w_dq'][j], wb['mla_w_dq'][j])
            q = _make_norm_linear(f"mla{j}_uq")(
                c_q, norms['mla_q_lat_norm'][j], _pad_heads(slots['mla_w_uq'][j], QK), _pad_heads(wb['mla_w_uq'][j], QK))
            o = mla_attention(q, kvp, kr, *q_tabs)
            x = _make_linear_res(f"mla{j}_o")(x, o, slots['mla_w_o'][j], wb['mla_w_o'][j])
        x = _make_mlp_res(f"mlp{layer}")(
            x, norms['mlp_norm'][layer], slots['mlp_w1'][layer], slots['mlp_w2'][layer],
            wb['mlp_w1'][layer], wb['mlp_w2'][layer])
    return x


def _merge_blocks(gathered, ax):
    shp = gathered.shape[1:]
    return jnp.moveaxis(gathered, 0, ax).reshape(shp[:ax] + (N_DEV * shp[ax],) + shp[ax + 1:])


def _split_blocks(full, ax):
    shp = full.shape
    return jnp.moveaxis(full.reshape(shp[:ax] + (N_DEV, shp[ax] // N_DEV) + shp[ax + 1:]), ax, 0)


def _pack(parts):
    flat = jnp.concatenate([p.reshape(-1) for p in parts])
    rows = -(-flat.shape[0] // PACK_COLS)
    rows = -(-rows // PACK_ROW_ALIGN) * PACK_ROW_ALIGN
    return jnp.pad(flat, (0, rows * PACK_COLS - flat.shape[0])).reshape(rows, PACK_COLS)


def _unpack(packed, shapes):
    flat = packed.reshape(-1)
    out, off = [], 0
    for shp in shapes:
        n = math.prod(shp)
        out.append(flat[off:off + n].reshape(shp))
        off += n
    return out


def _add_round(x, stage, core, name):
    _, r, c = x.shape
    tr = _tile(r, (256, 128, 64, 32, 16, 8))

    def body(core_ref, x_ref, s_ref, o_ref):
        o_ref[...] = (x_ref[0] + s_ref[...]).astype(o_ref.dtype)

    return pl.pallas_call(
        body, name=name,
        grid_spec=pltpu.PrefetchScalarGridSpec(
            num_scalar_prefetch=1, grid=(r // tr,),
            in_specs=[pl.BlockSpec((1, tr, c), lambda i, core_ref: (core_ref[0], i, 0)),
                      pl.BlockSpec((tr, c), lambda i, core_ref: (i, 0))],
            out_specs=pl.BlockSpec((tr, c), lambda i, core_ref: (i, 0))),
        out_shape=jax.ShapeDtypeStruct((r, c), BF16),
        compiler_params=pltpu.CompilerParams(dimension_semantics=("parallel",)),
    )(core, x, stage)


def _adamw_reduce(parts, w, m, v, name):
    n_parts, r, c = parts.shape
    tr = _tile(r, (128, 64, 32, 16, 8))
    bias1 = 1.0 - ADAM_B1 ** ADAM_STEP
    bias2 = 1.0 - ADAM_B2 ** ADAM_STEP

    def body(p_ref, w_ref, m_ref, v_ref, g_ref, d_ref, nm_ref, nv_ref):
        g = p_ref[0].astype(F32)
        for s in range(1, n_parts):
            g = g + p_ref[s].astype(F32)
        mn = ADAM_B1 * m_ref[...] + (1.0 - ADAM_B1) * g
        vn = ADAM_B2 * v_ref[...] + (1.0 - ADAM_B2) * (g * g)
        m_hat = mn / bias1
        v_hat = vn / bias2
        g_ref[...] = g
        d_ref[...] = -ADAM_LR * (m_hat / (jnp.sqrt(v_hat) + ADAM_EPS) + ADAM_WD * w_ref[...])
        nm_ref[...] = mn
        nv_ref[...] = vn

    blk = pl.BlockSpec((tr, c), lambda i: (i, 0))
    return pl.pallas_call(
        body, name=name, grid=(r // tr,),
        in_specs=[pl.BlockSpec((n_parts, tr, c), lambda i: (0, i, 0)), blk, blk, blk],
        out_specs=[blk] * 4,
        out_shape=[jax.ShapeDtypeStruct((r, c), F32)] * 4,
        compiler_params=pltpu.CompilerParams(dimension_semantics=("parallel",), vmem_limit_bytes=VMEM_MM),
    )(parts, w, m, v)


def kernel(x, positions, attn_norm, mlp_norm, sb_w_qkv, sb_w_o, kv_norm, mla_w_dkv, mla_kv_lat_norm, mla_w_ukv, mla_w_dq, mla_q_lat_norm, mla_w_uq, mla_w_o, mlp_w1, mlp_w2, final_norm, loss_target, m_attn_norm, m_mlp_norm, m_sb_w_qkv, m_sb_w_o, m_kv_norm, m_mla_w_dkv, m_mla_kv_lat_norm, m_mla_w_ukv, m_mla_w_dq, m_mla_q_lat_norm, m_mla_w_uq, m_mla_w_o, m_mlp_w1, m_mlp_w2, m_final_norm, v_attn_norm, v_mlp_norm, v_sb_w_qkv, v_sb_w_o, v_kv_norm, v_mla_w_dkv, v_mla_kv_lat_norm, v_mla_w_ukv, v_mla_w_dq, v_mla_q_lat_norm, v_mla_w_uq, v_mla_w_o, v_mlp_w1, v_mlp_w2, v_final_norm):
    weights = dict(attn_norm=attn_norm, mlp_norm=mlp_norm, sb_w_qkv=sb_w_qkv, sb_w_o=sb_w_o, kv_norm=kv_norm,
                   mla_w_dkv=mla_w_dkv, mla_kv_lat_norm=mla_kv_lat_norm, mla_w_ukv=mla_w_ukv, mla_w_dq=mla_w_dq,
                   mla_q_lat_norm=mla_q_lat_norm, mla_w_uq=mla_w_uq, mla_w_o=mla_w_o, mlp_w1=mlp_w1, mlp_w2=mlp_w2,
                   final_norm=final_norm)
    mom_m = dict(attn_norm=m_attn_norm, mlp_norm=m_mlp_norm, sb_w_qkv=m_sb_w_qkv, sb_w_o=m_sb_w_o, kv_norm=m_kv_norm,
                 mla_w_dkv=m_mla_w_dkv, mla_kv_lat_norm=m_mla_kv_lat_norm, mla_w_ukv=m_mla_w_ukv, mla_w_dq=m_mla_w_dq,
                 mla_q_lat_norm=m_mla_q_lat_norm, mla_w_uq=m_mla_w_uq, mla_w_o=m_mla_w_o, mlp_w1=m_mlp_w1,
                 mlp_w2=m_mlp_w2, final_norm=m_final_norm)
    mom_v = dict(attn_norm=v_attn_norm, mlp_norm=v_mlp_norm, sb_w_qkv=v_sb_w_qkv, sb_w_o=v_sb_w_o, kv_norm=v_kv_norm,
                 mla_w_dkv=v_mla_w_dkv, mla_kv_lat_norm=v_mla_kv_lat_norm, mla_w_ukv=v_mla_w_ukv, mla_w_dq=v_mla_w_dq,
                 mla_q_lat_norm=v_mla_q_lat_norm, mla_w_uq=v_mla_w_uq, mla_w_o=v_mla_w_o, mlp_w1=v_mlp_w1,
                 mlp_w2=v_mlp_w2, final_norm=v_final_norm)
    sharded_names = [n for n, _ in SHARDED]
    repl_shapes = [tuple(weights[n].shape) for n in REPLICATED]

    first_name, first_ax = SHARDED[0]
    w_qkv = _layers(_merge_blocks(_all_gather([weights[first_name].astype(BF16)], "gather_qkv")[0], first_ax))
    shards = [weights[n].astype(BF16) for n in sharded_names[1:]]
    slots, layer_ax = {}, {}
    for n, ax in SHARDED:
        shp = weights[n].shape
        full = shp[:ax] + (N_DEV * shp[ax],) + shp[ax + 1:]
        n_layers, per_layer = (full[0], full[1:]) if len(full) == 3 else (1, full)
        slots[n] = [jnp.zeros(per_layer, F32) for _ in range(n_layers)]
        layer_ax[n] = ax - 1 if len(full) == 3 else ax
    norms = {n: weights[n] for n in REPLICATED if n != 'final_norm'}
    late = [('sb_w_qkv', 0), ('sb_w_o', 0)]
    early = [(n, l) for n in sharded_names for l in range(len(slots[n])) if (n, l) not in late]

    q_tabs, kv_tabs = _rope_tables(positions[0])
    side = {}
    x_mid, pull_first, gathered = jax.vjp(
        lambda xx, sq, so, nn: _first_attention(xx, sq, so, nn, w_qkv[0], shards, side),
        x[0], slots['sb_w_qkv'][0], slots['sb_w_o'][0], norms, has_aux=True)
    wb = {'sb_w_qkv': w_qkv}
    for (n, ax), g in zip(SHARDED[1:], gathered):
        wb[n] = _layers(_merge_blocks(g, ax))
    rest_slots = {n: [None if (n, l) in late else a for l, a in enumerate(v)] for n, v in slots.items()}
    x_last, pull_rest = jax.vjp(lambda xx, ss, nn: _rest(xx, ss, nn, wb, q_tabs, kv_tabs), x_mid, rest_slots, norms)
    loss_part, dx_last, d_final = _loss_head(x_last, final_norm, loss_target[0])
    loss = lax.psum(loss_part[0, 0], ("x", "y", "c"))

    core = lax.axis_index("c").astype(jnp.int32).reshape(1)

    def pair_sums(units, grads, tag):
        halves, dims = [], []
        for (n, l), g in zip(units, grads):
            blocks = _split_blocks(g, layer_ax[n])
            two_d = (math.prod(blocks.shape[1:-1]), blocks.shape[-1])
            dims.append(two_d)
            halves.append(jnp.moveaxis(blocks.reshape((N_DEV // 2, 2) + two_d), 1, 0))
        staged = _pair_exchange(halves, "pair_grads_" + tag)
        sums = []
        for (n, l), h, st, (r, c) in zip(units, halves, staged, dims):
            rows = (N_DEV // 2) * r
            sums.append(_add_round(h.reshape(2, rows, c), st.reshape(rows, c), core,
                                   f"pair_sum_{n}_{l}").reshape(st.shape))
        return sums

    dx_mid, d_rest, d_norms_rest = pull_rest(dx_last)
    side['sums'] = pair_sums(early, [d_rest[n][l] for n, l in early], "early")
    dx, d_qkv0, d_o0, d_norms_first = pull_first(dx_mid)
    received = dict(zip(early, side['received']))
    received.update(zip(late, _chip_exchange(pair_sums(late, [d_qkv0, d_o0], "late"), "scatter_late")))
    d_norms = {n: d_norms_rest[n] + d_norms_first[n] for n in norms}
    d_norms['final_norm'] = d_final
    repl_parts = _all_gather([_pack([d_norms[n] for n in REPLICATED])], "gather_norm_grads")[0]

    results = {kind: {} for kind in ("grad", "delta", "new_m", "new_v")}
    for n in sharded_names:
        shp = weights[n].shape
        two_d = (math.prod(shp[:-1]), shp[-1])
        parts = jnp.concatenate([received[(n, l)] for l in range(len(slots[n]))], axis=1)
        res = _adamw_reduce(parts, weights[n].reshape(two_d), mom_m[n].reshape(two_d),
                            mom_v[n].reshape(two_d), "adamw_" + n)
        for kind, a in zip(results, res):
            results[kind][n] = a.reshape(shp)
    res = _adamw_reduce(repl_parts, _pack([weights[n] for n in REPLICATED]), _pack([mom_m[n] for n in REPLICATED]),
                        _pack([mom_v[n] for n in REPLICATED]), "adamw_replicated")
    for kind, a in zip(results, res):
        results[kind].update(zip(REPLICATED, _unpack(a, repl_shapes)))

    out = [loss, dx[None]]
    for kind in ("grad", "delta", "new_m", "new_v"):
        out += [results[kind][n] for n in WEIGHT_ORDER]
    return tuple(out)
```

```python
import math

import jax
import jax.numpy as jnp
from jax import lax
from jax.experimental import pallas as pl
from jax.experimental.pallas import tpu as pltpu

F32 = jnp.float32
BF16 = jnp.bfloat16
MESH = pl.DeviceIdType.MESH

N_DEV = 8
DEPTH = 4
N_A = 2
SB_HEADS = 16
SB_HD = 64
MLA_HEADS = 16
NOPE = 64
ROPE = 32
VDIM = 64
QK = NOPE + ROPE
KV_RANK = 256
DKV_PAD = 384
LANES = 128
CHUNK_SHIFT = 6
ROPE_THETA = 10000.0
EPS = 1e-6
SB_BLOCK = 256
MLA_BLOCK = 512
MLA_GROUP = 4
MLA_UNROLL = 4
PACK_COLS = 1024
PACK_ROW_ALIGN = 16
EXP_ZERO = -104.0
NEG_BIG = -1e30
LOG2_E = 1.4426950408889634
LN_2 = 0.6931471805599453
VMEM_ATT = 56 * 1024 * 1024
VMEM_MM = 48 * 1024 * 1024

ADAM_LR = 0.001
ADAM_B1 = 0.9
ADAM_B2 = 0.999
ADAM_EPS = 1e-08
ADAM_WD = 0.01
ADAM_STEP = 10

WEIGHT_ORDER = ['attn_norm', 'mlp_norm', 'sb_w_qkv', 'sb_w_o', 'kv_norm', 'mla_w_dkv', 'mla_kv_lat_norm',
                'mla_w_ukv', 'mla_w_dq', 'mla_q_lat_norm', 'mla_w_uq', 'mla_w_o', 'mlp_w1', 'mlp_w2', 'final_norm']
SHARDED = [('sb_w_qkv', 2), ('sb_w_o', 1), ('mla_w_dkv', 0), ('mla_w_ukv', 1), ('mla_w_dq', 1),
           ('mla_w_uq', 2), ('mla_w_o', 1), ('mlp_w1', 2), ('mlp_w2', 1)]
REPLICATED = ['attn_norm', 'mlp_norm', 'kv_norm', 'mla_kv_lat_norm', 'mla_q_lat_norm', 'final_norm']


def _tile(dim, prefs=(512, 384, 256, 128, 64, 32, 16, 8)):
    for t in prefs:
        if dim % t == 0:
            return t
    return dim


def _dot(a, b, dims):
    return lax.dot_general(a, b, (dims, ((), ())), preferred_element_type=F32)


NN = ((1,), (0,))
NT = ((1,), (1,))
TN = ((0,), (0,))


def _gather_phases(x_refs, out_refs, send_sems, recv_sems, local_sems):
    n_t = len(x_refs)
    x, y, c = lax.axis_index("x"), lax.axis_index("y"), lax.axis_index("c")
    me, sibling = (x, y, c), (x, y, 1 - c)
    chips = [(1 - x, y), (x, 1 - y), (1 - x, 1 - y)]

    def slot(t, px, py, pc):
        return out_refs[t].at[4 * px + 2 * py + pc]

    def copy(t, k, block, to, src=None):
        return pltpu.make_async_remote_copy(
            src_ref=slot(t, *block) if src is None else src, dst_ref=slot(t, *block),
            send_sem=send_sems.at[7 * t + k], recv_sem=recv_sems.at[7 * t + k],
            device_id=to, device_id_type=MESH)

    def mine():
        return [pltpu.make_async_copy(x_refs[t], slot(t, *me), local_sems.at[t]) for t in range(n_t)]

    def first():
        out = []
        for t in range(n_t):
            out.append(copy(t, 0, me, sibling, src=x_refs[t]))
            out += [copy(t, 1 + j, me, (*chip, c), src=x_refs[t]) for j, chip in enumerate(chips)]
        return out

    def passed():
        return [copy(t, 4 + j, (*chip, c), sibling) for t in range(n_t) for j, chip in enumerate(chips)]

    def start():
        for cp in mine() + first():
            cp.start()

    def forward(which=None):
        onward = passed()
        for t in range(n_t) if which is None else which:
            for j, chip in enumerate(chips):
                copy(t, 1 + j, (*chip, c), me).wait_recv()
                onward[3 * t + j].start()

    def finish():
        for t in range(n_t):
            copy(t, 0, sibling, me).wait_recv()
            for j, chip in enumerate(chips):
                copy(t, 4 + j, (*chip, 1 - c), me).wait_recv()
        for cp in first() + passed():
            cp.wait_send()
        for cp in mine():
            cp.wait()

    return start, forward, finish


def _gather_scratch(n_t):
    return [pltpu.SemaphoreType.DMA((7 * n_t,)), pltpu.SemaphoreType.DMA((7 * n_t,)), pltpu.SemaphoreType.DMA((n_t,))]


def _all_gather(shards, name):
    n_t = len(shards)

    def body(*refs):
        for phase in _gather_phases(refs[:n_t], refs[n_t:2 * n_t], *refs[2 * n_t:]):
            phase()

    any_spec = pl.BlockSpec(memory_space=pl.ANY)
    return pl.pallas_call(
        body, name=name,
        out_shape=[jax.ShapeDtypeStruct((N_DEV,) + tuple(s.shape), s.dtype) for s in shards],
        in_specs=[any_spec] * n_t, out_specs=[any_spec] * n_t,
        scratch_shapes=_gather_scratch(n_t),
    )(*shards)


def _pair_exchange(xs, name):
    n_t = len(xs)

    def body(*refs):
        x_refs, out_refs = refs[:n_t], refs[n_t:2 * n_t]
        send_sems, recv_sems = refs[2 * n_t:]
        x, y, c = lax.axis_index("x"), lax.axis_index("y"), lax.axis_index("c")
        copies = [pltpu.make_async_remote_copy(
            src_ref=x_refs[t].at[1 - c], dst_ref=out_refs[t], send_sem=send_sems.at[t], recv_sem=recv_sems.at[t],
            device_id=(x, y, 1 - c), device_id_type=MESH) for t in range(n_t)]
        for cp in copies:
            cp.start()
        for cp in copies:
            cp.wait()

    any_spec = pl.BlockSpec(memory_space=pl.ANY)
    return pl.pallas_call(
        body, name=name,
        out_shape=[jax.ShapeDtypeStruct(x.shape[1:], x.dtype) for x in xs],
        in_specs=[any_spec] * n_t, out_specs=[any_spec] * n_t,
        scratch_shapes=[pltpu.SemaphoreType.DMA((n_t,)), pltpu.SemaphoreType.DMA((n_t,))],
    )(*xs)


def _exchange_phases(p_refs, out_refs, send_sems, recv_sems, local_sems):
    n_t = len(p_refs)
    mx, my, mc = lax.axis_index("x"), lax.axis_index("y"), lax.axis_index("c")
    me = 2 * mx + my

    def mine():
        return [pltpu.make_async_copy(p_refs[t].at[me], out_refs[t].at[me], local_sems.at[t]) for t in range(n_t)]

    def copies():
        out = []
        for k in range(1, 4):
            px = 1 - mx if (k >> 1) & 1 else mx
            py = 1 - my if k & 1 else my
            peer = 2 * px + py
            for t in range(n_t):
                out.append(pltpu.make_async_remote_copy(
                    src_ref=p_refs[t].at[peer], dst_ref=out_refs[t].at[me],
                    send_sem=send_sems.at[3 * t + k - 1], recv_sem=recv_sems.at[3 * t + k - 1],
                    device_id=(px, py, mc), device_id_type=MESH))
        return out

    def start():
        for cp in mine() + copies():
            cp.start()

    def finish():
        for cp in copies():
            cp.wait_send()
        for cp in copies():
            cp.wait_recv()
        for cp in mine():
            cp.wait()

    return start, finish


def _exchange_scratch(n_t):
    return [pltpu.SemaphoreType.DMA((3 * n_t,)), pltpu.SemaphoreType.DMA((3 * n_t,)), pltpu.SemaphoreType.DMA((n_t,))]


def _chip_exchange(ps, name):
    n_t = len(ps)

    def body(*refs):
        for phase in _exchange_phases(refs[:n_t], refs[n_t:2 * n_t], *refs[2 * n_t:]):
            phase()

    any_spec = pl.BlockSpec(memory_space=pl.ANY)
    return pl.pallas_call(
        body, name=name,
        out_shape=[jax.ShapeDtypeStruct(p.shape, p.dtype) for p in ps],
        in_specs=[any_spec] * n_t, out_specs=[any_spec] * n_t,
        scratch_shapes=_exchange_scratch(n_t),
    )(*ps)


def _matmul(a, b, *, ta=False, tb=False, out_dtypes=(F32,), epilogue=None, extras=(), name):
    if ta:
        kdim, m = a.shape
    else:
        m, kdim = a.shape
    if tb:
        n, kb = b.shape
    else:
        kb, n = b.shape
    assert kdim == kb, (a.shape, b.shape, ta, tb)
    big = (1024, 768, 512, 384, 256, 128, 64, 32, 16, 8)
    tm, tn = _tile(m, big), _tile(n, big)
    tk = kdim if kdim <= 2048 else _tile(kdim, (2048, 1024, 512, 256, 128))
    nk = kdim // tk
    n_extra, n_out = len(extras), len(out_dtypes)
    a_spec = pl.BlockSpec((tk, tm), lambda i, j, k: (k, i)) if ta else pl.BlockSpec((tm, tk), lambda i, j, k: (i, k))
    b_spec = pl.BlockSpec((tn, tk), lambda i, j, k: (j, k)) if tb else pl.BlockSpec((tk, tn), lambda i, j, k: (k, j))
    tile_spec = pl.BlockSpec((tm, tn), lambda i, j, k: (i, j))
    dims = ((0,) if ta else (1,), (1,) if tb else (0,))

    def finish(acc, extra_refs, out_refs):
        outs = (acc,) if epilogue is None else epilogue(acc, *[r[...] for r in extra_refs])
        for o_ref, o in zip(out_refs, outs):
            o_ref[...] = o.astype(o_ref.dtype)

    def body_one(a_ref, b_ref, *rest):
        acc = _dot(a_ref[...].astype(BF16), b_ref[...].astype(BF16), dims)
        finish(acc, rest[:n_extra], rest[n_extra:n_extra + n_out])

    def body_acc(a_ref, b_ref, *rest):
        acc_ref = rest[-1]
        k = pl.program_id(2)

        @pl.when(k == 0)
        def _():
            acc_ref[...] = jnp.zeros_like(acc_ref)

        acc_ref[...] += _dot(a_ref[...].astype(BF16), b_ref[...].astype(BF16), dims)

        @pl.when(k == nk - 1)
        def _():
            finish(acc_ref[...], rest[:n_extra], rest[n_extra:n_extra + n_out])

    return pl.pallas_call(
        body_one if nk == 1 else body_acc, name=name, grid=(m // tm, n // tn, nk),
        in_specs=[a_spec, b_spec] + [tile_spec] * n_extra,
        out_specs=[tile_spec] * n_out,
        out_shape=[jax.ShapeDtypeStruct((m, n), dt) for dt in out_dtypes],
        scratch_shapes=[] if nk == 1 else [pltpu.VMEM((tm, tn), F32)],
        compiler_params=pltpu.CompilerParams(
            dimension_semantics=("parallel", "parallel", "arbitrary"), vmem_limit_bytes=VMEM_MM),
    )(a, b, *extras)


def _rms_matmul(x, g, w, *, out_dtypes=(F32,), epilogue=None, name):
    m, d = x.shape
    n = w.shape[1]
    big = (1024, 768, 512, 384, 256, 128, 64, 32, 16, 8)
    tm, tn = _tile(m, big), _tile(n, big)
    n_out = len(out_dtypes)

    def body(x_ref, g_ref, w_ref, h_ref, *rest):
        out_refs, h_scr = rest[:n_out], rest[n_out]

        @pl.when(pl.program_id(1) == 0)
        def _():
            xv = x_ref[...]
            r = lax.rsqrt(jnp.mean(xv * xv, axis=-1, keepdims=True) + EPS)
            h_scr[...] = (xv * r * g_ref[...]).astype(h_scr.dtype)
            h_ref[...] = h_scr[...]

        acc = _dot(h_scr[...], w_ref[...], NN)
        outs = (acc,) if epilogue is None else epilogue(acc)
        for o_ref, o in zip(out_refs, outs):
            o_ref[...] = o.astype(o_ref.dtype)

    row_spec = pl.BlockSpec((tm, d), lambda i, j: (i, 0))
    tile_spec = pl.BlockSpec((tm, tn), lambda i, j: (i, j))
    return pl.pallas_call(
        body, name=name, grid=(m // tm, n // tn),
        in_specs=[row_spec, pl.BlockSpec((1, d), lambda i, j: (0, 0)), pl.BlockSpec((d, tn), lambda i, j: (0, j))],
        out_specs=[row_spec] + [tile_spec] * n_out,
        out_shape=[jax.ShapeDtypeStruct((m, d), BF16)] + [jax.ShapeDtypeStruct((m, n), dt) for dt in out_dtypes],
        scratch_shapes=[pltpu.VMEM((tm, d), BF16)],
        compiler_params=pltpu.CompilerParams(
            dimension_semantics=("parallel", "arbitrary"), vmem_limit_bytes=VMEM_MM),
    )(x, g.reshape(1, d), w)


def _matmul_rms_bwd(dy, w, x, g, name, res=None):
    m, n = dy.shape
    d = w.shape[0]
    tm = _tile(m, (512, 256, 128, 64, 32, 16, 8))
    tk = n if n <= 2048 else _tile(n, (2048, 1024, 512, 256, 128))
    nk = n // tk
    has_res = res is not None

    def body(dy_ref, w_ref, x_ref, g_ref, *rest):
        dx_ref, dg_ref, acc_ref = rest[-3:]
        i, k = pl.program_id(0), pl.program_id(1)

        @pl.when(k == 0)
        def _():
            acc_ref[...] = jnp.zeros_like(acc_ref)

        @pl.when(jnp.logical_and(i == 0, k == 0))
        def _():
            dg_ref[...] = jnp.zeros_like(dg_ref)

        acc_ref[...] += _dot(dy_ref[...], w_ref[...], NT)

        @pl.when(k == nk - 1)
        def _():
            dh = acc_ref[...]
            xv = x_ref[...]
            r = lax.rsqrt(jnp.mean(xv * xv, axis=-1, keepdims=True) + EPS)
            xh = xv * r
            t = dh * g_ref[...]
            dx = r * (t - xh * jnp.mean(t * xh, axis=-1, keepdims=True))
            dx_ref[...] = dx + rest[0][...] if has_res else dx
            dg_ref[...] += jnp.sum(dh * xh, axis=0, keepdims=True)

    row_spec = pl.BlockSpec((tm, d), lambda i, k: (i, 0))
    vec_spec = pl.BlockSpec((1, d), lambda i, k: (0, 0))
    dx, dg = pl.pallas_call(
        body, name=name, grid=(m // tm, nk),
        in_specs=[pl.BlockSpec((tm, tk), lambda i, k: (i, k)), pl.BlockSpec((d, tk), lambda i, k: (0, k)),
                  row_spec, vec_spec] + ([row_spec] if has_res else []),
        out_specs=[row_spec, vec_spec],
        out_shape=[jax.ShapeDtypeStruct((m, d), F32), jax.ShapeDtypeStruct((1, d), F32)],
        scratch_shapes=[pltpu.VMEM((tm, d), F32)],
        compiler_params=pltpu.CompilerParams(
            dimension_semantics=("arbitrary", "arbitrary"), vmem_limit_bytes=VMEM_MM),
    )(dy, w, x, g.reshape(1, d), *((res,) if has_res else ()))
    return dx, dg.reshape(d)


def _loss_head(x, g, target):
    m, d = x.shape
    tm = _tile(m, (512, 256, 128, 64, 32, 16, 8))

    def body(x_ref, g_ref, t_ref, loss_ref, dx_ref, dg_ref):
        xv = x_ref[...]
        gv = g_ref[...]
        r = lax.rsqrt(jnp.mean(xv * xv, axis=-1, keepdims=True) + EPS)
        xh = xv * r
        err = xh * gv - t_ref[...]
        row_loss = jnp.mean(err * err, axis=-1, keepdims=True)
        dyv = err * (1.0 / d)
        t = dyv * gv
        dx_ref[...] = r * (t - xh * jnp.mean(t * xh, axis=-1, keepdims=True))

        @pl.when(pl.program_id(0) == 0)
        def _():
            dg_ref[...] = jnp.zeros_like(dg_ref)
            loss_ref[...] = jnp.zeros_like(loss_ref)

        dg_ref[...] += jnp.sum(dyv * xh, axis=0, keepdims=True)
        loss_ref[...] += 0.5 * jnp.sum(row_loss, axis=0, keepdims=True)

    loss, dx, dg = pl.pallas_call(
        body, name="loss_head", grid=(m // tm,),
        in_specs=[pl.BlockSpec((tm, d), lambda i: (i, 0)), pl.BlockSpec((1, d), lambda i: (0, 0)),
                  pl.BlockSpec((tm, d), lambda i: (i, 0))],
        out_specs=[pl.BlockSpec((1, 1), lambda i: (0, 0)), pl.BlockSpec((tm, d), lambda i: (i, 0)),
                   pl.BlockSpec((1, d), lambda i: (0, 0))],
        out_shape=[jax.ShapeDtypeStruct((1, 1), F32), jax.ShapeDtypeStruct((m, d), F32),
                   jax.ShapeDtypeStruct((1, d), F32)],
        compiler_params=pltpu.CompilerParams(dimension_semantics=("arbitrary",)),
    )(x, g.reshape(1, d), target)
    return loss, dx, dg.reshape(d)


def _rope_apply(x, tabs, transpose, name):
    m, w = x.shape
    wt = tabs[0].shape[1]
    reps = w // wt
    half = ROPE // 2
    tm = _tile(m, (256, 128, 64, 32, 16, 8))

    def body(x_ref, c_ref, a_ref, b_ref, y_ref):
        xv = x_ref[...]

        def wide(t_ref):
            t = t_ref[...]
            return t if reps == 1 else jnp.concatenate([t] * reps, axis=1)

        c, a, b = wide(c_ref), wide(a_ref), wide(b_ref)
        if transpose:
            y = xv * c + pltpu.roll(xv * a, half, 1) + pltpu.roll(xv * b, w - half, 1)
        else:
            y = xv * c + pltpu.roll(xv, w - half, 1) * a + pltpu.roll(xv, half, 1) * b
        y_ref[...] = y

    x_spec = pl.BlockSpec((tm, w), lambda i: (i, 0))
    t_spec = pl.BlockSpec((tm, wt), lambda i: (i, 0))
    return pl.pallas_call(
        body, name=name, grid=(m // tm,),
        in_specs=[x_spec, t_spec, t_spec, t_spec], out_specs=x_spec,
        out_shape=jax.ShapeDtypeStruct((m, w), F32),
        compiler_params=pltpu.CompilerParams(dimension_semantics=("parallel",)),
    )(x, *tabs)


def _log_sigmoid_pair(z):
    a = jnp.minimum(z, 0.0) - jnp.log(1.0 + jnp.exp(-jnp.abs(z)))
    return a, a - z


def _split_bf16(x):
    hi = x.astype(BF16)
    return hi, (x - hi.astype(F32)).astype(BF16)


def _wide(v, width):
    return v if width == LANES else jnp.concatenate([v] * (width // LANES), axis=1)


def _lanes(col):
    return jnp.broadcast_to(col, (col.shape[0], LANES))


def _rope_head(x, c, a, b):
    half = ROPE // 2
    return x * c + pltpu.roll(x, LANES - half, 1) * a + pltpu.roll(x, half, 1) * b


def _rope_head_t(dy, c, a, b):
    half = ROPE // 2
    return dy * c + pltpu.roll(dy * a, half, 1) + pltpu.roll(dy * b, LANES - half, 1)


def _sb_weights(qh, k2, valid):
    a, b = _log_sigmoid_pair(_dot(qh, k2, NT))
    return a, b if valid is None else jnp.where(valid, b, 0.0)


def _sb_fwd(qkv, shards=()):
    s = qkv.shape[0]
    t = min(SB_BLOCK, s)
    nq = s // t
    npair = SB_HEADS // 2
    scale = SB_HD ** -0.5

    n_t = len(shards)

    def body(q_ref, k_ref, v_ref, *rest):
        x_refs, rest = rest[:n_t], rest[n_t:]
        o_ref, bta_ref, btb_ref, js_ref = rest[:4]
        g_refs, rest = rest[4:4 + n_t], rest[4 + n_t:]
        acc_ref, ra_ref, rb_ref = rest[:3]
        p, i = pl.program_id(0), pl.program_id(1)
        if n_t:
            start, forward, finish = _gather_phases(x_refs, g_refs, *rest[3:])
            pl.when(jnp.logical_and(p == 0, i == 0))(start)
            pl.when(jnp.logical_and(p == (5 * npair) // 8, i == 0))(lambda: forward(range(n_t - 1)))
            pl.when(jnp.logical_and(p == (7 * npair) // 8, i == 0))(lambda: forward([n_t - 1]))
        q2 = q_ref[...] * scale
        first = lax.broadcasted_iota(jnp.int32, (t, LANES), 1) < SB_HD
        heads = (jnp.where(first, q2, jnp.zeros_like(q2)), jnp.where(first, jnp.zeros_like(q2), q2))
        row = lax.broadcasted_iota(jnp.int32, (t, t), 0)
        col = lax.broadcasted_iota(jnp.int32, (t, t), 1)
        later = jnp.where(row > col, 1.0, 0.0).astype(BF16)
        acc_ref[...] = jnp.zeros_like(acc_ref)
        ra_ref[...] = jnp.zeros_like(ra_ref)
        rb_ref[...] = jnp.zeros_like(rb_ref)

        def block(j, diagonal):
            off = pl.multiple_of(j * t, t)
            k2 = k_ref[pl.ds(off, t), :]
            v2 = v_ref[pl.ds(off, t), :]
            valid = col < row if diagonal else None
            outs, rmax = [], None
            for qh, r_ref in zip(heads, (ra_ref, rb_ref)):
                a, b = _sb_weights(qh, k2, valid)
                bh, bl = _split_bf16(b)
                inner = _dot(bh, later, NN) + _dot(bl, later, NN)
                r = r_ref[...]
                w = jnp.exp(a + inner + _wide(r, t))
                if diagonal:
                    w = jnp.where(valid, w, 0.0)
                outs.append(_dot(w.astype(BF16), v2, NN))
                rn = r + jnp.sum(b, axis=1, keepdims=True)
                r_ref[...] = rn
                rmax = jnp.max(rn) if rmax is None else jnp.maximum(rmax, jnp.max(rn))
            acc_ref[...] += jnp.where(first, outs[0], outs[1])
            return rmax

        def cond(carry):
            j, rmax = carry
            return jnp.logical_and(j >= 0, rmax > EXP_ZERO)

        def step(carry):
            j, _ = carry
            return j - 1, block(j, False)

        def first_two():
            block(i, True)
            return block(i - 1, False)

        rmax = lax.cond(i >= 1, first_two, lambda: block(i, True))
        jend, _ = lax.while_loop(cond, step, (jnp.maximum(i - 2, -1), rmax))
        o_ref[...] = acc_ref[...].astype(o_ref.dtype)
        bta_ref[0] = ra_ref[...][:, :1]
        btb_ref[0] = rb_ref[...][:, :1]
        js_ref[p, i] = (jend + 1).astype(F32)
        if n_t:
            pl.when(jnp.logical_and(p == npair - 1, i == nq - 1))(finish)

    stat = pl.BlockSpec((1, t, 1), lambda p, i: (p, i, 0))
    any_spec = pl.BlockSpec(memory_space=pl.ANY)
    return pl.pallas_call(
        body, name="sb_attn_fwd_gather" if n_t else "sb_attn_fwd", grid=(npair, nq),
        in_specs=[pl.BlockSpec((t, LANES), lambda p, i: (i, p)),
                  pl.BlockSpec((s, LANES), lambda p, i: (0, npair + p)),
                  pl.BlockSpec((s, LANES), lambda p, i: (0, 2 * npair + p))] + [any_spec] * n_t,
        out_specs=[pl.BlockSpec((t, LANES), lambda p, i: (i, p)), stat, stat,
                   pl.BlockSpec(memory_space=pltpu.SMEM)] + [any_spec] * n_t,
        out_shape=[jax.ShapeDtypeStruct((s, SB_HEADS * SB_HD), BF16), jax.ShapeDtypeStruct((npair, s, 1), F32),
                   jax.ShapeDtypeStruct((npair, s, 1), F32), jax.ShapeDtypeStruct((npair, nq), F32)]
        + [jax.ShapeDtypeStruct((N_DEV,) + tuple(x.shape), x.dtype) for x in shards],
        scratch_shapes=[pltpu.VMEM((t, LANES), F32)] * 3 + (_gather_scratch(n_t) if n_t else []),
        compiler_params=pltpu.CompilerParams(
            dimension_semantics=("arbitrary", "arbitrary"), vmem_limit_bytes=VMEM_ATT),
    )(qkv, qkv, qkv, *shards)


def _sb_bwd(qkv, do, btot_a, btot_b, jstart, sums=()):
    s = qkv.shape[0]
    t = min(SB_BLOCK, s)
    nq = s // t
    npair = SB_HEADS // 2
    scale = SB_HD ** -0.5

    n_t = len(sums)

    def body(js_ref, q_ref, k_ref, v_ref, do_ref, bta_ref, btb_ref, *rest):
        s_refs, rest = rest[:n_t], rest[n_t:]
        dq_ref, dk_ref, dv_ref = rest[:3]
        r_refs, rest = rest[3:3 + n_t], rest[3 + n_t:]
        dq_acc, pa_ref, pb_ref, ga_ref, gb_ref, dk_acc, dv_acc = rest[:7]
        p, i = pl.program_id(0), pl.program_id(1)
        if n_t:
            start, finish = _exchange_phases(s_refs, r_refs, *rest[7:])
            pl.when(jnp.logical_and(p == 0, i == 0))(start)

        @pl.when(i == 0)
        def _():
            dk_acc[...] = jnp.zeros_like(dk_acc)
            dv_acc[...] = jnp.zeros_like(dv_acc)

        q2 = q_ref[...] * scale
        do2 = do_ref[...]
        first = lax.broadcasted_iota(jnp.int32, (t, LANES), 1) < SB_HD
        zero = jnp.zeros_like(q2)
        q_heads = (jnp.where(first, q2, zero), jnp.where(first, zero, q2))
        do_heads = (jnp.where(first, do2, zero), jnp.where(first, zero, do2))
        bts = (_wide(_lanes(bta_ref[0]), t), _wide(_lanes(btb_ref[0]), t))
        row = lax.broadcasted_iota(jnp.int32, (t, t), 0)
        col = lax.broadcasted_iota(jnp.int32, (t, t), 1)
        upto = jnp.where(row <= col, 1.0, 0.0).astype(BF16)
        before = jnp.where(row < col, 1.0, 0.0).astype(BF16)
        dq_acc[...] = jnp.zeros_like(dq_acc)
        for r in (pa_ref, pb_ref, ga_ref, gb_ref):
            r[...] = jnp.zeros_like(r)
        j0 = jnp.clip(js_ref[p, i].astype(jnp.int32), 0, i)

        def block(j, diagonal):
            off = pl.multiple_of(j * t, t)
            k2 = k_ref[pl.ds(off, t), :]
            v2 = v_ref[pl.ds(off, t), :]
            valid = col < row if diagonal else None
            dqs, dk2, dv2 = [], None, None
            for qh, doh, bt, p_ref, g_ref in zip(q_heads, do_heads, bts, (pa_ref, pb_ref), (ga_ref, gb_ref)):
                a, b = _sb_weights(qh, k2, valid)
                bh, bl = _split_bf16(b)
                pin = _dot(bh, upto, NN) + _dot(bl, upto, NN)
                surv = bt - (_wide(p_ref[...], t) + pin)
                w = jnp.exp(a + surv)
                if diagonal:
                    w = jnp.where(valid, w, 0.0)
                g = w * _dot(doh, v2, NT)
                gh, gl = _split_bf16(g)
                gsum = _wide(g_ref[...], t) + _dot(gh, before, NN) + _dot(gl, before, NN)
                beta = jnp.exp(a)
                dz = g * (1.0 - beta) - gsum * beta
                if diagonal:
                    dz = jnp.where(valid, dz, 0.0)
                dz = dz.astype(BF16)
                dqs.append(_dot(dz, k2, NN))
                dkh = _dot(dz, qh, TN)
                dvh = _dot(w.astype(BF16), doh, TN)
                dk2 = dkh if dk2 is None else dk2 + dkh
                dv2 = dvh if dv2 is None else dv2 + dvh
                p_ref[...] += jnp.sum(b, axis=1, keepdims=True)
                g_ref[...] += jnp.sum(g, axis=1, keepdims=True)
            dq_acc[...] += jnp.where(first, dqs[0], dqs[1])
            dk_acc[pl.ds(off, t), :] += dk2
            dv_acc[pl.ds(off, t), :] += dv2

        def step(j, carry):
            block(j, False)
            return carry

        lax.fori_loop(j0, jnp.maximum(i - 1, j0), step, 0)

        @pl.when(j0 < i)
        def _():
            block(i - 1, False)
            block(i, True)

        @pl.when(j0 >= i)
        def _():
            block(i, True)

        dq_ref[...] = (dq_acc[...] * scale).astype(dq_ref.dtype)

        @pl.when(i == nq - 1)
        def _():
            dk_ref[...] = dk_acc[...].astype(dk_ref.dtype)
            dv_ref[...] = dv_acc[...].astype(dv_ref.dtype)

        if n_t:
            pl.when(jnp.logical_and(p == npair - 1, i == nq - 1))(finish)

    blk = pl.BlockSpec((t, LANES), lambda p, i: (i, p))
    full = pl.BlockSpec((s, LANES), lambda p, i: (0, p))
    stat = pl.BlockSpec((1, t, 1), lambda p, i: (p, i, 0))
    vec = pltpu.VMEM((t, LANES), F32)
    any_spec = pl.BlockSpec(memory_space=pl.ANY)
    return pl.pallas_call(
        body, name="sb_attn_bwd_exchange" if n_t else "sb_attn_bwd", grid=(npair, nq),
        in_specs=[pl.BlockSpec(memory_space=pltpu.SMEM), blk,
                  pl.BlockSpec((s, LANES), lambda p, i: (0, npair + p)),
                  pl.BlockSpec((s, LANES), lambda p, i: (0, 2 * npair + p)), blk, stat, stat] + [any_spec] * n_t,
        out_specs=[blk, full, full] + [any_spec] * n_t,
        out_shape=[jax.ShapeDtypeStruct((s, SB_HEADS * SB_HD), BF16)] * 3
        + [jax.ShapeDtypeStruct(x.shape, x.dtype) for x in sums],
        scratch_shapes=[pltpu.VMEM((t, LANES), F32), vec, vec, vec, vec,
                        pltpu.VMEM((s, LANES), F32), pltpu.VMEM((s, LANES), F32)]
        + (_exchange_scratch(n_t) if n_t else []),
        compiler_params=pltpu.CompilerParams(
            dimension_semantics=("arbitrary", "arbitrary"), vmem_limit_bytes=VMEM_ATT),
    )(jstart, qkv, qkv, qkv, do, btot_a, btot_b, *sums)


def _make_sb_attention(side=None):
    @jax.custom_vjp
    def f(qkv, *shards):
        o, _, _, _, *gathered = _sb_fwd(qkv, shards)
        return (o, *gathered)

    def fwd(qkv, *shards):
        o, btot_a, btot_b, jstart, *gathered = _sb_fwd(qkv, shards)
        return (o, *gathered), (qkv, btot_a, btot_b, jstart, shards)

    def bwd(res, cts):
        qkv, btot_a, btot_b, jstart, shards = res
        sums = () if side is None else tuple(side['sums'])
        dq, dk, dv, *received = _sb_bwd(qkv, cts[0], btot_a, btot_b, jstart, sums)
        if side is not None:
            side['received'] = received
        return (jnp.concatenate([dq, dk, dv], axis=1),) + tuple(jnp.zeros_like(x) for x in shards)

    f.defvjp(fwd, bwd)
    return f


def sb_attention(qkv):
    return _make_sb_attention()(qkv)[0]


def _chunk_allowed(row, col):
    return (col >> CHUNK_SHIFT) <= (row >> CHUNK_SHIFT)


def _mla_fwd(q, k, vx, tabs):
    s = q.shape[0]
    h_ = q.shape[1] // LANES
    t = min(MLA_BLOCK, s)
    nq = s // t
    scale = QK ** -0.5
    hp = MLA_GROUP

    def body(q_ref, k_ref, v_ref, c_ref, a_ref, b_ref, o_ref, lse_ref, acc_ref, m_ref):
        i = pl.program_id(1)
        acc_ref[...] = jnp.zeros_like(acc_ref)
        m_ref[...] = jnp.full_like(m_ref, NEG_BIG)
        rot = (c_ref[...], a_ref[...], b_ref[...])
        qs = [_rope_head(q_ref[:, hh * LANES:(hh + 1) * LANES], *rot).astype(BF16) for hh in range(hp)]

        def tile(j, diagonal):
            off = pl.multiple_of(j * t, t)
            for hh in range(hp):
                lanes = slice(hh * LANES, (hh + 1) * LANES)
                kj = k_ref[pl.ds(off, t), lanes]
                vj = v_ref[pl.ds(off, t), lanes]
                sc = _dot(qs[hh], kj, NT) * (scale * LOG2_E)
                if diagonal:
                    row = lax.broadcasted_iota(jnp.int32, (t, t), 0)
                    col = lax.broadcasted_iota(jnp.int32, (t, t), 1)
                    sc = jnp.where(_chunk_allowed(row, col), sc, NEG_BIG)
                m_old = m_ref[hh]
                m_new = jnp.maximum(m_old, jnp.max(sc, axis=1, keepdims=True))
                p = jnp.exp2(sc - _wide(m_new, t))
                acc_ref[hh] = jnp.exp2(m_old - m_new) * acc_ref[hh] + _dot(p.astype(BF16), vj, NN)
                m_ref[hh] = m_new

        def step(jj, carry):
            tile(2 * jj, False)
            tile(2 * jj + 1, False)
            return carry

        lax.fori_loop(0, i // 2, step, 0)

        @pl.when(i % 2 == 1)
        def _():
            tile(i - 1, False)

        tile(i, True)
        first = lax.broadcasted_iota(jnp.int32, (t, LANES), 1) < VDIM
        outs = []
        for hh in range(hp):
            acc = acc_ref[hh]
            den = acc[:, VDIM:VDIM + 1]
            outs.append(acc / den)
            lse_ref[hh] = (m_ref[hh][:, :1] + jnp.log2(den)) * LN_2
        for pp in range(hp // 2):
            o_ref[:, pp * LANES:(pp + 1) * LANES] = jnp.where(
                first, outs[2 * pp], pltpu.roll(outs[2 * pp + 1], VDIM, 1))

    wide = hp * LANES
    return pl.pallas_call(
        body, name="mla_attn_fwd", grid=(h_ // hp, nq),
        in_specs=[pl.BlockSpec((t, wide), lambda g, i: (i, g)),
                  pl.BlockSpec((s, wide), lambda g, i: (0, g)),
                  pl.BlockSpec((s, wide), lambda g, i: (0, g))] + [pl.BlockSpec((t, LANES), lambda g, i: (i, 0))] * 3,
        out_specs=[pl.BlockSpec((t, hp * VDIM), lambda g, i: (i, g)),
                   pl.BlockSpec((hp, t, 1), lambda g, i: (g, i, 0))],
        out_shape=[jax.ShapeDtypeStruct((s, h_ * VDIM), F32), jax.ShapeDtypeStruct((h_, s, 1), F32)],
        scratch_shapes=[pltpu.VMEM((hp, t, LANES), F32), pltpu.VMEM((hp, t, LANES), F32)],
        compiler_params=pltpu.CompilerParams(
            dimension_semantics=("arbitrary", "arbitrary"), vmem_limit_bytes=VMEM_ATT),
    )(q, k, vx, *tabs)


def _mla_bwd(q, k, vx, tabs, o, lse, do):
    s = q.shape[0]
    h_ = q.shape[1] // LANES
    t = min(MLA_BLOCK, s)
    nq = s // t
    scale = QK ** -0.5

    def body(q_ref, k_ref, v_ref, c_ref, a_ref, b_ref, o_ref, lse_ref, do_ref, dq_ref, dk_ref, dv_ref, dq_acc):
        h, i = pl.program_id(0), pl.program_id(1)
        rot = (c_ref[...], a_ref[...], b_ref[...])

        @pl.when(i == 0)
        def _():
            dk_ref[...] = jnp.zeros_like(dk_ref)
            dv_ref[...] = jnp.zeros_like(dv_ref)

        qv = _rope_head(q_ref[...], *rot).astype(BF16)
        lane = lax.broadcasted_iota(jnp.int32, (t, LANES), 1)
        start = (h % 2) * VDIM
        mine = jnp.logical_and(lane >= start, lane < start + VDIM)
        do2 = do_ref[...]
        delta = _lanes(jnp.sum(jnp.where(mine, do2 * o_ref[...], 0.0), axis=1, keepdims=True))
        odd = (start + jnp.zeros_like(lane)) > 0
        do_head = jnp.where(odd, pltpu.roll(do2, VDIM, 1), do2)
        dov = jnp.where(lane < VDIM, do_head, 0.0).astype(BF16)
        lse = _lanes(lse_ref[0])
        dq_acc[...] = jnp.zeros_like(dq_acc)

        def tile(j, diagonal):
            off = pl.multiple_of(j * t, t)
            kj = k_ref[pl.ds(off, t), :]
            vj = v_ref[pl.ds(off, t), :]
            p = jnp.exp(_dot(qv, kj, NT) * scale - _wide(lse, t))
            if diagonal:
                row = lax.broadcasted_iota(jnp.int32, (t, t), 0)
                col = lax.broadcasted_iota(jnp.int32, (t, t), 1)
                p = jnp.where(_chunk_allowed(row, col), p, 0.0)
            dp = _dot(dov, vj, NT)
            ds = (p * (dp - _wide(delta, t)) * scale).astype(BF16)
            dq_acc[...] += _dot(ds, kj, NN)
            dk_ref[pl.ds(off, t), :] += _dot(ds, qv, TN)
            dv_ref[pl.ds(off, t), :] += _dot(p.astype(BF16), dov, TN)

        def step(jj, carry):
            for u in range(MLA_UNROLL):
                tile(MLA_UNROLL * jj + u, False)
            return carry

        def single(j, carry):
            tile(j, False)
            return carry

        lax.fori_loop(0, i // MLA_UNROLL, step, 0)
        lax.fori_loop(i - i % MLA_UNROLL, i, single, 0)
        tile(i, True)
        dq_ref[...] = _rope_head_t(dq_acc[...], *rot)

    blk = pl.BlockSpec((t, LANES), lambda h, i: (i, h))
    full = pl.BlockSpec((s, LANES), lambda h, i: (0, h))
    pair = pl.BlockSpec((t, LANES), lambda h, i: (i, h // 2))
    stat = pl.BlockSpec((1, t, 1), lambda h, i: (h, i, 0))
    return pl.pallas_call(
        body, name="mla_attn_bwd", grid=(h_, nq),
        in_specs=[blk, full, full] + [pl.BlockSpec((t, LANES), lambda h, i: (i, 0))] * 3 + [pair, stat, pair],
        out_specs=[blk, full, full],
        out_shape=[jax.ShapeDtypeStruct((s, h_ * LANES), F32)] * 3,
        scratch_shapes=[pltpu.VMEM((t, LANES), F32)],
        compiler_params=pltpu.CompilerParams(
            dimension_semantics=("arbitrary", "arbitrary"), vmem_limit_bytes=VMEM_ATT),
    )(q, k, vx, *tabs, o, lse, do)


def _assemble_kv(kvp, kr):
    half = kvp.shape[1] // 2
    ones = jnp.tile((jnp.arange(LANES) == VDIM).astype(F32), MLA_HEADS)
    k = kvp[:, :half] + jnp.tile(kr, (1, MLA_HEADS))
    return k.astype(BF16), (kvp[:, half:] + ones).astype(BF16)


@jax.custom_vjp
def mla_attention(q, kvp, kr, c, a, b):
    return _mla_fwd(q, *_assemble_kv(kvp, kr), (c, a, b))[0]


def _mla_attention_fwd(q, kvp, kr, c, a, b):
    kb, vb = _assemble_kv(kvp, kr)
    o, lse = _mla_fwd(q, kb, vb, (c, a, b))
    return o, (q, kb, vb, (c, a, b), o, lse)


def _mla_attention_bwd(res, do):
    q, kb, vb, tabs, o, lse = res
    dq, dk, dvx = _mla_bwd(q, kb, vb, tabs, o, lse, do)
    dkr = dk.reshape(dk.shape[0], MLA_HEADS, LANES).sum(axis=1)
    return (dq, jnp.concatenate([dk, dvx], axis=1), dkr) + tuple(jnp.zeros_like(t) for t in tabs)


mla_attention.defvjp(_mla_attention_fwd, _mla_attention_bwd)


def _add_tile(acc, res):
    return (acc + res,)


def _make_linear_res(tag, a_dtype=F32):
    def forward(x, a, slot, wb):
        ab = a.astype(BF16)
        y = _matmul(ab, wb, epilogue=_add_tile, extras=(x,), name=tag + "_fwd")[0]
        return y, (ab, wb)

    @jax.custom_vjp
    def f(x, a, slot, wb):
        return forward(x, a, slot, wb)[0]

    def bwd(res, dy):
        ab, wb = res
        dyb = dy.astype(BF16)
        da = _matmul(dyb, wb, tb=True, out_dtypes=(a_dtype,), name=tag + "_da")[0]
        dw = _matmul(ab, dyb, ta=True, name=tag + "_dw")[0]
        return dy, da, dw, jnp.zeros_like(wb)

    f.defvjp(forward, bwd)
    return f


def _make_norm_linear(tag, out_dtype=F32):
    def forward(x, g, slot, wb):
        hb, y = _rms_matmul(x, g, wb, out_dtypes=(out_dtype,), name=tag + "_norm_fwd")
        return y, (x, g, wb, hb)

    @jax.custom_vjp
    def f(x, g, slot, wb):
        return forward(x, g, slot, wb)[0]

    def bwd(res, dy):
        x, g, wb, hb = res
        dyb = dy.astype(BF16)
        dw = _matmul(hb, dyb, ta=True, name=tag + "_dw")[0]
        dx, dg = _matmul_rms_bwd(dyb, wb, x, g, tag + "_dh_norm_bwd")
        return dx, dg, dw, jnp.zeros_like(wb)

    f.defvjp(forward, bwd)
    return f


def _relu2_fwd(acc):
    r = jnp.maximum(acc, 0.0)
    return acc, r * r


def _relu2_bwd(acc, u):
    return (acc * (2.0 * jnp.maximum(u.astype(F32), 0.0)),)


def _make_mlp_res(tag):
    def forward(x, g, slot1, slot2, w1b, w2b):
        hb, u, act = _rms_matmul(x, g, w1b, out_dtypes=(BF16, BF16), epilogue=_relu2_fwd, name=tag + "_norm_up")
        y = _matmul(act, w2b, epilogue=_add_tile, extras=(x,), name=tag + "_down")[0]
        return y, (x, g, w1b, w2b, hb, u, act)

    @jax.custom_vjp
    def f(x, g, slot1, slot2, w1b, w2b):
        return forward(x, g, slot1, slot2, w1b, w2b)[0]

    def bwd(res, dy):
        x, g, w1b, w2b, hb, u, act = res
        dyb = dy.astype(BF16)
        du = _matmul(dyb, w2b, tb=True, out_dtypes=(BF16,), epilogue=_relu2_bwd, extras=(u,), name=tag + "_du")[0]
        dw2 = _matmul(act, dyb, ta=True, name=tag + "_dw2")[0]
        dw1 = _matmul(hb, du, ta=True, name=tag + "_dw1")[0]
        dx, dg = _matmul_rms_bwd(du, w1b, x, g, tag + "_dh_norm_bwd", res=dy)
        return dx, dg, dw1, dw2, jnp.zeros_like(w1b), jnp.zeros_like(w2b)

    f.defvjp(forward, bwd)
    return f


def _make_rope(tag):
    @jax.custom_vjp
    def f(x, c, a, b):
        return _rope_apply(x, (c, a, b), False, tag + "_fwd")

    def fwd(x, c, a, b):
        return _rope_apply(x, (c, a, b), False, tag + "_fwd"), (c, a, b)

    def bwd(res, dy):
        c, a, b = res
        return _rope_apply(dy, (c, a, b), True, tag + "_bwd"), jnp.zeros_like(c), jnp.zeros_like(a), jnp.zeros_like(b)

    f.defvjp(fwd, bwd)
    return f


def _rope_tables(positions):
    half = ROPE // 2
    inv_freq = ROPE_THETA ** (-jnp.arange(0, ROPE, 2, dtype=F32) / ROPE)
    ang = positions.astype(F32)[:, None] * inv_freq
    cos, sin = jnp.cos(ang), jnp.sin(ang)
    s = positions.shape[0]
    one = lambda n: jnp.ones((s, n), F32)
    zero = lambda n: jnp.zeros((s, n), F32)

    def tables(before, after):
        return (jnp.concatenate([one(before), cos, cos, one(after)], axis=1),
                jnp.concatenate([zero(before), -sin, zero(half + after)], axis=1),
                jnp.concatenate([zero(before + half), sin, zero(after)], axis=1))

    return tables(NOPE, LANES - QK), tables(KV_RANK, DKV_PAD - KV_RANK - ROPE)


def _pad_heads(w, per):
    lead = w.shape[:-1]
    w = jnp.pad(w.reshape(lead + (MLA_HEADS, per)), [(0, 0)] * len(lead) + [(0, 0), (0, LANES - per)])
    return w.reshape(lead + (MLA_HEADS * LANES,))


def _split_kv_heads(w):
    w3 = w.reshape(w.shape[0], MLA_HEADS, NOPE + VDIM)
    return jnp.concatenate([_pad_heads(w3[..., :NOPE].reshape(w.shape[0], -1), NOPE),
                            _pad_heads(w3[..., NOPE:].reshape(w.shape[0], -1), VDIM)], axis=1)


def _layers(full):
    return [full[l] for l in range(full.shape[0])] if full.ndim == 3 else [full]


def _first_attention(x, slot_qkv, slot_o, norms, w_qkv0, shards, side):
    qkv = _make_norm_linear("sb0_qkv", BF16)(x, norms['attn_norm'][0], slot_qkv, w_qkv0)
    o, *gathered = _make_sb_attention(side)(qkv, *shards)
    w_o = _layers(_merge_blocks(gathered[1], SHARDED[1][1]))[0]
    return _make_linear_res("sb0_o", BF16)(x, o, slot_o, w_o), gathered


def _rest(x, slots, norms, wb, q_tabs, kv_tabs):
    kvp = kr = None
    for layer in range(DEPTH):
        if 0 < layer < N_A:
            qkv = _make_norm_linear(f"sb{layer}_qkv", BF16)(
                x, norms['attn_norm'][layer], slots['sb_w_qkv'][layer], wb['sb_w_qkv'][layer])
            x = _make_linear_res(f"sb{layer}_o", BF16)(
                x, sb_attention(qkv), slots['sb_w_o'][layer], wb['sb_w_o'][layer])
        elif layer >= N_A:
            j = layer - N_A
            if j == 0:
                pad = ((0, 0), (0, DKV_PAD - KV_RANK - ROPE))
                down = _make_norm_linear("kv_down")(
                    x, norms['kv_norm'], jnp.pad(slots['mla_w_dkv'][0], pad), jnp.pad(wb['mla_w_dkv'][0], pad))
                kvp = _make_norm_linear("kv_up")(
                    down[:, :KV_RANK], norms['mla_kv_lat_norm'],
                    _split_kv_heads(slots['mla_w_ukv'][0]), _split_kv_heads(wb['mla_w_ukv'][0]))
                k_rope = _make_rope("rope_k")(down, *kv_tabs)[:, KV_RANK:KV_RANK + ROPE]
                kr = jnp.pad(k_rope, ((0, 0), (NOPE, LANES - QK)))
            c_q = _make_norm_linear(f"mla{j}_dq")(
                x, norms['attn_norm'][layer], slots['mla_w_dq'][j], wb['mla_w_dq'][j])
            q = _make_norm_linear(f"mla{j}_uq")(
                c_q, norms['mla_q_lat_norm'][j], _pad_heads(slots['mla_w_uq'][j], QK), _pad_heads(wb['mla_w_uq'][j], QK))
            o = mla_attention(q, kvp, kr, *q_tabs)
            x = _make_linear_res(f"mla{j}_o")(x, o, slots['mla_w_o'][j], wb['mla_w_o'][j])
        x = _make_mlp_res(f"mlp{layer}")(
            x, norms['mlp_norm'][layer], slots['mlp_w1'][layer], slots['mlp_w2'][layer],
            wb['mlp_w1'][layer], wb['mlp_w2'][layer])
    return x


def _merge_blocks(gathered, ax):
    shp = gathered.shape[1:]
    return jnp.moveaxis(gathered, 0, ax).reshape(shp[:ax] + (N_DEV * shp[ax],) + shp[ax + 1:])


def _split_blocks(full, ax):
    shp = full.shape
    return jnp.moveaxis(full.reshape(shp[:ax] + (N_DEV, shp[ax] // N_DEV) + shp[ax + 1:]), ax, 0)


def _pack(parts):
    flat = jnp.concatenate([p.reshape(-1) for p in parts])
    rows = -(-flat.shape[0] // PACK_COLS)
    rows = -(-rows // PACK_ROW_ALIGN) * PACK_ROW_ALIGN
    return jnp.pad(flat, (0, rows * PACK_COLS - flat.shape[0])).reshape(rows, PACK_COLS)


def _unpack(packed, shapes):
    flat = packed.reshape(-1)
    out, off = [], 0
    for shp in shapes:
        n = math.prod(shp)
        out.append(flat[off:off + n].reshape(shp))
        off += n
    return out


def _add_round(x, stage, core, name):
    _, r, c = x.shape
    tr = _tile(r, (256, 128, 64, 32, 16, 8))

    def body(core_ref, x_ref, s_ref, o_ref):
        o_ref[...] = (x_ref[0] + s_ref[...]).astype(o_ref.dtype)

    return pl.pallas_call(
        body, name=name,
        grid_spec=pltpu.PrefetchScalarGridSpec(
            num_scalar_prefetch=1, grid=(r // tr,),
            in_specs=[pl.BlockSpec((1, tr, c), lambda i, core_ref: (core_ref[0], i, 0)),
                      pl.BlockSpec((tr, c), lambda i, core_ref: (i, 0))],
            out_specs=pl.BlockSpec((tr, c), lambda i, core_ref: (i, 0))),
        out_shape=jax.ShapeDtypeStruct((r, c), BF16),
        compiler_params=pltpu.CompilerParams(dimension_semantics=("parallel",)),
    )(core, x, stage)


def _adamw_reduce(parts, w, m, v, name):
    n_parts, r, c = parts.shape
    tr = _tile(r, (128, 64, 32, 16, 8))
    bias1 = 1.0 - ADAM_B1 ** ADAM_STEP
    bias2 = 1.0 - ADAM_B2 ** ADAM_STEP

    def body(p_ref, w_ref, m_ref, v_ref, g_ref, d_ref, nm_ref, nv_ref):
        g = p_ref[0].astype(F32)
        for s in range(1, n_parts):
            g = g + p_ref[s].astype(F32)
        mn = ADAM_B1 * m_ref[...] + (1.0 - ADAM_B1) * g
        vn = ADAM_B2 * v_ref[...] + (1.0 - ADAM_B2) * (g * g)
        m_hat = mn / bias1
        v_hat = vn / bias2
        g_ref[...] = g
        d_ref[...] = -ADAM_LR * (m_hat / (jnp.sqrt(v_hat) + ADAM_EPS) + ADAM_WD * w_ref[...])
        nm_ref[...] = mn
        nv_ref[...] = vn

    blk = pl.BlockSpec((tr, c), lambda i: (i, 0))
    return pl.pallas_call(
        body, name=name, grid=(r // tr,),
        in_specs=[pl.BlockSpec((n_parts, tr, c), lambda i: (0, i, 0)), blk, blk, blk],
        out_specs=[blk] * 4,
        out_shape=[jax.ShapeDtypeStruct((r, c), F32)] * 4,
        compiler_params=pltpu.CompilerParams(dimension_semantics=("parallel",), vmem_limit_bytes=VMEM_MM),
    )(parts, w, m, v)


def kernel(x, positions, attn_norm, mlp_norm, sb_w_qkv, sb_w_o, kv_norm, mla_w_dkv, mla_kv_lat_norm, mla_w_ukv, mla_w_dq, mla_q_lat_norm, mla_w_uq, mla_w_o, mlp_w1, mlp_w2, final_norm, loss_target, m_attn_norm, m_mlp_norm, m_sb_w_qkv, m_sb_w_o, m_kv_norm, m_mla_w_dkv, m_mla_kv_lat_norm, m_mla_w_ukv, m_mla_w_dq, m_mla_q_lat_norm, m_mla_w_uq, m_mla_w_o, m_mlp_w1, m_mlp_w2, m_final_norm, v_attn_norm, v_mlp_norm, v_sb_w_qkv, v_sb_w_o, v_kv_norm, v_mla_w_dkv, v_mla_kv_lat_norm, v_mla_w_ukv, v_mla_w_dq, v_mla_q_lat_norm, v_mla_w_uq, v_mla_w_o, v_mlp_w1, v_mlp_w2, v_final_norm):
    weights = dict(attn_norm=attn_norm, mlp_norm=mlp_norm, sb_w_qkv=sb_w_qkv, sb_w_o=sb_w_o, kv_norm=kv_norm,
                   mla_w_dkv=mla_w_dkv, mla_kv_lat_norm=mla_kv_lat_norm, mla_w_ukv=mla_w_ukv, mla_w_dq=mla_w_dq,
                   mla_q_lat_norm=mla_q_lat_norm, mla_w_uq=mla_w_uq, mla_w_o=mla_w_o, mlp_w1=mlp_w1, mlp_w2=mlp_w2,
                   final_norm=final_norm)
    mom_m = dict(attn_norm=m_attn_norm, mlp_norm=m_mlp_norm, sb_w_qkv=m_sb_w_qkv, sb_w_o=m_sb_w_o, kv_norm=m_kv_norm,
                 mla_w_dkv=m_mla_w_dkv, mla_kv_lat_norm=m_mla_kv_lat_norm, mla_w_ukv=m_mla_w_ukv, mla_w_dq=m_mla_w_dq,
                 mla_q_lat_norm=m_mla_q_lat_norm, mla_w_uq=m_mla_w_uq, mla_w_o=m_mla_w_o, mlp_w1=m_mlp_w1,
                 mlp_w2=m_mlp_w2, final_norm=m_final_norm)
    mom_v = dict(attn_norm=v_attn_norm, mlp_norm=v_mlp_norm, sb_w_qkv=v_sb_w_qkv, sb_w_o=v_sb_w_o, kv_norm=v_kv_norm,
                 mla_w_dkv=v_mla_w_dkv, mla_kv_lat_norm=v_mla_kv_lat_norm, mla_w_ukv=v_mla_w_ukv, mla_w_dq=v_mla_w_dq,
                 mla_q_lat_norm=v_mla_q_lat_norm, mla_w_uq=v_mla_w_uq, mla_w_o=v_mla_w_o, mlp_w1=v_mlp_w1,
                 mlp_w2=v_mlp_w2, final_norm=v_final_norm)
    sharded_names = [n for n, _ in SHARDED]
    repl_shapes = [tuple(weights[n].shape) for n in REPLICATED]

    first_name, first_ax = SHARDED[0]
    w_qkv0 = _merge_blocks(_all_gather([weights[first_name][0].astype(BF16)], "gather_qkv")[0], first_ax - 1)
    shards = [weights[first_name][1].astype(BF16)] + [weights[n].astype(BF16) for n in sharded_names[1:]]
    slots, layer_ax = {}, {}
    for n, ax in SHARDED:
        shp = weights[n].shape
        full = shp[:ax] + (N_DEV * shp[ax],) + shp[ax + 1:]
        n_layers, per_layer = (full[0], full[1:]) if len(full) == 3 else (1, full)
        slots[n] = [jnp.zeros(per_layer, F32) for _ in range(n_layers)]
        layer_ax[n] = ax - 1 if len(full) == 3 else ax
    norms = {n: weights[n] for n in REPLICATED if n != 'final_norm'}
    late = [('sb_w_qkv', 0), ('sb_w_o', 0)]
    early = [(n, l) for n in sharded_names for l in range(len(slots[n])) if (n, l) not in late]

    q_tabs, kv_tabs = _rope_tables(positions[0])
    side = {}
    x_mid, pull_first, gathered = jax.vjp(
        lambda xx, sq, so, nn: _first_attention(xx, sq, so, nn, w_qkv0, shards, side),
        x[0], slots['sb_w_qkv'][0], slots['sb_w_o'][0], norms, has_aux=True)
    wb = {'sb_w_qkv': [w_qkv0, _merge_blocks(gathered[0], first_ax - 1)]}
    for (n, ax), g in zip(SHARDED[1:], gathered[1:]):
        wb[n] = _layers(_merge_blocks(g, ax))
    rest_slots = {n: [None if (n, l) in late else a for l, a in enumerate(v)] for n, v in slots.items()}
    x_last, pull_rest = jax.vjp(lambda xx, ss, nn: _rest(xx, ss, nn, wb, q_tabs, kv_tabs), x_mid, rest_slots, norms)
    loss_part, dx_last, d_final = _loss_head(x_last, final_norm, loss_target[0])
    loss = lax.psum(loss_part[0, 0], ("x", "y", "c"))

    core = lax.axis_index("c").astype(jnp.int32).reshape(1)

    def pair_sums(units, grads, tag):
        halves, dims = [], []
        for (n, l), g in zip(units, grads):
            blocks = _split_blocks(g, layer_ax[n])
            two_d = (math.prod(blocks.shape[1:-1]), blocks.shape[-1])
            dims.append(two_d)
            halves.append(jnp.moveaxis(blocks.reshape((N_DEV // 2, 2) + two_d), 1, 0))
        staged = _pair_exchange(halves, "pair_grads_" + tag)
        sums = []
        for (n, l), h, st, (r, c) in zip(units, halves, staged, dims):
            rows = (N_DEV // 2) * r
            sums.append(_add_round(h.reshape(2, rows, c), st.reshape(rows, c), core,
                                   f"pair_sum_{n}_{l}").reshape(st.shape))
        return sums

    dx_mid, d_rest, d_norms_rest = pull_rest(dx_last)
    side['sums'] = pair_sums(early, [d_rest[n][l] for n, l in early], "early")
    dx, d_qkv0, d_o0, d_norms_first = pull_first(dx_mid)
    received = dict(zip(early, side['received']))
    received.update(zip(late, _chip_exchange(pair_sums(late, [d_qkv0, d_o0], "late"), "scatter_late")))
    d_norms = {n: d_norms_rest[n] + d_norms_first[n] for n in norms}
    d_norms['final_norm'] = d_final
    repl_parts = _all_gather([_pack([d_norms[n] for n in REPLICATED])], "gather_norm_grads")[0]

    results = {kind: {} for kind in ("grad", "delta", "new_m", "new_v")}
    for n in sharded_names:
        shp = weights[n].shape
        two_d = (math.prod(shp[:-1]), shp[-1])
        parts = jnp.concatenate([received[(n, l)] for l in range(len(slots[n]))], axis=1)
        res = _adamw_reduce(parts, weights[n].reshape(two_d), mom_m[n].reshape(two_d),
                            mom_v[n].reshape(two_d), "adamw_" + n)
        for kind, a in zip(results, res):
            results[kind][n] = a.reshape(shp)
    res = _adamw_reduce(repl_parts, _pack([weights[n] for n in REPLICATED]), _pack([mom_m[n] for n in REPLICATED]),
                        _pack([mom_v[n] for n in REPLICATED]), "adamw_replicated")
    for kind, a in zip(results, res):
        results[kind].update(zip(REPLICATED, _unpack(a, repl_shapes)))

    out = [loss, dx[None]]
    for kind in ("grad", "delta", "new_m", "new_v"):
        out += [results[kind][n] for n in WEIGHT_ORDER]
    return tuple(out)
```
